```python
import jax
import jax.numpy as jnp
from jax import lax
import numpy as np

D_MODEL = 1024
BATCH = 1
SEQ = 16384
DEPTH = 1
DEC_BATCH = 32
DEC_SEQ = 1
PAST_LEN = 16384
PAGE_SIZE = 128

N_HEADS_A = 8
N_KV_A = 2
HEAD_DIM_A = 64
TOPK_MAX = 256
Q_BLOCK = 128
N_HEADS_IDX = 8
D_IDX = 64
N_GROUPS_B = 4
GROUP_DIM_B = 128
WIDTH_B = N_GROUPS_B * GROUP_DIM_B
CHUNK = 128
N_MEM = 256
N_HEADS_M = 4
HEAD_DIM_M = 128
ROPE_THETA = 500000.0
ROT_FRACTION = 4
N_BRANCH = 3
N_EXPERTS = 32
TOP_K_MOE = 4
D_FF = 1024
MOE_BLOCK = 256
SWIGLU_LIMIT = 7.0
SWIGLU_ALPHA = 1.702
EPS = 1e-6

WA_Q = N_HEADS_A * HEAD_DIM_A
WA_KV = N_KV_A * HEAD_DIM_A
WI_Q = N_HEADS_IDX * D_IDX
WM_Q = N_HEADS_M * HEAD_DIM_M
IN_SPLITS = (WA_Q, WA_KV, WA_KV, WI_Q, D_IDX, N_HEADS_IDX, WIDTH_B, WIDTH_B, WM_Q, N_BRANCH * D_MODEL)
D_IN = sum(IN_SPLITS)

kernel_name = 'hybrid_dsa_gmlp_memory_moe_step'


def _f32(a):
    return a.astype(jnp.float32)


def rmsnorm(x, g):
    xf = _f32(x)
    y = xf * lax.rsqrt(jnp.mean(xf * xf, axis=-1, keepdims=True) + EPS)
    return (y * _f32(g)).astype(x.dtype)


def partial_rope(x, pos):
    rot = x.shape[-1] // ROT_FRACTION
    half = rot // 2
    inv_freq = ROPE_THETA ** (-jnp.arange(half, dtype=jnp.float32) / half)
    ang = _f32(pos)[:, None] * inv_freq[None, :]
    cos = jnp.cos(ang)[None, :, None, :]
    sin = jnp.sin(ang)[None, :, None, :]
    xr = _f32(x[..., :rot])
    x1, x2 = xr[..., :half], xr[..., half:]
    rotated = jnp.concatenate([x1 * cos - x2 * sin, x2 * cos + x1 * sin], axis=-1)
    return jnp.concatenate([rotated.astype(x.dtype), x[..., rot:]], axis=-1)


def split_in(p):
    bounds = np.cumsum(np.array(IN_SPLITS))[:-1].tolist()
    return jnp.split(p, bounds, axis=-1)


def mixer_front(x, pos, g_in, w_in, g_qa, g_ka, g_vb, g_qm):
    B, T, _ = x.shape
    h = rmsnorm(x, g_in)
    qa, ka, va, qi, ki, wi, ub, vb, qm, gates = split_in(h @ w_in)
    qa = partial_rope(rmsnorm(qa.reshape(B, T, N_HEADS_A, HEAD_DIM_A), g_qa), pos)
    ka = partial_rope(rmsnorm(ka.reshape(B, T, N_KV_A, HEAD_DIM_A), g_ka), pos)
    va = va.reshape(B, T, N_KV_A, HEAD_DIM_A)
    qi = partial_rope(qi.reshape(B, T, N_HEADS_IDX, D_IDX), pos)
    ki = partial_rope(ki.reshape(B, T, 1, D_IDX), pos).reshape(B, T, D_IDX)
    vb = rmsnorm(vb, g_vb)
    qm = rmsnorm(qm.reshape(B, T, N_HEADS_M, HEAD_DIM_M), g_qm)
    return qa, ka, va, qi, ki, wi, ub, vb, qm, gates


def indexer_scores(qi, wi, ki, t_pos, s_pos):
    dots = jnp.einsum('bqhd,bsd->bqhs', _f32(qi), _f32(ki))
    score = jnp.einsum('bqh,bqhs->bqs', _f32(wi), jax.nn.relu(dots)) * (D_IDX ** -0.5 * N_HEADS_IDX ** -0.5)
    return jnp.where(s_pos[None, None, :] <= t_pos[None, :, None], score, -jnp.inf)


def sparse_attend(q, ks, vs, valid):
    B, Q = q.shape[:2]
    G = N_HEADS_A // N_KV_A
    qg = _f32(q.reshape(B, Q, N_KV_A, G, HEAD_DIM_A))
    logits = jnp.einsum('bqkgd,bqskd->bqkgs', qg, _f32(ks)) * HEAD_DIM_A ** -0.5
    logits = jnp.where(valid[:, :, None, None, :], logits, -jnp.inf)
    p = jax.nn.softmax(logits, axis=-1)
    out = jnp.einsum('bqkgs,bqskd->bqkgd', p, _f32(vs))
    return out.reshape(B, Q, WA_Q).astype(q.dtype)


def prompt_sparse_attention(qa, ka, va, qi, ki, wi):
    B, T = qa.shape[:2]
    k_sel = min(TOPK_MAX, T // 4)
    n_blocks = T // Q_BLOCK
    s_pos = jnp.arange(T)
    b_idx = jnp.arange(B)[:, None, None]

    def block(i):
        start = i * Q_BLOCK
        t_pos = start + jnp.arange(Q_BLOCK)
        sl = lambda a: lax.dynamic_slice_in_dim(a, start, Q_BLOCK, axis=1)
        score = indexer_scores(sl(qi), sl(wi), ki, t_pos, s_pos)
        _, sel = lax.top_k(score, k_sel)
        valid = sel <= t_pos[None, :, None]
        return sparse_attend(sl(qa), ka[b_idx, sel], va[b_idx, sel], valid)

    out = lax.map(block, jnp.arange(n_blocks))
    return jnp.moveaxis(out, 0, 1).reshape(B, T, WA_Q)


def sample_sparse_attention(qa, ka, va, qi, ki, wi, cache_k, cache_v, cache_idx_k, page_table):
    B, T = qa.shape[:2]
    n_pages = page_table.shape[1]
    past = n_pages * PAGE_SIZE
    L = past + T
    k_sel = min(TOPK_MAX, L // 4)
    ki_past = cache_idx_k[page_table].reshape(B, past, D_IDX)
    ki_all = jnp.concatenate([ki_past.astype(ki.dtype), ki], axis=1)
    t_pos = past + jnp.arange(T)
    score = indexer_scores(qi, wi, ki_all, t_pos, jnp.arange(L))
    _, sel = lax.top_k(score, k_sel)
    valid = sel <= t_pos[None, :, None]
    in_past = (sel < past)[..., None, None]
    b_idx = jnp.arange(B)[:, None, None]
    s_past = jnp.minimum(sel, past - 1)
    phys = page_table[b_idx, s_past // PAGE_SIZE]
    off = s_past % PAGE_SIZE
    s_new = jnp.clip(sel - past, 0, T - 1)
    ks = jnp.where(in_past, cache_k[phys, off].astype(ka.dtype), ka[b_idx, s_new])
    vs = jnp.where(in_past, cache_v[phys, off].astype(va.dtype), va[b_idx, s_new])
    return sparse_attend(qa, ks, vs, valid)


def chunk_spatial_gate(ub, vb, w_s, b_s):
    B, T = ub.shape[:2]
    c = min(T, CHUNK)
    n_chunks = T // c
    v = vb.reshape(B, n_chunks, c, N_GROUPS_B, GROUP_DIM_B)
    w = w_s[:, :c, :c] * jnp.tril(jnp.ones((c, c), w_s.dtype))
    z = jnp.einsum('gts,bnsgd->bntgd', _f32(w), _f32(v)) + _f32(b_s[:, :c]).T[None, None, :, :, None]
    return (_f32(ub) * z.reshape(B, T, WIDTH_B)).astype(ub.dtype)


def memory_kv(mem, g_mem, w_mem_kv, g_km):
    B, M = mem.shape[:2]
    km, vm = jnp.split(rmsnorm(mem, g_mem) @ w_mem_kv, 2, axis=-1)
    km = rmsnorm(km.reshape(B, M, N_HEADS_M, HEAD_DIM_M), g_km)
    vm = vm.reshape(B, M, N_HEADS_M, HEAD_DIM_M)
    return km, vm


def memory_attention(qm, km, vm):
    B, T = qm.shape[:2]
    logits = jnp.einsum('bthd,bshd->bhts', _f32(qm), _f32(km)) * HEAD_DIM_M ** -0.5
    p = jax.nn.softmax(logits, axis=-1)
    out = jnp.einsum('bhts,bshd->bthd', p, _f32(vm))
    return out.reshape(B, T, WM_Q).astype(qm.dtype)


def merge_branches(x, out_a, out_b, out_m, gates, w_br_a, w_br_b, w_br_m, w_out):
    B, T, _ = x.shape
    g = jax.nn.sigmoid(_f32(gates)).reshape(B, T, N_BRANCH, D_MODEL)
    merged = (g[:, :, 0] * _f32(out_a @ w_br_a) + g[:, :, 1] * _f32(out_b @ w_br_b)
              + g[:, :, 2] * _f32(out_m @ w_br_m))
    return x + merged.astype(x.dtype) @ w_out


def moe_ffn(x, g_ffn, w_router, b_router, w_gate, b_gate, w_up, b_up, w_down, b_down):
    B, T, D = x.shape
    n = B * T
    h = rmsnorm(x, g_ffn).reshape(n, D)
    logits = _f32(h @ w_router) + _f32(b_router)
    top_logit, top_e = lax.top_k(logits, TOP_K_MOE)
    top_w = jax.nn.softmax(top_logit, axis=-1)
    A = n * TOP_K_MOE
    R = max(8, min(MOE_BLOCK, A // N_EXPERTS))
    NB = -(-A // R) + N_EXPERTS
    flat_e = top_e.reshape(A)
    order = jnp.argsort(flat_e)
    e_sorted = flat_e[order]
    counts = jnp.bincount(flat_e, length=N_EXPERTS)
    padded = (counts + R - 1) // R * R
    pad_end = jnp.cumsum(padded)
    pad_start = pad_end - padded
    raw_start = jnp.cumsum(counts) - counts
    dest = pad_start[e_sorted] + (jnp.arange(A) - raw_start[e_sorted])
    slot_tok = jnp.full((NB * R,), n, jnp.int32).at[dest].set((order // TOP_K_MOE).astype(jnp.int32))
    slot_w = jnp.zeros((NB * R,), jnp.float32).at[dest].set(top_w.reshape(A)[order])
    block_e = jnp.minimum(jnp.searchsorted(pad_end, jnp.arange(NB) * R, side='right'), N_EXPERTS - 1)
    h_pad = jnp.concatenate([h, jnp.zeros((1, D), h.dtype)], axis=0)

    def expert_block(args):
        e, tok, w = args
        xb = h_pad[tok]
        hg = jnp.minimum(xb @ w_gate[e] + b_gate[e], SWIGLU_LIMIT)
        hu = jnp.clip(xb @ w_up[e] + b_up[e], -SWIGLU_LIMIT, SWIGLU_LIMIT)
        act = hg * jax.nn.sigmoid(SWIGLU_ALPHA * hg) * (hu + 1)
        return (_f32(act @ w_down[e] + b_down[e]) * w[:, None]).astype(x.dtype)

    yb = lax.map(expert_block, (block_e, slot_tok.reshape(NB, R), slot_w.reshape(NB, R)))
    y = jax.ops.segment_sum(yb.reshape(NB * R, D), slot_tok, num_segments=n + 1)[:n]
    return x + y.reshape(B, T, D)


def setup_inputs(seed: int = 0) -> dict:
    key = jax.random.key(seed)
    ks = jax.random.split(key, 40)
    n_pages = PAST_LEN // PAGE_SIZE
    n_used = DEC_BATCH * n_pages
    n_pool = n_used + n_used // 4

    def nrm(k, shape, scale=1.0):
        return jax.random.normal(k, shape, jnp.float32) * scale

    def gain(k, shape):
        return 1.0 + 0.05 * jax.random.normal(k, shape, jnp.float32)

    page_table = jax.random.permutation(ks[8], n_pool)[:n_used].reshape(DEC_BATCH, n_pages).astype(jnp.int32)
    return {
        'x_prompt': nrm(ks[0], (BATCH, SEQ, D_MODEL)),
        'x_sample': nrm(ks[1], (DEC_BATCH, DEC_SEQ, D_MODEL)),
        'mem_prompt': nrm(ks[2], (BATCH, N_MEM, D_MODEL)),
        'cache_k': nrm(ks[3], (n_pool, PAGE_SIZE, N_KV_A, HEAD_DIM_A)),
        'cache_v': nrm(ks[4], (n_pool, PAGE_SIZE, N_KV_A, HEAD_DIM_A)),
        'cache_idx_k': nrm(ks[5], (n_pool, PAGE_SIZE, D_IDX)),
        'cache_mem_k': nrm(ks[6], (DEC_BATCH, N_MEM, N_HEADS_M, HEAD_DIM_M)),
        'cache_mem_v': nrm(ks[7], (DEC_BATCH, N_MEM, N_HEADS_M, HEAD_DIM_M)),
        'page_table': page_table,
        'g_in': gain(ks[9], (D_MODEL,)),
        'w_in': nrm(ks[10], (D_MODEL, D_IN), D_MODEL ** -0.5),
        'g_qa': gain(ks[11], (HEAD_DIM_A,)),
        'g_ka': gain(ks[12], (HEAD_DIM_A,)),
        'g_vb': gain(ks[13], (WIDTH_B,)),
        'w_s': nrm(ks[14], (N_GROUPS_B, CHUNK, CHUNK), CHUNK ** -0.5),
        'b_s': 1.0 + 0.1 * jax.random.normal(ks[15], (N_GROUPS_B, CHUNK), jnp.float32),
        'g_qm': gain(ks[16], (HEAD_DIM_M,)),
        'g_mem': gain(ks[17], (D_MODEL,)),
        'w_mem_kv': nrm(ks[18], (D_MODEL, 2 * WM_Q), D_MODEL ** -0.5),
        'g_km': gain(ks[19], (HEAD_DIM_M,)),
        'w_br_a': nrm(ks[20], (WA_Q, D_MODEL), WA_Q ** -0.5),
        'w_br_b': nrm(ks[21], (WIDTH_B, D_MODEL), WIDTH_B ** -0.5),
        'w_br_m': nrm(ks[22], (WM_Q, D_MODEL), WM_Q ** -0.5),
        'w_out': nrm(ks[23], (D_MODEL, D_MODEL), D_MODEL ** -0.5),
        'g_ffn': gain(ks[24], (D_MODEL,)),
        'w_router': nrm(ks[25], (D_MODEL, N_EXPERTS), D_MODEL ** -0.5),
        'b_router': nrm(ks[26], (N_EXPERTS,), 0.01),
        'w_gate': nrm(ks[27], (N_EXPERTS, D_MODEL, D_FF), D_MODEL ** -0.5),
        'b_gate': nrm(ks[28], (N_EXPERTS, D_FF), 0.01),
        'w_up': nrm(ks[29], (N_EXPERTS, D_MODEL, D_FF), D_MODEL ** -0.5),
        'b_up': nrm(ks[30], (N_EXPERTS, D_FF), 0.01),
        'w_down': nrm(ks[31], (N_EXPERTS, D_FF, D_MODEL), D_FF ** -0.5),
        'b_down': nrm(ks[32], (N_EXPERTS, D_MODEL), 0.01),
    }


def reference(x_prompt, x_sample, mem_prompt, cache_k, cache_v, cache_idx_k, cache_mem_k, cache_mem_v, page_table,
              g_in, w_in, g_qa, g_ka, g_vb, w_s, b_s, g_qm, g_mem, w_mem_kv, g_km, w_br_a, w_br_b, w_br_m, w_out,
              g_ffn, w_router, b_router, w_gate, b_gate, w_up, b_up, w_down, b_down):
    B, T = x_prompt.shape[:2]
    Ts = x_sample.shape[1]
    past = page_table.shape[1] * PAGE_SIZE
    pos_p = jnp.arange(T, dtype=jnp.int32)
    pos_s = past + jnp.arange(Ts, dtype=jnp.int32)

    qa, ka, va, qi, ki, wi, ub, vb, qm, gates = mixer_front(x_prompt, pos_p, g_in, w_in, g_qa, g_ka, g_vb, g_qm)
    mem_k_p, mem_v_p = memory_kv(mem_prompt, g_mem, w_mem_kv, g_km)
    out_a = prompt_sparse_attention(qa, ka, va, qi, ki, wi)
    out_b = chunk_spatial_gate(ub, vb, w_s, b_s)
    out_m = memory_attention(qm, mem_k_p, mem_v_p)
    x1 = merge_branches(x_prompt, out_a, out_b, out_m, gates, w_br_a, w_br_b, w_br_m, w_out)
    y_prompt = moe_ffn(x1, g_ffn, w_router, b_router, w_gate, b_gate, w_up, b_up, w_down, b_down)
    k_prompt = ka.reshape(B, T // PAGE_SIZE, PAGE_SIZE, N_KV_A, HEAD_DIM_A)
    v_prompt = va.reshape(B, T // PAGE_SIZE, PAGE_SIZE, N_KV_A, HEAD_DIM_A)
    idx_k_prompt = ki.reshape(B, T // PAGE_SIZE, PAGE_SIZE, D_IDX)

    qs, k_sample, v_sample, qis, idx_k_sample, wis, ubs, gmlp_v_sample, qms, gates_s = mixer_front(
        x_sample, pos_s, g_in, w_in, g_qa, g_ka, g_vb, g_qm)
    out_a_s = sample_sparse_attention(qs, k_sample, v_sample, qis, idx_k_sample, wis,
                                      cache_k, cache_v, cache_idx_k, page_table)
    out_b_s = chunk_spatial_gate(ubs, gmlp_v_sample, w_s, b_s)
    out_m_s = memory_attention(qms, cache_mem_k, cache_mem_v)
    x1s = merge_branches(x_sample, out_a_s, out_b_s, out_m_s, gates_s, w_br_a, w_br_b, w_br_m, w_out)
    y_sample = moe_ffn(x1s, g_ffn, w_router, b_router, w_gate, b_gate, w_up, b_up, w_down, b_down)

    return (y_prompt, y_sample, k_prompt, v_prompt, idx_k_prompt, mem_k_p, mem_v_p,
            k_sample, v_sample, idx_k_sample, gmlp_v_sample)
```

```python
import functools

import jax
import jax.numpy as jnp
from jax import lax
from jax.experimental import pallas as pl
from jax.experimental.pallas import tpu as pltpu

F32 = jnp.float32
BF16 = jnp.bfloat16
I32 = jnp.int32

N_HEADS_A = 8
N_KV_A = 2
HEAD_DIM_A = 64
TOPK_MAX = 256
N_HEADS_IDX = 8
D_IDX = 64
N_GROUPS_B = 4
GROUP_DIM_B = 128
WIDTH_B = N_GROUPS_B * GROUP_DIM_B
CHUNK = 128
N_HEADS_M = 4
HEAD_DIM_M = 128
ROPE_THETA = 500000.0
ROT_FRACTION = 4
N_BRANCH = 3
TOP_K_MOE = 4
SWIGLU_LIMIT = 7.0
SWIGLU_ALPHA = 1.702
EPS = 1e-6

WA_Q = N_HEADS_A * HEAD_DIM_A
WA_KV = N_KV_A * HEAD_DIM_A
WI_Q = N_HEADS_IDX * D_IDX
WM_Q = N_HEADS_M * HEAD_DIM_M
GQA = N_HEADS_A // N_KV_A

LANES = 128
SUBLANES = 8
VMEM_LIMIT = 56 * 1024 * 1024

INT_MIN = -(2 ** 31)
INT_MAX = 2 ** 31 - 1
NEG = -1e30

ATTN_BQ = 128
ATTN_SC = 512
MOE_BM = 512
FF_CHUNK = 256


def _cparams(n_axes=1, vmem=VMEM_LIMIT):
    return pltpu.CompilerParams(dimension_semantics=("arbitrary",) * n_axes, vmem_limit_bytes=vmem)


def _const_spec(shape):
    zeros = (0,) * len(shape)
    return pl.BlockSpec(shape, lambda *_: zeros)


def _dot(a, b):
    return jnp.dot(a, b, preferred_element_type=F32)


def _dot_nt(a, b):
    return lax.dot_general(a, b, (((1,), (1,)), ((), ())), preferred_element_type=F32)


def _rms(x, g):
    return x * lax.rsqrt(jnp.mean(x * x, axis=-1, keepdims=True) + EPS) * g


def _split_dot(x, w):
    hi = x.astype(BF16)
    lo = (x - hi.astype(F32)).astype(BF16)
    return _dot(hi, w) + _dot(lo, w)


def _front_kernel(x_ref, gin_ref, wa_ref, wwit_ref, wvt_ref, bd_ref, gqa_ref, gka_ref, cos_ref, sa_ref, sb_ref,
                  qa_ref, k32_ref, khm_ref, v32_ref, vt_ref, qi_ref, ki32_ref, kibf_ref, wit_ref, *, score_scale):
    x = x_ref[...]
    h = _rms(x, gin_ref[...]).astype(BF16)
    p = _dot(h, wa_ref[...])
    cos, sa, sb = cos_ref[...], sa_ref[...], sb_ref[...]
    bd = bd_ref[...]

    def head_norm(v, g):
        ssq = _split_dot(v * v, bd)
        return v * lax.rsqrt(ssq * (1.0 / HEAD_DIM_A) + EPS) * g

    def rope(v):
        return v * cos + pltpu.roll(v, LANES - 8, 1) * sa + pltpu.roll(v, 8, 1) * sb

    for s in range(WA_Q // LANES):
        v = p[:, s * LANES:(s + 1) * LANES]
        v = rope(head_norm(v, gqa_ref[...])) * (HEAD_DIM_A ** -0.5)
        vb = v.astype(BF16)
        qa_ref[2 * s] = vb[:, :HEAD_DIM_A]
        qa_ref[2 * s + 1] = vb[:, HEAD_DIM_A:]
    o = WA_Q
    k = rope(head_norm(p[:, o:o + WA_KV], gka_ref[...]))
    k32_ref[...] = k
    kb = k.astype(BF16)
    khm_ref[0] = kb[:, :HEAD_DIM_A]
    khm_ref[1] = kb[:, HEAD_DIM_A:]
    o += WA_KV
    v32_ref[...] = p[:, o:o + WA_KV]
    o += WA_KV
    for s in range(WI_Q // LANES):
        vb = rope(p[:, o + s * LANES:o + (s + 1) * LANES]).astype(BF16)
        qi_ref[2 * s] = vb[:, :D_IDX]
        qi_ref[2 * s + 1] = vb[:, D_IDX:]
    o += WI_Q
    ki = rope(p[:, o:o + LANES])[:, :D_IDX]
    ki32_ref[...] = ki
    kibf_ref[...] = ki.astype(BF16)
    wit_ref[...] = _dot_nt(wwit_ref[...], h) * score_scale
    vt_ref[...] = _dot_nt(wvt_ref[...], h).astype(BF16)


def _front(x2d, tabs, gin, wa, wwit, wvt, bd, gqa, gka, tm):
    n, d = x2d.shape
    cos, sa, sb = tabs
    row = lambda w: pl.BlockSpec((tm, w), lambda i: (i, 0))
    hm = lambda nh, w: pl.BlockSpec((nh, tm, w), lambda i: (0, i, 0))
    out_shape = (
        jax.ShapeDtypeStruct((N_HEADS_A, n, HEAD_DIM_A), BF16),
        jax.ShapeDtypeStruct((n, WA_KV), F32),
        jax.ShapeDtypeStruct((N_KV_A, n, HEAD_DIM_A), BF16),
        jax.ShapeDtypeStruct((n, WA_KV), F32),
        jax.ShapeDtypeStruct((WA_KV, n), BF16),
        jax.ShapeDtypeStruct((N_HEADS_IDX, n, D_IDX), BF16),
        jax.ShapeDtypeStruct((n, D_IDX), F32),
        jax.ShapeDtypeStruct((n, D_IDX), BF16),
        jax.ShapeDtypeStruct((N_HEADS_IDX, n), F32),
    )
    out_specs = (hm(N_HEADS_A, HEAD_DIM_A), row(WA_KV), hm(N_KV_A, HEAD_DIM_A), row(WA_KV),
                 pl.BlockSpec((WA_KV, tm), lambda i: (0, i)), hm(N_HEADS_IDX, D_IDX), row(D_IDX), row(D_IDX),
                 pl.BlockSpec((N_HEADS_IDX, tm), lambda i: (0, i)))
    in_specs = [row(d), _const_spec(gin.shape), _const_spec(wa.shape), _const_spec(wwit.shape),
                _const_spec(wvt.shape), _const_spec(bd.shape), _const_spec(gqa.shape), _const_spec(gka.shape),
                row(LANES), row(LANES), row(LANES)]
    kern = functools.partial(_front_kernel, score_scale=D_IDX ** -0.5 * N_HEADS_IDX ** -0.5)
    return pl.pallas_call(kern, out_shape=out_shape, grid=(n // tm,), in_specs=in_specs, out_specs=out_specs,
                          compiler_params=_cparams(), name="front")(x2d, gin, wa, wwit, wvt, bd, gqa, gka, cos, sa, sb)


def _memkv_kernel(mem_ref, gmem_ref, w_ref, gkm_ref, k32_ref, v32_ref):
    h = _rms(mem_ref[...], gmem_ref[...]).astype(BF16)
    kv = _dot(h, w_ref[...])
    for hh in range(N_HEADS_M):
        sl = slice(hh * HEAD_DIM_M, (hh + 1) * HEAD_DIM_M)
        k32_ref[:, sl] = _rms(kv[:, sl], gkm_ref[...])
    v32_ref[...] = kv[:, WM_Q:]


def _memory_kv(mem2d, gmem, w, gkm):
    n = mem2d.shape[0]
    out = jax.ShapeDtypeStruct((n, WM_Q), F32)
    return pl.pallas_call(_memkv_kernel, out_shape=(out, out), name="memory_kv",
                          compiler_params=pltpu.CompilerParams(vmem_limit_bytes=VMEM_LIMIT))(mem2d, gmem, w, gkm)


def _sort_key(s):
    s = jnp.where(s == 0.0, 0.0, s)
    bits = pltpu.bitcast(s, I32)
    return bits ^ ((bits >> 31) & INT_MAX)


def _attn_kernel(qi_ref, qa_ref, wi_ref, ki_ref, k_ref, vt_ref, out_ref,
                 keys_ref, thr_ref, cut_ref, m_ref, l_ref, acc_ref, *, k_sel, idx_bits):
    bq, sc = ATTN_BQ, ATTN_SC
    i = pl.program_id(0)
    n_ch = (i * bq) // sc + 1
    t_pos = i * bq + lax.broadcasted_iota(I32, (1, bq), 1)
    sub = lax.broadcasted_iota(I32, (sc, 1), 0)
    qi = qi_ref[...].reshape(N_HEADS_IDX * bq, D_IDX)
    w = wi_ref[...]

    def score_chunk(c, carry):
        off = pl.multiple_of(c * sc, sc)
        d = _dot_nt(ki_ref[pl.ds(off, sc), :], qi)
        s = jnp.zeros((sc, bq), F32)
        for h in range(N_HEADS_IDX):
            s = s + jnp.maximum(d[:, h * bq:(h + 1) * bq], 0.0) * w[h:h + 1, :]
        key = jnp.where(off + sub <= t_pos, _sort_key(s), INT_MIN)
        keys_ref[pl.ds(off, sc), :] = key
        return carry

    lax.fori_loop(0, n_ch, score_chunk, 0)

    def count(pred):
        def body(c, acc):
            off = pl.multiple_of(c * sc, sc)
            hit = jnp.where(pred(keys_ref[pl.ds(off, sc), :], off + sub), 1, 0)
            return acc + hit.reshape(sc // SUBLANES, SUBLANES, bq).sum(axis=0)
        acc = lax.fori_loop(0, n_ch, body, jnp.zeros((SUBLANES, bq), I32))
        return acc.sum(axis=0, keepdims=True)

    short = t_pos + 1 <= k_sel

    def search_cond(st):
        bit, _, cnt = st
        pending = jnp.where(jnp.logical_or(cnt == k_sel, short), 0, 1)
        return jnp.logical_and(bit >= 0, jnp.max(pending) > 0)

    def search_body(st):
        bit, thr, cnt = st
        cand = thr + (jnp.int32(1) << bit)
        c = count(lambda blk, _: blk >= cand)
        take = c >= k_sel
        return bit - 1, jnp.where(take, cand, thr), jnp.where(take, c, cnt)

    st0 = (jnp.int32(31), jnp.full((1, bq), INT_MIN, I32), jnp.zeros((1, bq), I32) + n_ch * sc)
    _, thr, cnt = lax.while_loop(search_cond, search_body, st0)
    thr_ref[...] = thr
    cut_ref[...] = jnp.full((1, bq), INT_MAX, I32)

    excess = jnp.logical_and(cnt > k_sel, thr > INT_MIN)

    @pl.when(jnp.max(jnp.where(excess, 1, 0)) > 0)
    def _():
        want = k_sel - count(lambda blk, _: blk > thr)

        def tie_body(j, cpos):
            cand = cpos + (jnp.int32(1) << (idx_bits - 1 - j))
            below = count(lambda blk, pos: jnp.where(blk == thr, pos, INT_MAX) < cand)
            return jnp.where(below < want, cand, cpos)

        cpos = lax.fori_loop(0, idx_bits, tie_body, jnp.zeros((1, bq), I32))
        cut_ref[...] = jnp.where(excess, cpos, INT_MAX)

    m_ref[...] = jnp.full(m_ref.shape, NEG, F32)
    l_ref[...] = jnp.zeros(l_ref.shape, F32)
    acc_ref[...] = jnp.zeros(acc_ref.shape, F32)
    cut = cut_ref[...]

    def attn_chunk(c, carry):
        off = pl.multiple_of(c * sc, sc)
        keyc = keys_ref[pl.ds(off, sc), :]
        tie = jnp.where(off + sub <= cut, 0.0, NEG)
        bias = jnp.where(keyc > thr, 0.0, jnp.where(keyc == thr, tie, NEG))
        bias = jnp.where(keyc == INT_MIN, NEG, bias)
        bias = jnp.concatenate([bias] * GQA, axis=1)
        for g in range(N_KV_A):
            qg = qa_ref[g * GQA:(g + 1) * GQA].reshape(GQA * bq, HEAD_DIM_A)
            lg = _dot_nt(k_ref[g, pl.ds(off, sc), :], qg) + bias
            m_old = m_ref[g]
            m_new = jnp.maximum(m_old, lg.max(axis=0, keepdims=True))
            p = jnp.exp(lg - m_new)
            alpha = jnp.exp(m_old - m_new)
            m_ref[g] = m_new
            l_ref[g] = alpha * l_ref[g] + p.sum(axis=0, keepdims=True)
            vt = vt_ref[pl.ds(g * HEAD_DIM_A, HEAD_DIM_A), pl.ds(off, sc)]
            acc_ref[g] = alpha * acc_ref[g] + _dot(vt, p.astype(BF16))
        return carry

    lax.fori_loop(0, n_ch, attn_chunk, 0)

    heads = []
    for g in range(N_KV_A):
        o = acc_ref[g] / l_ref[g]
        heads += [o[:, hh * bq:(hh + 1) * bq] for hh in range(GQA)]
    out_ref[...] = jnp.concatenate(heads, axis=0).T.astype(out_ref.dtype)


def _prompt_attention(qi_hm, qa_hm, wit, kibf, khm, vt):
    t = kibf.shape[0]
    bq = ATTN_BQ
    k_sel = min(TOPK_MAX, t // 4)
    kern = functools.partial(_attn_kernel, k_sel=k_sel, idx_bits=max(1, (t - 1).bit_length()))
    in_specs = [pl.BlockSpec((N_HEADS_IDX, bq, D_IDX), lambda i: (0, i, 0)),
                pl.BlockSpec((N_HEADS_A, bq, HEAD_DIM_A), lambda i: (0, i, 0)),
                pl.BlockSpec((N_HEADS_IDX, bq), lambda i: (0, i)),
                _const_spec(kibf.shape), _const_spec(khm.shape), _const_spec(vt.shape)]
    scratch = [pltpu.VMEM((t, bq), I32), pltpu.VMEM((1, bq), I32), pltpu.VMEM((1, bq), I32),
               pltpu.VMEM((N_KV_A, 1, GQA * bq), F32), pltpu.VMEM((N_KV_A, 1, GQA * bq), F32),
               pltpu.VMEM((N_KV_A, HEAD_DIM_A, GQA * bq), F32)]
    return pl.pallas_call(kern, out_shape=jax.ShapeDtypeStruct((t, WA_Q), BF16), grid=(t // bq,),
                          in_specs=in_specs, out_specs=pl.BlockSpec((bq, WA_Q), lambda i: (i, 0)),
                          scratch_shapes=scratch, compiler_params=_cparams(), name="attn")(
                              qi_hm, qa_hm, wit, kibf, khm, vt)


SAMPLE_CK = 2048


def _sample_attn_kernel(pt_ref, qi_ref, qa_ref, wi_ref, kin_ref, kn_ref, vn_ref, cidx_hbm, ck_hbm, cv_hbm, out_ref,
                        idx_buf, k_buf, v_buf, scr_ref, sems, *, n_pages, ps, ck, k_sel, idx_bits):
    b = pl.program_id(0)
    nb = pl.num_programs(0)
    past = n_pages * ps
    nc = past // ck

    def page_copy(which, bb, p):
        src, dst = ((cidx_hbm, idx_buf), (ck_hbm, k_buf), (cv_hbm, v_buf))[which]
        rows = pl.ds(pl.multiple_of(p * ps, ps), ps)
        return pltpu.make_async_copy(src.at[pt_ref[bb, p]], dst.at[rows, :], sems.at[which])

    def for_pages(fn):
        lax.fori_loop(0, n_pages, lambda p, c: (fn(p), c)[1], 0)

    @pl.when(b == 0)
    def _():
        for_pages(lambda p: page_copy(0, b, p).start())

    for_pages(lambda p: (page_copy(1, b, p).start(), page_copy(2, b, p).start()))
    for_pages(lambda p: page_copy(0, b, p).wait())

    qi = qi_ref[0]
    w = wi_ref[0]
    for c in range(nc):
        kc = idx_buf[c * ck:(c + 1) * ck, :].astype(BF16)
        d = _dot_nt(qi, kc)
        scr_ref[c:c + 1, :] = (jnp.maximum(d, 0.0) * w).sum(axis=0, keepdims=True)
    d_new = (qi.astype(F32) * kin_ref[0].astype(BF16).astype(F32)).sum(axis=-1, keepdims=True)
    s_new = (jnp.maximum(d_new, 0.0) * w).sum(axis=0, keepdims=True)

    @pl.when(b + 1 < nb)
    def _():
        for_pages(lambda p: page_copy(0, b + 1, p).start())

    keys = _sort_key(scr_ref[...])
    key_new = _sort_key(s_new)
    pos = (lax.broadcasted_iota(I32, keys.shape, 0) * ck + lax.broadcasted_iota(I32, keys.shape, 1))

    def count(pred):
        hit = jnp.where(pred(keys, pos), 1, 0).sum(axis=1, keepdims=True).sum(axis=0, keepdims=True)
        return hit + jnp.where(pred(key_new, past), 1, 0)

    def search_cond(st):
        bit, _, cnt = st
        return jnp.logical_and(bit >= 0, jnp.max(jnp.where(cnt == k_sel, 0, 1)) > 0)

    def search_body(st):
        bit, thr, cnt = st
        cand = thr + (jnp.int32(1) << bit)
        c = count(lambda kk, _: kk >= cand)
        take = c >= k_sel
        return bit - 1, jnp.where(take, cand, thr), jnp.where(take, c, cnt)

    st0 = (jnp.int32(31), jnp.full((1, 1), INT_MIN, I32), jnp.full((1, 1), past + 1, I32))
    _, thr, cnt = lax.while_loop(search_cond, search_body, st0)

    want = k_sel - count(lambda kk, _: kk > thr)

    def tie_body(j, cpos):
        cand = cpos + (jnp.int32(1) << (idx_bits - 1 - j))
        below = count(lambda kk, pp: jnp.where(kk == thr, pp, INT_MAX) < cand)
        return jnp.where(below < want, cand, cpos)

    cpos = lax.fori_loop(0, idx_bits, tie_body, jnp.zeros((1, 1), I32))
    cut = jnp.where(cnt > k_sel, cpos, INT_MAX)

    def sel_bias(kk, pp):
        tie = jnp.where(pp <= cut, 0.0, NEG)
        return jnp.where(kk > thr, 0.0, jnp.where(kk == thr, tie, NEG))

    bias = sel_bias(keys, pos)
    bias_new = sel_bias(key_new, past)

    for_pages(lambda p: (page_copy(1, b, p).wait(), page_copy(2, b, p).wait()))

    qa = qa_ref[0]
    heads = []
    for g in range(N_KV_A):
        qg = qa[g * GQA:(g + 1) * GQA]
        cols = slice(g * HEAD_DIM_A, (g + 1) * HEAD_DIM_A)
        m = jnp.full((GQA, 1), NEG, F32)
        l = jnp.zeros((GQA, 1), F32)
        acc = jnp.zeros((GQA, HEAD_DIM_A), F32)
        for c in range(nc):
            kc = k_buf[c * ck:(c + 1) * ck, cols].astype(BF16)
            lg = _dot_nt(qg, kc) + bias[c:c + 1, :]
            m_new = jnp.maximum(m, lg.max(axis=-1, keepdims=True))
            p = jnp.exp(lg - m_new)
            alpha = jnp.exp(m - m_new)
            l = alpha * l + p.sum(axis=-1, keepdims=True)
            acc = alpha * acc + _dot(p.astype(BF16), v_buf[c * ck:(c + 1) * ck, cols].astype(BF16))
            m = m_new
        kn = kn_ref[0][:, cols].astype(BF16).astype(F32)
        vn = vn_ref[0][:, cols].astype(BF16).astype(F32)
        lg = (qg.astype(F32) * kn).sum(axis=-1, keepdims=True) + bias_new
        m_new = jnp.maximum(m, lg)
        p = jnp.exp(lg - m_new)
        alpha = jnp.exp(m - m_new)
        l = alpha * l + p
        acc = alpha * acc + p * vn
        o = acc / l
        heads += [o[hh:hh + 1, :] for hh in range(GQA)]
    out_ref[0] = jnp.concatenate(heads, axis=1)


def _sample_attention(page_table, qi_s, qa_s, wi_s, ki_new, k_new, v_new, cache_idx_k, cache_k, cache_v):
    nbatch, n_pages = page_table.shape
    n_pool, ps, d_idx = cache_idx_k.shape
    past = n_pages * ps
    ck = min(SAMPLE_CK, past)
    assert past % ck == 0 and ck % ps == 0
    k_sel = min(TOPK_MAX, (past + 1) // 4)
    kern = functools.partial(_sample_attn_kernel, n_pages=n_pages, ps=ps, ck=ck, k_sel=k_sel,
                             idx_bits=past.bit_length())
    per_b = lambda shape: pl.BlockSpec((1,) + shape, lambda b, pt: (b, 0, 0))
    any_spec = pl.BlockSpec(memory_space=pl.ANY)
    grid_spec = pltpu.PrefetchScalarGridSpec(
        num_scalar_prefetch=1, grid=(nbatch,),
        in_specs=[per_b((N_HEADS_IDX, d_idx)), per_b((N_HEADS_A, HEAD_DIM_A)), per_b((N_HEADS_IDX, 1)),
                  per_b((1, d_idx)), per_b((1, WA_KV)), per_b((1, WA_KV)), any_spec, any_spec, any_spec],
        out_specs=per_b((1, WA_Q)),
        scratch_shapes=[pltpu.VMEM((past, d_idx), F32), pltpu.VMEM((past, WA_KV), F32), pltpu.VMEM((past, WA_KV), F32),
                        pltpu.VMEM((past // ck, ck), F32), pltpu.SemaphoreType.DMA((3,))])
    return pl.pallas_call(kern, out_shape=jax.ShapeDtypeStruct((nbatch, 1, WA_Q), F32), grid_spec=grid_spec,
                          compiler_params=_cparams(), name="sample_attn")(
                              page_table, qi_s, qa_s, wi_s, ki_new, k_new, v_new, cache_idx_k,
                              cache_k.reshape(n_pool, ps, WA_KV), cache_v.reshape(n_pool, ps, WA_KV))


def _rope_tables(pos):
    rot = HEAD_DIM_A // ROT_FRACTION
    half = rot // 2
    inv_freq = ROPE_THETA ** (-jnp.arange(half, dtype=F32) / half)
    ang = pos.astype(F32)[:, None] * inv_freq[None, :]
    cos, sin = jnp.cos(ang), jnp.sin(ang)
    n = pos.shape[0]
    z = lambda w: jnp.zeros((n, w), F32)
    c = jnp.concatenate([cos, cos, jnp.ones((n, HEAD_DIM_A - rot), F32)], axis=1)
    sa = jnp.concatenate([-sin, z(HEAD_DIM_A - half)], axis=1)
    sb = jnp.concatenate([z(half), sin, z(HEAD_DIM_A - rot)], axis=1)
    rep = LANES // HEAD_DIM_A
    return tuple(jnp.tile(a, (1, rep)) for a in (c, sa, sb))


def _prep_params(w_in, g_in, g_qa, g_ka):
    d = w_in.shape[0]
    w_qa, w_ka, w_va, w_qi, w_ki, w_wi = _split_w_in(w_in, d)[:6]
    wa = jnp.concatenate([w_qa, w_ka, w_va, w_qi, w_ki, jnp.zeros((d, LANES - D_IDX), w_in.dtype)], axis=1)
    lane = jnp.arange(LANES)
    bd = (lane[:, None] // HEAD_DIM_A == lane[None, :] // HEAD_DIM_A).astype(BF16)
    rep = LANES // HEAD_DIM_A
    return (g_in.reshape(1, d), wa.astype(BF16), w_wi.T.astype(BF16), w_va.T.astype(BF16), bd,
            jnp.tile(g_qa, rep).reshape(1, LANES), jnp.tile(g_ka, rep).reshape(1, LANES))


def _split_w_in(w_in, d):
    sizes = (WA_Q, WA_KV, WA_KV, WI_Q, D_IDX, N_HEADS_IDX, WIDTH_B, WIDTH_B, WM_Q, N_BRANCH * d)
    parts, o = [], 0
    for s in sizes:
        parts.append(w_in[:, o:o + s])
        o += s
    return parts


def _mix_kernel(*refs, chunked, shared_mem, emit_vb, n_experts):
    it = iter(refs)
    x_ref, oa_ref, gin_ref, wb_ref, gvb_ref, gqm_ref = (next(it) for _ in range(6))
    if chunked:
        wtril_ref, bt_ref = next(it), next(it)
    else:
        wdiag_ref, bdiag_ref = next(it), next(it)
    mk_ref, mv_ref = next(it), next(it)
    wbra_ref, wbrb_ref, wbrm_ref, wout_ref, gffn_ref, wrh_ref, wrl_ref, br_ref = (next(it) for _ in range(8))
    x1_ref, h2_ref, te_ref, tw_ref = (next(it) for _ in range(4))
    vb_ref = next(it) if emit_vb else None

    x = x_ref[...]
    tm, d = x.shape
    h = _rms(x, gin_ref[...]).astype(BF16)
    p = _dot(h, wb_ref[...])
    ub = p[:, :WIDTH_B]
    vb = _rms(p[:, WIDTH_B:2 * WIDTH_B], gvb_ref[...])
    if emit_vb:
        vb_ref[...] = vb
    o = 2 * WIDTH_B
    qm = p[:, o:o + WM_Q]
    o += WM_Q
    gates = p[:, o:o + N_BRANCH * d]

    if chunked:
        vbb = vb.astype(BF16)
        rows = []
        for cc in range(tm // CHUNK):
            cols = []
            for g in range(N_GROUPS_B):
                vg = vbb[cc * CHUNK:(cc + 1) * CHUNK, g * GROUP_DIM_B:(g + 1) * GROUP_DIM_B]
                cols.append(_dot(wtril_ref[g], vg))
            rows.append(jnp.concatenate(cols, axis=1) + bt_ref[...])
        z = jnp.concatenate(rows, axis=0) if len(rows) > 1 else rows[0]
    else:
        z = vb * wdiag_ref[...] + bdiag_ref[...]
    out_b = ub * z

    scale_m = HEAD_DIM_M ** -0.5
    outs = []
    for hh in range(N_HEADS_M):
        sl = slice(hh * HEAD_DIM_M, (hh + 1) * HEAD_DIM_M)
        qh = _rms(qm[:, sl], gqm_ref[...])
        if shared_mem:
            lg = _dot_nt(qh.astype(BF16), mk_ref[:, sl].astype(BF16)) * scale_m
            pm = jnp.exp(lg - lg.max(axis=-1, keepdims=True))
            pm = pm / pm.sum(axis=-1, keepdims=True)
            outs.append(_dot(pm.astype(BF16), mv_ref[:, sl].astype(BF16)))
        else:
            per_row = []
            for r in range(tm):
                kr = mk_ref[r, :, sl]
                lg = (kr * qh[r:r + 1, :]).sum(axis=-1, keepdims=True) * scale_m
                pm = jnp.exp(lg - lg.max(axis=0, keepdims=True))
                pm = pm / pm.sum(axis=0, keepdims=True)
                per_row.append((pm * mv_ref[r, :, sl]).sum(axis=0, keepdims=True))
            outs.append(jnp.concatenate(per_row, axis=0))
    out_m = jnp.concatenate(outs, axis=1)

    sig = jax.nn.sigmoid
    merged = (sig(gates[:, :d]) * _dot(oa_ref[...].astype(BF16), wbra_ref[...])
              + sig(gates[:, d:2 * d]) * _dot(out_b.astype(BF16), wbrb_ref[...])
              + sig(gates[:, 2 * d:]) * _dot(out_m.astype(BF16), wbrm_ref[...]))
    x1 = x + _dot(merged.astype(BF16), wout_ref[...])
    x1_ref[...] = x1
    h2 = _rms(x1, gffn_ref[...])
    h2_ref[...] = h2.astype(h2_ref.dtype)

    hi = h2.astype(BF16)
    lo = (h2 - hi.astype(F32)).astype(BF16)
    lg = _dot(hi, wrh_ref[...]) + _dot(hi, wrl_ref[...]) + _dot(lo, wrh_ref[...]) + br_ref[...]
    lane = lax.broadcasted_iota(I32, lg.shape, 1)
    lg = jnp.where(lane < n_experts, lg, -jnp.inf)
    vals, idxs = [], []
    for _ in range(TOP_K_MOE):
        m = lg.max(axis=-1, keepdims=True)
        idx = jnp.where(lg == m, lane, LANES).min(axis=-1, keepdims=True)
        vals.append(m)
        idxs.append(idx)
        lg = jnp.where(lane == idx, -jnp.inf, lg)
    ex = [jnp.exp(v - vals[0]) for v in vals]
    den = ex[0]
    for e in ex[1:]:
        den = den + e
    te = jnp.full(lane.shape, -1, I32)
    tw = jnp.zeros(lane.shape, F32)
    for j in range(TOP_K_MOE):
        te = jnp.where(lane == j, idxs[j], te)
        tw = jnp.where(lane == j, ex[j] / den, tw)
    te_ref[...] = te
    tw_ref[...] = tw


def _mix(x2d, out_a, mix_params, gmlp_params, mem_k, mem_v, *, tm, chunked, shared_mem, emit_vb, n_experts):
    n, d = x2d.shape
    gin, wb, gvb, gqm, wbra, wbrb, wbrm, wout, gffn, wrh, wrl, br = mix_params
    row = lambda w: pl.BlockSpec((tm, w), lambda i: (i, 0))
    if shared_mem:
        mem_specs = [_const_spec(mem_k.shape), _const_spec(mem_v.shape)]
    else:
        mspec = pl.BlockSpec((tm,) + mem_k.shape[1:], lambda i: (i, 0, 0))
        mem_specs = [mspec, mspec]
    consts = lambda arrs: [_const_spec(a.shape) for a in arrs]
    in_specs = ([row(d), row(WA_Q)] + consts([gin, wb, gvb, gqm]) + consts(gmlp_params) + mem_specs
                + consts([wbra, wbrb, wbrm, wout, gffn, wrh, wrl, br]))
    out_shape = [jax.ShapeDtypeStruct((n, d), F32), jax.ShapeDtypeStruct((n, d), F32),
                 jax.ShapeDtypeStruct((n, LANES), I32), jax.ShapeDtypeStruct((n, LANES), F32)]
    out_specs = [row(d), row(d), row(LANES), row(LANES)]
    if emit_vb:
        out_shape.append(jax.ShapeDtypeStruct((n, WIDTH_B), F32))
        out_specs.append(row(WIDTH_B))
    kern = functools.partial(_mix_kernel, chunked=chunked, shared_mem=shared_mem, emit_vb=emit_vb,
                             n_experts=n_experts)
    return pl.pallas_call(kern, out_shape=tuple(out_shape), grid=(n // tm,), in_specs=in_specs,
                          out_specs=tuple(out_specs), compiler_params=_cparams(), name="mix")(
                              x2d, out_a, gin, wb, gvb, gqm, *gmlp_params, mem_k, mem_v,
                              wbra, wbrb, wbrm, wout, gffn, wrh, wrl, br)


def _prep_mix_params(w_in, g_in, g_vb, g_qm, w_br_a, w_br_b, w_br_m, w_out, g_ffn, w_router, b_router):
    d = w_in.shape[0]
    parts = _split_w_in(w_in, d)
    wb = jnp.concatenate(parts[6:10], axis=1).astype(BF16)
    n_e = w_router.shape[1]
    wr = jnp.pad(w_router, ((0, 0), (0, LANES - n_e)))
    wrh = wr.astype(BF16)
    wrl = (wr - wrh.astype(F32)).astype(BF16)
    br = jnp.pad(b_router, (0, LANES - n_e)).reshape(1, LANES)
    return (g_in.reshape(1, d), wb, g_vb.reshape(1, WIDTH_B), g_qm.reshape(1, HEAD_DIM_M),
            w_br_a.astype(BF16), w_br_b.astype(BF16), w_br_m.astype(BF16), w_out.astype(BF16),
            g_ffn.reshape(1, d), wrh, wrl, br)


def _prep_gmlp_chunked(w_s, b_s):
    tril = jnp.tril(jnp.ones((CHUNK, CHUNK), w_s.dtype))
    wtril = (w_s * tril).astype(BF16)
    bt = jnp.repeat(b_s.T, GROUP_DIM_B, axis=1)
    return wtril, bt


def _prep_gmlp_single(w_s, b_s):
    wdiag = jnp.repeat(w_s[:, 0, 0], GROUP_DIM_B).reshape(1, WIDTH_B)
    bdiag = jnp.repeat(b_s[:, 0], GROUP_DIM_B).reshape(1, WIDTH_B)
    return wdiag, bdiag


MOE_TP = 256


def _plan_kernel(te_ref, tri_ref, rank_ref, cnt_ref, carry_ref):
    @pl.when(pl.program_id(0) == 0)
    def _():
        carry_ref[...] = jnp.zeros(carry_ref.shape, F32)

    te = te_ref[...]
    lane = lax.broadcasted_iota(I32, te.shape, 1)
    picks = [te[:, j:j + 1] for j in range(TOP_K_MOE)]
    onehot = jnp.zeros(te.shape, F32)
    for e in picks:
        onehot = onehot + jnp.where(lane == e, 1.0, 0.0)
    before = _dot(tri_ref[...], onehot.astype(BF16)) + carry_ref[...]
    rank = jnp.zeros(te.shape, I32)
    for j, e in enumerate(picks):
        r = jnp.where(lane == e, before, 0.0).sum(axis=-1, keepdims=True).astype(I32)
        rank = jnp.where(lane == j, r, rank)
    rank_ref[...] = rank
    carry_ref[...] = carry_ref[...] + onehot.sum(axis=0, keepdims=True)
    cnt_ref[...] = carry_ref[...].astype(I32)


def _moe_plan(te_all):
    n = te_all.shape[0]
    tp = MOE_TP
    r = jnp.arange(tp)
    tri = (r[:, None] > r[None, :]).astype(BF16)
    return pl.pallas_call(
        _plan_kernel,
        out_shape=(jax.ShapeDtypeStruct((n, LANES), I32), jax.ShapeDtypeStruct((1, LANES), I32)),
        grid=(n // tp,),
        in_specs=[pl.BlockSpec((tp, LANES), lambda i: (i, 0)), _const_spec((tp, tp))],
        out_specs=(pl.BlockSpec((tp, LANES), lambda i: (i, 0)), _const_spec((1, LANES))),
        scratch_shapes=[pltpu.VMEM((1, LANES), F32)],
        compiler_params=_cparams(), name="moe_plan")(te_all, tri)


def _load_slots(dest_ref, idx_smem, sem):
    cp = pltpu.make_async_copy(dest_ref.at[0, 0], idx_smem, sem)
    cp.start()
    cp.wait()


def _dispatch_kernel(dest_ref, h2_hbm, xs_in, xs_out, idx_smem, sem_idx, sem_rows, *, tp):
    del xs_in
    n_rows = tp * TOP_K_MOE
    base = pl.program_id(0) * tp
    _load_slots(dest_ref, idx_smem, sem_idx)

    def row_copy(a):
        return pltpu.make_async_copy(h2_hbm.at[pl.ds(base + a // TOP_K_MOE, 1), :],
                                     xs_out.at[pl.ds(idx_smem[a], 1), :], sem_rows)

    lax.fori_loop(0, n_rows, lambda a, c: (row_copy(a).start(), c)[1], 0)
    lax.fori_loop(0, n_rows, lambda a, c: (row_copy(a).wait(), c)[1], 0)


def _dispatch(dest3, h2, xs):
    n_tiles, _, n_rows = dest3.shape
    tp = n_rows // TOP_K_MOE
    kern = functools.partial(_dispatch_kernel, tp=tp)
    any_spec = pl.BlockSpec(memory_space=pl.ANY)
    return pl.pallas_call(
        kern, out_shape=jax.ShapeDtypeStruct(xs.shape, xs.dtype), grid=(n_tiles,),
        in_specs=[pl.BlockSpec((1, 1, n_rows), lambda i: (i, 0, 0)), any_spec, any_spec],
        out_specs=any_spec,
        scratch_shapes=[pltpu.SMEM((n_rows,), I32), pltpu.SemaphoreType.DMA, pltpu.SemaphoreType.DMA],
        input_output_aliases={2: 0},
        compiler_params=pltpu.CompilerParams(dimension_semantics=("arbitrary",), has_side_effects=True),
        name="dispatch")(dest3, h2, xs)


def _expert_kernel(be_ref, nu_ref, xs_ref, wg_ref, wu_ref, wd_ref, bg_ref, bu_ref, bd_ref, y_ref,
                   wg_bf, wu_bf, wd_bf):
    b = pl.program_id(0)

    @pl.when(b >= nu_ref[0])
    def _():
        y_ref[...] = jnp.zeros(y_ref.shape, F32)

    @pl.when(b < nu_ref[0])
    def _():
        prev = be_ref[jnp.maximum(b - 1, 0)]

        @pl.when(jnp.logical_or(b == 0, be_ref[b] != prev))
        def _():
            wg_bf[...] = wg_ref[0].astype(BF16)
            wu_bf[...] = wu_ref[0].astype(BF16)
            wd_bf[...] = wd_ref[0].astype(BF16)

        xb = xs_ref[...].astype(BF16)
        ff = wg_bf.shape[1]
        y = jnp.zeros(y_ref.shape, F32) + bd_ref[0]
        for n in range(ff // FF_CHUNK):
            sl = slice(n * FF_CHUNK, (n + 1) * FF_CHUNK)
            hg = jnp.minimum(_dot(xb, wg_bf[:, sl]) + bg_ref[0][:, sl], SWIGLU_LIMIT)
            hu = jnp.clip(_dot(xb, wu_bf[:, sl]) + bu_ref[0][:, sl], -SWIGLU_LIMIT, SWIGLU_LIMIT)
            act = hg * jax.nn.sigmoid(SWIGLU_ALPHA * hg) * (hu + 1.0)
            y = y + _dot(act.astype(BF16), wd_bf[sl, :])
        y_ref[...] = y


def _experts(xs, block_e, n_used, w_gate, b_gate, w_up, b_up, w_down, b_down):
    ns, d = xs.shape
    n_e, _, ff = w_gate.shape
    bm = MOE_BM
    blk = lambda b, be, nu: (jnp.minimum(b, nu[0] - 1), 0)
    wsel = lambda b, be, nu: (be[b], 0, 0)
    grid_spec = pltpu.PrefetchScalarGridSpec(
        num_scalar_prefetch=2, grid=(ns // bm,),
        in_specs=[pl.BlockSpec((bm, d), blk),
                  pl.BlockSpec((1, d, ff), wsel), pl.BlockSpec((1, d, ff), wsel), pl.BlockSpec((1, ff, d), wsel),
                  pl.BlockSpec((1, 1, ff), wsel), pl.BlockSpec((1, 1, ff), wsel), pl.BlockSpec((1, 1, d), wsel)],
        out_specs=pl.BlockSpec((bm, d), lambda b, be, nu: (b, 0)),
        scratch_shapes=[pltpu.VMEM((d, ff), BF16), pltpu.VMEM((d, ff), BF16), pltpu.VMEM((ff, d), BF16)])
    return pl.pallas_call(_expert_kernel, out_shape=jax.ShapeDtypeStruct((ns, d), F32), grid_spec=grid_spec,
                          compiler_params=_cparams(), name="experts")(
                              block_e, n_used, xs, w_gate, w_up, w_down,
                              b_gate.reshape(n_e, 1, ff), b_up.reshape(n_e, 1, ff), b_down.reshape(n_e, 1, d))


def _combine_kernel(dest_ref, tw_ref, x1_ref, yb_hbm, y_ref, idx_smem, buf, sem_idx, sem_rows, *, tp):
    n_rows = tp * TOP_K_MOE
    _load_slots(dest_ref, idx_smem, sem_idx)

    def row_copy(a):
        return pltpu.make_async_copy(yb_hbm.at[pl.ds(idx_smem[a], 1), :],
                                     buf.at[a % TOP_K_MOE, pl.ds(a // TOP_K_MOE, 1), :], sem_rows)

    lax.fori_loop(0, n_rows, lambda a, c: (row_copy(a).start(), c)[1], 0)
    lax.fori_loop(0, n_rows, lambda a, c: (row_copy(a).wait(), c)[1], 0)
    tw = tw_ref[...]
    y = x1_ref[...]
    for j in range(TOP_K_MOE):
        y = y + tw[:, j:j + 1] * buf[j]
    y_ref[...] = y


def _combine(dest3, tw, x1, yb):
    n_tiles, _, n_rows = dest3.shape
    tp = n_rows // TOP_K_MOE
    d = x1.shape[1]
    kern = functools.partial(_combine_kernel, tp=tp)
    return pl.pallas_call(
        kern, out_shape=jax.ShapeDtypeStruct(x1.shape, F32), grid=(n_tiles,),
        in_specs=[pl.BlockSpec((1, 1, n_rows), lambda i: (i, 0, 0)), pl.BlockSpec((tp, LANES), lambda i: (i, 0)),
                  pl.BlockSpec((tp, d), lambda i: (i, 0)), pl.BlockSpec(memory_space=pl.ANY)],
        out_specs=pl.BlockSpec((tp, d), lambda i: (i, 0)),
        scratch_shapes=[pltpu.SMEM((n_rows,), I32), pltpu.VMEM((TOP_K_MOE, tp, d), F32),
                        pltpu.SemaphoreType.DMA, pltpu.SemaphoreType.DMA],
        compiler_params=_cparams(), name="combine")(dest3, tw, x1, yb)


def _moe(groups, w_gate, b_gate, w_up, b_up, w_down, b_down):
    n_e = w_gate.shape[0]
    d = groups[0][0].shape[1]
    tp, bm = MOE_TP, MOE_BM
    sizes = [g[0].shape[0] for g in groups]
    assert all(n % tp == 0 or n < tp for n in sizes), sizes
    te_all = jnp.concatenate([g[2] for g in groups], axis=0)
    n_all = te_all.shape[0]
    n_plan = -(-n_all // tp) * tp
    te_all = jnp.pad(te_all, ((0, n_plan - n_all), (0, 0)), constant_values=-1)
    rank, counts = _moe_plan(te_all)
    counts = counts[0, :n_e]
    padded_cnt = (counts + bm - 1) // bm * bm
    pad_end = jnp.cumsum(padded_cnt)
    pad_start = pad_end - padded_cnt
    n_assign = n_all * TOP_K_MOE
    nb = -(-n_assign // bm) + n_e
    n_used = (pad_end[-1] // bm).astype(I32)
    blocks = jnp.arange(nb, dtype=I32)
    block_e = jnp.minimum(jnp.searchsorted(pad_end, blocks * bm, side='right'), n_e - 1).astype(I32)
    block_e = jnp.where(blocks < n_used, block_e, block_e[jnp.maximum(n_used - 1, 0)])
    te4 = te_all[:n_all, :TOP_K_MOE]
    dest = (pad_start[te4] + rank[:n_all, :TOP_K_MOE]).astype(I32)

    xs = jnp.zeros((nb * bm, d), F32)
    dests, o = [], 0
    for (x1, h2, te, tw), n in zip(groups, sizes):
        dg = dest[o:o + n]
        o += n
        t = tp if n >= tp else n
        dg3 = dg.reshape(n // t, 1, t * TOP_K_MOE)
        dests.append(dg3)
        xs = _dispatch(dg3, h2, xs)
    yb = _experts(xs, block_e, n_used.reshape(1), w_gate, b_gate, w_up, b_up, w_down, b_down)
    return [_combine(dg3, tw, x1, yb) for (x1, h2, te, tw), dg3 in zip(groups, dests)]


def _row_tile(n, want):
    return want if n % want == 0 else n


def kernel(x_prompt, x_sample, mem_prompt, cache_k, cache_v, cache_idx_k, cache_mem_k, cache_mem_v, page_table,
           g_in, w_in, g_qa, g_ka, g_vb, w_s, b_s, g_qm, g_mem, w_mem_kv, g_km, w_br_a, w_br_b, w_br_m, w_out,
           g_ffn, w_router, b_router, w_gate, b_gate, w_up, b_up, w_down, b_down):
    bp, t, d = x_prompt.shape
    bs, ts, _ = x_sample.shape
    assert bp == 1 and ts == 1 and t % CHUNK == 0
    n_mem = mem_prompt.shape[1]
    n_pages = page_table.shape[1]
    ps = cache_k.shape[1]
    past = n_pages * ps
    n_e = w_router.shape[1]

    fp = _prep_params(w_in, g_in, g_qa, g_ka)
    mp = _prep_mix_params(w_in, g_in, g_vb, g_qm, w_br_a, w_br_b, w_br_m, w_out, g_ffn, w_router, b_router)

    xp = x_prompt.reshape(t, d)
    qa_p, k32_p, khm_p, v32_p, vt_p, qi_p, ki32_p, kibf_p, wit_p = _front(
        xp, _rope_tables(jnp.arange(t, dtype=I32)), *fp, tm=_row_tile(t, 512))
    mem_k, mem_v = _memory_kv(mem_prompt.reshape(n_mem, d), g_mem.reshape(1, d), w_mem_kv.astype(BF16),
                              g_km.reshape(1, HEAD_DIM_M))
    out_a_p = _prompt_attention(qi_p, qa_p, wit_p, kibf_p, khm_p, vt_p)
    x1_p, h2_p, te_p, tw_p = _mix(xp, out_a_p, mp, _prep_gmlp_chunked(w_s, b_s), mem_k, mem_v,
                                  tm=_row_tile(t, 256), chunked=True, shared_mem=True, emit_vb=False, n_experts=n_e)

    xs = x_sample.reshape(bs, d)
    qa_s, k32_s, _, v32_s, _, qi_s, ki32_s, _, wit_s = _front(
        xs, _rope_tables(jnp.full((bs,), past, I32)), *fp, tm=bs)
    out_a_s = _sample_attention(page_table, jnp.moveaxis(qi_s, 0, 1), jnp.moveaxis(qa_s, 0, 1),
                                wit_s.T.reshape(bs, N_HEADS_IDX, 1), ki32_s.reshape(bs, 1, D_IDX),
                                k32_s.reshape(bs, 1, WA_KV), v32_s.reshape(bs, 1, WA_KV),
                                cache_idx_k, cache_k, cache_v)
    x1_s, h2_s, te_s, tw_s, vb_s = _mix(xs, out_a_s.reshape(bs, WA_Q), mp, _prep_gmlp_single(w_s, b_s),
                                        cache_mem_k.reshape(bs, n_mem, WM_Q), cache_mem_v.reshape(bs, n_mem, WM_Q),
                                        tm=_row_tile(bs, SUBLANES), chunked=False, shared_mem=False, emit_vb=True,
                                        n_experts=n_e)

    y_p, y_s = _moe([(x1_p, h2_p, te_p, tw_p), (x1_s, h2_s, te_s, tw_s)],
                    w_gate, b_gate, w_up, b_up, w_down, b_down)

    n_pg = t // ps
    return (y_p.reshape(1, t, d), y_s.reshape(bs, 1, d),
            k32_p.reshape(1, n_pg, ps, N_KV_A, HEAD_DIM_A), v32_p.reshape(1, n_pg, ps, N_KV_A, HEAD_DIM_A),
            ki32_p.reshape(1, n_pg, ps, D_IDX),
            mem_k.reshape(1, n_mem, N_HEADS_M, HEAD_DIM_M), mem_v.reshape(1, n_mem, N_HEADS_M, HEAD_DIM_M),
            k32_s.reshape(bs, 1, N_KV_A, HEAD_DIM_A), v32_s.reshape(bs, 1, N_KV_A, HEAD_DIM_A),
            ki32_s.reshape(bs, 1, D_IDX), vb_s.reshape(bs, 1, WIDTH_B))
```

```python
import functools

import jax
import jax.numpy as jnp
from jax import lax
from jax.experimental import pallas as pl
from jax.experimental.pallas import tpu as pltpu

F32 = jnp.float32
BF16 = jnp.bfloat16
I32 = jnp.int32

N_HEADS_A = 8
N_KV_A = 2
HEAD_DIM_A = 64
TOPK_MAX = 256
N_HEADS_IDX = 8
D_IDX = 64
N_GROUPS_B = 4
GROUP_DIM_B = 128
WIDTH_B = N_GROUPS_B * GROUP_DIM_B
CHUNK = 128
N_HEADS_M = 4
HEAD_DIM_M = 128
ROPE_THETA = 500000.0
ROT_FRACTION = 4
N_BRANCH = 3
TOP_K_MOE = 4
SWIGLU_LIMIT = 7.0
SWIGLU_ALPHA = 1.702
EPS = 1e-6

WA_Q = N_HEADS_A * HEAD_DIM_A
WA_KV = N_KV_A * HEAD_DIM_A
WI_Q = N_HEADS_IDX * D_IDX
WM_Q = N_HEADS_M * HEAD_DIM_M
GQA = N_HEADS_A // N_KV_A

LANES = 128
SUBLANES = 8
VMEM_LIMIT = 56 * 1024 * 1024

INT_MIN = -(2 ** 31)
INT_MAX = 2 ** 31 - 1
NEG = -1e30

ATTN_BQ = 128
ATTN_SC = 512
MOE_BM = 512
FF_CHUNK = 256


def _cparams(n_axes=1, vmem=VMEM_LIMIT):
    return pltpu.CompilerParams(dimension_semantics=("arbitrary",) * n_axes, vmem_limit_bytes=vmem)


def _const_spec(shape):
    zeros = (0,) * len(shape)
    return pl.BlockSpec(shape, lambda *_: zeros)


def _dot(a, b):
    return jnp.dot(a, b, preferred_element_type=F32)


def _dot_nt(a, b):
    return lax.dot_general(a, b, (((1,), (1,)), ((), ())), preferred_element_type=F32)


def _rms(x, g):
    return x * lax.rsqrt(jnp.mean(x * x, axis=-1, keepdims=True) + EPS) * g


def _split_dot(x, w):
    hi = x.astype(BF16)
    lo = (x - hi.astype(F32)).astype(BF16)
    return _dot(hi, w) + _dot(lo, w)


def _front_kernel(x_ref, gin_ref, wa_ref, wwit_ref, wvt_ref, bd_ref, gqa_ref, gka_ref, cos_ref, sa_ref, sb_ref,
                  qa_ref, k32_ref, khm_ref, v32_ref, vt_ref, qi_ref, ki32_ref, kibf_ref, wit_ref, *, score_scale):
    x = x_ref[...]
    h = _rms(x, gin_ref[...]).astype(BF16)
    p = _dot(h, wa_ref[...])
    cos, sa, sb = cos_ref[...], sa_ref[...], sb_ref[...]
    bd = bd_ref[...]

    def head_norm(v, g):
        ssq = _split_dot(v * v, bd)
        return v * lax.rsqrt(ssq * (1.0 / HEAD_DIM_A) + EPS) * g

    def rope(v):
        return v * cos + pltpu.roll(v, LANES - 8, 1) * sa + pltpu.roll(v, 8, 1) * sb

    for s in range(WA_Q // LANES):
        v = p[:, s * LANES:(s + 1) * LANES]
        v = rope(head_norm(v, gqa_ref[...])) * (HEAD_DIM_A ** -0.5)
        vb = v.astype(BF16)
        qa_ref[2 * s] = vb[:, :HEAD_DIM_A]
        qa_ref[2 * s + 1] = vb[:, HEAD_DIM_A:]
    o = WA_Q
    k = rope(head_norm(p[:, o:o + WA_KV], gka_ref[...]))
    k32_ref[...] = k
    kb = k.astype(BF16)
    khm_ref[0] = kb[:, :HEAD_DIM_A]
    khm_ref[1] = kb[:, HEAD_DIM_A:]
    o += WA_KV
    v32_ref[...] = p[:, o:o + WA_KV]
    o += WA_KV
    for s in range(WI_Q // LANES):
        vb = rope(p[:, o + s * LANES:o + (s + 1) * LANES]).astype(BF16)
        qi_ref[2 * s] = vb[:, :D_IDX]
        qi_ref[2 * s + 1] = vb[:, D_IDX:]
    o += WI_Q
    ki = rope(p[:, o:o + LANES])[:, :D_IDX]
    ki32_ref[...] = ki
    kibf_ref[...] = ki.astype(BF16)
    wit_ref[...] = _dot_nt(wwit_ref[...], h) * score_scale
    vt_ref[...] = _dot_nt(wvt_ref[...], h).astype(BF16)


def _front(x2d, tabs, gin, wa, wwit, wvt, bd, gqa, gka, tm):
    n, d = x2d.shape
    cos, sa, sb = tabs
    row = lambda w: pl.BlockSpec((tm, w), lambda i: (i, 0))
    hm = lambda nh, w: pl.BlockSpec((nh, tm, w), lambda i: (0, i, 0))
    out_shape = (
        jax.ShapeDtypeStruct((N_HEADS_A, n, HEAD_DIM_A), BF16),
        jax.ShapeDtypeStruct((n, WA_KV), F32),
        jax.ShapeDtypeStruct((N_KV_A, n, HEAD_DIM_A), BF16),
        jax.ShapeDtypeStruct((n, WA_KV), F32),
        jax.ShapeDtypeStruct((WA_KV, n), BF16),
        jax.ShapeDtypeStruct((N_HEADS_IDX, n, D_IDX), BF16),
        jax.ShapeDtypeStruct((n, D_IDX), F32),
        jax.ShapeDtypeStruct((n, D_IDX), BF16),
        jax.ShapeDtypeStruct((N_HEADS_IDX, n), F32),
    )
    out_specs = (hm(N_HEADS_A, HEAD_DIM_A), row(WA_KV), hm(N_KV_A, HEAD_DIM_A), row(WA_KV),
                 pl.BlockSpec((WA_KV, tm), lambda i: (0, i)), hm(N_HEADS_IDX, D_IDX), row(D_IDX), row(D_IDX),
                 pl.BlockSpec((N_HEADS_IDX, tm), lambda i: (0, i)))
    in_specs = [row(d), _const_spec(gin.shape), _const_spec(wa.shape), _const_spec(wwit.shape),
                _const_spec(wvt.shape), _const_spec(bd.shape), _const_spec(gqa.shape), _const_spec(gka.shape),
                row(LANES), row(LANES), row(LANES)]
    kern = functools.partial(_front_kernel, score_scale=D_IDX ** -0.5 * N_HEADS_IDX ** -0.5)
    return pl.pallas_call(kern, out_shape=out_shape, grid=(n // tm,), in_specs=in_specs, out_specs=out_specs,
                          compiler_params=_cparams(), name="front")(x2d, gin, wa, wwit, wvt, bd, gqa, gka, cos, sa, sb)


def _memkv_kernel(mem_ref, gmem_ref, w_ref, gkm_ref, k32_ref, v32_ref):
    h = _rms(mem_ref[...], gmem_ref[...]).astype(BF16)
    kv = _dot(h, w_ref[...])
    for hh in range(N_HEADS_M):
        sl = slice(hh * HEAD_DIM_M, (hh + 1) * HEAD_DIM_M)
        k32_ref[:, sl] = _rms(kv[:, sl], gkm_ref[...])
    v32_ref[...] = kv[:, WM_Q:]


def _memory_kv(mem2d, gmem, w, gkm):
    n = mem2d.shape[0]
    out = jax.ShapeDtypeStruct((n, WM_Q), F32)
    return pl.pallas_call(_memkv_kernel, out_shape=(out, out), name="memory_kv",
                          compiler_params=pltpu.CompilerParams(vmem_limit_bytes=VMEM_LIMIT))(mem2d, gmem, w, gkm)


def _sort_key(s):
    s = jnp.where(s == 0.0, 0.0, s)
    bits = pltpu.bitcast(s, I32)
    return bits ^ ((bits >> 31) & INT_MAX)


def _attn_kernel(qi_ref, qa_ref, wi_ref, ki_ref, k_ref, vt_ref, out_ref,
                 keys_ref, thr_ref, cut_ref, m_ref, l_ref, acc_ref, *, k_sel, idx_bits):
    bq, sc = ATTN_BQ, ATTN_SC
    i = pl.program_id(0)
    n_ch = (i * bq) // sc + 1
    t_pos = i * bq + lax.broadcasted_iota(I32, (1, bq), 1)
    sub = lax.broadcasted_iota(I32, (sc, 1), 0)
    qi = qi_ref[...].reshape(N_HEADS_IDX * bq, D_IDX)
    w = wi_ref[...]

    def score_chunk(c, carry):
        off = pl.multiple_of(c * sc, sc)
        d = _dot_nt(ki_ref[pl.ds(off, sc), :], qi)
        s = jnp.zeros((sc, bq), F32)
        for h in range(N_HEADS_IDX):
            s = s + jnp.maximum(d[:, h * bq:(h + 1) * bq], 0.0) * w[h:h + 1, :]
        key = jnp.where(off + sub <= t_pos, _sort_key(s), INT_MIN)
        keys_ref[pl.ds(off, sc), :] = key
        return carry

    lax.fori_loop(0, n_ch, score_chunk, 0)

    def count(pred):
        def body(c, acc):
            off = pl.multiple_of(c * sc, sc)
            hit = jnp.where(pred(keys_ref[pl.ds(off, sc), :], off + sub), 1, 0)
            return acc + hit.reshape(sc // SUBLANES, SUBLANES, bq).sum(axis=0)
        acc = lax.fori_loop(0, n_ch, body, jnp.zeros((SUBLANES, bq), I32))
        return acc.sum(axis=0, keepdims=True)

    short = t_pos + 1 <= k_sel

    def search_cond(st):
        bit, _, cnt = st
        pending = jnp.where(jnp.logical_or(cnt == k_sel, short), 0, 1)
        return jnp.logical_and(bit >= 0, jnp.max(pending) > 0)

    def search_body(st):
        bit, thr, cnt = st
        cand = thr + (jnp.int32(1) << bit)
        c = count(lambda blk, _: blk >= cand)
        take = c >= k_sel
        return bit - 1, jnp.where(take, cand, thr), jnp.where(take, c, cnt)

    st0 = (jnp.int32(31), jnp.full((1, bq), INT_MIN, I32), jnp.zeros((1, bq), I32) + n_ch * sc)
    _, thr, cnt = lax.while_loop(search_cond, search_body, st0)
    thr_ref[...] = thr
    cut_ref[...] = jnp.full((1, bq), INT_MAX, I32)

    excess = jnp.logical_and(cnt > k_sel, thr > INT_MIN)

    @pl.when(jnp.max(jnp.where(excess, 1, 0)) > 0)
    def _():
        want = k_sel - count(lambda blk, _: blk > thr)

        def tie_body(j, cpos):
            cand = cpos + (jnp.int32(1) << (idx_bits - 1 - j))
            below = count(lambda blk, pos: jnp.where(blk == thr, pos, INT_MAX) < cand)
            return jnp.where(below < want, cand, cpos)

        cpos = lax.fori_loop(0, idx_bits, tie_body, jnp.zeros((1, bq), I32))
        cut_ref[...] = jnp.where(excess, cpos, INT_MAX)

    m_ref[...] = jnp.full(m_ref.shape, NEG, F32)
    l_ref[...] = jnp.zeros(l_ref.shape, F32)
    acc_ref[...] = jnp.zeros(acc_ref.shape, F32)
    cut = cut_ref[...]

    def attn_chunk(c, carry):
        off = pl.multiple_of(c * sc, sc)
        keyc = keys_ref[pl.ds(off, sc), :]
        tie = jnp.where(off + sub <= cut, 0.0, NEG)
        bias = jnp.where(keyc > thr, 0.0, jnp.where(keyc == thr, tie, NEG))
        bias = jnp.where(keyc == INT_MIN, NEG, bias)
        bias = jnp.concatenate([bias] * GQA, axis=1)
        for g in range(N_KV_A):
            qg = qa_ref[g * GQA:(g + 1) * GQA].reshape(GQA * bq, HEAD_DIM_A)
            lg = _dot_nt(k_ref[g, pl.ds(off, sc), :], qg) + bias
            m_old = m_ref[g]
            m_new = jnp.maximum(m_old, lg.max(axis=0, keepdims=True))
            p = jnp.exp(lg - m_new)
            alpha = jnp.exp(m_old - m_new)
            m_ref[g] = m_new
            l_ref[g] = alpha * l_ref[g] + p.sum(axis=0, keepdims=True)
            vt = vt_ref[pl.ds(g * HEAD_DIM_A, HEAD_DIM_A), pl.ds(off, sc)]
            acc_ref[g] = alpha * acc_ref[g] + _dot(vt, p.astype(BF16))
        return carry

    lax.fori_loop(0, n_ch, attn_chunk, 0)

    heads = []
    for g in range(N_KV_A):
        o = acc_ref[g] / l_ref[g]
        heads += [o[:, hh * bq:(hh + 1) * bq] for hh in range(GQA)]
    out_ref[...] = jnp.concatenate(heads, axis=0).T.astype(out_ref.dtype)


def _prompt_attention(qi_hm, qa_hm, wit, kibf, khm, vt):
    t = kibf.shape[0]
    bq = ATTN_BQ
    k_sel = min(TOPK_MAX, t // 4)
    kern = functools.partial(_attn_kernel, k_sel=k_sel, idx_bits=max(1, (t - 1).bit_length()))
    in_specs = [pl.BlockSpec((N_HEADS_IDX, bq, D_IDX), lambda i: (0, i, 0)),
                pl.BlockSpec((N_HEADS_A, bq, HEAD_DIM_A), lambda i: (0, i, 0)),
                pl.BlockSpec((N_HEADS_IDX, bq), lambda i: (0, i)),
                _const_spec(kibf.shape), _const_spec(khm.shape), _const_spec(vt.shape)]
    scratch = [pltpu.VMEM((t, bq), I32), pltpu.VMEM((1, bq), I32), pltpu.VMEM((1, bq), I32),
               pltpu.VMEM((N_KV_A, 1, GQA * bq), F32), pltpu.VMEM((N_KV_A, 1, GQA * bq), F32),
               pltpu.VMEM((N_KV_A, HEAD_DIM_A, GQA * bq), F32)]
    return pl.pallas_call(kern, out_shape=jax.ShapeDtypeStruct((t, WA_Q), BF16), grid=(t // bq,),
                          in_specs=in_specs, out_specs=pl.BlockSpec((bq, WA_Q), lambda i: (i, 0)),
                          scratch_shapes=scratch, compiler_params=_cparams(), name="attn")(
                              qi_hm, qa_hm, wit, kibf, khm, vt)


SAMPLE_CK = 2048


def _sample_attn_kernel(pt_ref, qi_ref, qa_ref, wi_ref, kin_ref, kn_ref, vn_ref, cidx_hbm, ck_hbm, cv_hbm, out_ref,
                        idx_buf, k_buf, v_buf, scr_ref, sems, *, n_pages, ps, ck, k_sel, idx_bits):
    b = pl.program_id(0)
    nb = pl.num_programs(0)
    past = n_pages * ps
    nc = past // ck

    def page_copy(which, bb, p):
        src, dst = ((cidx_hbm, idx_buf), (ck_hbm, k_buf), (cv_hbm, v_buf))[which]
        cols = pl.ds(pl.multiple_of(p * ps, ps), ps)
        dst = dst.at[:, cols] if which == 0 else dst.at[:, :, cols]
        return pltpu.make_async_copy(src.at[pt_ref[bb, p]], dst, sems.at[which])

    def for_pages(fn):
        lax.fori_loop(0, n_pages, lambda p, c: (fn(p), c)[1], 0)

    @pl.when(b == 0)
    def _():
        for_pages(lambda p: page_copy(0, b, p).start())

    for_pages(lambda p: (page_copy(1, b, p).start(), page_copy(2, b, p).start()))
    for_pages(lambda p: page_copy(0, b, p).wait())

    qi = qi_ref[0]
    w = wi_ref[0]
    for c in range(nc):
        d = _dot(qi, idx_buf[:, c * ck:(c + 1) * ck].astype(BF16))
        scr_ref[c:c + 1, :] = (jnp.maximum(d, 0.0) * w).sum(axis=0, keepdims=True)
    d_new = (qi.astype(F32) * kin_ref[0].astype(BF16).astype(F32)).sum(axis=-1, keepdims=True)
    s_new = (jnp.maximum(d_new, 0.0) * w).sum(axis=0, keepdims=True)

    @pl.when(b + 1 < nb)
    def _():
        for_pages(lambda p: page_copy(0, b + 1, p).start())

    keys = _sort_key(scr_ref[...])
    key_new = _sort_key(s_new)
    pos = (lax.broadcasted_iota(I32, keys.shape, 0) * ck + lax.broadcasted_iota(I32, keys.shape, 1))

    def count(pred):
        hit = jnp.where(pred(keys, pos), 1, 0).sum(axis=1, keepdims=True).sum(axis=0, keepdims=True)
        return hit + jnp.where(pred(key_new, past), 1, 0)

    def search_cond(st):
        bit, _, cnt = st
        return jnp.logical_and(bit >= 0, jnp.max(jnp.where(cnt == k_sel, 0, 1)) > 0)

    def search_body(st):
        bit, thr, cnt = st
        cand = thr + (jnp.int32(1) << bit)
        c = count(lambda kk, _: kk >= cand)
        take = c >= k_sel
        return bit - 1, jnp.where(take, cand, thr), jnp.where(take, c, cnt)

    st0 = (jnp.int32(31), jnp.full((1, 1), INT_MIN, I32), jnp.full((1, 1), past + 1, I32))
    _, thr, cnt = lax.while_loop(search_cond, search_body, st0)

    want = k_sel - count(lambda kk, _: kk > thr)

    def tie_body(j, cpos):
        cand = cpos + (jnp.int32(1) << (idx_bits - 1 - j))
        below = count(lambda kk, pp: jnp.where(kk == thr, pp, INT_MAX) < cand)
        return jnp.where(below < want, cand, cpos)

    cpos = lax.fori_loop(0, idx_bits, tie_body, jnp.zeros((1, 1), I32))
    cut = jnp.where(cnt > k_sel, cpos, INT_MAX)

    def sel_bias(kk, pp):
        tie = jnp.where(pp <= cut, 0.0, NEG)
        return jnp.where(kk > thr, 0.0, jnp.where(kk == thr, tie, NEG))

    bias = sel_bias(keys, pos)
    bias_new = sel_bias(key_new, past)

    for_pages(lambda p: (page_copy(1, b, p).wait(), page_copy(2, b, p).wait()))

    qa = qa_ref[0]
    heads = []
    for g in range(N_KV_A):
        qg = qa[g * GQA:(g + 1) * GQA]
        cols = slice(g * HEAD_DIM_A, (g + 1) * HEAD_DIM_A)
        m = jnp.full((GQA, 1), NEG, F32)
        l = jnp.zeros((GQA, 1), F32)
        acc = jnp.zeros((GQA, HEAD_DIM_A), F32)
        for c in range(nc):
            span = slice(c * ck, (c + 1) * ck)
            lg = _dot(qg, k_buf[g, :, span].astype(BF16)) + bias[c:c + 1, :]
            m_new = jnp.maximum(m, lg.max(axis=-1, keepdims=True))
            p = jnp.exp(lg - m_new)
            alpha = jnp.exp(m - m_new)
            l = alpha * l + p.sum(axis=-1, keepdims=True)
            acc = alpha * acc + _dot_nt(p.astype(BF16), v_buf[g, :, span].astype(BF16))
            m = m_new
        kn = kn_ref[0][:, cols].astype(BF16).astype(F32)
        vn = vn_ref[0][:, cols].astype(BF16).astype(F32)
        lg = (qg.astype(F32) * kn).sum(axis=-1, keepdims=True) + bias_new
        m_new = jnp.maximum(m, lg)
        p = jnp.exp(lg - m_new)
        alpha = jnp.exp(m - m_new)
        l = alpha * l + p
        acc = alpha * acc + p * vn
        o = acc / l
        heads += [o[hh:hh + 1, :] for hh in range(GQA)]
    out_ref[0] = jnp.concatenate(heads, axis=1)


def _sample_attention(page_table, qi_s, qa_s, wi_s, ki_new, k_new, v_new, cache_idx_k, cache_k, cache_v):
    nbatch, n_pages = page_table.shape
    n_pool, ps, d_idx = cache_idx_k.shape
    past = n_pages * ps
    ck = min(SAMPLE_CK, past)
    assert past % ck == 0 and ck % ps == 0
    k_sel = min(TOPK_MAX, (past + 1) // 4)
    kern = functools.partial(_sample_attn_kernel, n_pages=n_pages, ps=ps, ck=ck, k_sel=k_sel,
                             idx_bits=past.bit_length())
    per_b = lambda shape: pl.BlockSpec((1,) + shape, lambda b, pt: (b, 0, 0))
    any_spec = pl.BlockSpec(memory_space=pl.ANY)
    grid_spec = pltpu.PrefetchScalarGridSpec(
        num_scalar_prefetch=1, grid=(nbatch,),
        in_specs=[per_b((N_HEADS_IDX, d_idx)), per_b((N_HEADS_A, HEAD_DIM_A)), per_b((N_HEADS_IDX, 1)),
                  per_b((1, d_idx)), per_b((1, WA_KV)), per_b((1, WA_KV)), any_spec, any_spec, any_spec],
        out_specs=per_b((1, WA_Q)),
        scratch_shapes=[pltpu.VMEM((d_idx, past), F32), pltpu.VMEM((N_KV_A, HEAD_DIM_A, past), F32),
                        pltpu.VMEM((N_KV_A, HEAD_DIM_A, past), F32),
                        pltpu.VMEM((past // ck, ck), F32), pltpu.SemaphoreType.DMA((3,))])
    return pl.pallas_call(kern, out_shape=jax.ShapeDtypeStruct((nbatch, 1, WA_Q), F32), grid_spec=grid_spec,
                          compiler_params=_cparams(), name="sample_attn")(
                              page_table, qi_s, qa_s, wi_s, ki_new, k_new, v_new,
                              jnp.transpose(cache_idx_k, (0, 2, 1)), jnp.transpose(cache_k, (0, 2, 3, 1)),
                              jnp.transpose(cache_v, (0, 2, 3, 1)))


def _rope_tables(pos):
    rot = HEAD_DIM_A // ROT_FRACTION
    half = rot // 2
    inv_freq = ROPE_THETA ** (-jnp.arange(half, dtype=F32) / half)
    ang = pos.astype(F32)[:, None] * inv_freq[None, :]
    cos, sin = jnp.cos(ang), jnp.sin(ang)
    n = pos.shape[0]
    z = lambda w: jnp.zeros((n, w), F32)
    c = jnp.concatenate([cos, cos, jnp.ones((n, HEAD_DIM_A - rot), F32)], axis=1)
    sa = jnp.concatenate([-sin, z(HEAD_DIM_A - half)], axis=1)
    sb = jnp.concatenate([z(half), sin, z(HEAD_DIM_A - rot)], axis=1)
    rep = LANES // HEAD_DIM_A
    return tuple(jnp.tile(a, (1, rep)) for a in (c, sa, sb))


def _prep_params(w_in, g_in, g_qa, g_ka):
    d = w_in.shape[0]
    w_qa, w_ka, w_va, w_qi, w_ki, w_wi = _split_w_in(w_in, d)[:6]
    wa = jnp.concatenate([w_qa, w_ka, w_va, w_qi, w_ki, jnp.zeros((d, LANES - D_IDX), w_in.dtype)], axis=1)
    lane = jnp.arange(LANES)
    bd = (lane[:, None] // HEAD_DIM_A == lane[None, :] // HEAD_DIM_A).astype(BF16)
    rep = LANES // HEAD_DIM_A
    return (g_in.reshape(1, d), wa.astype(BF16), w_wi.T.astype(BF16), w_va.T.astype(BF16), bd,
            jnp.tile(g_qa, rep).reshape(1, LANES), jnp.tile(g_ka, rep).reshape(1, LANES))


def _split_w_in(w_in, d):
    sizes = (WA_Q, WA_KV, WA_KV, WI_Q, D_IDX, N_HEADS_IDX, WIDTH_B, WIDTH_B, WM_Q, N_BRANCH * d)
    parts, o = [], 0
    for s in sizes:
        parts.append(w_in[:, o:o + s])
        o += s
    return parts


def _mix_kernel(*refs, chunked, shared_mem, emit_vb, n_experts):
    it = iter(refs)
    x_ref, oa_ref, gin_ref, wb_ref, gvb_ref, gqm_ref = (next(it) for _ in range(6))
    if chunked:
        wtril_ref, bt_ref = next(it), next(it)
    else:
        wdiag_ref, bdiag_ref = next(it), next(it)
    mk_ref, mv_ref = next(it), next(it)
    wbra_ref, wbrb_ref, wbrm_ref, wout_ref, gffn_ref, wrh_ref, wrl_ref, br_ref = (next(it) for _ in range(8))
    x1_ref, h2_ref, te_ref, tw_ref = (next(it) for _ in range(4))
    vb_ref = next(it) if emit_vb else None

    x = x_ref[...]
    tm, d = x.shape
    h = _rms(x, gin_ref[...]).astype(BF16)
    p = _dot(h, wb_ref[...])
    ub = p[:, :WIDTH_B]
    vb = _rms(p[:, WIDTH_B:2 * WIDTH_B], gvb_ref[...])
    if emit_vb:
        vb_ref[...] = vb
    o = 2 * WIDTH_B
    qm = p[:, o:o + WM_Q]
    o += WM_Q
    gates = p[:, o:o + N_BRANCH * d]

    if chunked:
        vbb = vb.astype(BF16)
        rows = []
        for cc in range(tm // CHUNK):
            cols = []
            for g in range(N_GROUPS_B):
                vg = vbb[cc * CHUNK:(cc + 1) * CHUNK, g * GROUP_DIM_B:(g + 1) * GROUP_DIM_B]
                cols.append(_dot(wtril_ref[g], vg))
            rows.append(jnp.concatenate(cols, axis=1) + bt_ref[...])
        z = jnp.concatenate(rows, axis=0) if len(rows) > 1 else rows[0]
    else:
        z = vb * wdiag_ref[...] + bdiag_ref[...]
    out_b = ub * z

    scale_m = HEAD_DIM_M ** -0.5
    outs = []
    for hh in range(N_HEADS_M):
        sl = slice(hh * HEAD_DIM_M, (hh + 1) * HEAD_DIM_M)
        qh = _rms(qm[:, sl], gqm_ref[...])
        if shared_mem:
            lg = _dot_nt(qh.astype(BF16), mk_ref[:, sl].astype(BF16)) * scale_m
            pm = jnp.exp(lg - lg.max(axis=-1, keepdims=True))
            pm = pm / pm.sum(axis=-1, keepdims=True)
            outs.append(_dot(pm.astype(BF16), mv_ref[:, sl].astype(BF16)))
        else:
            per_row = []
            for r in range(tm):
                kr = mk_ref[r, :, sl]
                lg = (kr * qh[r:r + 1, :]).sum(axis=-1, keepdims=True) * scale_m
                pm = jnp.exp(lg - lg.max(axis=0, keepdims=True))
                pm = pm / pm.sum(axis=0, keepdims=True)
                per_row.append((pm * mv_ref[r, :, sl]).sum(axis=0, keepdims=True))
            outs.append(jnp.concatenate(per_row, axis=0))
    out_m = jnp.concatenate(outs, axis=1)

    sig = jax.nn.sigmoid
    merged = (sig(gates[:, :d]) * _dot(oa_ref[...].astype(BF16), wbra_ref[...])
              + sig(gates[:, d:2 * d]) * _dot(out_b.astype(BF16), wbrb_ref[...])
              + sig(gates[:, 2 * d:]) * _dot(out_m.astype(BF16), wbrm_ref[...]))
    x1 = x + _dot(merged.astype(BF16), wout_ref[...])
    x1_ref[...] = x1
    h2 = _rms(x1, gffn_ref[...])
    h2_ref[...] = h2.astype(h2_ref.dtype)

    hi = h2.astype(BF16)
    lo = (h2 - hi.astype(F32)).astype(BF16)
    lg = _dot(hi, wrh_ref[...]) + _dot(hi, wrl_ref[...]) + _dot(lo, wrh_ref[...]) + br_ref[...]
    lane = lax.broadcasted_iota(I32, lg.shape, 1)
    lg = jnp.where(lane < n_experts, lg, -jnp.inf)
    vals, idxs = [], []
    for _ in range(TOP_K_MOE):
        m = lg.max(axis=-1, keepdims=True)
        idx = jnp.where(lg == m, lane, LANES).min(axis=-1, keepdims=True)
        vals.append(m)
        idxs.append(idx)
        lg = jnp.where(lane == idx, -jnp.inf, lg)
    ex = [jnp.exp(v - vals[0]) for v in vals]
    den = ex[0]
    for e in ex[1:]:
        den = den + e
    te = jnp.full(lane.shape, -1, I32)
    tw = jnp.zeros(lane.shape, F32)
    for j in range(TOP_K_MOE):
        te = jnp.where(lane == j, idxs[j], te)
        tw = jnp.where(lane == j, ex[j] / den, tw)
    te_ref[...] = te
    tw_ref[...] = tw


def _mix(x2d, out_a, mix_params, gmlp_params, mem_k, mem_v, *, tm, chunked, shared_mem, emit_vb, n_experts):
    n, d = x2d.shape
    gin, wb, gvb, gqm, wbra, wbrb, wbrm, wout, gffn, wrh, wrl, br = mix_params
    row = lambda w: pl.BlockSpec((tm, w), lambda i: (i, 0))
    if shared_mem:
        mem_specs = [_const_spec(mem_k.shape), _const_spec(mem_v.shape)]
    else:
        mspec = pl.BlockSpec((tm,) + mem_k.shape[1:], lambda i: (i, 0, 0))
        mem_specs = [mspec, mspec]
    consts = lambda arrs: [_const_spec(a.shape) for a in arrs]
    in_specs = ([row(d), row(WA_Q)] + consts([gin, wb, gvb, gqm]) + consts(gmlp_params) + mem_specs
                + consts([wbra, wbrb, wbrm, wout, gffn, wrh, wrl, br]))
    out_shape = [jax.ShapeDtypeStruct((n, d), F32), jax.ShapeDtypeStruct((n, d), F32),
                 jax.ShapeDtypeStruct((n, LANES), I32), jax.ShapeDtypeStruct((n, LANES), F32)]
    out_specs = [row(d), row(d), row(LANES), row(LANES)]
    if emit_vb:
        out_shape.append(jax.ShapeDtypeStruct((n, WIDTH_B), F32))
        out_specs.append(row(WIDTH_B))
    kern = functools.partial(_mix_kernel, chunked=chunked, shared_mem=shared_mem, emit_vb=emit_vb,
                             n_experts=n_experts)
    return pl.pallas_call(kern, out_shape=tuple(out_shape), grid=(n // tm,), in_specs=in_specs,
                          out_specs=tuple(out_specs), compiler_params=_cparams(), name="mix")(
                              x2d, out_a, gin, wb, gvb, gqm, *gmlp_params, mem_k, mem_v,
                              wbra, wbrb, wbrm, wout, gffn, wrh, wrl, br)


def _prep_mix_params(w_in, g_in, g_vb, g_qm, w_br_a, w_br_b, w_br_m, w_out, g_ffn, w_router, b_router):
    d = w_in.shape[0]
    parts = _split_w_in(w_in, d)
    wb = jnp.concatenate(parts[6:10], axis=1).astype(BF16)
    n_e = w_router.shape[1]
    wr = jnp.pad(w_router, ((0, 0), (0, LANES - n_e)))
    wrh = wr.astype(BF16)
    wrl = (wr - wrh.astype(F32)).astype(BF16)
    br = jnp.pad(b_router, (0, LANES - n_e)).reshape(1, LANES)
    return (g_in.reshape(1, d), wb, g_vb.reshape(1, WIDTH_B), g_qm.reshape(1, HEAD_DIM_M),
            w_br_a.astype(BF16), w_br_b.astype(BF16), w_br_m.astype(BF16), w_out.astype(BF16),
            g_ffn.reshape(1, d), wrh, wrl, br)


def _prep_gmlp_chunked(w_s, b_s):
    tril = jnp.tril(jnp.ones((CHUNK, CHUNK), w_s.dtype))
    wtril = (w_s * tril).astype(BF16)
    bt = jnp.repeat(b_s.T, GROUP_DIM_B, axis=1)
    return wtril, bt


def _prep_gmlp_single(w_s, b_s):
    wdiag = jnp.repeat(w_s[:, 0, 0], GROUP_DIM_B).reshape(1, WIDTH_B)
    bdiag = jnp.repeat(b_s[:, 0], GROUP_DIM_B).reshape(1, WIDTH_B)
    return wdiag, bdiag


MOE_TP = 256


def _plan_kernel(te_ref, tri_ref, upper_ref, lslot_ref, cnt_ref, tstart_ref, gprev_ref, total_ref, carry_ref):
    @pl.when(pl.program_id(0) == 0)
    def _():
        carry_ref[...] = jnp.zeros(carry_ref.shape, F32)

    te = te_ref[...]
    lane = lax.broadcasted_iota(I32, te.shape, 1)
    picks = [te[:, j:j + 1] for j in range(TOP_K_MOE)]
    onehot = jnp.zeros(te.shape, F32)
    for e in picks:
        onehot = onehot + jnp.where(lane == e, 1.0, 0.0)
    before = _dot(tri_ref[...], onehot.astype(BF16))
    cnt = onehot.sum(axis=0, keepdims=True)
    cnt = jnp.floor((cnt + (SUBLANES - 1)) * (1.0 / SUBLANES)) * SUBLANES
    cnt8 = jnp.broadcast_to(cnt, (SUBLANES, LANES)).astype(BF16)
    tstart = _dot(cnt8, upper_ref[...])[0:1, :]
    lslot = jnp.zeros(te.shape, I32)
    for j, e in enumerate(picks):
        r = jnp.where(lane == e, before + tstart, 0.0).sum(axis=-1, keepdims=True).astype(I32)
        lslot = jnp.where(lane == j, r, lslot)
    lslot_ref[...] = lslot
    cnt_ref[0] = cnt.astype(I32)
    tstart_ref[0] = tstart.astype(I32)
    gprev_ref[0] = carry_ref[...].astype(I32)
    carry_ref[...] = carry_ref[...] + cnt
    total_ref[...] = carry_ref[...].astype(I32)


def _moe_plan(te_all):
    n = te_all.shape[0]
    tp = MOE_TP
    n_tiles = n // tp
    r = jnp.arange(tp)
    tri = (r[:, None] > r[None, :]).astype(BF16)
    e = jnp.arange(LANES)
    upper = (e[:, None] < e[None, :]).astype(BF16)
    per_tile = jax.ShapeDtypeStruct((n_tiles, 1, LANES), I32)
    tile_spec = pl.BlockSpec((1, 1, LANES), lambda i: (i, 0, 0))
    return pl.pallas_call(
        _plan_kernel,
        out_shape=(jax.ShapeDtypeStruct((n, LANES), I32), per_tile, per_tile, per_tile,
                   jax.ShapeDtypeStruct((1, LANES), I32)),
        grid=(n_tiles,),
        in_specs=[pl.BlockSpec((tp, LANES), lambda i: (i, 0)), _const_spec((tp, tp)), _const_spec((LANES, LANES))],
        out_specs=(pl.BlockSpec((tp, LANES), lambda i: (i, 0)), tile_spec, tile_spec, tile_spec,
                   _const_spec((1, LANES))),
        scratch_shapes=[pltpu.VMEM((1, LANES), F32)],
        compiler_params=_cparams(), name="moe_plan")(te_all, tri, upper)


def _run_pieces(tp):
    sizes, s = [], SUBLANES
    while s <= tp:
        sizes.append(s)
        s *= 2
    return tuple(reversed(sizes))


def _local_rows(tp, n_e):
    return tp * TOP_K_MOE + n_e * SUBLANES


def _for_runs(tabs, tile, n_e, tp, fn):
    cnt_s, tstart_s, gbase_s = tabs

    def per_expert(e, c):
        idx = tile * n_e + e
        length, src0, dst0 = cnt_s[idx], tstart_s[idx], gbase_s[idx]
        off = jnp.int32(0)
        for size in _run_pieces(tp):
            piece = length & size

            @pl.when(piece != 0)
            def _():
                fn(pl.multiple_of(src0 + off, SUBLANES), pl.multiple_of(dst0 + off, SUBLANES), size)
            off = off + piece
        return c

    lax.fori_loop(0, n_e, per_expert, 0)


def _slot_matrix(lslot, vals, n_rows):
    col = lax.broadcasted_iota(I32, (lslot.shape[0], n_rows), 1)
    m = jnp.zeros(col.shape, F32)
    for j in range(TOP_K_MOE):
        v = 1.0 if vals is None else vals[:, j:j + 1]
        m = m + jnp.where(lslot[:, j:j + 1] == col, v, 0.0)
    return m


def _dispatch_kernel(cnt_s, tstart_s, gbase_s, lslot_ref, h2_ref, xs_in, xs_out, stage, sem, *, tp, n_e, tile0):
    del xs_in
    n_rows = _local_rows(tp, n_e)
    i = pl.program_id(0)
    slot = i % 2
    pt = _slot_matrix(lslot_ref[...], None, n_rows).astype(BF16)
    stage[slot] = lax.dot_general(pt, h2_ref[...].astype(BF16), (((0,), (0,)), ((), ())),
                                  preferred_element_type=F32)

    def runs(tile, s, op):
        _for_runs((cnt_s, tstart_s, gbase_s), tile0 + tile, n_e, tp,
                  lambda src, dst, size: op(pltpu.make_async_copy(
                      stage.at[s, pl.ds(src, size), :], xs_out.at[pl.ds(dst, size), :], sem.at[s])))

    @pl.when(i > 0)
    def _():
        runs(i - 1, 1 - slot, lambda cp: cp.wait())

    runs(i, slot, lambda cp: cp.start())

    @pl.when(i == pl.num_programs(0) - 1)
    def _():
        runs(i, slot, lambda cp: cp.wait())


def _dispatch(tabs, tile0, lslot, h2, xs, n_e):
    n, d = h2.shape
    tp = min(MOE_TP, n)
    n_rows = _local_rows(tp, n_e)
    kern = functools.partial(_dispatch_kernel, tp=tp, n_e=n_e, tile0=tile0)
    any_spec = pl.BlockSpec(memory_space=pl.ANY)
    grid_spec = pltpu.PrefetchScalarGridSpec(
        num_scalar_prefetch=3, grid=(n // tp,),
        in_specs=[pl.BlockSpec((tp, LANES), lambda i, *_: (i, 0)), pl.BlockSpec((tp, d), lambda i, *_: (i, 0)), any_spec],
        out_specs=any_spec,
        scratch_shapes=[pltpu.VMEM((2, n_rows, d), F32), pltpu.SemaphoreType.DMA((2,))])
    return pl.pallas_call(
        kern, out_shape=jax.ShapeDtypeStruct(xs.shape, xs.dtype), grid_spec=grid_spec,
        input_output_aliases={5: 0},
        compiler_params=pltpu.CompilerParams(dimension_semantics=("arbitrary",), has_side_effects=True,
                                             vmem_limit_bytes=VMEM_LIMIT),
        name="dispatch")(*tabs, lslot, h2, xs)


def _expert_kernel(be_ref, nu_ref, xs_ref, wg_ref, wu_ref, wd_ref, bg_ref, bu_ref, bd_ref, y_ref,
                   wg_bf, wu_bf, wd_bf):
    b = pl.program_id(0)

    @pl.when(b >= nu_ref[0])
    def _():
        y_ref[...] = jnp.zeros(y_ref.shape, F32)

    @pl.when(b < nu_ref[0])
    def _():
        prev = be_ref[jnp.maximum(b - 1, 0)]

        @pl.when(jnp.logical_or(b == 0, be_ref[b] != prev))
        def _():
            wg_bf[...] = wg_ref[0].astype(BF16)
            wu_bf[...] = wu_ref[0].astype(BF16)
            wd_bf[...] = wd_ref[0].astype(BF16)

        xb = xs_ref[...].astype(BF16)
        ff = wg_bf.shape[1]
        y = jnp.zeros(y_ref.shape, F32) + bd_ref[0]
        for n in range(ff // FF_CHUNK):
            sl = slice(n * FF_CHUNK, (n + 1) * FF_CHUNK)
            hg = jnp.minimum(_dot(xb, wg_bf[:, sl]) + bg_ref[0][:, sl], SWIGLU_LIMIT)
            hu = jnp.clip(_dot(xb, wu_bf[:, sl]) + bu_ref[0][:, sl], -SWIGLU_LIMIT, SWIGLU_LIMIT)
            act = hg * jax.nn.sigmoid(SWIGLU_ALPHA * hg) * (hu + 1.0)
            y = y + _dot(act.astype(BF16), wd_bf[sl, :])
        y_ref[...] = y


def _experts(xs, block_e, n_used, w_gate, b_gate, w_up, b_up, w_down, b_down):
    ns, d = xs.shape
    n_e, _, ff = w_gate.shape
    bm = MOE_BM
    blk = lambda b, be, nu: (jnp.minimum(b, nu[0] - 1), 0)
    wsel = lambda b, be, nu: (be[b], 0, 0)
    grid_spec = pltpu.PrefetchScalarGridSpec(
        num_scalar_prefetch=2, grid=(ns // bm,),
        in_specs=[pl.BlockSpec((bm, d), blk),
                  pl.BlockSpec((1, d, ff), wsel), pl.BlockSpec((1, d, ff), wsel), pl.BlockSpec((1, ff, d), wsel),
                  pl.BlockSpec((1, 1, ff), wsel), pl.BlockSpec((1, 1, ff), wsel), pl.BlockSpec((1, 1, d), wsel)],
        out_specs=pl.BlockSpec((bm, d), lambda b, be, nu: (b, 0)),
        scratch_shapes=[pltpu.VMEM((d, ff), BF16), pltpu.VMEM((d, ff), BF16), pltpu.VMEM((ff, d), BF16)])
    return pl.pallas_call(_expert_kernel, out_shape=jax.ShapeDtypeStruct((ns, d), F32), grid_spec=grid_spec,
                          compiler_params=_cparams(), name="experts")(
                              block_e, n_used, xs, w_gate, w_up, w_down,
                              b_gate.reshape(n_e, 1, ff), b_up.reshape(n_e, 1, ff), b_down.reshape(n_e, 1, d))


def _combine_kernel(cnt_s, tstart_s, gbase_s, lslot_ref, tw_ref, x1_ref, yb_hbm, y_ref, buf, sem, *, tp, n_e, tile0):
    n_rows = _local_rows(tp, n_e)
    i = pl.program_id(0)
    slot = i % 2

    def runs(tile, s, op):
        _for_runs((cnt_s, tstart_s, gbase_s), tile0 + tile, n_e, tp,
                  lambda loc, glob, size: op(pltpu.make_async_copy(
                      yb_hbm.at[pl.ds(glob, size), :], buf.at[s, pl.ds(loc, size), :], sem.at[s])))

    @pl.when(i == 0)
    def _():
        buf[...] = jnp.zeros(buf.shape, F32)
        runs(i, slot, lambda cp: cp.start())

    @pl.when(i + 1 < pl.num_programs(0))
    def _():
        runs(i + 1, 1 - slot, lambda cp: cp.start())

    runs(i, slot, lambda cp: cp.wait())

    ptw = _slot_matrix(lslot_ref[...], tw_ref[...], n_rows)
    rows = buf[slot]
    w_hi = ptw.astype(BF16)
    w_lo = (ptw - w_hi.astype(F32)).astype(BF16)
    r_hi = rows.astype(BF16)
    r_lo = (rows - r_hi.astype(F32)).astype(BF16)
    y_ref[...] = x1_ref[...] + _dot(w_hi, r_hi) + (_dot(w_hi, r_lo) + _dot(w_lo, r_hi))


def _combine(tabs, tile0, lslot, tw, x1, yb, n_e):
    n, d = x1.shape
    tp = min(MOE_TP, n)
    n_rows = _local_rows(tp, n_e)
    kern = functools.partial(_combine_kernel, tp=tp, n_e=n_e, tile0=tile0)
    row = lambda w: pl.BlockSpec((tp, w), lambda i, *_: (i, 0))
    grid_spec = pltpu.PrefetchScalarGridSpec(
        num_scalar_prefetch=3, grid=(n // tp,),
        in_specs=[row(LANES), row(LANES), row(d), pl.BlockSpec(memory_space=pl.ANY)],
        out_specs=row(d),
        scratch_shapes=[pltpu.VMEM((2, n_rows, d), F32), pltpu.SemaphoreType.DMA((2,))])
    return pl.pallas_call(kern, out_shape=jax.ShapeDtypeStruct(x1.shape, F32), grid_spec=grid_spec,
                          compiler_params=_cparams(), name="combine")(*tabs, lslot, tw, x1, yb)


def _moe(groups, w_gate, b_gate, w_up, b_up, w_down, b_down):
    n_e = w_gate.shape[0]
    d = groups[0][0].shape[1]
    tp, bm = MOE_TP, MOE_BM
    sizes = [g[0].shape[0] for g in groups]
    assert all(n % tp == 0 or n < tp for n in sizes), sizes
    parts, tile0s, o = [], [], 0
    for g, n in zip(groups, sizes):
        n_pad = -(-n // tp) * tp
        parts.append(jnp.pad(g[2], ((0, n_pad - n), (0, 0)), constant_values=-1))
        tile0s.append(o // tp)
        o += n_pad
    lslot, cnt3, tstart3, gprev3, total = _moe_plan(jnp.concatenate(parts, axis=0))
    counts = total[0, :n_e]
    padded_cnt = (counts + bm - 1) // bm * bm
    pad_end = jnp.cumsum(padded_cnt)
    pad_start = pad_end - padded_cnt
    n_assign = sum(sizes) * TOP_K_MOE
    n_slack = (o // tp) * n_e * (SUBLANES - 1)
    nb = -(-(n_assign + n_slack) // bm) + n_e
    n_used = (pad_end[-1] // bm).astype(I32)
    blocks = jnp.arange(nb, dtype=I32)
    block_e = jnp.minimum((pad_end[None, :] <= (blocks * bm)[:, None]).sum(axis=1), n_e - 1).astype(I32)
    block_e = jnp.where(blocks < n_used, block_e, block_e[jnp.maximum(n_used - 1, 0)])
    flat = lambda a: a[:, 0, :n_e].reshape(-1).astype(I32)
    gbase3 = pad_start[None, None, :] + gprev3[:, :, :n_e]
    tabs = (flat(cnt3), flat(tstart3), flat(gbase3))

    xs = jnp.zeros((nb * bm, d), F32)
    lslots, o = [], 0
    for (x1, h2, te, tw), n, t0 in zip(groups, sizes, tile0s):
        ls = lslot[t0 * tp:t0 * tp + n]
        lslots.append(ls)
        xs = _dispatch(tabs, t0, ls, h2, xs, n_e)
    yb = _experts(xs, block_e, n_used.reshape(1), w_gate, b_gate, w_up, b_up, w_down, b_down)
    return [_combine(tabs, t0, ls, tw, x1, yb, n_e)
            for (x1, h2, te, tw), ls, t0 in zip(groups, lslots, tile0s)]


def _row_tile(n, want):
    return want if n % want == 0 else n


def kernel(x_prompt, x_sample, mem_prompt, cache_k, cache_v, cache_idx_k, cache_mem_k, cache_mem_v, page_table,
           g_in, w_in, g_qa, g_ka, g_vb, w_s, b_s, g_qm, g_mem, w_mem_kv, g_km, w_br_a, w_br_b, w_br_m, w_out,
           g_ffn, w_router, b_router, w_gate, b_gate, w_up, b_up, w_down, b_down):
    bp, t, d = x_prompt.shape
    bs, ts, _ = x_sample.shape
    assert bp == 1 and ts == 1 and t % CHUNK == 0
    n_mem = mem_prompt.shape[1]
    n_pages = page_table.shape[1]
    ps = cache_k.shape[1]
    past = n_pages * ps
    n_e = w_router.shape[1]

    fp = _prep_params(w_in, g_in, g_qa, g_ka)
    mp = _prep_mix_params(w_in, g_in, g_vb, g_qm, w_br_a, w_br_b, w_br_m, w_out, g_ffn, w_router, b_router)

    xp = x_prompt.reshape(t, d)
    qa_p, k32_p, khm_p, v32_p, vt_p, qi_p, ki32_p, kibf_p, wit_p = _front(
        xp, _rope_tables(jnp.arange(t, dtype=I32)), *fp, tm=_row_tile(t, 512))
    mem_k, mem_v = _memory_kv(mem_prompt.reshape(n_mem, d), g_mem.reshape(1, d), w_mem_kv.astype(BF16),
                              g_km.reshape(1, HEAD_DIM_M))
    out_a_p = _prompt_attention(qi_p, qa_p, wit_p, kibf_p, khm_p, vt_p)
    x1_p, h2_p, te_p, tw_p = _mix(xp, out_a_p, mp, _prep_gmlp_chunked(w_s, b_s), mem_k, mem_v,
                                  tm=_row_tile(t, 256), chunked=True, shared_mem=True, emit_vb=False, n_experts=n_e)

    xs = x_sample.reshape(bs, d)
    qa_s, k32_s, _, v32_s, _, qi_s, ki32_s, _, wit_s = _front(
        xs, _rope_tables(jnp.full((bs,), past, I32)), *fp, tm=bs)
    out_a_s = _sample_attention(page_table, jnp.moveaxis(qi_s, 0, 1), jnp.moveaxis(qa_s, 0, 1),
                                wit_s.T.reshape(bs, N_HEADS_IDX, 1), ki32_s.reshape(bs, 1, D_IDX),
                                k32_s.reshape(bs, 1, WA_KV), v32_s.reshape(bs, 1, WA_KV),
                                cache_idx_k, cache_k, cache_v)
    x1_s, h2_s, te_s, tw_s, vb_s = _mix(xs, out_a_s.reshape(bs, WA_Q), mp, _prep_gmlp_single(w_s, b_s),
                                        cache_mem_k.reshape(bs, n_mem, WM_Q), cache_mem_v.reshape(bs, n_mem, WM_Q),
                                        tm=_row_tile(bs, SUBLANES), chunked=False, shared_mem=False, emit_vb=True,
                                        n_experts=n_e)

    y_p, y_s = _moe([(x1_p, h2_p, te_p, tw_p), (x1_s, h2_s, te_s, tw_s)],
                    w_gate, b_gate, w_up, b_up, w_down, b_down)

    n_pg = t // ps
    return (y_p.reshape(1, t, d), y_s.reshape(bs, 1, d),
            k32_p.reshape(1, n_pg, ps, N_KV_A, HEAD_DIM_A), v32_p.reshape(1, n_pg, ps, N_KV_A, HEAD_DIM_A),
            ki32_p.reshape(1, n_pg, ps, D_IDX),
            mem_k.reshape(1, n_mem, N_HEADS_M, HEAD_DIM_M), mem_v.reshape(1, n_mem, N_HEADS_M, HEAD_DIM_M),
            k32_s.reshape(bs, 1, N_KV_A, HEAD_DIM_A), v32_s.reshape(bs, 1, N_KV_A, HEAD_DIM_A),
            ki32_s.reshape(bs, 1, D_IDX), vb_s.reshape(bs, 1, WIDTH_B))
```

```python
import functools

import jax
import jax.numpy as jnp
from jax import lax
from jax.experimental import pallas as pl
from jax.experimental.pallas import tpu as pltpu

F32 = jnp.float32
BF16 = jnp.bfloat16
I32 = jnp.int32

N_HEADS_A = 8
N_KV_A = 2
HEAD_DIM_A = 64
TOPK_MAX = 256
N_HEADS_IDX = 8
D_IDX = 64
N_GROUPS_B = 4
GROUP_DIM_B = 128
WIDTH_B = N_GROUPS_B * GROUP_DIM_B
CHUNK = 128
N_HEADS_M = 4
HEAD_DIM_M = 128
ROPE_THETA = 500000.0
ROT_FRACTION = 4
N_BRANCH = 3
TOP_K_MOE = 4
SWIGLU_LIMIT = 7.0
SWIGLU_ALPHA = 1.702
EPS = 1e-6

WA_Q = N_HEADS_A * HEAD_DIM_A
WA_KV = N_KV_A * HEAD_DIM_A
WI_Q = N_HEADS_IDX * D_IDX
WM_Q = N_HEADS_M * HEAD_DIM_M
GQA = N_HEADS_A // N_KV_A

LANES = 128
SUBLANES = 8
VMEM_LIMIT = 56 * 1024 * 1024

LOG2_E = 1.4426950408889634
INT_MIN = -(2 ** 31)
INT_MAX = 2 ** 31 - 1
NEG = -1e30

VT_ROWS = HEAD_DIM_A + 16
ATTN_BQ = 128
ATTN_SC = 512
MOE_BM = 512
FF_CHUNK = 256


def _cparams(n_axes=1, vmem=VMEM_LIMIT):
    return pltpu.CompilerParams(dimension_semantics=("arbitrary",) * n_axes, vmem_limit_bytes=vmem)


def _const_spec(shape):
    zeros = (0,) * len(shape)
    return pl.BlockSpec(shape, lambda *_: zeros)


def _dot(a, b):
    return jnp.dot(a, b, preferred_element_type=F32)


def _dot_nt(a, b):
    return lax.dot_general(a, b, (((1,), (1,)), ((), ())), preferred_element_type=F32)


def _rms(x, g):
    return x * lax.rsqrt(jnp.mean(x * x, axis=-1, keepdims=True) + EPS) * g


def _split_dot(x, w):
    hi = x.astype(BF16)
    lo = (x - hi.astype(F32)).astype(BF16)
    return _dot(hi, w) + _dot(lo, w)


def _front_kernel(x_ref, gin_ref, wa_ref, wwit_ref, wvt_ref, bd_ref, gqa_ref, gka_ref, cos_ref, sa_ref, sb_ref,
                  qa_ref, k32_ref, khm_ref, v32_ref, vt_ref, qi_ref, ki32_ref, kibf_ref, wit_ref, *, score_scale):
    x = x_ref[...]
    h = _rms(x, gin_ref[...]).astype(BF16)
    p = _dot(h, wa_ref[...])
    cos, sa, sb = cos_ref[...], sa_ref[...], sb_ref[...]
    bd = bd_ref[...]

    def head_norm(v, g):
        ssq = _split_dot(v * v, bd)
        return v * lax.rsqrt(ssq * (1.0 / HEAD_DIM_A) + EPS) * g

    def rope(v):
        return v * cos + pltpu.roll(v, LANES - 8, 1) * sa + pltpu.roll(v, 8, 1) * sb

    for s in range(WA_Q // LANES):
        v = p[:, s * LANES:(s + 1) * LANES]
        v = rope(head_norm(v, gqa_ref[...])) * (HEAD_DIM_A ** -0.5 * LOG2_E)
        vb = v.astype(BF16)
        qa_ref[2 * s] = vb[:, :HEAD_DIM_A]
        qa_ref[2 * s + 1] = vb[:, HEAD_DIM_A:]
    o = WA_Q
    k = rope(head_norm(p[:, o:o + WA_KV], gka_ref[...]))
    k32_ref[...] = k
    kb = k.astype(BF16)
    khm_ref[0] = kb[:, :HEAD_DIM_A]
    khm_ref[1] = kb[:, HEAD_DIM_A:]
    o += WA_KV
    v32_ref[...] = p[:, o:o + WA_KV]
    o += WA_KV
    for s in range(WI_Q // LANES):
        vb = rope(p[:, o + s * LANES:o + (s + 1) * LANES]).astype(BF16)
        qi_ref[2 * s] = vb[:, :D_IDX]
        qi_ref[2 * s + 1] = vb[:, D_IDX:]
    o += WI_Q
    ki = rope(p[:, o:o + LANES])[:, :D_IDX]
    ki32_ref[...] = ki
    kibf_ref[...] = ki.astype(BF16)
    wit_ref[...] = _dot_nt(wwit_ref[...], h) * score_scale
    vt = _dot_nt(wvt_ref[...], h)
    tail = jnp.where(lax.broadcasted_iota(I32, (VT_ROWS - HEAD_DIM_A, vt.shape[1]), 0) == 0, 1.0, 0.0)
    for g in range(N_KV_A):
        vt_ref[g] = jnp.concatenate([vt[g * HEAD_DIM_A:(g + 1) * HEAD_DIM_A], tail], axis=0).astype(BF16)


def _front(x2d, tabs, gin, wa, wwit, wvt, bd, gqa, gka, tm):
    n, d = x2d.shape
    cos, sa, sb = tabs
    row = lambda w: pl.BlockSpec((tm, w), lambda i: (i, 0))
    hm = lambda nh, w: pl.BlockSpec((nh, tm, w), lambda i: (0, i, 0))
    out_shape = (
        jax.ShapeDtypeStruct((N_HEADS_A, n, HEAD_DIM_A), BF16),
        jax.ShapeDtypeStruct((n, WA_KV), F32),
        jax.ShapeDtypeStruct((N_KV_A, n, HEAD_DIM_A), BF16),
        jax.ShapeDtypeStruct((n, WA_KV), F32),
        jax.ShapeDtypeStruct((N_KV_A, VT_ROWS, n), BF16),
        jax.ShapeDtypeStruct((N_HEADS_IDX, n, D_IDX), BF16),
        jax.ShapeDtypeStruct((n, D_IDX), F32),
        jax.ShapeDtypeStruct((n, D_IDX), BF16),
        jax.ShapeDtypeStruct((N_HEADS_IDX, n), F32),
    )
    out_specs = (hm(N_HEADS_A, HEAD_DIM_A), row(WA_KV), hm(N_KV_A, HEAD_DIM_A), row(WA_KV),
                 pl.BlockSpec((N_KV_A, VT_ROWS, tm), lambda i: (0, 0, i)), hm(N_HEADS_IDX, D_IDX), row(D_IDX), row(D_IDX),
                 pl.BlockSpec((N_HEADS_IDX, tm), lambda i: (0, i)))
    in_specs = [row(d), _const_spec(gin.shape), _const_spec(wa.shape), _const_spec(wwit.shape),
                _const_spec(wvt.shape), _const_spec(bd.shape), _const_spec(gqa.shape), _const_spec(gka.shape),
                row(LANES), row(LANES), row(LANES)]
    kern = functools.partial(_front_kernel, score_scale=D_IDX ** -0.5 * N_HEADS_IDX ** -0.5)
    return pl.pallas_call(kern, out_shape=out_shape, grid=(n // tm,), in_specs=in_specs, out_specs=out_specs,
                          compiler_params=_cparams(), name="front")(x2d, gin, wa, wwit, wvt, bd, gqa, gka, cos, sa, sb)


def _memkv_kernel(mem_ref, gmem_ref, w_ref, gkm_ref, k32_ref, v32_ref):
    h = _rms(mem_ref[...], gmem_ref[...]).astype(BF16)
    kv = _dot(h, w_ref[...])
    for hh in range(N_HEADS_M):
        sl = slice(hh * HEAD_DIM_M, (hh + 1) * HEAD_DIM_M)
        k32_ref[:, sl] = _rms(kv[:, sl], gkm_ref[...])
    v32_ref[...] = kv[:, WM_Q:]


def _memory_kv(mem2d, gmem, w, gkm):
    n = mem2d.shape[0]
    out = jax.ShapeDtypeStruct((n, WM_Q), F32)
    return pl.pallas_call(_memkv_kernel, out_shape=(out, out), name="memory_kv",
                          compiler_params=pltpu.CompilerParams(vmem_limit_bytes=VMEM_LIMIT))(mem2d, gmem, w, gkm)


def _sort_key(s):
    s = jnp.where(s == 0.0, 0.0, s)
    bits = pltpu.bitcast(s, I32)
    return bits ^ ((bits >> 31) & INT_MAX)


I16 = jnp.int16
I16_MIN = -(2 ** 15)
ATTN_SUB = 128


def _attn_kernel(qi_ref, qa_ref, wi_ref, ki_ref, k_ref, vt_ref, out_ref,
                 keys_ref, hi_ref, lo_ref, cut_ref, bias_ref, lg_ref, m_ref, acc_ref, *, k_sel, idx_bits):
    bq, sc = ATTN_BQ, ATTN_SC
    i = pl.program_id(0)
    n_ch = (i * bq) // sc + 1
    t_pos = i * bq + lax.broadcasted_iota(I32, (1, bq), 1)
    sub = lax.broadcasted_iota(I32, (sc, 1), 0)
    qi = qi_ref[...].reshape(N_HEADS_IDX * bq, D_IDX)
    w = wi_ref[...]

    def chunk(c):
        return pl.ds(pl.multiple_of(c * sc, sc), sc)

    def score_chunk(c, carry):
        off = pl.multiple_of(c * sc, sc)
        d = _dot_nt(ki_ref[chunk(c), :], qi)
        s = jnp.zeros((sc, bq), F32)
        for h in range(N_HEADS_IDX):
            s = s + jnp.maximum(d[:, h * bq:(h + 1) * bq], 0.0) * w[h:h + 1, :]
        key = jnp.where(off + sub <= t_pos, _sort_key(s), INT_MIN)
        keys_ref[chunk(c), :] = key
        hi_ref[chunk(c), :] = (key >> 16).astype(I16)
        lo_ref[chunk(c), :] = ((key & 0xFFFF) + I16_MIN).astype(I16)
        return carry

    lax.fori_loop(0, n_ch, score_chunk, 0)

    def count(pred):
        def body(c, acc):
            hit = jnp.where(pred(keys_ref[chunk(c), :], c * sc + sub), 1, 0)
            return acc + hit.reshape(sc // SUBLANES, SUBLANES, bq).sum(axis=0)
        acc = lax.fori_loop(0, n_ch, body, jnp.zeros((SUBLANES, bq), I32))
        return acc.sum(axis=0, keepdims=True)

    def count16(ref, cand):
        c16 = cand.astype(I16)
        rows = 2 * SUBLANES

        def body(c, acc):
            hit = jnp.where(ref[chunk(c), :] >= c16, jnp.ones((), I16), jnp.zeros((), I16))
            parts = [hit[r * rows:(r + 1) * rows, :] for r in range(sc // rows)]
            while len(parts) > 1:
                parts = [a + b for a, b in zip(parts[::2], parts[1::2])]
            return acc + parts[0]
        acc = lax.fori_loop(0, n_ch, body, jnp.zeros((rows, bq), I16))
        return acc.astype(I32).sum(axis=0, keepdims=True)

    def search16(ref, want, frozen, cnt0):
        def cond(st):
            bit, _, cnt = st
            pending = jnp.where(jnp.logical_or(cnt == want, frozen), 0, 1)
            return jnp.logical_and(bit >= 0, jnp.max(pending) > 0)

        def body(st):
            bit, thr, cnt = st
            cand = thr + (jnp.int32(1) << bit)
            c = count16(ref, cand)
            take = jnp.logical_and(c >= want, jnp.logical_not(frozen))
            return bit - 1, jnp.where(take, cand, thr), jnp.where(take, c, cnt)

        _, thr, cnt = lax.while_loop(cond, body, (jnp.int32(15), jnp.full((1, bq), I16_MIN, I32), cnt0))
        return thr, cnt

    short = t_pos + 1 <= k_sel
    total = jnp.zeros((1, bq), I32) + n_ch * sc
    thr_hi, cnt_hi = search16(hi_ref, k_sel, short, total)
    above = count16(hi_ref, jnp.minimum(thr_hi + 1, -I16_MIN - 1))
    above = jnp.where(thr_hi == -I16_MIN - 1, 0, above)
    settled = jnp.logical_or(cnt_hi == k_sel, short)
    thr_hi16 = thr_hi.astype(I16)

    def mask_low(c, carry):
        lo_ref[chunk(c), :] = jnp.where(hi_ref[chunk(c), :] == thr_hi16, lo_ref[chunk(c), :],
                                        jnp.full((), I16_MIN, I16))
        return carry

    lax.fori_loop(0, n_ch, mask_low, 0)
    thr_lo, cnt_lo = search16(lo_ref, k_sel - above, settled, cnt_hi - above)
    thr = (thr_hi << 16) + (thr_lo - I16_MIN)
    cnt = jnp.where(settled, cnt_hi, above + cnt_lo)
    cut_ref[...] = jnp.full((1, bq), INT_MAX, I32)

    excess = jnp.logical_and(cnt > k_sel, thr > INT_MIN)
    has_ties = jnp.max(jnp.where(excess, 1, 0)) > 0

    @pl.when(has_ties)
    def _():
        want = k_sel - count(lambda blk, _: blk > thr)

        def tie_body(j, cpos):
            cand = cpos + (jnp.int32(1) << (idx_bits - 1 - j))
            below = count(lambda blk, pos: jnp.where(blk == thr, pos, INT_MAX) < cand)
            return jnp.where(below < want, cand, cpos)

        cpos = lax.fori_loop(0, idx_bits, tie_body, jnp.zeros((1, bq), I32))
        cut_ref[...] = jnp.where(excess, cpos, INT_MAX)

    m_ref[...] = jnp.full(m_ref.shape, NEG, F32)
    acc_ref[...] = jnp.zeros(acc_ref.shape, F32)
    cut = cut_ref[...]
    floor = jnp.where(thr == INT_MIN, INT_MIN, thr - 1)
    n_sub = sc // ATTN_SUB

    def attn_chunk(c, carry):
        off = pl.multiple_of(c * sc, sc)
        keyc = keys_ref[chunk(c), :]

        @pl.when(has_ties)
        def _():
            tie = jnp.where(off + sub <= cut, 0.0, NEG)
            bias = jnp.where(keyc > thr, 0.0, jnp.where(keyc == thr, tie, NEG))
            bias_ref[...] = jnp.where(keyc == INT_MIN, NEG, bias)

        @pl.when(jnp.logical_not(has_ties))
        def _():
            bias_ref[...] = jnp.where(keyc > floor, 0.0, NEG)

        qs = [qa_ref[g * GQA:(g + 1) * GQA].reshape(GQA * bq, HEAD_DIM_A) for g in range(N_KV_A)]
        cmax = [jnp.full((1, GQA * bq), NEG, F32) for _ in range(N_KV_A)]
        for r in range(n_sub):
            rows = slice(r * ATTN_SUB, (r + 1) * ATTN_SUB)
            bias4 = jnp.concatenate([bias_ref[rows, :]] * GQA, axis=1)
            for g in range(N_KV_A):
                lg = _dot_nt(k_ref[g, pl.ds(off + r * ATTN_SUB, ATTN_SUB), :], qs[g]) + bias4
                lg_ref[g, rows, :] = lg
                cmax[g] = jnp.maximum(cmax[g], lg.max(axis=0, keepdims=True))
        m_new, acc = [], []
        for g in range(N_KV_A):
            m_old = m_ref[g]
            m_new.append(jnp.maximum(m_old, cmax[g]))
            acc.append(jnp.exp2(m_old - m_new[g]) * acc_ref[g])
            m_ref[g] = m_new[g]
        for r in range(n_sub):
            rows = slice(r * ATTN_SUB, (r + 1) * ATTN_SUB)
            for g in range(N_KV_A):
                p = jnp.exp2(lg_ref[g, rows, :] - m_new[g])
                acc[g] = acc[g] + _dot(vt_ref[g, :, pl.ds(off + r * ATTN_SUB, ATTN_SUB)], p.astype(BF16))
        for g in range(N_KV_A):
            acc_ref[g] = acc[g]
        return carry

    lax.fori_loop(0, n_ch, attn_chunk, 0)

    heads = []
    for g in range(N_KV_A):
        acc = acc_ref[g]
        o = acc[:HEAD_DIM_A] / acc[HEAD_DIM_A:HEAD_DIM_A + 1]
        heads += [o[:, hh * bq:(hh + 1) * bq] for hh in range(GQA)]
    out_ref[...] = jnp.concatenate(heads, axis=0).T.astype(out_ref.dtype)


def _prompt_attention(qi_hm, qa_hm, wit, kibf, khm, vt):
    t = kibf.shape[0]
    bq = ATTN_BQ
    k_sel = min(TOPK_MAX, t // 4)
    kern = functools.partial(_attn_kernel, k_sel=k_sel, idx_bits=max(1, (t - 1).bit_length()))
    in_specs = [pl.BlockSpec((N_HEADS_IDX, bq, D_IDX), lambda i: (0, i, 0)),
                pl.BlockSpec((N_HEADS_A, bq, HEAD_DIM_A), lambda i: (0, i, 0)),
                pl.BlockSpec((N_HEADS_IDX, bq), lambda i: (0, i)),
                _const_spec(kibf.shape), _const_spec(khm.shape), _const_spec(vt.shape)]
    scratch = [pltpu.VMEM((t, bq), I32), pltpu.VMEM((t, bq), I16), pltpu.VMEM((t, bq), I16),
               pltpu.VMEM((1, bq), I32), pltpu.VMEM((ATTN_SC, bq), F32),
               pltpu.VMEM((N_KV_A, ATTN_SC, GQA * bq), F32), pltpu.VMEM((N_KV_A, 1, GQA * bq), F32),
               pltpu.VMEM((N_KV_A, VT_ROWS, GQA * bq), F32)]
    return pl.pallas_call(kern, out_shape=jax.ShapeDtypeStruct((t, WA_Q), BF16), grid=(t // bq,),
                          in_specs=in_specs, out_specs=pl.BlockSpec((bq, WA_Q), lambda i: (i, 0)),
                          scratch_shapes=scratch, compiler_params=_cparams(), name="attn")(
                              qi_hm, qa_hm, wit, kibf, khm, vt)


SAMPLE_CK = 2048


def _sample_attn_kernel(pt_ref, qi_ref, qa_ref, wi_ref, kin_ref, kn_ref, vn_ref, cidx_hbm, ck_hbm, cv_hbm, out_ref,
                        idx_buf, k_buf, v_buf, scr_ref, sems, *, n_pages, ps, ck, k_sel, idx_bits):
    b = pl.program_id(0)
    nb = pl.num_programs(0)
    past = n_pages * ps
    nc = past // ck

    def page_copy(which, bb, p):
        src, dst = ((cidx_hbm, idx_buf), (ck_hbm, k_buf), (cv_hbm, v_buf))[which]
        cols = pl.ds(pl.multiple_of(p * ps, ps), ps)
        dst = dst.at[:, cols] if which == 0 else dst.at[:, :, cols]
        return pltpu.make_async_copy(src.at[pt_ref[bb, p]], dst, sems.at[which])

    def for_pages(fn):
        lax.fori_loop(0, n_pages, lambda p, c: (fn(p), c)[1], 0)

    @pl.when(b == 0)
    def _():
        for_pages(lambda p: page_copy(0, b, p).start())

    for_pages(lambda p: (page_copy(1, b, p).start(), page_copy(2, b, p).start()))
    for_pages(lambda p: page_copy(0, b, p).wait())

    qi = qi_ref[0]
    w = wi_ref[0]
    for c in range(nc):
        d = _dot(qi, idx_buf[:, c * ck:(c + 1) * ck].astype(BF16))
        scr_ref[c:c + 1, :] = (jnp.maximum(d, 0.0) * w).sum(axis=0, keepdims=True)
    d_new = (qi.astype(F32) * kin_ref[0].astype(BF16).astype(F32)).sum(axis=-1, keepdims=True)
    s_new = (jnp.maximum(d_new, 0.0) * w).sum(axis=0, keepdims=True)

    @pl.when(b + 1 < nb)
    def _():
        for_pages(lambda p: page_copy(0, b + 1, p).start())

    keys = _sort_key(scr_ref[...])
    key_new = _sort_key(s_new)
    pos = (lax.broadcasted_iota(I32, keys.shape, 0) * ck + lax.broadcasted_iota(I32, keys.shape, 1))

    def count(pred):
        hit = jnp.where(pred(keys, pos), 1, 0).sum(axis=1, keepdims=True).sum(axis=0, keepdims=True)
        return hit + jnp.where(pred(key_new, past), 1, 0)

    def search_cond(st):
        bit, _, cnt = st
        return jnp.logical_and(bit >= 0, jnp.max(jnp.where(cnt == k_sel, 0, 1)) > 0)

    def search_body(st):
        bit, thr, cnt = st
        cand = thr + (jnp.int32(1) << bit)
        c = count(lambda kk, _: kk >= cand)
        take = c >= k_sel
        return bit - 1, jnp.where(take, cand, thr), jnp.where(take, c, cnt)

    st0 = (jnp.int32(31), jnp.full((1, 1), INT_MIN, I32), jnp.full((1, 1), past + 1, I32))
    _, thr, cnt = lax.while_loop(search_cond, search_body, st0)

    want = k_sel - count(lambda kk, _: kk > thr)

    def tie_body(j, cpos):
        cand = cpos + (jnp.int32(1) << (idx_bits - 1 - j))
        below = count(lambda kk, pp: jnp.where(kk == thr, pp, INT_MAX) < cand)
        return jnp.where(below < want, cand, cpos)

    cpos = lax.fori_loop(0, idx_bits, tie_body, jnp.zeros((1, 1), I32))
    cut = jnp.where(cnt > k_sel, cpos, INT_MAX)

    def sel_bias(kk, pp):
        tie = jnp.where(pp <= cut, 0.0, NEG)
        return jnp.where(kk > thr, 0.0, jnp.where(kk == thr, tie, NEG))

    bias = sel_bias(keys, pos)
    bias_new = sel_bias(key_new, past)

    for_pages(lambda p: (page_copy(1, b, p).wait(), page_copy(2, b, p).wait()))

    qa = qa_ref[0]
    heads = []
    for g in range(N_KV_A):
        qg = qa[g * GQA:(g + 1) * GQA]
        cols = slice(g * HEAD_DIM_A, (g + 1) * HEAD_DIM_A)
        m = jnp.full((GQA, 1), NEG, F32)
        l = jnp.zeros((GQA, 1), F32)
        acc = jnp.zeros((GQA, HEAD_DIM_A), F32)
        for c in range(nc):
            span = slice(c * ck, (c + 1) * ck)
            lg = _dot(qg, k_buf[g, :, span].astype(BF16)) + bias[c:c + 1, :]
            m_new = jnp.maximum(m, lg.max(axis=-1, keepdims=True))
            p = jnp.exp2(lg - m_new)
            alpha = jnp.exp2(m - m_new)
            l = alpha * l + p.sum(axis=-1, keepdims=True)
            acc = alpha * acc + _dot_nt(p.astype(BF16), v_buf[g, :, span].astype(BF16))
            m = m_new
        kn = kn_ref[0][:, cols].astype(BF16).astype(F32)
        vn = vn_ref[0][:, cols].astype(BF16).astype(F32)
        lg = (qg.astype(F32) * kn).sum(axis=-1, keepdims=True) + bias_new
        m_new = jnp.maximum(m, lg)
        p = jnp.exp2(lg - m_new)
        alpha = jnp.exp2(m - m_new)
        l = alpha * l + p
        acc = alpha * acc + p * vn
        o = acc / l
        heads += [o[hh:hh + 1, :] for hh in range(GQA)]
    out_ref[0] = jnp.concatenate(heads, axis=1)


def _sample_attention(page_table, qi_s, qa_s, wi_s, ki_new, k_new, v_new, cache_idx_k, cache_k, cache_v):
    nbatch, n_pages = page_table.shape
    n_pool, ps, d_idx = cache_idx_k.shape
    past = n_pages * ps
    ck = min(SAMPLE_CK, past)
    assert past % ck == 0 and ck % ps == 0
    k_sel = min(TOPK_MAX, (past + 1) // 4)
    kern = functools.partial(_sample_attn_kernel, n_pages=n_pages, ps=ps, ck=ck, k_sel=k_sel,
                             idx_bits=past.bit_length())
    per_b = lambda shape: pl.BlockSpec((1,) + shape, lambda b, pt: (b, 0, 0))
    any_spec = pl.BlockSpec(memory_space=pl.ANY)
    grid_spec = pltpu.PrefetchScalarGridSpec(
        num_scalar_prefetch=1, grid=(nbatch,),
        in_specs=[per_b((N_HEADS_IDX, d_idx)), per_b((N_HEADS_A, HEAD_DIM_A)), per_b((N_HEADS_IDX, 1)),
                  per_b((1, d_idx)), per_b((1, WA_KV)), per_b((1, WA_KV)), any_spec, any_spec, any_spec],
        out_specs=per_b((1, WA_Q)),
        scratch_shapes=[pltpu.VMEM((d_idx, past), F32), pltpu.VMEM((N_KV_A, HEAD_DIM_A, past), F32),
                        pltpu.VMEM((N_KV_A, HEAD_DIM_A, past), F32),
                        pltpu.VMEM((past // ck, ck), F32), pltpu.SemaphoreType.DMA((3,))])
    return pl.pallas_call(kern, out_shape=jax.ShapeDtypeStruct((nbatch, 1, WA_Q), F32), grid_spec=grid_spec,
                          compiler_params=_cparams(), name="sample_attn")(
                              page_table, qi_s, qa_s, wi_s, ki_new, k_new, v_new,
                              jnp.transpose(cache_idx_k, (0, 2, 1)), jnp.transpose(cache_k, (0, 2, 3, 1)),
                              jnp.transpose(cache_v, (0, 2, 3, 1)))


def _rope_tables(pos):
    rot = HEAD_DIM_A // ROT_FRACTION
    half = rot // 2
    inv_freq = ROPE_THETA ** (-jnp.arange(half, dtype=F32) / half)
    ang = pos.astype(F32)[:, None] * inv_freq[None, :]
    cos, sin = jnp.cos(ang), jnp.sin(ang)
    n = pos.shape[0]
    z = lambda w: jnp.zeros((n, w), F32)
    c = jnp.concatenate([cos, cos, jnp.ones((n, HEAD_DIM_A - rot), F32)], axis=1)
    sa = jnp.concatenate([-sin, z(HEAD_DIM_A - half)], axis=1)
    sb = jnp.concatenate([z(half), sin, z(HEAD_DIM_A - rot)], axis=1)
    rep = LANES // HEAD_DIM_A
    return tuple(jnp.tile(a, (1, rep)) for a in (c, sa, sb))


def _prep_params(w_in, g_in, g_qa, g_ka):
    d = w_in.shape[0]
    w_qa, w_ka, w_va, w_qi, w_ki, w_wi = _split_w_in(w_in, d)[:6]
    wa = jnp.concatenate([w_qa, w_ka, w_va, w_qi, w_ki, jnp.zeros((d, LANES - D_IDX), w_in.dtype)], axis=1)
    lane = jnp.arange(LANES)
    bd = (lane[:, None] // HEAD_DIM_A == lane[None, :] // HEAD_DIM_A).astype(BF16)
    rep = LANES // HEAD_DIM_A
    return (g_in.reshape(1, d), wa.astype(BF16), w_wi.T.astype(BF16), w_va.T.astype(BF16), bd,
            jnp.tile(g_qa, rep).reshape(1, LANES), jnp.tile(g_ka, rep).reshape(1, LANES))


def _split_w_in(w_in, d):
    sizes = (WA_Q, WA_KV, WA_KV, WI_Q, D_IDX, N_HEADS_IDX, WIDTH_B, WIDTH_B, WM_Q, N_BRANCH * d)
    parts, o = [], 0
    for s in sizes:
        parts.append(w_in[:, o:o + s])
        o += s
    return parts


def _mix_kernel(*refs, chunked, shared_mem, emit_vb, n_experts):
    it = iter(refs)
    x_ref, oa_ref, gin_ref, wb_ref, gvb_ref, gqm_ref = (next(it) for _ in range(6))
    if chunked:
        wtril_ref, bt_ref = next(it), next(it)
    else:
        wdiag_ref, bdiag_ref = next(it), next(it)
    mk_ref, mv_ref = next(it), next(it)
    wbra_ref, wbrb_ref, wbrm_ref, wout_ref, gffn_ref, wrh_ref, wrl_ref, br_ref = (next(it) for _ in range(8))
    x1_ref, h2_ref, te_ref, tw_ref = (next(it) for _ in range(4))
    vb_ref = next(it) if emit_vb else None

    x = x_ref[...]
    tm, d = x.shape
    h = _rms(x, gin_ref[...]).astype(BF16)
    p = _dot(h, wb_ref[...])
    ub = p[:, :WIDTH_B]
    vb = _rms(p[:, WIDTH_B:2 * WIDTH_B], gvb_ref[...])
    if emit_vb:
        vb_ref[...] = vb
    o = 2 * WIDTH_B
    qm = p[:, o:o + WM_Q]
    o += WM_Q
    gates = p[:, o:o + N_BRANCH * d]

    if chunked:
        vbb = vb.astype(BF16)
        rows = []
        for cc in range(tm // CHUNK):
            cols = []
            for g in range(N_GROUPS_B):
                vg = vbb[cc * CHUNK:(cc + 1) * CHUNK, g * GROUP_DIM_B:(g + 1) * GROUP_DIM_B]
                cols.append(_dot(wtril_ref[g], vg))
            rows.append(jnp.concatenate(cols, axis=1) + bt_ref[...])
        z = jnp.concatenate(rows, axis=0) if len(rows) > 1 else rows[0]
    else:
        z = vb * wdiag_ref[...] + bdiag_ref[...]
    out_b = ub * z

    scale_m = HEAD_DIM_M ** -0.5
    outs = []
    for hh in range(N_HEADS_M):
        sl = slice(hh * HEAD_DIM_M, (hh + 1) * HEAD_DIM_M)
        qh = _rms(qm[:, sl], gqm_ref[...])
        if shared_mem:
            lg = _dot_nt(qh.astype(BF16), mk_ref[:, sl].astype(BF16)) * scale_m
            pm = jnp.exp(lg - lg.max(axis=-1, keepdims=True))
            pm = pm / pm.sum(axis=-1, keepdims=True)
            outs.append(_dot(pm.astype(BF16), mv_ref[:, sl].astype(BF16)))
        else:
            per_row = []
            for r in range(tm):
                kr = mk_ref[r, :, sl]
                lg = (kr * qh[r:r + 1, :]).sum(axis=-1, keepdims=True) * scale_m
                pm = jnp.exp(lg - lg.max(axis=0, keepdims=True))
                pm = pm / pm.sum(axis=0, keepdims=True)
                per_row.append((pm * mv_ref[r, :, sl]).sum(axis=0, keepdims=True))
            outs.append(jnp.concatenate(per_row, axis=0))
    out_m = jnp.concatenate(outs, axis=1)

    sig = jax.nn.sigmoid
    merged = (sig(gates[:, :d]) * _dot(oa_ref[...].astype(BF16), wbra_ref[...])
              + sig(gates[:, d:2 * d]) * _dot(out_b.astype(BF16), wbrb_ref[...])
              + sig(gates[:, 2 * d:]) * _dot(out_m.astype(BF16), wbrm_ref[...]))
    x1 = x + _dot(merged.astype(BF16), wout_ref[...])
    x1_ref[...] = x1
    h2 = _rms(x1, gffn_ref[...])
    h2_ref[...] = h2.astype(h2_ref.dtype)

    hi = h2.astype(BF16)
    lo = (h2 - hi.astype(F32)).astype(BF16)
    lg = _dot(hi, wrh_ref[...]) + _dot(hi, wrl_ref[...]) + _dot(lo, wrh_ref[...]) + br_ref[...]
    lane = lax.broadcasted_iota(I32, lg.shape, 1)
    lg = jnp.where(lane < n_experts, lg, -jnp.inf)
    vals, idxs = [], []
    for _ in range(TOP_K_MOE):
        m = lg.max(axis=-1, keepdims=True)
        idx = jnp.where(lg == m, lane, LANES).min(axis=-1, keepdims=True)
        vals.append(m)
        idxs.append(idx)
        lg = jnp.where(lane == idx, -jnp.inf, lg)
    ex = [jnp.exp(v - vals[0]) for v in vals]
    den = ex[0]
    for e in ex[1:]:
        den = den + e
    te = jnp.full(lane.shape, -1, I32)
    tw = jnp.zeros(lane.shape, F32)
    for j in range(TOP_K_MOE):
        te = jnp.where(lane == j, idxs[j], te)
        tw = jnp.where(lane == j, ex[j] / den, tw)
    te_ref[...] = te
    tw_ref[...] = tw


def _mix(x2d, out_a, mix_params, gmlp_params, mem_k, mem_v, *, tm, chunked, shared_mem, emit_vb, n_experts):
    n, d = x2d.shape
    gin, wb, gvb, gqm, wbra, wbrb, wbrm, wout, gffn, wrh, wrl, br = mix_params
    row = lambda w: pl.BlockSpec((tm, w), lambda i: (i, 0))
    if shared_mem:
        mem_specs = [_const_spec(mem_k.shape), _const_spec(mem_v.shape)]
    else:
        mspec = pl.BlockSpec((tm,) + mem_k.shape[1:], lambda i: (i, 0, 0))
        mem_specs = [mspec, mspec]
    consts = lambda arrs: [_const_spec(a.shape) for a in arrs]
    in_specs = ([row(d), row(WA_Q)] + consts([gin, wb, gvb, gqm]) + consts(gmlp_params) + mem_specs
                + consts([wbra, wbrb, wbrm, wout, gffn, wrh, wrl, br]))
    out_shape = [jax.ShapeDtypeStruct((n, d), F32), jax.ShapeDtypeStruct((n, d), F32),
                 jax.ShapeDtypeStruct((n, LANES), I32), jax.ShapeDtypeStruct((n, LANES), F32)]
    out_specs = [row(d), row(d), row(LANES), row(LANES)]
    if emit_vb:
        out_shape.append(jax.ShapeDtypeStruct((n, WIDTH_B), F32))
        out_specs.append(row(WIDTH_B))
    kern = functools.partial(_mix_kernel, chunked=chunked, shared_mem=shared_mem, emit_vb=emit_vb,
                             n_experts=n_experts)
    return pl.pallas_call(kern, out_shape=tuple(out_shape), grid=(n // tm,), in_specs=in_specs,
                          out_specs=tuple(out_specs), compiler_params=_cparams(), name="mix")(
                              x2d, out_a, gin, wb, gvb, gqm, *gmlp_params, mem_k, mem_v,
                              wbra, wbrb, wbrm, wout, gffn, wrh, wrl, br)


def _prep_mix_params(w_in, g_in, g_vb, g_qm, w_br_a, w_br_b, w_br_m, w_out, g_ffn, w_router, b_router):
    d = w_in.shape[0]
    parts = _split_w_in(w_in, d)
    wb = jnp.concatenate(parts[6:10], axis=1).astype(BF16)
    n_e = w_router.shape[1]
    wr = jnp.pad(w_router, ((0, 0), (0, LANES - n_e)))
    wrh = wr.astype(BF16)
    wrl = (wr - wrh.astype(F32)).astype(BF16)
    br = jnp.pad(b_router, (0, LANES - n_e)).reshape(1, LANES)
    return (g_in.reshape(1, d), wb, g_vb.reshape(1, WIDTH_B), g_qm.reshape(1, HEAD_DIM_M),
            w_br_a.astype(BF16), w_br_b.astype(BF16), w_br_m.astype(BF16), w_out.astype(BF16),
            g_ffn.reshape(1, d), wrh, wrl, br)


def _prep_gmlp_chunked(w_s, b_s):
    tril = jnp.tril(jnp.ones((CHUNK, CHUNK), w_s.dtype))
    wtril = (w_s * tril).astype(BF16)
    bt = jnp.repeat(b_s.T, GROUP_DIM_B, axis=1)
    return wtril, bt


def _prep_gmlp_single(w_s, b_s):
    wdiag = jnp.repeat(w_s[:, 0, 0], GROUP_DIM_B).reshape(1, WIDTH_B)
    bdiag = jnp.repeat(b_s[:, 0], GROUP_DIM_B).reshape(1, WIDTH_B)
    return wdiag, bdiag


MOE_TP = 256


def _plan_kernel(te_ref, tri_ref, upper_ref, lslot_ref, cnt_ref, tstart_ref, gprev_ref, total_ref, carry_ref):
    @pl.when(pl.program_id(0) == 0)
    def _():
        carry_ref[...] = jnp.zeros(carry_ref.shape, F32)

    te = te_ref[...]
    lane = lax.broadcasted_iota(I32, te.shape, 1)
    picks = [te[:, j:j + 1] for j in range(TOP_K_MOE)]
    onehot = jnp.zeros(te.shape, F32)
    for e in picks:
        onehot = onehot + jnp.where(lane == e, 1.0, 0.0)
    before = _dot(tri_ref[...], onehot.astype(BF16))
    cnt = onehot.sum(axis=0, keepdims=True)
    cnt = jnp.floor((cnt + (SUBLANES - 1)) * (1.0 / SUBLANES)) * SUBLANES
    cnt8 = jnp.broadcast_to(cnt, (SUBLANES, LANES)).astype(BF16)
    tstart = _dot(cnt8, upper_ref[...])[0:1, :]
    lslot = jnp.zeros(te.shape, I32)
    for j, e in enumerate(picks):
        r = jnp.where(lane == e, before + tstart, 0.0).sum(axis=-1, keepdims=True).astype(I32)
        lslot = jnp.where(lane == j, r, lslot)
    lslot_ref[...] = lslot
    cnt_ref[0] = cnt.astype(I32)
    tstart_ref[0] = tstart.astype(I32)
    gprev_ref[0] = carry_ref[...].astype(I32)
    carry_ref[...] = carry_ref[...] + cnt
    total_ref[...] = carry_ref[...].astype(I32)


def _moe_plan(te_all):
    n = te_all.shape[0]
    tp = MOE_TP
    n_tiles = n // tp
    r = jnp.arange(tp)
    tri = (r[:, None] > r[None, :]).astype(BF16)
    e = jnp.arange(LANES)
    upper = (e[:, None] < e[None, :]).astype(BF16)
    per_tile = jax.ShapeDtypeStruct((n_tiles, 1, LANES), I32)
    tile_spec = pl.BlockSpec((1, 1, LANES), lambda i: (i, 0, 0))
    return pl.pallas_call(
        _plan_kernel,
        out_shape=(jax.ShapeDtypeStruct((n, LANES), I32), per_tile, per_tile, per_tile,
                   jax.ShapeDtypeStruct((1, LANES), I32)),
        grid=(n_tiles,),
        in_specs=[pl.BlockSpec((tp, LANES), lambda i: (i, 0)), _const_spec((tp, tp)), _const_spec((LANES, LANES))],
        out_specs=(pl.BlockSpec((tp, LANES), lambda i: (i, 0)), tile_spec, tile_spec, tile_spec,
                   _const_spec((1, LANES))),
        scratch_shapes=[pltpu.VMEM((1, LANES), F32)],
        compiler_params=_cparams(), name="moe_plan")(te_all, tri, upper)


def _run_pieces(tp):
    sizes, s = [], SUBLANES
    while s <= tp:
        sizes.append(s)
        s *= 2
    return tuple(reversed(sizes))


def _local_rows(tp, n_e):
    return tp * TOP_K_MOE + n_e * SUBLANES


def _for_runs(tabs, tile, n_e, tp, fn):
    cnt_s, tstart_s, gbase_s = tabs

    def per_expert(e, c):
        idx = tile * n_e + e
        length, src0, dst0 = cnt_s[idx], tstart_s[idx], gbase_s[idx]
        off = jnp.int32(0)
        for size in _run_pieces(tp):
            piece = length & size

            @pl.when(piece != 0)
            def _():
                fn(pl.multiple_of(src0 + off, SUBLANES), pl.multiple_of(dst0 + off, SUBLANES), size)
            off = off + piece
        return c

    lax.fori_loop(0, n_e, per_expert, 0)


def _slot_matrix(lslot, vals, n_rows):
    col = lax.broadcasted_iota(I32, (lslot.shape[0], n_rows), 1)
    m = jnp.zeros(col.shape, F32)
    for j in range(TOP_K_MOE):
        v = 1.0 if vals is None else vals[:, j:j + 1]
        m = m + jnp.where(lslot[:, j:j + 1] == col, v, 0.0)
    return m


def _dispatch_kernel(cnt_s, tstart_s, gbase_s, lslot_ref, h2_ref, xs_in, xs_out, stage, sem, *, tp, n_e, tile0):
    del xs_in
    n_rows = _local_rows(tp, n_e)
    i = pl.program_id(0)
    slot = i % 2
    pt = _slot_matrix(lslot_ref[...], None, n_rows).astype(BF16)
    stage[slot] = lax.dot_general(pt, h2_ref[...].astype(BF16), (((0,), (0,)), ((), ())),
                                  preferred_element_type=F32)

    def runs(tile, s, op):
        _for_runs((cnt_s, tstart_s, gbase_s), tile0 + tile, n_e, tp,
                  lambda src, dst, size: op(pltpu.make_async_copy(
                      stage.at[s, pl.ds(src, size), :], xs_out.at[pl.ds(dst, size), :], sem.at[s])))

    @pl.when(i > 0)
    def _():
        runs(i - 1, 1 - slot, lambda cp: cp.wait())

    runs(i, slot, lambda cp: cp.start())

    @pl.when(i == pl.num_programs(0) - 1)
    def _():
        runs(i, slot, lambda cp: cp.wait())


def _dispatch(tabs, tile0, lslot, h2, xs, n_e):
    n, d = h2.shape
    tp = min(MOE_TP, n)
    n_rows = _local_rows(tp, n_e)
    kern = functools.partial(_dispatch_kernel, tp=tp, n_e=n_e, tile0=tile0)
    any_spec = pl.BlockSpec(memory_space=pl.ANY)
    grid_spec = pltpu.PrefetchScalarGridSpec(
        num_scalar_prefetch=3, grid=(n // tp,),
        in_specs=[pl.BlockSpec((tp, LANES), lambda i, *_: (i, 0)), pl.BlockSpec((tp, d), lambda i, *_: (i, 0)), any_spec],
        out_specs=any_spec,
        scratch_shapes=[pltpu.VMEM((2, n_rows, d), F32), pltpu.SemaphoreType.DMA((2,))])
    return pl.pallas_call(
        kern, out_shape=jax.ShapeDtypeStruct(xs.shape, xs.dtype), grid_spec=grid_spec,
        input_output_aliases={5: 0},
        compiler_params=pltpu.CompilerParams(dimension_semantics=("arbitrary",), has_side_effects=True,
                                             vmem_limit_bytes=VMEM_LIMIT),
        name="dispatch")(*tabs, lslot, h2, xs)


def _expert_kernel(be_ref, nu_ref, xs_ref, wg_ref, wu_ref, wd_ref, bg_ref, bu_ref, bd_ref, y_ref,
                   wg_bf, wu_bf, wd_bf):
    b = pl.program_id(0)

    @pl.when(b >= nu_ref[0])
    def _():
        y_ref[...] = jnp.zeros(y_ref.shape, F32)

    @pl.when(b < nu_ref[0])
    def _():
        prev = be_ref[jnp.maximum(b - 1, 0)]

        @pl.when(jnp.logical_or(b == 0, be_ref[b] != prev))
        def _():
            wg_bf[...] = wg_ref[0].astype(BF16)
            wu_bf[...] = wu_ref[0].astype(BF16)
            wd_bf[...] = wd_ref[0].astype(BF16)

        xb = xs_ref[...].astype(BF16)
        ff = wg_bf.shape[1]
        y = jnp.zeros(y_ref.shape, F32) + bd_ref[0]
        for n in range(ff // FF_CHUNK):
            sl = slice(n * FF_CHUNK, (n + 1) * FF_CHUNK)
            hg = jnp.minimum(_dot(xb, wg_bf[:, sl]) + bg_ref[0][:, sl], SWIGLU_LIMIT)
            hu = jnp.clip(_dot(xb, wu_bf[:, sl]) + bu_ref[0][:, sl], -SWIGLU_LIMIT, SWIGLU_LIMIT)
            act = hg * jax.nn.sigmoid(SWIGLU_ALPHA * hg) * (hu + 1.0)
            y = y + _dot(act.astype(BF16), wd_bf[sl, :])
        y_ref[...] = y


def _experts(xs, block_e, n_used, w_gate, b_gate, w_up, b_up, w_down, b_down):
    ns, d = xs.shape
    n_e, _, ff = w_gate.shape
    bm = MOE_BM
    blk = lambda b, be, nu: (jnp.minimum(b, nu[0] - 1), 0)
    wsel = lambda b, be, nu: (be[b], 0, 0)
    grid_spec = pltpu.PrefetchScalarGridSpec(
        num_scalar_prefetch=2, grid=(ns // bm,),
        in_specs=[pl.BlockSpec((bm, d), blk),
                  pl.BlockSpec((1, d, ff), wsel), pl.BlockSpec((1, d, ff), wsel), pl.BlockSpec((1, ff, d), wsel),
                  pl.BlockSpec((1, 1, ff), wsel), pl.BlockSpec((1, 1, ff), wsel), pl.BlockSpec((1, 1, d), wsel)],
        out_specs=pl.BlockSpec((bm, d), lambda b, be, nu: (b, 0)),
        scratch_shapes=[pltpu.VMEM((d, ff), BF16), pltpu.VMEM((d, ff), BF16), pltpu.VMEM((ff, d), BF16)])
    return pl.pallas_call(_expert_kernel, out_shape=jax.ShapeDtypeStruct((ns, d), F32), grid_spec=grid_spec,
                          compiler_params=_cparams(), name="experts")(
                              block_e, n_used, xs, w_gate, w_up, w_down,
                              b_gate.reshape(n_e, 1, ff), b_up.reshape(n_e, 1, ff), b_down.reshape(n_e, 1, d))


def _combine_kernel(cnt_s, tstart_s, gbase_s, lslot_ref, tw_ref, x1_ref, yb_hbm, y_ref, buf, sem, *, tp, n_e, tile0):
    n_rows = _local_rows(tp, n_e)
    i = pl.program_id(0)
    slot = i % 2

    def runs(tile, s, op):
        _for_runs((cnt_s, tstart_s, gbase_s), tile0 + tile, n_e, tp,
                  lambda loc, glob, size: op(pltpu.make_async_copy(
                      yb_hbm.at[pl.ds(glob, size), :], buf.at[s, pl.ds(loc, size), :], sem.at[s])))

    @pl.when(i == 0)
    def _():
        buf[...] = jnp.zeros(buf.shape, F32)
        runs(i, slot, lambda cp: cp.start())

    @pl.when(i + 1 < pl.num_programs(0))
    def _():
        runs(i + 1, 1 - slot, lambda cp: cp.start())

    runs(i, slot, lambda cp: cp.wait())

    ptw = _slot_matrix(lslot_ref[...], tw_ref[...], n_rows)
    rows = buf[slot]
    w_hi = ptw.astype(BF16)
    w_lo = (ptw - w_hi.astype(F32)).astype(BF16)
    r_hi = rows.astype(BF16)
    r_lo = (rows - r_hi.astype(F32)).astype(BF16)
    y_ref[...] = x1_ref[...] + _dot(w_hi, r_hi) + (_dot(w_hi, r_lo) + _dot(w_lo, r_hi))


def _combine(tabs, tile0, lslot, tw, x1, yb, n_e):
    n, d = x1.shape
    tp = min(MOE_TP, n)
    n_rows = _local_rows(tp, n_e)
    kern = functools.partial(_combine_kernel, tp=tp, n_e=n_e, tile0=tile0)
    row = lambda w: pl.BlockSpec((tp, w), lambda i, *_: (i, 0))
    grid_spec = pltpu.PrefetchScalarGridSpec(
        num_scalar_prefetch=3, grid=(n // tp,),
        in_specs=[row(LANES), row(LANES), row(d), pl.BlockSpec(memory_space=pl.ANY)],
        out_specs=row(d),
        scratch_shapes=[pltpu.VMEM((2, n_rows, d), F32), pltpu.SemaphoreType.DMA((2,))])
    return pl.pallas_call(kern, out_shape=jax.ShapeDtypeStruct(x1.shape, F32), grid_spec=grid_spec,
                          compiler_params=_cparams(), name="combine")(*tabs, lslot, tw, x1, yb)


def _moe(groups, w_gate, b_gate, w_up, b_up, w_down, b_down):
    n_e = w_gate.shape[0]
    d = groups[0][0].shape[1]
    tp, bm = MOE_TP, MOE_BM
    sizes = [g[0].shape[0] for g in groups]
    assert all(n % tp == 0 or n < tp for n in sizes), sizes
    parts, tile0s, o = [], [], 0
    for g, n in zip(groups, sizes):
        n_pad = -(-n // tp) * tp
        parts.append(jnp.pad(g[2], ((0, n_pad - n), (0, 0)), constant_values=-1))
        tile0s.append(o // tp)
        o += n_pad
    lslot, cnt3, tstart3, gprev3, total = _moe_plan(jnp.concatenate(parts, axis=0))
    counts = total[0, :n_e]
    padded_cnt = (counts + bm - 1) // bm * bm
    pad_end = jnp.cumsum(padded_cnt)
    pad_start = pad_end - padded_cnt
    n_assign = sum(sizes) * TOP_K_MOE
    n_slack = (o // tp) * n_e * (SUBLANES - 1)
    nb = -(-(n_assign + n_slack) // bm) + n_e
    n_used = (pad_end[-1] // bm).astype(I32)
    blocks = jnp.arange(nb, dtype=I32)
    block_e = jnp.minimum((pad_end[None, :] <= (blocks * bm)[:, None]).sum(axis=1), n_e - 1).astype(I32)
    block_e = jnp.where(blocks < n_used, block_e, block_e[jnp.maximum(n_used - 1, 0)])
    flat = lambda a: a[:, 0, :n_e].reshape(-1).astype(I32)
    gbase3 = pad_start[None, None, :] + gprev3[:, :, :n_e]
    tabs = (flat(cnt3), flat(tstart3), flat(gbase3))

    xs = jnp.zeros((nb * bm, d), F32)
    lslots, o = [], 0
    for (x1, h2, te, tw), n, t0 in zip(groups, sizes, tile0s):
        ls = lslot[t0 * tp:t0 * tp + n]
        lslots.append(ls)
        xs = _dispatch(tabs, t0, ls, h2, xs, n_e)
    yb = _experts(xs, block_e, n_used.reshape(1), w_gate, b_gate, w_up, b_up, w_down, b_down)
    return [_combine(tabs, t0, ls, tw, x1, yb, n_e)
            for (x1, h2, te, tw), ls, t0 in zip(groups, lslots, tile0s)]


def _row_tile(n, want):
    return want if n % want == 0 else n


def kernel(x_prompt, x_sample, mem_prompt, cache_k, cache_v, cache_idx_k, cache_mem_k, cache_mem_v, page_table,
           g_in, w_in, g_qa, g_ka, g_vb, w_s, b_s, g_qm, g_mem, w_mem_kv, g_km, w_br_a, w_br_b, w_br_m, w_out,
           g_ffn, w_router, b_router, w_gate, b_gate, w_up, b_up, w_down, b_down):
    bp, t, d = x_prompt.shape
    bs, ts, _ = x_sample.shape
    assert bp == 1 and ts == 1 and t % CHUNK == 0
    n_mem = mem_prompt.shape[1]
    n_pages = page_table.shape[1]
    ps = cache_k.shape[1]
    past = n_pages * ps
    n_e = w_router.shape[1]

    fp = _prep_params(w_in, g_in, g_qa, g_ka)
    mp = _prep_mix_params(w_in, g_in, g_vb, g_qm, w_br_a, w_br_b, w_br_m, w_out, g_ffn, w_router, b_router)

    xp = x_prompt.reshape(t, d)
    qa_p, k32_p, khm_p, v32_p, vt_p, qi_p, ki32_p, kibf_p, wit_p = _front(
        xp, _rope_tables(jnp.arange(t, dtype=I32)), *fp, tm=_row_tile(t, 512))
    mem_k, mem_v = _memory_kv(mem_prompt.reshape(n_mem, d), g_mem.reshape(1, d), w_mem_kv.astype(BF16),
                              g_km.reshape(1, HEAD_DIM_M))
    out_a_p = _prompt_attention(qi_p, qa_p, wit_p, kibf_p, khm_p, vt_p)
    x1_p, h2_p, te_p, tw_p = _mix(xp, out_a_p, mp, _prep_gmlp_chunked(w_s, b_s), mem_k, mem_v,
                                  tm=_row_tile(t, 256), chunked=True, shared_mem=True, emit_vb=False, n_experts=n_e)

    xs = x_sample.reshape(bs, d)
    qa_s, k32_s, _, v32_s, _, qi_s, ki32_s, _, wit_s = _front(
        xs, _rope_tables(jnp.full((bs,), past, I32)), *fp, tm=bs)
    out_a_s = _sample_attention(page_table, jnp.moveaxis(qi_s, 0, 1), jnp.moveaxis(qa_s, 0, 1),
                                wit_s.T.reshape(bs, N_HEADS_IDX, 1), ki32_s.reshape(bs, 1, D_IDX),
                                k32_s.reshape(bs, 1, WA_KV), v32_s.reshape(bs, 1, WA_KV),
                                cache_idx_k, cache_k, cache_v)
    x1_s, h2_s, te_s, tw_s, vb_s = _mix(xs, out_a_s.reshape(bs, WA_Q), mp, _prep_gmlp_single(w_s, b_s),
                                        cache_mem_k.reshape(bs, n_mem, WM_Q), cache_mem_v.reshape(bs, n_mem, WM_Q),
                                        tm=_row_tile(bs, SUBLANES), chunked=False, shared_mem=False, emit_vb=True,
                                        n_experts=n_e)

    y_p, y_s = _moe([(x1_p, h2_p, te_p, tw_p), (x1_s, h2_s, te_s, tw_s)],
                    w_gate, b_gate, w_up, b_up, w_down, b_down)

    n_pg = t // ps
    return (y_p.reshape(1, t, d), y_s.reshape(bs, 1, d),
            k32_p.reshape(1, n_pg, ps, N_KV_A, HEAD_DIM_A), v32_p.reshape(1, n_pg, ps, N_KV_A, HEAD_DIM_A),
            ki32_p.reshape(1, n_pg, ps, D_IDX),
            mem_k.reshape(1, n_mem, N_HEADS_M, HEAD_DIM_M), mem_v.reshape(1, n_mem, N_HEADS_M, HEAD_DIM_M),
            k32_s.reshape(bs, 1, N_KV_A, HEAD_DIM_A), v32_s.reshape(bs, 1, N_KV_A, HEAD_DIM_A),
            ki32_s.reshape(bs, 1, D_IDX), vb_s.reshape(bs, 1, WIDTH_B))
```

```python
import functools

import jax
import jax.numpy as jnp
from jax import lax
from jax.experimental import pallas as pl
from jax.experimental.pallas import tpu as pltpu

F32 = jnp.float32
BF16 = jnp.bfloat16
I32 = jnp.int32

N_HEADS_A = 8
N_KV_A = 2
HEAD_DIM_A = 64
TOPK_MAX = 256
N_HEADS_IDX = 8
D_IDX = 64
N_GROUPS_B = 4
GROUP_DIM_B = 128
WIDTH_B = N_GROUPS_B * GROUP_DIM_B
CHUNK = 128
N_HEADS_M = 4
HEAD_DIM_M = 128
ROPE_THETA = 500000.0
ROT_FRACTION = 4
N_BRANCH = 3
TOP_K_MOE = 4
SWIGLU_LIMIT = 7.0
SWIGLU_ALPHA = 1.702
EPS = 1e-6

WA_Q = N_HEADS_A * HEAD_DIM_A
WA_KV = N_KV_A * HEAD_DIM_A
WI_Q = N_HEADS_IDX * D_IDX
WM_Q = N_HEADS_M * HEAD_DIM_M
GQA = N_HEADS_A // N_KV_A

LANES = 128
SUBLANES = 8
VMEM_LIMIT = 56 * 1024 * 1024

LOG2_E = 1.4426950408889634
INT_MIN = -(2 ** 31)
INT_MAX = 2 ** 31 - 1
NEG = -1e30

VT_ROWS = HEAD_DIM_A + 16
ATTN_BQ = 128
ATTN_SC = 512
MOE_BM = 512
FF_CHUNK = 256


def _cparams(n_axes=1, vmem=VMEM_LIMIT):
    return pltpu.CompilerParams(dimension_semantics=("arbitrary",) * n_axes, vmem_limit_bytes=vmem)


def _const_spec(shape, single=False):
    zeros = (0,) * len(shape)
    if single:
        return pl.BlockSpec(shape, lambda *_: zeros, pipeline_mode=pl.Buffered(1))
    return pl.BlockSpec(shape, lambda *_: zeros)


def _dot(a, b):
    return jnp.dot(a, b, preferred_element_type=F32)


def _dot_nt(a, b):
    return lax.dot_general(a, b, (((1,), (1,)), ((), ())), preferred_element_type=F32)


def _rms(x, g):
    return x * lax.rsqrt(jnp.mean(x * x, axis=-1, keepdims=True) + EPS) * g


def _split_dot(x, w):
    hi = x.astype(BF16)
    lo = (x - hi.astype(F32)).astype(BF16)
    return _dot(hi, w) + _dot(lo, w)


def _front_kernel(x_ref, gin_ref, wa_ref, wwit_ref, wvt_ref, bd_ref, gqa_ref, gka_ref, cos_ref, sa_ref, sb_ref,
                  qa_ref, k32_ref, khm_ref, v32_ref, vt_ref, qi_ref, ki32_ref, kibf_ref, wit_ref, *, score_scale):
    x = x_ref[...]
    h = _rms(x, gin_ref[...]).astype(BF16)
    p = _dot(h, wa_ref[...])
    cos, sa, sb = cos_ref[...], sa_ref[...], sb_ref[...]
    bd = bd_ref[...]

    def head_norm(v, g):
        ssq = _split_dot(v * v, bd)
        return v * lax.rsqrt(ssq * (1.0 / HEAD_DIM_A) + EPS) * g

    def rope(v):
        return v * cos + pltpu.roll(v, LANES - 8, 1) * sa + pltpu.roll(v, 8, 1) * sb

    for s in range(WA_Q // LANES):
        v = p[:, s * LANES:(s + 1) * LANES]
        v = rope(head_norm(v, gqa_ref[...])) * (HEAD_DIM_A ** -0.5 * LOG2_E)
        vb = v.astype(BF16)
        qa_ref[2 * s] = vb[:, :HEAD_DIM_A]
        qa_ref[2 * s + 1] = vb[:, HEAD_DIM_A:]
    o = WA_Q
    k = rope(head_norm(p[:, o:o + WA_KV], gka_ref[...]))
    k32_ref[...] = k
    kb = k.astype(BF16)
    khm_ref[0] = kb[:, :HEAD_DIM_A]
    khm_ref[1] = kb[:, HEAD_DIM_A:]
    o += WA_KV
    v32_ref[...] = p[:, o:o + WA_KV]
    o += WA_KV
    for s in range(WI_Q // LANES):
        vb = rope(p[:, o + s * LANES:o + (s + 1) * LANES]).astype(BF16)
        qi_ref[2 * s] = vb[:, :D_IDX]
        qi_ref[2 * s + 1] = vb[:, D_IDX:]
    o += WI_Q
    ki = rope(p[:, o:o + LANES])[:, :D_IDX]
    ki32_ref[...] = ki
    kibf_ref[...] = ki.astype(BF16)
    wit_ref[...] = _dot_nt(wwit_ref[...], h) * score_scale
    vt = _dot_nt(wvt_ref[...], h)
    tail = jnp.where(lax.broadcasted_iota(I32, (VT_ROWS - HEAD_DIM_A, vt.shape[1]), 0) == 0, 1.0, 0.0)
    for g in range(N_KV_A):
        vt_ref[g] = jnp.concatenate([vt[g * HEAD_DIM_A:(g + 1) * HEAD_DIM_A], tail], axis=0).astype(BF16)


def _front(x2d, tabs, gin, wa, wwit, wvt, bd, gqa, gka, tm):
    n, d = x2d.shape
    cos, sa, sb = tabs
    row = lambda w: pl.BlockSpec((tm, w), lambda i: (i, 0))
    hm = lambda nh, w: pl.BlockSpec((nh, tm, w), lambda i: (0, i, 0))
    out_shape = (
        jax.ShapeDtypeStruct((N_HEADS_A, n, HEAD_DIM_A), BF16),
        jax.ShapeDtypeStruct((n, WA_KV), F32),
        jax.ShapeDtypeStruct((N_KV_A, n, HEAD_DIM_A), BF16),
        jax.ShapeDtypeStruct((n, WA_KV), F32),
        jax.ShapeDtypeStruct((N_KV_A, VT_ROWS, n), BF16),
        jax.ShapeDtypeStruct((N_HEADS_IDX, n, D_IDX), BF16),
        jax.ShapeDtypeStruct((n, D_IDX), F32),
        jax.ShapeDtypeStruct((n, D_IDX), BF16),
        jax.ShapeDtypeStruct((N_HEADS_IDX, n), F32),
    )
    out_specs = (hm(N_HEADS_A, HEAD_DIM_A), row(WA_KV), hm(N_KV_A, HEAD_DIM_A), row(WA_KV),
                 pl.BlockSpec((N_KV_A, VT_ROWS, tm), lambda i: (0, 0, i)), hm(N_HEADS_IDX, D_IDX), row(D_IDX), row(D_IDX),
                 pl.BlockSpec((N_HEADS_IDX, tm), lambda i: (0, i)))
    in_specs = [row(d), _const_spec(gin.shape), _const_spec(wa.shape), _const_spec(wwit.shape),
                _const_spec(wvt.shape), _const_spec(bd.shape), _const_spec(gqa.shape), _const_spec(gka.shape),
                row(LANES), row(LANES), row(LANES)]
    kern = functools.partial(_front_kernel, score_scale=D_IDX ** -0.5 * N_HEADS_IDX ** -0.5)
    return pl.pallas_call(kern, out_shape=out_shape, grid=(n // tm,), in_specs=in_specs, out_specs=out_specs,
                          compiler_params=_cparams(), name="front")(x2d, gin, wa, wwit, wvt, bd, gqa, gka, cos, sa, sb)


def _memkv_kernel(mem_ref, gmem_ref, w_ref, gkm_ref, k32_ref, v32_ref):
    h = _rms(mem_ref[...], gmem_ref[...]).astype(BF16)
    kv = _dot(h, w_ref[...])
    for hh in range(N_HEADS_M):
        sl = slice(hh * HEAD_DIM_M, (hh + 1) * HEAD_DIM_M)
        k32_ref[:, sl] = _rms(kv[:, sl], gkm_ref[...])
    v32_ref[...] = kv[:, WM_Q:]


def _memory_kv(mem2d, gmem, w, gkm):
    n = mem2d.shape[0]
    out = jax.ShapeDtypeStruct((n, WM_Q), F32)
    return pl.pallas_call(_memkv_kernel, out_shape=(out, out), name="memory_kv",
                          compiler_params=pltpu.CompilerParams(vmem_limit_bytes=VMEM_LIMIT))(mem2d, gmem, w, gkm)


def _sort_key(s):
    s = jnp.where(s == 0.0, 0.0, s)
    bits = pltpu.bitcast(s, I32)
    return bits ^ ((bits >> 31) & INT_MAX)


I16 = jnp.int16
I16_MIN = -(2 ** 15)
ATTN_SUB = 128


def _attn_kernel(qi_ref, qa_ref, wi_ref, ki_ref, k_ref, vt_ref, out_ref,
                 keys_ref, hi_ref, lo_ref, cut_ref, bias_ref, lg_ref, cm_ref, m_ref, acc_ref, *, k_sel, idx_bits):
    bq, sc = ATTN_BQ, ATTN_SC
    i = pl.program_id(0)
    n_ch = (i * bq) // sc + 1
    t_pos = i * bq + lax.broadcasted_iota(I32, (1, bq), 1)
    sub = lax.broadcasted_iota(I32, (sc, 1), 0)
    qi = qi_ref[...].reshape(N_HEADS_IDX * bq, D_IDX)
    w = wi_ref[...]

    def chunk(c):
        return pl.ds(pl.multiple_of(c * sc, sc), sc)

    def score_chunk(c, carry):
        off = pl.multiple_of(c * sc, sc)
        d = _dot_nt(ki_ref[chunk(c), :], qi)
        s = jnp.zeros((sc, bq), F32)
        for h in range(N_HEADS_IDX):
            s = s + jnp.maximum(d[:, h * bq:(h + 1) * bq], 0.0) * w[h:h + 1, :]
        key = jnp.where(off + sub <= t_pos, _sort_key(s), INT_MIN)
        keys_ref[chunk(c), :] = key
        hi_ref[chunk(c), :] = (key >> 16).astype(I16)
        lo_ref[chunk(c), :] = ((key & 0xFFFF) + I16_MIN).astype(I16)
        return carry

    lax.fori_loop(0, n_ch, score_chunk, 0)

    def count(pred):
        def body(c, acc):
            hit = jnp.where(pred(keys_ref[chunk(c), :], c * sc + sub), 1, 0)
            return acc + hit.reshape(sc // SUBLANES, SUBLANES, bq).sum(axis=0)
        acc = lax.fori_loop(0, n_ch, body, jnp.zeros((SUBLANES, bq), I32))
        return acc.sum(axis=0, keepdims=True)

    def count16(ref, cand):
        c16 = cand.astype(I16)
        rows = 2 * SUBLANES

        def body(c, acc):
            hit = jnp.where(ref[chunk(c), :] >= c16, jnp.ones((), I16), jnp.zeros((), I16))
            parts = [hit[r * rows:(r + 1) * rows, :] for r in range(sc // rows)]
            while len(parts) > 1:
                parts = [a + b for a, b in zip(parts[::2], parts[1::2])]
            return acc + parts[0]
        acc = lax.fori_loop(0, n_ch, body, jnp.zeros((rows, bq), I16))
        return acc.astype(I32).sum(axis=0, keepdims=True)

    def search16(ref, want, frozen, cnt0):
        def body(j, st):
            thr, cnt = st
            cand = thr + (jnp.int32(1) << (15 - j))
            c = count16(ref, cand)
            take = jnp.logical_and(c >= want, jnp.logical_not(frozen))
            return jnp.where(take, cand, thr), jnp.where(take, c, cnt)

        return lax.fori_loop(0, 16, body, (jnp.full((1, bq), I16_MIN, I32), cnt0))

    short = t_pos + 1 <= k_sel
    total = jnp.zeros((1, bq), I32) + n_ch * sc
    thr_hi, cnt_hi = search16(hi_ref, k_sel, short, total)
    above = count16(hi_ref, jnp.minimum(thr_hi + 1, -I16_MIN - 1))
    above = jnp.where(thr_hi == -I16_MIN - 1, 0, above)
    settled = jnp.logical_or(cnt_hi == k_sel, short)
    thr_hi16 = thr_hi.astype(I16)

    def mask_low(c, carry):
        lo_ref[chunk(c), :] = jnp.where(hi_ref[chunk(c), :] == thr_hi16, lo_ref[chunk(c), :],
                                        jnp.full((), I16_MIN, I16))
        return carry

    lax.fori_loop(0, n_ch, mask_low, 0)
    thr_lo, cnt_lo = search16(lo_ref, k_sel - above, settled, cnt_hi - above)
    thr = (thr_hi << 16) + (thr_lo - I16_MIN)
    cnt = jnp.where(settled, cnt_hi, above + cnt_lo)
    cut_ref[...] = jnp.full((1, bq), INT_MAX, I32)

    excess = jnp.logical_and(cnt > k_sel, thr > INT_MIN)
    has_ties = jnp.max(jnp.where(excess, 1, 0)) > 0

    @pl.when(has_ties)
    def _():
        want = k_sel - count(lambda blk, _: blk > thr)

        def tie_body(j, cpos):
            cand = cpos + (jnp.int32(1) << (idx_bits - 1 - j))
            below = count(lambda blk, pos: jnp.where(blk == thr, pos, INT_MAX) < cand)
            return jnp.where(below < want, cand, cpos)

        cpos = lax.fori_loop(0, idx_bits, tie_body, jnp.zeros((1, bq), I32))
        cut_ref[...] = jnp.where(excess, cpos, INT_MAX)

    m_ref[...] = jnp.full(m_ref.shape, NEG, F32)
    acc_ref[...] = jnp.zeros(acc_ref.shape, F32)
    cut = cut_ref[...]
    floor = jnp.where(thr == INT_MIN, INT_MIN, thr - 1)
    n_sub = sc // ATTN_SUB

    qs = [qa_ref[g * GQA:(g + 1) * GQA].reshape(GQA * bq, HEAD_DIM_A) for g in range(N_KV_A)]
    neg_row = jnp.full((1, GQA * bq), NEG, F32)

    def stage(c, par, cb, par_b):
        do_a, do_b = c is not None, cb is not None
        if do_a:
            off = pl.multiple_of(c * sc, sc)
            keyc = keys_ref[chunk(c), :]

            @pl.when(has_ties)
            def _():
                tie = jnp.where(off + sub <= cut, 0.0, NEG)
                bias = jnp.where(keyc > thr, 0.0, jnp.where(keyc == thr, tie, NEG))
                bias_ref[...] = jnp.where(keyc == INT_MIN, NEG, bias)

            @pl.when(jnp.logical_not(has_ties))
            def _():
                bias_ref[...] = jnp.where(keyc > floor, 0.0, NEG)
            cmax = [neg_row] * N_KV_A
        if do_b:
            off_b = pl.multiple_of(cb * sc, sc)
            m_new, acc = [], []
            for g in range(N_KV_A):
                m_old = m_ref[g]
                m_new.append(jnp.maximum(m_old, cm_ref[par_b, g]))
                acc.append(jnp.exp2(m_old - m_new[g]) * acc_ref[g])
                m_ref[g] = m_new[g]
        for r in range(n_sub):
            rows = slice(r * ATTN_SUB, (r + 1) * ATTN_SUB)
            if do_a:
                bias4 = jnp.concatenate([bias_ref[rows, :]] * GQA, axis=1)
                for g in range(N_KV_A):
                    lg = _dot_nt(k_ref[g, pl.ds(off + r * ATTN_SUB, ATTN_SUB), :], qs[g]) + bias4
                    lg_ref[par, g, rows, :] = lg
                    cmax[g] = jnp.maximum(cmax[g], lg.max(axis=0, keepdims=True))
            if do_b:
                for g in range(N_KV_A):
                    p = jnp.exp2(lg_ref[par_b, g, rows, :] - m_new[g])
                    vt = vt_ref[g, :, pl.ds(off_b + r * ATTN_SUB, ATTN_SUB)]
                    acc[g] = acc[g] + _dot(vt, p.astype(BF16))
        if do_b:
            for g in range(N_KV_A):
                acc_ref[g] = acc[g]
        if do_a:
            for g in range(N_KV_A):
                cm_ref[par, g] = cmax[g]

    stage(jnp.int32(0), 0, None, None)

    def stage_pair(j, carry):
        stage(2 * j + 1, 1, 2 * j, 0)
        stage(2 * j + 2, 0, 2 * j + 1, 1)
        return carry

    last = n_ch - 1
    lax.fori_loop(0, last // 2, stage_pair, 0)

    @pl.when(last % 2 == 1)
    def _():
        stage(last, 1, last - 1, 0)
        stage(None, None, last, 1)

    @pl.when(last % 2 == 0)
    def _():
        stage(None, None, last, 0)

    heads = []
    for g in range(N_KV_A):
        acc = acc_ref[g]
        o = acc[:HEAD_DIM_A] / acc[HEAD_DIM_A:HEAD_DIM_A + 1]
        heads += [o[:, hh * bq:(hh + 1) * bq] for hh in range(GQA)]
    out_ref[...] = jnp.concatenate(heads, axis=0).T.astype(out_ref.dtype)


def _prompt_attention(qi_hm, qa_hm, wit, kibf, khm, vt):
    t = kibf.shape[0]
    bq = ATTN_BQ
    k_sel = min(TOPK_MAX, t // 4)
    kern = functools.partial(_attn_kernel, k_sel=k_sel, idx_bits=max(1, (t - 1).bit_length()))
    in_specs = [pl.BlockSpec((N_HEADS_IDX, bq, D_IDX), lambda i: (0, i, 0)),
                pl.BlockSpec((N_HEADS_A, bq, HEAD_DIM_A), lambda i: (0, i, 0)),
                pl.BlockSpec((N_HEADS_IDX, bq), lambda i: (0, i)),
                _const_spec(kibf.shape, True), _const_spec(khm.shape, True), _const_spec(vt.shape, True)]
    scratch = [pltpu.VMEM((t, bq), I32), pltpu.VMEM((t, bq), I16), pltpu.VMEM((t, bq), I16),
               pltpu.VMEM((1, bq), I32), pltpu.VMEM((ATTN_SC, bq), F32),
               pltpu.VMEM((2, N_KV_A, ATTN_SC, GQA * bq), F32), pltpu.VMEM((2, N_KV_A, 1, GQA * bq), F32),
               pltpu.VMEM((N_KV_A, 1, GQA * bq), F32),
               pltpu.VMEM((N_KV_A, VT_ROWS, GQA * bq), F32)]
    return pl.pallas_call(kern, out_shape=jax.ShapeDtypeStruct((t, WA_Q), BF16), grid=(t // bq,),
                          in_specs=in_specs, out_specs=pl.BlockSpec((bq, WA_Q), lambda i: (i, 0)),
                          scratch_shapes=scratch, compiler_params=_cparams(), name="attn")(
                              qi_hm, qa_hm, wit, kibf, khm, vt)


SAMPLE_CK = 2048


def _sample_attn_kernel(pt_ref, qi_ref, qa_ref, wi_ref, kin_ref, kn_ref, vn_ref, cidx_hbm, ck_hbm, cv_hbm, out_ref,
                        idx_buf, k_buf, v_buf, scr_ref, sems, *, n_pages, ps, ck, k_sel, idx_bits):
    b = pl.program_id(0)
    nb = pl.num_programs(0)
    past = n_pages * ps
    nc = past // ck

    def page_copy(which, bb, p):
        src, dst = ((cidx_hbm, idx_buf), (ck_hbm, k_buf), (cv_hbm, v_buf))[which]
        cols = pl.ds(pl.multiple_of(p * ps, ps), ps)
        dst = dst.at[:, cols] if which == 0 else dst.at[:, :, cols]
        return pltpu.make_async_copy(src.at[pt_ref[bb, p]], dst, sems.at[which])

    def for_pages(fn):
        lax.fori_loop(0, n_pages, lambda p, c: (fn(p), c)[1], 0)

    @pl.when(b == 0)
    def _():
        for_pages(lambda p: page_copy(0, b, p).start())

    for_pages(lambda p: (page_copy(1, b, p).start(), page_copy(2, b, p).start()))
    for_pages(lambda p: page_copy(0, b, p).wait())

    qi = qi_ref[0]
    w = wi_ref[0]
    for c in range(nc):
        d = _dot(qi, idx_buf[:, c * ck:(c + 1) * ck].astype(BF16))
        scr_ref[c:c + 1, :] = (jnp.maximum(d, 0.0) * w).sum(axis=0, keepdims=True)
    d_new = (qi.astype(F32) * kin_ref[0].astype(BF16).astype(F32)).sum(axis=-1, keepdims=True)
    s_new = (jnp.maximum(d_new, 0.0) * w).sum(axis=0, keepdims=True)

    @pl.when(b + 1 < nb)
    def _():
        for_pages(lambda p: page_copy(0, b + 1, p).start())

    keys = _sort_key(scr_ref[...])
    key_new = _sort_key(s_new)
    pos = (lax.broadcasted_iota(I32, keys.shape, 0) * ck + lax.broadcasted_iota(I32, keys.shape, 1))

    def count(pred):
        hit = jnp.where(pred(keys, pos), 1, 0).sum(axis=1, keepdims=True).sum(axis=0, keepdims=True)
        return hit + jnp.where(pred(key_new, past), 1, 0)

    def search_body(j, st):
        thr, cnt = st
        cand = thr + (jnp.int32(1) << (31 - j))
        c = count(lambda kk, _: kk >= cand)
        take = c >= k_sel
        return jnp.where(take, cand, thr), jnp.where(take, c, cnt)

    st0 = (jnp.full((1, 1), INT_MIN, I32), jnp.full((1, 1), past + 1, I32))
    thr, cnt = lax.fori_loop(0, 32, search_body, st0)

    want = k_sel - count(lambda kk, _: kk > thr)

    def tie_body(j, cpos):
        cand = cpos + (jnp.int32(1) << (idx_bits - 1 - j))
        below = count(lambda kk, pp: jnp.where(kk == thr, pp, INT_MAX) < cand)
        return jnp.where(below < want, cand, cpos)

    cpos = lax.fori_loop(0, idx_bits, tie_body, jnp.zeros((1, 1), I32))
    cut = jnp.where(cnt > k_sel, cpos, INT_MAX)

    def sel_bias(kk, pp):
        tie = jnp.where(pp <= cut, 0.0, NEG)
        return jnp.where(kk > thr, 0.0, jnp.where(kk == thr, tie, NEG))

    bias = sel_bias(keys, pos)
    bias_new = sel_bias(key_new, past)

    for_pages(lambda p: (page_copy(1, b, p).wait(), page_copy(2, b, p).wait()))

    qa = qa_ref[0]
    heads = []
    for g in range(N_KV_A):
        qg = qa[g * GQA:(g + 1) * GQA]
        cols = slice(g * HEAD_DIM_A, (g + 1) * HEAD_DIM_A)
        m = jnp.full((GQA, 1), NEG, F32)
        l = jnp.zeros((GQA, 1), F32)
        acc = jnp.zeros((GQA, HEAD_DIM_A), F32)
        for c in range(nc):
            span = slice(c * ck, (c + 1) * ck)
            lg = _dot(qg, k_buf[g, :, span].astype(BF16)) + bias[c:c + 1, :]
            m_new = jnp.maximum(m, lg.max(axis=-1, keepdims=True))
            p = jnp.exp2(lg - m_new)
            alpha = jnp.exp2(m - m_new)
            l = alpha * l + p.sum(axis=-1, keepdims=True)
            acc = alpha * acc + _dot_nt(p.astype(BF16), v_buf[g, :, span].astype(BF16))
            m = m_new
        kn = kn_ref[0][:, cols].astype(BF16).astype(F32)
        vn = vn_ref[0][:, cols].astype(BF16).astype(F32)
        lg = (qg.astype(F32) * kn).sum(axis=-1, keepdims=True) + bias_new
        m_new = jnp.maximum(m, lg)
        p = jnp.exp2(lg - m_new)
        alpha = jnp.exp2(m - m_new)
        l = alpha * l + p
        acc = alpha * acc + p * vn
        o = acc / l
        heads += [o[hh:hh + 1, :] for hh in range(GQA)]
    out_ref[0] = jnp.concatenate(heads, axis=1)


def _sample_attention(page_table, qi_s, qa_s, wi_s, ki_new, k_new, v_new, cache_idx_k, cache_k, cache_v):
    nbatch, n_pages = page_table.shape
    n_pool, ps, d_idx = cache_idx_k.shape
    past = n_pages * ps
    ck = min(SAMPLE_CK, past)
    assert past % ck == 0 and ck % ps == 0
    k_sel = min(TOPK_MAX, (past + 1) // 4)
    kern = functools.partial(_sample_attn_kernel, n_pages=n_pages, ps=ps, ck=ck, k_sel=k_sel,
                             idx_bits=past.bit_length())
    per_b = lambda shape: pl.BlockSpec((1,) + shape, lambda b, pt: (b, 0, 0))
    any_spec = pl.BlockSpec(memory_space=pl.ANY)
    grid_spec = pltpu.PrefetchScalarGridSpec(
        num_scalar_prefetch=1, grid=(nbatch,),
        in_specs=[per_b((N_HEADS_IDX, d_idx)), per_b((N_HEADS_A, HEAD_DIM_A)), per_b((N_HEADS_IDX, 1)),
                  per_b((1, d_idx)), per_b((1, WA_KV)), per_b((1, WA_KV)), any_spec, any_spec, any_spec],
        out_specs=per_b((1, WA_Q)),
        scratch_shapes=[pltpu.VMEM((d_idx, past), F32), pltpu.VMEM((N_KV_A, HEAD_DIM_A, past), F32),
                        pltpu.VMEM((N_KV_A, HEAD_DIM_A, past), F32),
                        pltpu.VMEM((past // ck, ck), F32), pltpu.SemaphoreType.DMA((3,))])
    return pl.pallas_call(kern, out_shape=jax.ShapeDtypeStruct((nbatch, 1, WA_Q), F32), grid_spec=grid_spec,
                          compiler_params=_cparams(), name="sample_attn")(
                              page_table, qi_s, qa_s, wi_s, ki_new, k_new, v_new,
                              jnp.transpose(cache_idx_k, (0, 2, 1)), jnp.transpose(cache_k, (0, 2, 3, 1)),
                              jnp.transpose(cache_v, (0, 2, 3, 1)))


def _rope_tables(pos):
    rot = HEAD_DIM_A // ROT_FRACTION
    half = rot // 2
    inv_freq = ROPE_THETA ** (-jnp.arange(half, dtype=F32) / half)
    ang = pos.astype(F32)[:, None] * inv_freq[None, :]
    cos, sin = jnp.cos(ang), jnp.sin(ang)
    n = pos.shape[0]
    z = lambda w: jnp.zeros((n, w), F32)
    c = jnp.concatenate([cos, cos, jnp.ones((n, HEAD_DIM_A - rot), F32)], axis=1)
    sa = jnp.concatenate([-sin, z(HEAD_DIM_A - half)], axis=1)
    sb = jnp.concatenate([z(half), sin, z(HEAD_DIM_A - rot)], axis=1)
    rep = LANES // HEAD_DIM_A
    return tuple(jnp.tile(a, (1, rep)) for a in (c, sa, sb))


def _prep_params(w_in, g_in, g_qa, g_ka):
    d = w_in.shape[0]
    w_qa, w_ka, w_va, w_qi, w_ki, w_wi = _split_w_in(w_in, d)[:6]
    wa = jnp.concatenate([w_qa, w_ka, w_va, w_qi, w_ki, jnp.zeros((d, LANES - D_IDX), w_in.dtype)], axis=1)
    lane = jnp.arange(LANES)
    bd = (lane[:, None] // HEAD_DIM_A == lane[None, :] // HEAD_DIM_A).astype(BF16)
    rep = LANES // HEAD_DIM_A
    return (g_in.reshape(1, d), wa.astype(BF16), w_wi.T.astype(BF16), w_va.T.astype(BF16), bd,
            jnp.tile(g_qa, rep).reshape(1, LANES), jnp.tile(g_ka, rep).reshape(1, LANES))


def _split_w_in(w_in, d):
    sizes = (WA_Q, WA_KV, WA_KV, WI_Q, D_IDX, N_HEADS_IDX, WIDTH_B, WIDTH_B, WM_Q, N_BRANCH * d)
    parts, o = [], 0
    for s in sizes:
        parts.append(w_in[:, o:o + s])
        o += s
    return parts


def _mix_kernel(*refs, chunked, shared_mem, emit_vb, n_experts):
    it = iter(refs)
    x_ref, oa_ref, gin_ref, wb_ref, gvb_ref, gqm_ref = (next(it) for _ in range(6))
    if chunked:
        wtril_ref, bt_ref = next(it), next(it)
    else:
        wdiag_ref, bdiag_ref = next(it), next(it)
    mk_ref, mv_ref = next(it), next(it)
    wbra_ref, wbrb_ref, wbrm_ref, wout_ref, gffn_ref, wrh_ref, wrl_ref, br_ref = (next(it) for _ in range(8))
    x1_ref, h2_ref, te_ref, tw_ref = (next(it) for _ in range(4))
    vb_ref = next(it) if emit_vb else None

    x = x_ref[...]
    tm, d = x.shape
    h = _rms(x, gin_ref[...]).astype(BF16)
    p = _dot(h, wb_ref[...])
    ub = p[:, :WIDTH_B]
    vb = _rms(p[:, WIDTH_B:2 * WIDTH_B], gvb_ref[...])
    if emit_vb:
        vb_ref[...] = vb
    o = 2 * WIDTH_B
    qm = p[:, o:o + WM_Q]
    o += WM_Q
    gates = p[:, o:o + N_BRANCH * d]

    if chunked:
        vbb = vb.astype(BF16)
        rows = []
        for cc in range(tm // CHUNK):
            cols = []
            for g in range(N_GROUPS_B):
                vg = vbb[cc * CHUNK:(cc + 1) * CHUNK, g * GROUP_DIM_B:(g + 1) * GROUP_DIM_B]
                cols.append(_dot(wtril_ref[g], vg))
            rows.append(jnp.concatenate(cols, axis=1) + bt_ref[...])
        z = jnp.concatenate(rows, axis=0) if len(rows) > 1 else rows[0]
    else:
        z = vb * wdiag_ref[...] + bdiag_ref[...]
    out_b = ub * z

    scale_m = HEAD_DIM_M ** -0.5
    outs = []
    for hh in range(N_HEADS_M):
        sl = slice(hh * HEAD_DIM_M, (hh + 1) * HEAD_DIM_M)
        qh = _rms(qm[:, sl], gqm_ref[...])
        if shared_mem:
            lg = _dot_nt(qh.astype(BF16), mk_ref[:, sl].astype(BF16)) * scale_m
            pm = jnp.exp(lg - lg.max(axis=-1, keepdims=True))
            pm = pm / pm.sum(axis=-1, keepdims=True)
            outs.append(_dot(pm.astype(BF16), mv_ref[:, sl].astype(BF16)))
        else:
            per_row = []
            for r in range(tm):
                kr = mk_ref[r, :, sl]
                lg = (kr * qh[r:r + 1, :]).sum(axis=-1, keepdims=True) * scale_m
                pm = jnp.exp(lg - lg.max(axis=0, keepdims=True))
                pm = pm / pm.sum(axis=0, keepdims=True)
                per_row.append((pm * mv_ref[r, :, sl]).sum(axis=0, keepdims=True))
            outs.append(jnp.concatenate(per_row, axis=0))
    out_m = jnp.concatenate(outs, axis=1)

    sig = jax.nn.sigmoid
    merged = (sig(gates[:, :d]) * _dot(oa_ref[...].astype(BF16), wbra_ref[...])
              + sig(gates[:, d:2 * d]) * _dot(out_b.astype(BF16), wbrb_ref[...])
              + sig(gates[:, 2 * d:]) * _dot(out_m.astype(BF16), wbrm_ref[...]))
    x1 = x + _dot(merged.astype(BF16), wout_ref[...])
    x1_ref[...] = x1
    h2 = _rms(x1, gffn_ref[...])
    h2_ref[...] = h2.astype(h2_ref.dtype)

    hi = h2.astype(BF16)
    lo = (h2 - hi.astype(F32)).astype(BF16)
    lg = _dot(hi, wrh_ref[...]) + _dot(hi, wrl_ref[...]) + _dot(lo, wrh_ref[...]) + br_ref[...]
    lane = lax.broadcasted_iota(I32, lg.shape, 1)
    lg = jnp.where(lane < n_experts, lg, -jnp.inf)
    vals, idxs = [], []
    for _ in range(TOP_K_MOE):
        m = lg.max(axis=-1, keepdims=True)
        idx = jnp.where(lg == m, lane, LANES).min(axis=-1, keepdims=True)
        vals.append(m)
        idxs.append(idx)
        lg = jnp.where(lane == idx, -jnp.inf, lg)
    ex = [jnp.exp(v - vals[0]) for v in vals]
    den = ex[0]
    for e in ex[1:]:
        den = den + e
    te = jnp.full(lane.shape, -1, I32)
    tw = jnp.zeros(lane.shape, F32)
    for j in range(TOP_K_MOE):
        te = jnp.where(lane == j, idxs[j], te)
        tw = jnp.where(lane == j, ex[j] / den, tw)
    te_ref[...] = te
    tw_ref[...] = tw


def _mix(x2d, out_a, mix_params, gmlp_params, mem_k, mem_v, *, tm, chunked, shared_mem, emit_vb, n_experts):
    n, d = x2d.shape
    gin, wb, gvb, gqm, wbra, wbrb, wbrm, wout, gffn, wrh, wrl, br = mix_params
    row = lambda w: pl.BlockSpec((tm, w), lambda i: (i, 0))
    if shared_mem:
        mem_specs = [_const_spec(mem_k.shape), _const_spec(mem_v.shape)]
    else:
        mspec = pl.BlockSpec((tm,) + mem_k.shape[1:], lambda i: (i, 0, 0))
        mem_specs = [mspec, mspec]
    consts = lambda arrs: [_const_spec(a.shape) for a in arrs]
    in_specs = ([row(d), row(WA_Q)] + consts([gin, wb, gvb, gqm]) + consts(gmlp_params) + mem_specs
                + consts([wbra, wbrb, wbrm, wout, gffn, wrh, wrl, br]))
    out_shape = [jax.ShapeDtypeStruct((n, d), F32), jax.ShapeDtypeStruct((n, d), F32),
                 jax.ShapeDtypeStruct((n, LANES), I32), jax.ShapeDtypeStruct((n, LANES), F32)]
    out_specs = [row(d), row(d), row(LANES), row(LANES)]
    if emit_vb:
        out_shape.append(jax.ShapeDtypeStruct((n, WIDTH_B), F32))
        out_specs.append(row(WIDTH_B))
    kern = functools.partial(_mix_kernel, chunked=chunked, shared_mem=shared_mem, emit_vb=emit_vb,
                             n_experts=n_experts)
    return pl.pallas_call(kern, out_shape=tuple(out_shape), grid=(n // tm,), in_specs=in_specs,
                          out_specs=tuple(out_specs), compiler_params=_cparams(), name="mix")(
                              x2d, out_a, gin, wb, gvb, gqm, *gmlp_params, mem_k, mem_v,
                              wbra, wbrb, wbrm, wout, gffn, wrh, wrl, br)


def _prep_mix_params(w_in, g_in, g_vb, g_qm, w_br_a, w_br_b, w_br_m, w_out, g_ffn, w_router, b_router):
    d = w_in.shape[0]
    parts = _split_w_in(w_in, d)
    wb = jnp.concatenate(parts[6:10], axis=1).astype(BF16)
    n_e = w_router.shape[1]
    wr = jnp.pad(w_router, ((0, 0), (0, LANES - n_e)))
    wrh = wr.astype(BF16)
    wrl = (wr - wrh.astype(F32)).astype(BF16)
    br = jnp.pad(b_router, (0, LANES - n_e)).reshape(1, LANES)
    return (g_in.reshape(1, d), wb, g_vb.reshape(1, WIDTH_B), g_qm.reshape(1, HEAD_DIM_M),
            w_br_a.astype(BF16), w_br_b.astype(BF16), w_br_m.astype(BF16), w_out.astype(BF16),
            g_ffn.reshape(1, d), wrh, wrl, br)


def _prep_gmlp_chunked(w_s, b_s):
    tril = jnp.tril(jnp.ones((CHUNK, CHUNK), w_s.dtype))
    wtril = (w_s * tril).astype(BF16)
    bt = jnp.repeat(b_s.T, GROUP_DIM_B, axis=1)
    return wtril, bt


def _prep_gmlp_single(w_s, b_s):
    wdiag = jnp.repeat(w_s[:, 0, 0], GROUP_DIM_B).reshape(1, WIDTH_B)
    bdiag = jnp.repeat(b_s[:, 0], GROUP_DIM_B).reshape(1, WIDTH_B)
    return wdiag, bdiag


MOE_TP = 256


def _plan_kernel(te_ref, tri_ref, upper_ref, lslot_ref, cnt_ref, tstart_ref, gprev_ref, total_ref, carry_ref):
    @pl.when(pl.program_id(0) == 0)
    def _():
        carry_ref[...] = jnp.zeros(carry_ref.shape, F32)

    te = te_ref[...]
    lane = lax.broadcasted_iota(I32, te.shape, 1)
    picks = [te[:, j:j + 1] for j in range(TOP_K_MOE)]
    onehot = jnp.zeros(te.shape, F32)
    for e in picks:
        onehot = onehot + jnp.where(lane == e, 1.0, 0.0)
    before = _dot(tri_ref[...], onehot.astype(BF16))
    cnt = onehot.sum(axis=0, keepdims=True)
    cnt = jnp.floor((cnt + (SUBLANES - 1)) * (1.0 / SUBLANES)) * SUBLANES
    cnt8 = jnp.broadcast_to(cnt, (SUBLANES, LANES)).astype(BF16)
    tstart = _dot(cnt8, upper_ref[...])[0:1, :]
    lslot = jnp.zeros(te.shape, I32)
    for j, e in enumerate(picks):
        r = jnp.where(lane == e, before + tstart, 0.0).sum(axis=-1, keepdims=True).astype(I32)
        lslot = jnp.where(lane == j, r, lslot)
    lslot_ref[...] = lslot
    cnt_ref[0] = cnt.astype(I32)
    tstart_ref[0] = tstart.astype(I32)
    gprev_ref[0] = carry_ref[...].astype(I32)
    carry_ref[...] = carry_ref[...] + cnt
    total_ref[...] = carry_ref[...].astype(I32)


def _moe_plan(te_all):
    n = te_all.shape[0]
    tp = MOE_TP
    n_tiles = n // tp
    r = jnp.arange(tp)
    tri = (r[:, None] > r[None, :]).astype(BF16)
    e = jnp.arange(LANES)
    upper = (e[:, None] < e[None, :]).astype(BF16)
    per_tile = jax.ShapeDtypeStruct((n_tiles, 1, LANES), I32)
    tile_spec = pl.BlockSpec((1, 1, LANES), lambda i: (i, 0, 0))
    return pl.pallas_call(
        _plan_kernel,
        out_shape=(jax.ShapeDtypeStruct((n, LANES), I32), per_tile, per_tile, per_tile,
                   jax.ShapeDtypeStruct((1, LANES), I32)),
        grid=(n_tiles,),
        in_specs=[pl.BlockSpec((tp, LANES), lambda i: (i, 0)), _const_spec((tp, tp)), _const_spec((LANES, LANES))],
        out_specs=(pl.BlockSpec((tp, LANES), lambda i: (i, 0)), tile_spec, tile_spec, tile_spec,
                   _const_spec((1, LANES))),
        scratch_shapes=[pltpu.VMEM((1, LANES), F32)],
        compiler_params=_cparams(), name="moe_plan")(te_all, tri, upper)


def _run_pieces(tp):
    sizes, s = [], SUBLANES
    while s <= tp:
        sizes.append(s)
        s *= 2
    return tuple(reversed(sizes))


def _local_rows(tp, n_e):
    return tp * TOP_K_MOE + n_e * SUBLANES


def _for_runs(tabs, tile, n_e, tp, fn):
    cnt_s, tstart_s, gbase_s = tabs

    def per_expert(e, c):
        idx = tile * n_e + e
        length, src0, dst0 = cnt_s[idx], tstart_s[idx], gbase_s[idx]
        off = jnp.int32(0)
        for size in _run_pieces(tp):
            piece = length & size

            @pl.when(piece != 0)
            def _():
                fn(pl.multiple_of(src0 + off, SUBLANES), pl.multiple_of(dst0 + off, SUBLANES), size)
            off = off + piece
        return c

    lax.fori_loop(0, n_e, per_expert, 0)


def _slot_matrix(lslot, vals, n_rows):
    col = lax.broadcasted_iota(I32, (lslot.shape[0], n_rows), 1)
    m = jnp.zeros(col.shape, F32)
    for j in range(TOP_K_MOE):
        v = 1.0 if vals is None else vals[:, j:j + 1]
        m = m + jnp.where(lslot[:, j:j + 1] == col, v, 0.0)
    return m


def _dispatch_kernel(cnt_s, tstart_s, gbase_s, lslot_ref, h2_ref, xs_in, xs_out, stage, sem, *, tp, n_e, tile0):
    del xs_in
    n_rows = _local_rows(tp, n_e)
    i = pl.program_id(0)
    slot = i % 2
    pt = _slot_matrix(lslot_ref[...], None, n_rows).astype(BF16)
    stage[slot] = lax.dot_general(pt, h2_ref[...].astype(BF16), (((0,), (0,)), ((), ())),
                                  preferred_element_type=F32)

    def runs(tile, s, op):
        _for_runs((cnt_s, tstart_s, gbase_s), tile0 + tile, n_e, tp,
                  lambda src, dst, size: op(pltpu.make_async_copy(
                      stage.at[s, pl.ds(src, size), :], xs_out.at[pl.ds(dst, size), :], sem.at[s])))

    @pl.when(i > 0)
    def _():
        runs(i - 1, 1 - slot, lambda cp: cp.wait())

    runs(i, slot, lambda cp: cp.start())

    @pl.when(i == pl.num_programs(0) - 1)
    def _():
        runs(i, slot, lambda cp: cp.wait())


def _dispatch(tabs, tile0, lslot, h2, xs, n_e):
    n, d = h2.shape
    tp = min(MOE_TP, n)
    n_rows = _local_rows(tp, n_e)
    kern = functools.partial(_dispatch_kernel, tp=tp, n_e=n_e, tile0=tile0)
    any_spec = pl.BlockSpec(memory_space=pl.ANY)
    grid_spec = pltpu.PrefetchScalarGridSpec(
        num_scalar_prefetch=3, grid=(n // tp,),
        in_specs=[pl.BlockSpec((tp, LANES), lambda i, *_: (i, 0)), pl.BlockSpec((tp, d), lambda i, *_: (i, 0)), any_spec],
        out_specs=any_spec,
        scratch_shapes=[pltpu.VMEM((2, n_rows, d), F32), pltpu.SemaphoreType.DMA((2,))])
    return pl.pallas_call(
        kern, out_shape=jax.ShapeDtypeStruct(xs.shape, xs.dtype), grid_spec=grid_spec,
        input_output_aliases={5: 0},
        compiler_params=pltpu.CompilerParams(dimension_semantics=("arbitrary",), has_side_effects=True,
                                             vmem_limit_bytes=VMEM_LIMIT),
        name="dispatch")(*tabs, lslot, h2, xs)


def _expert_kernel(be_ref, nu_ref, xs_ref, wg_ref, wu_ref, wd_ref, bg_ref, bu_ref, bd_ref, y_ref,
                   wg_bf, wu_bf, wd_bf):
    b = pl.program_id(0)

    @pl.when(b >= nu_ref[0])
    def _():
        y_ref[...] = jnp.zeros(y_ref.shape, F32)

    @pl.when(b < nu_ref[0])
    def _():
        prev = be_ref[jnp.maximum(b - 1, 0)]

        @pl.when(jnp.logical_or(b == 0, be_ref[b] != prev))
        def _():
            wg_bf[...] = wg_ref[0].astype(BF16)
            wu_bf[...] = wu_ref[0].astype(BF16)
            wd_bf[...] = wd_ref[0].astype(BF16)

        xb = xs_ref[...].astype(BF16)
        ff = wg_bf.shape[1]
        y = jnp.zeros(y_ref.shape, F32) + bd_ref[0]
        for n in range(ff // FF_CHUNK):
            sl = slice(n * FF_CHUNK, (n + 1) * FF_CHUNK)
            hg = jnp.minimum(_dot(xb, wg_bf[:, sl]) + bg_ref[0][:, sl], SWIGLU_LIMIT)
            hu = jnp.clip(_dot(xb, wu_bf[:, sl]) + bu_ref[0][:, sl], -SWIGLU_LIMIT, SWIGLU_LIMIT)
            act = hg * jax.nn.sigmoid(SWIGLU_ALPHA * hg) * (hu + 1.0)
            y = y + _dot(act.astype(BF16), wd_bf[sl, :])
        y_ref[...] = y


def _experts(xs, block_e, n_used, w_gate, b_gate, w_up, b_up, w_down, b_down):
    ns, d = xs.shape
    n_e, _, ff = w_gate.shape
    bm = MOE_BM
    blk = lambda b, be, nu: (jnp.minimum(b, nu[0] - 1), 0)
    wsel = lambda b, be, nu: (be[b], 0, 0)
    grid_spec = pltpu.PrefetchScalarGridSpec(
        num_scalar_prefetch=2, grid=(ns // bm,),
        in_specs=[pl.BlockSpec((bm, d), blk),
                  pl.BlockSpec((1, d, ff), wsel), pl.BlockSpec((1, d, ff), wsel), pl.BlockSpec((1, ff, d), wsel),
                  pl.BlockSpec((1, 1, ff), wsel), pl.BlockSpec((1, 1, ff), wsel), pl.BlockSpec((1, 1, d), wsel)],
        out_specs=pl.BlockSpec((bm, d), lambda b, be, nu: (b, 0)),
        scratch_shapes=[pltpu.VMEM((d, ff), BF16), pltpu.VMEM((d, ff), BF16), pltpu.VMEM((ff, d), BF16)])
    return pl.pallas_call(_expert_kernel, out_shape=jax.ShapeDtypeStruct((ns, d), F32), grid_spec=grid_spec,
                          compiler_params=_cparams(), name="experts")(
                              block_e, n_used, xs, w_gate, w_up, w_down,
                              b_gate.reshape(n_e, 1, ff), b_up.reshape(n_e, 1, ff), b_down.reshape(n_e, 1, d))


def _combine_kernel(cnt_s, tstart_s, gbase_s, lslot_ref, tw_ref, x1_ref, yb_hbm, y_ref, buf, sem, *, tp, n_e, tile0):
    n_rows = _local_rows(tp, n_e)
    i = pl.program_id(0)
    slot = i % 2

    def runs(tile, s, op):
        _for_runs((cnt_s, tstart_s, gbase_s), tile0 + tile, n_e, tp,
                  lambda loc, glob, size: op(pltpu.make_async_copy(
                      yb_hbm.at[pl.ds(glob, size), :], buf.at[s, pl.ds(loc, size), :], sem.at[s])))

    @pl.when(i == 0)
    def _():
        buf[...] = jnp.zeros(buf.shape, F32)
        runs(i, slot, lambda cp: cp.start())

    @pl.when(i + 1 < pl.num_programs(0))
    def _():
        runs(i + 1, 1 - slot, lambda cp: cp.start())

    runs(i, slot, lambda cp: cp.wait())

    ptw = _slot_matrix(lslot_ref[...], tw_ref[...], n_rows)
    rows = buf[slot]
    w_hi = ptw.astype(BF16)
    w_lo = (ptw - w_hi.astype(F32)).astype(BF16)
    r_hi = rows.astype(BF16)
    r_lo = (rows - r_hi.astype(F32)).astype(BF16)
    y_ref[...] = x1_ref[...] + _dot(w_hi, r_hi) + (_dot(w_hi, r_lo) + _dot(w_lo, r_hi))


def _combine(tabs, tile0, lslot, tw, x1, yb, n_e):
    n, d = x1.shape
    tp = min(MOE_TP, n)
    n_rows = _local_rows(tp, n_e)
    kern = functools.partial(_combine_kernel, tp=tp, n_e=n_e, tile0=tile0)
    row = lambda w: pl.BlockSpec((tp, w), lambda i, *_: (i, 0))
    grid_spec = pltpu.PrefetchScalarGridSpec(
        num_scalar_prefetch=3, grid=(n // tp,),
        in_specs=[row(LANES), row(LANES), row(d), pl.BlockSpec(memory_space=pl.ANY)],
        out_specs=row(d),
        scratch_shapes=[pltpu.VMEM((2, n_rows, d), F32), pltpu.SemaphoreType.DMA((2,))])
    return pl.pallas_call(kern, out_shape=jax.ShapeDtypeStruct(x1.shape, F32), grid_spec=grid_spec,
                          compiler_params=_cparams(), name="combine")(*tabs, lslot, tw, x1, yb)


def _moe(groups, w_gate, b_gate, w_up, b_up, w_down, b_down):
    n_e = w_gate.shape[0]
    d = groups[0][0].shape[1]
    tp, bm = MOE_TP, MOE_BM
    sizes = [g[0].shape[0] for g in groups]
    assert all(n % tp == 0 or n < tp for n in sizes), sizes
    parts, tile0s, o = [], [], 0
    for g, n in zip(groups, sizes):
        n_pad = -(-n // tp) * tp
        parts.append(jnp.pad(g[2], ((0, n_pad - n), (0, 0)), constant_values=-1))
        tile0s.append(o // tp)
        o += n_pad
    lslot, cnt3, tstart3, gprev3, total = _moe_plan(jnp.concatenate(parts, axis=0))
    counts = total[0, :n_e]
    padded_cnt = (counts + bm - 1) // bm * bm
    pad_end = jnp.cumsum(padded_cnt)
    pad_start = pad_end - padded_cnt
    n_assign = sum(sizes) * TOP_K_MOE
    n_slack = (o // tp) * n_e * (SUBLANES - 1)
    nb = -(-(n_assign + n_slack) // bm) + n_e
    n_used = (pad_end[-1] // bm).astype(I32)
    blocks = jnp.arange(nb, dtype=I32)
    block_e = jnp.minimum((pad_end[None, :] <= (blocks * bm)[:, None]).sum(axis=1), n_e - 1).astype(I32)
    block_e = jnp.where(blocks < n_used, block_e, block_e[jnp.maximum(n_used - 1, 0)])
    flat = lambda a: a[:, 0, :n_e].reshape(-1).astype(I32)
    gbase3 = pad_start[None, None, :] + gprev3[:, :, :n_e]
    tabs = (flat(cnt3), flat(tstart3), flat(gbase3))

    xs = jnp.zeros((nb * bm, d), F32)
    lslots, o = [], 0
    for (x1, h2, te, tw), n, t0 in zip(groups, sizes, tile0s):
        ls = lslot[t0 * tp:t0 * tp + n]
        lslots.append(ls)
        xs = _dispatch(tabs, t0, ls, h2, xs, n_e)
    yb = _experts(xs, block_e, n_used.reshape(1), w_gate, b_gate, w_up, b_up, w_down, b_down)
    return [_combine(tabs, t0, ls, tw, x1, yb, n_e)
            for (x1, h2, te, tw), ls, t0 in zip(groups, lslots, tile0s)]


def _row_tile(n, want):
    return want if n % want == 0 else n


def kernel(x_prompt, x_sample, mem_prompt, cache_k, cache_v, cache_idx_k, cache_mem_k, cache_mem_v, page_table,
           g_in, w_in, g_qa, g_ka, g_vb, w_s, b_s, g_qm, g_mem, w_mem_kv, g_km, w_br_a, w_br_b, w_br_m, w_out,
           g_ffn, w_router, b_router, w_gate, b_gate, w_up, b_up, w_down, b_down):
    bp, t, d = x_prompt.shape
    bs, ts, _ = x_sample.shape
    assert bp == 1 and ts == 1 and t % CHUNK == 0
    n_mem = mem_prompt.shape[1]
    n_pages = page_table.shape[1]
    ps = cache_k.shape[1]
    past = n_pages * ps
    n_e = w_router.shape[1]

    fp = _prep_params(w_in, g_in, g_qa, g_ka)
    mp = _prep_mix_params(w_in, g_in, g_vb, g_qm, w_br_a, w_br_b, w_br_m, w_out, g_ffn, w_router, b_router)

    xp = x_prompt.reshape(t, d)
    qa_p, k32_p, khm_p, v32_p, vt_p, qi_p, ki32_p, kibf_p, wit_p = _front(
        xp, _rope_tables(jnp.arange(t, dtype=I32)), *fp, tm=_row_tile(t, 512))
    mem_k, mem_v = _memory_kv(mem_prompt.reshape(n_mem, d), g_mem.reshape(1, d), w_mem_kv.astype(BF16),
                              g_km.reshape(1, HEAD_DIM_M))
    out_a_p = _prompt_attention(qi_p, qa_p, wit_p, kibf_p, khm_p, vt_p)
    x1_p, h2_p, te_p, tw_p = _mix(xp, out_a_p, mp, _prep_gmlp_chunked(w_s, b_s), mem_k, mem_v,
                                  tm=_row_tile(t, 256), chunked=True, shared_mem=True, emit_vb=False, n_experts=n_e)

    xs = x_sample.reshape(bs, d)
    qa_s, k32_s, _, v32_s, _, qi_s, ki32_s, _, wit_s = _front(
        xs, _rope_tables(jnp.full((bs,), past, I32)), *fp, tm=bs)
    out_a_s = _sample_attention(page_table, jnp.moveaxis(qi_s, 0, 1), jnp.moveaxis(qa_s, 0, 1),
                                wit_s.T.reshape(bs, N_HEADS_IDX, 1), ki32_s.reshape(bs, 1, D_IDX),
                                k32_s.reshape(bs, 1, WA_KV), v32_s.reshape(bs, 1, WA_KV),
                                cache_idx_k, cache_k, cache_v)
    x1_s, h2_s, te_s, tw_s, vb_s = _mix(xs, out_a_s.reshape(bs, WA_Q), mp, _prep_gmlp_single(w_s, b_s),
                                        cache_mem_k.reshape(bs, n_mem, WM_Q), cache_mem_v.reshape(bs, n_mem, WM_Q),
                                        tm=_row_tile(bs, SUBLANES), chunked=False, shared_mem=False, emit_vb=True,
                                        n_experts=n_e)

    y_p, y_s = _moe([(x1_p, h2_p, te_p, tw_p), (x1_s, h2_s, te_s, tw_s)],
                    w_gate, b_gate, w_up, b_up, w_down, b_down)

    n_pg = t // ps
    return (y_p.reshape(1, t, d), y_s.reshape(bs, 1, d),
            k32_p.reshape(1, n_pg, ps, N_KV_A, HEAD_DIM_A), v32_p.reshape(1, n_pg, ps, N_KV_A, HEAD_DIM_A),
            ki32_p.reshape(1, n_pg, ps, D_IDX),
            mem_k.reshape(1, n_mem, N_HEADS_M, HEAD_DIM_M), mem_v.reshape(1, n_mem, N_HEADS_M, HEAD_DIM_M),
            k32_s.reshape(bs, 1, N_KV_A, HEAD_DIM_A), v32_s.reshape(bs, 1, N_KV_A, HEAD_DIM_A),
            ki32_s.reshape(bs, 1, D_IDX), vb_s.reshape(bs, 1, WIDTH_B))
```

```python
import functools

import jax
import jax.numpy as jnp
from jax import lax
from jax.experimental import pallas as pl
from jax.experimental.pallas import tpu as pltpu

F32 = jnp.float32
BF16 = jnp.bfloat16
I32 = jnp.int32

N_HEADS_A = 8
N_KV_A = 2
HEAD_DIM_A = 64
TOPK_MAX = 256
N_HEADS_IDX = 8
D_IDX = 64
N_GROUPS_B = 4
GROUP_DIM_B = 128
WIDTH_B = N_GROUPS_B * GROUP_DIM_B
CHUNK = 128
N_HEADS_M = 4
HEAD_DIM_M = 128
ROPE_THETA = 500000.0
ROT_FRACTION = 4
N_BRANCH = 3
TOP_K_MOE = 4
SWIGLU_LIMIT = 7.0
SWIGLU_ALPHA = 1.702
EPS = 1e-6

WA_Q = N_HEADS_A * HEAD_DIM_A
WA_KV = N_KV_A * HEAD_DIM_A
WI_Q = N_HEADS_IDX * D_IDX
WM_Q = N_HEADS_M * HEAD_DIM_M
GQA = N_HEADS_A // N_KV_A

LANES = 128
SUBLANES = 8
VMEM_LIMIT = 56 * 1024 * 1024

LOG2_E = 1.4426950408889634
INT_MIN = -(2 ** 31)
INT_MAX = 2 ** 31 - 1
NEG = -1e30

VT_ROWS = HEAD_DIM_A + 16
ATTN_BQ = 128
ATTN_SC = 512
MOE_BM = 512
FF_CHUNK = 256


def _cparams(n_axes=1, vmem=VMEM_LIMIT):
    return pltpu.CompilerParams(dimension_semantics=("arbitrary",) * n_axes, vmem_limit_bytes=vmem)


def _const_spec(shape, single=False):
    zeros = (0,) * len(shape)
    if single:
        return pl.BlockSpec(shape, lambda *_: zeros, pipeline_mode=pl.Buffered(1))
    return pl.BlockSpec(shape, lambda *_: zeros)


def _dot(a, b):
    return jnp.dot(a, b, preferred_element_type=F32)


def _dot_nt(a, b):
    return lax.dot_general(a, b, (((1,), (1,)), ((), ())), preferred_element_type=F32)


def _rms(x, g):
    return x * lax.rsqrt(jnp.mean(x * x, axis=-1, keepdims=True) + EPS) * g


def _split_dot(x, w):
    hi = x.astype(BF16)
    lo = (x - hi.astype(F32)).astype(BF16)
    return _dot(hi, w) + _dot(lo, w)


def _front_kernel(x_ref, gin_ref, wa_ref, wwit_ref, wvt_ref, bd_ref, gqa_ref, gka_ref, cos_ref, sa_ref, sb_ref,
                  qa_ref, k32_ref, khm_ref, v32_ref, vt_ref, qi_ref, ki32_ref, kibf_ref, wit_ref, *, score_scale):
    x = x_ref[...]
    h = _rms(x, gin_ref[...]).astype(BF16)
    p = _dot(h, wa_ref[...])
    cos, sa, sb = cos_ref[...], sa_ref[...], sb_ref[...]
    bd = bd_ref[...]

    def head_norm(v, g):
        ssq = _split_dot(v * v, bd)
        return v * lax.rsqrt(ssq * (1.0 / HEAD_DIM_A) + EPS) * g

    def rope(v):
        return v * cos + pltpu.roll(v, LANES - 8, 1) * sa + pltpu.roll(v, 8, 1) * sb

    for s in range(WA_Q // LANES):
        v = p[:, s * LANES:(s + 1) * LANES]
        v = rope(head_norm(v, gqa_ref[...])) * (HEAD_DIM_A ** -0.5 * LOG2_E)
        vb = v.astype(BF16)
        qa_ref[2 * s] = vb[:, :HEAD_DIM_A]
        qa_ref[2 * s + 1] = vb[:, HEAD_DIM_A:]
    o = WA_Q
    k = rope(head_norm(p[:, o:o + WA_KV], gka_ref[...]))
    k32_ref[...] = k
    kb = k.astype(BF16)
    khm_ref[0] = kb[:, :HEAD_DIM_A]
    khm_ref[1] = kb[:, HEAD_DIM_A:]
    o += WA_KV
    v32_ref[...] = p[:, o:o + WA_KV]
    o += WA_KV
    for s in range(WI_Q // LANES):
        vb = rope(p[:, o + s * LANES:o + (s + 1) * LANES]).astype(BF16)
        qi_ref[2 * s] = vb[:, :D_IDX]
        qi_ref[2 * s + 1] = vb[:, D_IDX:]
    o += WI_Q
    ki = rope(p[:, o:o + LANES])[:, :D_IDX]
    ki32_ref[...] = ki
    kibf_ref[...] = ki.astype(BF16)
    wit_ref[...] = _dot_nt(wwit_ref[...], h) * score_scale
    vt = _dot_nt(wvt_ref[...], h)
    tail = jnp.where(lax.broadcasted_iota(I32, (VT_ROWS - HEAD_DIM_A, vt.shape[1]), 0) == 0, 1.0, 0.0)
    for g in range(N_KV_A):
        vt_ref[g] = jnp.concatenate([vt[g * HEAD_DIM_A:(g + 1) * HEAD_DIM_A], tail], axis=0).astype(BF16)


def _front(x2d, tabs, gin, wa, wwit, wvt, bd, gqa, gka, tm):
    n, d = x2d.shape
    cos, sa, sb = tabs
    row = lambda w: pl.BlockSpec((tm, w), lambda i: (i, 0))
    hm = lambda nh, w: pl.BlockSpec((nh, tm, w), lambda i: (0, i, 0))
    out_shape = (
        jax.ShapeDtypeStruct((N_HEADS_A, n, HEAD_DIM_A), BF16),
        jax.ShapeDtypeStruct((n, WA_KV), F32),
        jax.ShapeDtypeStruct((N_KV_A, n, HEAD_DIM_A), BF16),
        jax.ShapeDtypeStruct((n, WA_KV), F32),
        jax.ShapeDtypeStruct((N_KV_A, VT_ROWS, n), BF16),
        jax.ShapeDtypeStruct((N_HEADS_IDX, n, D_IDX), BF16),
        jax.ShapeDtypeStruct((n, D_IDX), F32),
        jax.ShapeDtypeStruct((n, D_IDX), BF16),
        jax.ShapeDtypeStruct((N_HEADS_IDX, n), F32),
    )
    out_specs = (hm(N_HEADS_A, HEAD_DIM_A), row(WA_KV), hm(N_KV_A, HEAD_DIM_A), row(WA_KV),
                 pl.BlockSpec((N_KV_A, VT_ROWS, tm), lambda i: (0, 0, i)), hm(N_HEADS_IDX, D_IDX), row(D_IDX), row(D_IDX),
                 pl.BlockSpec((N_HEADS_IDX, tm), lambda i: (0, i)))
    in_specs = [row(d), _const_spec(gin.shape), _const_spec(wa.shape), _const_spec(wwit.shape),
                _const_spec(wvt.shape), _const_spec(bd.shape), _const_spec(gqa.shape), _const_spec(gka.shape),
                row(LANES), row(LANES), row(LANES)]
    kern = functools.partial(_front_kernel, score_scale=D_IDX ** -0.5 * N_HEADS_IDX ** -0.5)
    return pl.pallas_call(kern, out_shape=out_shape, grid=(n // tm,), in_specs=in_specs, out_specs=out_specs,
                          compiler_params=_cparams(), name="front")(x2d, gin, wa, wwit, wvt, bd, gqa, gka, cos, sa, sb)


def _memkv_kernel(mem_ref, gmem_ref, w_ref, gkm_ref, k32_ref, v32_ref):
    h = _rms(mem_ref[...], gmem_ref[...]).astype(BF16)
    kv = _dot(h, w_ref[...])
    for hh in range(N_HEADS_M):
        sl = slice(hh * HEAD_DIM_M, (hh + 1) * HEAD_DIM_M)
        k32_ref[:, sl] = _rms(kv[:, sl], gkm_ref[...])
    v32_ref[...] = kv[:, WM_Q:]


def _memory_kv(mem2d, gmem, w, gkm):
    n = mem2d.shape[0]
    out = jax.ShapeDtypeStruct((n, WM_Q), F32)
    return pl.pallas_call(_memkv_kernel, out_shape=(out, out), name="memory_kv",
                          compiler_params=pltpu.CompilerParams(vmem_limit_bytes=VMEM_LIMIT))(mem2d, gmem, w, gkm)


def _sort_key(s):
    s = jnp.where(s == 0.0, 0.0, s)
    bits = pltpu.bitcast(s, I32)
    return bits ^ ((bits >> 31) & INT_MAX)


I16 = jnp.int16
I16_MIN = -(2 ** 15)
ATTN_SUB = 128


def _attn_kernel(qi_ref, qa_ref, wi_ref, ki_ref, k_ref, vt_ref, out_ref,
                 keys_ref, hi_ref, lo_ref, cut_ref, bias_ref, lg_ref, cm_ref, m_ref, acc_ref, *, k_sel, idx_bits):
    bq, sc = ATTN_BQ, ATTN_SC
    i = pl.program_id(0)
    n_ch = (i * bq) // sc + 1
    t_pos = i * bq + lax.broadcasted_iota(I32, (1, bq), 1)
    sub = lax.broadcasted_iota(I32, (sc, 1), 0)
    qi = qi_ref[...].reshape(N_HEADS_IDX * bq, D_IDX)
    w = wi_ref[...]

    def chunk(c):
        return pl.ds(pl.multiple_of(c * sc, sc), sc)

    def score_chunk(c, carry):
        off = pl.multiple_of(c * sc, sc)
        d = _dot_nt(ki_ref[chunk(c), :], qi)
        s = jnp.zeros((sc, bq), F32)
        for h in range(N_HEADS_IDX):
            s = s + jnp.maximum(d[:, h * bq:(h + 1) * bq], 0.0) * w[h:h + 1, :]
        key = jnp.where(off + sub <= t_pos, _sort_key(s), INT_MIN)
        keys_ref[chunk(c), :] = key
        hi_ref[chunk(c), :] = (key >> 16).astype(I16)
        lo_ref[chunk(c), :] = ((key & 0xFFFF) + I16_MIN).astype(I16)
        return carry

    lax.fori_loop(0, n_ch, score_chunk, 0)

    def count(pred):
        def body(c, acc):
            hit = jnp.where(pred(keys_ref[chunk(c), :], c * sc + sub), 1, 0)
            return acc + hit.reshape(sc // SUBLANES, SUBLANES, bq).sum(axis=0)
        acc = lax.fori_loop(0, n_ch, body, jnp.zeros((SUBLANES, bq), I32))
        return acc.sum(axis=0, keepdims=True)

    def count16(ref, cands):
        c16 = [cand.astype(I16) for cand in cands]
        rows = 2 * SUBLANES

        def body(c, accs):
            blk = ref[chunk(c), :]
            out = []
            for cand, acc in zip(c16, accs):
                hit = jnp.where(blk >= cand, jnp.ones((), I16), jnp.zeros((), I16))
                parts = [hit[r * rows:(r + 1) * rows, :] for r in range(sc // rows)]
                while len(parts) > 1:
                    parts = [a + b for a, b in zip(parts[::2], parts[1::2])]
                out.append(acc + parts[0])
            return tuple(out)
        accs = lax.fori_loop(0, n_ch, body, tuple(jnp.zeros((rows, bq), I16) for _ in cands))
        return [acc.astype(I32).sum(axis=0, keepdims=True) for acc in accs]

    def search16(ref, want, frozen, cnt0):
        def body(j, st):
            thr, cnt = st
            step = jnp.int32(1) << (14 - 2 * j)
            c1, c2, c3 = count16(ref, [thr + step, thr + 2 * step, thr + 3 * step])
            live = jnp.logical_not(frozen)
            t1, t2, t3 = (jnp.logical_and(c >= want, live) for c in (c1, c2, c3))
            digit = jnp.where(t3, 3, jnp.where(t2, 2, jnp.where(t1, 1, 0)))
            cnt = jnp.where(t3, c3, jnp.where(t2, c2, jnp.where(t1, c1, cnt)))
            return thr + digit * step, cnt

        return lax.fori_loop(0, 8, body, (jnp.full((1, bq), I16_MIN, I32), cnt0))

    short = t_pos + 1 <= k_sel
    total = jnp.zeros((1, bq), I32) + n_ch * sc
    thr_hi, cnt_hi = search16(hi_ref, k_sel, short, total)
    above, = count16(hi_ref, [jnp.minimum(thr_hi + 1, -I16_MIN - 1)])
    above = jnp.where(thr_hi == -I16_MIN - 1, 0, above)
    settled = jnp.logical_or(cnt_hi == k_sel, short)
    thr_hi16 = thr_hi.astype(I16)

    def mask_low(c, carry):
        lo_ref[chunk(c), :] = jnp.where(hi_ref[chunk(c), :] == thr_hi16, lo_ref[chunk(c), :],
                                        jnp.full((), I16_MIN, I16))
        return carry

    lax.fori_loop(0, n_ch, mask_low, 0)
    thr_lo, cnt_lo = search16(lo_ref, k_sel - above, settled, cnt_hi - above)
    thr = (thr_hi << 16) + (thr_lo - I16_MIN)
    cnt = jnp.where(settled, cnt_hi, above + cnt_lo)
    cut_ref[...] = jnp.full((1, bq), INT_MAX, I32)

    excess = jnp.logical_and(cnt > k_sel, thr > INT_MIN)
    has_ties = jnp.max(jnp.where(excess, 1, 0)) > 0

    @pl.when(has_ties)
    def _():
        want = k_sel - count(lambda blk, _: blk > thr)

        def tie_body(j, cpos):
            cand = cpos + (jnp.int32(1) << (idx_bits - 1 - j))
            below = count(lambda blk, pos: jnp.where(blk == thr, pos, INT_MAX) < cand)
            return jnp.where(below < want, cand, cpos)

        cpos = lax.fori_loop(0, idx_bits, tie_body, jnp.zeros((1, bq), I32))
        cut_ref[...] = jnp.where(excess, cpos, INT_MAX)

    m_ref[...] = jnp.full(m_ref.shape, NEG, F32)
    acc_ref[...] = jnp.zeros(acc_ref.shape, F32)
    cut = cut_ref[...]
    floor = jnp.where(thr == INT_MIN, INT_MIN, thr - 1)
    n_sub = sc // ATTN_SUB

    qs = [qa_ref[g * GQA:(g + 1) * GQA].reshape(GQA * bq, HEAD_DIM_A) for g in range(N_KV_A)]
    neg_row = jnp.full((1, GQA * bq), NEG, F32)

    def stage(c, par, cb, par_b):
        do_a, do_b = c is not None, cb is not None
        if do_a:
            off = pl.multiple_of(c * sc, sc)
            keyc = keys_ref[chunk(c), :]

            @pl.when(has_ties)
            def _():
                tie = jnp.where(off + sub <= cut, 0.0, NEG)
                bias = jnp.where(keyc > thr, 0.0, jnp.where(keyc == thr, tie, NEG))
                bias_ref[...] = jnp.where(keyc == INT_MIN, NEG, bias)

            @pl.when(jnp.logical_not(has_ties))
            def _():
                bias_ref[...] = jnp.where(keyc > floor, 0.0, NEG)
            cmax = [neg_row] * N_KV_A
        if do_b:
            off_b = pl.multiple_of(cb * sc, sc)
            m_new, acc = [], []
            for g in range(N_KV_A):
                m_old = m_ref[g]
                m_new.append(jnp.maximum(m_old, cm_ref[par_b, g]))
                acc.append(jnp.exp2(m_old - m_new[g]) * acc_ref[g])
                m_ref[g] = m_new[g]
        for r in range(n_sub):
            rows = slice(r * ATTN_SUB, (r + 1) * ATTN_SUB)
            if do_a:
                bias4 = jnp.concatenate([bias_ref[rows, :]] * GQA, axis=1)
                for g in range(N_KV_A):
                    lg = _dot_nt(k_ref[g, pl.ds(off + r * ATTN_SUB, ATTN_SUB), :], qs[g]) + bias4
                    lg_ref[par, g, rows, :] = lg
                    cmax[g] = jnp.maximum(cmax[g], lg.max(axis=0, keepdims=True))
            if do_b:
                for g in range(N_KV_A):
                    p = jnp.exp2(lg_ref[par_b, g, rows, :] - m_new[g])
                    vt = vt_ref[g, :, pl.ds(off_b + r * ATTN_SUB, ATTN_SUB)]
                    acc[g] = acc[g] + _dot(vt, p.astype(BF16))
        if do_b:
            for g in range(N_KV_A):
                acc_ref[g] = acc[g]
        if do_a:
            for g in range(N_KV_A):
                cm_ref[par, g] = cmax[g]

    stage(jnp.int32(0), 0, None, None)

    def stage_pair(j, carry):
        stage(2 * j + 1, 1, 2 * j, 0)
        stage(2 * j + 2, 0, 2 * j + 1, 1)
        return carry

    last = n_ch - 1
    lax.fori_loop(0, last // 2, stage_pair, 0)

    @pl.when(last % 2 == 1)
    def _():
        stage(last, 1, last - 1, 0)
        stage(None, None, last, 1)

    @pl.when(last % 2 == 0)
    def _():
        stage(None, None, last, 0)

    heads = []
    for g in range(N_KV_A):
        acc = acc_ref[g]
        o = acc[:HEAD_DIM_A] / acc[HEAD_DIM_A:HEAD_DIM_A + 1]
        heads += [o[:, hh * bq:(hh + 1) * bq] for hh in range(GQA)]
    out_ref[...] = jnp.concatenate(heads, axis=0).T.astype(out_ref.dtype)


def _prompt_attention(qi_hm, qa_hm, wit, kibf, khm, vt):
    t = kibf.shape[0]
    bq = ATTN_BQ
    k_sel = min(TOPK_MAX, t // 4)
    kern = functools.partial(_attn_kernel, k_sel=k_sel, idx_bits=max(1, (t - 1).bit_length()))
    in_specs = [pl.BlockSpec((N_HEADS_IDX, bq, D_IDX), lambda i: (0, i, 0)),
                pl.BlockSpec((N_HEADS_A, bq, HEAD_DIM_A), lambda i: (0, i, 0)),
                pl.BlockSpec((N_HEADS_IDX, bq), lambda i: (0, i)),
                _const_spec(kibf.shape, True), _const_spec(khm.shape, True), _const_spec(vt.shape, True)]
    scratch = [pltpu.VMEM((t, bq), I32), pltpu.VMEM((t, bq), I16), pltpu.VMEM((t, bq), I16),
               pltpu.VMEM((1, bq), I32), pltpu.VMEM((ATTN_SC, bq), F32),
               pltpu.VMEM((2, N_KV_A, ATTN_SC, GQA * bq), F32), pltpu.VMEM((2, N_KV_A, 1, GQA * bq), F32),
               pltpu.VMEM((N_KV_A, 1, GQA * bq), F32),
               pltpu.VMEM((N_KV_A, VT_ROWS, GQA * bq), F32)]
    return pl.pallas_call(kern, out_shape=jax.ShapeDtypeStruct((t, WA_Q), BF16), grid=(t // bq,),
                          in_specs=in_specs, out_specs=pl.BlockSpec((bq, WA_Q), lambda i: (i, 0)),
                          scratch_shapes=scratch, compiler_params=_cparams(), name="attn")(
                              qi_hm, qa_hm, wit, kibf, khm, vt)


SAMPLE_CK = 2048


def _sample_attn_kernel(pt_ref, qi_ref, qa_ref, wi_ref, kin_ref, kn_ref, vn_ref, cidx_hbm, ck_hbm, cv_hbm, out_ref,
                        idx_buf, k_buf, v_buf, scr_ref, sems, *, n_pages, ps, ck, k_sel, idx_bits):
    b = pl.program_id(0)
    nb = pl.num_programs(0)
    past = n_pages * ps
    nc = past // ck

    def page_copy(which, bb, p):
        src, dst = ((cidx_hbm, idx_buf), (ck_hbm, k_buf), (cv_hbm, v_buf))[which]
        cols = pl.ds(pl.multiple_of(p * ps, ps), ps)
        dst = dst.at[:, cols] if which == 0 else dst.at[:, :, cols]
        return pltpu.make_async_copy(src.at[pt_ref[bb, p]], dst, sems.at[which])

    def for_pages(fn):
        lax.fori_loop(0, n_pages, lambda p, c: (fn(p), c)[1], 0)

    @pl.when(b == 0)
    def _():
        for_pages(lambda p: page_copy(0, b, p).start())

    for_pages(lambda p: (page_copy(1, b, p).start(), page_copy(2, b, p).start()))
    for_pages(lambda p: page_copy(0, b, p).wait())

    qi = qi_ref[0]
    w = wi_ref[0]
    for c in range(nc):
        d = _dot(qi, idx_buf[:, c * ck:(c + 1) * ck].astype(BF16))
        scr_ref[c:c + 1, :] = (jnp.maximum(d, 0.0) * w).sum(axis=0, keepdims=True)
    d_new = (qi.astype(F32) * kin_ref[0].astype(BF16).astype(F32)).sum(axis=-1, keepdims=True)
    s_new = (jnp.maximum(d_new, 0.0) * w).sum(axis=0, keepdims=True)

    @pl.when(b + 1 < nb)
    def _():
        for_pages(lambda p: page_copy(0, b + 1, p).start())

    keys = _sort_key(scr_ref[...])
    key_new = _sort_key(s_new)
    pos = (lax.broadcasted_iota(I32, keys.shape, 0) * ck + lax.broadcasted_iota(I32, keys.shape, 1))

    def count(pred):
        hit = jnp.where(pred(keys, pos), 1, 0).sum(axis=1, keepdims=True).sum(axis=0, keepdims=True)
        return hit + jnp.where(pred(key_new, past), 1, 0)

    def search_cond(st):
        bit, _, cnt = st
        return jnp.logical_and(bit >= 0, jnp.max(jnp.where(cnt == k_sel, 0, 1)) > 0)

    def search_body(st):
        bit, thr, cnt = st
        cand = thr + (jnp.int32(1) << bit)
        c = count(lambda kk, _: kk >= cand)
        take = c >= k_sel
        return bit - 1, jnp.where(take, cand, thr), jnp.where(take, c, cnt)

    st0 = (jnp.int32(31), jnp.full((1, 1), INT_MIN, I32), jnp.full((1, 1), past + 1, I32))
    _, thr, cnt = lax.while_loop(search_cond, search_body, st0)

    want = k_sel - count(lambda kk, _: kk > thr)

    def tie_body(j, cpos):
        cand = cpos + (jnp.int32(1) << (idx_bits - 1 - j))
        below = count(lambda kk, pp: jnp.where(kk == thr, pp, INT_MAX) < cand)
        return jnp.where(below < want, cand, cpos)

    cpos = lax.fori_loop(0, idx_bits, tie_body, jnp.zeros((1, 1), I32))
    cut = jnp.where(cnt > k_sel, cpos, INT_MAX)

    def sel_bias(kk, pp):
        tie = jnp.where(pp <= cut, 0.0, NEG)
        return jnp.where(kk > thr, 0.0, jnp.where(kk == thr, tie, NEG))

    bias = sel_bias(keys, pos)
    bias_new = sel_bias(key_new, past)

    for_pages(lambda p: (page_copy(1, b, p).wait(), page_copy(2, b, p).wait()))

    qa = qa_ref[0]
    heads = []
    for g in range(N_KV_A):
        qg = qa[g * GQA:(g + 1) * GQA]
        cols = slice(g * HEAD_DIM_A, (g + 1) * HEAD_DIM_A)
        m = jnp.full((GQA, 1), NEG, F32)
        l = jnp.zeros((GQA, 1), F32)
        acc = jnp.zeros((GQA, HEAD_DIM_A), F32)
        for c in range(nc):
            span = slice(c * ck, (c + 1) * ck)
            lg = _dot(qg, k_buf[g, :, span].astype(BF16)) + bias[c:c + 1, :]
            m_new = jnp.maximum(m, lg.max(axis=-1, keepdims=True))
            p = jnp.exp2(lg - m_new)
            alpha = jnp.exp2(m - m_new)
            l = alpha * l + p.sum(axis=-1, keepdims=True)
            acc = alpha * acc + _dot_nt(p.astype(BF16), v_buf[g, :, span].astype(BF16))
            m = m_new
        kn = kn_ref[0][:, cols].astype(BF16).astype(F32)
        vn = vn_ref[0][:, cols].astype(BF16).astype(F32)
        lg = (qg.astype(F32) * kn).sum(axis=-1, keepdims=True) + bias_new
        m_new = jnp.maximum(m, lg)
        p = jnp.exp2(lg - m_new)
        alpha = jnp.exp2(m - m_new)
        l = alpha * l + p
        acc = alpha * acc + p * vn
        o = acc / l
        heads += [o[hh:hh + 1, :] for hh in range(GQA)]
    out_ref[0] = jnp.concatenate(heads, axis=1)


def _sample_attention(page_table, qi_s, qa_s, wi_s, ki_new, k_new, v_new, cache_idx_k, cache_k, cache_v):
    nbatch, n_pages = page_table.shape
    n_pool, ps, d_idx = cache_idx_k.shape
    past = n_pages * ps
    ck = min(SAMPLE_CK, past)
    assert past % ck == 0 and ck % ps == 0
    k_sel = min(TOPK_MAX, (past + 1) // 4)
    kern = functools.partial(_sample_attn_kernel, n_pages=n_pages, ps=ps, ck=ck, k_sel=k_sel,
                             idx_bits=past.bit_length())
    per_b = lambda shape: pl.BlockSpec((1,) + shape, lambda b, pt: (b, 0, 0))
    any_spec = pl.BlockSpec(memory_space=pl.ANY)
    grid_spec = pltpu.PrefetchScalarGridSpec(
        num_scalar_prefetch=1, grid=(nbatch,),
        in_specs=[per_b((N_HEADS_IDX, d_idx)), per_b((N_HEADS_A, HEAD_DIM_A)), per_b((N_HEADS_IDX, 1)),
                  per_b((1, d_idx)), per_b((1, WA_KV)), per_b((1, WA_KV)), any_spec, any_spec, any_spec],
        out_specs=per_b((1, WA_Q)),
        scratch_shapes=[pltpu.VMEM((d_idx, past), F32), pltpu.VMEM((N_KV_A, HEAD_DIM_A, past), F32),
                        pltpu.VMEM((N_KV_A, HEAD_DIM_A, past), F32),
                        pltpu.VMEM((past // ck, ck), F32), pltpu.SemaphoreType.DMA((3,))])
    return pl.pallas_call(kern, out_shape=jax.ShapeDtypeStruct((nbatch, 1, WA_Q), F32), grid_spec=grid_spec,
                          compiler_params=_cparams(), name="sample_attn")(
                              page_table, qi_s, qa_s, wi_s, ki_new, k_new, v_new,
                              jnp.transpose(cache_idx_k, (0, 2, 1)), jnp.transpose(cache_k, (0, 2, 3, 1)),
                              jnp.transpose(cache_v, (0, 2, 3, 1)))


def _rope_tables(pos):
    rot = HEAD_DIM_A // ROT_FRACTION
    half = rot // 2
    inv_freq = ROPE_THETA ** (-jnp.arange(half, dtype=F32) / half)
    ang = pos.astype(F32)[:, None] * inv_freq[None, :]
    cos, sin = jnp.cos(ang), jnp.sin(ang)
    n = pos.shape[0]
    z = lambda w: jnp.zeros((n, w), F32)
    c = jnp.concatenate([cos, cos, jnp.ones((n, HEAD_DIM_A - rot), F32)], axis=1)
    sa = jnp.concatenate([-sin, z(HEAD_DIM_A - half)], axis=1)
    sb = jnp.concatenate([z(half), sin, z(HEAD_DIM_A - rot)], axis=1)
    rep = LANES // HEAD_DIM_A
    return tuple(jnp.tile(a, (1, rep)) for a in (c, sa, sb))


def _prep_params(w_in, g_in, g_qa, g_ka):
    d = w_in.shape[0]
    w_qa, w_ka, w_va, w_qi, w_ki, w_wi = _split_w_in(w_in, d)[:6]
    wa = jnp.concatenate([w_qa, w_ka, w_va, w_qi, w_ki, jnp.zeros((d, LANES - D_IDX), w_in.dtype)], axis=1)
    lane = jnp.arange(LANES)
    bd = (lane[:, None] // HEAD_DIM_A == lane[None, :] // HEAD_DIM_A).astype(BF16)
    rep = LANES // HEAD_DIM_A
    return (g_in.reshape(1, d), wa.astype(BF16), w_wi.T.astype(BF16), w_va.T.astype(BF16), bd,
            jnp.tile(g_qa, rep).reshape(1, LANES), jnp.tile(g_ka, rep).reshape(1, LANES))


def _split_w_in(w_in, d):
    sizes = (WA_Q, WA_KV, WA_KV, WI_Q, D_IDX, N_HEADS_IDX, WIDTH_B, WIDTH_B, WM_Q, N_BRANCH * d)
    parts, o = [], 0
    for s in sizes:
        parts.append(w_in[:, o:o + s])
        o += s
    return parts


def _mix_kernel(*refs, chunked, shared_mem, emit_vb, n_experts):
    it = iter(refs)
    x_ref, oa_ref, gin_ref, wb_ref, gvb_ref, gqm_ref = (next(it) for _ in range(6))
    if chunked:
        wtril_ref, bt_ref = next(it), next(it)
    else:
        wdiag_ref, bdiag_ref = next(it), next(it)
    mk_ref, mv_ref = next(it), next(it)
    wbra_ref, wbrb_ref, wbrm_ref, wout_ref, gffn_ref, wrh_ref, wrl_ref, br_ref = (next(it) for _ in range(8))
    x1_ref, h2_ref, te_ref, tw_ref = (next(it) for _ in range(4))
    vb_ref = next(it) if emit_vb else None

    x = x_ref[...]
    tm, d = x.shape
    h = _rms(x, gin_ref[...]).astype(BF16)
    p = _dot(h, wb_ref[...])
    ub = p[:, :WIDTH_B]
    vb = _rms(p[:, WIDTH_B:2 * WIDTH_B], gvb_ref[...])
    if emit_vb:
        vb_ref[...] = vb
    o = 2 * WIDTH_B
    qm = p[:, o:o + WM_Q]
    o += WM_Q
    gates = p[:, o:o + N_BRANCH * d]

    if chunked:
        vbb = vb.astype(BF16)
        rows = []
        for cc in range(tm // CHUNK):
            cols = []
            for g in range(N_GROUPS_B):
                vg = vbb[cc * CHUNK:(cc + 1) * CHUNK, g * GROUP_DIM_B:(g + 1) * GROUP_DIM_B]
                cols.append(_dot(wtril_ref[g], vg))
            rows.append(jnp.concatenate(cols, axis=1) + bt_ref[...])
        z = jnp.concatenate(rows, axis=0) if len(rows) > 1 else rows[0]
    else:
        z = vb * wdiag_ref[...] + bdiag_ref[...]
    out_b = ub * z

    scale_m = HEAD_DIM_M ** -0.5
    outs = []
    for hh in range(N_HEADS_M):
        sl = slice(hh * HEAD_DIM_M, (hh + 1) * HEAD_DIM_M)
        qh = _rms(qm[:, sl], gqm_ref[...])
        if shared_mem:
            lg = _dot_nt(qh.astype(BF16), mk_ref[:, sl].astype(BF16)) * scale_m
            pm = jnp.exp(lg - lg.max(axis=-1, keepdims=True))
            pm = pm / pm.sum(axis=-1, keepdims=True)
            outs.append(_dot(pm.astype(BF16), mv_ref[:, sl].astype(BF16)))
        else:
            per_row = []
            for r in range(tm):
                kr = mk_ref[r, :, sl]
                lg = (kr * qh[r:r + 1, :]).sum(axis=-1, keepdims=True) * scale_m
                pm = jnp.exp(lg - lg.max(axis=0, keepdims=True))
                pm = pm / pm.sum(axis=0, keepdims=True)
                per_row.append((pm * mv_ref[r, :, sl]).sum(axis=0, keepdims=True))
            outs.append(jnp.concatenate(per_row, axis=0))
    out_m = jnp.concatenate(outs, axis=1)

    sig = jax.nn.sigmoid
    merged = (sig(gates[:, :d]) * _dot(oa_ref[...].astype(BF16), wbra_ref[...])
              + sig(gates[:, d:2 * d]) * _dot(out_b.astype(BF16), wbrb_ref[...])
              + sig(gates[:, 2 * d:]) * _dot(out_m.astype(BF16), wbrm_ref[...]))
    x1 = x + _dot(merged.astype(BF16), wout_ref[...])
    x1_ref[...] = x1
    h2 = _rms(x1, gffn_ref[...])
    h2_ref[...] = h2.astype(h2_ref.dtype)

    hi = h2.astype(BF16)
    lo = (h2 - hi.astype(F32)).astype(BF16)
    lg = _dot(hi, wrh_ref[...]) + _dot(hi, wrl_ref[...]) + _dot(lo, wrh_ref[...]) + br_ref[...]
    lane = lax.broadcasted_iota(I32, lg.shape, 1)
    lg = jnp.where(lane < n_experts, lg, -jnp.inf)
    vals, idxs = [], []
    for _ in range(TOP_K_MOE):
        m = lg.max(axis=-1, keepdims=True)
        idx = jnp.where(lg == m, lane, LANES).min(axis=-1, keepdims=True)
        vals.append(m)
        idxs.append(idx)
        lg = jnp.where(lane == idx, -jnp.inf, lg)
    ex = [jnp.exp(v - vals[0]) for v in vals]
    den = ex[0]
    for e in ex[1:]:
        den = den + e
    te = jnp.full(lane.shape, -1, I32)
    tw = jnp.zeros(lane.shape, F32)
    for j in range(TOP_K_MOE):
        te = jnp.where(lane == j, idxs[j], te)
        tw = jnp.where(lane == j, ex[j] / den, tw)
    te_ref[...] = te
    tw_ref[...] = tw


def _mix(x2d, out_a, mix_params, gmlp_params, mem_k, mem_v, *, tm, chunked, shared_mem, emit_vb, n_experts):
    n, d = x2d.shape
    gin, wb, gvb, gqm, wbra, wbrb, wbrm, wout, gffn, wrh, wrl, br = mix_params
    row = lambda w: pl.BlockSpec((tm, w), lambda i: (i, 0))
    if shared_mem:
        mem_specs = [_const_spec(mem_k.shape), _const_spec(mem_v.shape)]
    else:
        mspec = pl.BlockSpec((tm,) + mem_k.shape[1:], lambda i: (i, 0, 0))
        mem_specs = [mspec, mspec]
    consts = lambda arrs: [_const_spec(a.shape) for a in arrs]
    in_specs = ([row(d), row(WA_Q)] + consts([gin, wb, gvb, gqm]) + consts(gmlp_params) + mem_specs
                + consts([wbra, wbrb, wbrm, wout, gffn, wrh, wrl, br]))
    out_shape = [jax.ShapeDtypeStruct((n, d), F32), jax.ShapeDtypeStruct((n, d), F32),
                 jax.ShapeDtypeStruct((n, LANES), I32), jax.ShapeDtypeStruct((n, LANES), F32)]
    out_specs = [row(d), row(d), row(LANES), row(LANES)]
    if emit_vb:
        out_shape.append(jax.ShapeDtypeStruct((n, WIDTH_B), F32))
        out_specs.append(row(WIDTH_B))
    kern = functools.partial(_mix_kernel, chunked=chunked, shared_mem=shared_mem, emit_vb=emit_vb,
                             n_experts=n_experts)
    return pl.pallas_call(kern, out_shape=tuple(out_shape), grid=(n // tm,), in_specs=in_specs,
                          out_specs=tuple(out_specs), compiler_params=_cparams(), name="mix")(
                              x2d, out_a, gin, wb, gvb, gqm, *gmlp_params, mem_k, mem_v,
                              wbra, wbrb, wbrm, wout, gffn, wrh, wrl, br)


def _prep_mix_params(w_in, g_in, g_vb, g_qm, w_br_a, w_br_b, w_br_m, w_out, g_ffn, w_router, b_router):
    d = w_in.shape[0]
    parts = _split_w_in(w_in, d)
    wb = jnp.concatenate(parts[6:10], axis=1).astype(BF16)
    n_e = w_router.shape[1]
    wr = jnp.pad(w_router, ((0, 0), (0, LANES - n_e)))
    wrh = wr.astype(BF16)
    wrl = (wr - wrh.astype(F32)).astype(BF16)
    br = jnp.pad(b_router, (0, LANES - n_e)).reshape(1, LANES)
    return (g_in.reshape(1, d), wb, g_vb.reshape(1, WIDTH_B), g_qm.reshape(1, HEAD_DIM_M),
            w_br_a.astype(BF16), w_br_b.astype(BF16), w_br_m.astype(BF16), w_out.astype(BF16),
            g_ffn.reshape(1, d), wrh, wrl, br)


def _prep_gmlp_chunked(w_s, b_s):
    tril = jnp.tril(jnp.ones((CHUNK, CHUNK), w_s.dtype))
    wtril = (w_s * tril).astype(BF16)
    bt = jnp.repeat(b_s.T, GROUP_DIM_B, axis=1)
    return wtril, bt


def _prep_gmlp_single(w_s, b_s):
    wdiag = jnp.repeat(w_s[:, 0, 0], GROUP_DIM_B).reshape(1, WIDTH_B)
    bdiag = jnp.repeat(b_s[:, 0], GROUP_DIM_B).reshape(1, WIDTH_B)
    return wdiag, bdiag


MOE_TP = 256


def _plan_kernel(te_ref, tri_ref, upper_ref, lslot_ref, cnt_ref, tstart_ref, gprev_ref, total_ref, carry_ref):
    @pl.when(pl.program_id(0) == 0)
    def _():
        carry_ref[...] = jnp.zeros(carry_ref.shape, F32)

    te = te_ref[...]
    lane = lax.broadcasted_iota(I32, te.shape, 1)
    picks = [te[:, j:j + 1] for j in range(TOP_K_MOE)]
    onehot = jnp.zeros(te.shape, F32)
    for e in picks:
        onehot = onehot + jnp.where(lane == e, 1.0, 0.0)
    before = _dot(tri_ref[...], onehot.astype(BF16))
    cnt = onehot.sum(axis=0, keepdims=True)
    cnt = jnp.floor((cnt + (SUBLANES - 1)) * (1.0 / SUBLANES)) * SUBLANES
    cnt8 = jnp.broadcast_to(cnt, (SUBLANES, LANES)).astype(BF16)
    tstart = _dot(cnt8, upper_ref[...])[0:1, :]
    lslot = jnp.zeros(te.shape, I32)
    for j, e in enumerate(picks):
        r = jnp.where(lane == e, before + tstart, 0.0).sum(axis=-1, keepdims=True).astype(I32)
        lslot = jnp.where(lane == j, r, lslot)
    lslot_ref[...] = lslot
    cnt_ref[0] = cnt.astype(I32)
    tstart_ref[0] = tstart.astype(I32)
    gprev_ref[0] = carry_ref[...].astype(I32)
    carry_ref[...] = carry_ref[...] + cnt
    total_ref[...] = carry_ref[...].astype(I32)


def _moe_plan(te_all):
    n = te_all.shape[0]
    tp = MOE_TP
    n_tiles = n // tp
    r = jnp.arange(tp)
    tri = (r[:, None] > r[None, :]).astype(BF16)
    e = jnp.arange(LANES)
    upper = (e[:, None] < e[None, :]).astype(BF16)
    per_tile = jax.ShapeDtypeStruct((n_tiles, 1, LANES), I32)
    tile_spec = pl.BlockSpec((1, 1, LANES), lambda i: (i, 0, 0))
    return pl.pallas_call(
        _plan_kernel,
        out_shape=(jax.ShapeDtypeStruct((n, LANES), I32), per_tile, per_tile, per_tile,
                   jax.ShapeDtypeStruct((1, LANES), I32)),
        grid=(n_tiles,),
        in_specs=[pl.BlockSpec((tp, LANES), lambda i: (i, 0)), _const_spec((tp, tp)), _const_spec((LANES, LANES))],
        out_specs=(pl.BlockSpec((tp, LANES), lambda i: (i, 0)), tile_spec, tile_spec, tile_spec,
                   _const_spec((1, LANES))),
        scratch_shapes=[pltpu.VMEM((1, LANES), F32)],
        compiler_params=_cparams(), name="moe_plan")(te_all, tri, upper)


def _run_pieces(tp):
    sizes, s = [], SUBLANES
    while s <= tp:
        sizes.append(s)
        s *= 2
    return tuple(reversed(sizes))


def _local_rows(tp, n_e):
    return tp * TOP_K_MOE + n_e * SUBLANES


def _for_runs(tabs, tile, n_e, tp, fn):
    cnt_s, tstart_s, gbase_s = tabs

    def per_expert(e, c):
        idx = tile * n_e + e
        length, src0, dst0 = cnt_s[idx], tstart_s[idx], gbase_s[idx]
        off = jnp.int32(0)
        for size in _run_pieces(tp):
            piece = length & size

            @pl.when(piece != 0)
            def _():
                fn(pl.multiple_of(src0 + off, SUBLANES), pl.multiple_of(dst0 + off, SUBLANES), size)
            off = off + piece
        return c

    lax.fori_loop(0, n_e, per_expert, 0)


def _slot_matrix(lslot, vals, n_rows):
    col = lax.broadcasted_iota(I32, (lslot.shape[0], n_rows), 1)
    m = jnp.zeros(col.shape, F32)
    for j in range(TOP_K_MOE):
        v = 1.0 if vals is None else vals[:, j:j + 1]
        m = m + jnp.where(lslot[:, j:j + 1] == col, v, 0.0)
    return m


def _dispatch_kernel(cnt_s, tstart_s, gbase_s, lslot_ref, h2_ref, xs_in, xs_out, stage, sem, *, tp, n_e, tile0):
    del xs_in
    n_rows = _local_rows(tp, n_e)
    i = pl.program_id(0)
    slot = i % 2
    pt = _slot_matrix(lslot_ref[...], None, n_rows).astype(BF16)
    stage[slot] = lax.dot_general(pt, h2_ref[...].astype(BF16), (((0,), (0,)), ((), ())),
                                  preferred_element_type=F32)

    def runs(tile, s, op):
        _for_runs((cnt_s, tstart_s, gbase_s), tile0 + tile, n_e, tp,
                  lambda src, dst, size: op(pltpu.make_async_copy(
                      stage.at[s, pl.ds(src, size), :], xs_out.at[pl.ds(dst, size), :], sem.at[s])))

    @pl.when(i > 0)
    def _():
        runs(i - 1, 1 - slot, lambda cp: cp.wait())

    runs(i, slot, lambda cp: cp.start())

    @pl.when(i == pl.num_programs(0) - 1)
    def _():
        runs(i, slot, lambda cp: cp.wait())


def _dispatch(tabs, tile0, lslot, h2, xs, n_e):
    n, d = h2.shape
    tp = min(MOE_TP, n)
    n_rows = _local_rows(tp, n_e)
    kern = functools.partial(_dispatch_kernel, tp=tp, n_e=n_e, tile0=tile0)
    any_spec = pl.BlockSpec(memory_space=pl.ANY)
    grid_spec = pltpu.PrefetchScalarGridSpec(
        num_scalar_prefetch=3, grid=(n // tp,),
        in_specs=[pl.BlockSpec((tp, LANES), lambda i, *_: (i, 0)), pl.BlockSpec((tp, d), lambda i, *_: (i, 0)), any_spec],
        out_specs=any_spec,
        scratch_shapes=[pltpu.VMEM((2, n_rows, d), F32), pltpu.SemaphoreType.DMA((2,))])
    return pl.pallas_call(
        kern, out_shape=jax.ShapeDtypeStruct(xs.shape, xs.dtype), grid_spec=grid_spec,
        input_output_aliases={5: 0},
        compiler_params=pltpu.CompilerParams(dimension_semantics=("arbitrary",), has_side_effects=True,
                                             vmem_limit_bytes=VMEM_LIMIT),
        name="dispatch")(*tabs, lslot, h2, xs)


def _expert_kernel(be_ref, nu_ref, xs_ref, wg_ref, wu_ref, wd_ref, bg_ref, bu_ref, bd_ref, y_ref,
                   wg_bf, wu_bf, wd_bf):
    b = pl.program_id(0)

    @pl.when(b >= nu_ref[0])
    def _():
        y_ref[...] = jnp.zeros(y_ref.shape, F32)

    @pl.when(b < nu_ref[0])
    def _():
        prev = be_ref[jnp.maximum(b - 1, 0)]

        @pl.when(jnp.logical_or(b == 0, be_ref[b] != prev))
        def _():
            wg_bf[...] = wg_ref[0].astype(BF16)
            wu_bf[...] = wu_ref[0].astype(BF16)
            wd_bf[...] = wd_ref[0].astype(BF16)

        xb = xs_ref[...].astype(BF16)
        ff = wg_bf.shape[1]
        y = jnp.zeros(y_ref.shape, F32) + bd_ref[0]
        for n in range(ff // FF_CHUNK):
            sl = slice(n * FF_CHUNK, (n + 1) * FF_CHUNK)
            hg = jnp.minimum(_dot(xb, wg_bf[:, sl]) + bg_ref[0][:, sl], SWIGLU_LIMIT)
            hu = jnp.clip(_dot(xb, wu_bf[:, sl]) + bu_ref[0][:, sl], -SWIGLU_LIMIT, SWIGLU_LIMIT)
            act = hg * jax.nn.sigmoid(SWIGLU_ALPHA * hg) * (hu + 1.0)
            y = y + _dot(act.astype(BF16), wd_bf[sl, :])
        y_ref[...] = y


def _experts(xs, block_e, n_used, w_gate, b_gate, w_up, b_up, w_down, b_down):
    ns, d = xs.shape
    n_e, _, ff = w_gate.shape
    bm = MOE_BM
    blk = lambda b, be, nu: (jnp.minimum(b, nu[0] - 1), 0)
    wsel = lambda b, be, nu: (be[b], 0, 0)
    grid_spec = pltpu.PrefetchScalarGridSpec(
        num_scalar_prefetch=2, grid=(ns // bm,),
        in_specs=[pl.BlockSpec((bm, d), blk),
                  pl.BlockSpec((1, d, ff), wsel), pl.BlockSpec((1, d, ff), wsel), pl.BlockSpec((1, ff, d), wsel),
                  pl.BlockSpec((1, 1, ff), wsel), pl.BlockSpec((1, 1, ff), wsel), pl.BlockSpec((1, 1, d), wsel)],
        out_specs=pl.BlockSpec((bm, d), lambda b, be, nu: (b, 0)),
        scratch_shapes=[pltpu.VMEM((d, ff), BF16), pltpu.VMEM((d, ff), BF16), pltpu.VMEM((ff, d), BF16)])
    return pl.pallas_call(_expert_kernel, out_shape=jax.ShapeDtypeStruct((ns, d), F32), grid_spec=grid_spec,
                          compiler_params=_cparams(), name="experts")(
                              block_e, n_used, xs, w_gate, w_up, w_down,
                              b_gate.reshape(n_e, 1, ff), b_up.reshape(n_e, 1, ff), b_down.reshape(n_e, 1, d))


def _combine_kernel(cnt_s, tstart_s, gbase_s, lslot_ref, tw_ref, x1_ref, yb_hbm, y_ref, buf, sem, *, tp, n_e, tile0):
    n_rows = _local_rows(tp, n_e)
    i = pl.program_id(0)
    slot = i % 2

    def runs(tile, s, op):
        _for_runs((cnt_s, tstart_s, gbase_s), tile0 + tile, n_e, tp,
                  lambda loc, glob, size: op(pltpu.make_async_copy(
                      yb_hbm.at[pl.ds(glob, size), :], buf.at[s, pl.ds(loc, size), :], sem.at[s])))

    @pl.when(i == 0)
    def _():
        buf[...] = jnp.zeros(buf.shape, F32)
        runs(i, slot, lambda cp: cp.start())

    @pl.when(i + 1 < pl.num_programs(0))
    def _():
        runs(i + 1, 1 - slot, lambda cp: cp.start())

    runs(i, slot, lambda cp: cp.wait())

    ptw = _slot_matrix(lslot_ref[...], tw_ref[...], n_rows)
    rows = buf[slot]
    w_hi = ptw.astype(BF16)
    w_lo = (ptw - w_hi.astype(F32)).astype(BF16)
    r_hi = rows.astype(BF16)
    r_lo = (rows - r_hi.astype(F32)).astype(BF16)
    y_ref[...] = x1_ref[...] + _dot(w_hi, r_hi) + (_dot(w_hi, r_lo) + _dot(w_lo, r_hi))


def _combine(tabs, tile0, lslot, tw, x1, yb, n_e):
    n, d = x1.shape
    tp = min(MOE_TP, n)
    n_rows = _local_rows(tp, n_e)
    kern = functools.partial(_combine_kernel, tp=tp, n_e=n_e, tile0=tile0)
    row = lambda w: pl.BlockSpec((tp, w), lambda i, *_: (i, 0))
    grid_spec = pltpu.PrefetchScalarGridSpec(
        num_scalar_prefetch=3, grid=(n // tp,),
        in_specs=[row(LANES), row(LANES), row(d), pl.BlockSpec(memory_space=pl.ANY)],
        out_specs=row(d),
        scratch_shapes=[pltpu.VMEM((2, n_rows, d), F32), pltpu.SemaphoreType.DMA((2,))])
    return pl.pallas_call(kern, out_shape=jax.ShapeDtypeStruct(x1.shape, F32), grid_spec=grid_spec,
                          compiler_params=_cparams(), name="combine")(*tabs, lslot, tw, x1, yb)


def _moe(groups, w_gate, b_gate, w_up, b_up, w_down, b_down):
    n_e = w_gate.shape[0]
    d = groups[0][0].shape[1]
    tp, bm = MOE_TP, MOE_BM
    sizes = [g[0].shape[0] for g in groups]
    assert all(n % tp == 0 or n < tp for n in sizes), sizes
    parts, tile0s, o = [], [], 0
    for g, n in zip(groups, sizes):
        n_pad = -(-n // tp) * tp
        parts.append(jnp.pad(g[2], ((0, n_pad - n), (0, 0)), constant_values=-1))
        tile0s.append(o // tp)
        o += n_pad
    lslot, cnt3, tstart3, gprev3, total = _moe_plan(jnp.concatenate(parts, axis=0))
    counts = total[0, :n_e]
    padded_cnt = (counts + bm - 1) // bm * bm
    pad_end = jnp.cumsum(padded_cnt)
    pad_start = pad_end - padded_cnt
    n_assign = sum(sizes) * TOP_K_MOE
    n_slack = (o // tp) * n_e * (SUBLANES - 1)
    nb = -(-(n_assign + n_slack) // bm) + n_e
    n_used = (pad_end[-1] // bm).astype(I32)
    blocks = jnp.arange(nb, dtype=I32)
    block_e = jnp.minimum((pad_end[None, :] <= (blocks * bm)[:, None]).sum(axis=1), n_e - 1).astype(I32)
    block_e = jnp.where(blocks < n_used, block_e, block_e[jnp.maximum(n_used - 1, 0)])
    flat = lambda a: a[:, 0, :n_e].reshape(-1).astype(I32)
    gbase3 = pad_start[None, None, :] + gprev3[:, :, :n_e]
    tabs = (flat(cnt3), flat(tstart3), flat(gbase3))

    xs = jnp.zeros((nb * bm, d), F32)
    lslots, o = [], 0
    for (x1, h2, te, tw), n, t0 in zip(groups, sizes, tile0s):
        ls = lslot[t0 * tp:t0 * tp + n]
        lslots.append(ls)
        xs = _dispatch(tabs, t0, ls, h2, xs, n_e)
    yb = _experts(xs, block_e, n_used.reshape(1), w_gate, b_gate, w_up, b_up, w_down, b_down)
    return [_combine(tabs, t0, ls, tw, x1, yb, n_e)
            for (x1, h2, te, tw), ls, t0 in zip(groups, lslots, tile0s)]


def _row_tile(n, want):
    return want if n % want == 0 else n


def kernel(x_prompt, x_sample, mem_prompt, cache_k, cache_v, cache_idx_k, cache_mem_k, cache_mem_v, page_table,
           g_in, w_in, g_qa, g_ka, g_vb, w_s, b_s, g_qm, g_mem, w_mem_kv, g_km, w_br_a, w_br_b, w_br_m, w_out,
           g_ffn, w_router, b_router, w_gate, b_gate, w_up, b_up, w_down, b_down):
    bp, t, d = x_prompt.shape
    bs, ts, _ = x_sample.shape
    assert bp == 1 and ts == 1 and t % CHUNK == 0
    n_mem = mem_prompt.shape[1]
    n_pages = page_table.shape[1]
    ps = cache_k.shape[1]
    past = n_pages * ps
    n_e = w_router.shape[1]

    fp = _prep_params(w_in, g_in, g_qa, g_ka)
    mp = _prep_mix_params(w_in, g_in, g_vb, g_qm, w_br_a, w_br_b, w_br_m, w_out, g_ffn, w_router, b_router)

    xp = x_prompt.reshape(t, d)
    qa_p, k32_p, khm_p, v32_p, vt_p, qi_p, ki32_p, kibf_p, wit_p = _front(
        xp, _rope_tables(jnp.arange(t, dtype=I32)), *fp, tm=_row_tile(t, 512))
    mem_k, mem_v = _memory_kv(mem_prompt.reshape(n_mem, d), g_mem.reshape(1, d), w_mem_kv.astype(BF16),
                              g_km.reshape(1, HEAD_DIM_M))
    out_a_p = _prompt_attention(qi_p, qa_p, wit_p, kibf_p, khm_p, vt_p)
    x1_p, h2_p, te_p, tw_p = _mix(xp, out_a_p, mp, _prep_gmlp_chunked(w_s, b_s), mem_k, mem_v,
                                  tm=_row_tile(t, 256), chunked=True, shared_mem=True, emit_vb=False, n_experts=n_e)

    xs = x_sample.reshape(bs, d)
    qa_s, k32_s, _, v32_s, _, qi_s, ki32_s, _, wit_s = _front(
        xs, _rope_tables(jnp.full((bs,), past, I32)), *fp, tm=bs)
    out_a_s = _sample_attention(page_table, jnp.moveaxis(qi_s, 0, 1), jnp.moveaxis(qa_s, 0, 1),
                                wit_s.T.reshape(bs, N_HEADS_IDX, 1), ki32_s.reshape(bs, 1, D_IDX),
                                k32_s.reshape(bs, 1, WA_KV), v32_s.reshape(bs, 1, WA_KV),
                                cache_idx_k, cache_k, cache_v)
    x1_s, h2_s, te_s, tw_s, vb_s = _mix(xs, out_a_s.reshape(bs, WA_Q), mp, _prep_gmlp_single(w_s, b_s),
                                        cache_mem_k.reshape(bs, n_mem, WM_Q), cache_mem_v.reshape(bs, n_mem, WM_Q),
                                        tm=_row_tile(bs, SUBLANES), chunked=False, shared_mem=False, emit_vb=True,
                                        n_experts=n_e)

    y_p, y_s = _moe([(x1_p, h2_p, te_p, tw_p), (x1_s, h2_s, te_s, tw_s)],
                    w_gate, b_gate, w_up, b_up, w_down, b_down)

    n_pg = t // ps
    return (y_p.reshape(1, t, d), y_s.reshape(bs, 1, d),
            k32_p.reshape(1, n_pg, ps, N_KV_A, HEAD_DIM_A), v32_p.reshape(1, n_pg, ps, N_KV_A, HEAD_DIM_A),
            ki32_p.reshape(1, n_pg, ps, D_IDX),
            mem_k.reshape(1, n_mem, N_HEADS_M, HEAD_DIM_M), mem_v.reshape(1, n_mem, N_HEADS_M, HEAD_DIM_M),
            k32_s.reshape(bs, 1, N_KV_A, HEAD_DIM_A), v32_s.reshape(bs, 1, N_KV_A, HEAD_DIM_A),
            ki32_s.reshape(bs, 1, D_IDX), vb_s.reshape(bs, 1, WIDTH_B))
```

```python
import functools

import jax
import jax.numpy as jnp
from jax import lax
from jax.experimental import pallas as pl
from jax.experimental.pallas import tpu as pltpu

F32 = jnp.float32
BF16 = jnp.bfloat16
I32 = jnp.int32

N_HEADS_A = 8
N_KV_A = 2
HEAD_DIM_A = 64
TOPK_MAX = 256
N_HEADS_IDX = 8
D_IDX = 64
N_GROUPS_B = 4
GROUP_DIM_B = 128
WIDTH_B = N_GROUPS_B * GROUP_DIM_B
CHUNK = 128
N_HEADS_M = 4
HEAD_DIM_M = 128
ROPE_THETA = 500000.0
ROT_FRACTION = 4
N_BRANCH = 3
TOP_K_MOE = 4
SWIGLU_LIMIT = 7.0
SWIGLU_ALPHA = 1.702
EPS = 1e-6

WA_Q = N_HEADS_A * HEAD_DIM_A
WA_KV = N_KV_A * HEAD_DIM_A
WI_Q = N_HEADS_IDX * D_IDX
WM_Q = N_HEADS_M * HEAD_DIM_M
GQA = N_HEADS_A // N_KV_A

LANES = 128
SUBLANES = 8
VMEM_LIMIT = 56 * 1024 * 1024

LOG2_E = 1.4426950408889634
INT_MIN = -(2 ** 31)
INT_MAX = 2 ** 31 - 1
NEG = -1e30

VT_ROWS = HEAD_DIM_A + 16
ATTN_BQ = 128
ATTN_SC = 512
MOE_BM = 512
FF_CHUNK = 256


def _cparams(n_axes=1, vmem=VMEM_LIMIT):
    return pltpu.CompilerParams(dimension_semantics=("arbitrary",) * n_axes, vmem_limit_bytes=vmem)


def _const_spec(shape, single=False):
    zeros = (0,) * len(shape)
    if single:
        return pl.BlockSpec(shape, lambda *_: zeros, pipeline_mode=pl.Buffered(1))
    return pl.BlockSpec(shape, lambda *_: zeros)


def _dot(a, b):
    return jnp.dot(a, b, preferred_element_type=F32)


def _dot_nt(a, b):
    return lax.dot_general(a, b, (((1,), (1,)), ((), ())), preferred_element_type=F32)


def _rms(x, g):
    return x * lax.rsqrt(jnp.mean(x * x, axis=-1, keepdims=True) + EPS) * g


def _split_dot(x, w):
    hi = x.astype(BF16)
    lo = (x - hi.astype(F32)).astype(BF16)
    return _dot(hi, w) + _dot(lo, w)


def _front_kernel(x_ref, gin_ref, wa_ref, wwit_ref, wvt_ref, bd_ref, gqa_ref, gka_ref, cos_ref, sa_ref, sb_ref,
                  qa_ref, k32_ref, khm_ref, v32_ref, vt_ref, qi_ref, ki32_ref, kibf_ref, wit_ref, *, score_scale):
    x = x_ref[...]
    h = _rms(x, gin_ref[...]).astype(BF16)
    p = _dot(h, wa_ref[...])
    cos, sa, sb = cos_ref[...], sa_ref[...], sb_ref[...]
    bd = bd_ref[...]

    def head_norm(v, g):
        ssq = _split_dot(v * v, bd)
        return v * lax.rsqrt(ssq * (1.0 / HEAD_DIM_A) + EPS) * g

    def rope(v):
        return v * cos + pltpu.roll(v, LANES - 8, 1) * sa + pltpu.roll(v, 8, 1) * sb

    for s in range(WA_Q // LANES):
        v = p[:, s * LANES:(s + 1) * LANES]
        v = rope(head_norm(v, gqa_ref[...])) * (HEAD_DIM_A ** -0.5 * LOG2_E)
        vb = v.astype(BF16)
        qa_ref[2 * s] = vb[:, :HEAD_DIM_A]
        qa_ref[2 * s + 1] = vb[:, HEAD_DIM_A:]
    o = WA_Q
    k = rope(head_norm(p[:, o:o + WA_KV], gka_ref[...]))
    k32_ref[...] = k
    kb = k.astype(BF16)
    khm_ref[0] = kb[:, :HEAD_DIM_A]
    khm_ref[1] = kb[:, HEAD_DIM_A:]
    o += WA_KV
    v32_ref[...] = p[:, o:o + WA_KV]
    o += WA_KV
    for s in range(WI_Q // LANES):
        vb = rope(p[:, o + s * LANES:o + (s + 1) * LANES]).astype(BF16)
        qi_ref[2 * s] = vb[:, :D_IDX]
        qi_ref[2 * s + 1] = vb[:, D_IDX:]
    o += WI_Q
    ki = rope(p[:, o:o + LANES])[:, :D_IDX]
    ki32_ref[...] = ki
    kibf_ref[...] = ki.astype(BF16)
    wit_ref[...] = _dot_nt(wwit_ref[...], h) * score_scale
    vt = _dot_nt(wvt_ref[...], h)
    tail = jnp.where(lax.broadcasted_iota(I32, (VT_ROWS - HEAD_DIM_A, vt.shape[1]), 0) == 0, 1.0, 0.0)
    for g in range(N_KV_A):
        vt_ref[g] = jnp.concatenate([vt[g * HEAD_DIM_A:(g + 1) * HEAD_DIM_A], tail], axis=0).astype(BF16)


def _front(x2d, tabs, gin, wa, wwit, wvt, bd, gqa, gka, tm):
    n, d = x2d.shape
    cos, sa, sb = tabs
    row = lambda w: pl.BlockSpec((tm, w), lambda i: (i, 0))
    hm = lambda nh, w: pl.BlockSpec((nh, tm, w), lambda i: (0, i, 0))
    out_shape = (
        jax.ShapeDtypeStruct((N_HEADS_A, n, HEAD_DIM_A), BF16),
        jax.ShapeDtypeStruct((n, WA_KV), F32),
        jax.ShapeDtypeStruct((N_KV_A, n, HEAD_DIM_A), BF16),
        jax.ShapeDtypeStruct((n, WA_KV), F32),
        jax.ShapeDtypeStruct((N_KV_A, VT_ROWS, n), BF16),
        jax.ShapeDtypeStruct((N_HEADS_IDX, n, D_IDX), BF16),
        jax.ShapeDtypeStruct((n, D_IDX), F32),
        jax.ShapeDtypeStruct((n, D_IDX), BF16),
        jax.ShapeDtypeStruct((N_HEADS_IDX, n), F32),
    )
    out_specs = (hm(N_HEADS_A, HEAD_DIM_A), row(WA_KV), hm(N_KV_A, HEAD_DIM_A), row(WA_KV),
                 pl.BlockSpec((N_KV_A, VT_ROWS, tm), lambda i: (0, 0, i)), hm(N_HEADS_IDX, D_IDX), row(D_IDX), row(D_IDX),
                 pl.BlockSpec((N_HEADS_IDX, tm), lambda i: (0, i)))
    in_specs = [row(d), _const_spec(gin.shape), _const_spec(wa.shape), _const_spec(wwit.shape),
                _const_spec(wvt.shape), _const_spec(bd.shape), _const_spec(gqa.shape), _const_spec(gka.shape),
                row(LANES), row(LANES), row(LANES)]
    kern = functools.partial(_front_kernel, score_scale=D_IDX ** -0.5 * N_HEADS_IDX ** -0.5)
    return pl.pallas_call(kern, out_shape=out_shape, grid=(n // tm,), in_specs=in_specs, out_specs=out_specs,
                          compiler_params=_cparams(), name="front")(x2d, gin, wa, wwit, wvt, bd, gqa, gka, cos, sa, sb)


def _memkv_kernel(mem_ref, gmem_ref, w_ref, gkm_ref, k32_ref, v32_ref):
    h = _rms(mem_ref[...], gmem_ref[...]).astype(BF16)
    kv = _dot(h, w_ref[...])
    for hh in range(N_HEADS_M):
        sl = slice(hh * HEAD_DIM_M, (hh + 1) * HEAD_DIM_M)
        k32_ref[:, sl] = _rms(kv[:, sl], gkm_ref[...])
    v32_ref[...] = kv[:, WM_Q:]


def _memory_kv(mem2d, gmem, w, gkm):
    n = mem2d.shape[0]
    out = jax.ShapeDtypeStruct((n, WM_Q), F32)
    return pl.pallas_call(_memkv_kernel, out_shape=(out, out), name="memory_kv",
                          compiler_params=pltpu.CompilerParams(vmem_limit_bytes=VMEM_LIMIT))(mem2d, gmem, w, gkm)


def _sort_key(s):
    s = jnp.where(s == 0.0, 0.0, s)
    bits = pltpu.bitcast(s, I32)
    return bits ^ ((bits >> 31) & INT_MAX)


ATTN_SUB = 128


def _attn_kernel(qi_ref, qa_ref, wi_ref, ki_ref, k_ref, vt_ref, out_ref,
                 keys_ref, cut_ref, bias_ref, lg_ref, cm_ref, m_ref, acc_ref, *, k_sel, idx_bits):
    bq, sc = ATTN_BQ, ATTN_SC
    i = pl.program_id(0)
    n_ch = (i * bq) // sc + 1
    t_pos = i * bq + lax.broadcasted_iota(I32, (1, bq), 1)
    sub = lax.broadcasted_iota(I32, (sc, 1), 0)
    qi = qi_ref[...].reshape(N_HEADS_IDX * bq, D_IDX)
    w = wi_ref[...]

    def chunk(c):
        return pl.ds(pl.multiple_of(c * sc, sc), sc)

    def score_chunk(c, carry):
        off = pl.multiple_of(c * sc, sc)
        d = _dot_nt(ki_ref[chunk(c), :], qi)
        s = jnp.zeros((sc, bq), F32)
        for h in range(N_HEADS_IDX):
            s = s + jnp.maximum(d[:, h * bq:(h + 1) * bq], 0.0) * w[h:h + 1, :]
        key = jnp.where(off + sub <= t_pos, _sort_key(s), INT_MIN)
        keys_ref[chunk(c), :] = key
        return carry

    lax.fori_loop(0, n_ch, score_chunk, 0)

    def count(pred):
        def body(c, acc):
            hit = jnp.where(pred(keys_ref[chunk(c), :], c * sc + sub), 1, 0)
            return acc + hit.reshape(sc // SUBLANES, SUBLANES, bq).sum(axis=0)

        def body2(j, acc):
            return body(2 * j + 1, body(2 * j, acc))
        acc = lax.fori_loop(0, n_ch // 2, body2, jnp.zeros((SUBLANES, bq), I32))
        acc = lax.fori_loop(n_ch - n_ch % 2, n_ch, body, acc)
        return acc.sum(axis=0, keepdims=True)

    def search_body(j, st):
        thr, cnt = st
        cand = thr + (jnp.int32(1) << (31 - j))
        c = count(lambda blk, _: blk >= cand)
        take = c >= k_sel
        return jnp.where(take, cand, thr), jnp.where(take, c, cnt)

    st0 = (jnp.full((1, bq), INT_MIN, I32), jnp.zeros((1, bq), I32) + n_ch * sc)
    thr, cnt = lax.fori_loop(0, 32, search_body, st0)
    cut_ref[...] = jnp.full((1, bq), INT_MAX, I32)

    excess = jnp.logical_and(cnt > k_sel, thr > INT_MIN)
    has_ties = jnp.max(jnp.where(excess, 1, 0)) > 0

    @pl.when(has_ties)
    def _():
        want = k_sel - count(lambda blk, _: blk > thr)

        def tie_body(j, cpos):
            cand = cpos + (jnp.int32(1) << (idx_bits - 1 - j))
            below = count(lambda blk, pos: jnp.where(blk == thr, pos, INT_MAX) < cand)
            return jnp.where(below < want, cand, cpos)

        cpos = lax.fori_loop(0, idx_bits, tie_body, jnp.zeros((1, bq), I32))
        cut_ref[...] = jnp.where(excess, cpos, INT_MAX)

    m_ref[...] = jnp.full(m_ref.shape, NEG, F32)
    acc_ref[...] = jnp.zeros(acc_ref.shape, F32)
    cut = cut_ref[...]
    floor = jnp.where(thr == INT_MIN, INT_MIN, thr - 1)
    n_sub = sc // ATTN_SUB

    qs = [qa_ref[g * GQA:(g + 1) * GQA].reshape(GQA * bq, HEAD_DIM_A) for g in range(N_KV_A)]
    neg_row = jnp.full((1, GQA * bq), NEG, F32)

    def stage(c, par, cb, par_b):
        do_a, do_b = c is not None, cb is not None
        if do_a:
            off = pl.multiple_of(c * sc, sc)
            keyc = keys_ref[chunk(c), :]

            @pl.when(has_ties)
            def _():
                tie = jnp.where(off + sub <= cut, 0.0, NEG)
                bias = jnp.where(keyc > thr, 0.0, jnp.where(keyc == thr, tie, NEG))
                bias_ref[...] = jnp.where(keyc == INT_MIN, NEG, bias)

            @pl.when(jnp.logical_not(has_ties))
            def _():
                bias_ref[...] = jnp.where(keyc > floor, 0.0, NEG)
            cmax = [neg_row] * N_KV_A
        if do_b:
            off_b = pl.multiple_of(cb * sc, sc)
            m_new, acc = [], []
            for g in range(N_KV_A):
                m_old = m_ref[g]
                m_new.append(jnp.maximum(m_old, cm_ref[par_b, g]))
                acc.append(jnp.exp2(m_old - m_new[g]) * acc_ref[g])
                m_ref[g] = m_new[g]
        for r in range(n_sub):
            rows = slice(r * ATTN_SUB, (r + 1) * ATTN_SUB)
            if do_a:
                bias4 = jnp.concatenate([bias_ref[rows, :]] * GQA, axis=1)
                for g in range(N_KV_A):
                    lg = _dot_nt(k_ref[g, pl.ds(off + r * ATTN_SUB, ATTN_SUB), :], qs[g]) + bias4
                    lg_ref[par, g, rows, :] = lg
                    cmax[g] = jnp.maximum(cmax[g], lg.max(axis=0, keepdims=True))
            if do_b:
                for g in range(N_KV_A):
                    p = jnp.exp2(lg_ref[par_b, g, rows, :] - m_new[g])
                    vt = vt_ref[g, :, pl.ds(off_b + r * ATTN_SUB, ATTN_SUB)]
                    acc[g] = acc[g] + _dot(vt, p.astype(BF16))
        if do_b:
            for g in range(N_KV_A):
                acc_ref[g] = acc[g]
        if do_a:
            for g in range(N_KV_A):
                cm_ref[par, g] = cmax[g]

    stage(jnp.int32(0), 0, None, None)

    def stage_pair(j, carry):
        stage(2 * j + 1, 1, 2 * j, 0)
        stage(2 * j + 2, 0, 2 * j + 1, 1)
        return carry

    last = n_ch - 1
    lax.fori_loop(0, last // 2, stage_pair, 0)

    @pl.when(last % 2 == 1)
    def _():
        stage(last, 1, last - 1, 0)
        stage(None, None, last, 1)

    @pl.when(last % 2 == 0)
    def _():
        stage(None, None, last, 0)

    heads = []
    for g in range(N_KV_A):
        acc = acc_ref[g]
        o = acc[:HEAD_DIM_A] / acc[HEAD_DIM_A:HEAD_DIM_A + 1]
        heads += [o[:, hh * bq:(hh + 1) * bq] for hh in range(GQA)]
    out_ref[...] = jnp.concatenate(heads, axis=0).T.astype(out_ref.dtype)


def _prompt_attention(qi_hm, qa_hm, wit, kibf, khm, vt):
    t = kibf.shape[0]
    bq = ATTN_BQ
    k_sel = min(TOPK_MAX, t // 4)
    kern = functools.partial(_attn_kernel, k_sel=k_sel, idx_bits=max(1, (t - 1).bit_length()))
    in_specs = [pl.BlockSpec((N_HEADS_IDX, bq, D_IDX), lambda i: (0, i, 0)),
                pl.BlockSpec((N_HEADS_A, bq, HEAD_DIM_A), lambda i: (0, i, 0)),
                pl.BlockSpec((N_HEADS_IDX, bq), lambda i: (0, i)),
                _const_spec(kibf.shape, True), _const_spec(khm.shape, True), _const_spec(vt.shape, True)]
    scratch = [pltpu.VMEM((t, bq), I32),
               pltpu.VMEM((1, bq), I32), pltpu.VMEM((ATTN_SC, bq), F32),
               pltpu.VMEM((2, N_KV_A, ATTN_SC, GQA * bq), F32), pltpu.VMEM((2, N_KV_A, 1, GQA * bq), F32),
               pltpu.VMEM((N_KV_A, 1, GQA * bq), F32),
               pltpu.VMEM((N_KV_A, VT_ROWS, GQA * bq), F32)]
    return pl.pallas_call(kern, out_shape=jax.ShapeDtypeStruct((t, WA_Q), BF16), grid=(t // bq,),
                          in_specs=in_specs, out_specs=pl.BlockSpec((bq, WA_Q), lambda i: (i, 0)),
                          scratch_shapes=scratch, compiler_params=_cparams(), name="attn")(
                              qi_hm, qa_hm, wit, kibf, khm, vt)


SAMPLE_CK = 2048


def _sample_attn_kernel(pt_ref, qi_ref, qa_ref, wi_ref, kin_ref, kn_ref, vn_ref, cidx_hbm, ck_hbm, cv_hbm, out_ref,
                        idx_buf, k_buf, v_buf, scr_ref, sems, *, n_pages, ps, ck, k_sel, idx_bits):
    b = pl.program_id(0)
    nb = pl.num_programs(0)
    past = n_pages * ps
    nc = past // ck

    def page_copy(which, bb, p):
        src, dst = ((cidx_hbm, idx_buf), (ck_hbm, k_buf), (cv_hbm, v_buf))[which]
        cols = pl.ds(pl.multiple_of(p * ps, ps), ps)
        dst = dst.at[:, cols] if which == 0 else dst.at[:, :, cols]
        return pltpu.make_async_copy(src.at[pt_ref[bb, p]], dst, sems.at[which])

    def for_pages(fn):
        lax.fori_loop(0, n_pages, lambda p, c: (fn(p), c)[1], 0)

    @pl.when(b == 0)
    def _():
        for_pages(lambda p: page_copy(0, b, p).start())

    for_pages(lambda p: (page_copy(1, b, p).start(), page_copy(2, b, p).start()))
    for_pages(lambda p: page_copy(0, b, p).wait())

    qi = qi_ref[0]
    w = wi_ref[0]
    for c in range(nc):
        d = _dot(qi, idx_buf[:, c * ck:(c + 1) * ck].astype(BF16))
        scr_ref[c:c + 1, :] = (jnp.maximum(d, 0.0) * w).sum(axis=0, keepdims=True)
    d_new = (qi.astype(F32) * kin_ref[0].astype(BF16).astype(F32)).sum(axis=-1, keepdims=True)
    s_new = (jnp.maximum(d_new, 0.0) * w).sum(axis=0, keepdims=True)

    @pl.when(b + 1 < nb)
    def _():
        for_pages(lambda p: page_copy(0, b + 1, p).start())

    keys = _sort_key(scr_ref[...])
    key_new = _sort_key(s_new)
    pos = (lax.broadcasted_iota(I32, keys.shape, 0) * ck + lax.broadcasted_iota(I32, keys.shape, 1))

    def count(pred):
        hit = jnp.where(pred(keys, pos), 1, 0).sum(axis=1, keepdims=True).sum(axis=0, keepdims=True)
        return hit + jnp.where(pred(key_new, past), 1, 0)

    def search_cond(st):
        bit, _, cnt = st
        return jnp.logical_and(bit >= 0, jnp.max(jnp.where(cnt == k_sel, 0, 1)) > 0)

    def search_body(st):
        bit, thr, cnt = st
        cand = thr + (jnp.int32(1) << bit)
        c = count(lambda kk, _: kk >= cand)
        take = c >= k_sel
        return bit - 1, jnp.where(take, cand, thr), jnp.where(take, c, cnt)

    st0 = (jnp.int32(31), jnp.full((1, 1), INT_MIN, I32), jnp.full((1, 1), past + 1, I32))
    _, thr, cnt = lax.while_loop(search_cond, search_body, st0)

    want = k_sel - count(lambda kk, _: kk > thr)

    def tie_body(j, cpos):
        cand = cpos + (jnp.int32(1) << (idx_bits - 1 - j))
        below = count(lambda kk, pp: jnp.where(kk == thr, pp, INT_MAX) < cand)
        return jnp.where(below < want, cand, cpos)

    cpos = lax.fori_loop(0, idx_bits, tie_body, jnp.zeros((1, 1), I32))
    cut = jnp.where(cnt > k_sel, cpos, INT_MAX)

    def sel_bias(kk, pp):
        tie = jnp.where(pp <= cut, 0.0, NEG)
        return jnp.where(kk > thr, 0.0, jnp.where(kk == thr, tie, NEG))

    bias = sel_bias(keys, pos)
    bias_new = sel_bias(key_new, past)

    for_pages(lambda p: (page_copy(1, b, p).wait(), page_copy(2, b, p).wait()))

    qa = qa_ref[0]
    heads = []
    for g in range(N_KV_A):
        qg = qa[g * GQA:(g + 1) * GQA]
        cols = slice(g * HEAD_DIM_A, (g + 1) * HEAD_DIM_A)
        m = jnp.full((GQA, 1), NEG, F32)
        l = jnp.zeros((GQA, 1), F32)
        acc = jnp.zeros((GQA, HEAD_DIM_A), F32)
        for c in range(nc):
            span = slice(c * ck, (c + 1) * ck)
            lg = _dot(qg, k_buf[g, :, span].astype(BF16)) + bias[c:c + 1, :]
            m_new = jnp.maximum(m, lg.max(axis=-1, keepdims=True))
            p = jnp.exp2(lg - m_new)
            alpha = jnp.exp2(m - m_new)
            l = alpha * l + p.sum(axis=-1, keepdims=True)
            acc = alpha * acc + _dot_nt(p.astype(BF16), v_buf[g, :, span].astype(BF16))
            m = m_new
        kn = kn_ref[0][:, cols].astype(BF16).astype(F32)
        vn = vn_ref[0][:, cols].astype(BF16).astype(F32)
        lg = (qg.astype(F32) * kn).sum(axis=-1, keepdims=True) + bias_new
        m_new = jnp.maximum(m, lg)
        p = jnp.exp2(lg - m_new)
        alpha = jnp.exp2(m - m_new)
        l = alpha * l + p
        acc = alpha * acc + p * vn
        o = acc / l
        heads += [o[hh:hh + 1, :] for hh in range(GQA)]
    out_ref[0] = jnp.concatenate(heads, axis=1)


def _sample_attention(page_table, qi_s, qa_s, wi_s, ki_new, k_new, v_new, cache_idx_k, cache_k, cache_v):
    nbatch, n_pages = page_table.shape
    n_pool, ps, d_idx = cache_idx_k.shape
    past = n_pages * ps
    ck = min(SAMPLE_CK, past)
    assert past % ck == 0 and ck % ps == 0
    k_sel = min(TOPK_MAX, (past + 1) // 4)
    kern = functools.partial(_sample_attn_kernel, n_pages=n_pages, ps=ps, ck=ck, k_sel=k_sel,
                             idx_bits=past.bit_length())
    per_b = lambda shape: pl.BlockSpec((1,) + shape, lambda b, pt: (b, 0, 0))
    any_spec = pl.BlockSpec(memory_space=pl.ANY)
    grid_spec = pltpu.PrefetchScalarGridSpec(
        num_scalar_prefetch=1, grid=(nbatch,),
        in_specs=[per_b((N_HEADS_IDX, d_idx)), per_b((N_HEADS_A, HEAD_DIM_A)), per_b((N_HEADS_IDX, 1)),
                  per_b((1, d_idx)), per_b((1, WA_KV)), per_b((1, WA_KV)), any_spec, any_spec, any_spec],
        out_specs=per_b((1, WA_Q)),
        scratch_shapes=[pltpu.VMEM((d_idx, past), F32), pltpu.VMEM((N_KV_A, HEAD_DIM_A, past), F32),
                        pltpu.VMEM((N_KV_A, HEAD_DIM_A, past), F32),
                        pltpu.VMEM((past // ck, ck), F32), pltpu.SemaphoreType.DMA((3,))])
    return pl.pallas_call(kern, out_shape=jax.ShapeDtypeStruct((nbatch, 1, WA_Q), F32), grid_spec=grid_spec,
                          compiler_params=_cparams(), name="sample_attn")(
                              page_table, qi_s, qa_s, wi_s, ki_new, k_new, v_new,
                              jnp.transpose(cache_idx_k, (0, 2, 1)), jnp.transpose(cache_k, (0, 2, 3, 1)),
                              jnp.transpose(cache_v, (0, 2, 3, 1)))


def _rope_tables(pos):
    rot = HEAD_DIM_A // ROT_FRACTION
    half = rot // 2
    inv_freq = ROPE_THETA ** (-jnp.arange(half, dtype=F32) / half)
    ang = pos.astype(F32)[:, None] * inv_freq[None, :]
    cos, sin = jnp.cos(ang), jnp.sin(ang)
    n = pos.shape[0]
    z = lambda w: jnp.zeros((n, w), F32)
    c = jnp.concatenate([cos, cos, jnp.ones((n, HEAD_DIM_A - rot), F32)], axis=1)
    sa = jnp.concatenate([-sin, z(HEAD_DIM_A - half)], axis=1)
    sb = jnp.concatenate([z(half), sin, z(HEAD_DIM_A - rot)], axis=1)
    rep = LANES // HEAD_DIM_A
    return tuple(jnp.tile(a, (1, rep)) for a in (c, sa, sb))


def _prep_params(w_in, g_in, g_qa, g_ka):
    d = w_in.shape[0]
    w_qa, w_ka, w_va, w_qi, w_ki, w_wi = _split_w_in(w_in, d)[:6]
    wa = jnp.concatenate([w_qa, w_ka, w_va, w_qi, w_ki, jnp.zeros((d, LANES - D_IDX), w_in.dtype)], axis=1)
    lane = jnp.arange(LANES)
    bd = (lane[:, None] // HEAD_DIM_A == lane[None, :] // HEAD_DIM_A).astype(BF16)
    rep = LANES // HEAD_DIM_A
    return (g_in.reshape(1, d), wa.astype(BF16), w_wi.T.astype(BF16), w_va.T.astype(BF16), bd,
            jnp.tile(g_qa, rep).reshape(1, LANES), jnp.tile(g_ka, rep).reshape(1, LANES))


def _split_w_in(w_in, d):
    sizes = (WA_Q, WA_KV, WA_KV, WI_Q, D_IDX, N_HEADS_IDX, WIDTH_B, WIDTH_B, WM_Q, N_BRANCH * d)
    parts, o = [], 0
    for s in sizes:
        parts.append(w_in[:, o:o + s])
        o += s
    return parts


def _mix_kernel(*refs, chunked, shared_mem, emit_vb, n_experts):
    it = iter(refs)
    x_ref, oa_ref, gin_ref, wb_ref, gvb_ref, gqm_ref = (next(it) for _ in range(6))
    if chunked:
        wtril_ref, bt_ref = next(it), next(it)
    else:
        wdiag_ref, bdiag_ref = next(it), next(it)
    mk_ref, mv_ref = next(it), next(it)
    wbra_ref, wbrb_ref, wbrm_ref, wout_ref, gffn_ref, wrh_ref, wrl_ref, br_ref = (next(it) for _ in range(8))
    x1_ref, h2_ref, te_ref, tw_ref = (next(it) for _ in range(4))
    vb_ref = next(it) if emit_vb else None

    x = x_ref[...]
    tm, d = x.shape
    h = _rms(x, gin_ref[...]).astype(BF16)
    p = _dot(h, wb_ref[...])
    ub = p[:, :WIDTH_B]
    vb = _rms(p[:, WIDTH_B:2 * WIDTH_B], gvb_ref[...])
    if emit_vb:
        vb_ref[...] = vb
    o = 2 * WIDTH_B
    qm = p[:, o:o + WM_Q]
    o += WM_Q
    gates = p[:, o:o + N_BRANCH * d]

    if chunked:
        vbb = vb.astype(BF16)
        rows = []
        for cc in range(tm // CHUNK):
            cols = []
            for g in range(N_GROUPS_B):
                vg = vbb[cc * CHUNK:(cc + 1) * CHUNK, g * GROUP_DIM_B:(g + 1) * GROUP_DIM_B]
                cols.append(_dot(wtril_ref[g], vg))
            rows.append(jnp.concatenate(cols, axis=1) + bt_ref[...])
        z = jnp.concatenate(rows, axis=0) if len(rows) > 1 else rows[0]
    else:
        z = vb * wdiag_ref[...] + bdiag_ref[...]
    out_b = ub * z

    scale_m = HEAD_DIM_M ** -0.5
    outs = []
    for hh in range(N_HEADS_M):
        sl = slice(hh * HEAD_DIM_M, (hh + 1) * HEAD_DIM_M)
        qh = _rms(qm[:, sl], gqm_ref[...])
        if shared_mem:
            lg = _dot_nt(qh.astype(BF16), mk_ref[:, sl].astype(BF16)) * scale_m
            pm = jnp.exp(lg - lg.max(axis=-1, keepdims=True))
            pm = pm / pm.sum(axis=-1, keepdims=True)
            outs.append(_dot(pm.astype(BF16), mv_ref[:, sl].astype(BF16)))
        else:
            per_row = []
            for r in range(tm):
                kr = mk_ref[r, :, sl]
                lg = (kr * qh[r:r + 1, :]).sum(axis=-1, keepdims=True) * scale_m
                pm = jnp.exp(lg - lg.max(axis=0, keepdims=True))
                pm = pm / pm.sum(axis=0, keepdims=True)
                per_row.append((pm * mv_ref[r, :, sl]).sum(axis=0, keepdims=True))
            outs.append(jnp.concatenate(per_row, axis=0))
    out_m = jnp.concatenate(outs, axis=1)

    sig = jax.nn.sigmoid
    merged = (sig(gates[:, :d]) * _dot(oa_ref[...].astype(BF16), wbra_ref[...])
              + sig(gates[:, d:2 * d]) * _dot(out_b.astype(BF16), wbrb_ref[...])
              + sig(gates[:, 2 * d:]) * _dot(out_m.astype(BF16), wbrm_ref[...]))
    x1 = x + _dot(merged.astype(BF16), wout_ref[...])
    x1_ref[...] = x1
    h2 = _rms(x1, gffn_ref[...])
    h2_ref[...] = h2.astype(h2_ref.dtype)

    hi = h2.astype(BF16)
    lo = (h2 - hi.astype(F32)).astype(BF16)
    lg = _dot(hi, wrh_ref[...]) + _dot(hi, wrl_ref[...]) + _dot(lo, wrh_ref[...]) + br_ref[...]
    lane = lax.broadcasted_iota(I32, lg.shape, 1)
    lg = jnp.where(lane < n_experts, lg, -jnp.inf)
    vals, idxs = [], []
    for _ in range(TOP_K_MOE):
        m = lg.max(axis=-1, keepdims=True)
        idx = jnp.where(lg == m, lane, LANES).min(axis=-1, keepdims=True)
        vals.append(m)
        idxs.append(idx)
        lg = jnp.where(lane == idx, -jnp.inf, lg)
    ex = [jnp.exp(v - vals[0]) for v in vals]
    den = ex[0]
    for e in ex[1:]:
        den = den + e
    te = jnp.full(lane.shape, -1, I32)
    tw = jnp.zeros(lane.shape, F32)
    for j in range(TOP_K_MOE):
        te = jnp.where(lane == j, idxs[j], te)
        tw = jnp.where(lane == j, ex[j] / den, tw)
    te_ref[...] = te
    tw_ref[...] = tw


def _mix(x2d, out_a, mix_params, gmlp_params, mem_k, mem_v, *, tm, chunked, shared_mem, emit_vb, n_experts):
    n, d = x2d.shape
    gin, wb, gvb, gqm, wbra, wbrb, wbrm, wout, gffn, wrh, wrl, br = mix_params
    row = lambda w: pl.BlockSpec((tm, w), lambda i: (i, 0))
    if shared_mem:
        mem_specs = [_const_spec(mem_k.shape), _const_spec(mem_v.shape)]
    else:
        mspec = pl.BlockSpec((tm,) + mem_k.shape[1:], lambda i: (i, 0, 0))
        mem_specs = [mspec, mspec]
    consts = lambda arrs: [_const_spec(a.shape) for a in arrs]
    in_specs = ([row(d), row(WA_Q)] + consts([gin, wb, gvb, gqm]) + consts(gmlp_params) + mem_specs
                + consts([wbra, wbrb, wbrm, wout, gffn, wrh, wrl, br]))
    out_shape = [jax.ShapeDtypeStruct((n, d), F32), jax.ShapeDtypeStruct((n, d), F32),
                 jax.ShapeDtypeStruct((n, LANES), I32), jax.ShapeDtypeStruct((n, LANES), F32)]
    out_specs = [row(d), row(d), row(LANES), row(LANES)]
    if emit_vb:
        out_shape.append(jax.ShapeDtypeStruct((n, WIDTH_B), F32))
        out_specs.append(row(WIDTH_B))
    kern = functools.partial(_mix_kernel, chunked=chunked, shared_mem=shared_mem, emit_vb=emit_vb,
                             n_experts=n_experts)
    return pl.pallas_call(kern, out_shape=tuple(out_shape), grid=(n // tm,), in_specs=in_specs,
                          out_specs=tuple(out_specs), compiler_params=_cparams(), name="mix")(
                              x2d, out_a, gin, wb, gvb, gqm, *gmlp_params, mem_k, mem_v,
                              wbra, wbrb, wbrm, wout, gffn, wrh, wrl, br)


def _prep_mix_params(w_in, g_in, g_vb, g_qm, w_br_a, w_br_b, w_br_m, w_out, g_ffn, w_router, b_router):
    d = w_in.shape[0]
    parts = _split_w_in(w_in, d)
    wb = jnp.concatenate(parts[6:10], axis=1).astype(BF16)
    n_e = w_router.shape[1]
    wr = jnp.pad(w_router, ((0, 0), (0, LANES - n_e)))
    wrh = wr.astype(BF16)
    wrl = (wr - wrh.astype(F32)).astype(BF16)
    br = jnp.pad(b_router, (0, LANES - n_e)).reshape(1, LANES)
    return (g_in.reshape(1, d), wb, g_vb.reshape(1, WIDTH_B), g_qm.reshape(1, HEAD_DIM_M),
            w_br_a.astype(BF16), w_br_b.astype(BF16), w_br_m.astype(BF16), w_out.astype(BF16),
            g_ffn.reshape(1, d), wrh, wrl, br)


def _prep_gmlp_chunked(w_s, b_s):
    tril = jnp.tril(jnp.ones((CHUNK, CHUNK), w_s.dtype))
    wtril = (w_s * tril).astype(BF16)
    bt = jnp.repeat(b_s.T, GROUP_DIM_B, axis=1)
    return wtril, bt


def _prep_gmlp_single(w_s, b_s):
    wdiag = jnp.repeat(w_s[:, 0, 0], GROUP_DIM_B).reshape(1, WIDTH_B)
    bdiag = jnp.repeat(b_s[:, 0], GROUP_DIM_B).reshape(1, WIDTH_B)
    return wdiag, bdiag


MOE_TP = 256


def _plan_kernel(te_ref, tri_ref, upper_ref, lslot_ref, cnt_ref, tstart_ref, gprev_ref, total_ref, carry_ref):
    @pl.when(pl.program_id(0) == 0)
    def _():
        carry_ref[...] = jnp.zeros(carry_ref.shape, F32)

    te = te_ref[...]
    lane = lax.broadcasted_iota(I32, te.shape, 1)
    picks = [te[:, j:j + 1] for j in range(TOP_K_MOE)]
    onehot = jnp.zeros(te.shape, F32)
    for e in picks:
        onehot = onehot + jnp.where(lane == e, 1.0, 0.0)
    before = _dot(tri_ref[...], onehot.astype(BF16))
    cnt = onehot.sum(axis=0, keepdims=True)
    cnt = jnp.floor((cnt + (SUBLANES - 1)) * (1.0 / SUBLANES)) * SUBLANES
    cnt8 = jnp.broadcast_to(cnt, (SUBLANES, LANES)).astype(BF16)
    tstart = _dot(cnt8, upper_ref[...])[0:1, :]
    lslot = jnp.zeros(te.shape, I32)
    for j, e in enumerate(picks):
        r = jnp.where(lane == e, before + tstart, 0.0).sum(axis=-1, keepdims=True).astype(I32)
        lslot = jnp.where(lane == j, r, lslot)
    lslot_ref[...] = lslot
    cnt_ref[0] = cnt.astype(I32)
    tstart_ref[0] = tstart.astype(I32)
    gprev_ref[0] = carry_ref[...].astype(I32)
    carry_ref[...] = carry_ref[...] + cnt
    total_ref[...] = carry_ref[...].astype(I32)


def _moe_plan(te_all):
    n = te_all.shape[0]
    tp = MOE_TP
    n_tiles = n // tp
    r = jnp.arange(tp)
    tri = (r[:, None] > r[None, :]).astype(BF16)
    e = jnp.arange(LANES)
    upper = (e[:, None] < e[None, :]).astype(BF16)
    per_tile = jax.ShapeDtypeStruct((n_tiles, 1, LANES), I32)
    tile_spec = pl.BlockSpec((1, 1, LANES), lambda i: (i, 0, 0))
    return pl.pallas_call(
        _plan_kernel,
        out_shape=(jax.ShapeDtypeStruct((n, LANES), I32), per_tile, per_tile, per_tile,
                   jax.ShapeDtypeStruct((1, LANES), I32)),
        grid=(n_tiles,),
        in_specs=[pl.BlockSpec((tp, LANES), lambda i: (i, 0)), _const_spec((tp, tp)), _const_spec((LANES, LANES))],
        out_specs=(pl.BlockSpec((tp, LANES), lambda i: (i, 0)), tile_spec, tile_spec, tile_spec,
                   _const_spec((1, LANES))),
        scratch_shapes=[pltpu.VMEM((1, LANES), F32)],
        compiler_params=_cparams(), name="moe_plan")(te_all, tri, upper)


def _run_pieces(tp):
    sizes, s = [], SUBLANES
    while s <= tp:
        sizes.append(s)
        s *= 2
    return tuple(reversed(sizes))


def _local_rows(tp, n_e):
    return tp * TOP_K_MOE + n_e * SUBLANES


def _for_runs(tabs, tile, n_e, tp, fn):
    cnt_s, tstart_s, gbase_s = tabs

    def per_expert(e, c):
        idx = tile * n_e + e
        length, src0, dst0 = cnt_s[idx], tstart_s[idx], gbase_s[idx]
        off = jnp.int32(0)
        for size in _run_pieces(tp):
            piece = length & size

            @pl.when(piece != 0)
            def _():
                fn(pl.multiple_of(src0 + off, SUBLANES), pl.multiple_of(dst0 + off, SUBLANES), size)
            off = off + piece
        return c

    lax.fori_loop(0, n_e, per_expert, 0)


def _slot_matrix(lslot, vals, n_rows):
    col = lax.broadcasted_iota(I32, (lslot.shape[0], n_rows), 1)
    m = jnp.zeros(col.shape, F32)
    for j in range(TOP_K_MOE):
        v = 1.0 if vals is None else vals[:, j:j + 1]
        m = m + jnp.where(lslot[:, j:j + 1] == col, v, 0.0)
    return m


def _dispatch_kernel(cnt_s, tstart_s, gbase_s, lslot_ref, h2_ref, xs_in, xs_out, stage, sem, *, tp, n_e, tile0):
    del xs_in
    n_rows = _local_rows(tp, n_e)
    i = pl.program_id(0)
    slot = i % 2
    pt = _slot_matrix(lslot_ref[...], None, n_rows).astype(BF16)
    stage[slot] = lax.dot_general(pt, h2_ref[...].astype(BF16), (((0,), (0,)), ((), ())),
                                  preferred_element_type=F32)

    def runs(tile, s, op):
        _for_runs((cnt_s, tstart_s, gbase_s), tile0 + tile, n_e, tp,
                  lambda src, dst, size: op(pltpu.make_async_copy(
                      stage.at[s, pl.ds(src, size), :], xs_out.at[pl.ds(dst, size), :], sem.at[s])))

    @pl.when(i > 0)
    def _():
        runs(i - 1, 1 - slot, lambda cp: cp.wait())

    runs(i, slot, lambda cp: cp.start())

    @pl.when(i == pl.num_programs(0) - 1)
    def _():
        runs(i, slot, lambda cp: cp.wait())


def _dispatch(tabs, tile0, lslot, h2, xs, n_e):
    n, d = h2.shape
    tp = min(MOE_TP, n)
    n_rows = _local_rows(tp, n_e)
    kern = functools.partial(_dispatch_kernel, tp=tp, n_e=n_e, tile0=tile0)
    any_spec = pl.BlockSpec(memory_space=pl.ANY)
    grid_spec = pltpu.PrefetchScalarGridSpec(
        num_scalar_prefetch=3, grid=(n // tp,),
        in_specs=[pl.BlockSpec((tp, LANES), lambda i, *_: (i, 0)), pl.BlockSpec((tp, d), lambda i, *_: (i, 0)), any_spec],
        out_specs=any_spec,
        scratch_shapes=[pltpu.VMEM((2, n_rows, d), F32), pltpu.SemaphoreType.DMA((2,))])
    return pl.pallas_call(
        kern, out_shape=jax.ShapeDtypeStruct(xs.shape, xs.dtype), grid_spec=grid_spec,
        input_output_aliases={5: 0},
        compiler_params=pltpu.CompilerParams(dimension_semantics=("arbitrary",), has_side_effects=True,
                                             vmem_limit_bytes=VMEM_LIMIT),
        name="dispatch")(*tabs, lslot, h2, xs)


def _expert_kernel(be_ref, nu_ref, xs_ref, wg_ref, wu_ref, wd_ref, bg_ref, bu_ref, bd_ref, y_ref,
                   wg_bf, wu_bf, wd_bf):
    b = pl.program_id(0)

    @pl.when(b >= nu_ref[0])
    def _():
        y_ref[...] = jnp.zeros(y_ref.shape, F32)

    @pl.when(b < nu_ref[0])
    def _():
        prev = be_ref[jnp.maximum(b - 1, 0)]

        @pl.when(jnp.logical_or(b == 0, be_ref[b] != prev))
        def _():
            wg_bf[...] = wg_ref[0].astype(BF16)
            wu_bf[...] = wu_ref[0].astype(BF16)
            wd_bf[...] = wd_ref[0].astype(BF16)

        xb = xs_ref[...].astype(BF16)
        ff = wg_bf.shape[1]
        y = jnp.zeros(y_ref.shape, F32) + bd_ref[0]
        for n in range(ff // FF_CHUNK):
            sl = slice(n * FF_CHUNK, (n + 1) * FF_CHUNK)
            hg = jnp.minimum(_dot(xb, wg_bf[:, sl]) + bg_ref[0][:, sl], SWIGLU_LIMIT)
            hu = jnp.clip(_dot(xb, wu_bf[:, sl]) + bu_ref[0][:, sl], -SWIGLU_LIMIT, SWIGLU_LIMIT)
            act = hg * jax.nn.sigmoid(SWIGLU_ALPHA * hg) * (hu + 1.0)
            y = y + _dot(act.astype(BF16), wd_bf[sl, :])
        y_ref[...] = y


def _experts(xs, block_e, n_used, w_gate, b_gate, w_up, b_up, w_down, b_down):
    ns, d = xs.shape
    n_e, _, ff = w_gate.shape
    bm = MOE_BM
    blk = lambda b, be, nu: (jnp.minimum(b, nu[0] - 1), 0)
    wsel = lambda b, be, nu: (be[b], 0, 0)
    grid_spec = pltpu.PrefetchScalarGridSpec(
        num_scalar_prefetch=2, grid=(ns // bm,),
        in_specs=[pl.BlockSpec((bm, d), blk),
                  pl.BlockSpec((1, d, ff), wsel), pl.BlockSpec((1, d, ff), wsel), pl.BlockSpec((1, ff, d), wsel),
                  pl.BlockSpec((1, 1, ff), wsel), pl.BlockSpec((1, 1, ff), wsel), pl.BlockSpec((1, 1, d), wsel)],
        out_specs=pl.BlockSpec((bm, d), lambda b, be, nu: (b, 0)),
        scratch_shapes=[pltpu.VMEM((d, ff), BF16), pltpu.VMEM((d, ff), BF16), pltpu.VMEM((ff, d), BF16)])
    return pl.pallas_call(_expert_kernel, out_shape=jax.ShapeDtypeStruct((ns, d), F32), grid_spec=grid_spec,
                          compiler_params=_cparams(), name="experts")(
                              block_e, n_used, xs, w_gate, w_up, w_down,
                              b_gate.reshape(n_e, 1, ff), b_up.reshape(n_e, 1, ff), b_down.reshape(n_e, 1, d))


def _combine_kernel(cnt_s, tstart_s, gbase_s, lslot_ref, tw_ref, x1_ref, yb_hbm, y_ref, buf, sem, *, tp, n_e, tile0):
    n_rows = _local_rows(tp, n_e)
    i = pl.program_id(0)
    slot = i % 2

    def runs(tile, s, op):
        _for_runs((cnt_s, tstart_s, gbase_s), tile0 + tile, n_e, tp,
                  lambda loc, glob, size: op(pltpu.make_async_copy(
                      yb_hbm.at[pl.ds(glob, size), :], buf.at[s, pl.ds(loc, size), :], sem.at[s])))

    @pl.when(i == 0)
    def _():
        buf[...] = jnp.zeros(buf.shape, F32)
        runs(i, slot, lambda cp: cp.start())

    @pl.when(i + 1 < pl.num_programs(0))
    def _():
        runs(i + 1, 1 - slot, lambda cp: cp.start())

    runs(i, slot, lambda cp: cp.wait())

    ptw = _slot_matrix(lslot_ref[...], tw_ref[...], n_rows)
    rows = buf[slot]
    w_hi = ptw.astype(BF16)
    w_lo = (ptw - w_hi.astype(F32)).astype(BF16)
    r_hi = rows.astype(BF16)
    r_lo = (rows - r_hi.astype(F32)).astype(BF16)
    y_ref[...] = x1_ref[...] + _dot(w_hi, r_hi) + (_dot(w_hi, r_lo) + _dot(w_lo, r_hi))


def _combine(tabs, tile0, lslot, tw, x1, yb, n_e):
    n, d = x1.shape
    tp = min(MOE_TP, n)
    n_rows = _local_rows(tp, n_e)
    kern = functools.partial(_combine_kernel, tp=tp, n_e=n_e, tile0=tile0)
    row = lambda w: pl.BlockSpec((tp, w), lambda i, *_: (i, 0))
    grid_spec = pltpu.PrefetchScalarGridSpec(
        num_scalar_prefetch=3, grid=(n // tp,),
        in_specs=[row(LANES), row(LANES), row(d), pl.BlockSpec(memory_space=pl.ANY)],
        out_specs=row(d),
        scratch_shapes=[pltpu.VMEM((2, n_rows, d), F32), pltpu.SemaphoreType.DMA((2,))])
    return pl.pallas_call(kern, out_shape=jax.ShapeDtypeStruct(x1.shape, F32), grid_spec=grid_spec,
                          compiler_params=_cparams(), name="combine")(*tabs, lslot, tw, x1, yb)


def _moe(groups, w_gate, b_gate, w_up, b_up, w_down, b_down):
    n_e = w_gate.shape[0]
    d = groups[0][0].shape[1]
    tp, bm = MOE_TP, MOE_BM
    sizes = [g[0].shape[0] for g in groups]
    assert all(n % tp == 0 or n < tp for n in sizes), sizes
    parts, tile0s, o = [], [], 0
    for g, n in zip(groups, sizes):
        n_pad = -(-n // tp) * tp
        parts.append(jnp.pad(g[2], ((0, n_pad - n), (0, 0)), constant_values=-1))
        tile0s.append(o // tp)
        o += n_pad
    lslot, cnt3, tstart3, gprev3, total = _moe_plan(jnp.concatenate(parts, axis=0))
    counts = total[0, :n_e]
    padded_cnt = (counts + bm - 1) // bm * bm
    pad_end = jnp.cumsum(padded_cnt)
    pad_start = pad_end - padded_cnt
    n_assign = sum(sizes) * TOP_K_MOE
    n_slack = (o // tp) * n_e * (SUBLANES - 1)
    nb = -(-(n_assign + n_slack) // bm) + n_e
    n_used = (pad_end[-1] // bm).astype(I32)
    blocks = jnp.arange(nb, dtype=I32)
    block_e = jnp.minimum((pad_end[None, :] <= (blocks * bm)[:, None]).sum(axis=1), n_e - 1).astype(I32)
    block_e = jnp.where(blocks < n_used, block_e, block_e[jnp.maximum(n_used - 1, 0)])
    flat = lambda a: a[:, 0, :n_e].reshape(-1).astype(I32)
    gbase3 = pad_start[None, None, :] + gprev3[:, :, :n_e]
    tabs = (flat(cnt3), flat(tstart3), flat(gbase3))

    xs = jnp.zeros((nb * bm, d), F32)
    lslots, o = [], 0
    for (x1, h2, te, tw), n, t0 in zip(groups, sizes, tile0s):
        ls = lslot[t0 * tp:t0 * tp + n]
        lslots.append(ls)
        xs = _dispatch(tabs, t0, ls, h2, xs, n_e)
    yb = _experts(xs, block_e, n_used.reshape(1), w_gate, b_gate, w_up, b_up, w_down, b_down)
    return [_combine(tabs, t0, ls, tw, x1, yb, n_e)
            for (x1, h2, te, tw), ls, t0 in zip(groups, lslots, tile0s)]


def _row_tile(n, want):
    return want if n % want == 0 else n


def kernel(x_prompt, x_sample, mem_prompt, cache_k, cache_v, cache_idx_k, cache_mem_k, cache_mem_v, page_table,
           g_in, w_in, g_qa, g_ka, g_vb, w_s, b_s, g_qm, g_mem, w_mem_kv, g_km, w_br_a, w_br_b, w_br_m, w_out,
           g_ffn, w_router, b_router, w_gate, b_gate, w_up, b_up, w_down, b_down):
    bp, t, d = x_prompt.shape
    bs, ts, _ = x_sample.shape
    assert bp == 1 and ts == 1 and t % CHUNK == 0
    n_mem = mem_prompt.shape[1]
    n_pages = page_table.shape[1]
    ps = cache_k.shape[1]
    past = n_pages * ps
    n_e = w_router.shape[1]

    fp = _prep_params(w_in, g_in, g_qa, g_ka)
    mp = _prep_mix_params(w_in, g_in, g_vb, g_qm, w_br_a, w_br_b, w_br_m, w_out, g_ffn, w_router, b_router)

    xp = x_prompt.reshape(t, d)
    qa_p, k32_p, khm_p, v32_p, vt_p, qi_p, ki32_p, kibf_p, wit_p = _front(
        xp, _rope_tables(jnp.arange(t, dtype=I32)), *fp, tm=_row_tile(t, 512))
    mem_k, mem_v = _memory_kv(mem_prompt.reshape(n_mem, d), g_mem.reshape(1, d), w_mem_kv.astype(BF16),
                              g_km.reshape(1, HEAD_DIM_M))
    out_a_p = _prompt_attention(qi_p, qa_p, wit_p, kibf_p, khm_p, vt_p)
    x1_p, h2_p, te_p, tw_p = _mix(xp, out_a_p, mp, _prep_gmlp_chunked(w_s, b_s), mem_k, mem_v,
                                  tm=_row_tile(t, 256), chunked=True, shared_mem=True, emit_vb=False, n_experts=n_e)

    xs = x_sample.reshape(bs, d)
    qa_s, k32_s, _, v32_s, _, qi_s, ki32_s, _, wit_s = _front(
        xs, _rope_tables(jnp.full((bs,), past, I32)), *fp, tm=bs)
    out_a_s = _sample_attention(page_table, jnp.moveaxis(qi_s, 0, 1), jnp.moveaxis(qa_s, 0, 1),
                                wit_s.T.reshape(bs, N_HEADS_IDX, 1), ki32_s.reshape(bs, 1, D_IDX),
                                k32_s.reshape(bs, 1, WA_KV), v32_s.reshape(bs, 1, WA_KV),
                                cache_idx_k, cache_k, cache_v)
    x1_s, h2_s, te_s, tw_s, vb_s = _mix(xs, out_a_s.reshape(bs, WA_Q), mp, _prep_gmlp_single(w_s, b_s),
                                        cache_mem_k.reshape(bs, n_mem, WM_Q), cache_mem_v.reshape(bs, n_mem, WM_Q),
                                        tm=_row_tile(bs, SUBLANES), chunked=False, shared_mem=False, emit_vb=True,
                                        n_experts=n_e)

    y_p, y_s = _moe([(x1_p, h2_p, te_p, tw_p), (x1_s, h2_s, te_s, tw_s)],
                    w_gate, b_gate, w_up, b_up, w_down, b_down)

    n_pg = t // ps
    return (y_p.reshape(1, t, d), y_s.reshape(bs, 1, d),
            k32_p.reshape(1, n_pg, ps, N_KV_A, HEAD_DIM_A), v32_p.reshape(1, n_pg, ps, N_KV_A, HEAD_DIM_A),
            ki32_p.reshape(1, n_pg, ps, D_IDX),
            mem_k.reshape(1, n_mem, N_HEADS_M, HEAD_DIM_M), mem_v.reshape(1, n_mem, N_HEADS_M, HEAD_DIM_M),
            k32_s.reshape(bs, 1, N_KV_A, HEAD_DIM_A), v32_s.reshape(bs, 1, N_KV_A, HEAD_DIM_A),
            ki32_s.reshape(bs, 1, D_IDX), vb_s.reshape(bs, 1, WIDTH_B))
```

```python
import functools

import jax
import jax.numpy as jnp
from jax import lax
from jax.experimental import pallas as pl
from jax.experimental.pallas import tpu as pltpu

F32 = jnp.float32
BF16 = jnp.bfloat16
I32 = jnp.int32

N_HEADS_A = 8
N_KV_A = 2
HEAD_DIM_A = 64
TOPK_MAX = 256
N_HEADS_IDX = 8
D_IDX = 64
N_GROUPS_B = 4
GROUP_DIM_B = 128
WIDTH_B = N_GROUPS_B * GROUP_DIM_B
CHUNK = 128
N_HEADS_M = 4
HEAD_DIM_M = 128
ROPE_THETA = 500000.0
ROT_FRACTION = 4
N_BRANCH = 3
TOP_K_MOE = 4
SWIGLU_LIMIT = 7.0
SWIGLU_ALPHA = 1.702
EPS = 1e-6

WA_Q = N_HEADS_A * HEAD_DIM_A
WA_KV = N_KV_A * HEAD_DIM_A
WI_Q = N_HEADS_IDX * D_IDX
WM_Q = N_HEADS_M * HEAD_DIM_M
GQA = N_HEADS_A // N_KV_A

LANES = 128
SUBLANES = 8
VMEM_LIMIT = 56 * 1024 * 1024

LOG2_E = 1.4426950408889634
INT_MIN = -(2 ** 31)
INT_MAX = 2 ** 31 - 1
NEG = -1e30

VT_ROWS = HEAD_DIM_A + 16
ATTN_BQ = 128
ATTN_SC = 512
MOE_BM = 512
FF_CHUNK = 256


def _cparams(n_axes=1, vmem=VMEM_LIMIT):
    return pltpu.CompilerParams(dimension_semantics=("arbitrary",) * n_axes, vmem_limit_bytes=vmem)


def _const_spec(shape, single=False):
    zeros = (0,) * len(shape)
    if single:
        return pl.BlockSpec(shape, lambda *_: zeros, pipeline_mode=pl.Buffered(1))
    return pl.BlockSpec(shape, lambda *_: zeros)


def _dot(a, b):
    return jnp.dot(a, b, preferred_element_type=F32)


def _dot_nt(a, b):
    return lax.dot_general(a, b, (((1,), (1,)), ((), ())), preferred_element_type=F32)


def _rms(x, g):
    return x * lax.rsqrt(jnp.mean(x * x, axis=-1, keepdims=True) + EPS) * g


def _split_dot(x, w):
    hi = x.astype(BF16)
    lo = (x - hi.astype(F32)).astype(BF16)
    return _dot(hi, w) + _dot(lo, w)


def _front_kernel(x_ref, gin_ref, wa_ref, wwit_ref, wvt_ref, bd_ref, gqa_ref, gka_ref, cos_ref, sa_ref, sb_ref,
                  qa_ref, k32_ref, khm_ref, v32_ref, vt_ref, qi_ref, ki32_ref, kibf_ref, wit_ref, *, score_scale):
    x = x_ref[...]
    h = _rms(x, gin_ref[...]).astype(BF16)
    p = _dot(h, wa_ref[...])
    cos, sa, sb = cos_ref[...], sa_ref[...], sb_ref[...]
    bd = bd_ref[...]

    def head_norm(v, g):
        ssq = _split_dot(v * v, bd)
        return v * lax.rsqrt(ssq * (1.0 / HEAD_DIM_A) + EPS) * g

    def rope(v):
        return v * cos + pltpu.roll(v, LANES - 8, 1) * sa + pltpu.roll(v, 8, 1) * sb

    for s in range(WA_Q // LANES):
        v = p[:, s * LANES:(s + 1) * LANES]
        v = rope(head_norm(v, gqa_ref[...])) * (HEAD_DIM_A ** -0.5 * LOG2_E)
        vb = v.astype(BF16)
        qa_ref[2 * s] = vb[:, :HEAD_DIM_A]
        qa_ref[2 * s + 1] = vb[:, HEAD_DIM_A:]
    o = WA_Q
    k = rope(head_norm(p[:, o:o + WA_KV], gka_ref[...]))
    k32_ref[...] = k
    kb = k.astype(BF16)
    khm_ref[0] = kb[:, :HEAD_DIM_A]
    khm_ref[1] = kb[:, HEAD_DIM_A:]
    o += WA_KV
    v32_ref[...] = p[:, o:o + WA_KV]
    o += WA_KV
    for s in range(WI_Q // LANES):
        vb = rope(p[:, o + s * LANES:o + (s + 1) * LANES]).astype(BF16)
        qi_ref[2 * s] = vb[:, :D_IDX]
        qi_ref[2 * s + 1] = vb[:, D_IDX:]
    o += WI_Q
    ki = rope(p[:, o:o + LANES])[:, :D_IDX]
    ki32_ref[...] = ki
    kibf_ref[...] = ki.astype(BF16)
    wit_ref[...] = _dot_nt(wwit_ref[...], h) * score_scale
    vt = _dot_nt(wvt_ref[...], h)
    tail = jnp.where(lax.broadcasted_iota(I32, (VT_ROWS - HEAD_DIM_A, vt.shape[1]), 0) == 0, 1.0, 0.0)
    for g in range(N_KV_A):
        vt_ref[g] = jnp.concatenate([vt[g * HEAD_DIM_A:(g + 1) * HEAD_DIM_A], tail], axis=0).astype(BF16)


def _front(x2d, tabs, gin, wa, wwit, wvt, bd, gqa, gka, tm):
    n, d = x2d.shape
    cos, sa, sb = tabs
    row = lambda w: pl.BlockSpec((tm, w), lambda i: (i, 0))
    hm = lambda nh, w: pl.BlockSpec((nh, tm, w), lambda i: (0, i, 0))
    out_shape = (
        jax.ShapeDtypeStruct((N_HEADS_A, n, HEAD_DIM_A), BF16),
        jax.ShapeDtypeStruct((n, WA_KV), F32),
        jax.ShapeDtypeStruct((N_KV_A, n, HEAD_DIM_A), BF16),
        jax.ShapeDtypeStruct((n, WA_KV), F32),
        jax.ShapeDtypeStruct((N_KV_A, VT_ROWS, n), BF16),
        jax.ShapeDtypeStruct((N_HEADS_IDX, n, D_IDX), BF16),
        jax.ShapeDtypeStruct((n, D_IDX), F32),
        jax.ShapeDtypeStruct((n, D_IDX), BF16),
        jax.ShapeDtypeStruct((N_HEADS_IDX, n), F32),
    )
    out_specs = (hm(N_HEADS_A, HEAD_DIM_A), row(WA_KV), hm(N_KV_A, HEAD_DIM_A), row(WA_KV),
                 pl.BlockSpec((N_KV_A, VT_ROWS, tm), lambda i: (0, 0, i)), hm(N_HEADS_IDX, D_IDX), row(D_IDX), row(D_IDX),
                 pl.BlockSpec((N_HEADS_IDX, tm), lambda i: (0, i)))
    in_specs = [row(d), _const_spec(gin.shape), _const_spec(wa.shape), _const_spec(wwit.shape),
                _const_spec(wvt.shape), _const_spec(bd.shape), _const_spec(gqa.shape), _const_spec(gka.shape),
                row(LANES), row(LANES), row(LANES)]
    kern = functools.partial(_front_kernel, score_scale=D_IDX ** -0.5 * N_HEADS_IDX ** -0.5)
    return pl.pallas_call(kern, out_shape=out_shape, grid=(n // tm,), in_specs=in_specs, out_specs=out_specs,
                          compiler_params=_cparams(), name="front")(x2d, gin, wa, wwit, wvt, bd, gqa, gka, cos, sa, sb)


def _memkv_kernel(mem_ref, gmem_ref, w_ref, gkm_ref, k32_ref, v32_ref):
    h = _rms(mem_ref[...], gmem_ref[...]).astype(BF16)
    kv = _dot(h, w_ref[...])
    for hh in range(N_HEADS_M):
        sl = slice(hh * HEAD_DIM_M, (hh + 1) * HEAD_DIM_M)
        k32_ref[:, sl] = _rms(kv[:, sl], gkm_ref[...])
    v32_ref[...] = kv[:, WM_Q:]


def _memory_kv(mem2d, gmem, w, gkm):
    n = mem2d.shape[0]
    out = jax.ShapeDtypeStruct((n, WM_Q), F32)
    return pl.pallas_call(_memkv_kernel, out_shape=(out, out), name="memory_kv",
                          compiler_params=pltpu.CompilerParams(vmem_limit_bytes=VMEM_LIMIT))(mem2d, gmem, w, gkm)


def _sort_key(s):
    s = jnp.where(s == 0.0, 0.0, s)
    bits = pltpu.bitcast(s, I32)
    return bits ^ ((bits >> 31) & INT_MAX)


KEYS_PER_GROUP = 32 * SUBLANES
SEARCH_SLAB = 64


def _bit_planes(words):
    a = list(words)
    j, m = 16, 0x0000FFFF
    while j:
        k = 0
        while k < 32:
            t = (a[k] ^ lax.shift_right_logical(a[k + j], jnp.int32(j))) & m
            a[k] = a[k] ^ t
            a[k + j] = a[k + j] ^ (t << j)
            k = (k + j + 1) & ~j
        j >>= 1
        m = (m ^ (m << j)) & 0xFFFFFFFF if j else m
    return a


ATTN_SUB = 128


def _attn_kernel(qi_ref, qa_ref, wi_ref, ki_ref, k_ref, vt_ref, out_ref,
                 keys_ref, planes_ref, cand_ref, cut_ref, bias_ref, lg_ref, cm_ref, m_ref, acc_ref, *, k_sel, idx_bits):
    bq, sc = ATTN_BQ, ATTN_SC
    wpc = sc // 32
    i = pl.program_id(0)
    n_ch = (i * bq) // sc + 1
    t_pos = i * bq + lax.broadcasted_iota(I32, (1, bq), 1)
    sub = lax.broadcasted_iota(I32, (sc, 1), 0)
    qi = qi_ref[...].reshape(N_HEADS_IDX * bq, D_IDX)
    w = wi_ref[...]

    @pl.when(i == 0)
    def _():
        planes_ref[...] = jnp.zeros(planes_ref.shape, I32)

    def chunk(c):
        return pl.ds(pl.multiple_of(c * sc, sc), sc)

    def score_chunk(c, carry):
        off = pl.multiple_of(c * sc, sc)
        d = _dot_nt(ki_ref[chunk(c), :], qi)
        s = jnp.zeros((sc, bq), F32)
        for h in range(N_HEADS_IDX):
            s = s + jnp.maximum(d[:, h * bq:(h + 1) * bq], 0.0) * w[h:h + 1, :]
        key = jnp.where(off + sub <= t_pos, _sort_key(s), INT_MIN)
        keys_ref[chunk(c), :] = key
        u = key ^ INT_MIN
        for grp in range(sc // KEYS_PER_GROUP):
            base = grp * KEYS_PER_GROUP
            words = _bit_planes([u[base + SUBLANES * v:base + SUBLANES * (v + 1), :] for v in range(32)])
            rows = pl.ds(pl.multiple_of(c * wpc + grp * SUBLANES, SUBLANES), SUBLANES)
            for p in range(32):
                planes_ref[p, rows, :] = words[p]
        return carry

    lax.fori_loop(0, n_ch, score_chunk, 0)

    def count(pred):
        def body(c, acc):
            hit = jnp.where(pred(keys_ref[chunk(c), :], c * sc + sub), 1, 0)
            return acc + hit.reshape(sc // SUBLANES, SUBLANES, bq).sum(axis=0)

        def body2(j, acc):
            return body(2 * j + 1, body(2 * j, acc))
        acc = lax.fori_loop(0, n_ch // 2, body2, jnp.zeros((SUBLANES, bq), I32))
        acc = lax.fori_loop(n_ch - n_ch % 2, n_ch, body, acc)
        return acc.sum(axis=0, keepdims=True)

    slab = SEARCH_SLAB
    n_slab = (n_ch * wpc + slab - 1) // slab
    srow = lax.broadcasted_iota(I32, (slab, 1), 0)

    def slab_rows(sl):
        return pl.ds(pl.multiple_of(sl * slab, slab), slab)

    def init_cand(sl, carry):
        cand_ref[slab_rows(sl), :] = jnp.where(sl * slab + srow < n_ch * wpc, -1, 0) + jnp.zeros((slab, bq), I32)
        return carry

    lax.fori_loop(0, n_slab, init_cand, 0)

    def sweep(prev, cur):
        def body(sl, acc):
            r = slab_rows(sl)
            cand = cand_ref[r, :]
            if prev is not None:
                ones = cand & planes_ref[prev[0], r, :]
                cand = jnp.where(prev[1] != 0, ones, cand ^ ones)
                cand_ref[r, :] = cand
            hits = cand if cur is None else cand & planes_ref[cur, r, :]
            cnt = lax.population_count(hits)
            return acc + cnt.reshape(slab // SUBLANES, SUBLANES, bq).sum(axis=0)
        acc = lax.fori_loop(0, n_slab, body, jnp.zeros((SUBLANES, bq), I32))
        return acc.sum(axis=0, keepdims=True)

    def decide(ones, p, need, thr_u):
        keep = ones >= need
        bit = jnp.int32(1) << (31 - p)
        return jnp.where(keep, 1, 0), jnp.where(keep, need, need - ones), jnp.where(keep, thr_u | bit, thr_u)

    keep, need, thr_u = decide(sweep(None, 0), 0, jnp.full((1, bq), k_sel, I32), jnp.zeros((1, bq), I32))

    def search_pass(p, st):
        keep, need, thr_u = st
        return decide(sweep((p - 1, keep), p), p, need, thr_u)

    keep, need, thr_u = lax.fori_loop(1, 32, search_pass, (keep, need, thr_u))
    tied = sweep((31, keep), None)
    thr = thr_u ^ INT_MIN
    cnt = (k_sel - need) + tied
    cut_ref[...] = jnp.full((1, bq), INT_MAX, I32)

    excess = jnp.logical_and(cnt > k_sel, thr > INT_MIN)
    has_ties = jnp.max(jnp.where(excess, 1, 0)) > 0

    @pl.when(has_ties)
    def _():
        want = k_sel - count(lambda blk, _: blk > thr)

        def tie_body(j, cpos):
            cand = cpos + (jnp.int32(1) << (idx_bits - 1 - j))
            below = count(lambda blk, pos: jnp.where(blk == thr, pos, INT_MAX) < cand)
            return jnp.where(below < want, cand, cpos)

        cpos = lax.fori_loop(0, idx_bits, tie_body, jnp.zeros((1, bq), I32))
        cut_ref[...] = jnp.where(excess, cpos, INT_MAX)

    m_ref[...] = jnp.full(m_ref.shape, NEG, F32)
    acc_ref[...] = jnp.zeros(acc_ref.shape, F32)
    cut = cut_ref[...]
    floor = jnp.where(thr == INT_MIN, INT_MIN, thr - 1)
    n_sub = sc // ATTN_SUB

    qs = [qa_ref[g * GQA:(g + 1) * GQA].reshape(GQA * bq, HEAD_DIM_A) for g in range(N_KV_A)]
    neg_row = jnp.full((1, GQA * bq), NEG, F32)

    def stage(c, par, cb, par_b):
        do_a, do_b = c is not None, cb is not None
        if do_a:
            off = pl.multiple_of(c * sc, sc)
            keyc = keys_ref[chunk(c), :]

            @pl.when(has_ties)
            def _():
                tie = jnp.where(off + sub <= cut, 0.0, NEG)
                bias = jnp.where(keyc > thr, 0.0, jnp.where(keyc == thr, tie, NEG))
                bias_ref[...] = jnp.where(keyc == INT_MIN, NEG, bias)

            @pl.when(jnp.logical_not(has_ties))
            def _():
                bias_ref[...] = jnp.where(keyc > floor, 0.0, NEG)
            cmax = [neg_row] * N_KV_A
        if do_b:
            off_b = pl.multiple_of(cb * sc, sc)
            m_new, acc = [], []
            for g in range(N_KV_A):
                m_old = m_ref[g]
                m_new.append(jnp.maximum(m_old, cm_ref[par_b, g]))
                acc.append(jnp.exp2(m_old - m_new[g]) * acc_ref[g])
                m_ref[g] = m_new[g]
        for r in range(n_sub):
            rows = slice(r * ATTN_SUB, (r + 1) * ATTN_SUB)
            if do_a:
                bias4 = jnp.concatenate([bias_ref[rows, :]] * GQA, axis=1)
                for g in range(N_KV_A):
                    lg = _dot_nt(k_ref[g, pl.ds(off + r * ATTN_SUB, ATTN_SUB), :], qs[g]) + bias4
                    lg_ref[par, g, rows, :] = lg
                    cmax[g] = jnp.maximum(cmax[g], lg.max(axis=0, keepdims=True))
            if do_b:
                for g in range(N_KV_A):
                    p = jnp.exp2(lg_ref[par_b, g, rows, :] - m_new[g])
                    vt = vt_ref[g, :, pl.ds(off_b + r * ATTN_SUB, ATTN_SUB)]
                    acc[g] = acc[g] + _dot(vt, p.astype(BF16))
        if do_b:
            for g in range(N_KV_A):
                acc_ref[g] = acc[g]
        if do_a:
            for g in range(N_KV_A):
                cm_ref[par, g] = cmax[g]

    stage(jnp.int32(0), 0, None, None)

    def stage_pair(j, carry):
        stage(2 * j + 1, 1, 2 * j, 0)
        stage(2 * j + 2, 0, 2 * j + 1, 1)
        return carry

    last = n_ch - 1
    lax.fori_loop(0, last // 2, stage_pair, 0)

    @pl.when(last % 2 == 1)
    def _():
        stage(last, 1, last - 1, 0)
        stage(None, None, last, 1)

    @pl.when(last % 2 == 0)
    def _():
        stage(None, None, last, 0)

    heads = []
    for g in range(N_KV_A):
        acc = acc_ref[g]
        o = acc[:HEAD_DIM_A] / acc[HEAD_DIM_A:HEAD_DIM_A + 1]
        heads += [o[:, hh * bq:(hh + 1) * bq] for hh in range(GQA)]
    out_ref[...] = jnp.concatenate(heads, axis=0).T.astype(out_ref.dtype)


def _prompt_attention(qi_hm, qa_hm, wit, kibf, khm, vt):
    t = kibf.shape[0]
    bq = ATTN_BQ
    k_sel = min(TOPK_MAX, t // 4)
    kern = functools.partial(_attn_kernel, k_sel=k_sel, idx_bits=max(1, (t - 1).bit_length()))
    in_specs = [pl.BlockSpec((N_HEADS_IDX, bq, D_IDX), lambda i: (0, i, 0)),
                pl.BlockSpec((N_HEADS_A, bq, HEAD_DIM_A), lambda i: (0, i, 0)),
                pl.BlockSpec((N_HEADS_IDX, bq), lambda i: (0, i)),
                _const_spec(kibf.shape, True), _const_spec(khm.shape, True), _const_spec(vt.shape, True)]
    n_words = -(-(t // 32) // SEARCH_SLAB) * SEARCH_SLAB
    scratch = [pltpu.VMEM((t, bq), I32), pltpu.VMEM((32, n_words, bq), I32), pltpu.VMEM((n_words, bq), I32),
               pltpu.VMEM((1, bq), I32), pltpu.VMEM((ATTN_SC, bq), F32),
               pltpu.VMEM((2, N_KV_A, ATTN_SC, GQA * bq), F32), pltpu.VMEM((2, N_KV_A, 1, GQA * bq), F32),
               pltpu.VMEM((N_KV_A, 1, GQA * bq), F32),
               pltpu.VMEM((N_KV_A, VT_ROWS, GQA * bq), F32)]
    return pl.pallas_call(kern, out_shape=jax.ShapeDtypeStruct((t, WA_Q), BF16), grid=(t // bq,),
                          in_specs=in_specs, out_specs=pl.BlockSpec((bq, WA_Q), lambda i: (i, 0)),
                          scratch_shapes=scratch, compiler_params=_cparams(), name="attn")(
                              qi_hm, qa_hm, wit, kibf, khm, vt)


SAMPLE_CK = 2048


def _sample_attn_kernel(pt_ref, qi_ref, qa_ref, wi_ref, kin_ref, kn_ref, vn_ref, cidx_hbm, ck_hbm, cv_hbm, out_ref,
                        idx_buf, k_buf, v_buf, scr_ref, sems, *, n_pages, ps, ck, k_sel, idx_bits):
    b = pl.program_id(0)
    nb = pl.num_programs(0)
    past = n_pages * ps
    nc = past // ck

    def page_copy(which, bb, p):
        src, dst = ((cidx_hbm, idx_buf), (ck_hbm, k_buf), (cv_hbm, v_buf))[which]
        cols = pl.ds(pl.multiple_of(p * ps, ps), ps)
        dst = dst.at[:, cols] if which == 0 else dst.at[:, :, cols]
        return pltpu.make_async_copy(src.at[pt_ref[bb, p]], dst, sems.at[which])

    def for_pages(fn):
        lax.fori_loop(0, n_pages, lambda p, c: (fn(p), c)[1], 0)

    @pl.when(b == 0)
    def _():
        for_pages(lambda p: page_copy(0, b, p).start())

    for_pages(lambda p: (page_copy(1, b, p).start(), page_copy(2, b, p).start()))
    for_pages(lambda p: page_copy(0, b, p).wait())

    qi = qi_ref[0]
    w = wi_ref[0]
    for c in range(nc):
        d = _dot(qi, idx_buf[:, c * ck:(c + 1) * ck].astype(BF16))
        scr_ref[c:c + 1, :] = (jnp.maximum(d, 0.0) * w).sum(axis=0, keepdims=True)
    d_new = (qi.astype(F32) * kin_ref[0].astype(BF16).astype(F32)).sum(axis=-1, keepdims=True)
    s_new = (jnp.maximum(d_new, 0.0) * w).sum(axis=0, keepdims=True)

    @pl.when(b + 1 < nb)
    def _():
        for_pages(lambda p: page_copy(0, b + 1, p).start())

    keys = _sort_key(scr_ref[...])
    key_new = _sort_key(s_new)
    pos = (lax.broadcasted_iota(I32, keys.shape, 0) * ck + lax.broadcasted_iota(I32, keys.shape, 1))

    def count(pred):
        hit = jnp.where(pred(keys, pos), 1, 0).sum(axis=1, keepdims=True).sum(axis=0, keepdims=True)
        return hit + jnp.where(pred(key_new, past), 1, 0)

    def search_cond(st):
        bit, _, cnt = st
        return jnp.logical_and(bit >= 0, jnp.max(jnp.where(cnt == k_sel, 0, 1)) > 0)

    def search_body(st):
        bit, thr, cnt = st
        cand = thr + (jnp.int32(1) << bit)
        c = count(lambda kk, _: kk >= cand)
        take = c >= k_sel
        return bit - 1, jnp.where(take, cand, thr), jnp.where(take, c, cnt)

    st0 = (jnp.int32(31), jnp.full((1, 1), INT_MIN, I32), jnp.full((1, 1), past + 1, I32))
    _, thr, cnt = lax.while_loop(search_cond, search_body, st0)

    want = k_sel - count(lambda kk, _: kk > thr)

    def tie_body(j, cpos):
        cand = cpos + (jnp.int32(1) << (idx_bits - 1 - j))
        below = count(lambda kk, pp: jnp.where(kk == thr, pp, INT_MAX) < cand)
        return jnp.where(below < want, cand, cpos)

    cpos = lax.fori_loop(0, idx_bits, tie_body, jnp.zeros((1, 1), I32))
    cut = jnp.where(cnt > k_sel, cpos, INT_MAX)

    def sel_bias(kk, pp):
        tie = jnp.where(pp <= cut, 0.0, NEG)
        return jnp.where(kk > thr, 0.0, jnp.where(kk == thr, tie, NEG))

    bias = sel_bias(keys, pos)
    bias_new = sel_bias(key_new, past)

    for_pages(lambda p: (page_copy(1, b, p).wait(), page_copy(2, b, p).wait()))

    qa = qa_ref[0]
    heads = []
    for g in range(N_KV_A):
        qg = qa[g * GQA:(g + 1) * GQA]
        cols = slice(g * HEAD_DIM_A, (g + 1) * HEAD_DIM_A)
        m = jnp.full((GQA, 1), NEG, F32)
        l = jnp.zeros((GQA, 1), F32)
        acc = jnp.zeros((GQA, HEAD_DIM_A), F32)
        for c in range(nc):
            span = slice(c * ck, (c + 1) * ck)
            lg = _dot(qg, k_buf[g, :, span].astype(BF16)) + bias[c:c + 1, :]
            m_new = jnp.maximum(m, lg.max(axis=-1, keepdims=True))
            p = jnp.exp2(lg - m_new)
            alpha = jnp.exp2(m - m_new)
            l = alpha * l + p.sum(axis=-1, keepdims=True)
            acc = alpha * acc + _dot_nt(p.astype(BF16), v_buf[g, :, span].astype(BF16))
            m = m_new
        kn = kn_ref[0][:, cols].astype(BF16).astype(F32)
        vn = vn_ref[0][:, cols].astype(BF16).astype(F32)
        lg = (qg.astype(F32) * kn).sum(axis=-1, keepdims=True) + bias_new
        m_new = jnp.maximum(m, lg)
        p = jnp.exp2(lg - m_new)
        alpha = jnp.exp2(m - m_new)
        l = alpha * l + p
        acc = alpha * acc + p * vn
        o = acc / l
        heads += [o[hh:hh + 1, :] for hh in range(GQA)]
    out_ref[0] = jnp.concatenate(heads, axis=1)


def _sample_attention(page_table, qi_s, qa_s, wi_s, ki_new, k_new, v_new, cache_idx_k, cache_k, cache_v):
    nbatch, n_pages = page_table.shape
    n_pool, ps, d_idx = cache_idx_k.shape
    past = n_pages * ps
    ck = min(SAMPLE_CK, past)
    assert past % ck == 0 and ck % ps == 0
    k_sel = min(TOPK_MAX, (past + 1) // 4)
    kern = functools.partial(_sample_attn_kernel, n_pages=n_pages, ps=ps, ck=ck, k_sel=k_sel,
                             idx_bits=past.bit_length())
    per_b = lambda shape: pl.BlockSpec((1,) + shape, lambda b, pt: (b, 0, 0))
    any_spec = pl.BlockSpec(memory_space=pl.ANY)
    grid_spec = pltpu.PrefetchScalarGridSpec(
        num_scalar_prefetch=1, grid=(nbatch,),
        in_specs=[per_b((N_HEADS_IDX, d_idx)), per_b((N_HEADS_A, HEAD_DIM_A)), per_b((N_HEADS_IDX, 1)),
                  per_b((1, d_idx)), per_b((1, WA_KV)), per_b((1, WA_KV)), any_spec, any_spec, any_spec],
        out_specs=per_b((1, WA_Q)),
        scratch_shapes=[pltpu.VMEM((d_idx, past), F32), pltpu.VMEM((N_KV_A, HEAD_DIM_A, past), F32),
                        pltpu.VMEM((N_KV_A, HEAD_DIM_A, past), F32),
                        pltpu.VMEM((past // ck, ck), F32), pltpu.SemaphoreType.DMA((3,))])
    return pl.pallas_call(kern, out_shape=jax.ShapeDtypeStruct((nbatch, 1, WA_Q), F32), grid_spec=grid_spec,
                          compiler_params=_cparams(), name="sample_attn")(
                              page_table, qi_s, qa_s, wi_s, ki_new, k_new, v_new,
                              jnp.transpose(cache_idx_k, (0, 2, 1)), jnp.transpose(cache_k, (0, 2, 3, 1)),
                              jnp.transpose(cache_v, (0, 2, 3, 1)))


def _rope_tables(pos):
    rot = HEAD_DIM_A // ROT_FRACTION
    half = rot // 2
    inv_freq = ROPE_THETA ** (-jnp.arange(half, dtype=F32) / half)
    ang = pos.astype(F32)[:, None] * inv_freq[None, :]
    cos, sin = jnp.cos(ang), jnp.sin(ang)
    n = pos.shape[0]
    z = lambda w: jnp.zeros((n, w), F32)
    c = jnp.concatenate([cos, cos, jnp.ones((n, HEAD_DIM_A - rot), F32)], axis=1)
    sa = jnp.concatenate([-sin, z(HEAD_DIM_A - half)], axis=1)
    sb = jnp.concatenate([z(half), sin, z(HEAD_DIM_A - rot)], axis=1)
    rep = LANES // HEAD_DIM_A
    return tuple(jnp.tile(a, (1, rep)) for a in (c, sa, sb))


def _prep_params(w_in, g_in, g_qa, g_ka):
    d = w_in.shape[0]
    w_qa, w_ka, w_va, w_qi, w_ki, w_wi = _split_w_in(w_in, d)[:6]
    wa = jnp.concatenate([w_qa, w_ka, w_va, w_qi, w_ki, jnp.zeros((d, LANES - D_IDX), w_in.dtype)], axis=1)
    lane = jnp.arange(LANES)
    bd = (lane[:, None] // HEAD_DIM_A == lane[None, :] // HEAD_DIM_A).astype(BF16)
    rep = LANES // HEAD_DIM_A
    return (g_in.reshape(1, d), wa.astype(BF16), w_wi.T.astype(BF16), w_va.T.astype(BF16), bd,
            jnp.tile(g_qa, rep).reshape(1, LANES), jnp.tile(g_ka, rep).reshape(1, LANES))


def _split_w_in(w_in, d):
    sizes = (WA_Q, WA_KV, WA_KV, WI_Q, D_IDX, N_HEADS_IDX, WIDTH_B, WIDTH_B, WM_Q, N_BRANCH * d)
    parts, o = [], 0
    for s in sizes:
        parts.append(w_in[:, o:o + s])
        o += s
    return parts


def _mix_kernel(*refs, chunked, shared_mem, emit_vb, n_experts):
    it = iter(refs)
    x_ref, oa_ref, gin_ref, wb_ref, gvb_ref, gqm_ref = (next(it) for _ in range(6))
    if chunked:
        wtril_ref, bt_ref = next(it), next(it)
    else:
        wdiag_ref, bdiag_ref = next(it), next(it)
    mk_ref, mv_ref = next(it), next(it)
    wbra_ref, wbrb_ref, wbrm_ref, wout_ref, gffn_ref, wrh_ref, wrl_ref, br_ref = (next(it) for _ in range(8))
    x1_ref, h2_ref, te_ref, tw_ref = (next(it) for _ in range(4))
    vb_ref = next(it) if emit_vb else None

    x = x_ref[...]
    tm, d = x.shape
    h = _rms(x, gin_ref[...]).astype(BF16)
    p = _dot(h, wb_ref[...])
    ub = p[:, :WIDTH_B]
    vb = _rms(p[:, WIDTH_B:2 * WIDTH_B], gvb_ref[...])
    if emit_vb:
        vb_ref[...] = vb
    o = 2 * WIDTH_B
    qm = p[:, o:o + WM_Q]
    o += WM_Q
    gates = p[:, o:o + N_BRANCH * d]

    if chunked:
        vbb = vb.astype(BF16)
        rows = []
        for cc in range(tm // CHUNK):
            cols = []
            for g in range(N_GROUPS_B):
                vg = vbb[cc * CHUNK:(cc + 1) * CHUNK, g * GROUP_DIM_B:(g + 1) * GROUP_DIM_B]
                cols.append(_dot(wtril_ref[g], vg))
            rows.append(jnp.concatenate(cols, axis=1) + bt_ref[...])
        z = jnp.concatenate(rows, axis=0) if len(rows) > 1 else rows[0]
    else:
        z = vb * wdiag_ref[...] + bdiag_ref[...]
    out_b = ub * z

    scale_m = HEAD_DIM_M ** -0.5
    outs = []
    for hh in range(N_HEADS_M):
        sl = slice(hh * HEAD_DIM_M, (hh + 1) * HEAD_DIM_M)
        qh = _rms(qm[:, sl], gqm_ref[...])
        if shared_mem:
            lg = _dot_nt(qh.astype(BF16), mk_ref[:, sl].astype(BF16)) * scale_m
            pm = jnp.exp(lg - lg.max(axis=-1, keepdims=True))
            pm = pm / pm.sum(axis=-1, keepdims=True)
            outs.append(_dot(pm.astype(BF16), mv_ref[:, sl].astype(BF16)))
        else:
            per_row = []
            for r in range(tm):
                kr = mk_ref[r, :, sl]
                lg = (kr * qh[r:r + 1, :]).sum(axis=-1, keepdims=True) * scale_m
                pm = jnp.exp(lg - lg.max(axis=0, keepdims=True))
                pm = pm / pm.sum(axis=0, keepdims=True)
                per_row.append((pm * mv_ref[r, :, sl]).sum(axis=0, keepdims=True))
            outs.append(jnp.concatenate(per_row, axis=0))
    out_m = jnp.concatenate(outs, axis=1)

    sig = jax.nn.sigmoid
    merged = (sig(gates[:, :d]) * _dot(oa_ref[...].astype(BF16), wbra_ref[...])
              + sig(gates[:, d:2 * d]) * _dot(out_b.astype(BF16), wbrb_ref[...])
              + sig(gates[:, 2 * d:]) * _dot(out_m.astype(BF16), wbrm_ref[...]))
    x1 = x + _dot(merged.astype(BF16), wout_ref[...])
    x1_ref[...] = x1
    h2 = _rms(x1, gffn_ref[...])
    h2_ref[...] = h2.astype(h2_ref.dtype)

    hi = h2.astype(BF16)
    lo = (h2 - hi.astype(F32)).astype(BF16)
    lg = _dot(hi, wrh_ref[...]) + _dot(hi, wrl_ref[...]) + _dot(lo, wrh_ref[...]) + br_ref[...]
    lane = lax.broadcasted_iota(I32, lg.shape, 1)
    lg = jnp.where(lane < n_experts, lg, -jnp.inf)
    vals, idxs = [], []
    for _ in range(TOP_K_MOE):
        m = lg.max(axis=-1, keepdims=True)
        idx = jnp.where(lg == m, lane, LANES).min(axis=-1, keepdims=True)
        vals.append(m)
        idxs.append(idx)
        lg = jnp.where(lane == idx, -jnp.inf, lg)
    ex = [jnp.exp(v - vals[0]) for v in vals]
    den = ex[0]
    for e in ex[1:]:
        den = den + e
    te = jnp.full(lane.shape, -1, I32)
    tw = jnp.zeros(lane.shape, F32)
    for j in range(TOP_K_MOE):
        te = jnp.where(lane == j, idxs[j], te)
        tw = jnp.where(lane == j, ex[j] / den, tw)
    te_ref[...] = te
    tw_ref[...] = tw


def _mix(x2d, out_a, mix_params, gmlp_params, mem_k, mem_v, *, tm, chunked, shared_mem, emit_vb, n_experts):
    n, d = x2d.shape
    gin, wb, gvb, gqm, wbra, wbrb, wbrm, wout, gffn, wrh, wrl, br = mix_params
    row = lambda w: pl.BlockSpec((tm, w), lambda i: (i, 0))
    if shared_mem:
        mem_specs = [_const_spec(mem_k.shape), _const_spec(mem_v.shape)]
    else:
        mspec = pl.BlockSpec((tm,) + mem_k.shape[1:], lambda i: (i, 0, 0))
        mem_specs = [mspec, mspec]
    consts = lambda arrs: [_const_spec(a.shape) for a in arrs]
    in_specs = ([row(d), row(WA_Q)] + consts([gin, wb, gvb, gqm]) + consts(gmlp_params) + mem_specs
                + consts([wbra, wbrb, wbrm, wout, gffn, wrh, wrl, br]))
    out_shape = [jax.ShapeDtypeStruct((n, d), F32), jax.ShapeDtypeStruct((n, d), F32),
                 jax.ShapeDtypeStruct((n, LANES), I32), jax.ShapeDtypeStruct((n, LANES), F32)]
    out_specs = [row(d), row(d), row(LANES), row(LANES)]
    if emit_vb:
        out_shape.append(jax.ShapeDtypeStruct((n, WIDTH_B), F32))
        out_specs.append(row(WIDTH_B))
    kern = functools.partial(_mix_kernel, chunked=chunked, shared_mem=shared_mem, emit_vb=emit_vb,
                             n_experts=n_experts)
    return pl.pallas_call(kern, out_shape=tuple(out_shape), grid=(n // tm,), in_specs=in_specs,
                          out_specs=tuple(out_specs), compiler_params=_cparams(), name="mix")(
                              x2d, out_a, gin, wb, gvb, gqm, *gmlp_params, mem_k, mem_v,
                              wbra, wbrb, wbrm, wout, gffn, wrh, wrl, br)


def _prep_mix_params(w_in, g_in, g_vb, g_qm, w_br_a, w_br_b, w_br_m, w_out, g_ffn, w_router, b_router):
    d = w_in.shape[0]
    parts = _split_w_in(w_in, d)
    wb = jnp.concatenate(parts[6:10], axis=1).astype(BF16)
    n_e = w_router.shape[1]
    wr = jnp.pad(w_router, ((0, 0), (0, LANES - n_e)))
    wrh = wr.astype(BF16)
    wrl = (wr - wrh.astype(F32)).astype(BF16)
    br = jnp.pad(b_router, (0, LANES - n_e)).reshape(1, LANES)
    return (g_in.reshape(1, d), wb, g_vb.reshape(1, WIDTH_B), g_qm.reshape(1, HEAD_DIM_M),
            w_br_a.astype(BF16), w_br_b.astype(BF16), w_br_m.astype(BF16), w_out.astype(BF16),
            g_ffn.reshape(1, d), wrh, wrl, br)


def _prep_gmlp_chunked(w_s, b_s):
    tril = jnp.tril(jnp.ones((CHUNK, CHUNK), w_s.dtype))
    wtril = (w_s * tril).astype(BF16)
    bt = jnp.repeat(b_s.T, GROUP_DIM_B, axis=1)
    return wtril, bt


def _prep_gmlp_single(w_s, b_s):
    wdiag = jnp.repeat(w_s[:, 0, 0], GROUP_DIM_B).reshape(1, WIDTH_B)
    bdiag = jnp.repeat(b_s[:, 0], GROUP_DIM_B).reshape(1, WIDTH_B)
    return wdiag, bdiag


MOE_TP = 256


def _plan_kernel(te_ref, tri_ref, upper_ref, lslot_ref, cnt_ref, tstart_ref, gprev_ref, total_ref, carry_ref):
    @pl.when(pl.program_id(0) == 0)
    def _():
        carry_ref[...] = jnp.zeros(carry_ref.shape, F32)

    te = te_ref[...]
    lane = lax.broadcasted_iota(I32, te.shape, 1)
    picks = [te[:, j:j + 1] for j in range(TOP_K_MOE)]
    onehot = jnp.zeros(te.shape, F32)
    for e in picks:
        onehot = onehot + jnp.where(lane == e, 1.0, 0.0)
    before = _dot(tri_ref[...], onehot.astype(BF16))
    cnt = onehot.sum(axis=0, keepdims=True)
    cnt = jnp.floor((cnt + (SUBLANES - 1)) * (1.0 / SUBLANES)) * SUBLANES
    cnt8 = jnp.broadcast_to(cnt, (SUBLANES, LANES)).astype(BF16)
    tstart = _dot(cnt8, upper_ref[...])[0:1, :]
    lslot = jnp.zeros(te.shape, I32)
    for j, e in enumerate(picks):
        r = jnp.where(lane == e, before + tstart, 0.0).sum(axis=-1, keepdims=True).astype(I32)
        lslot = jnp.where(lane == j, r, lslot)
    lslot_ref[...] = lslot
    cnt_ref[0] = cnt.astype(I32)
    tstart_ref[0] = tstart.astype(I32)
    gprev_ref[0] = carry_ref[...].astype(I32)
    carry_ref[...] = carry_ref[...] + cnt
    total_ref[...] = carry_ref[...].astype(I32)


def _moe_plan(te_all):
    n = te_all.shape[0]
    tp = MOE_TP
    n_tiles = n // tp
    r = jnp.arange(tp)
    tri = (r[:, None] > r[None, :]).astype(BF16)
    e = jnp.arange(LANES)
    upper = (e[:, None] < e[None, :]).astype(BF16)
    per_tile = jax.ShapeDtypeStruct((n_tiles, 1, LANES), I32)
    tile_spec = pl.BlockSpec((1, 1, LANES), lambda i: (i, 0, 0))
    return pl.pallas_call(
        _plan_kernel,
        out_shape=(jax.ShapeDtypeStruct((n, LANES), I32), per_tile, per_tile, per_tile,
                   jax.ShapeDtypeStruct((1, LANES), I32)),
        grid=(n_tiles,),
        in_specs=[pl.BlockSpec((tp, LANES), lambda i: (i, 0)), _const_spec((tp, tp)), _const_spec((LANES, LANES))],
        out_specs=(pl.BlockSpec((tp, LANES), lambda i: (i, 0)), tile_spec, tile_spec, tile_spec,
                   _const_spec((1, LANES))),
        scratch_shapes=[pltpu.VMEM((1, LANES), F32)],
        compiler_params=_cparams(), name="moe_plan")(te_all, tri, upper)


def _run_pieces(tp):
    sizes, s = [], SUBLANES
    while s <= tp:
        sizes.append(s)
        s *= 2
    return tuple(reversed(sizes))


def _local_rows(tp, n_e):
    return tp * TOP_K_MOE + n_e * SUBLANES


def _for_runs(tabs, tile, n_e, tp, fn):
    cnt_s, tstart_s, gbase_s = tabs

    def per_expert(e, c):
        idx = tile * n_e + e
        length, src0, dst0 = cnt_s[idx], tstart_s[idx], gbase_s[idx]
        off = jnp.int32(0)
        for size in _run_pieces(tp):
            piece = length & size

            @pl.when(piece != 0)
            def _():
                fn(pl.multiple_of(src0 + off, SUBLANES), pl.multiple_of(dst0 + off, SUBLANES), size)
            off = off + piece
        return c

    lax.fori_loop(0, n_e, per_expert, 0)


def _slot_matrix(lslot, vals, n_rows):
    col = lax.broadcasted_iota(I32, (lslot.shape[0], n_rows), 1)
    m = jnp.zeros(col.shape, F32)
    for j in range(TOP_K_MOE):
        v = 1.0 if vals is None else vals[:, j:j + 1]
        m = m + jnp.where(lslot[:, j:j + 1] == col, v, 0.0)
    return m


def _dispatch_kernel(cnt_s, tstart_s, gbase_s, lslot_ref, h2_ref, xs_in, xs_out, stage, sem, *, tp, n_e, tile0):
    del xs_in
    n_rows = _local_rows(tp, n_e)
    i = pl.program_id(0)
    slot = i % 2
    pt = _slot_matrix(lslot_ref[...], None, n_rows).astype(BF16)
    stage[slot] = lax.dot_general(pt, h2_ref[...].astype(BF16), (((0,), (0,)), ((), ())),
                                  preferred_element_type=F32)

    def runs(tile, s, op):
        _for_runs((cnt_s, tstart_s, gbase_s), tile0 + tile, n_e, tp,
                  lambda src, dst, size: op(pltpu.make_async_copy(
                      stage.at[s, pl.ds(src, size), :], xs_out.at[pl.ds(dst, size), :], sem.at[s])))

    @pl.when(i > 0)
    def _():
        runs(i - 1, 1 - slot, lambda cp: cp.wait())

    runs(i, slot, lambda cp: cp.start())

    @pl.when(i == pl.num_programs(0) - 1)
    def _():
        runs(i, slot, lambda cp: cp.wait())


def _dispatch(tabs, tile0, lslot, h2, xs, n_e):
    n, d = h2.shape
    tp = min(MOE_TP, n)
    n_rows = _local_rows(tp, n_e)
    kern = functools.partial(_dispatch_kernel, tp=tp, n_e=n_e, tile0=tile0)
    any_spec = pl.BlockSpec(memory_space=pl.ANY)
    grid_spec = pltpu.PrefetchScalarGridSpec(
        num_scalar_prefetch=3, grid=(n // tp,),
        in_specs=[pl.BlockSpec((tp, LANES), lambda i, *_: (i, 0)), pl.BlockSpec((tp, d), lambda i, *_: (i, 0)), any_spec],
        out_specs=any_spec,
        scratch_shapes=[pltpu.VMEM((2, n_rows, d), F32), pltpu.SemaphoreType.DMA((2,))])
    return pl.pallas_call(
        kern, out_shape=jax.ShapeDtypeStruct(xs.shape, xs.dtype), grid_spec=grid_spec,
        input_output_aliases={5: 0},
        compiler_params=pltpu.CompilerParams(dimension_semantics=("arbitrary",), has_side_effects=True,
                                             vmem_limit_bytes=VMEM_LIMIT),
        name="dispatch")(*tabs, lslot, h2, xs)


def _expert_kernel(be_ref, nu_ref, xs_ref, wg_ref, wu_ref, wd_ref, bg_ref, bu_ref, bd_ref, y_ref,
                   wg_bf, wu_bf, wd_bf):
    b = pl.program_id(0)

    @pl.when(b >= nu_ref[0])
    def _():
        y_ref[...] = jnp.zeros(y_ref.shape, F32)

    @pl.when(b < nu_ref[0])
    def _():
        prev = be_ref[jnp.maximum(b - 1, 0)]

        @pl.when(jnp.logical_or(b == 0, be_ref[b] != prev))
        def _():
            wg_bf[...] = wg_ref[0].astype(BF16)
            wu_bf[...] = wu_ref[0].astype(BF16)
            wd_bf[...] = wd_ref[0].astype(BF16)

        xb = xs_ref[...].astype(BF16)
        ff = wg_bf.shape[1]
        y = jnp.zeros(y_ref.shape, F32) + bd_ref[0]
        for n in range(ff // FF_CHUNK):
            sl = slice(n * FF_CHUNK, (n + 1) * FF_CHUNK)
            hg = jnp.minimum(_dot(xb, wg_bf[:, sl]) + bg_ref[0][:, sl], SWIGLU_LIMIT)
            hu = jnp.clip(_dot(xb, wu_bf[:, sl]) + bu_ref[0][:, sl], -SWIGLU_LIMIT, SWIGLU_LIMIT)
            act = hg * jax.nn.sigmoid(SWIGLU_ALPHA * hg) * (hu + 1.0)
            y = y + _dot(act.astype(BF16), wd_bf[sl, :])
        y_ref[...] = y


def _experts(xs, block_e, n_used, w_gate, b_gate, w_up, b_up, w_down, b_down):
    ns, d = xs.shape
    n_e, _, ff = w_gate.shape
    bm = MOE_BM
    blk = lambda b, be, nu: (jnp.minimum(b, nu[0] - 1), 0)
    wsel = lambda b, be, nu: (be[b], 0, 0)
    grid_spec = pltpu.PrefetchScalarGridSpec(
        num_scalar_prefetch=2, grid=(ns // bm,),
        in_specs=[pl.BlockSpec((bm, d), blk),
                  pl.BlockSpec((1, d, ff), wsel), pl.BlockSpec((1, d, ff), wsel), pl.BlockSpec((1, ff, d), wsel),
                  pl.BlockSpec((1, 1, ff), wsel), pl.BlockSpec((1, 1, ff), wsel), pl.BlockSpec((1, 1, d), wsel)],
        out_specs=pl.BlockSpec((bm, d), lambda b, be, nu: (b, 0)),
        scratch_shapes=[pltpu.VMEM((d, ff), BF16), pltpu.VMEM((d, ff), BF16), pltpu.VMEM((ff, d), BF16)])
    return pl.pallas_call(_expert_kernel, out_shape=jax.ShapeDtypeStruct((ns, d), F32), grid_spec=grid_spec,
                          compiler_params=_cparams(), name="experts")(
                              block_e, n_used, xs, w_gate, w_up, w_down,
                              b_gate.reshape(n_e, 1, ff), b_up.reshape(n_e, 1, ff), b_down.reshape(n_e, 1, d))


def _combine_kernel(cnt_s, tstart_s, gbase_s, lslot_ref, tw_ref, x1_ref, yb_hbm, y_ref, buf, sem, *, tp, n_e, tile0):
    n_rows = _local_rows(tp, n_e)
    i = pl.program_id(0)
    slot = i % 2

    def runs(tile, s, op):
        _for_runs((cnt_s, tstart_s, gbase_s), tile0 + tile, n_e, tp,
                  lambda loc, glob, size: op(pltpu.make_async_copy(
                      yb_hbm.at[pl.ds(glob, size), :], buf.at[s, pl.ds(loc, size), :], sem.at[s])))

    @pl.when(i == 0)
    def _():
        buf[...] = jnp.zeros(buf.shape, F32)
        runs(i, slot, lambda cp: cp.start())

    @pl.when(i + 1 < pl.num_programs(0))
    def _():
        runs(i + 1, 1 - slot, lambda cp: cp.start())

    runs(i, slot, lambda cp: cp.wait())

    ptw = _slot_matrix(lslot_ref[...], tw_ref[...], n_rows)
    rows = buf[slot]
    w_hi = ptw.astype(BF16)
    w_lo = (ptw - w_hi.astype(F32)).astype(BF16)
    r_hi = rows.astype(BF16)
    r_lo = (rows - r_hi.astype(F32)).astype(BF16)
    y_ref[...] = x1_ref[...] + _dot(w_hi, r_hi) + (_dot(w_hi, r_lo) + _dot(w_lo, r_hi))


def _combine(tabs, tile0, lslot, tw, x1, yb, n_e):
    n, d = x1.shape
    tp = min(MOE_TP, n)
    n_rows = _local_rows(tp, n_e)
    kern = functools.partial(_combine_kernel, tp=tp, n_e=n_e, tile0=tile0)
    row = lambda w: pl.BlockSpec((tp, w), lambda i, *_: (i, 0))
    grid_spec = pltpu.PrefetchScalarGridSpec(
        num_scalar_prefetch=3, grid=(n // tp,),
        in_specs=[row(LANES), row(LANES), row(d), pl.BlockSpec(memory_space=pl.ANY)],
        out_specs=row(d),
        scratch_shapes=[pltpu.VMEM((2, n_rows, d), F32), pltpu.SemaphoreType.DMA((2,))])
    return pl.pallas_call(kern, out_shape=jax.ShapeDtypeStruct(x1.shape, F32), grid_spec=grid_spec,
                          compiler_params=_cparams(), name="combine")(*tabs, lslot, tw, x1, yb)


def _moe(groups, w_gate, b_gate, w_up, b_up, w_down, b_down):
    n_e = w_gate.shape[0]
    d = groups[0][0].shape[1]
    tp, bm = MOE_TP, MOE_BM
    sizes = [g[0].shape[0] for g in groups]
    assert all(n % tp == 0 or n < tp for n in sizes), sizes
    parts, tile0s, o = [], [], 0
    for g, n in zip(groups, sizes):
        n_pad = -(-n // tp) * tp
        parts.append(jnp.pad(g[2], ((0, n_pad - n), (0, 0)), constant_values=-1))
        tile0s.append(o // tp)
        o += n_pad
    lslot, cnt3, tstart3, gprev3, total = _moe_plan(jnp.concatenate(parts, axis=0))
    counts = total[0, :n_e]
    padded_cnt = (counts + bm - 1) // bm * bm
    pad_end = jnp.cumsum(padded_cnt)
    pad_start = pad_end - padded_cnt
    n_assign = sum(sizes) * TOP_K_MOE
    n_slack = (o // tp) * n_e * (SUBLANES - 1)
    nb = -(-(n_assign + n_slack) // bm) + n_e
    n_used = (pad_end[-1] // bm).astype(I32)
    blocks = jnp.arange(nb, dtype=I32)
    block_e = jnp.minimum((pad_end[None, :] <= (blocks * bm)[:, None]).sum(axis=1), n_e - 1).astype(I32)
    block_e = jnp.where(blocks < n_used, block_e, block_e[jnp.maximum(n_used - 1, 0)])
    flat = lambda a: a[:, 0, :n_e].reshape(-1).astype(I32)
    gbase3 = pad_start[None, None, :] + gprev3[:, :, :n_e]
    tabs = (flat(cnt3), flat(tstart3), flat(gbase3))

    xs = jnp.zeros((nb * bm, d), F32)
    lslots, o = [], 0
    for (x1, h2, te, tw), n, t0 in zip(groups, sizes, tile0s):
        ls = lslot[t0 * tp:t0 * tp + n]
        lslots.append(ls)
        xs = _dispatch(tabs, t0, ls, h2, xs, n_e)
    yb = _experts(xs, block_e, n_used.reshape(1), w_gate, b_gate, w_up, b_up, w_down, b_down)
    return [_combine(tabs, t0, ls, tw, x1, yb, n_e)
            for (x1, h2, te, tw), ls, t0 in zip(groups, lslots, tile0s)]


def _row_tile(n, want):
    return want if n % want == 0 else n


def kernel(x_prompt, x_sample, mem_prompt, cache_k, cache_v, cache_idx_k, cache_mem_k, cache_mem_v, page_table,
           g_in, w_in, g_qa, g_ka, g_vb, w_s, b_s, g_qm, g_mem, w_mem_kv, g_km, w_br_a, w_br_b, w_br_m, w_out,
           g_ffn, w_router, b_router, w_gate, b_gate, w_up, b_up, w_down, b_down):
    bp, t, d = x_prompt.shape
    bs, ts, _ = x_sample.shape
    assert bp == 1 and ts == 1 and t % CHUNK == 0
    n_mem = mem_prompt.shape[1]
    n_pages = page_table.shape[1]
    ps = cache_k.shape[1]
    past = n_pages * ps
    n_e = w_router.shape[1]

    fp = _prep_params(w_in, g_in, g_qa, g_ka)
    mp = _prep_mix_params(w_in, g_in, g_vb, g_qm, w_br_a, w_br_b, w_br_m, w_out, g_ffn, w_router, b_router)

    xp = x_prompt.reshape(t, d)
    qa_p, k32_p, khm_p, v32_p, vt_p, qi_p, ki32_p, kibf_p, wit_p = _front(
        xp, _rope_tables(jnp.arange(t, dtype=I32)), *fp, tm=_row_tile(t, 512))
    mem_k, mem_v = _memory_kv(mem_prompt.reshape(n_mem, d), g_mem.reshape(1, d), w_mem_kv.astype(BF16),
                              g_km.reshape(1, HEAD_DIM_M))
    out_a_p = _prompt_attention(qi_p, qa_p, wit_p, kibf_p, khm_p, vt_p)
    x1_p, h2_p, te_p, tw_p = _mix(xp, out_a_p, mp, _prep_gmlp_chunked(w_s, b_s), mem_k, mem_v,
                                  tm=_row_tile(t, 256), chunked=True, shared_mem=True, emit_vb=False, n_experts=n_e)

    xs = x_sample.reshape(bs, d)
    qa_s, k32_s, _, v32_s, _, qi_s, ki32_s, _, wit_s = _front(
        xs, _rope_tables(jnp.full((bs,), past, I32)), *fp, tm=bs)
    out_a_s = _sample_attention(page_table, jnp.moveaxis(qi_s, 0, 1), jnp.moveaxis(qa_s, 0, 1),
                                wit_s.T.reshape(bs, N_HEADS_IDX, 1), ki32_s.reshape(bs, 1, D_IDX),
                                k32_s.reshape(bs, 1, WA_KV), v32_s.reshape(bs, 1, WA_KV),
                                cache_idx_k, cache_k, cache_v)
    x1_s, h2_s, te_s, tw_s, vb_s = _mix(xs, out_a_s.reshape(bs, WA_Q), mp, _prep_gmlp_single(w_s, b_s),
                                        cache_mem_k.reshape(bs, n_mem, WM_Q), cache_mem_v.reshape(bs, n_mem, WM_Q),
                                        tm=_row_tile(bs, SUBLANES), chunked=False, shared_mem=False, emit_vb=True,
                                        n_experts=n_e)

    y_p, y_s = _moe([(x1_p, h2_p, te_p, tw_p), (x1_s, h2_s, te_s, tw_s)],
                    w_gate, b_gate, w_up, b_up, w_down, b_down)

    n_pg = t // ps
    return (y_p.reshape(1, t, d), y_s.reshape(bs, 1, d),
            k32_p.reshape(1, n_pg, ps, N_KV_A, HEAD_DIM_A), v32_p.reshape(1, n_pg, ps, N_KV_A, HEAD_DIM_A),
            ki32_p.reshape(1, n_pg, ps, D_IDX),
            mem_k.reshape(1, n_mem, N_HEADS_M, HEAD_DIM_M), mem_v.reshape(1, n_mem, N_HEADS_M, HEAD_DIM_M),
            k32_s.reshape(bs, 1, N_KV_A, HEAD_DIM_A), v32_s.reshape(bs, 1, N_KV_A, HEAD_DIM_A),
            ki32_s.reshape(bs, 1, D_IDX), vb_s.reshape(bs, 1, WIDTH_B))
```

```python
import functools

import jax
import jax.numpy as jnp
from jax import lax
from jax.experimental import pallas as pl
from jax.experimental.pallas import tpu as pltpu

F32 = jnp.float32
BF16 = jnp.bfloat16
I32 = jnp.int32

N_HEADS_A = 8
N_KV_A = 2
HEAD_DIM_A = 64
TOPK_MAX = 256
N_HEADS_IDX = 8
D_IDX = 64
N_GROUPS_B = 4
GROUP_DIM_B = 128
WIDTH_B = N_GROUPS_B * GROUP_DIM_B
CHUNK = 128
N_HEADS_M = 4
HEAD_DIM_M = 128
ROPE_THETA = 500000.0
ROT_FRACTION = 4
N_BRANCH = 3
TOP_K_MOE = 4
SWIGLU_LIMIT = 7.0
SWIGLU_ALPHA = 1.702
EPS = 1e-6

WA_Q = N_HEADS_A * HEAD_DIM_A
WA_KV = N_KV_A * HEAD_DIM_A
WI_Q = N_HEADS_IDX * D_IDX
WM_Q = N_HEADS_M * HEAD_DIM_M
GQA = N_HEADS_A // N_KV_A

LANES = 128
SUBLANES = 8
VMEM_LIMIT = 56 * 1024 * 1024

LOG2_E = 1.4426950408889634
INT_MIN = -(2 ** 31)
INT_MAX = 2 ** 31 - 1
NEG = -1e30

VT_ROWS = HEAD_DIM_A + 16
ATTN_BQ = 128
ATTN_SC = 512
MOE_BM = 512
FF_CHUNK = 512


def _cparams(n_axes=1, vmem=VMEM_LIMIT):
    return pltpu.CompilerParams(dimension_semantics=("arbitrary",) * n_axes, vmem_limit_bytes=vmem)


def _const_spec(shape, single=False):
    zeros = (0,) * len(shape)
    if single:
        return pl.BlockSpec(shape, lambda *_: zeros, pipeline_mode=pl.Buffered(1))
    return pl.BlockSpec(shape, lambda *_: zeros)


def _dot(a, b):
    return jnp.dot(a, b, preferred_element_type=F32)


def _dot_nt(a, b):
    return lax.dot_general(a, b, (((1,), (1,)), ((), ())), preferred_element_type=F32)


def _rms(x, g):
    return x * lax.rsqrt(jnp.mean(x * x, axis=-1, keepdims=True) + EPS) * g


def _split_dot(x, w):
    hi = x.astype(BF16)
    lo = (x - hi.astype(F32)).astype(BF16)
    return _dot(hi, w) + _dot(lo, w)


def _front_kernel(x_ref, gin_ref, wa_ref, wwit_ref, wvt_ref, bd_ref, gqa_ref, gka_ref, cos_ref, sa_ref, sb_ref,
                  qa_ref, k32_ref, khm_ref, v32_ref, vt_ref, qi_ref, ki32_ref, kibf_ref, wit_ref, *, score_scale, paged):
    x = x_ref[...]
    h = _rms(x, gin_ref[...]).astype(BF16)
    p = _dot(h, wa_ref[...])
    cos, sa, sb = cos_ref[...], sa_ref[...], sb_ref[...]
    bd = bd_ref[...]

    def head_norm(v, g):
        ssq = _split_dot(v * v, bd)
        return v * lax.rsqrt(ssq * (1.0 / HEAD_DIM_A) + EPS) * g

    def rope(v):
        return v * cos + pltpu.roll(v, LANES - 8, 1) * sa + pltpu.roll(v, 8, 1) * sb

    for s in range(WA_Q // LANES):
        v = p[:, s * LANES:(s + 1) * LANES]
        v = rope(head_norm(v, gqa_ref[...])) * (HEAD_DIM_A ** -0.5 * LOG2_E)
        vb = v.astype(BF16)
        qa_ref[2 * s] = vb[:, :HEAD_DIM_A]
        qa_ref[2 * s + 1] = vb[:, HEAD_DIM_A:]
    o = WA_Q
    n_pg = x.shape[0] // LANES
    k = rope(head_norm(p[:, o:o + WA_KV], gka_ref[...]))
    if paged:
        for j in range(n_pg):
            k32_ref[j] = k[j * LANES:(j + 1) * LANES, :].T
    else:
        k32_ref[...] = k
    kb = k.astype(BF16)
    khm_ref[0] = kb[:, :HEAD_DIM_A]
    khm_ref[1] = kb[:, HEAD_DIM_A:]
    o += WA_KV
    if not paged:
        v32_ref[...] = p[:, o:o + WA_KV]
    o += WA_KV
    for s in range(WI_Q // LANES):
        vb = rope(p[:, o + s * LANES:o + (s + 1) * LANES]).astype(BF16)
        qi_ref[2 * s] = vb[:, :D_IDX]
        qi_ref[2 * s + 1] = vb[:, D_IDX:]
    o += WI_Q
    ki_slab = rope(p[:, o:o + LANES])
    ki = ki_slab[:, :D_IDX]
    if paged:
        for j in range(n_pg):
            ki32_ref[j] = ki_slab[j * LANES:(j + 1) * LANES, :].T[:D_IDX, :]
    else:
        ki32_ref[...] = ki
    kibf_ref[...] = ki.astype(BF16)
    wit_ref[...] = _dot_nt(wwit_ref[...], h) * score_scale
    vt = _dot_nt(wvt_ref[...], h)
    if paged:
        for j in range(n_pg):
            v32_ref[j] = vt[:, j * LANES:(j + 1) * LANES]
    tail = jnp.where(lax.broadcasted_iota(I32, (VT_ROWS - HEAD_DIM_A, vt.shape[1]), 0) == 0, 1.0, 0.0)
    for g in range(N_KV_A):
        vt_ref[g] = jnp.concatenate([vt[g * HEAD_DIM_A:(g + 1) * HEAD_DIM_A], tail], axis=0).astype(BF16)


def _front(x2d, tabs, gin, wa, wwit, wvt, bd, gqa, gka, tm, paged):
    n, d = x2d.shape
    cos, sa, sb = tabs
    row = lambda w: pl.BlockSpec((tm, w), lambda i: (i, 0))
    hm = lambda nh, w: pl.BlockSpec((nh, tm, w), lambda i: (0, i, 0))
    if paged:
        assert tm % LANES == 0
        f32_out = lambda w: jax.ShapeDtypeStruct((n // LANES, w, LANES), F32)
        f32_spec = lambda w: pl.BlockSpec((tm // LANES, w, LANES), lambda i: (i, 0, 0))
    else:
        f32_out = lambda w: jax.ShapeDtypeStruct((n, w), F32)
        f32_spec = row
    out_shape = (
        jax.ShapeDtypeStruct((N_HEADS_A, n, HEAD_DIM_A), BF16),
        f32_out(WA_KV),
        jax.ShapeDtypeStruct((N_KV_A, n, HEAD_DIM_A), BF16),
        f32_out(WA_KV),
        jax.ShapeDtypeStruct((N_KV_A, VT_ROWS, n), BF16),
        jax.ShapeDtypeStruct((N_HEADS_IDX, n, D_IDX), BF16),
        f32_out(D_IDX),
        jax.ShapeDtypeStruct((n, D_IDX), BF16),
        jax.ShapeDtypeStruct((N_HEADS_IDX, n), F32),
    )
    out_specs = (hm(N_HEADS_A, HEAD_DIM_A), f32_spec(WA_KV), hm(N_KV_A, HEAD_DIM_A), f32_spec(WA_KV),
                 pl.BlockSpec((N_KV_A, VT_ROWS, tm), lambda i: (0, 0, i)), hm(N_HEADS_IDX, D_IDX), f32_spec(D_IDX),
                 row(D_IDX), pl.BlockSpec((N_HEADS_IDX, tm), lambda i: (0, i)))
    in_specs = [row(d), _const_spec(gin.shape), _const_spec(wa.shape), _const_spec(wwit.shape),
                _const_spec(wvt.shape), _const_spec(bd.shape), _const_spec(gqa.shape), _const_spec(gka.shape),
                row(LANES), row(LANES), row(LANES)]
    kern = functools.partial(_front_kernel, score_scale=D_IDX ** -0.5 * N_HEADS_IDX ** -0.5, paged=paged)
    return pl.pallas_call(kern, out_shape=out_shape, grid=(n // tm,), in_specs=in_specs, out_specs=out_specs,
                          compiler_params=_cparams(), name="front")(x2d, gin, wa, wwit, wvt, bd, gqa, gka, cos, sa, sb)


def _memkv_kernel(mem_ref, gmem_ref, w_ref, gkm_ref, k32_ref, v32_ref):
    h = _rms(mem_ref[...], gmem_ref[...]).astype(BF16)
    kv = _dot(h, w_ref[...])
    for hh in range(N_HEADS_M):
        sl = slice(hh * HEAD_DIM_M, (hh + 1) * HEAD_DIM_M)
        k32_ref[:, sl] = _rms(kv[:, sl], gkm_ref[...])
    v32_ref[...] = kv[:, WM_Q:]


def _memory_kv(mem2d, gmem, w, gkm):
    n = mem2d.shape[0]
    out = jax.ShapeDtypeStruct((n, WM_Q), F32)
    return pl.pallas_call(_memkv_kernel, out_shape=(out, out), name="memory_kv",
                          compiler_params=pltpu.CompilerParams(vmem_limit_bytes=VMEM_LIMIT))(mem2d, gmem, w, gkm)


def _sort_key(s):
    s = jnp.where(s == 0.0, 0.0, s)
    bits = pltpu.bitcast(s, I32)
    return bits ^ ((bits >> 31) & INT_MAX)


KEYS_PER_GROUP = 32 * SUBLANES
SEARCH_SLAB = 64


def _bit_planes(words):
    a = list(words)
    j, m = 16, 0x0000FFFF
    while j:
        k = 0
        while k < 32:
            t = (a[k] ^ lax.shift_right_logical(a[k + j], jnp.int32(j))) & m
            a[k] = a[k] ^ t
            a[k + j] = a[k + j] ^ (t << j)
            k = (k + j + 1) & ~j
        j >>= 1
        m = (m ^ (m << j)) & 0xFFFFFFFF if j else m
    return a


ATTN_SUB = 128


def _attn_kernel(qi_ref, qa_ref, wi_ref, ki_ref, k_ref, vt_ref, out_ref,
                 keys_ref, planes_ref, cand_ref, cut_ref, bias_ref, lg_ref, cm_ref, m_ref, acc_ref, *, k_sel, idx_bits):
    bq, sc = ATTN_BQ, ATTN_SC
    wpc = sc // 32
    i = pl.program_id(0)
    n_ch = (i * bq) // sc + 1
    t_pos = i * bq + lax.broadcasted_iota(I32, (1, bq), 1)
    sub = lax.broadcasted_iota(I32, (sc, 1), 0)
    qi = qi_ref[...].reshape(N_HEADS_IDX * bq, D_IDX)
    w = wi_ref[...]

    @pl.when(i == 0)
    def _():
        planes_ref[...] = jnp.zeros(planes_ref.shape, I32)

    def chunk(c):
        return pl.ds(pl.multiple_of(c * sc, sc), sc)

    def score_chunk(c, carry):
        off = pl.multiple_of(c * sc, sc)
        d = _dot_nt(ki_ref[chunk(c), :], qi)
        s = jnp.zeros((sc, bq), F32)
        for h in range(N_HEADS_IDX):
            s = s + jnp.maximum(d[:, h * bq:(h + 1) * bq], 0.0) * w[h:h + 1, :]
        key = jnp.where(off + sub <= t_pos, _sort_key(s), INT_MIN)
        keys_ref[chunk(c), :] = key
        u = key ^ INT_MIN
        for grp in range(sc // KEYS_PER_GROUP):
            base = grp * KEYS_PER_GROUP
            words = _bit_planes([u[base + SUBLANES * v:base + SUBLANES * (v + 1), :] for v in range(32)])
            rows = pl.ds(pl.multiple_of(c * wpc + grp * SUBLANES, SUBLANES), SUBLANES)
            for p in range(32):
                planes_ref[p, rows, :] = words[p]
        return carry

    lax.fori_loop(0, n_ch, score_chunk, 0)

    def count(pred):
        def body(c, acc):
            hit = jnp.where(pred(keys_ref[chunk(c), :], c * sc + sub), 1, 0)
            return acc + hit.reshape(sc // SUBLANES, SUBLANES, bq).sum(axis=0)

        def body2(j, acc):
            return body(2 * j + 1, body(2 * j, acc))
        acc = lax.fori_loop(0, n_ch // 2, body2, jnp.zeros((SUBLANES, bq), I32))
        acc = lax.fori_loop(n_ch - n_ch % 2, n_ch, body, acc)
        return acc.sum(axis=0, keepdims=True)

    slab = SEARCH_SLAB
    n_slab = (n_ch * wpc + slab - 1) // slab
    srow = lax.broadcasted_iota(I32, (slab, 1), 0)

    def slab_rows(sl):
        return pl.ds(pl.multiple_of(sl * slab, slab), slab)

    def init_cand(sl, carry):
        cand_ref[slab_rows(sl), :] = jnp.where(sl * slab + srow < n_ch * wpc, -1, 0) + jnp.zeros((slab, bq), I32)
        return carry

    lax.fori_loop(0, n_slab, init_cand, 0)

    def sweep(prev, cur):
        def body(sl, acc):
            r = slab_rows(sl)
            cand = cand_ref[r, :]
            if prev is not None:
                ones = cand & planes_ref[prev[0], r, :]
                cand = jnp.where(prev[1] != 0, ones, cand ^ ones)
                cand_ref[r, :] = cand
            hits = cand if cur is None else cand & planes_ref[cur, r, :]
            cnt = lax.population_count(hits)
            return acc + cnt.reshape(slab // SUBLANES, SUBLANES, bq).sum(axis=0)
        acc = lax.fori_loop(0, n_slab, body, jnp.zeros((SUBLANES, bq), I32))
        return acc.sum(axis=0, keepdims=True)

    def decide(ones, p, need, thr_u):
        keep = ones >= need
        bit = jnp.int32(1) << (31 - p)
        return jnp.where(keep, 1, 0), jnp.where(keep, need, need - ones), jnp.where(keep, thr_u | bit, thr_u)

    keep, need, thr_u = decide(sweep(None, 0), 0, jnp.full((1, bq), k_sel, I32), jnp.zeros((1, bq), I32))

    def search_pass(p, st):
        keep, need, thr_u = st
        return decide(sweep((p - 1, keep), p), p, need, thr_u)

    keep, need, thr_u = lax.fori_loop(1, 32, search_pass, (keep, need, thr_u))
    tied = sweep((31, keep), None)
    thr = thr_u ^ INT_MIN
    cnt = (k_sel - need) + tied
    cut_ref[...] = jnp.full((1, bq), INT_MAX, I32)

    excess = jnp.logical_and(cnt > k_sel, thr > INT_MIN)
    has_ties = jnp.max(jnp.where(excess, 1, 0)) > 0

    @pl.when(has_ties)
    def _():
        want = k_sel - count(lambda blk, _: blk > thr)

        def tie_body(j, cpos):
            cand = cpos + (jnp.int32(1) << (idx_bits - 1 - j))
            below = count(lambda blk, pos: jnp.where(blk == thr, pos, INT_MAX) < cand)
            return jnp.where(below < want, cand, cpos)

        cpos = lax.fori_loop(0, idx_bits, tie_body, jnp.zeros((1, bq), I32))
        cut_ref[...] = jnp.where(excess, cpos, INT_MAX)

    m_ref[...] = jnp.full(m_ref.shape, NEG, F32)
    acc_ref[...] = jnp.zeros(acc_ref.shape, F32)
    cut = cut_ref[...]
    floor = jnp.where(thr == INT_MIN, INT_MIN, thr - 1)
    n_sub = sc // ATTN_SUB

    qs = [qa_ref[g * GQA:(g + 1) * GQA].reshape(GQA * bq, HEAD_DIM_A) for g in range(N_KV_A)]
    neg_row = jnp.full((1, GQA * bq), NEG, F32)

    def stage(c, par, cb, par_b):
        do_a, do_b = c is not None, cb is not None
        if do_a:
            off = pl.multiple_of(c * sc, sc)
            keyc = keys_ref[chunk(c), :]

            @pl.when(has_ties)
            def _():
                tie = jnp.where(off + sub <= cut, 0.0, NEG)
                bias = jnp.where(keyc > thr, 0.0, jnp.where(keyc == thr, tie, NEG))
                bias_ref[...] = jnp.where(keyc == INT_MIN, NEG, bias)

            @pl.when(jnp.logical_not(has_ties))
            def _():
                bias_ref[...] = jnp.where(keyc > floor, 0.0, NEG)
            cmax = [neg_row] * N_KV_A
        if do_b:
            off_b = pl.multiple_of(cb * sc, sc)
            m_new, acc = [], []
            for g in range(N_KV_A):
                m_old = m_ref[g]
                m_new.append(jnp.maximum(m_old, cm_ref[par_b, g]))
                acc.append(jnp.exp2(m_old - m_new[g]) * acc_ref[g])
                m_ref[g] = m_new[g]
        for r in range(n_sub):
            rows = slice(r * ATTN_SUB, (r + 1) * ATTN_SUB)
            if do_a:
                bias4 = jnp.concatenate([bias_ref[rows, :]] * GQA, axis=1)
                for g in range(N_KV_A):
                    lg = _dot_nt(k_ref[g, pl.ds(off + r * ATTN_SUB, ATTN_SUB), :], qs[g]) + bias4
                    lg_ref[par, g, rows, :] = lg
                    cmax[g] = jnp.maximum(cmax[g], lg.max(axis=0, keepdims=True))
            if do_b:
                for g in range(N_KV_A):
                    p = jnp.exp2(lg_ref[par_b, g, rows, :] - m_new[g])
                    vt = vt_ref[g, :, pl.ds(off_b + r * ATTN_SUB, ATTN_SUB)]
                    acc[g] = acc[g] + _dot(vt, p.astype(BF16))
        if do_b:
            for g in range(N_KV_A):
                acc_ref[g] = acc[g]
        if do_a:
            for g in range(N_KV_A):
                cm_ref[par, g] = cmax[g]

    stage(jnp.int32(0), 0, None, None)

    def stage_pair(j, carry):
        stage(2 * j + 1, 1, 2 * j, 0)
        stage(2 * j + 2, 0, 2 * j + 1, 1)
        return carry

    last = n_ch - 1
    lax.fori_loop(0, last // 2, stage_pair, 0)

    @pl.when(last % 2 == 1)
    def _():
        stage(last, 1, last - 1, 0)
        stage(None, None, last, 1)

    @pl.when(last % 2 == 0)
    def _():
        stage(None, None, last, 0)

    heads = []
    for g in range(N_KV_A):
        acc = acc_ref[g]
        o = acc[:HEAD_DIM_A] / acc[HEAD_DIM_A:HEAD_DIM_A + 1]
        heads += [o[:, hh * bq:(hh + 1) * bq] for hh in range(GQA)]
    out_ref[...] = jnp.concatenate(heads, axis=0).T.astype(out_ref.dtype)


def _prompt_attention(qi_hm, qa_hm, wit, kibf, khm, vt):
    t = kibf.shape[0]
    bq = ATTN_BQ
    k_sel = min(TOPK_MAX, t // 4)
    kern = functools.partial(_attn_kernel, k_sel=k_sel, idx_bits=max(1, (t - 1).bit_length()))
    in_specs = [pl.BlockSpec((N_HEADS_IDX, bq, D_IDX), lambda i: (0, i, 0)),
                pl.BlockSpec((N_HEADS_A, bq, HEAD_DIM_A), lambda i: (0, i, 0)),
                pl.BlockSpec((N_HEADS_IDX, bq), lambda i: (0, i)),
                _const_spec(kibf.shape, True), _const_spec(khm.shape, True), _const_spec(vt.shape, True)]
    n_words = -(-(t // 32) // SEARCH_SLAB) * SEARCH_SLAB
    scratch = [pltpu.VMEM((t, bq), I32), pltpu.VMEM((32, n_words, bq), I32), pltpu.VMEM((n_words, bq), I32),
               pltpu.VMEM((1, bq), I32), pltpu.VMEM((ATTN_SC, bq), F32),
               pltpu.VMEM((2, N_KV_A, ATTN_SC, GQA * bq), F32), pltpu.VMEM((2, N_KV_A, 1, GQA * bq), F32),
               pltpu.VMEM((N_KV_A, 1, GQA * bq), F32),
               pltpu.VMEM((N_KV_A, VT_ROWS, GQA * bq), F32)]
    return pl.pallas_call(kern, out_shape=jax.ShapeDtypeStruct((t, WA_Q), BF16), grid=(t // bq,),
                          in_specs=in_specs, out_specs=pl.BlockSpec((bq, WA_Q), lambda i: (i, 0)),
                          scratch_shapes=scratch, compiler_params=_cparams(), name="attn")(
                              qi_hm, qa_hm, wit, kibf, khm, vt)


SAMPLE_CK = 2048


def _sample_attn_kernel(pt_ref, qi_ref, qa_ref, wi_ref, kin_ref, kn_ref, vn_ref, cidx_hbm, ck_hbm, cv_hbm, out_ref,
                        idx_buf, k_buf, v_buf, scr_ref, sems, *, n_pages, ps, ck, k_sel, idx_bits):
    b = pl.program_id(0)
    nb = pl.num_programs(0)
    past = n_pages * ps
    nc = past // ck

    def page_copy(which, bb, p):
        src, dst = ((cidx_hbm, idx_buf), (ck_hbm, k_buf), (cv_hbm, v_buf))[which]
        cols = pl.ds(pl.multiple_of(p * ps, ps), ps)
        dst = dst.at[:, cols] if which == 0 else dst.at[:, :, cols]
        return pltpu.make_async_copy(src.at[pt_ref[bb, p]], dst, sems.at[which])

    def for_pages(fn):
        lax.fori_loop(0, n_pages, lambda p, c: (fn(p), c)[1], 0, unroll=8)

    def wait_all(which):
        buf = (idx_buf, k_buf, v_buf)[which]
        pltpu.make_async_copy(buf, buf, sems.at[which]).wait()

    @pl.when(b == 0)
    def _():
        for_pages(lambda p: page_copy(0, b, p).start())

    for_pages(lambda p: (page_copy(1, b, p).start(), page_copy(2, b, p).start()))
    wait_all(0)

    qi = qi_ref[0]
    w = wi_ref[0]
    for c in range(nc):
        d = _dot(qi, idx_buf[:, c * ck:(c + 1) * ck].astype(BF16))
        scr_ref[c:c + 1, :] = (jnp.maximum(d, 0.0) * w).sum(axis=0, keepdims=True)
    d_new = (qi.astype(F32) * kin_ref[0].astype(BF16).astype(F32)).sum(axis=-1, keepdims=True)
    s_new = (jnp.maximum(d_new, 0.0) * w).sum(axis=0, keepdims=True)

    @pl.when(b + 1 < nb)
    def _():
        for_pages(lambda p: page_copy(0, b + 1, p).start())

    keys = _sort_key(scr_ref[...])
    key_new = _sort_key(s_new)
    pos = (lax.broadcasted_iota(I32, keys.shape, 0) * ck + lax.broadcasted_iota(I32, keys.shape, 1))

    def count(pred):
        hit = jnp.where(pred(keys, pos), 1, 0).sum(axis=1, keepdims=True).sum(axis=0, keepdims=True)
        return hit + jnp.where(pred(key_new, past), 1, 0)

    def search_cond(st):
        bit, _, cnt = st
        return jnp.logical_and(bit >= 0, jnp.max(jnp.where(cnt == k_sel, 0, 1)) > 0)

    def search_body(st):
        bit, thr, cnt = st
        cand = thr + (jnp.int32(1) << bit)
        c = count(lambda kk, _: kk >= cand)
        take = c >= k_sel
        return bit - 1, jnp.where(take, cand, thr), jnp.where(take, c, cnt)

    st0 = (jnp.int32(31), jnp.full((1, 1), INT_MIN, I32), jnp.full((1, 1), past + 1, I32))
    _, thr, cnt = lax.while_loop(search_cond, search_body, st0)

    want = k_sel - count(lambda kk, _: kk > thr)

    def tie_body(j, cpos):
        cand = cpos + (jnp.int32(1) << (idx_bits - 1 - j))
        below = count(lambda kk, pp: jnp.where(kk == thr, pp, INT_MAX) < cand)
        return jnp.where(below < want, cand, cpos)

    cpos = lax.fori_loop(0, idx_bits, tie_body, jnp.zeros((1, 1), I32))
    cut = jnp.where(cnt > k_sel, cpos, INT_MAX)

    def sel_bias(kk, pp):
        tie = jnp.where(pp <= cut, 0.0, NEG)
        return jnp.where(kk > thr, 0.0, jnp.where(kk == thr, tie, NEG))

    bias = sel_bias(keys, pos)
    bias_new = sel_bias(key_new, past)

    wait_all(1)
    wait_all(2)

    qa = qa_ref[0]
    heads = []
    for g in range(N_KV_A):
        qg = qa[g * GQA:(g + 1) * GQA]
        cols = slice(g * HEAD_DIM_A, (g + 1) * HEAD_DIM_A)
        m = jnp.full((GQA, 1), NEG, F32)
        l = jnp.zeros((GQA, 1), F32)
        acc = jnp.zeros((GQA, HEAD_DIM_A), F32)
        for c in range(nc):
            span = slice(c * ck, (c + 1) * ck)
            lg = _dot(qg, k_buf[g, :, span].astype(BF16)) + bias[c:c + 1, :]
            m_new = jnp.maximum(m, lg.max(axis=-1, keepdims=True))
            p = jnp.exp2(lg - m_new)
            alpha = jnp.exp2(m - m_new)
            l = alpha * l + p.sum(axis=-1, keepdims=True)
            acc = alpha * acc + _dot_nt(p.astype(BF16), v_buf[g, :, span].astype(BF16))
            m = m_new
        kn = kn_ref[0][:, cols].astype(BF16).astype(F32)
        vn = vn_ref[0][:, cols].astype(BF16).astype(F32)
        lg = (qg.astype(F32) * kn).sum(axis=-1, keepdims=True) + bias_new
        m_new = jnp.maximum(m, lg)
        p = jnp.exp2(lg - m_new)
        alpha = jnp.exp2(m - m_new)
        l = alpha * l + p
        acc = alpha * acc + p * vn
        o = acc / l
        heads += [o[hh:hh + 1, :] for hh in range(GQA)]
    out_ref[0] = jnp.concatenate(heads, axis=1)


def _sample_attention(page_table, qi_s, qa_s, wi_s, ki_new, k_new, v_new, cache_idx_k, cache_k, cache_v):
    nbatch, n_pages = page_table.shape
    n_pool, ps, d_idx = cache_idx_k.shape
    past = n_pages * ps
    ck = min(SAMPLE_CK, past)
    assert past % ck == 0 and ck % ps == 0
    k_sel = min(TOPK_MAX, (past + 1) // 4)
    kern = functools.partial(_sample_attn_kernel, n_pages=n_pages, ps=ps, ck=ck, k_sel=k_sel,
                             idx_bits=past.bit_length())
    per_b = lambda shape: pl.BlockSpec((1,) + shape, lambda b, pt: (b, 0, 0))
    any_spec = pl.BlockSpec(memory_space=pl.ANY)
    grid_spec = pltpu.PrefetchScalarGridSpec(
        num_scalar_prefetch=1, grid=(nbatch,),
        in_specs=[per_b((N_HEADS_IDX, d_idx)), per_b((N_HEADS_A, HEAD_DIM_A)), per_b((N_HEADS_IDX, 1)),
                  per_b((1, d_idx)), per_b((1, WA_KV)), per_b((1, WA_KV)), any_spec, any_spec, any_spec],
        out_specs=per_b((1, WA_Q)),
        scratch_shapes=[pltpu.VMEM((d_idx, past), F32), pltpu.VMEM((N_KV_A, HEAD_DIM_A, past), F32),
                        pltpu.VMEM((N_KV_A, HEAD_DIM_A, past), F32),
                        pltpu.VMEM((past // ck, ck), F32), pltpu.SemaphoreType.DMA((3,))])
    return pl.pallas_call(kern, out_shape=jax.ShapeDtypeStruct((nbatch, 1, WA_Q), F32), grid_spec=grid_spec,
                          compiler_params=_cparams(), name="sample_attn")(
                              page_table, qi_s, qa_s, wi_s, ki_new, k_new, v_new,
                              jnp.transpose(cache_idx_k, (0, 2, 1)), jnp.transpose(cache_k, (0, 2, 3, 1)),
                              jnp.transpose(cache_v, (0, 2, 3, 1)))


def _rope_tables(pos):
    rot = HEAD_DIM_A // ROT_FRACTION
    half = rot // 2
    inv_freq = ROPE_THETA ** (-jnp.arange(half, dtype=F32) / half)
    ang = pos.astype(F32)[:, None] * inv_freq[None, :]
    cos, sin = jnp.cos(ang), jnp.sin(ang)
    n = pos.shape[0]
    z = lambda w: jnp.zeros((n, w), F32)
    c = jnp.concatenate([cos, cos, jnp.ones((n, HEAD_DIM_A - rot), F32)], axis=1)
    sa = jnp.concatenate([-sin, z(HEAD_DIM_A - half)], axis=1)
    sb = jnp.concatenate([z(half), sin, z(HEAD_DIM_A - rot)], axis=1)
    rep = LANES // HEAD_DIM_A
    return tuple(jnp.tile(a, (1, rep)) for a in (c, sa, sb))


def _prep_params(w_in, g_in, g_qa, g_ka):
    d = w_in.shape[0]
    w_qa, w_ka, w_va, w_qi, w_ki, w_wi = _split_w_in(w_in, d)[:6]
    wa = jnp.concatenate([w_qa, w_ka, w_va, w_qi, w_ki, jnp.zeros((d, LANES - D_IDX), w_in.dtype)], axis=1)
    lane = jnp.arange(LANES)
    bd = (lane[:, None] // HEAD_DIM_A == lane[None, :] // HEAD_DIM_A).astype(BF16)
    rep = LANES // HEAD_DIM_A
    return (g_in.reshape(1, d), wa.astype(BF16), w_wi.T.astype(BF16), w_va.T.astype(BF16), bd,
            jnp.tile(g_qa, rep).reshape(1, LANES), jnp.tile(g_ka, rep).reshape(1, LANES))


def _split_w_in(w_in, d):
    sizes = (WA_Q, WA_KV, WA_KV, WI_Q, D_IDX, N_HEADS_IDX, WIDTH_B, WIDTH_B, WM_Q, N_BRANCH * d)
    parts, o = [], 0
    for s in sizes:
        parts.append(w_in[:, o:o + s])
        o += s
    return parts


def _mix_kernel(*refs, chunked, shared_mem, emit_vb, n_experts):
    it = iter(refs)
    x_ref, oa_ref, gin_ref, wb_ref, gvb_ref, gqm_ref = (next(it) for _ in range(6))
    if chunked:
        wtril_ref, bt_ref = next(it), next(it)
    else:
        wdiag_ref, bdiag_ref = next(it), next(it)
    mk_ref, mv_ref = next(it), next(it)
    wbra_ref, wbrb_ref, wbrm_ref, wout_ref, gffn_ref, wrh_ref, wrl_ref, br_ref = (next(it) for _ in range(8))
    x1_ref, h2_ref, te_ref, tw_ref = (next(it) for _ in range(4))
    vb_ref = next(it) if emit_vb else None

    x = x_ref[...]
    tm, d = x.shape
    h = _rms(x, gin_ref[...]).astype(BF16)
    p = _dot(h, wb_ref[...])
    ub = p[:, :WIDTH_B]
    vb = _rms(p[:, WIDTH_B:2 * WIDTH_B], gvb_ref[...])
    if emit_vb:
        vb_ref[...] = vb
    o = 2 * WIDTH_B
    qm = p[:, o:o + WM_Q]
    o += WM_Q
    gates = p[:, o:o + N_BRANCH * d]

    if chunked:
        vbb = vb.astype(BF16)
        rows = []
        for cc in range(tm // CHUNK):
            cols = []
            for g in range(N_GROUPS_B):
                vg = vbb[cc * CHUNK:(cc + 1) * CHUNK, g * GROUP_DIM_B:(g + 1) * GROUP_DIM_B]
                cols.append(_dot(wtril_ref[g], vg))
            rows.append(jnp.concatenate(cols, axis=1) + bt_ref[...])
        z = jnp.concatenate(rows, axis=0) if len(rows) > 1 else rows[0]
    else:
        z = vb * wdiag_ref[...] + bdiag_ref[...]
    out_b = ub * z

    scale_m = HEAD_DIM_M ** -0.5
    outs = []
    for hh in range(N_HEADS_M):
        sl = slice(hh * HEAD_DIM_M, (hh + 1) * HEAD_DIM_M)
        qh = _rms(qm[:, sl], gqm_ref[...])
        if shared_mem:
            lg = _dot_nt(qh.astype(BF16), mk_ref[:, sl].astype(BF16)) * scale_m
            pm = jnp.exp(lg - lg.max(axis=-1, keepdims=True))
            pm = pm / pm.sum(axis=-1, keepdims=True)
            outs.append(_dot(pm.astype(BF16), mv_ref[:, sl].astype(BF16)))
        else:
            per_row = []
            for r in range(tm):
                kr = mk_ref[r, :, sl]
                lg = (kr * qh[r:r + 1, :]).sum(axis=-1, keepdims=True) * scale_m
                pm = jnp.exp(lg - lg.max(axis=0, keepdims=True))
                pm = pm / pm.sum(axis=0, keepdims=True)
                per_row.append((pm * mv_ref[r, :, sl]).sum(axis=0, keepdims=True))
            outs.append(jnp.concatenate(per_row, axis=0))
    out_m = jnp.concatenate(outs, axis=1)

    sig = jax.nn.sigmoid
    merged = (sig(gates[:, :d]) * _dot(oa_ref[...].astype(BF16), wbra_ref[...])
              + sig(gates[:, d:2 * d]) * _dot(out_b.astype(BF16), wbrb_ref[...])
              + sig(gates[:, 2 * d:]) * _dot(out_m.astype(BF16), wbrm_ref[...]))
    x1 = x + _dot(merged.astype(BF16), wout_ref[...])
    x1_ref[...] = x1
    h2 = _rms(x1, gffn_ref[...])
    h2_ref[...] = h2.astype(h2_ref.dtype)

    hi = h2.astype(BF16)
    lo = (h2 - hi.astype(F32)).astype(BF16)
    lg = _dot(hi, wrh_ref[...]) + _dot(hi, wrl_ref[...]) + _dot(lo, wrh_ref[...]) + br_ref[...]
    lane = lax.broadcasted_iota(I32, lg.shape, 1)
    lg = jnp.where(lane < n_experts, lg, -jnp.inf)
    vals, idxs = [], []
    for _ in range(TOP_K_MOE):
        m = lg.max(axis=-1, keepdims=True)
        idx = jnp.where(lg == m, lane, LANES).min(axis=-1, keepdims=True)
        vals.append(m)
        idxs.append(idx)
        lg = jnp.where(lane == idx, -jnp.inf, lg)
    ex = [jnp.exp(v - vals[0]) for v in vals]
    den = ex[0]
    for e in ex[1:]:
        den = den + e
    te = jnp.full(lane.shape, -1, I32)
    tw = jnp.zeros(lane.shape, F32)
    for j in range(TOP_K_MOE):
        te = jnp.where(lane == j, idxs[j], te)
        tw = jnp.where(lane == j, ex[j] / den, tw)
    te_ref[...] = te
    tw_ref[...] = tw


def _mix(x2d, out_a, mix_params, gmlp_params, mem_k, mem_v, *, tm, chunked, shared_mem, emit_vb, n_experts):
    n, d = x2d.shape
    gin, wb, gvb, gqm, wbra, wbrb, wbrm, wout, gffn, wrh, wrl, br = mix_params
    row = lambda w: pl.BlockSpec((tm, w), lambda i: (i, 0))
    if shared_mem:
        mem_specs = [_const_spec(mem_k.shape), _const_spec(mem_v.shape)]
    else:
        mspec = pl.BlockSpec((tm,) + mem_k.shape[1:], lambda i: (i, 0, 0))
        mem_specs = [mspec, mspec]
    consts = lambda arrs: [_const_spec(a.shape) for a in arrs]
    in_specs = ([row(d), row(WA_Q)] + consts([gin, wb, gvb, gqm]) + consts(gmlp_params) + mem_specs
                + consts([wbra, wbrb, wbrm, wout, gffn, wrh, wrl, br]))
    out_shape = [jax.ShapeDtypeStruct((n, d), F32), jax.ShapeDtypeStruct((n, d), F32),
                 jax.ShapeDtypeStruct((n, LANES), I32), jax.ShapeDtypeStruct((n, LANES), F32)]
    out_specs = [row(d), row(d), row(LANES), row(LANES)]
    if emit_vb:
        out_shape.append(jax.ShapeDtypeStruct((n, WIDTH_B), F32))
        out_specs.append(row(WIDTH_B))
    kern = functools.partial(_mix_kernel, chunked=chunked, shared_mem=shared_mem, emit_vb=emit_vb,
                             n_experts=n_experts)
    return pl.pallas_call(kern, out_shape=tuple(out_shape), grid=(n // tm,), in_specs=in_specs,
                          out_specs=tuple(out_specs), compiler_params=_cparams(), name="mix")(
                              x2d, out_a, gin, wb, gvb, gqm, *gmlp_params, mem_k, mem_v,
                              wbra, wbrb, wbrm, wout, gffn, wrh, wrl, br)


def _prep_mix_params(w_in, g_in, g_vb, g_qm, w_br_a, w_br_b, w_br_m, w_out, g_ffn, w_router, b_router):
    d = w_in.shape[0]
    parts = _split_w_in(w_in, d)
    wb = jnp.concatenate(parts[6:10], axis=1).astype(BF16)
    n_e = w_router.shape[1]
    wr = jnp.pad(w_router, ((0, 0), (0, LANES - n_e)))
    wrh = wr.astype(BF16)
    wrl = (wr - wrh.astype(F32)).astype(BF16)
    br = jnp.pad(b_router, (0, LANES - n_e)).reshape(1, LANES)
    return (g_in.reshape(1, d), wb, g_vb.reshape(1, WIDTH_B), g_qm.reshape(1, HEAD_DIM_M),
            w_br_a.astype(BF16), w_br_b.astype(BF16), w_br_m.astype(BF16), w_out.astype(BF16),
            g_ffn.reshape(1, d), wrh, wrl, br)


def _prep_gmlp_chunked(w_s, b_s):
    tril = jnp.tril(jnp.ones((CHUNK, CHUNK), w_s.dtype))
    wtril = (w_s * tril).astype(BF16)
    bt = jnp.repeat(b_s.T, GROUP_DIM_B, axis=1)
    return wtril, bt


def _prep_gmlp_single(w_s, b_s):
    wdiag = jnp.repeat(w_s[:, 0, 0], GROUP_DIM_B).reshape(1, WIDTH_B)
    bdiag = jnp.repeat(b_s[:, 0], GROUP_DIM_B).reshape(1, WIDTH_B)
    return wdiag, bdiag


MOE_TP = 256


def _plan_kernel(te_ref, tri_ref, upper_ref, lslot_ref, cnt_ref, tstart_ref, gprev_ref, total_ref, carry_ref):
    @pl.when(pl.program_id(0) == 0)
    def _():
        carry_ref[...] = jnp.zeros(carry_ref.shape, F32)

    te = te_ref[...]
    lane = lax.broadcasted_iota(I32, te.shape, 1)
    picks = [te[:, j:j + 1] for j in range(TOP_K_MOE)]
    onehot = jnp.zeros(te.shape, F32)
    for e in picks:
        onehot = onehot + jnp.where(lane == e, 1.0, 0.0)
    before = _dot(tri_ref[...], onehot.astype(BF16))
    cnt = onehot.sum(axis=0, keepdims=True)
    cnt = jnp.floor((cnt + (SUBLANES - 1)) * (1.0 / SUBLANES)) * SUBLANES
    cnt8 = jnp.broadcast_to(cnt, (SUBLANES, LANES)).astype(BF16)
    tstart = _dot(cnt8, upper_ref[...])[0:1, :]
    lslot = jnp.zeros(te.shape, I32)
    for j, e in enumerate(picks):
        r = jnp.where(lane == e, before + tstart, 0.0).sum(axis=-1, keepdims=True).astype(I32)
        lslot = jnp.where(lane == j, r, lslot)
    lslot_ref[...] = lslot
    cnt_ref[0] = cnt.astype(I32)
    tstart_ref[0] = tstart.astype(I32)
    gprev_ref[0] = carry_ref[...].astype(I32)
    carry_ref[...] = carry_ref[...] + cnt
    total_ref[...] = carry_ref[...].astype(I32)


def _moe_plan(te_all):
    n = te_all.shape[0]
    tp = MOE_TP
    n_tiles = n // tp
    r = jnp.arange(tp)
    tri = (r[:, None] > r[None, :]).astype(BF16)
    e = jnp.arange(LANES)
    upper = (e[:, None] < e[None, :]).astype(BF16)
    per_tile = jax.ShapeDtypeStruct((n_tiles, 1, LANES), I32)
    tile_spec = pl.BlockSpec((1, 1, LANES), lambda i: (i, 0, 0))
    return pl.pallas_call(
        _plan_kernel,
        out_shape=(jax.ShapeDtypeStruct((n, LANES), I32), per_tile, per_tile, per_tile,
                   jax.ShapeDtypeStruct((1, LANES), I32)),
        grid=(n_tiles,),
        in_specs=[pl.BlockSpec((tp, LANES), lambda i: (i, 0)), _const_spec((tp, tp)), _const_spec((LANES, LANES))],
        out_specs=(pl.BlockSpec((tp, LANES), lambda i: (i, 0)), tile_spec, tile_spec, tile_spec,
                   _const_spec((1, LANES))),
        scratch_shapes=[pltpu.VMEM((1, LANES), F32)],
        compiler_params=_cparams(), name="moe_plan")(te_all, tri, upper)


def _run_pieces(tp):
    sizes, s = [], SUBLANES
    while s <= tp:
        sizes.append(s)
        s *= 2
    return tuple(reversed(sizes))


def _local_rows(tp, n_e):
    return tp * TOP_K_MOE + n_e * SUBLANES


def _for_runs(tabs, tile, n_e, tp, fn):
    cnt_s, tstart_s, gbase_s = tabs

    def per_expert(e, c):
        idx = tile * n_e + e
        length, src0, dst0 = cnt_s[idx], tstart_s[idx], gbase_s[idx]
        off = jnp.int32(0)
        for size in _run_pieces(tp):
            piece = length & size

            @pl.when(piece != 0)
            def _():
                fn(pl.multiple_of(src0 + off, SUBLANES), pl.multiple_of(dst0 + off, SUBLANES), size)
            off = off + piece
        return c

    lax.fori_loop(0, n_e, per_expert, 0)


def _slot_matrix(lslot, vals, n_rows):
    col = lax.broadcasted_iota(I32, (lslot.shape[0], n_rows), 1)
    m = jnp.zeros(col.shape, F32)
    for j in range(TOP_K_MOE):
        v = 1.0 if vals is None else vals[:, j:j + 1]
        m = m + jnp.where(lslot[:, j:j + 1] == col, v, 0.0)
    return m


def _dispatch_kernel(cnt_s, tstart_s, gbase_s, lslot_ref, h2_ref, xs_in, xs_out, stage, sem, *, tp, n_e, tile0):
    del xs_in
    n_rows = _local_rows(tp, n_e)
    i = pl.program_id(0)
    slot = i % 2
    pt = _slot_matrix(lslot_ref[...], None, n_rows).astype(BF16)
    stage[slot] = lax.dot_general(pt, h2_ref[...].astype(BF16), (((0,), (0,)), ((), ())),
                                  preferred_element_type=F32)

    def runs(tile, s, op):
        _for_runs((cnt_s, tstart_s, gbase_s), tile0 + tile, n_e, tp,
                  lambda src, dst, size: op(pltpu.make_async_copy(
                      stage.at[s, pl.ds(src, size), :], xs_out.at[pl.ds(dst, size), :], sem.at[s])))

    @pl.when(i > 0)
    def _():
        runs(i - 1, 1 - slot, lambda cp: cp.wait())

    runs(i, slot, lambda cp: cp.start())

    @pl.when(i == pl.num_programs(0) - 1)
    def _():
        runs(i, slot, lambda cp: cp.wait())


def _dispatch(tabs, tile0, lslot, h2, xs, n_e):
    n, d = h2.shape
    tp = min(MOE_TP, n)
    n_rows = _local_rows(tp, n_e)
    kern = functools.partial(_dispatch_kernel, tp=tp, n_e=n_e, tile0=tile0)
    any_spec = pl.BlockSpec(memory_space=pl.ANY)
    grid_spec = pltpu.PrefetchScalarGridSpec(
        num_scalar_prefetch=3, grid=(n // tp,),
        in_specs=[pl.BlockSpec((tp, LANES), lambda i, *_: (i, 0)), pl.BlockSpec((tp, d), lambda i, *_: (i, 0)), any_spec],
        out_specs=any_spec,
        scratch_shapes=[pltpu.VMEM((2, n_rows, d), F32), pltpu.SemaphoreType.DMA((2,))])
    return pl.pallas_call(
        kern, out_shape=jax.ShapeDtypeStruct(xs.shape, xs.dtype), grid_spec=grid_spec,
        input_output_aliases={5: 0},
        compiler_params=pltpu.CompilerParams(dimension_semantics=("arbitrary",), has_side_effects=True,
                                             vmem_limit_bytes=VMEM_LIMIT),
        name="dispatch")(*tabs, lslot, h2, xs)


def _expert_kernel(be_ref, nu_ref, xs_ref, wg_ref, wu_ref, wd_ref, bg_ref, bu_ref, bd_ref, y_ref,
                   wg_bf, wu_bf, wd_bf):
    b = pl.program_id(0)

    @pl.when(b >= nu_ref[0])
    def _():
        y_ref[...] = jnp.zeros(y_ref.shape, F32)

    @pl.when(b < nu_ref[0])
    def _():
        prev = be_ref[jnp.maximum(b - 1, 0)]

        @pl.when(jnp.logical_or(b == 0, be_ref[b] != prev))
        def _():
            wg_bf[...] = wg_ref[0].astype(BF16)
            wu_bf[...] = wu_ref[0].astype(BF16)
            wd_bf[...] = wd_ref[0].astype(BF16)

        xb = xs_ref[...].astype(BF16)
        ff = wg_bf.shape[1]
        y = jnp.zeros(y_ref.shape, F32) + bd_ref[0]
        for n in range(ff // FF_CHUNK):
            sl = slice(n * FF_CHUNK, (n + 1) * FF_CHUNK)
            hg = jnp.minimum(_dot(xb, wg_bf[:, sl]) + bg_ref[0][:, sl], SWIGLU_LIMIT)
            hu = jnp.clip(_dot(xb, wu_bf[:, sl]) + bu_ref[0][:, sl], -SWIGLU_LIMIT, SWIGLU_LIMIT)
            act = hg * jax.nn.sigmoid(SWIGLU_ALPHA * hg) * (hu + 1.0)
            y = y + _dot(act.astype(BF16), wd_bf[sl, :])
        y_ref[...] = y


def _experts(xs, block_e, n_used, w_gate, b_gate, w_up, b_up, w_down, b_down):
    ns, d = xs.shape
    n_e, _, ff = w_gate.shape
    bm = MOE_BM
    blk = lambda b, be, nu: (jnp.minimum(b, nu[0] - 1), 0)
    wsel = lambda b, be, nu: (be[b], 0, 0)
    grid_spec = pltpu.PrefetchScalarGridSpec(
        num_scalar_prefetch=2, grid=(ns // bm,),
        in_specs=[pl.BlockSpec((bm, d), blk),
                  pl.BlockSpec((1, d, ff), wsel), pl.BlockSpec((1, d, ff), wsel), pl.BlockSpec((1, ff, d), wsel),
                  pl.BlockSpec((1, 1, ff), wsel), pl.BlockSpec((1, 1, ff), wsel), pl.BlockSpec((1, 1, d), wsel)],
        out_specs=pl.BlockSpec((bm, d), lambda b, be, nu: (b, 0)),
        scratch_shapes=[pltpu.VMEM((d, ff), BF16), pltpu.VMEM((d, ff), BF16), pltpu.VMEM((ff, d), BF16)])
    return pl.pallas_call(_expert_kernel, out_shape=jax.ShapeDtypeStruct((ns, d), F32), grid_spec=grid_spec,
                          compiler_params=_cparams(), name="experts")(
                              block_e, n_used, xs, w_gate, w_up, w_down,
                              b_gate.reshape(n_e, 1, ff), b_up.reshape(n_e, 1, ff), b_down.reshape(n_e, 1, d))


def _combine_kernel(cnt_s, tstart_s, gbase_s, lslot_ref, tw_ref, x1_ref, yb_hbm, y_ref, buf, sem, *, tp, n_e, tile0):
    n_rows = _local_rows(tp, n_e)
    i = pl.program_id(0)
    slot = i % 2

    def runs(tile, s, op):
        _for_runs((cnt_s, tstart_s, gbase_s), tile0 + tile, n_e, tp,
                  lambda loc, glob, size: op(pltpu.make_async_copy(
                      yb_hbm.at[pl.ds(glob, size), :], buf.at[s, pl.ds(loc, size), :], sem.at[s])))

    @pl.when(i == 0)
    def _():
        buf[...] = jnp.zeros(buf.shape, F32)
        runs(i, slot, lambda cp: cp.start())

    @pl.when(i + 1 < pl.num_programs(0))
    def _():
        runs(i + 1, 1 - slot, lambda cp: cp.start())

    runs(i, slot, lambda cp: cp.wait())

    ptw = _slot_matrix(lslot_ref[...], tw_ref[...], n_rows)
    rows = buf[slot]
    w_hi = ptw.astype(BF16)
    w_lo = (ptw - w_hi.astype(F32)).astype(BF16)
    r_hi = rows.astype(BF16)
    r_lo = (rows - r_hi.astype(F32)).astype(BF16)
    y_ref[...] = x1_ref[...] + _dot(w_hi, r_hi) + (_dot(w_hi, r_lo) + _dot(w_lo, r_hi))


def _combine(tabs, tile0, lslot, tw, x1, yb, n_e):
    n, d = x1.shape
    tp = min(MOE_TP, n)
    n_rows = _local_rows(tp, n_e)
    kern = functools.partial(_combine_kernel, tp=tp, n_e=n_e, tile0=tile0)
    row = lambda w: pl.BlockSpec((tp, w), lambda i, *_: (i, 0))
    grid_spec = pltpu.PrefetchScalarGridSpec(
        num_scalar_prefetch=3, grid=(n // tp,),
        in_specs=[row(LANES), row(LANES), row(d), pl.BlockSpec(memory_space=pl.ANY)],
        out_specs=row(d),
        scratch_shapes=[pltpu.VMEM((2, n_rows, d), F32), pltpu.SemaphoreType.DMA((2,))])
    return pl.pallas_call(kern, out_shape=jax.ShapeDtypeStruct(x1.shape, F32), grid_spec=grid_spec,
                          compiler_params=_cparams(), name="combine")(*tabs, lslot, tw, x1, yb)


def _moe(groups, w_gate, b_gate, w_up, b_up, w_down, b_down):
    n_e = w_gate.shape[0]
    d = groups[0][0].shape[1]
    tp, bm = MOE_TP, MOE_BM
    sizes = [g[0].shape[0] for g in groups]
    assert all(n % tp == 0 or n < tp for n in sizes), sizes
    parts, tile0s, o = [], [], 0
    for g, n in zip(groups, sizes):
        n_pad = -(-n // tp) * tp
        parts.append(jnp.pad(g[2], ((0, n_pad - n), (0, 0)), constant_values=-1))
        tile0s.append(o // tp)
        o += n_pad
    lslot, cnt3, tstart3, gprev3, total = _moe_plan(jnp.concatenate(parts, axis=0))
    counts = total[0, :n_e]
    padded_cnt = (counts + bm - 1) // bm * bm
    pad_end = jnp.cumsum(padded_cnt)
    pad_start = pad_end - padded_cnt
    n_assign = sum(sizes) * TOP_K_MOE
    n_slack = (o // tp) * n_e * (SUBLANES - 1)
    nb = -(-(n_assign + n_slack) // bm) + n_e
    n_used = (pad_end[-1] // bm).astype(I32)
    blocks = jnp.arange(nb, dtype=I32)
    block_e = jnp.minimum((pad_end[None, :] <= (blocks * bm)[:, None]).sum(axis=1), n_e - 1).astype(I32)
    block_e = jnp.where(blocks < n_used, block_e, block_e[jnp.maximum(n_used - 1, 0)])
    flat = lambda a: a[:, 0, :n_e].reshape(-1).astype(I32)
    gbase3 = pad_start[None, None, :] + gprev3[:, :, :n_e]
    tabs = (flat(cnt3), flat(tstart3), flat(gbase3))

    xs = jnp.zeros((nb * bm, d), F32)
    lslots, o = [], 0
    for (x1, h2, te, tw), n, t0 in zip(groups, sizes, tile0s):
        ls = lslot[t0 * tp:t0 * tp + n]
        lslots.append(ls)
        xs = _dispatch(tabs, t0, ls, h2, xs, n_e)
    yb = _experts(xs, block_e, n_used.reshape(1), w_gate, b_gate, w_up, b_up, w_down, b_down)
    return [_combine(tabs, t0, ls, tw, x1, yb, n_e)
            for (x1, h2, te, tw), ls, t0 in zip(groups, lslots, tile0s)]


def _row_tile(n, want):
    return want if n % want == 0 else n


def kernel(x_prompt, x_sample, mem_prompt, cache_k, cache_v, cache_idx_k, cache_mem_k, cache_mem_v, page_table,
           g_in, w_in, g_qa, g_ka, g_vb, w_s, b_s, g_qm, g_mem, w_mem_kv, g_km, w_br_a, w_br_b, w_br_m, w_out,
           g_ffn, w_router, b_router, w_gate, b_gate, w_up, b_up, w_down, b_down):
    bp, t, d = x_prompt.shape
    bs, ts, _ = x_sample.shape
    assert bp == 1 and ts == 1 and t % CHUNK == 0 and cache_k.shape[1] == LANES
    n_mem = mem_prompt.shape[1]
    n_pages = page_table.shape[1]
    ps = cache_k.shape[1]
    past = n_pages * ps
    n_e = w_router.shape[1]

    fp = _prep_params(w_in, g_in, g_qa, g_ka)
    mp = _prep_mix_params(w_in, g_in, g_vb, g_qm, w_br_a, w_br_b, w_br_m, w_out, g_ffn, w_router, b_router)

    xp = x_prompt.reshape(t, d)
    qa_p, k32_p, khm_p, v32_p, vt_p, qi_p, ki32_p, kibf_p, wit_p = _front(
        xp, _rope_tables(jnp.arange(t, dtype=I32)), *fp, tm=_row_tile(t, 512), paged=True)
    mem_k, mem_v = _memory_kv(mem_prompt.reshape(n_mem, d), g_mem.reshape(1, d), w_mem_kv.astype(BF16),
                              g_km.reshape(1, HEAD_DIM_M))
    out_a_p = _prompt_attention(qi_p, qa_p, wit_p, kibf_p, khm_p, vt_p)
    x1_p, h2_p, te_p, tw_p = _mix(xp, out_a_p, mp, _prep_gmlp_chunked(w_s, b_s), mem_k, mem_v,
                                  tm=_row_tile(t, 256), chunked=True, shared_mem=True, emit_vb=False, n_experts=n_e)

    xs = x_sample.reshape(bs, d)
    qa_s, k32_s, _, v32_s, _, qi_s, ki32_s, _, wit_s = _front(
        xs, _rope_tables(jnp.full((bs,), past, I32)), *fp, tm=bs, paged=False)
    out_a_s = _sample_attention(page_table, jnp.moveaxis(qi_s, 0, 1), jnp.moveaxis(qa_s, 0, 1),
                                wit_s.T.reshape(bs, N_HEADS_IDX, 1), ki32_s.reshape(bs, 1, D_IDX),
                                k32_s.reshape(bs, 1, WA_KV), v32_s.reshape(bs, 1, WA_KV),
                                cache_idx_k, cache_k, cache_v)
    x1_s, h2_s, te_s, tw_s, vb_s = _mix(xs, out_a_s.reshape(bs, WA_Q), mp, _prep_gmlp_single(w_s, b_s),
                                        cache_mem_k.reshape(bs, n_mem, WM_Q), cache_mem_v.reshape(bs, n_mem, WM_Q),
                                        tm=_row_tile(bs, SUBLANES), chunked=False, shared_mem=False, emit_vb=True,
                                        n_experts=n_e)

    y_p, y_s = _moe([(x1_p, h2_p, te_p, tw_p), (x1_s, h2_s, te_s, tw_s)],
                    w_gate, b_gate, w_up, b_up, w_down, b_down)

    n_pg = t // ps
    paged_kv = lambda a: jnp.transpose(a.reshape(1, n_pg, N_KV_A, HEAD_DIM_A, ps), (0, 1, 4, 2, 3))
    return (y_p.reshape(1, t, d), y_s.reshape(bs, 1, d),
            paged_kv(k32_p), paged_kv(v32_p), jnp.transpose(ki32_p, (0, 2, 1)).reshape(1, n_pg, ps, D_IDX),
            mem_k.reshape(1, n_mem, N_HEADS_M, HEAD_DIM_M), mem_v.reshape(1, n_mem, N_HEADS_M, HEAD_DIM_M),
            k32_s.reshape(bs, 1, N_KV_A, HEAD_DIM_A), v32_s.reshape(bs, 1, N_KV_A, HEAD_DIM_A),
            ki32_s.reshape(bs, 1, D_IDX), vb_s.reshape(bs, 1, WIDTH_B))
```

```python
import functools

import jax
import jax.numpy as jnp
from jax import lax
from jax.experimental import pallas as pl
from jax.experimental.pallas import tpu as pltpu

F32 = jnp.float32
BF16 = jnp.bfloat16
I32 = jnp.int32

N_HEADS_A = 8
N_KV_A = 2
HEAD_DIM_A = 64
TOPK_MAX = 256
N_HEADS_IDX = 8
D_IDX = 64
N_GROUPS_B = 4
GROUP_DIM_B = 128
WIDTH_B = N_GROUPS_B * GROUP_DIM_B
CHUNK = 128
N_HEADS_M = 4
HEAD_DIM_M = 128
ROPE_THETA = 500000.0
ROT_FRACTION = 4
N_BRANCH = 3
TOP_K_MOE = 4
SWIGLU_LIMIT = 7.0
SWIGLU_ALPHA = 1.702
EPS = 1e-6

WA_Q = N_HEADS_A * HEAD_DIM_A
WA_KV = N_KV_A * HEAD_DIM_A
WI_Q = N_HEADS_IDX * D_IDX
WM_Q = N_HEADS_M * HEAD_DIM_M
GQA = N_HEADS_A // N_KV_A

LANES = 128
SUBLANES = 8
VMEM_LIMIT = 56 * 1024 * 1024

LOG2_E = 1.4426950408889634
INT_MIN = -(2 ** 31)
INT_MAX = 2 ** 31 - 1
NEG = -1e30

VT_ROWS = HEAD_DIM_A + 16
ATTN_BQ = 128
ATTN_SC = 512
MOE_BM = 512
FF_CHUNK = 512


def _cparams(n_axes=1, vmem=VMEM_LIMIT):
    return pltpu.CompilerParams(dimension_semantics=("arbitrary",) * n_axes, vmem_limit_bytes=vmem)


def _const_spec(shape, single=False):
    zeros = (0,) * len(shape)
    if single:
        return pl.BlockSpec(shape, lambda *_: zeros, pipeline_mode=pl.Buffered(1))
    return pl.BlockSpec(shape, lambda *_: zeros)


def _dot(a, b):
    return jnp.dot(a, b, preferred_element_type=F32)


def _dot_nt(a, b):
    return lax.dot_general(a, b, (((1,), (1,)), ((), ())), preferred_element_type=F32)


def _rms(x, g):
    return x * lax.rsqrt(jnp.mean(x * x, axis=-1, keepdims=True) + EPS) * g


def _split_dot(x, w):
    hi = x.astype(BF16)
    lo = (x - hi.astype(F32)).astype(BF16)
    return _dot(hi, w) + _dot(lo, w)


def _front_kernel(x_ref, gin_ref, wa_ref, wwit_ref, wvt_ref, bd_ref, gqa_ref, gka_ref, cos_ref, sa_ref, sb_ref,
                  qa_ref, k32_ref, khm_ref, v32_ref, vt_ref, qi_ref, ki32_ref, kibf_ref, wit_ref, *, score_scale, paged):
    x = x_ref[...]
    h = _rms(x, gin_ref[...]).astype(BF16)
    p = _dot(h, wa_ref[...])
    cos, sa, sb = cos_ref[...], sa_ref[...], sb_ref[...]
    bd = bd_ref[...]

    def head_norm(v, g):
        ssq = _split_dot(v * v, bd)
        return v * lax.rsqrt(ssq * (1.0 / HEAD_DIM_A) + EPS) * g

    def rope(v):
        return v * cos + pltpu.roll(v, LANES - 8, 1) * sa + pltpu.roll(v, 8, 1) * sb

    for s in range(WA_Q // LANES):
        v = p[:, s * LANES:(s + 1) * LANES]
        v = rope(head_norm(v, gqa_ref[...])) * (HEAD_DIM_A ** -0.5 * LOG2_E)
        vb = v.astype(BF16)
        qa_ref[2 * s] = vb[:, :HEAD_DIM_A]
        qa_ref[2 * s + 1] = vb[:, HEAD_DIM_A:]
    o = WA_Q
    n_pg = x.shape[0] // LANES
    k = rope(head_norm(p[:, o:o + WA_KV], gka_ref[...]))
    if paged:
        for j in range(n_pg):
            k32_ref[j] = k[j * LANES:(j + 1) * LANES, :].T
    else:
        k32_ref[...] = k
    kb = k.astype(BF16)
    khm_ref[0] = kb[:, :HEAD_DIM_A]
    khm_ref[1] = kb[:, HEAD_DIM_A:]
    o += WA_KV
    if not paged:
        v32_ref[...] = p[:, o:o + WA_KV]
    o += WA_KV
    for s in range(WI_Q // LANES):
        vb = rope(p[:, o + s * LANES:o + (s + 1) * LANES]).astype(BF16)
        qi_ref[2 * s] = vb[:, :D_IDX]
        qi_ref[2 * s + 1] = vb[:, D_IDX:]
    o += WI_Q
    ki_slab = rope(p[:, o:o + LANES])
    ki = ki_slab[:, :D_IDX]
    if paged:
        for j in range(n_pg):
            ki32_ref[j] = ki_slab[j * LANES:(j + 1) * LANES, :].T[:D_IDX, :]
    else:
        ki32_ref[...] = ki
    kibf_ref[...] = ki.astype(BF16)
    wit_ref[...] = _dot_nt(wwit_ref[...], h) * score_scale
    vt = _dot_nt(wvt_ref[...], h)
    if paged:
        for j in range(n_pg):
            v32_ref[j] = vt[:, j * LANES:(j + 1) * LANES]
    tail = jnp.where(lax.broadcasted_iota(I32, (VT_ROWS - HEAD_DIM_A, vt.shape[1]), 0) == 0, 1.0, 0.0)
    for g in range(N_KV_A):
        vt_ref[g] = jnp.concatenate([vt[g * HEAD_DIM_A:(g + 1) * HEAD_DIM_A], tail], axis=0).astype(BF16)


def _front(x2d, tabs, gin, wa, wwit, wvt, bd, gqa, gka, tm, paged):
    n, d = x2d.shape
    cos, sa, sb = tabs
    row = lambda w: pl.BlockSpec((tm, w), lambda i: (i, 0))
    hm = lambda nh, w: pl.BlockSpec((nh, tm, w), lambda i: (0, i, 0))
    if paged:
        assert tm % LANES == 0
        f32_out = lambda w: jax.ShapeDtypeStruct((n // LANES, w, LANES), F32)
        f32_spec = lambda w: pl.BlockSpec((tm // LANES, w, LANES), lambda i: (i, 0, 0))
    else:
        f32_out = lambda w: jax.ShapeDtypeStruct((n, w), F32)
        f32_spec = row
    out_shape = (
        jax.ShapeDtypeStruct((N_HEADS_A, n, HEAD_DIM_A), BF16),
        f32_out(WA_KV),
        jax.ShapeDtypeStruct((N_KV_A, n, HEAD_DIM_A), BF16),
        f32_out(WA_KV),
        jax.ShapeDtypeStruct((N_KV_A, VT_ROWS, n), BF16),
        jax.ShapeDtypeStruct((N_HEADS_IDX, n, D_IDX), BF16),
        f32_out(D_IDX),
        jax.ShapeDtypeStruct((n, D_IDX), BF16),
        jax.ShapeDtypeStruct((N_HEADS_IDX, n), F32),
    )
    out_specs = (hm(N_HEADS_A, HEAD_DIM_A), f32_spec(WA_KV), hm(N_KV_A, HEAD_DIM_A), f32_spec(WA_KV),
                 pl.BlockSpec((N_KV_A, VT_ROWS, tm), lambda i: (0, 0, i)), hm(N_HEADS_IDX, D_IDX), f32_spec(D_IDX),
                 row(D_IDX), pl.BlockSpec((N_HEADS_IDX, tm), lambda i: (0, i)))
    in_specs = [row(d), _const_spec(gin.shape), _const_spec(wa.shape), _const_spec(wwit.shape),
                _const_spec(wvt.shape), _const_spec(bd.shape), _const_spec(gqa.shape), _const_spec(gka.shape),
                row(LANES), row(LANES), row(LANES)]
    kern = functools.partial(_front_kernel, score_scale=D_IDX ** -0.5 * N_HEADS_IDX ** -0.5, paged=paged)
    return pl.pallas_call(kern, out_shape=out_shape, grid=(n // tm,), in_specs=in_specs, out_specs=out_specs,
                          compiler_params=_cparams(), name="front")(x2d, gin, wa, wwit, wvt, bd, gqa, gka, cos, sa, sb)


def _memkv_kernel(mem_ref, gmem_ref, w_ref, gkm_ref, k32_ref, v32_ref):
    h = _rms(mem_ref[...], gmem_ref[...]).astype(BF16)
    kv = _dot(h, w_ref[...])
    for hh in range(N_HEADS_M):
        sl = slice(hh * HEAD_DIM_M, (hh + 1) * HEAD_DIM_M)
        k32_ref[:, sl] = _rms(kv[:, sl], gkm_ref[...])
    v32_ref[...] = kv[:, WM_Q:]


def _memory_kv(mem2d, gmem, w, gkm):
    n = mem2d.shape[0]
    out = jax.ShapeDtypeStruct((n, WM_Q), F32)
    return pl.pallas_call(_memkv_kernel, out_shape=(out, out), name="memory_kv",
                          compiler_params=pltpu.CompilerParams(vmem_limit_bytes=VMEM_LIMIT))(mem2d, gmem, w, gkm)


def _sort_key(s):
    s = jnp.where(s == 0.0, 0.0, s)
    bits = pltpu.bitcast(s, I32)
    return bits ^ ((bits >> 31) & INT_MAX)


KEYS_PER_GROUP = 32 * SUBLANES
SEARCH_SLAB = 64


def _bit_planes(words):
    a = list(words)
    j, m = 16, 0x0000FFFF
    while j:
        k = 0
        while k < 32:
            t = (a[k] ^ lax.shift_right_logical(a[k + j], jnp.int32(j))) & m
            a[k] = a[k] ^ t
            a[k + j] = a[k + j] ^ (t << j)
            k = (k + j + 1) & ~j
        j >>= 1
        m = (m ^ (m << j)) & 0xFFFFFFFF if j else m
    return a


ATTN_SUB = 128


def _attn_kernel(qi_ref, qa_ref, wi_ref, ki_ref, k_ref, vt_ref, out_ref,
                 keys_ref, planes_ref, cand_ref, cut_ref, bias_ref, lg_ref, cm_ref, m_ref, acc_ref, *, k_sel, idx_bits):
    bq, sc = ATTN_BQ, ATTN_SC
    wpc = sc // 32
    i = pl.program_id(0)
    n_ch = (i * bq) // sc + 1
    t_pos = i * bq + lax.broadcasted_iota(I32, (1, bq), 1)
    sub = lax.broadcasted_iota(I32, (sc, 1), 0)
    qi = qi_ref[...].reshape(N_HEADS_IDX * bq, D_IDX)
    w = wi_ref[...]

    @pl.when(i == 0)
    def _():
        planes_ref[...] = jnp.zeros(planes_ref.shape, I32)

    def chunk(c):
        return pl.ds(pl.multiple_of(c * sc, sc), sc)

    def score_chunk(c, carry):
        off = pl.multiple_of(c * sc, sc)
        d = _dot_nt(ki_ref[chunk(c), :], qi)
        s = jnp.zeros((sc, bq), F32)
        for h in range(N_HEADS_IDX):
            s = s + jnp.maximum(d[:, h * bq:(h + 1) * bq], 0.0) * w[h:h + 1, :]
        key = jnp.where(off + sub <= t_pos, _sort_key(s), INT_MIN)
        keys_ref[chunk(c), :] = key
        u = key ^ INT_MIN
        for grp in range(sc // KEYS_PER_GROUP):
            base = grp * KEYS_PER_GROUP
            words = _bit_planes([u[base + SUBLANES * v:base + SUBLANES * (v + 1), :] for v in range(32)])
            rows = pl.ds(pl.multiple_of(c * wpc + grp * SUBLANES, SUBLANES), SUBLANES)
            for p in range(32):
                planes_ref[p, rows, :] = words[p]
        return carry

    lax.fori_loop(0, n_ch // 2, lambda j, cr: score_chunk(2 * j + 1, score_chunk(2 * j, cr)), 0)
    lax.fori_loop(n_ch - n_ch % 2, n_ch, score_chunk, 0)

    def count(pred):
        def body(c, acc):
            hit = jnp.where(pred(keys_ref[chunk(c), :], c * sc + sub), 1, 0)
            return acc + hit.reshape(sc // SUBLANES, SUBLANES, bq).sum(axis=0)

        def body2(j, acc):
            return body(2 * j + 1, body(2 * j, acc))
        acc = lax.fori_loop(0, n_ch // 2, body2, jnp.zeros((SUBLANES, bq), I32))
        acc = lax.fori_loop(n_ch - n_ch % 2, n_ch, body, acc)
        return acc.sum(axis=0, keepdims=True)

    slab = SEARCH_SLAB
    n_slab = (n_ch * wpc + slab - 1) // slab
    srow = lax.broadcasted_iota(I32, (slab, 1), 0)

    def slab_rows(sl):
        return pl.ds(pl.multiple_of(sl * slab, slab), slab)

    def init_cand(sl, carry):
        cand_ref[slab_rows(sl), :] = jnp.where(sl * slab + srow < n_ch * wpc, -1, 0) + jnp.zeros((slab, bq), I32)
        return carry

    lax.fori_loop(0, n_slab, init_cand, 0)

    def sweep(prev, cur):
        def body(sl, acc):
            r = slab_rows(sl)
            cand = cand_ref[r, :]
            if prev is not None:
                ones = cand & planes_ref[prev[0], r, :]
                cand = jnp.where(prev[1] != 0, ones, cand ^ ones)
                cand_ref[r, :] = cand
            hits = cand if cur is None else cand & planes_ref[cur, r, :]
            cnt = lax.population_count(hits)
            return acc + cnt.reshape(slab // SUBLANES, SUBLANES, bq).sum(axis=0)
        acc = lax.fori_loop(0, n_slab, body, jnp.zeros((SUBLANES, bq), I32))
        return acc.sum(axis=0, keepdims=True)

    def decide(ones, p, need, thr_u):
        keep = ones >= need
        bit = jnp.int32(1) << (31 - p)
        return jnp.where(keep, 1, 0), jnp.where(keep, need, need - ones), jnp.where(keep, thr_u | bit, thr_u)

    keep, need, thr_u = decide(sweep(None, 0), 0, jnp.full((1, bq), k_sel, I32), jnp.zeros((1, bq), I32))

    def search_pass(p, st):
        keep, need, thr_u = st
        return decide(sweep((p - 1, keep), p), p, need, thr_u)

    keep, need, thr_u = lax.fori_loop(1, 32, search_pass, (keep, need, thr_u))
    tied = sweep((31, keep), None)
    thr = thr_u ^ INT_MIN
    cnt = (k_sel - need) + tied
    cut_ref[...] = jnp.full((1, bq), INT_MAX, I32)

    excess = jnp.logical_and(cnt > k_sel, thr > INT_MIN)
    has_ties = jnp.max(jnp.where(excess, 1, 0)) > 0

    @pl.when(has_ties)
    def _():
        want = k_sel - count(lambda blk, _: blk > thr)

        def tie_body(j, cpos):
            cand = cpos + (jnp.int32(1) << (idx_bits - 1 - j))
            below = count(lambda blk, pos: jnp.where(blk == thr, pos, INT_MAX) < cand)
            return jnp.where(below < want, cand, cpos)

        cpos = lax.fori_loop(0, idx_bits, tie_body, jnp.zeros((1, bq), I32))
        cut_ref[...] = jnp.where(excess, cpos, INT_MAX)

    m_ref[...] = jnp.full(m_ref.shape, NEG, F32)
    acc_ref[...] = jnp.zeros(acc_ref.shape, F32)
    cut = cut_ref[...]
    floor = jnp.where(thr == INT_MIN, INT_MIN, thr - 1)
    n_sub = sc // ATTN_SUB

    qs = [qa_ref[g * GQA:(g + 1) * GQA].reshape(GQA * bq, HEAD_DIM_A) for g in range(N_KV_A)]
    neg_row = jnp.full((1, GQA * bq), NEG, F32)

    def stage(c, par, cb, par_b):
        do_a, do_b = c is not None, cb is not None
        if do_a:
            off = pl.multiple_of(c * sc, sc)
            keyc = keys_ref[chunk(c), :]

            @pl.when(has_ties)
            def _():
                tie = jnp.where(off + sub <= cut, 0.0, NEG)
                bias = jnp.where(keyc > thr, 0.0, jnp.where(keyc == thr, tie, NEG))
                bias_ref[...] = jnp.where(keyc == INT_MIN, NEG, bias)

            @pl.when(jnp.logical_not(has_ties))
            def _():
                bias_ref[...] = jnp.where(keyc > floor, 0.0, NEG)
            cmax = [neg_row] * N_KV_A
        if do_b:
            off_b = pl.multiple_of(cb * sc, sc)
            m_new, acc = [], []
            for g in range(N_KV_A):
                m_old = m_ref[g]
                m_new.append(jnp.maximum(m_old, cm_ref[par_b, g]))
                acc.append(jnp.exp2(m_old - m_new[g]) * acc_ref[g])
                m_ref[g] = m_new[g]
        for r in range(n_sub):
            rows = slice(r * ATTN_SUB, (r + 1) * ATTN_SUB)
            if do_a:
                bias4 = jnp.concatenate([bias_ref[rows, :]] * GQA, axis=1)
                for g in range(N_KV_A):
                    lg = _dot_nt(k_ref[g, pl.ds(off + r * ATTN_SUB, ATTN_SUB), :], qs[g]) + bias4
                    lg_ref[par, g, rows, :] = lg
                    cmax[g] = jnp.maximum(cmax[g], lg.max(axis=0, keepdims=True))
            if do_b:
                for g in range(N_KV_A):
                    p = jnp.exp2(lg_ref[par_b, g, rows, :] - m_new[g])
                    vt = vt_ref[g, :, pl.ds(off_b + r * ATTN_SUB, ATTN_SUB)]
                    acc[g] = acc[g] + _dot(vt, p.astype(BF16))
        if do_b:
            for g in range(N_KV_A):
                acc_ref[g] = acc[g]
        if do_a:
            for g in range(N_KV_A):
                cm_ref[par, g] = cmax[g]

    stage(jnp.int32(0), 0, None, None)

    def stage_pair(j, carry):
        stage(2 * j + 1, 1, 2 * j, 0)
        stage(2 * j + 2, 0, 2 * j + 1, 1)
        return carry

    last = n_ch - 1
    lax.fori_loop(0, last // 2, stage_pair, 0)

    @pl.when(last % 2 == 1)
    def _():
        stage(last, 1, last - 1, 0)
        stage(None, None, last, 1)

    @pl.when(last % 2 == 0)
    def _():
        stage(None, None, last, 0)

    heads = []
    for g in range(N_KV_A):
        acc = acc_ref[g]
        o = acc[:HEAD_DIM_A] / acc[HEAD_DIM_A:HEAD_DIM_A + 1]
        heads += [o[:, hh * bq:(hh + 1) * bq] for hh in range(GQA)]
    out_ref[...] = jnp.concatenate(heads, axis=0).T.astype(out_ref.dtype)


def _prompt_attention(qi_hm, qa_hm, wit, kibf, khm, vt):
    t = kibf.shape[0]
    bq = ATTN_BQ
    k_sel = min(TOPK_MAX, t // 4)
    kern = functools.partial(_attn_kernel, k_sel=k_sel, idx_bits=max(1, (t - 1).bit_length()))
    in_specs = [pl.BlockSpec((N_HEADS_IDX, bq, D_IDX), lambda i: (0, i, 0)),
                pl.BlockSpec((N_HEADS_A, bq, HEAD_DIM_A), lambda i: (0, i, 0)),
                pl.BlockSpec((N_HEADS_IDX, bq), lambda i: (0, i)),
                _const_spec(kibf.shape, True), _const_spec(khm.shape, True), _const_spec(vt.shape, True)]
    n_words = -(-(t // 32) // SEARCH_SLAB) * SEARCH_SLAB
    scratch = [pltpu.VMEM((t, bq), I32), pltpu.VMEM((32, n_words, bq), I32), pltpu.VMEM((n_words, bq), I32),
               pltpu.VMEM((1, bq), I32), pltpu.VMEM((ATTN_SC, bq), F32),
               pltpu.VMEM((2, N_KV_A, ATTN_SC, GQA * bq), F32), pltpu.VMEM((2, N_KV_A, 1, GQA * bq), F32),
               pltpu.VMEM((N_KV_A, 1, GQA * bq), F32),
               pltpu.VMEM((N_KV_A, VT_ROWS, GQA * bq), F32)]
    return pl.pallas_call(kern, out_shape=jax.ShapeDtypeStruct((t, WA_Q), BF16), grid=(t // bq,),
                          in_specs=in_specs, out_specs=pl.BlockSpec((bq, WA_Q), lambda i: (i, 0)),
                          scratch_shapes=scratch, compiler_params=_cparams(), name="attn")(
                              qi_hm, qa_hm, wit, kibf, khm, vt)


SAMPLE_CK = 2048


def _sample_attn_kernel(pt_ref, qi_ref, qa_ref, wi_ref, kin_ref, kn_ref, vn_ref, cidx_hbm, ck_hbm, cv_hbm, out_ref,
                        idx_buf, k_buf, v_buf, scr_ref, sems, *, n_pages, ps, ck, k_sel, idx_bits):
    b = pl.program_id(0)
    nb = pl.num_programs(0)
    past = n_pages * ps
    nc = past // ck

    def page_copy(which, bb, p):
        src, dst = ((cidx_hbm, idx_buf), (ck_hbm, k_buf), (cv_hbm, v_buf))[which]
        cols = pl.ds(pl.multiple_of(p * ps, ps), ps)
        dst = dst.at[:, cols] if which == 0 else dst.at[:, :, cols]
        return pltpu.make_async_copy(src.at[pt_ref[bb, p]], dst, sems.at[which])

    def for_pages(fn):
        lax.fori_loop(0, n_pages, lambda p, c: (fn(p), c)[1], 0, unroll=8)

    def wait_all(which):
        buf = (idx_buf, k_buf, v_buf)[which]
        pltpu.make_async_copy(buf, buf, sems.at[which]).wait()

    @pl.when(b == 0)
    def _():
        for_pages(lambda p: page_copy(0, b, p).start())

    for_pages(lambda p: (page_copy(1, b, p).start(), page_copy(2, b, p).start()))
    wait_all(0)

    qi = qi_ref[0]
    w = wi_ref[0]
    for c in range(nc):
        d = _dot(qi, idx_buf[:, c * ck:(c + 1) * ck].astype(BF16))
        scr_ref[c:c + 1, :] = (jnp.maximum(d, 0.0) * w).sum(axis=0, keepdims=True)
    d_new = (qi.astype(F32) * kin_ref[0].astype(BF16).astype(F32)).sum(axis=-1, keepdims=True)
    s_new = (jnp.maximum(d_new, 0.0) * w).sum(axis=0, keepdims=True)

    @pl.when(b + 1 < nb)
    def _():
        for_pages(lambda p: page_copy(0, b + 1, p).start())

    keys = _sort_key(scr_ref[...])
    key_new = _sort_key(s_new)
    pos = (lax.broadcasted_iota(I32, keys.shape, 0) * ck + lax.broadcasted_iota(I32, keys.shape, 1))

    def count(pred):
        hit = jnp.where(pred(keys, pos), 1, 0).sum(axis=1, keepdims=True).sum(axis=0, keepdims=True)
        return hit + jnp.where(pred(key_new, past), 1, 0)

    def search_cond(st):
        bit, _, cnt = st
        return jnp.logical_and(bit >= 0, jnp.max(jnp.where(cnt == k_sel, 0, 1)) > 0)

    def search_body(st):
        bit, thr, cnt = st
        cand = thr + (jnp.int32(1) << bit)
        c = count(lambda kk, _: kk >= cand)
        take = c >= k_sel
        return bit - 1, jnp.where(take, cand, thr), jnp.where(take, c, cnt)

    st0 = (jnp.int32(31), jnp.full((1, 1), INT_MIN, I32), jnp.full((1, 1), past + 1, I32))
    _, thr, cnt = lax.while_loop(search_cond, search_body, st0)

    want = k_sel - count(lambda kk, _: kk > thr)

    def tie_body(j, cpos):
        cand = cpos + (jnp.int32(1) << (idx_bits - 1 - j))
        below = count(lambda kk, pp: jnp.where(kk == thr, pp, INT_MAX) < cand)
        return jnp.where(below < want, cand, cpos)

    cpos = lax.fori_loop(0, idx_bits, tie_body, jnp.zeros((1, 1), I32))
    cut = jnp.where(cnt > k_sel, cpos, INT_MAX)

    def sel_bias(kk, pp):
        tie = jnp.where(pp <= cut, 0.0, NEG)
        return jnp.where(kk > thr, 0.0, jnp.where(kk == thr, tie, NEG))

    bias = sel_bias(keys, pos)
    bias_new = sel_bias(key_new, past)

    wait_all(1)
    wait_all(2)

    qa = qa_ref[0]
    heads = []
    for g in range(N_KV_A):
        qg = qa[g * GQA:(g + 1) * GQA]
        cols = slice(g * HEAD_DIM_A, (g + 1) * HEAD_DIM_A)
        m = jnp.full((GQA, 1), NEG, F32)
        l = jnp.zeros((GQA, 1), F32)
        acc = jnp.zeros((GQA, HEAD_DIM_A), F32)
        for c in range(nc):
            span = slice(c * ck, (c + 1) * ck)
            lg = _dot(qg, k_buf[g, :, span].astype(BF16)) + bias[c:c + 1, :]
            m_new = jnp.maximum(m, lg.max(axis=-1, keepdims=True))
            p = jnp.exp2(lg - m_new)
            alpha = jnp.exp2(m - m_new)
            l = alpha * l + p.sum(axis=-1, keepdims=True)
            acc = alpha * acc + _dot_nt(p.astype(BF16), v_buf[g, :, span].astype(BF16))
            m = m_new
        kn = kn_ref[0][:, cols].astype(BF16).astype(F32)
        vn = vn_ref[0][:, cols].astype(BF16).astype(F32)
        lg = (qg.astype(F32) * kn).sum(axis=-1, keepdims=True) + bias_new
        m_new = jnp.maximum(m, lg)
        p = jnp.exp2(lg - m_new)
        alpha = jnp.exp2(m - m_new)
        l = alpha * l + p
        acc = alpha * acc + p * vn
        o = acc / l
        heads += [o[hh:hh + 1, :] for hh in range(GQA)]
    out_ref[0] = jnp.concatenate(heads, axis=1)


def _sample_attention(page_table, qi_s, qa_s, wi_s, ki_new, k_new, v_new, cache_idx_k, cache_k, cache_v):
    nbatch, n_pages = page_table.shape
    n_pool, ps, d_idx = cache_idx_k.shape
    past = n_pages * ps
    ck = min(SAMPLE_CK, past)
    assert past % ck == 0 and ck % ps == 0
    k_sel = min(TOPK_MAX, (past + 1) // 4)
    kern = functools.partial(_sample_attn_kernel, n_pages=n_pages, ps=ps, ck=ck, k_sel=k_sel,
                             idx_bits=past.bit_length())
    per_b = lambda shape: pl.BlockSpec((1,) + shape, lambda b, pt: (b, 0, 0))
    any_spec = pl.BlockSpec(memory_space=pl.ANY)
    grid_spec = pltpu.PrefetchScalarGridSpec(
        num_scalar_prefetch=1, grid=(nbatch,),
        in_specs=[per_b((N_HEADS_IDX, d_idx)), per_b((N_HEADS_A, HEAD_DIM_A)), per_b((N_HEADS_IDX, 1)),
                  per_b((1, d_idx)), per_b((1, WA_KV)), per_b((1, WA_KV)), any_spec, any_spec, any_spec],
        out_specs=per_b((1, WA_Q)),
        scratch_shapes=[pltpu.VMEM((d_idx, past), F32), pltpu.VMEM((N_KV_A, HEAD_DIM_A, past), F32),
                        pltpu.VMEM((N_KV_A, HEAD_DIM_A, past), F32),
                        pltpu.VMEM((past // ck, ck), F32), pltpu.SemaphoreType.DMA((3,))])
    return pl.pallas_call(kern, out_shape=jax.ShapeDtypeStruct((nbatch, 1, WA_Q), F32), grid_spec=grid_spec,
                          compiler_params=_cparams(), name="sample_attn")(
                              page_table, qi_s, qa_s, wi_s, ki_new, k_new, v_new,
                              jnp.transpose(cache_idx_k, (0, 2, 1)), jnp.transpose(cache_k, (0, 2, 3, 1)),
                              jnp.transpose(cache_v, (0, 2, 3, 1)))


def _rope_tables(pos):
    rot = HEAD_DIM_A // ROT_FRACTION
    half = rot // 2
    inv_freq = ROPE_THETA ** (-jnp.arange(half, dtype=F32) / half)
    ang = pos.astype(F32)[:, None] * inv_freq[None, :]
    cos, sin = jnp.cos(ang), jnp.sin(ang)
    n = pos.shape[0]
    z = lambda w: jnp.zeros((n, w), F32)
    c = jnp.concatenate([cos, cos, jnp.ones((n, HEAD_DIM_A - rot), F32)], axis=1)
    sa = jnp.concatenate([-sin, z(HEAD_DIM_A - half)], axis=1)
    sb = jnp.concatenate([z(half), sin, z(HEAD_DIM_A - rot)], axis=1)
    rep = LANES // HEAD_DIM_A
    return tuple(jnp.tile(a, (1, rep)) for a in (c, sa, sb))


def _prep_params(w_in, g_in, g_qa, g_ka):
    d = w_in.shape[0]
    w_qa, w_ka, w_va, w_qi, w_ki, w_wi = _split_w_in(w_in, d)[:6]
    wa = jnp.concatenate([w_qa, w_ka, w_va, w_qi, w_ki, jnp.zeros((d, LANES - D_IDX), w_in.dtype)], axis=1)
    lane = jnp.arange(LANES)
    bd = (lane[:, None] // HEAD_DIM_A == lane[None, :] // HEAD_DIM_A).astype(BF16)
    rep = LANES // HEAD_DIM_A
    return (g_in.reshape(1, d), wa.astype(BF16), w_wi.T.astype(BF16), w_va.T.astype(BF16), bd,
            jnp.tile(g_qa, rep).reshape(1, LANES), jnp.tile(g_ka, rep).reshape(1, LANES))


def _split_w_in(w_in, d):
    sizes = (WA_Q, WA_KV, WA_KV, WI_Q, D_IDX, N_HEADS_IDX, WIDTH_B, WIDTH_B, WM_Q, N_BRANCH * d)
    parts, o = [], 0
    for s in sizes:
        parts.append(w_in[:, o:o + s])
        o += s
    return parts


def _mix_kernel(*refs, chunked, shared_mem, emit_vb, n_experts):
    it = iter(refs)
    x_ref, oa_ref, gin_ref, wb_ref, gvb_ref, gqm_ref = (next(it) for _ in range(6))
    if chunked:
        wtril_ref, bt_ref = next(it), next(it)
    else:
        wdiag_ref, bdiag_ref = next(it), next(it)
    mk_ref, mv_ref = next(it), next(it)
    wbra_ref, wbrb_ref, wbrm_ref, wout_ref, gffn_ref, wrh_ref, wrl_ref, br_ref = (next(it) for _ in range(8))
    x1_ref, h2_ref, te_ref, tw_ref = (next(it) for _ in range(4))
    vb_ref = next(it) if emit_vb else None

    x = x_ref[...]
    tm, d = x.shape
    h = _rms(x, gin_ref[...]).astype(BF16)
    p = _dot(h, wb_ref[...])
    ub = p[:, :WIDTH_B]
    vb = _rms(p[:, WIDTH_B:2 * WIDTH_B], gvb_ref[...])
    if emit_vb:
        vb_ref[...] = vb
    o = 2 * WIDTH_B
    qm = p[:, o:o + WM_Q]
    o += WM_Q
    gates = p[:, o:o + N_BRANCH * d]

    if chunked:
        vbb = vb.astype(BF16)
        rows = []
        for cc in range(tm // CHUNK):
            cols = []
            for g in range(N_GROUPS_B):
                vg = vbb[cc * CHUNK:(cc + 1) * CHUNK, g * GROUP_DIM_B:(g + 1) * GROUP_DIM_B]
                cols.append(_dot(wtril_ref[g], vg))
            rows.append(jnp.concatenate(cols, axis=1) + bt_ref[...])
        z = jnp.concatenate(rows, axis=0) if len(rows) > 1 else rows[0]
    else:
        z = vb * wdiag_ref[...] + bdiag_ref[...]
    out_b = ub * z

    scale_m = HEAD_DIM_M ** -0.5
    outs = []
    for hh in range(N_HEADS_M):
        sl = slice(hh * HEAD_DIM_M, (hh + 1) * HEAD_DIM_M)
        qh = _rms(qm[:, sl], gqm_ref[...])
        if shared_mem:
            lg = _dot_nt(qh.astype(BF16), mk_ref[:, sl].astype(BF16)) * scale_m
            pm = jnp.exp(lg - lg.max(axis=-1, keepdims=True))
            pm = pm / pm.sum(axis=-1, keepdims=True)
            outs.append(_dot(pm.astype(BF16), mv_ref[:, sl].astype(BF16)))
        else:
            per_row = []
            for r in range(tm):
                kr = mk_ref[r, :, sl]
                lg = (kr * qh[r:r + 1, :]).sum(axis=-1, keepdims=True) * scale_m
                pm = jnp.exp(lg - lg.max(axis=0, keepdims=True))
                pm = pm / pm.sum(axis=0, keepdims=True)
                per_row.append((pm * mv_ref[r, :, sl]).sum(axis=0, keepdims=True))
            outs.append(jnp.concatenate(per_row, axis=0))
    out_m = jnp.concatenate(outs, axis=1)

    sig = jax.nn.sigmoid
    merged = (sig(gates[:, :d]) * _dot(oa_ref[...].astype(BF16), wbra_ref[...])
              + sig(gates[:, d:2 * d]) * _dot(out_b.astype(BF16), wbrb_ref[...])
              + sig(gates[:, 2 * d:]) * _dot(out_m.astype(BF16), wbrm_ref[...]))
    x1 = x + _dot(merged.astype(BF16), wout_ref[...])
    x1_ref[...] = x1
    h2 = _rms(x1, gffn_ref[...])
    h2_ref[...] = h2.astype(h2_ref.dtype)

    hi = h2.astype(BF16)
    lo = (h2 - hi.astype(F32)).astype(BF16)
    lg = _dot(hi, wrh_ref[...]) + _dot(hi, wrl_ref[...]) + _dot(lo, wrh_ref[...]) + br_ref[...]
    lane = lax.broadcasted_iota(I32, lg.shape, 1)
    lg = jnp.where(lane < n_experts, lg, -jnp.inf)
    vals, idxs = [], []
    for _ in range(TOP_K_MOE):
        m = lg.max(axis=-1, keepdims=True)
        idx = jnp.where(lg == m, lane, LANES).min(axis=-1, keepdims=True)
        vals.append(m)
        idxs.append(idx)
        lg = jnp.where(lane == idx, -jnp.inf, lg)
    ex = [jnp.exp(v - vals[0]) for v in vals]
    den = ex[0]
    for e in ex[1:]:
        den = den + e
    te = jnp.full(lane.shape, -1, I32)
    tw = jnp.zeros(lane.shape, F32)
    for j in range(TOP_K_MOE):
        te = jnp.where(lane == j, idxs[j], te)
        tw = jnp.where(lane == j, ex[j] / den, tw)
    te_ref[...] = te
    tw_ref[...] = tw


def _mix(x2d, out_a, mix_params, gmlp_params, mem_k, mem_v, *, tm, chunked, shared_mem, emit_vb, n_experts):
    n, d = x2d.shape
    gin, wb, gvb, gqm, wbra, wbrb, wbrm, wout, gffn, wrh, wrl, br = mix_params
    row = lambda w: pl.BlockSpec((tm, w), lambda i: (i, 0))
    if shared_mem:
        mem_specs = [_const_spec(mem_k.shape), _const_spec(mem_v.shape)]
    else:
        mspec = pl.BlockSpec((tm,) + mem_k.shape[1:], lambda i: (i, 0, 0))
        mem_specs = [mspec, mspec]
    consts = lambda arrs: [_const_spec(a.shape) for a in arrs]
    in_specs = ([row(d), row(WA_Q)] + consts([gin, wb, gvb, gqm]) + consts(gmlp_params) + mem_specs
                + consts([wbra, wbrb, wbrm, wout, gffn, wrh, wrl, br]))
    out_shape = [jax.ShapeDtypeStruct((n, d), F32), jax.ShapeDtypeStruct((n, d), F32),
                 jax.ShapeDtypeStruct((n, LANES), I32), jax.ShapeDtypeStruct((n, LANES), F32)]
    out_specs = [row(d), row(d), row(LANES), row(LANES)]
    if emit_vb:
        out_shape.append(jax.ShapeDtypeStruct((n, WIDTH_B), F32))
        out_specs.append(row(WIDTH_B))
    kern = functools.partial(_mix_kernel, chunked=chunked, shared_mem=shared_mem, emit_vb=emit_vb,
                             n_experts=n_experts)
    return pl.pallas_call(kern, out_shape=tuple(out_shape), grid=(n // tm,), in_specs=in_specs,
                          out_specs=tuple(out_specs), compiler_params=_cparams(), name="mix")(
                              x2d, out_a, gin, wb, gvb, gqm, *gmlp_params, mem_k, mem_v,
                              wbra, wbrb, wbrm, wout, gffn, wrh, wrl, br)


def _prep_mix_params(w_in, g_in, g_vb, g_qm, w_br_a, w_br_b, w_br_m, w_out, g_ffn, w_router, b_router):
    d = w_in.shape[0]
    parts = _split_w_in(w_in, d)
    wb = jnp.concatenate(parts[6:10], axis=1).astype(BF16)
    n_e = w_router.shape[1]
    wr = jnp.pad(w_router, ((0, 0), (0, LANES - n_e)))
    wrh = wr.astype(BF16)
    wrl = (wr - wrh.astype(F32)).astype(BF16)
    br = jnp.pad(b_router, (0, LANES - n_e)).reshape(1, LANES)
    return (g_in.reshape(1, d), wb, g_vb.reshape(1, WIDTH_B), g_qm.reshape(1, HEAD_DIM_M),
            w_br_a.astype(BF16), w_br_b.astype(BF16), w_br_m.astype(BF16), w_out.astype(BF16),
            g_ffn.reshape(1, d), wrh, wrl, br)


def _prep_gmlp_chunked(w_s, b_s):
    tril = jnp.tril(jnp.ones((CHUNK, CHUNK), w_s.dtype))
    wtril = (w_s * tril).astype(BF16)
    bt = jnp.repeat(b_s.T, GROUP_DIM_B, axis=1)
    return wtril, bt


def _prep_gmlp_single(w_s, b_s):
    wdiag = jnp.repeat(w_s[:, 0, 0], GROUP_DIM_B).reshape(1, WIDTH_B)
    bdiag = jnp.repeat(b_s[:, 0], GROUP_DIM_B).reshape(1, WIDTH_B)
    return wdiag, bdiag


MOE_TP = 256


def _plan_kernel(te_ref, tri_ref, upper_ref, lslot_ref, cnt_ref, tstart_ref, gprev_ref, total_ref, carry_ref):
    @pl.when(pl.program_id(0) == 0)
    def _():
        carry_ref[...] = jnp.zeros(carry_ref.shape, F32)

    te = te_ref[...]
    lane = lax.broadcasted_iota(I32, te.shape, 1)
    picks = [te[:, j:j + 1] for j in range(TOP_K_MOE)]
    onehot = jnp.zeros(te.shape, F32)
    for e in picks:
        onehot = onehot + jnp.where(lane == e, 1.0, 0.0)
    before = _dot(tri_ref[...], onehot.astype(BF16))
    cnt = onehot.sum(axis=0, keepdims=True)
    cnt = jnp.floor((cnt + (SUBLANES - 1)) * (1.0 / SUBLANES)) * SUBLANES
    cnt8 = jnp.broadcast_to(cnt, (SUBLANES, LANES)).astype(BF16)
    tstart = _dot(cnt8, upper_ref[...])[0:1, :]
    lslot = jnp.zeros(te.shape, I32)
    for j, e in enumerate(picks):
        r = jnp.where(lane == e, before + tstart, 0.0).sum(axis=-1, keepdims=True).astype(I32)
        lslot = jnp.where(lane == j, r, lslot)
    lslot_ref[...] = lslot
    cnt_ref[0] = cnt.astype(I32)
    tstart_ref[0] = tstart.astype(I32)
    gprev_ref[0] = carry_ref[...].astype(I32)
    carry_ref[...] = carry_ref[...] + cnt
    total_ref[...] = carry_ref[...].astype(I32)


def _moe_plan(te_all):
    n = te_all.shape[0]
    tp = MOE_TP
    n_tiles = n // tp
    r = jnp.arange(tp)
    tri = (r[:, None] > r[None, :]).astype(BF16)
    e = jnp.arange(LANES)
    upper = (e[:, None] < e[None, :]).astype(BF16)
    per_tile = jax.ShapeDtypeStruct((n_tiles, 1, LANES), I32)
    tile_spec = pl.BlockSpec((1, 1, LANES), lambda i: (i, 0, 0))
    return pl.pallas_call(
        _plan_kernel,
        out_shape=(jax.ShapeDtypeStruct((n, LANES), I32), per_tile, per_tile, per_tile,
                   jax.ShapeDtypeStruct((1, LANES), I32)),
        grid=(n_tiles,),
        in_specs=[pl.BlockSpec((tp, LANES), lambda i: (i, 0)), _const_spec((tp, tp)), _const_spec((LANES, LANES))],
        out_specs=(pl.BlockSpec((tp, LANES), lambda i: (i, 0)), tile_spec, tile_spec, tile_spec,
                   _const_spec((1, LANES))),
        scratch_shapes=[pltpu.VMEM((1, LANES), F32)],
        compiler_params=_cparams(), name="moe_plan")(te_all, tri, upper)


def _run_pieces(tp):
    sizes, s = [], SUBLANES
    while s <= tp:
        sizes.append(s)
        s *= 2
    return tuple(reversed(sizes))


def _local_rows(tp, n_e):
    return tp * TOP_K_MOE + n_e * SUBLANES


def _for_runs(tabs, tile, n_e, tp, fn):
    cnt_s, tstart_s, gbase_s = tabs

    def per_expert(e, c):
        idx = tile * n_e + e
        length, src0, dst0 = cnt_s[idx], tstart_s[idx], gbase_s[idx]
        off = jnp.int32(0)
        for size in _run_pieces(tp):
            piece = length & size

            @pl.when(piece != 0)
            def _():
                fn(pl.multiple_of(src0 + off, SUBLANES), pl.multiple_of(dst0 + off, SUBLANES), size)
            off = off + piece
        return c

    lax.fori_loop(0, n_e, per_expert, 0)


def _slot_matrix(lslot, vals, n_rows):
    col = lax.broadcasted_iota(I32, (lslot.shape[0], n_rows), 1)
    m = jnp.zeros(col.shape, F32)
    for j in range(TOP_K_MOE):
        v = 1.0 if vals is None else vals[:, j:j + 1]
        m = m + jnp.where(lslot[:, j:j + 1] == col, v, 0.0)
    return m


def _dispatch_kernel(cnt_s, tstart_s, gbase_s, fstart_s, flen_s, nused_s, lslot_ref, h2_ref, *rest,
                     tp, n_e, tile0, first, bm):
    if first:
        xs_out, stage, sem, zbuf, zsem = rest
    else:
        _, xs_out, stage, sem = rest
    n_rows = _local_rows(tp, n_e)
    i = pl.program_id(0)
    slot = i % 2
    pt = _slot_matrix(lslot_ref[...], None, n_rows).astype(BF16)
    stage[slot] = lax.dot_general(pt, h2_ref[...].astype(BF16), (((0,), (0,)), ((), ())),
                                  preferred_element_type=F32)

    def runs(tile, s, op):
        _for_runs((cnt_s, tstart_s, gbase_s), tile0 + tile, n_e, tp,
                  lambda src, dst, size: op(pltpu.make_async_copy(
                      stage.at[s, pl.ds(src, size), :], xs_out.at[pl.ds(dst, size), :], sem.at[s])))

    @pl.when(i > 0)
    def _():
        runs(i - 1, 1 - slot, lambda cp: cp.wait())

    runs(i, slot, lambda cp: cp.start())

    @pl.when(i == pl.num_programs(0) - 1)
    def _():
        runs(i, slot, lambda cp: cp.wait())
        if first:
            zbuf[...] = jnp.zeros(zbuf.shape, F32)
            nb = xs_out.shape[0] // bm

            def fill(op):
                def per_expert(e, c):
                    length, dst0 = flen_s[e], fstart_s[e]
                    off = jnp.int32(0)
                    for size in _run_pieces(bm):
                        piece = length & size

                        @pl.when(piece != 0)
                        def _():
                            op(pltpu.make_async_copy(
                                zbuf.at[pl.ds(0, size), :],
                                xs_out.at[pl.ds(pl.multiple_of(dst0 + off, SUBLANES), size), :], zsem))
                        off = off + piece
                    return c

                lax.fori_loop(0, n_e, per_expert, 0)
                lax.fori_loop(nused_s[0], nb, lambda b, c: (op(pltpu.make_async_copy(
                    zbuf, xs_out.at[pl.ds(pl.multiple_of(b * bm, bm), bm), :], zsem)), c)[1], 0)

            fill(lambda cp: cp.start())
            fill(lambda cp: cp.wait())


def _dispatch(tabs, fill_tabs, tile0, lslot, h2, xs, n_e, ns, bm):
    n, d = h2.shape
    tp = min(MOE_TP, n)
    n_rows = _local_rows(tp, n_e)
    first = xs is None
    kern = functools.partial(_dispatch_kernel, tp=tp, n_e=n_e, tile0=tile0, first=first, bm=bm)
    any_spec = pl.BlockSpec(memory_space=pl.ANY)
    scratch = [pltpu.VMEM((2, n_rows, d), F32), pltpu.SemaphoreType.DMA((2,))]
    if first:
        scratch += [pltpu.VMEM((bm, d), F32), pltpu.SemaphoreType.DMA]
    grid_spec = pltpu.PrefetchScalarGridSpec(
        num_scalar_prefetch=6, grid=(n // tp,),
        in_specs=[pl.BlockSpec((tp, LANES), lambda i, *_: (i, 0)), pl.BlockSpec((tp, d), lambda i, *_: (i, 0))]
        + ([] if first else [any_spec]),
        out_specs=any_spec, scratch_shapes=scratch)
    args = (*tabs, *fill_tabs, lslot, h2) + (() if first else (xs,))
    return pl.pallas_call(
        kern, out_shape=jax.ShapeDtypeStruct((ns, d), F32), grid_spec=grid_spec,
        input_output_aliases={} if first else {8: 0},
        compiler_params=pltpu.CompilerParams(dimension_semantics=("arbitrary",), has_side_effects=True,
                                             vmem_limit_bytes=VMEM_LIMIT),
        name="dispatch")(*args)


def _expert_kernel(be_ref, nu_ref, xs_ref, wg_ref, wu_ref, wd_ref, bg_ref, bu_ref, bd_ref, y_ref,
                   wg_bf, wu_bf, wd_bf):
    b = pl.program_id(0)

    @pl.when(b >= nu_ref[0])
    def _():
        y_ref[...] = jnp.zeros(y_ref.shape, F32)

    @pl.when(b < nu_ref[0])
    def _():
        prev = be_ref[jnp.maximum(b - 1, 0)]

        @pl.when(jnp.logical_or(b == 0, be_ref[b] != prev))
        def _():
            wg_bf[...] = wg_ref[0].astype(BF16)
            wu_bf[...] = wu_ref[0].astype(BF16)
            wd_bf[...] = wd_ref[0].astype(BF16)

        xb = xs_ref[...].astype(BF16)
        ff = wg_bf.shape[1]
        y = jnp.zeros(y_ref.shape, F32) + bd_ref[0]
        for n in range(ff // FF_CHUNK):
            sl = slice(n * FF_CHUNK, (n + 1) * FF_CHUNK)
            hg = jnp.minimum(_dot(xb, wg_bf[:, sl]) + bg_ref[0][:, sl], SWIGLU_LIMIT)
            hu = jnp.clip(_dot(xb, wu_bf[:, sl]) + bu_ref[0][:, sl], -SWIGLU_LIMIT, SWIGLU_LIMIT)
            act = hg * jax.nn.sigmoid(SWIGLU_ALPHA * hg) * (hu + 1.0)
            y = y + _dot(act.astype(BF16), wd_bf[sl, :])
        y_ref[...] = y


def _experts(xs, block_e, n_used, w_gate, b_gate, w_up, b_up, w_down, b_down):
    ns, d = xs.shape
    n_e, _, ff = w_gate.shape
    bm = MOE_BM
    blk = lambda b, be, nu: (jnp.minimum(b, nu[0] - 1), 0)
    wsel = lambda b, be, nu: (be[b], 0, 0)
    grid_spec = pltpu.PrefetchScalarGridSpec(
        num_scalar_prefetch=2, grid=(ns // bm,),
        in_specs=[pl.BlockSpec((bm, d), blk),
                  pl.BlockSpec((1, d, ff), wsel), pl.BlockSpec((1, d, ff), wsel), pl.BlockSpec((1, ff, d), wsel),
                  pl.BlockSpec((1, 1, ff), wsel), pl.BlockSpec((1, 1, ff), wsel), pl.BlockSpec((1, 1, d), wsel)],
        out_specs=pl.BlockSpec((bm, d), lambda b, be, nu: (b, 0)),
        scratch_shapes=[pltpu.VMEM((d, ff), BF16), pltpu.VMEM((d, ff), BF16), pltpu.VMEM((ff, d), BF16)])
    return pl.pallas_call(_expert_kernel, out_shape=jax.ShapeDtypeStruct((ns, d), F32), grid_spec=grid_spec,
                          compiler_params=_cparams(), name="experts")(
                              block_e, n_used, xs, w_gate, w_up, w_down,
                              b_gate.reshape(n_e, 1, ff), b_up.reshape(n_e, 1, ff), b_down.reshape(n_e, 1, d))


def _combine_kernel(cnt_s, tstart_s, gbase_s, lslot_ref, tw_ref, x1_ref, yb_hbm, y_ref, buf, sem, *, tp, n_e, tile0):
    n_rows = _local_rows(tp, n_e)
    i = pl.program_id(0)
    slot = i % 2

    def runs(tile, s, op):
        _for_runs((cnt_s, tstart_s, gbase_s), tile0 + tile, n_e, tp,
                  lambda loc, glob, size: op(pltpu.make_async_copy(
                      yb_hbm.at[pl.ds(glob, size), :], buf.at[s, pl.ds(loc, size), :], sem.at[s])))

    @pl.when(i == 0)
    def _():
        buf[...] = jnp.zeros(buf.shape, F32)
        runs(i, slot, lambda cp: cp.start())

    @pl.when(i + 1 < pl.num_programs(0))
    def _():
        runs(i + 1, 1 - slot, lambda cp: cp.start())

    runs(i, slot, lambda cp: cp.wait())

    ptw = _slot_matrix(lslot_ref[...], tw_ref[...], n_rows)
    rows = buf[slot]
    w_hi = ptw.astype(BF16)
    w_lo = (ptw - w_hi.astype(F32)).astype(BF16)
    r_hi = rows.astype(BF16)
    r_lo = (rows - r_hi.astype(F32)).astype(BF16)
    y_ref[...] = x1_ref[...] + _dot(w_hi, r_hi) + (_dot(w_hi, r_lo) + _dot(w_lo, r_hi))


def _combine(tabs, tile0, lslot, tw, x1, yb, n_e):
    n, d = x1.shape
    tp = min(MOE_TP, n)
    n_rows = _local_rows(tp, n_e)
    kern = functools.partial(_combine_kernel, tp=tp, n_e=n_e, tile0=tile0)
    row = lambda w: pl.BlockSpec((tp, w), lambda i, *_: (i, 0))
    grid_spec = pltpu.PrefetchScalarGridSpec(
        num_scalar_prefetch=3, grid=(n // tp,),
        in_specs=[row(LANES), row(LANES), row(d), pl.BlockSpec(memory_space=pl.ANY)],
        out_specs=row(d),
        scratch_shapes=[pltpu.VMEM((2, n_rows, d), F32), pltpu.SemaphoreType.DMA((2,))])
    return pl.pallas_call(kern, out_shape=jax.ShapeDtypeStruct(x1.shape, F32), grid_spec=grid_spec,
                          compiler_params=_cparams(), name="combine")(*tabs, lslot, tw, x1, yb)


def _moe(groups, w_gate, b_gate, w_up, b_up, w_down, b_down):
    n_e = w_gate.shape[0]
    d = groups[0][0].shape[1]
    tp, bm = MOE_TP, MOE_BM
    sizes = [g[0].shape[0] for g in groups]
    assert all(n % tp == 0 or n < tp for n in sizes), sizes
    parts, tile0s, o = [], [], 0
    for g, n in zip(groups, sizes):
        n_pad = -(-n // tp) * tp
        parts.append(jnp.pad(g[2], ((0, n_pad - n), (0, 0)), constant_values=-1))
        tile0s.append(o // tp)
        o += n_pad
    lslot, cnt3, tstart3, gprev3, total = _moe_plan(jnp.concatenate(parts, axis=0))
    counts = total[0, :n_e]
    padded_cnt = (counts + bm - 1) // bm * bm
    pad_end = jnp.cumsum(padded_cnt)
    pad_start = pad_end - padded_cnt
    n_assign = sum(sizes) * TOP_K_MOE
    n_slack = (o // tp) * n_e * (SUBLANES - 1)
    nb = -(-(n_assign + n_slack) // bm) + n_e
    n_used = (pad_end[-1] // bm).astype(I32)
    blocks = jnp.arange(nb, dtype=I32)
    block_e = jnp.minimum((pad_end[None, :] <= (blocks * bm)[:, None]).sum(axis=1), n_e - 1).astype(I32)
    block_e = jnp.where(blocks < n_used, block_e, block_e[jnp.maximum(n_used - 1, 0)])
    flat = lambda a: a[:, 0, :n_e].reshape(-1).astype(I32)
    gbase3 = pad_start[None, None, :] + gprev3[:, :, :n_e]
    tabs = (flat(cnt3), flat(tstart3), flat(gbase3))

    after_first = gprev3[tile0s[1], 0, :n_e] if len(groups) > 1 else counts
    fill_start = (pad_start + after_first).astype(I32)
    fill_tabs = (fill_start, (pad_end - fill_start).astype(I32), n_used.reshape(1))
    xs, lslots = None, []
    for (x1, h2, te, tw), n, t0 in zip(groups, sizes, tile0s):
        ls = lslot[t0 * tp:t0 * tp + n]
        lslots.append(ls)
        xs = _dispatch(tabs, fill_tabs, t0, ls, h2, xs, n_e, nb * bm, bm)
    yb = _experts(xs, block_e, n_used.reshape(1), w_gate, b_gate, w_up, b_up, w_down, b_down)
    return [_combine(tabs, t0, ls, tw, x1, yb, n_e)
            for (x1, h2, te, tw), ls, t0 in zip(groups, lslots, tile0s)]


def _row_tile(n, want):
    return want if n % want == 0 else n


def kernel(x_prompt, x_sample, mem_prompt, cache_k, cache_v, cache_idx_k, cache_mem_k, cache_mem_v, page_table,
           g_in, w_in, g_qa, g_ka, g_vb, w_s, b_s, g_qm, g_mem, w_mem_kv, g_km, w_br_a, w_br_b, w_br_m, w_out,
           g_ffn, w_router, b_router, w_gate, b_gate, w_up, b_up, w_down, b_down):
    bp, t, d = x_prompt.shape
    bs, ts, _ = x_sample.shape
    assert bp == 1 and ts == 1 and t % CHUNK == 0 and cache_k.shape[1] == LANES
    n_mem = mem_prompt.shape[1]
    n_pages = page_table.shape[1]
    ps = cache_k.shape[1]
    past = n_pages * ps
    n_e = w_router.shape[1]

    fp = _prep_params(w_in, g_in, g_qa, g_ka)
    mp = _prep_mix_params(w_in, g_in, g_vb, g_qm, w_br_a, w_br_b, w_br_m, w_out, g_ffn, w_router, b_router)

    xp = x_prompt.reshape(t, d)
    qa_p, k32_p, khm_p, v32_p, vt_p, qi_p, ki32_p, kibf_p, wit_p = _front(
        xp, _rope_tables(jnp.arange(t, dtype=I32)), *fp, tm=_row_tile(t, 512), paged=True)
    mem_k, mem_v = _memory_kv(mem_prompt.reshape(n_mem, d), g_mem.reshape(1, d), w_mem_kv.astype(BF16),
                              g_km.reshape(1, HEAD_DIM_M))
    out_a_p = _prompt_attention(qi_p, qa_p, wit_p, kibf_p, khm_p, vt_p)
    x1_p, h2_p, te_p, tw_p = _mix(xp, out_a_p, mp, _prep_gmlp_chunked(w_s, b_s), mem_k, mem_v,
                                  tm=_row_tile(t, 256), chunked=True, shared_mem=True, emit_vb=False, n_experts=n_e)

    xs = x_sample.reshape(bs, d)
    qa_s, k32_s, _, v32_s, _, qi_s, ki32_s, _, wit_s = _front(
        xs, _rope_tables(jnp.full((bs,), past, I32)), *fp, tm=bs, paged=False)
    out_a_s = _sample_attention(page_table, jnp.moveaxis(qi_s, 0, 1), jnp.moveaxis(qa_s, 0, 1),
                                wit_s.T.reshape(bs, N_HEADS_IDX, 1), ki32_s.reshape(bs, 1, D_IDX),
                                k32_s.reshape(bs, 1, WA_KV), v32_s.reshape(bs, 1, WA_KV),
                                cache_idx_k, cache_k, cache_v)
    x1_s, h2_s, te_s, tw_s, vb_s = _mix(xs, out_a_s.reshape(bs, WA_Q), mp, _prep_gmlp_single(w_s, b_s),
                                        cache_mem_k.reshape(bs, n_mem, WM_Q), cache_mem_v.reshape(bs, n_mem, WM_Q),
                                        tm=_row_tile(bs, SUBLANES), chunked=False, shared_mem=False, emit_vb=True,
                                        n_experts=n_e)

    y_p, y_s = _moe([(x1_p, h2_p, te_p, tw_p), (x1_s, h2_s, te_s, tw_s)],
                    w_gate, b_gate, w_up, b_up, w_down, b_down)

    n_pg = t // ps
    paged_kv = lambda a: jnp.transpose(a.reshape(1, n_pg, N_KV_A, HEAD_DIM_A, ps), (0, 1, 4, 2, 3))
    return (y_p.reshape(1, t, d), y_s.reshape(bs, 1, d),
            paged_kv(k32_p), paged_kv(v32_p), jnp.transpose(ki32_p, (0, 2, 1)).reshape(1, n_pg, ps, D_IDX),
            mem_k.reshape(1, n_mem, N_HEADS_M, HEAD_DIM_M), mem_v.reshape(1, n_mem, N_HEADS_M, HEAD_DIM_M),
            k32_s.reshape(bs, 1, N_KV_A, HEAD_DIM_A), v32_s.reshape(bs, 1, N_KV_A, HEAD_DIM_A),
            ki32_s.reshape(bs, 1, D_IDX), vb_s.reshape(bs, 1, WIDTH_B))
```

```python
import functools

import jax
import jax.numpy as jnp
from jax import lax
from jax.experimental import pallas as pl
from jax.experimental.pallas import tpu as pltpu

F32 = jnp.float32
BF16 = jnp.bfloat16
I32 = jnp.int32

N_HEADS_A = 8
N_KV_A = 2
HEAD_DIM_A = 64
TOPK_MAX = 256
N_HEADS_IDX = 8
D_IDX = 64
N_GROUPS_B = 4
GROUP_DIM_B = 128
WIDTH_B = N_GROUPS_B * GROUP_DIM_B
CHUNK = 128
N_HEADS_M = 4
HEAD_DIM_M = 128
ROPE_THETA = 500000.0
ROT_FRACTION = 4
N_BRANCH = 3
TOP_K_MOE = 4
SWIGLU_LIMIT = 7.0
SWIGLU_ALPHA = 1.702
EPS = 1e-6

WA_Q = N_HEADS_A * HEAD_DIM_A
WA_KV = N_KV_A * HEAD_DIM_A
WI_Q = N_HEADS_IDX * D_IDX
WM_Q = N_HEADS_M * HEAD_DIM_M
GQA = N_HEADS_A // N_KV_A

LANES = 128
SUBLANES = 8
VMEM_LIMIT = 56 * 1024 * 1024

LOG2_E = 1.4426950408889634
INT_MIN = -(2 ** 31)
INT_MAX = 2 ** 31 - 1
NEG = -1e30

VT_ROWS = HEAD_DIM_A + 16
ATTN_BQ = 128
ATTN_SC = 512
MOE_BM = 512
FF_CHUNK = 512


def _cparams(n_axes=1, vmem=VMEM_LIMIT):
    return pltpu.CompilerParams(dimension_semantics=("arbitrary",) * n_axes, vmem_limit_bytes=vmem)


def _const_spec(shape, single=False):
    zeros = (0,) * len(shape)
    if single:
        return pl.BlockSpec(shape, lambda *_: zeros, pipeline_mode=pl.Buffered(1))
    return pl.BlockSpec(shape, lambda *_: zeros)


def _dot(a, b):
    return jnp.dot(a, b, preferred_element_type=F32)


def _dot_nt(a, b):
    return lax.dot_general(a, b, (((1,), (1,)), ((), ())), preferred_element_type=F32)


def _rms(x, g):
    return x * lax.rsqrt(jnp.mean(x * x, axis=-1, keepdims=True) + EPS) * g


def _split_dot(x, w):
    hi = x.astype(BF16)
    lo = (x - hi.astype(F32)).astype(BF16)
    return _dot(hi, w) + _dot(lo, w)


def _front_kernel(x_ref, gin_ref, wa_ref, wwit_ref, wvt_ref, bd_ref, gqa_ref, gka_ref, cos_ref, sa_ref, sb_ref,
                  qa_ref, k32_ref, khm_ref, v32_ref, vt_ref, qi_ref, ki32_ref, kibf_ref, wit_ref, *, score_scale, paged):
    x = x_ref[...]
    h = _rms(x, gin_ref[...]).astype(BF16)
    p = _dot(h, wa_ref[...])
    cos, sa, sb = cos_ref[...], sa_ref[...], sb_ref[...]
    bd = bd_ref[...]

    def head_norm(v, g):
        ssq = _split_dot(v * v, bd)
        return v * lax.rsqrt(ssq * (1.0 / HEAD_DIM_A) + EPS) * g

    def rope(v):
        return v * cos + pltpu.roll(v, LANES - 8, 1) * sa + pltpu.roll(v, 8, 1) * sb

    for s in range(WA_Q // LANES):
        v = p[:, s * LANES:(s + 1) * LANES]
        v = rope(head_norm(v, gqa_ref[...])) * (HEAD_DIM_A ** -0.5 * LOG2_E)
        vb = v.astype(BF16)
        qa_ref[2 * s] = vb[:, :HEAD_DIM_A]
        qa_ref[2 * s + 1] = vb[:, HEAD_DIM_A:]
    o = WA_Q
    n_pg = x.shape[0] // LANES
    k = rope(head_norm(p[:, o:o + WA_KV], gka_ref[...]))
    if paged:
        for j in range(n_pg):
            k32_ref[j] = k[j * LANES:(j + 1) * LANES, :].T
    else:
        k32_ref[...] = k
    kb = k.astype(BF16)
    khm_ref[0] = kb[:, :HEAD_DIM_A]
    khm_ref[1] = kb[:, HEAD_DIM_A:]
    o += WA_KV
    if not paged:
        v32_ref[...] = p[:, o:o + WA_KV]
    o += WA_KV
    for s in range(WI_Q // LANES):
        vb = rope(p[:, o + s * LANES:o + (s + 1) * LANES]).astype(BF16)
        qi_ref[2 * s] = vb[:, :D_IDX]
        qi_ref[2 * s + 1] = vb[:, D_IDX:]
    o += WI_Q
    ki_slab = rope(p[:, o:o + LANES])
    ki = ki_slab[:, :D_IDX]
    if paged:
        for j in range(n_pg):
            ki32_ref[j] = ki_slab[j * LANES:(j + 1) * LANES, :].T[:D_IDX, :]
    else:
        ki32_ref[...] = ki
    kibf_ref[...] = ki.astype(BF16)
    wit_ref[...] = _dot_nt(wwit_ref[...], h) * score_scale
    vt = _dot_nt(wvt_ref[...], h)
    if paged:
        for j in range(n_pg):
            v32_ref[j] = vt[:, j * LANES:(j + 1) * LANES]
    tail = jnp.where(lax.broadcasted_iota(I32, (VT_ROWS - HEAD_DIM_A, vt.shape[1]), 0) == 0, 1.0, 0.0)
    for g in range(N_KV_A):
        vt_ref[g] = jnp.concatenate([vt[g * HEAD_DIM_A:(g + 1) * HEAD_DIM_A], tail], axis=0).astype(BF16)


def _front(x2d, tabs, gin, wa, wwit, wvt, bd, gqa, gka, tm, paged):
    n, d = x2d.shape
    cos, sa, sb = tabs
    row = lambda w: pl.BlockSpec((tm, w), lambda i: (i, 0))
    hm = lambda nh, w: pl.BlockSpec((nh, tm, w), lambda i: (0, i, 0))
    if paged:
        assert tm % LANES == 0
        f32_out = lambda w: jax.ShapeDtypeStruct((n // LANES, w, LANES), F32)
        f32_spec = lambda w: pl.BlockSpec((tm // LANES, w, LANES), lambda i: (i, 0, 0))
    else:
        f32_out = lambda w: jax.ShapeDtypeStruct((n, w), F32)
        f32_spec = row
    out_shape = (
        jax.ShapeDtypeStruct((N_HEADS_A, n, HEAD_DIM_A), BF16),
        f32_out(WA_KV),
        jax.ShapeDtypeStruct((N_KV_A, n, HEAD_DIM_A), BF16),
        f32_out(WA_KV),
        jax.ShapeDtypeStruct((N_KV_A, VT_ROWS, n), BF16),
        jax.ShapeDtypeStruct((N_HEADS_IDX, n, D_IDX), BF16),
        f32_out(D_IDX),
        jax.ShapeDtypeStruct((n, D_IDX), BF16),
        jax.ShapeDtypeStruct((N_HEADS_IDX, n), F32),
    )
    out_specs = (hm(N_HEADS_A, HEAD_DIM_A), f32_spec(WA_KV), hm(N_KV_A, HEAD_DIM_A), f32_spec(WA_KV),
                 pl.BlockSpec((N_KV_A, VT_ROWS, tm), lambda i: (0, 0, i)), hm(N_HEADS_IDX, D_IDX), f32_spec(D_IDX),
                 row(D_IDX), pl.BlockSpec((N_HEADS_IDX, tm), lambda i: (0, i)))
    in_specs = [row(d), _const_spec(gin.shape), _const_spec(wa.shape), _const_spec(wwit.shape),
                _const_spec(wvt.shape), _const_spec(bd.shape), _const_spec(gqa.shape), _const_spec(gka.shape),
                row(LANES), row(LANES), row(LANES)]
    kern = functools.partial(_front_kernel, score_scale=D_IDX ** -0.5 * N_HEADS_IDX ** -0.5, paged=paged)
    return pl.pallas_call(kern, out_shape=out_shape, grid=(n // tm,), in_specs=in_specs, out_specs=out_specs,
                          compiler_params=_cparams(), name="front")(x2d, gin, wa, wwit, wvt, bd, gqa, gka, cos, sa, sb)


def _memkv_kernel(mem_ref, gmem_ref, w_ref, gkm_ref, k32_ref, v32_ref):
    h = _rms(mem_ref[...], gmem_ref[...]).astype(BF16)
    kv = _dot(h, w_ref[...])
    for hh in range(N_HEADS_M):
        sl = slice(hh * HEAD_DIM_M, (hh + 1) * HEAD_DIM_M)
        k32_ref[:, sl] = _rms(kv[:, sl], gkm_ref[...])
    v32_ref[...] = kv[:, WM_Q:]


def _memory_kv(mem2d, gmem, w, gkm):
    n = mem2d.shape[0]
    out = jax.ShapeDtypeStruct((n, WM_Q), F32)
    return pl.pallas_call(_memkv_kernel, out_shape=(out, out), name="memory_kv",
                          compiler_params=pltpu.CompilerParams(vmem_limit_bytes=VMEM_LIMIT))(mem2d, gmem, w, gkm)


def _sort_key(s):
    s = jnp.where(s == 0.0, 0.0, s)
    bits = pltpu.bitcast(s, I32)
    return bits ^ ((bits >> 31) & INT_MAX)


KEYS_PER_GROUP = 32 * SUBLANES
SEARCH_SLAB = 64


def _bit_planes(words):
    a = list(words)
    j, m = 16, 0x0000FFFF
    while j:
        k = 0
        while k < 32:
            t = (a[k] ^ lax.shift_right_logical(a[k + j], jnp.int32(j))) & m
            a[k] = a[k] ^ t
            a[k + j] = a[k + j] ^ (t << j)
            k = (k + j + 1) & ~j
        j >>= 1
        m = (m ^ (m << j)) & 0xFFFFFFFF if j else m
    return a


ATTN_SUB = 128


def _attn_kernel(qi_ref, qa_ref, wi_ref, ki_ref, k_ref, vt_ref, out_ref,
                 keys_ref, planes_ref, cand_ref, cut_ref, bias_ref, lg_ref, cm_ref, m_ref, acc_ref, *, k_sel, idx_bits):
    bq, sc = ATTN_BQ, ATTN_SC
    wpc = sc // 32
    i = pl.program_id(0)
    n_ch = (i * bq) // sc + 1
    t_pos = i * bq + lax.broadcasted_iota(I32, (1, bq), 1)
    sub = lax.broadcasted_iota(I32, (sc, 1), 0)
    qi = qi_ref[...].reshape(N_HEADS_IDX * bq, D_IDX)
    w = wi_ref[...]

    @pl.when(i == 0)
    def _():
        planes_ref[...] = jnp.zeros(planes_ref.shape, I32)

    def chunk(c):
        return pl.ds(pl.multiple_of(c * sc, sc), sc)

    def score_chunk(c, carry):
        off = pl.multiple_of(c * sc, sc)
        d = _dot_nt(ki_ref[chunk(c), :], qi)
        s = jnp.zeros((sc, bq), F32)
        for h in range(N_HEADS_IDX):
            s = s + jnp.maximum(d[:, h * bq:(h + 1) * bq], 0.0) * w[h:h + 1, :]
        key = jnp.where(off + sub <= t_pos, _sort_key(s), INT_MIN)
        keys_ref[chunk(c), :] = key
        u = key ^ INT_MIN
        for grp in range(sc // KEYS_PER_GROUP):
            base = grp * KEYS_PER_GROUP
            words = _bit_planes([u[base + SUBLANES * v:base + SUBLANES * (v + 1), :] for v in range(32)])
            rows = pl.ds(pl.multiple_of(c * wpc + grp * SUBLANES, SUBLANES), SUBLANES)
            for p in range(32):
                planes_ref[p, rows, :] = words[p]
        return carry

    lax.fori_loop(0, n_ch // 2, lambda j, cr: score_chunk(2 * j + 1, score_chunk(2 * j, cr)), 0)
    lax.fori_loop(n_ch - n_ch % 2, n_ch, score_chunk, 0)

    def count(pred):
        def body(c, acc):
            hit = jnp.where(pred(keys_ref[chunk(c), :], c * sc + sub), 1, 0)
            return acc + hit.reshape(sc // SUBLANES, SUBLANES, bq).sum(axis=0)

        def body2(j, acc):
            return body(2 * j + 1, body(2 * j, acc))
        acc = lax.fori_loop(0, n_ch // 2, body2, jnp.zeros((SUBLANES, bq), I32))
        acc = lax.fori_loop(n_ch - n_ch % 2, n_ch, body, acc)
        return acc.sum(axis=0, keepdims=True)

    slab = SEARCH_SLAB
    n_slab = (n_ch * wpc + slab - 1) // slab
    srow = lax.broadcasted_iota(I32, (slab, 1), 0)

    def slab_rows(sl):
        return pl.ds(pl.multiple_of(sl * slab, slab), slab)

    def init_cand(sl, carry):
        cand_ref[slab_rows(sl), :] = jnp.where(sl * slab + srow < n_ch * wpc, -1, 0) + jnp.zeros((slab, bq), I32)
        return carry

    lax.fori_loop(0, n_slab, init_cand, 0)

    def sweep(prev, cur):
        def body(sl, acc):
            r = slab_rows(sl)
            cand = cand_ref[r, :]
            if prev is not None:
                ones = cand & planes_ref[prev[0], r, :]
                cand = jnp.where(prev[1] != 0, ones, cand ^ ones)
                cand_ref[r, :] = cand
            hits = cand if cur is None else cand & planes_ref[cur, r, :]
            cnt = lax.population_count(hits)
            return acc + cnt.reshape(slab // SUBLANES, SUBLANES, bq).sum(axis=0)
        acc = lax.fori_loop(0, n_slab, body, jnp.zeros((SUBLANES, bq), I32))
        return acc.sum(axis=0, keepdims=True)

    def decide(ones, p, need, thr_u):
        keep = ones >= need
        bit = jnp.int32(1) << (31 - p)
        return jnp.where(keep, 1, 0), jnp.where(keep, need, need - ones), jnp.where(keep, thr_u | bit, thr_u)

    keep, need, thr_u = decide(sweep(None, 0), 0, jnp.full((1, bq), k_sel, I32), jnp.zeros((1, bq), I32))

    def search_pass(p, st):
        keep, need, thr_u = st
        return decide(sweep((p - 1, keep), p), p, need, thr_u)

    keep, need, thr_u = lax.fori_loop(1, 32, search_pass, (keep, need, thr_u))
    tied = sweep((31, keep), None)
    thr = thr_u ^ INT_MIN
    cnt = (k_sel - need) + tied
    cut_ref[...] = jnp.full((1, bq), INT_MAX, I32)

    excess = jnp.logical_and(cnt > k_sel, thr > INT_MIN)
    has_ties = jnp.max(jnp.where(excess, 1, 0)) > 0

    @pl.when(has_ties)
    def _():
        want = k_sel - count(lambda blk, _: blk > thr)

        def tie_body(j, cpos):
            cand = cpos + (jnp.int32(1) << (idx_bits - 1 - j))
            below = count(lambda blk, pos: jnp.where(blk == thr, pos, INT_MAX) < cand)
            return jnp.where(below < want, cand, cpos)

        cpos = lax.fori_loop(0, idx_bits, tie_body, jnp.zeros((1, bq), I32))
        cut_ref[...] = jnp.where(excess, cpos, INT_MAX)

    m_ref[...] = jnp.full(m_ref.shape, NEG, F32)
    acc_ref[...] = jnp.zeros(acc_ref.shape, F32)
    cut = cut_ref[...]
    floor = jnp.where(thr == INT_MIN, INT_MIN, thr - 1)
    n_sub = sc // ATTN_SUB

    qs = [qa_ref[g * GQA:(g + 1) * GQA].reshape(GQA * bq, HEAD_DIM_A) for g in range(N_KV_A)]
    neg_row = jnp.full((1, GQA * bq), NEG, F32)

    def stage(c, par, cb, par_b):
        do_a, do_b = c is not None, cb is not None
        if do_a:
            off = pl.multiple_of(c * sc, sc)
            keyc = keys_ref[chunk(c), :]

            @pl.when(has_ties)
            def _():
                tie = jnp.where(off + sub <= cut, 0.0, NEG)
                bias = jnp.where(keyc > thr, 0.0, jnp.where(keyc == thr, tie, NEG))
                bias_ref[...] = jnp.where(keyc == INT_MIN, NEG, bias)

            @pl.when(jnp.logical_not(has_ties))
            def _():
                bias_ref[...] = jnp.where(keyc > floor, 0.0, NEG)
            cmax = [neg_row] * N_KV_A
        if do_b:
            off_b = pl.multiple_of(cb * sc, sc)
            m_new, acc = [], []
            for g in range(N_KV_A):
                m_old = m_ref[g]
                m_new.append(jnp.maximum(m_old, cm_ref[par_b, g]))
                acc.append(jnp.exp2(m_old - m_new[g]) * acc_ref[g])
                m_ref[g] = m_new[g]
        for r in range(n_sub):
            rows = slice(r * ATTN_SUB, (r + 1) * ATTN_SUB)
            if do_a:
                bias4 = jnp.concatenate([bias_ref[rows, :]] * GQA, axis=1)
                for g in range(N_KV_A):
                    lg = _dot_nt(k_ref[g, pl.ds(off + r * ATTN_SUB, ATTN_SUB), :], qs[g]) + bias4
                    lg_ref[par, g, rows, :] = lg
                    cmax[g] = jnp.maximum(cmax[g], lg.max(axis=0, keepdims=True))
            if do_b:
                for g in range(N_KV_A):
                    p = jnp.exp2(lg_ref[par_b, g, rows, :] - m_new[g])
                    vt = vt_ref[g, :, pl.ds(off_b + r * ATTN_SUB, ATTN_SUB)]
                    acc[g] = acc[g] + _dot(vt, p.astype(BF16))
        if do_b:
            for g in range(N_KV_A):
                acc_ref[g] = acc[g]
        if do_a:
            for g in range(N_KV_A):
                cm_ref[par, g] = cmax[g]

    stage(jnp.int32(0), 0, None, None)

    def stage_pair(j, carry):
        stage(2 * j + 1, 1, 2 * j, 0)
        stage(2 * j + 2, 0, 2 * j + 1, 1)
        return carry

    last = n_ch - 1
    lax.fori_loop(0, last // 2, stage_pair, 0)

    @pl.when(last % 2 == 1)
    def _():
        stage(last, 1, last - 1, 0)
        stage(None, None, last, 1)

    @pl.when(last % 2 == 0)
    def _():
        stage(None, None, last, 0)

    heads = []
    for g in range(N_KV_A):
        acc = acc_ref[g]
        o = acc[:HEAD_DIM_A] / acc[HEAD_DIM_A:HEAD_DIM_A + 1]
        heads += [o[:, hh * bq:(hh + 1) * bq] for hh in range(GQA)]
    out_ref[...] = jnp.concatenate(heads, axis=0).T.astype(out_ref.dtype)


def _prompt_attention(qi_hm, qa_hm, wit, kibf, khm, vt):
    t = kibf.shape[0]
    bq = ATTN_BQ
    k_sel = min(TOPK_MAX, t // 4)
    kern = functools.partial(_attn_kernel, k_sel=k_sel, idx_bits=max(1, (t - 1).bit_length()))
    in_specs = [pl.BlockSpec((N_HEADS_IDX, bq, D_IDX), lambda i: (0, i, 0)),
                pl.BlockSpec((N_HEADS_A, bq, HEAD_DIM_A), lambda i: (0, i, 0)),
                pl.BlockSpec((N_HEADS_IDX, bq), lambda i: (0, i)),
                _const_spec(kibf.shape, True), _const_spec(khm.shape, True), _const_spec(vt.shape, True)]
    n_words = -(-(t // 32) // SEARCH_SLAB) * SEARCH_SLAB
    scratch = [pltpu.VMEM((t, bq), I32), pltpu.VMEM((32, n_words, bq), I32), pltpu.VMEM((n_words, bq), I32),
               pltpu.VMEM((1, bq), I32), pltpu.VMEM((ATTN_SC, bq), F32),
               pltpu.VMEM((2, N_KV_A, ATTN_SC, GQA * bq), F32), pltpu.VMEM((2, N_KV_A, 1, GQA * bq), F32),
               pltpu.VMEM((N_KV_A, 1, GQA * bq), F32),
               pltpu.VMEM((N_KV_A, VT_ROWS, GQA * bq), F32)]
    return pl.pallas_call(kern, out_shape=jax.ShapeDtypeStruct((t, WA_Q), BF16), grid=(t // bq,),
                          in_specs=in_specs, out_specs=pl.BlockSpec((bq, WA_Q), lambda i: (i, 0)),
                          scratch_shapes=scratch, compiler_params=_cparams(), name="attn")(
                              qi_hm, qa_hm, wit, kibf, khm, vt)


SAMPLE_CK = 2048


def _sample_attn_kernel(pt_ref, qi_ref, qa_ref, wi_ref, kin_ref, kn_ref, vn_ref, cidx_hbm, ck_hbm, cv_hbm, out_ref,
                        idx_buf, k_buf, v_buf, scr_ref, cut_ref, sems, *, n_pages, ps, ck, k_sel, idx_bits):
    b = pl.program_id(0)
    nb = pl.num_programs(0)
    past = n_pages * ps
    nc = past // ck

    def page_copy(which, bb, p):
        src, dst = ((cidx_hbm, idx_buf), (ck_hbm, k_buf), (cv_hbm, v_buf))[which]
        cols = pl.ds(pl.multiple_of(p * ps, ps), ps)
        dst = dst.at[:, cols] if which == 0 else dst.at[:, :, cols]
        return pltpu.make_async_copy(src.at[pt_ref[bb, p]], dst, sems.at[which])

    def for_pages(fn):
        lax.fori_loop(0, n_pages, lambda p, c: (fn(p), c)[1], 0, unroll=8)

    def wait_all(which):
        buf = (idx_buf, k_buf, v_buf)[which]
        pltpu.make_async_copy(buf, buf, sems.at[which]).wait()

    @pl.when(b == 0)
    def _():
        for_pages(lambda p: page_copy(0, b, p).start())

    for_pages(lambda p: (page_copy(1, b, p).start(), page_copy(2, b, p).start()))
    wait_all(0)

    qi = qi_ref[0]
    w = wi_ref[0]
    for c in range(nc):
        d = _dot(qi, idx_buf[:, c * ck:(c + 1) * ck].astype(BF16))
        scr_ref[c:c + 1, :] = (jnp.maximum(d, 0.0) * w).sum(axis=0, keepdims=True)
    d_new = (qi.astype(F32) * kin_ref[0].astype(BF16).astype(F32)).sum(axis=-1, keepdims=True)
    s_new = (jnp.maximum(d_new, 0.0) * w).sum(axis=0, keepdims=True)

    @pl.when(b + 1 < nb)
    def _():
        for_pages(lambda p: page_copy(0, b + 1, p).start())

    keys = _sort_key(scr_ref[...])
    key_new = _sort_key(s_new)
    pos = (lax.broadcasted_iota(I32, keys.shape, 0) * ck + lax.broadcasted_iota(I32, keys.shape, 1))

    def count(pred):
        hit = jnp.where(pred(keys, pos), 1, 0).sum(axis=1, keepdims=True).sum(axis=0, keepdims=True)
        return hit + jnp.where(pred(key_new, past), 1, 0)

    def search_body(j, st):
        thr, cnt = st
        step = jnp.int32(1) << (28 - 4 * j)
        digit = jnp.zeros((1, 1), I32)
        for d in range(1, 16):
            c = count(lambda kk, _: kk >= thr + d * step)
            ok = c >= k_sel
            digit = digit + jnp.where(ok, 1, 0)
            cnt = jnp.where(ok, c, cnt)
        return thr + digit * step, cnt

    st0 = (jnp.full((1, 1), INT_MIN, I32), jnp.full((1, 1), past + 1, I32))
    thr, cnt = lax.fori_loop(0, 8, search_body, st0)
    cut_ref[...] = jnp.full((1, 1), INT_MAX, I32)

    @pl.when(jnp.max(jnp.where(cnt > k_sel, 1, 0)) > 0)
    def _():
        want = k_sel - count(lambda kk, _: kk > thr)

        def tie_body(j, cpos):
            cand = cpos + (jnp.int32(1) << (idx_bits - 1 - j))
            below = count(lambda kk, pp: jnp.where(kk == thr, pp, INT_MAX) < cand)
            return jnp.where(below < want, cand, cpos)

        cut_ref[...] = lax.fori_loop(0, idx_bits, tie_body, jnp.zeros((1, 1), I32))

    cut = cut_ref[...]

    def sel_bias(kk, pp):
        tie = jnp.where(pp <= cut, 0.0, NEG)
        return jnp.where(kk > thr, 0.0, jnp.where(kk == thr, tie, NEG))

    bias = sel_bias(keys, pos)
    bias_new = sel_bias(key_new, past)

    wait_all(1)
    wait_all(2)

    qa = qa_ref[0]
    heads = []
    for g in range(N_KV_A):
        qg = qa[g * GQA:(g + 1) * GQA]
        cols = slice(g * HEAD_DIM_A, (g + 1) * HEAD_DIM_A)
        m = jnp.full((GQA, 1), NEG, F32)
        l = jnp.zeros((GQA, 1), F32)
        acc = jnp.zeros((GQA, HEAD_DIM_A), F32)
        for c in range(nc):
            span = slice(c * ck, (c + 1) * ck)
            lg = _dot(qg, k_buf[g, :, span].astype(BF16)) + bias[c:c + 1, :]
            m_new = jnp.maximum(m, lg.max(axis=-1, keepdims=True))
            p = jnp.exp2(lg - m_new)
            alpha = jnp.exp2(m - m_new)
            l = alpha * l + p.sum(axis=-1, keepdims=True)
            acc = alpha * acc + _dot_nt(p.astype(BF16), v_buf[g, :, span].astype(BF16))
            m = m_new
        kn = kn_ref[0][:, cols].astype(BF16).astype(F32)
        vn = vn_ref[0][:, cols].astype(BF16).astype(F32)
        lg = (qg.astype(F32) * kn).sum(axis=-1, keepdims=True) + bias_new
        m_new = jnp.maximum(m, lg)
        p = jnp.exp2(lg - m_new)
        alpha = jnp.exp2(m - m_new)
        l = alpha * l + p
        acc = alpha * acc + p * vn
        o = acc / l
        heads += [o[hh:hh + 1, :] for hh in range(GQA)]
    out_ref[0] = jnp.concatenate(heads, axis=1)


def _sample_attention(page_table, qi_s, qa_s, wi_s, ki_new, k_new, v_new, cache_idx_k, cache_k, cache_v):
    nbatch, n_pages = page_table.shape
    n_pool, ps, d_idx = cache_idx_k.shape
    past = n_pages * ps
    ck = min(SAMPLE_CK, past)
    assert past % ck == 0 and ck % ps == 0
    k_sel = min(TOPK_MAX, (past + 1) // 4)
    kern = functools.partial(_sample_attn_kernel, n_pages=n_pages, ps=ps, ck=ck, k_sel=k_sel,
                             idx_bits=past.bit_length())
    per_b = lambda shape: pl.BlockSpec((1,) + shape, lambda b, pt: (b, 0, 0))
    any_spec = pl.BlockSpec(memory_space=pl.ANY)
    grid_spec = pltpu.PrefetchScalarGridSpec(
        num_scalar_prefetch=1, grid=(nbatch,),
        in_specs=[per_b((N_HEADS_IDX, d_idx)), per_b((N_HEADS_A, HEAD_DIM_A)), per_b((N_HEADS_IDX, 1)),
                  per_b((1, d_idx)), per_b((1, WA_KV)), per_b((1, WA_KV)), any_spec, any_spec, any_spec],
        out_specs=per_b((1, WA_Q)),
        scratch_shapes=[pltpu.VMEM((d_idx, past), F32), pltpu.VMEM((N_KV_A, HEAD_DIM_A, past), F32),
                        pltpu.VMEM((N_KV_A, HEAD_DIM_A, past), F32),
                        pltpu.VMEM((past // ck, ck), F32), pltpu.VMEM((1, 1), I32),
                        pltpu.SemaphoreType.DMA((3,))])
    return pl.pallas_call(kern, out_shape=jax.ShapeDtypeStruct((nbatch, 1, WA_Q), F32), grid_spec=grid_spec,
                          compiler_params=_cparams(), name="sample_attn")(
                              page_table, qi_s, qa_s, wi_s, ki_new, k_new, v_new,
                              jnp.transpose(cache_idx_k, (0, 2, 1)), jnp.transpose(cache_k, (0, 2, 3, 1)),
                              jnp.transpose(cache_v, (0, 2, 3, 1)))


def _rope_tables(pos):
    rot = HEAD_DIM_A // ROT_FRACTION
    half = rot // 2
    inv_freq = ROPE_THETA ** (-jnp.arange(half, dtype=F32) / half)
    ang = pos.astype(F32)[:, None] * inv_freq[None, :]
    cos, sin = jnp.cos(ang), jnp.sin(ang)
    n = pos.shape[0]
    z = lambda w: jnp.zeros((n, w), F32)
    c = jnp.concatenate([cos, cos, jnp.ones((n, HEAD_DIM_A - rot), F32)], axis=1)
    sa = jnp.concatenate([-sin, z(HEAD_DIM_A - half)], axis=1)
    sb = jnp.concatenate([z(half), sin, z(HEAD_DIM_A - rot)], axis=1)
    rep = LANES // HEAD_DIM_A
    return tuple(jnp.tile(a, (1, rep)) for a in (c, sa, sb))


def _prep_params(w_in, g_in, g_qa, g_ka):
    d = w_in.shape[0]
    w_qa, w_ka, w_va, w_qi, w_ki, w_wi = _split_w_in(w_in, d)[:6]
    wa = jnp.concatenate([w_qa, w_ka, w_va, w_qi, w_ki, jnp.zeros((d, LANES - D_IDX), w_in.dtype)], axis=1)
    lane = jnp.arange(LANES)
    bd = (lane[:, None] // HEAD_DIM_A == lane[None, :] // HEAD_DIM_A).astype(BF16)
    rep = LANES // HEAD_DIM_A
    return (g_in.reshape(1, d), wa.astype(BF16), w_wi.T.astype(BF16), w_va.T.astype(BF16), bd,
            jnp.tile(g_qa, rep).reshape(1, LANES), jnp.tile(g_ka, rep).reshape(1, LANES))


def _split_w_in(w_in, d):
    sizes = (WA_Q, WA_KV, WA_KV, WI_Q, D_IDX, N_HEADS_IDX, WIDTH_B, WIDTH_B, WM_Q, N_BRANCH * d)
    parts, o = [], 0
    for s in sizes:
        parts.append(w_in[:, o:o + s])
        o += s
    return parts


def _mix_kernel(*refs, chunked, shared_mem, emit_vb, n_experts):
    it = iter(refs)
    x_ref, oa_ref, gin_ref, wb_ref, gvb_ref, gqm_ref = (next(it) for _ in range(6))
    if chunked:
        wtril_ref, bt_ref = next(it), next(it)
    else:
        wdiag_ref, bdiag_ref = next(it), next(it)
    mk_ref, mv_ref = next(it), next(it)
    wbra_ref, wbrb_ref, wbrm_ref, wout_ref, gffn_ref, wrh_ref, wrl_ref, br_ref = (next(it) for _ in range(8))
    x1_ref, h2_ref, te_ref, tw_ref = (next(it) for _ in range(4))
    vb_ref = next(it) if emit_vb else None

    x = x_ref[...]
    tm, d = x.shape
    h = _rms(x, gin_ref[...]).astype(BF16)
    p = _dot(h, wb_ref[...])
    ub = p[:, :WIDTH_B]
    vb = _rms(p[:, WIDTH_B:2 * WIDTH_B], gvb_ref[...])
    if emit_vb:
        vb_ref[...] = vb
    o = 2 * WIDTH_B
    qm = p[:, o:o + WM_Q]
    o += WM_Q
    gates = p[:, o:o + N_BRANCH * d]

    if chunked:
        vbb = vb.astype(BF16)
        rows = []
        for cc in range(tm // CHUNK):
            cols = []
            for g in range(N_GROUPS_B):
                vg = vbb[cc * CHUNK:(cc + 1) * CHUNK, g * GROUP_DIM_B:(g + 1) * GROUP_DIM_B]
                cols.append(_dot(wtril_ref[g], vg))
            rows.append(jnp.concatenate(cols, axis=1) + bt_ref[...])
        z = jnp.concatenate(rows, axis=0) if len(rows) > 1 else rows[0]
    else:
        z = vb * wdiag_ref[...] + bdiag_ref[...]
    out_b = ub * z

    scale_m = HEAD_DIM_M ** -0.5
    outs = []
    for hh in range(N_HEADS_M):
        sl = slice(hh * HEAD_DIM_M, (hh + 1) * HEAD_DIM_M)
        qh = _rms(qm[:, sl], gqm_ref[...])
        if shared_mem:
            lg = _dot_nt(qh.astype(BF16), mk_ref[:, sl].astype(BF16)) * scale_m
            pm = jnp.exp(lg - lg.max(axis=-1, keepdims=True))
            pm = pm / pm.sum(axis=-1, keepdims=True)
            outs.append(_dot(pm.astype(BF16), mv_ref[:, sl].astype(BF16)))
        else:
            per_row = []
            for r in range(tm):
                kr = mk_ref[r, :, sl]
                lg = (kr * qh[r:r + 1, :]).sum(axis=-1, keepdims=True) * scale_m
                pm = jnp.exp(lg - lg.max(axis=0, keepdims=True))
                pm = pm / pm.sum(axis=0, keepdims=True)
                per_row.append((pm * mv_ref[r, :, sl]).sum(axis=0, keepdims=True))
            outs.append(jnp.concatenate(per_row, axis=0))
    out_m = jnp.concatenate(outs, axis=1)

    sig = jax.nn.sigmoid
    merged = (sig(gates[:, :d]) * _dot(oa_ref[...].astype(BF16), wbra_ref[...])
              + sig(gates[:, d:2 * d]) * _dot(out_b.astype(BF16), wbrb_ref[...])
              + sig(gates[:, 2 * d:]) * _dot(out_m.astype(BF16), wbrm_ref[...]))
    x1 = x + _dot(merged.astype(BF16), wout_ref[...])
    x1_ref[...] = x1
    h2 = _rms(x1, gffn_ref[...])
    h2_ref[...] = h2.astype(h2_ref.dtype)

    hi = h2.astype(BF16)
    lo = (h2 - hi.astype(F32)).astype(BF16)
    lg = _dot(hi, wrh_ref[...]) + _dot(hi, wrl_ref[...]) + _dot(lo, wrh_ref[...]) + br_ref[...]
    lane = lax.broadcasted_iota(I32, lg.shape, 1)
    lg = jnp.where(lane < n_experts, lg, -jnp.inf)
    vals, idxs = [], []
    for _ in range(TOP_K_MOE):
        m = lg.max(axis=-1, keepdims=True)
        idx = jnp.where(lg == m, lane, LANES).min(axis=-1, keepdims=True)
        vals.append(m)
        idxs.append(idx)
        lg = jnp.where(lane == idx, -jnp.inf, lg)
    ex = [jnp.exp(v - vals[0]) for v in vals]
    den = ex[0]
    for e in ex[1:]:
        den = den + e
    te = jnp.full(lane.shape, -1, I32)
    tw = jnp.zeros(lane.shape, F32)
    for j in range(TOP_K_MOE):
        te = jnp.where(lane == j, idxs[j], te)
        tw = jnp.where(lane == j, ex[j] / den, tw)
    te_ref[...] = te
    tw_ref[...] = tw


def _mix(x2d, out_a, mix_params, gmlp_params, mem_k, mem_v, *, tm, chunked, shared_mem, emit_vb, n_experts):
    n, d = x2d.shape
    gin, wb, gvb, gqm, wbra, wbrb, wbrm, wout, gffn, wrh, wrl, br = mix_params
    row = lambda w: pl.BlockSpec((tm, w), lambda i: (i, 0))
    if shared_mem:
        mem_specs = [_const_spec(mem_k.shape), _const_spec(mem_v.shape)]
    else:
        mspec = pl.BlockSpec((tm,) + mem_k.shape[1:], lambda i: (i, 0, 0))
        mem_specs = [mspec, mspec]
    consts = lambda arrs: [_const_spec(a.shape) for a in arrs]
    in_specs = ([row(d), row(WA_Q)] + consts([gin, wb, gvb, gqm]) + consts(gmlp_params) + mem_specs
                + consts([wbra, wbrb, wbrm, wout, gffn, wrh, wrl, br]))
    out_shape = [jax.ShapeDtypeStruct((n, d), F32), jax.ShapeDtypeStruct((n, d), F32),
                 jax.ShapeDtypeStruct((n, LANES), I32), jax.ShapeDtypeStruct((n, LANES), F32)]
    out_specs = [row(d), row(d), row(LANES), row(LANES)]
    if emit_vb:
        out_shape.append(jax.ShapeDtypeStruct((n, WIDTH_B), F32))
        out_specs.append(row(WIDTH_B))
    kern = functools.partial(_mix_kernel, chunked=chunked, shared_mem=shared_mem, emit_vb=emit_vb,
                             n_experts=n_experts)
    return pl.pallas_call(kern, out_shape=tuple(out_shape), grid=(n // tm,), in_specs=in_specs,
                          out_specs=tuple(out_specs), compiler_params=_cparams(), name="mix")(
                              x2d, out_a, gin, wb, gvb, gqm, *gmlp_params, mem_k, mem_v,
                              wbra, wbrb, wbrm, wout, gffn, wrh, wrl, br)


def _prep_mix_params(w_in, g_in, g_vb, g_qm, w_br_a, w_br_b, w_br_m, w_out, g_ffn, w_router, b_router):
    d = w_in.shape[0]
    parts = _split_w_in(w_in, d)
    wb = jnp.concatenate(parts[6:10], axis=1).astype(BF16)
    n_e = w_router.shape[1]
    wr = jnp.pad(w_router, ((0, 0), (0, LANES - n_e)))
    wrh = wr.astype(BF16)
    wrl = (wr - wrh.astype(F32)).astype(BF16)
    br = jnp.pad(b_router, (0, LANES - n_e)).reshape(1, LANES)
    return (g_in.reshape(1, d), wb, g_vb.reshape(1, WIDTH_B), g_qm.reshape(1, HEAD_DIM_M),
            w_br_a.astype(BF16), w_br_b.astype(BF16), w_br_m.astype(BF16), w_out.astype(BF16),
            g_ffn.reshape(1, d), wrh, wrl, br)


def _prep_gmlp_chunked(w_s, b_s):
    tril = jnp.tril(jnp.ones((CHUNK, CHUNK), w_s.dtype))
    wtril = (w_s * tril).astype(BF16)
    bt = jnp.repeat(b_s.T, GROUP_DIM_B, axis=1)
    return wtril, bt


def _prep_gmlp_single(w_s, b_s):
    wdiag = jnp.repeat(w_s[:, 0, 0], GROUP_DIM_B).reshape(1, WIDTH_B)
    bdiag = jnp.repeat(b_s[:, 0], GROUP_DIM_B).reshape(1, WIDTH_B)
    return wdiag, bdiag


MOE_TP = 256


def _plan_kernel(te_ref, tri_ref, upper_ref, lslot_ref, cnt_ref, tstart_ref, gprev_ref, total_ref, carry_ref):
    @pl.when(pl.program_id(0) == 0)
    def _():
        carry_ref[...] = jnp.zeros(carry_ref.shape, F32)

    te = te_ref[...]
    lane = lax.broadcasted_iota(I32, te.shape, 1)
    picks = [te[:, j:j + 1] for j in range(TOP_K_MOE)]
    onehot = jnp.zeros(te.shape, F32)
    for e in picks:
        onehot = onehot + jnp.where(lane == e, 1.0, 0.0)
    before = _dot(tri_ref[...], onehot.astype(BF16))
    cnt = onehot.sum(axis=0, keepdims=True)
    cnt = jnp.floor((cnt + (SUBLANES - 1)) * (1.0 / SUBLANES)) * SUBLANES
    cnt8 = jnp.broadcast_to(cnt, (SUBLANES, LANES)).astype(BF16)
    tstart = _dot(cnt8, upper_ref[...])[0:1, :]
    lslot = jnp.zeros(te.shape, I32)
    for j, e in enumerate(picks):
        r = jnp.where(lane == e, before + tstart, 0.0).sum(axis=-1, keepdims=True).astype(I32)
        lslot = jnp.where(lane == j, r, lslot)
    lslot_ref[...] = lslot
    cnt_ref[0] = cnt.astype(I32)
    tstart_ref[0] = tstart.astype(I32)
    gprev_ref[0] = carry_ref[...].astype(I32)
    carry_ref[...] = carry_ref[...] + cnt
    total_ref[...] = carry_ref[...].astype(I32)


def _moe_plan(te_all):
    n = te_all.shape[0]
    tp = MOE_TP
    n_tiles = n // tp
    r = jnp.arange(tp)
    tri = (r[:, None] > r[None, :]).astype(BF16)
    e = jnp.arange(LANES)
    upper = (e[:, None] < e[None, :]).astype(BF16)
    per_tile = jax.ShapeDtypeStruct((n_tiles, 1, LANES), I32)
    tile_spec = pl.BlockSpec((1, 1, LANES), lambda i: (i, 0, 0))
    return pl.pallas_call(
        _plan_kernel,
        out_shape=(jax.ShapeDtypeStruct((n, LANES), I32), per_tile, per_tile, per_tile,
                   jax.ShapeDtypeStruct((1, LANES), I32)),
        grid=(n_tiles,),
        in_specs=[pl.BlockSpec((tp, LANES), lambda i: (i, 0)), _const_spec((tp, tp)), _const_spec((LANES, LANES))],
        out_specs=(pl.BlockSpec((tp, LANES), lambda i: (i, 0)), tile_spec, tile_spec, tile_spec,
                   _const_spec((1, LANES))),
        scratch_shapes=[pltpu.VMEM((1, LANES), F32)],
        compiler_params=_cparams(), name="moe_plan")(te_all, tri, upper)


def _run_pieces(tp):
    sizes, s = [], SUBLANES
    while s <= tp:
        sizes.append(s)
        s *= 2
    return tuple(reversed(sizes))


def _local_rows(tp, n_e):
    return tp * TOP_K_MOE + n_e * SUBLANES


def _for_runs(tabs, tile, n_e, tp, fn):
    cnt_s, tstart_s, gbase_s = tabs

    def per_expert(e, c):
        idx = tile * n_e + e
        length, src0, dst0 = cnt_s[idx], tstart_s[idx], gbase_s[idx]
        off = jnp.int32(0)
        for size in _run_pieces(tp):
            piece = length & size

            @pl.when(piece != 0)
            def _():
                fn(pl.multiple_of(src0 + off, SUBLANES), pl.multiple_of(dst0 + off, SUBLANES), size)
            off = off + piece
        return c

    lax.fori_loop(0, n_e, per_expert, 0)


def _slot_matrix(lslot, vals, n_rows):
    col = lax.broadcasted_iota(I32, (lslot.shape[0], n_rows), 1)
    m = jnp.zeros(col.shape, F32)
    for j in range(TOP_K_MOE):
        v = 1.0 if vals is None else vals[:, j:j + 1]
        m = m + jnp.where(lslot[:, j:j + 1] == col, v, 0.0)
    return m


def _dispatch_kernel(cnt_s, tstart_s, gbase_s, fstart_s, flen_s, nused_s, lslot_ref, h2_ref, *rest,
                     tp, n_e, tile0, first, bm):
    if first:
        xs_out, stage, sem, zbuf, zsem = rest
    else:
        _, xs_out, stage, sem = rest
    n_rows = _local_rows(tp, n_e)
    i = pl.program_id(0)
    slot = i % 2
    pt = _slot_matrix(lslot_ref[...], None, n_rows).astype(BF16)
    stage[slot] = lax.dot_general(pt, h2_ref[...].astype(BF16), (((0,), (0,)), ((), ())),
                                  preferred_element_type=F32)

    def runs(tile, s, op):
        _for_runs((cnt_s, tstart_s, gbase_s), tile0 + tile, n_e, tp,
                  lambda src, dst, size: op(pltpu.make_async_copy(
                      stage.at[s, pl.ds(src, size), :], xs_out.at[pl.ds(dst, size), :], sem.at[s])))

    @pl.when(i > 0)
    def _():
        runs(i - 1, 1 - slot, lambda cp: cp.wait())

    runs(i, slot, lambda cp: cp.start())

    @pl.when(i == pl.num_programs(0) - 1)
    def _():
        runs(i, slot, lambda cp: cp.wait())
        if first:
            zbuf[...] = jnp.zeros(zbuf.shape, F32)
            nb = xs_out.shape[0] // bm

            def fill(op):
                def per_expert(e, c):
                    length, dst0 = flen_s[e], fstart_s[e]
                    off = jnp.int32(0)
                    for size in _run_pieces(bm):
                        piece = length & size

                        @pl.when(piece != 0)
                        def _():
                            op(pltpu.make_async_copy(
                                zbuf.at[pl.ds(0, size), :],
                                xs_out.at[pl.ds(pl.multiple_of(dst0 + off, SUBLANES), size), :], zsem))
                        off = off + piece
                    return c

                lax.fori_loop(0, n_e, per_expert, 0)
                lax.fori_loop(nused_s[0], nb, lambda b, c: (op(pltpu.make_async_copy(
                    zbuf, xs_out.at[pl.ds(pl.multiple_of(b * bm, bm), bm), :], zsem)), c)[1], 0)

            fill(lambda cp: cp.start())
            fill(lambda cp: cp.wait())


def _dispatch(tabs, fill_tabs, tile0, lslot, h2, xs, n_e, ns, bm):
    n, d = h2.shape
    tp = min(MOE_TP, n)
    n_rows = _local_rows(tp, n_e)
    first = xs is None
    kern = functools.partial(_dispatch_kernel, tp=tp, n_e=n_e, tile0=tile0, first=first, bm=bm)
    any_spec = pl.BlockSpec(memory_space=pl.ANY)
    scratch = [pltpu.VMEM((2, n_rows, d), F32), pltpu.SemaphoreType.DMA((2,))]
    if first:
        scratch += [pltpu.VMEM((bm, d), F32), pltpu.SemaphoreType.DMA]
    grid_spec = pltpu.PrefetchScalarGridSpec(
        num_scalar_prefetch=6, grid=(n // tp,),
        in_specs=[pl.BlockSpec((tp, LANES), lambda i, *_: (i, 0)), pl.BlockSpec((tp, d), lambda i, *_: (i, 0))]
        + ([] if first else [any_spec]),
        out_specs=any_spec, scratch_shapes=scratch)
    args = (*tabs, *fill_tabs, lslot, h2) + (() if first else (xs,))
    return pl.pallas_call(
        kern, out_shape=jax.ShapeDtypeStruct((ns, d), F32), grid_spec=grid_spec,
        input_output_aliases={} if first else {8: 0},
        compiler_params=pltpu.CompilerParams(dimension_semantics=("arbitrary",), has_side_effects=True,
                                             vmem_limit_bytes=VMEM_LIMIT),
        name="dispatch")(*args)


def _expert_kernel(be_ref, nu_ref, xs_ref, wg_ref, wu_ref, wd_ref, bg_ref, bu_ref, bd_ref, y_ref,
                   wg_bf, wu_bf, wd_bf):
    b = pl.program_id(0)

    @pl.when(b >= nu_ref[0])
    def _():
        y_ref[...] = jnp.zeros(y_ref.shape, F32)

    @pl.when(b < nu_ref[0])
    def _():
        prev = be_ref[jnp.maximum(b - 1, 0)]

        @pl.when(jnp.logical_or(b == 0, be_ref[b] != prev))
        def _():
            wg_bf[...] = wg_ref[0].astype(BF16)
            wu_bf[...] = wu_ref[0].astype(BF16)
            wd_bf[...] = wd_ref[0].astype(BF16)

        xb = xs_ref[...].astype(BF16)
        ff = wg_bf.shape[1]
        y = jnp.zeros(y_ref.shape, F32) + bd_ref[0]
        for n in range(ff // FF_CHUNK):
            sl = slice(n * FF_CHUNK, (n + 1) * FF_CHUNK)
            hg = jnp.minimum(_dot(xb, wg_bf[:, sl]) + bg_ref[0][:, sl], SWIGLU_LIMIT)
            hu = jnp.clip(_dot(xb, wu_bf[:, sl]) + bu_ref[0][:, sl], -SWIGLU_LIMIT, SWIGLU_LIMIT)
            act = hg * jax.nn.sigmoid(SWIGLU_ALPHA * hg) * (hu + 1.0)
            y = y + _dot(act.astype(BF16), wd_bf[sl, :])
        y_ref[...] = y


def _experts(xs, block_e, n_used, w_gate, b_gate, w_up, b_up, w_down, b_down):
    ns, d = xs.shape
    n_e, _, ff = w_gate.shape
    bm = MOE_BM
    blk = lambda b, be, nu: (jnp.minimum(b, nu[0] - 1), 0)
    wsel = lambda b, be, nu: (be[b], 0, 0)
    grid_spec = pltpu.PrefetchScalarGridSpec(
        num_scalar_prefetch=2, grid=(ns // bm,),
        in_specs=[pl.BlockSpec((bm, d), blk),
                  pl.BlockSpec((1, d, ff), wsel), pl.BlockSpec((1, d, ff), wsel), pl.BlockSpec((1, ff, d), wsel),
                  pl.BlockSpec((1, 1, ff), wsel), pl.BlockSpec((1, 1, ff), wsel), pl.BlockSpec((1, 1, d), wsel)],
        out_specs=pl.BlockSpec((bm, d), lambda b, be, nu: (b, 0)),
        scratch_shapes=[pltpu.VMEM((d, ff), BF16), pltpu.VMEM((d, ff), BF16), pltpu.VMEM((ff, d), BF16)])
    return pl.pallas_call(_expert_kernel, out_shape=jax.ShapeDtypeStruct((ns, d), F32), grid_spec=grid_spec,
                          compiler_params=_cparams(), name="experts")(
                              block_e, n_used, xs, w_gate, w_up, w_down,
                              b_gate.reshape(n_e, 1, ff), b_up.reshape(n_e, 1, ff), b_down.reshape(n_e, 1, d))


def _combine_kernel(cnt_s, tstart_s, gbase_s, lslot_ref, tw_ref, x1_ref, yb_hbm, y_ref, buf, sem, *, tp, n_e, tile0):
    n_rows = _local_rows(tp, n_e)
    i = pl.program_id(0)
    slot = i % 2

    def runs(tile, s, op):
        _for_runs((cnt_s, tstart_s, gbase_s), tile0 + tile, n_e, tp,
                  lambda loc, glob, size: op(pltpu.make_async_copy(
                      yb_hbm.at[pl.ds(glob, size), :], buf.at[s, pl.ds(loc, size), :], sem.at[s])))

    @pl.when(i == 0)
    def _():
        buf[...] = jnp.zeros(buf.shape, F32)
        runs(i, slot, lambda cp: cp.start())

    @pl.when(i + 1 < pl.num_programs(0))
    def _():
        runs(i + 1, 1 - slot, lambda cp: cp.start())

    runs(i, slot, lambda cp: cp.wait())

    ptw = _slot_matrix(lslot_ref[...], tw_ref[...], n_rows)
    rows = buf[slot]
    w_hi = ptw.astype(BF16)
    w_lo = (ptw - w_hi.astype(F32)).astype(BF16)
    r_hi = rows.astype(BF16)
    r_lo = (rows - r_hi.astype(F32)).astype(BF16)
    y_ref[...] = x1_ref[...] + _dot(w_hi, r_hi) + (_dot(w_hi, r_lo) + _dot(w_lo, r_hi))


def _combine(tabs, tile0, lslot, tw, x1, yb, n_e):
    n, d = x1.shape
    tp = min(MOE_TP, n)
    n_rows = _local_rows(tp, n_e)
    kern = functools.partial(_combine_kernel, tp=tp, n_e=n_e, tile0=tile0)
    row = lambda w: pl.BlockSpec((tp, w), lambda i, *_: (i, 0))
    grid_spec = pltpu.PrefetchScalarGridSpec(
        num_scalar_prefetch=3, grid=(n // tp,),
        in_specs=[row(LANES), row(LANES), row(d), pl.BlockSpec(memory_space=pl.ANY)],
        out_specs=row(d),
        scratch_shapes=[pltpu.VMEM((2, n_rows, d), F32), pltpu.SemaphoreType.DMA((2,))])
    return pl.pallas_call(kern, out_shape=jax.ShapeDtypeStruct(x1.shape, F32), grid_spec=grid_spec,
                          compiler_params=_cparams(), name="combine")(*tabs, lslot, tw, x1, yb)


def _moe(groups, w_gate, b_gate, w_up, b_up, w_down, b_down):
    n_e = w_gate.shape[0]
    d = groups[0][0].shape[1]
    tp, bm = MOE_TP, MOE_BM
    sizes = [g[0].shape[0] for g in groups]
    assert all(n % tp == 0 or n < tp for n in sizes), sizes
    parts, tile0s, o = [], [], 0
    for g, n in zip(groups, sizes):
        n_pad = -(-n // tp) * tp
        parts.append(jnp.pad(g[2], ((0, n_pad - n), (0, 0)), constant_values=-1))
        tile0s.append(o // tp)
        o += n_pad
    lslot, cnt3, tstart3, gprev3, total = _moe_plan(jnp.concatenate(parts, axis=0))
    counts = total[0, :n_e]
    padded_cnt = (counts + bm - 1) // bm * bm
    pad_end = jnp.cumsum(padded_cnt)
    pad_start = pad_end - padded_cnt
    n_assign = sum(sizes) * TOP_K_MOE
    n_slack = (o // tp) * n_e * (SUBLANES - 1)
    nb = -(-(n_assign + n_slack) // bm) + n_e
    n_used = (pad_end[-1] // bm).astype(I32)
    blocks = jnp.arange(nb, dtype=I32)
    block_e = jnp.minimum((pad_end[None, :] <= (blocks * bm)[:, None]).sum(axis=1), n_e - 1).astype(I32)
    block_e = jnp.where(blocks < n_used, block_e, block_e[jnp.maximum(n_used - 1, 0)])
    flat = lambda a: a[:, 0, :n_e].reshape(-1).astype(I32)
    gbase3 = pad_start[None, None, :] + gprev3[:, :, :n_e]
    tabs = (flat(cnt3), flat(tstart3), flat(gbase3))

    after_first = gprev3[tile0s[1], 0, :n_e] if len(groups) > 1 else counts
    fill_start = (pad_start + after_first).astype(I32)
    fill_tabs = (fill_start, (pad_end - fill_start).astype(I32), n_used.reshape(1))
    xs, lslots = None, []
    for (x1, h2, te, tw), n, t0 in zip(groups, sizes, tile0s):
        ls = lslot[t0 * tp:t0 * tp + n]
        lslots.append(ls)
        xs = _dispatch(tabs, fill_tabs, t0, ls, h2, xs, n_e, nb * bm, bm)
    yb = _experts(xs, block_e, n_used.reshape(1), w_gate, b_gate, w_up, b_up, w_down, b_down)
    return [_combine(tabs, t0, ls, tw, x1, yb, n_e)
            for (x1, h2, te, tw), ls, t0 in zip(groups, lslots, tile0s)]


def _row_tile(n, want):
    return want if n % want == 0 else n


def kernel(x_prompt, x_sample, mem_prompt, cache_k, cache_v, cache_idx_k, cache_mem_k, cache_mem_v, page_table,
           g_in, w_in, g_qa, g_ka, g_vb, w_s, b_s, g_qm, g_mem, w_mem_kv, g_km, w_br_a, w_br_b, w_br_m, w_out,
           g_ffn, w_router, b_router, w_gate, b_gate, w_up, b_up, w_down, b_down):
    bp, t, d = x_prompt.shape
    bs, ts, _ = x_sample.shape
    assert bp == 1 and ts == 1 and t % CHUNK == 0 and cache_k.shape[1] == LANES
    n_mem = mem_prompt.shape[1]
    n_pages = page_table.shape[1]
    ps = cache_k.shape[1]
    past = n_pages * ps
    n_e = w_router.shape[1]

    fp = _prep_params(w_in, g_in, g_qa, g_ka)
    mp = _prep_mix_params(w_in, g_in, g_vb, g_qm, w_br_a, w_br_b, w_br_m, w_out, g_ffn, w_router, b_router)

    xp = x_prompt.reshape(t, d)
    qa_p, k32_p, khm_p, v32_p, vt_p, qi_p, ki32_p, kibf_p, wit_p = _front(
        xp, _rope_tables(jnp.arange(t, dtype=I32)), *fp, tm=_row_tile(t, 512), paged=True)
    mem_k, mem_v = _memory_kv(mem_prompt.reshape(n_mem, d), g_mem.reshape(1, d), w_mem_kv.astype(BF16),
                              g_km.reshape(1, HEAD_DIM_M))
    out_a_p = _prompt_attention(qi_p, qa_p, wit_p, kibf_p, khm_p, vt_p)
    x1_p, h2_p, te_p, tw_p = _mix(xp, out_a_p, mp, _prep_gmlp_chunked(w_s, b_s), mem_k, mem_v,
                                  tm=_row_tile(t, 256), chunked=True, shared_mem=True, emit_vb=False, n_experts=n_e)

    xs = x_sample.reshape(bs, d)
    qa_s, k32_s, _, v32_s, _, qi_s, ki32_s, _, wit_s = _front(
        xs, _rope_tables(jnp.full((bs,), past, I32)), *fp, tm=bs, paged=False)
    out_a_s = _sample_attention(page_table, jnp.moveaxis(qi_s, 0, 1), jnp.moveaxis(qa_s, 0, 1),
                                wit_s.T.reshape(bs, N_HEADS_IDX, 1), ki32_s.reshape(bs, 1, D_IDX),
                                k32_s.reshape(bs, 1, WA_KV), v32_s.reshape(bs, 1, WA_KV),
                                cache_idx_k, cache_k, cache_v)
    x1_s, h2_s, te_s, tw_s, vb_s = _mix(xs, out_a_s.reshape(bs, WA_Q), mp, _prep_gmlp_single(w_s, b_s),
                                        cache_mem_k.reshape(bs, n_mem, WM_Q), cache_mem_v.reshape(bs, n_mem, WM_Q),
                                        tm=_row_tile(bs, SUBLANES), chunked=False, shared_mem=False, emit_vb=True,
                                        n_experts=n_e)

    y_p, y_s = _moe([(x1_p, h2_p, te_p, tw_p), (x1_s, h2_s, te_s, tw_s)],
                    w_gate, b_gate, w_up, b_up, w_down, b_down)

    n_pg = t // ps
    paged_kv = lambda a: jnp.transpose(a.reshape(1, n_pg, N_KV_A, HEAD_DIM_A, ps), (0, 1, 4, 2, 3))
    return (y_p.reshape(1, t, d), y_s.reshape(bs, 1, d),
            paged_kv(k32_p), paged_kv(v32_p), jnp.transpose(ki32_p, (0, 2, 1)).reshape(1, n_pg, ps, D_IDX),
            mem_k.reshape(1, n_mem, N_HEADS_M, HEAD_DIM_M), mem_v.reshape(1, n_mem, N_HEADS_M, HEAD_DIM_M),
            k32_s.reshape(bs, 1, N_KV_A, HEAD_DIM_A), v32_s.reshape(bs, 1, N_KV_A, HEAD_DIM_A),
            ki32_s.reshape(bs, 1, D_IDX), vb_s.reshape(bs, 1, WIDTH_B))
```

```python
import functools

import jax
import jax.numpy as jnp
from jax import lax
from jax.experimental import pallas as pl
from jax.experimental.pallas import tpu as pltpu

F32 = jnp.float32
BF16 = jnp.bfloat16
I32 = jnp.int32

N_HEADS_A = 8
N_KV_A = 2
HEAD_DIM_A = 64
TOPK_MAX = 256
N_HEADS_IDX = 8
D_IDX = 64
N_GROUPS_B = 4
GROUP_DIM_B = 128
WIDTH_B = N_GROUPS_B * GROUP_DIM_B
CHUNK = 128
N_HEADS_M = 4
HEAD_DIM_M = 128
ROPE_THETA = 500000.0
ROT_FRACTION = 4
N_BRANCH = 3
TOP_K_MOE = 4
SWIGLU_LIMIT = 7.0
SWIGLU_ALPHA = 1.702
EPS = 1e-6

WA_Q = N_HEADS_A * HEAD_DIM_A
WA_KV = N_KV_A * HEAD_DIM_A
WI_Q = N_HEADS_IDX * D_IDX
WM_Q = N_HEADS_M * HEAD_DIM_M
GQA = N_HEADS_A // N_KV_A

LANES = 128
SUBLANES = 8
VMEM_LIMIT = 56 * 1024 * 1024

LOG2_E = 1.4426950408889634
INT_MIN = -(2 ** 31)
INT_MAX = 2 ** 31 - 1
NEG = -1e30

VT_ROWS = HEAD_DIM_A + 16
ATTN_BQ = 128
ATTN_SC = 512
MOE_BM = 512
FF_CHUNK = 512


def _cparams(n_axes=1, vmem=VMEM_LIMIT):
    return pltpu.CompilerParams(dimension_semantics=("arbitrary",) * n_axes, vmem_limit_bytes=vmem)


def _const_spec(shape, single=False):
    zeros = (0,) * len(shape)
    if single:
        return pl.BlockSpec(shape, lambda *_: zeros, pipeline_mode=pl.Buffered(1))
    return pl.BlockSpec(shape, lambda *_: zeros)


def _dot(a, b):
    return jnp.dot(a, b, preferred_element_type=F32)


def _dot_nt(a, b):
    return lax.dot_general(a, b, (((1,), (1,)), ((), ())), preferred_element_type=F32)


def _rms(x, g):
    return x * lax.rsqrt(jnp.mean(x * x, axis=-1, keepdims=True) + EPS) * g


def _split_dot(x, w):
    hi = x.astype(BF16)
    lo = (x - hi.astype(F32)).astype(BF16)
    return _dot(hi, w) + _dot(lo, w)


def _front_kernel(x_ref, gin_ref, wa_ref, wwit_ref, wvt_ref, bd_ref, gqa_ref, gka_ref, cos_ref, sa_ref, sb_ref,
                  qa_ref, k32_ref, khm_ref, v32_ref, vt_ref, qi_ref, ki32_ref, kibf_ref, wit_ref, *, score_scale, paged):
    x = x_ref[...]
    h = _rms(x, gin_ref[...]).astype(BF16)
    p = _dot(h, wa_ref[...])
    cos, sa, sb = cos_ref[...], sa_ref[...], sb_ref[...]
    bd = bd_ref[...]

    def head_norm(v, g):
        ssq = _split_dot(v * v, bd)
        return v * lax.rsqrt(ssq * (1.0 / HEAD_DIM_A) + EPS) * g

    def rope(v):
        return v * cos + pltpu.roll(v, LANES - 8, 1) * sa + pltpu.roll(v, 8, 1) * sb

    for s in range(WA_Q // LANES):
        v = p[:, s * LANES:(s + 1) * LANES]
        v = rope(head_norm(v, gqa_ref[...])) * (HEAD_DIM_A ** -0.5 * LOG2_E)
        vb = v.astype(BF16)
        qa_ref[2 * s] = vb[:, :HEAD_DIM_A]
        qa_ref[2 * s + 1] = vb[:, HEAD_DIM_A:]
    o = WA_Q
    n_pg = x.shape[0] // LANES
    k = rope(head_norm(p[:, o:o + WA_KV], gka_ref[...]))
    if paged:
        for j in range(n_pg):
            k32_ref[j] = k[j * LANES:(j + 1) * LANES, :].T
    else:
        k32_ref[...] = k
    kb = k.astype(BF16)
    khm_ref[0] = kb[:, :HEAD_DIM_A]
    khm_ref[1] = kb[:, HEAD_DIM_A:]
    o += WA_KV
    if not paged:
        v32_ref[...] = p[:, o:o + WA_KV]
    o += WA_KV
    for s in range(WI_Q // LANES):
        vb = rope(p[:, o + s * LANES:o + (s + 1) * LANES]).astype(BF16)
        qi_ref[2 * s] = vb[:, :D_IDX]
        qi_ref[2 * s + 1] = vb[:, D_IDX:]
    o += WI_Q
    ki_slab = rope(p[:, o:o + LANES])
    ki = ki_slab[:, :D_IDX]
    if paged:
        for j in range(n_pg):
            ki32_ref[j] = ki_slab[j * LANES:(j + 1) * LANES, :].T[:D_IDX, :]
    else:
        ki32_ref[...] = ki
    kibf_ref[...] = ki.astype(BF16)
    wit_ref[...] = _dot_nt(wwit_ref[...], h) * score_scale
    vt = _dot_nt(wvt_ref[...], h)
    if paged:
        for j in range(n_pg):
            v32_ref[j] = vt[:, j * LANES:(j + 1) * LANES]
    tail = jnp.where(lax.broadcasted_iota(I32, (VT_ROWS - HEAD_DIM_A, vt.shape[1]), 0) == 0, 1.0, 0.0)
    for g in range(N_KV_A):
        vt_ref[g] = jnp.concatenate([vt[g * HEAD_DIM_A:(g + 1) * HEAD_DIM_A], tail], axis=0).astype(BF16)


def _front(x2d, tabs, gin, wa, wwit, wvt, bd, gqa, gka, tm, paged):
    n, d = x2d.shape
    cos, sa, sb = tabs
    row = lambda w: pl.BlockSpec((tm, w), lambda i: (i, 0))
    hm = lambda nh, w: pl.BlockSpec((nh, tm, w), lambda i: (0, i, 0))
    if paged:
        assert tm % LANES == 0
        f32_out = lambda w: jax.ShapeDtypeStruct((n // LANES, w, LANES), F32)
        f32_spec = lambda w: pl.BlockSpec((tm // LANES, w, LANES), lambda i: (i, 0, 0))
    else:
        f32_out = lambda w: jax.ShapeDtypeStruct((n, w), F32)
        f32_spec = row
    out_shape = (
        jax.ShapeDtypeStruct((N_HEADS_A, n, HEAD_DIM_A), BF16),
        f32_out(WA_KV),
        jax.ShapeDtypeStruct((N_KV_A, n, HEAD_DIM_A), BF16),
        f32_out(WA_KV),
        jax.ShapeDtypeStruct((N_KV_A, VT_ROWS, n), BF16),
        jax.ShapeDtypeStruct((N_HEADS_IDX, n, D_IDX), BF16),
        f32_out(D_IDX),
        jax.ShapeDtypeStruct((n, D_IDX), BF16),
        jax.ShapeDtypeStruct((N_HEADS_IDX, n), F32),
    )
    out_specs = (hm(N_HEADS_A, HEAD_DIM_A), f32_spec(WA_KV), hm(N_KV_A, HEAD_DIM_A), f32_spec(WA_KV),
                 pl.BlockSpec((N_KV_A, VT_ROWS, tm), lambda i: (0, 0, i)), hm(N_HEADS_IDX, D_IDX), f32_spec(D_IDX),
                 row(D_IDX), pl.BlockSpec((N_HEADS_IDX, tm), lambda i: (0, i)))
    in_specs = [row(d), _const_spec(gin.shape), _const_spec(wa.shape), _const_spec(wwit.shape),
                _const_spec(wvt.shape), _const_spec(bd.shape), _const_spec(gqa.shape), _const_spec(gka.shape),
                row(LANES), row(LANES), row(LANES)]
    kern = functools.partial(_front_kernel, score_scale=D_IDX ** -0.5 * N_HEADS_IDX ** -0.5, paged=paged)
    return pl.pallas_call(kern, out_shape=out_shape, grid=(n // tm,), in_specs=in_specs, out_specs=out_specs,
                          compiler_params=_cparams(), name="front")(x2d, gin, wa, wwit, wvt, bd, gqa, gka, cos, sa, sb)


def _memkv_kernel(mem_ref, gmem_ref, w_ref, gkm_ref, k32_ref, v32_ref):
    h = _rms(mem_ref[...], gmem_ref[...]).astype(BF16)
    kv = _dot(h, w_ref[...])
    for hh in range(N_HEADS_M):
        sl = slice(hh * HEAD_DIM_M, (hh + 1) * HEAD_DIM_M)
        k32_ref[:, sl] = _rms(kv[:, sl], gkm_ref[...])
    v32_ref[...] = kv[:, WM_Q:]


def _memory_kv(mem2d, gmem, w, gkm):
    n = mem2d.shape[0]
    out = jax.ShapeDtypeStruct((n, WM_Q), F32)
    return pl.pallas_call(_memkv_kernel, out_shape=(out, out), name="memory_kv",
                          compiler_params=pltpu.CompilerParams(vmem_limit_bytes=VMEM_LIMIT))(mem2d, gmem, w, gkm)


def _sort_key(s):
    s = jnp.where(s == 0.0, 0.0, s)
    bits = pltpu.bitcast(s, I32)
    return bits ^ ((bits >> 31) & INT_MAX)


KEYS_PER_GROUP = 32 * SUBLANES
SEARCH_SLAB = 64


def _bit_planes(words):
    a = list(words)
    j, m = 16, 0x0000FFFF
    while j:
        k = 0
        while k < 32:
            t = (a[k] ^ lax.shift_right_logical(a[k + j], jnp.int32(j))) & m
            a[k] = a[k] ^ t
            a[k + j] = a[k + j] ^ (t << j)
            k = (k + j + 1) & ~j
        j >>= 1
        m = (m ^ (m << j)) & 0xFFFFFFFF if j else m
    return a


ATTN_SUB = 128


def _attn_kernel(qi_ref, qa_ref, wi_ref, ki_ref, k_ref, vt_ref, out_ref,
                 keys_ref, planes_ref, cand_ref, cut_ref, bias_ref, lg_ref, cm_ref, m_ref, acc_ref, *, k_sel, idx_bits):
    bq, sc = ATTN_BQ, ATTN_SC
    wpc = sc // 32
    i = pl.program_id(0)
    n_ch = (i * bq) // sc + 1
    t_pos = i * bq + lax.broadcasted_iota(I32, (1, bq), 1)
    sub = lax.broadcasted_iota(I32, (sc, 1), 0)
    qi = qi_ref[...].reshape(N_HEADS_IDX * bq, D_IDX)
    w = wi_ref[...]

    @pl.when(i == 0)
    def _():
        planes_ref[...] = jnp.zeros(planes_ref.shape, I32)

    def chunk(c):
        return pl.ds(pl.multiple_of(c * sc, sc), sc)

    def score_chunk(c, carry):
        off = pl.multiple_of(c * sc, sc)
        d = _dot_nt(ki_ref[chunk(c), :], qi)
        s = jnp.zeros((sc, bq), F32)
        for h in range(N_HEADS_IDX):
            s = s + jnp.maximum(d[:, h * bq:(h + 1) * bq], 0.0) * w[h:h + 1, :]
        key = jnp.where(off + sub <= t_pos, _sort_key(s), INT_MIN)
        keys_ref[chunk(c), :] = key
        u = key ^ INT_MIN
        for grp in range(sc // KEYS_PER_GROUP):
            base = grp * KEYS_PER_GROUP
            words = _bit_planes([u[base + SUBLANES * v:base + SUBLANES * (v + 1), :] for v in range(32)])
            rows = pl.ds(pl.multiple_of(c * wpc + grp * SUBLANES, SUBLANES), SUBLANES)
            for p in range(32):
                planes_ref[p, rows, :] = words[p]
        return carry

    lax.fori_loop(0, n_ch // 2, lambda j, cr: score_chunk(2 * j + 1, score_chunk(2 * j, cr)), 0)
    lax.fori_loop(n_ch - n_ch % 2, n_ch, score_chunk, 0)

    def count(pred):
        def body(c, acc):
            hit = jnp.where(pred(keys_ref[chunk(c), :], c * sc + sub), 1, 0)
            return acc + hit.reshape(sc // SUBLANES, SUBLANES, bq).sum(axis=0)

        def body2(j, acc):
            return body(2 * j + 1, body(2 * j, acc))
        acc = lax.fori_loop(0, n_ch // 2, body2, jnp.zeros((SUBLANES, bq), I32))
        acc = lax.fori_loop(n_ch - n_ch % 2, n_ch, body, acc)
        return acc.sum(axis=0, keepdims=True)

    slab = SEARCH_SLAB
    n_slab = (n_ch * wpc + slab - 1) // slab
    srow = lax.broadcasted_iota(I32, (slab, 1), 0)

    def slab_rows(sl):
        return pl.ds(pl.multiple_of(sl * slab, slab), slab)

    def init_cand(sl, carry):
        cand_ref[slab_rows(sl), :] = jnp.where(sl * slab + srow < n_ch * wpc, -1, 0) + jnp.zeros((slab, bq), I32)
        return carry

    lax.fori_loop(0, n_slab, init_cand, 0)

    def sweep(prev, cur):
        def body(sl, acc):
            r = slab_rows(sl)
            cand = cand_ref[r, :]
            if prev is not None:
                ones = cand & planes_ref[prev[0], r, :]
                cand = jnp.where(prev[1] != 0, ones, cand ^ ones)
                cand_ref[r, :] = cand
            hits = cand if cur is None else cand & planes_ref[cur, r, :]
            cnt = lax.population_count(hits)
            return acc + cnt.reshape(slab // SUBLANES, SUBLANES, bq).sum(axis=0)
        acc = lax.fori_loop(0, n_slab, body, jnp.zeros((SUBLANES, bq), I32))
        return acc.sum(axis=0, keepdims=True)

    def decide(ones, p, need, thr_u):
        keep = ones >= need
        bit = jnp.int32(1) << (31 - p)
        return jnp.where(keep, 1, 0), jnp.where(keep, need, need - ones), jnp.where(keep, thr_u | bit, thr_u)

    keep, need, thr_u = decide(sweep(None, 0), 0, jnp.full((1, bq), k_sel, I32), jnp.zeros((1, bq), I32))

    def search_pass(p, st):
        keep, need, thr_u = st
        return decide(sweep((p - 1, keep), p), p, need, thr_u)

    keep, need, thr_u = lax.fori_loop(1, 32, search_pass, (keep, need, thr_u))
    tied = sweep((31, keep), None)
    thr = thr_u ^ INT_MIN
    cnt = (k_sel - need) + tied
    cut_ref[...] = jnp.full((1, bq), INT_MAX, I32)

    excess = jnp.logical_and(cnt > k_sel, thr > INT_MIN)
    has_ties = jnp.max(jnp.where(excess, 1, 0)) > 0

    @pl.when(has_ties)
    def _():
        want = k_sel - count(lambda blk, _: blk > thr)

        def tie_body(j, cpos):
            cand = cpos + (jnp.int32(1) << (idx_bits - 1 - j))
            below = count(lambda blk, pos: jnp.where(blk == thr, pos, INT_MAX) < cand)
            return jnp.where(below < want, cand, cpos)

        cpos = lax.fori_loop(0, idx_bits, tie_body, jnp.zeros((1, bq), I32))
        cut_ref[...] = jnp.where(excess, cpos, INT_MAX)

    m_ref[...] = jnp.full(m_ref.shape, NEG, F32)
    acc_ref[...] = jnp.zeros(acc_ref.shape, F32)
    cut = cut_ref[...]
    floor = jnp.where(thr == INT_MIN, INT_MIN, thr - 1)
    n_sub = sc // ATTN_SUB

    qs = [qa_ref[g * GQA:(g + 1) * GQA].reshape(GQA * bq, HEAD_DIM_A) for g in range(N_KV_A)]
    neg_row = jnp.full((1, GQA * bq), NEG, F32)

    def stage(c, par, cb, par_b):
        do_a, do_b = c is not None, cb is not None
        if do_a:
            off = pl.multiple_of(c * sc, sc)
            keyc = keys_ref[chunk(c), :]

            @pl.when(has_ties)
            def _():
                tie = jnp.where(off + sub <= cut, 0.0, NEG)
                bias = jnp.where(keyc > thr, 0.0, jnp.where(keyc == thr, tie, NEG))
                bias_ref[...] = jnp.where(keyc == INT_MIN, NEG, bias)

            @pl.when(jnp.logical_not(has_ties))
            def _():
                bias_ref[...] = jnp.where(keyc > floor, 0.0, NEG)
            cmax = [neg_row] * N_KV_A
        if do_b:
            off_b = pl.multiple_of(cb * sc, sc)
            m_new, acc = [], []
            for g in range(N_KV_A):
                m_old = m_ref[g]
                m_new.append(jnp.maximum(m_old, cm_ref[par_b, g]))
                acc.append(jnp.exp2(m_old - m_new[g]) * acc_ref[g])
                m_ref[g] = m_new[g]
        for r in range(n_sub):
            rows = slice(r * ATTN_SUB, (r + 1) * ATTN_SUB)
            if do_a:
                bias4 = jnp.concatenate([bias_ref[rows, :]] * GQA, axis=1)
                for g in range(N_KV_A):
                    lg = _dot_nt(k_ref[g, pl.ds(off + r * ATTN_SUB, ATTN_SUB), :], qs[g]) + bias4
                    lg_ref[par, g, rows, :] = lg
                    cmax[g] = jnp.maximum(cmax[g], lg.max(axis=0, keepdims=True))
            if do_b:
                for g in range(N_KV_A):
                    p = jnp.exp2(lg_ref[par_b, g, rows, :] - m_new[g])
                    vt = vt_ref[g, :, pl.ds(off_b + r * ATTN_SUB, ATTN_SUB)]
                    acc[g] = acc[g] + _dot(vt, p.astype(BF16))
        if do_b:
            for g in range(N_KV_A):
                acc_ref[g] = acc[g]
        if do_a:
            for g in range(N_KV_A):
                cm_ref[par, g] = cmax[g]

    stage(jnp.int32(0), 0, None, None)

    def stage_pair(j, carry):
        stage(2 * j + 1, 1, 2 * j, 0)
        stage(2 * j + 2, 0, 2 * j + 1, 1)
        return carry

    last = n_ch - 1
    lax.fori_loop(0, last // 2, stage_pair, 0)

    @pl.when(last % 2 == 1)
    def _():
        stage(last, 1, last - 1, 0)
        stage(None, None, last, 1)

    @pl.when(last % 2 == 0)
    def _():
        stage(None, None, last, 0)

    heads = []
    for g in range(N_KV_A):
        acc = acc_ref[g]
        o = acc[:HEAD_DIM_A] / acc[HEAD_DIM_A:HEAD_DIM_A + 1]
        heads += [o[:, hh * bq:(hh + 1) * bq] for hh in range(GQA)]
    out_ref[...] = jnp.concatenate(heads, axis=0).T.astype(out_ref.dtype)


def _prompt_attention(qi_hm, qa_hm, wit, kibf, khm, vt):
    t = kibf.shape[0]
    bq = ATTN_BQ
    k_sel = min(TOPK_MAX, t // 4)
    kern = functools.partial(_attn_kernel, k_sel=k_sel, idx_bits=max(1, (t - 1).bit_length()))
    in_specs = [pl.BlockSpec((N_HEADS_IDX, bq, D_IDX), lambda i: (0, i, 0)),
                pl.BlockSpec((N_HEADS_A, bq, HEAD_DIM_A), lambda i: (0, i, 0)),
                pl.BlockSpec((N_HEADS_IDX, bq), lambda i: (0, i)),
                _const_spec(kibf.shape, True), _const_spec(khm.shape, True), _const_spec(vt.shape, True)]
    n_words = -(-(t // 32) // SEARCH_SLAB) * SEARCH_SLAB
    scratch = [pltpu.VMEM((t, bq), I32), pltpu.VMEM((32, n_words, bq), I32), pltpu.VMEM((n_words, bq), I32),
               pltpu.VMEM((1, bq), I32), pltpu.VMEM((ATTN_SC, bq), F32),
               pltpu.VMEM((2, N_KV_A, ATTN_SC, GQA * bq), F32), pltpu.VMEM((2, N_KV_A, 1, GQA * bq), F32),
               pltpu.VMEM((N_KV_A, 1, GQA * bq), F32),
               pltpu.VMEM((N_KV_A, VT_ROWS, GQA * bq), F32)]
    return pl.pallas_call(kern, out_shape=jax.ShapeDtypeStruct((t, WA_Q), BF16), grid=(t // bq,),
                          in_specs=in_specs, out_specs=pl.BlockSpec((bq, WA_Q), lambda i: (i, 0)),
                          scratch_shapes=scratch, compiler_params=_cparams(), name="attn")(
                              qi_hm, qa_hm, wit, kibf, khm, vt)


SAMPLE_CK = 2048


def _sample_attn_kernel(pt_ref, qi_ref, qa_ref, wi_ref, kin_ref, kn_ref, vn_ref, cidx_hbm, ck_hbm, cv_hbm, out_ref,
                        idx_buf, k_buf, v_buf, scr_ref, cut_ref, sems, *, n_pages, ps, ck, k_sel, idx_bits):
    b = pl.program_id(0)
    nb = pl.num_programs(0)
    past = n_pages * ps
    nc = past // ck

    def page_copy(which, bb, p):
        src, dst = ((cidx_hbm, idx_buf), (ck_hbm, k_buf), (cv_hbm, v_buf))[which]
        cols = pl.ds(pl.multiple_of(p * ps, ps), ps)
        dst = dst.at[:, cols] if which == 0 else dst.at[:, :, cols]
        return pltpu.make_async_copy(src.at[pt_ref[bb, p]], dst, sems.at[which])

    def for_pages(fn):
        lax.fori_loop(0, n_pages, lambda p, c: (fn(p), c)[1], 0, unroll=8)

    def wait_all(which):
        buf = (idx_buf, k_buf, v_buf)[which]
        pltpu.make_async_copy(buf, buf, sems.at[which]).wait()

    @pl.when(b == 0)
    def _():
        for_pages(lambda p: page_copy(0, b, p).start())

    for_pages(lambda p: (page_copy(1, b, p).start(), page_copy(2, b, p).start()))
    wait_all(0)

    qi = qi_ref[0]
    w = wi_ref[0]
    for c in range(nc):
        d = _dot(qi, idx_buf[:, c * ck:(c + 1) * ck].astype(BF16))
        scr_ref[c:c + 1, :] = (jnp.maximum(d, 0.0) * w).sum(axis=0, keepdims=True)
    d_new = (qi.astype(F32) * kin_ref[0].astype(BF16).astype(F32)).sum(axis=-1, keepdims=True)
    s_new = (jnp.maximum(d_new, 0.0) * w).sum(axis=0, keepdims=True)

    @pl.when(b + 1 < nb)
    def _():
        for_pages(lambda p: page_copy(0, b + 1, p).start())

    keys = _sort_key(scr_ref[...])
    key_new = _sort_key(s_new)
    pos = (lax.broadcasted_iota(I32, keys.shape, 0) * ck + lax.broadcasted_iota(I32, keys.shape, 1))

    def count(pred):
        hit = jnp.where(pred(keys, pos), 1, 0)
        parts = [hit[:, j * LANES:(j + 1) * LANES] for j in range(ck // LANES)]
        while len(parts) > 1:
            parts = [a + b for a, b in zip(parts[::2], parts[1::2])] + parts[len(parts) - len(parts) % 2:]
        total = parts[0].sum(axis=1, keepdims=True).sum(axis=0, keepdims=True)
        return total + jnp.where(pred(key_new, past), 1, 0)

    dcol = lax.broadcasted_iota(I32, (16, 1), 0)

    def search_body(j, st):
        thr, cnt = st
        step = jnp.int32(1) << (28 - 4 * j)
        rows = [jnp.zeros((1, LANES), I32)]
        for d in range(1, 16):
            hit = jnp.where(keys >= thr + d * step, 1, 0)
            parts = [hit[:, t * LANES:(t + 1) * LANES] for t in range(ck // LANES)]
            while len(parts) > 1:
                parts = [a + b for a, b in zip(parts[::2], parts[1::2])] + parts[len(parts) - len(parts) % 2:]
            rows.append(parts[0].sum(axis=0, keepdims=True))
        cands = thr + dcol * step
        cnts = jnp.concatenate(rows, axis=0).sum(axis=1, keepdims=True) + jnp.where(key_new >= cands, 1, 0)
        ok = jnp.logical_and(cnts >= k_sel, dcol >= 1)
        digit = jnp.where(ok, 1, 0).sum(axis=0, keepdims=True)
        best = jnp.where(ok, cnts, INT_MAX).min(axis=0, keepdims=True)
        return thr + digit * step, jnp.where(digit > 0, best, cnt)

    st0 = (jnp.full((1, 1), INT_MIN, I32), jnp.full((1, 1), past + 1, I32))
    thr, cnt = lax.fori_loop(0, 8, search_body, st0)
    cut_ref[...] = jnp.full((1, 1), INT_MAX, I32)

    @pl.when(jnp.max(jnp.where(cnt > k_sel, 1, 0)) > 0)
    def _():
        want = k_sel - count(lambda kk, _: kk > thr)

        def tie_body(j, cpos):
            cand = cpos + (jnp.int32(1) << (idx_bits - 1 - j))
            below = count(lambda kk, pp: jnp.where(kk == thr, pp, INT_MAX) < cand)
            return jnp.where(below < want, cand, cpos)

        cut_ref[...] = lax.fori_loop(0, idx_bits, tie_body, jnp.zeros((1, 1), I32))

    cut = cut_ref[...]

    def sel_bias(kk, pp):
        tie = jnp.where(pp <= cut, 0.0, NEG)
        return jnp.where(kk > thr, 0.0, jnp.where(kk == thr, tie, NEG))

    bias = sel_bias(keys, pos)
    bias_new = sel_bias(key_new, past)

    wait_all(1)
    wait_all(2)

    qa = qa_ref[0]
    heads = []
    for g in range(N_KV_A):
        qg = qa[g * GQA:(g + 1) * GQA]
        cols = slice(g * HEAD_DIM_A, (g + 1) * HEAD_DIM_A)
        spans = [slice(c * ck, (c + 1) * ck) for c in range(nc)]
        lgs = [_dot(qg, k_buf[g, :, sp].astype(BF16)) + bias[c:c + 1, :] for c, sp in enumerate(spans)]
        kn = kn_ref[0][:, cols].astype(BF16).astype(F32)
        vn = vn_ref[0][:, cols].astype(BF16).astype(F32)
        lg_new = (qg.astype(F32) * kn).sum(axis=-1, keepdims=True) + bias_new
        m = lg_new
        for lg in lgs:
            m = jnp.maximum(m, lg.max(axis=-1, keepdims=True))
        p_new = jnp.exp2(lg_new - m)
        l = p_new
        acc = p_new * vn
        for lg, sp in zip(lgs, spans):
            p = jnp.exp2(lg - m)
            l = l + p.sum(axis=-1, keepdims=True)
            acc = acc + _dot_nt(p.astype(BF16), v_buf[g, :, sp].astype(BF16))
        o = acc / l
        heads += [o[hh:hh + 1, :] for hh in range(GQA)]
    out_ref[0] = jnp.concatenate(heads, axis=1)


def _sample_attention(page_table, qi_s, qa_s, wi_s, ki_new, k_new, v_new, cache_idx_k, cache_k, cache_v):
    nbatch, n_pages = page_table.shape
    n_pool, ps, d_idx = cache_idx_k.shape
    past = n_pages * ps
    ck = min(SAMPLE_CK, past)
    assert past % ck == 0 and ck % ps == 0
    k_sel = min(TOPK_MAX, (past + 1) // 4)
    kern = functools.partial(_sample_attn_kernel, n_pages=n_pages, ps=ps, ck=ck, k_sel=k_sel,
                             idx_bits=past.bit_length())
    per_b = lambda shape: pl.BlockSpec((1,) + shape, lambda b, pt: (b, 0, 0))
    any_spec = pl.BlockSpec(memory_space=pl.ANY)
    grid_spec = pltpu.PrefetchScalarGridSpec(
        num_scalar_prefetch=1, grid=(nbatch,),
        in_specs=[per_b((N_HEADS_IDX, d_idx)), per_b((N_HEADS_A, HEAD_DIM_A)), per_b((N_HEADS_IDX, 1)),
                  per_b((1, d_idx)), per_b((1, WA_KV)), per_b((1, WA_KV)), any_spec, any_spec, any_spec],
        out_specs=per_b((1, WA_Q)),
        scratch_shapes=[pltpu.VMEM((d_idx, past), F32), pltpu.VMEM((N_KV_A, HEAD_DIM_A, past), F32),
                        pltpu.VMEM((N_KV_A, HEAD_DIM_A, past), F32),
                        pltpu.VMEM((past // ck, ck), F32), pltpu.VMEM((1, 1), I32),
                        pltpu.SemaphoreType.DMA((3,))])
    return pl.pallas_call(kern, out_shape=jax.ShapeDtypeStruct((nbatch, 1, WA_Q), F32), grid_spec=grid_spec,
                          compiler_params=_cparams(), name="sample_attn")(
                              page_table, qi_s, qa_s, wi_s, ki_new, k_new, v_new,
                              jnp.transpose(cache_idx_k, (0, 2, 1)), jnp.transpose(cache_k, (0, 2, 3, 1)),
                              jnp.transpose(cache_v, (0, 2, 3, 1)))


def _rope_tables(pos):
    rot = HEAD_DIM_A // ROT_FRACTION
    half = rot // 2
    inv_freq = ROPE_THETA ** (-jnp.arange(half, dtype=F32) / half)
    ang = pos.astype(F32)[:, None] * inv_freq[None, :]
    cos, sin = jnp.cos(ang), jnp.sin(ang)
    n = pos.shape[0]
    z = lambda w: jnp.zeros((n, w), F32)
    c = jnp.concatenate([cos, cos, jnp.ones((n, HEAD_DIM_A - rot), F32)], axis=1)
    sa = jnp.concatenate([-sin, z(HEAD_DIM_A - half)], axis=1)
    sb = jnp.concatenate([z(half), sin, z(HEAD_DIM_A - rot)], axis=1)
    rep = LANES // HEAD_DIM_A
    return tuple(jnp.tile(a, (1, rep)) for a in (c, sa, sb))


def _prep_params(w_in, g_in, g_qa, g_ka):
    d = w_in.shape[0]
    w_qa, w_ka, w_va, w_qi, w_ki, w_wi = _split_w_in(w_in, d)[:6]
    wa = jnp.concatenate([w_qa, w_ka, w_va, w_qi, w_ki, jnp.zeros((d, LANES - D_IDX), w_in.dtype)], axis=1)
    lane = jnp.arange(LANES)
    bd = (lane[:, None] // HEAD_DIM_A == lane[None, :] // HEAD_DIM_A).astype(BF16)
    rep = LANES // HEAD_DIM_A
    return (g_in.reshape(1, d), wa.astype(BF16), w_wi.T.astype(BF16), w_va.T.astype(BF16), bd,
            jnp.tile(g_qa, rep).reshape(1, LANES), jnp.tile(g_ka, rep).reshape(1, LANES))


def _split_w_in(w_in, d):
    sizes = (WA_Q, WA_KV, WA_KV, WI_Q, D_IDX, N_HEADS_IDX, WIDTH_B, WIDTH_B, WM_Q, N_BRANCH * d)
    parts, o = [], 0
    for s in sizes:
        parts.append(w_in[:, o:o + s])
        o += s
    return parts


def _mix_kernel(*refs, chunked, shared_mem, emit_vb, n_experts):
    it = iter(refs)
    x_ref, oa_ref, gin_ref, wb_ref, gvb_ref, gqm_ref = (next(it) for _ in range(6))
    if chunked:
        wtril_ref, bt_ref = next(it), next(it)
    else:
        wdiag_ref, bdiag_ref = next(it), next(it)
    mk_ref, mv_ref = next(it), next(it)
    wbra_ref, wbrb_ref, wbrm_ref, wout_ref, gffn_ref, wrh_ref, wrl_ref, br_ref = (next(it) for _ in range(8))
    x1_ref, h2_ref, te_ref, tw_ref = (next(it) for _ in range(4))
    vb_ref = next(it) if emit_vb else None

    x = x_ref[...]
    tm, d = x.shape
    h = _rms(x, gin_ref[...]).astype(BF16)
    p = _dot(h, wb_ref[...])
    ub = p[:, :WIDTH_B]
    vb = _rms(p[:, WIDTH_B:2 * WIDTH_B], gvb_ref[...])
    if emit_vb:
        vb_ref[...] = vb
    o = 2 * WIDTH_B
    qm = p[:, o:o + WM_Q]
    o += WM_Q
    gates = p[:, o:o + N_BRANCH * d]

    if chunked:
        vbb = vb.astype(BF16)
        rows = []
        for cc in range(tm // CHUNK):
            cols = []
            for g in range(N_GROUPS_B):
                vg = vbb[cc * CHUNK:(cc + 1) * CHUNK, g * GROUP_DIM_B:(g + 1) * GROUP_DIM_B]
                cols.append(_dot(wtril_ref[g], vg))
            rows.append(jnp.concatenate(cols, axis=1) + bt_ref[...])
        z = jnp.concatenate(rows, axis=0) if len(rows) > 1 else rows[0]
    else:
        z = vb * wdiag_ref[...] + bdiag_ref[...]
    out_b = ub * z

    scale_m = HEAD_DIM_M ** -0.5
    outs = []
    for hh in range(N_HEADS_M):
        sl = slice(hh * HEAD_DIM_M, (hh + 1) * HEAD_DIM_M)
        qh = _rms(qm[:, sl], gqm_ref[...])
        if shared_mem:
            lg = _dot_nt(qh.astype(BF16), mk_ref[:, sl].astype(BF16)) * scale_m
            pm = jnp.exp(lg - lg.max(axis=-1, keepdims=True))
            pm = pm / pm.sum(axis=-1, keepdims=True)
            outs.append(_dot(pm.astype(BF16), mv_ref[:, sl].astype(BF16)))
        else:
            per_row = []
            for r in range(tm):
                kr = mk_ref[r, :, sl]
                lg = (kr * qh[r:r + 1, :]).sum(axis=-1, keepdims=True) * scale_m
                pm = jnp.exp(lg - lg.max(axis=0, keepdims=True))
                pm = pm / pm.sum(axis=0, keepdims=True)
                per_row.append((pm * mv_ref[r, :, sl]).sum(axis=0, keepdims=True))
            outs.append(jnp.concatenate(per_row, axis=0))
    out_m = jnp.concatenate(outs, axis=1)

    sig = jax.nn.sigmoid
    merged = (sig(gates[:, :d]) * _dot(oa_ref[...].astype(BF16), wbra_ref[...])
              + sig(gates[:, d:2 * d]) * _dot(out_b.astype(BF16), wbrb_ref[...])
              + sig(gates[:, 2 * d:]) * _dot(out_m.astype(BF16), wbrm_ref[...]))
    x1 = x + _dot(merged.astype(BF16), wout_ref[...])
    x1_ref[...] = x1
    h2 = _rms(x1, gffn_ref[...])
    h2_ref[...] = h2.astype(h2_ref.dtype)

    hi = h2.astype(BF16)
    lo = (h2 - hi.astype(F32)).astype(BF16)
    lg = _dot(hi, wrh_ref[...]) + _dot(hi, wrl_ref[...]) + _dot(lo, wrh_ref[...]) + br_ref[...]
    lane = lax.broadcasted_iota(I32, lg.shape, 1)
    lg = jnp.where(lane < n_experts, lg, -jnp.inf)
    vals, idxs = [], []
    for _ in range(TOP_K_MOE):
        m = lg.max(axis=-1, keepdims=True)
        idx = jnp.where(lg == m, lane, LANES).min(axis=-1, keepdims=True)
        vals.append(m)
        idxs.append(idx)
        lg = jnp.where(lane == idx, -jnp.inf, lg)
    ex = [jnp.exp(v - vals[0]) for v in vals]
    den = ex[0]
    for e in ex[1:]:
        den = den + e
    te = jnp.full(lane.shape, -1, I32)
    tw = jnp.zeros(lane.shape, F32)
    for j in range(TOP_K_MOE):
        te = jnp.where(lane == j, idxs[j], te)
        tw = jnp.where(lane == j, ex[j] / den, tw)
    te_ref[...] = te
    tw_ref[...] = tw


def _mix(x2d, out_a, mix_params, gmlp_params, mem_k, mem_v, *, tm, chunked, shared_mem, emit_vb, n_experts):
    n, d = x2d.shape
    gin, wb, gvb, gqm, wbra, wbrb, wbrm, wout, gffn, wrh, wrl, br = mix_params
    row = lambda w: pl.BlockSpec((tm, w), lambda i: (i, 0))
    if shared_mem:
        mem_specs = [_const_spec(mem_k.shape), _const_spec(mem_v.shape)]
    else:
        mspec = pl.BlockSpec((tm,) + mem_k.shape[1:], lambda i: (i, 0, 0))
        mem_specs = [mspec, mspec]
    consts = lambda arrs: [_const_spec(a.shape) for a in arrs]
    in_specs = ([row(d), row(WA_Q)] + consts([gin, wb, gvb, gqm]) + consts(gmlp_params) + mem_specs
                + consts([wbra, wbrb, wbrm, wout, gffn, wrh, wrl, br]))
    out_shape = [jax.ShapeDtypeStruct((n, d), F32), jax.ShapeDtypeStruct((n, d), F32),
                 jax.ShapeDtypeStruct((n, LANES), I32), jax.ShapeDtypeStruct((n, LANES), F32)]
    out_specs = [row(d), row(d), row(LANES), row(LANES)]
    if emit_vb:
        out_shape.append(jax.ShapeDtypeStruct((n, WIDTH_B), F32))
        out_specs.append(row(WIDTH_B))
    kern = functools.partial(_mix_kernel, chunked=chunked, shared_mem=shared_mem, emit_vb=emit_vb,
                             n_experts=n_experts)
    return pl.pallas_call(kern, out_shape=tuple(out_shape), grid=(n // tm,), in_specs=in_specs,
                          out_specs=tuple(out_specs), compiler_params=_cparams(), name="mix")(
                              x2d, out_a, gin, wb, gvb, gqm, *gmlp_params, mem_k, mem_v,
                              wbra, wbrb, wbrm, wout, gffn, wrh, wrl, br)


def _prep_mix_params(w_in, g_in, g_vb, g_qm, w_br_a, w_br_b, w_br_m, w_out, g_ffn, w_router, b_router):
    d = w_in.shape[0]
    parts = _split_w_in(w_in, d)
    wb = jnp.concatenate(parts[6:10], axis=1).astype(BF16)
    n_e = w_router.shape[1]
    wr = jnp.pad(w_router, ((0, 0), (0, LANES - n_e)))
    wrh = wr.astype(BF16)
    wrl = (wr - wrh.astype(F32)).astype(BF16)
    br = jnp.pad(b_router, (0, LANES - n_e)).reshape(1, LANES)
    return (g_in.reshape(1, d), wb, g_vb.reshape(1, WIDTH_B), g_qm.reshape(1, HEAD_DIM_M),
            w_br_a.astype(BF16), w_br_b.astype(BF16), w_br_m.astype(BF16), w_out.astype(BF16),
            g_ffn.reshape(1, d), wrh, wrl, br)


def _prep_gmlp_chunked(w_s, b_s):
    tril = jnp.tril(jnp.ones((CHUNK, CHUNK), w_s.dtype))
    wtril = (w_s * tril).astype(BF16)
    bt = jnp.repeat(b_s.T, GROUP_DIM_B, axis=1)
    return wtril, bt


def _prep_gmlp_single(w_s, b_s):
    wdiag = jnp.repeat(w_s[:, 0, 0], GROUP_DIM_B).reshape(1, WIDTH_B)
    bdiag = jnp.repeat(b_s[:, 0], GROUP_DIM_B).reshape(1, WIDTH_B)
    return wdiag, bdiag


MOE_TP = 256


def _plan_kernel(te_ref, tri_ref, upper_ref, lslot_ref, cnt_ref, tstart_ref, gprev_ref, total_ref, carry_ref):
    @pl.when(pl.program_id(0) == 0)
    def _():
        carry_ref[...] = jnp.zeros(carry_ref.shape, F32)

    te = te_ref[...]
    lane = lax.broadcasted_iota(I32, te.shape, 1)
    picks = [te[:, j:j + 1] for j in range(TOP_K_MOE)]
    onehot = jnp.zeros(te.shape, F32)
    for e in picks:
        onehot = onehot + jnp.where(lane == e, 1.0, 0.0)
    before = _dot(tri_ref[...], onehot.astype(BF16))
    cnt = onehot.sum(axis=0, keepdims=True)
    cnt = jnp.floor((cnt + (SUBLANES - 1)) * (1.0 / SUBLANES)) * SUBLANES
    cnt8 = jnp.broadcast_to(cnt, (SUBLANES, LANES)).astype(BF16)
    tstart = _dot(cnt8, upper_ref[...])[0:1, :]
    lslot = jnp.zeros(te.shape, I32)
    for j, e in enumerate(picks):
        r = jnp.where(lane == e, before + tstart, 0.0).sum(axis=-1, keepdims=True).astype(I32)
        lslot = jnp.where(lane == j, r, lslot)
    lslot_ref[...] = lslot
    cnt_ref[0] = cnt.astype(I32)
    tstart_ref[0] = tstart.astype(I32)
    gprev_ref[0] = carry_ref[...].astype(I32)
    carry_ref[...] = carry_ref[...] + cnt
    total_ref[...] = carry_ref[...].astype(I32)


def _moe_plan(te_all):
    n = te_all.shape[0]
    tp = MOE_TP
    n_tiles = n // tp
    r = jnp.arange(tp)
    tri = (r[:, None] > r[None, :]).astype(BF16)
    e = jnp.arange(LANES)
    upper = (e[:, None] < e[None, :]).astype(BF16)
    per_tile = jax.ShapeDtypeStruct((n_tiles, 1, LANES), I32)
    tile_spec = pl.BlockSpec((1, 1, LANES), lambda i: (i, 0, 0))
    return pl.pallas_call(
        _plan_kernel,
        out_shape=(jax.ShapeDtypeStruct((n, LANES), I32), per_tile, per_tile, per_tile,
                   jax.ShapeDtypeStruct((1, LANES), I32)),
        grid=(n_tiles,),
        in_specs=[pl.BlockSpec((tp, LANES), lambda i: (i, 0)), _const_spec((tp, tp)), _const_spec((LANES, LANES))],
        out_specs=(pl.BlockSpec((tp, LANES), lambda i: (i, 0)), tile_spec, tile_spec, tile_spec,
                   _const_spec((1, LANES))),
        scratch_shapes=[pltpu.VMEM((1, LANES), F32)],
        compiler_params=_cparams(), name="moe_plan")(te_all, tri, upper)


def _run_pieces(tp):
    sizes, s = [], SUBLANES
    while s <= tp:
        sizes.append(s)
        s *= 2
    return tuple(reversed(sizes))


def _local_rows(tp, n_e):
    return tp * TOP_K_MOE + n_e * SUBLANES


def _for_runs(tabs, tile, n_e, tp, fn):
    cnt_s, tstart_s, gbase_s = tabs

    def per_expert(e, c):
        idx = tile * n_e + e
        length, src0, dst0 = cnt_s[idx], tstart_s[idx], gbase_s[idx]
        off = jnp.int32(0)
        for size in _run_pieces(tp):
            piece = length & size

            @pl.when(piece != 0)
            def _():
                fn(pl.multiple_of(src0 + off, SUBLANES), pl.multiple_of(dst0 + off, SUBLANES), size)
            off = off + piece
        return c

    lax.fori_loop(0, n_e, per_expert, 0)


def _slot_matrix(lslot, vals, n_rows):
    col = lax.broadcasted_iota(I32, (lslot.shape[0], n_rows), 1)
    m = jnp.zeros(col.shape, F32)
    for j in range(TOP_K_MOE):
        v = 1.0 if vals is None else vals[:, j:j + 1]
        m = m + jnp.where(lslot[:, j:j + 1] == col, v, 0.0)
    return m


def _dispatch_kernel(cnt_s, tstart_s, gbase_s, fstart_s, flen_s, nused_s, lslot_ref, h2_ref, *rest,
                     tp, n_e, tile0, first, bm):
    if first:
        xs_out, stage, sem, zbuf, zsem = rest
    else:
        _, xs_out, stage, sem = rest
    n_rows = _local_rows(tp, n_e)
    i = pl.program_id(0)
    slot = i % 2
    pt = _slot_matrix(lslot_ref[...], None, n_rows).astype(BF16)
    stage[slot] = lax.dot_general(pt, h2_ref[...].astype(BF16), (((0,), (0,)), ((), ())),
                                  preferred_element_type=F32)

    def runs(tile, s, op):
        _for_runs((cnt_s, tstart_s, gbase_s), tile0 + tile, n_e, tp,
                  lambda src, dst, size: op(pltpu.make_async_copy(
                      stage.at[s, pl.ds(src, size), :], xs_out.at[pl.ds(dst, size), :], sem.at[s])))

    @pl.when(i > 0)
    def _():
        runs(i - 1, 1 - slot, lambda cp: cp.wait())

    runs(i, slot, lambda cp: cp.start())

    @pl.when(i == pl.num_programs(0) - 1)
    def _():
        runs(i, slot, lambda cp: cp.wait())
        if first:
            zbuf[...] = jnp.zeros(zbuf.shape, F32)
            nb = xs_out.shape[0] // bm

            def fill(op):
                def per_expert(e, c):
                    length, dst0 = flen_s[e], fstart_s[e]
                    off = jnp.int32(0)
                    for size in _run_pieces(bm):
                        piece = length & size

                        @pl.when(piece != 0)
                        def _():
                            op(pltpu.make_async_copy(
                                zbuf.at[pl.ds(0, size), :],
                                xs_out.at[pl.ds(pl.multiple_of(dst0 + off, SUBLANES), size), :], zsem))
                        off = off + piece
                    return c

                lax.fori_loop(0, n_e, per_expert, 0)
                lax.fori_loop(nused_s[0], nb, lambda b, c: (op(pltpu.make_async_copy(
                    zbuf, xs_out.at[pl.ds(pl.multiple_of(b * bm, bm), bm), :], zsem)), c)[1], 0)

            fill(lambda cp: cp.start())
            fill(lambda cp: cp.wait())


def _dispatch(tabs, fill_tabs, tile0, lslot, h2, xs, n_e, ns, bm):
    n, d = h2.shape
    tp = min(MOE_TP, n)
    n_rows = _local_rows(tp, n_e)
    first = xs is None
    kern = functools.partial(_dispatch_kernel, tp=tp, n_e=n_e, tile0=tile0, first=first, bm=bm)
    any_spec = pl.BlockSpec(memory_space=pl.ANY)
    scratch = [pltpu.VMEM((2, n_rows, d), F32), pltpu.SemaphoreType.DMA((2,))]
    if first:
        scratch += [pltpu.VMEM((bm, d), F32), pltpu.SemaphoreType.DMA]
    grid_spec = pltpu.PrefetchScalarGridSpec(
        num_scalar_prefetch=6, grid=(n // tp,),
        in_specs=[pl.BlockSpec((tp, LANES), lambda i, *_: (i, 0)), pl.BlockSpec((tp, d), lambda i, *_: (i, 0))]
        + ([] if first else [any_spec]),
        out_specs=any_spec, scratch_shapes=scratch)
    args = (*tabs, *fill_tabs, lslot, h2) + (() if first else (xs,))
    return pl.pallas_call(
        kern, out_shape=jax.ShapeDtypeStruct((ns, d), F32), grid_spec=grid_spec,
        input_output_aliases={} if first else {8: 0},
        compiler_params=pltpu.CompilerParams(dimension_semantics=("arbitrary",), has_side_effects=True,
                                             vmem_limit_bytes=VMEM_LIMIT),
        name="dispatch")(*args)


def _expert_kernel(be_ref, nu_ref, xs_ref, wg_ref, wu_ref, wd_ref, bg_ref, bu_ref, bd_ref, y_ref,
                   wg_bf, wu_bf, wd_bf):
    b = pl.program_id(0)

    @pl.when(b >= nu_ref[0])
    def _():
        y_ref[...] = jnp.zeros(y_ref.shape, F32)

    @pl.when(b < nu_ref[0])
    def _():
        prev = be_ref[jnp.maximum(b - 1, 0)]

        @pl.when(jnp.logical_or(b == 0, be_ref[b] != prev))
        def _():
            wg_bf[...] = wg_ref[0].astype(BF16)
            wu_bf[...] = wu_ref[0].astype(BF16)
            wd_bf[...] = wd_ref[0].astype(BF16)

        xb = xs_ref[...].astype(BF16)
        ff = wg_bf.shape[1]
        y = jnp.zeros(y_ref.shape, F32) + bd_ref[0]
        for n in range(ff // FF_CHUNK):
            sl = slice(n * FF_CHUNK, (n + 1) * FF_CHUNK)
            hg = jnp.minimum(_dot(xb, wg_bf[:, sl]) + bg_ref[0][:, sl], SWIGLU_LIMIT)
            hu = jnp.clip(_dot(xb, wu_bf[:, sl]) + bu_ref[0][:, sl], -SWIGLU_LIMIT, SWIGLU_LIMIT)
            act = hg * jax.nn.sigmoid(SWIGLU_ALPHA * hg) * (hu + 1.0)
            y = y + _dot(act.astype(BF16), wd_bf[sl, :])
        y_ref[...] = y


def _experts(xs, block_e, n_used, w_gate, b_gate, w_up, b_up, w_down, b_down):
    ns, d = xs.shape
    n_e, _, ff = w_gate.shape
    bm = MOE_BM
    blk = lambda b, be, nu: (jnp.minimum(b, nu[0] - 1), 0)
    wsel = lambda b, be, nu: (be[b], 0, 0)
    grid_spec = pltpu.PrefetchScalarGridSpec(
        num_scalar_prefetch=2, grid=(ns // bm,),
        in_specs=[pl.BlockSpec((bm, d), blk),
                  pl.BlockSpec((1, d, ff), wsel), pl.BlockSpec((1, d, ff), wsel), pl.BlockSpec((1, ff, d), wsel),
                  pl.BlockSpec((1, 1, ff), wsel), pl.BlockSpec((1, 1, ff), wsel), pl.BlockSpec((1, 1, d), wsel)],
        out_specs=pl.BlockSpec((bm, d), lambda b, be, nu: (b, 0)),
        scratch_shapes=[pltpu.VMEM((d, ff), BF16), pltpu.VMEM((d, ff), BF16), pltpu.VMEM((ff, d), BF16)])
    return pl.pallas_call(_expert_kernel, out_shape=jax.ShapeDtypeStruct((ns, d), F32), grid_spec=grid_spec,
                          compiler_params=_cparams(), name="experts")(
                              block_e, n_used, xs, w_gate, w_up, w_down,
                              b_gate.reshape(n_e, 1, ff), b_up.reshape(n_e, 1, ff), b_down.reshape(n_e, 1, d))


def _combine_kernel(cnt_s, tstart_s, gbase_s, lslot_ref, tw_ref, x1_ref, yb_hbm, y_ref, buf, sem, *, tp, n_e, tile0):
    n_rows = _local_rows(tp, n_e)
    i = pl.program_id(0)
    slot = i % 2

    def runs(tile, s, op):
        _for_runs((cnt_s, tstart_s, gbase_s), tile0 + tile, n_e, tp,
                  lambda loc, glob, size: op(pltpu.make_async_copy(
                      yb_hbm.at[pl.ds(glob, size), :], buf.at[s, pl.ds(loc, size), :], sem.at[s])))

    @pl.when(i == 0)
    def _():
        buf[...] = jnp.zeros(buf.shape, F32)
        runs(i, slot, lambda cp: cp.start())

    @pl.when(i + 1 < pl.num_programs(0))
    def _():
        runs(i + 1, 1 - slot, lambda cp: cp.start())

    runs(i, slot, lambda cp: cp.wait())

    ptw = _slot_matrix(lslot_ref[...], tw_ref[...], n_rows)
    rows = buf[slot]
    w_hi = ptw.astype(BF16)
    w_lo = (ptw - w_hi.astype(F32)).astype(BF16)
    r_hi = rows.astype(BF16)
    r_lo = (rows - r_hi.astype(F32)).astype(BF16)
    y_ref[...] = x1_ref[...] + _dot(w_hi, r_hi) + (_dot(w_hi, r_lo) + _dot(w_lo, r_hi))


def _combine(tabs, tile0, lslot, tw, x1, yb, n_e):
    n, d = x1.shape
    tp = min(MOE_TP, n)
    n_rows = _local_rows(tp, n_e)
    kern = functools.partial(_combine_kernel, tp=tp, n_e=n_e, tile0=tile0)
    row = lambda w: pl.BlockSpec((tp, w), lambda i, *_: (i, 0))
    grid_spec = pltpu.PrefetchScalarGridSpec(
        num_scalar_prefetch=3, grid=(n // tp,),
        in_specs=[row(LANES), row(LANES), row(d), pl.BlockSpec(memory_space=pl.ANY)],
        out_specs=row(d),
        scratch_shapes=[pltpu.VMEM((2, n_rows, d), F32), pltpu.SemaphoreType.DMA((2,))])
    return pl.pallas_call(kern, out_shape=jax.ShapeDtypeStruct(x1.shape, F32), grid_spec=grid_spec,
                          compiler_params=_cparams(), name="combine")(*tabs, lslot, tw, x1, yb)


def _moe(groups, w_gate, b_gate, w_up, b_up, w_down, b_down):
    n_e = w_gate.shape[0]
    d = groups[0][0].shape[1]
    tp, bm = MOE_TP, MOE_BM
    sizes = [g[0].shape[0] for g in groups]
    assert all(n % tp == 0 or n < tp for n in sizes), sizes
    parts, tile0s, o = [], [], 0
    for g, n in zip(groups, sizes):
        n_pad = -(-n // tp) * tp
        parts.append(jnp.pad(g[2], ((0, n_pad - n), (0, 0)), constant_values=-1))
        tile0s.append(o // tp)
        o += n_pad
    lslot, cnt3, tstart3, gprev3, total = _moe_plan(jnp.concatenate(parts, axis=0))
    counts = total[0, :n_e]
    padded_cnt = (counts + bm - 1) // bm * bm
    pad_end = jnp.cumsum(padded_cnt)
    pad_start = pad_end - padded_cnt
    n_assign = sum(sizes) * TOP_K_MOE
    n_slack = (o // tp) * n_e * (SUBLANES - 1)
    nb = -(-(n_assign + n_slack) // bm) + n_e
    n_used = (pad_end[-1] // bm).astype(I32)
    blocks = jnp.arange(nb, dtype=I32)
    block_e = jnp.minimum((pad_end[None, :] <= (blocks * bm)[:, None]).sum(axis=1), n_e - 1).astype(I32)
    block_e = jnp.where(blocks < n_used, block_e, block_e[jnp.maximum(n_used - 1, 0)])
    flat = lambda a: a[:, 0, :n_e].reshape(-1).astype(I32)
    gbase3 = pad_start[None, None, :] + gprev3[:, :, :n_e]
    tabs = (flat(cnt3), flat(tstart3), flat(gbase3))

    after_first = gprev3[tile0s[1], 0, :n_e] if len(groups) > 1 else counts
    fill_start = (pad_start + after_first).astype(I32)
    fill_tabs = (fill_start, (pad_end - fill_start).astype(I32), n_used.reshape(1))
    xs, lslots = None, []
    for (x1, h2, te, tw), n, t0 in zip(groups, sizes, tile0s):
        ls = lslot[t0 * tp:t0 * tp + n]
        lslots.append(ls)
        xs = _dispatch(tabs, fill_tabs, t0, ls, h2, xs, n_e, nb * bm, bm)
    yb = _experts(xs, block_e, n_used.reshape(1), w_gate, b_gate, w_up, b_up, w_down, b_down)
    return [_combine(tabs, t0, ls, tw, x1, yb, n_e)
            for (x1, h2, te, tw), ls, t0 in zip(groups, lslots, tile0s)]


def _row_tile(n, want):
    return want if n % want == 0 else n


def kernel(x_prompt, x_sample, mem_prompt, cache_k, cache_v, cache_idx_k, cache_mem_k, cache_mem_v, page_table,
           g_in, w_in, g_qa, g_ka, g_vb, w_s, b_s, g_qm, g_mem, w_mem_kv, g_km, w_br_a, w_br_b, w_br_m, w_out,
           g_ffn, w_router, b_router, w_gate, b_gate, w_up, b_up, w_down, b_down):
    bp, t, d = x_prompt.shape
    bs, ts, _ = x_sample.shape
    assert bp == 1 and ts == 1 and t % CHUNK == 0 and cache_k.shape[1] == LANES
    n_mem = mem_prompt.shape[1]
    n_pages = page_table.shape[1]
    ps = cache_k.shape[1]
    past = n_pages * ps
    n_e = w_router.shape[1]

    fp = _prep_params(w_in, g_in, g_qa, g_ka)
    mp = _prep_mix_params(w_in, g_in, g_vb, g_qm, w_br_a, w_br_b, w_br_m, w_out, g_ffn, w_router, b_router)

    xp = x_prompt.reshape(t, d)
    qa_p, k32_p, khm_p, v32_p, vt_p, qi_p, ki32_p, kibf_p, wit_p = _front(
        xp, _rope_tables(jnp.arange(t, dtype=I32)), *fp, tm=_row_tile(t, 512), paged=True)
    mem_k, mem_v = _memory_kv(mem_prompt.reshape(n_mem, d), g_mem.reshape(1, d), w_mem_kv.astype(BF16),
                              g_km.reshape(1, HEAD_DIM_M))
    out_a_p = _prompt_attention(qi_p, qa_p, wit_p, kibf_p, khm_p, vt_p)
    x1_p, h2_p, te_p, tw_p = _mix(xp, out_a_p, mp, _prep_gmlp_chunked(w_s, b_s), mem_k, mem_v,
                                  tm=_row_tile(t, 256), chunked=True, shared_mem=True, emit_vb=False, n_experts=n_e)

    xs = x_sample.reshape(bs, d)
    qa_s, k32_s, _, v32_s, _, qi_s, ki32_s, _, wit_s = _front(
        xs, _rope_tables(jnp.full((bs,), past, I32)), *fp, tm=bs, paged=False)
    out_a_s = _sample_attention(page_table, jnp.moveaxis(qi_s, 0, 1), jnp.moveaxis(qa_s, 0, 1),
                                wit_s.T.reshape(bs, N_HEADS_IDX, 1), ki32_s.reshape(bs, 1, D_IDX),
                                k32_s.reshape(bs, 1, WA_KV), v32_s.reshape(bs, 1, WA_KV),
                                cache_idx_k, cache_k, cache_v)
    x1_s, h2_s, te_s, tw_s, vb_s = _mix(xs, out_a_s.reshape(bs, WA_Q), mp, _prep_gmlp_single(w_s, b_s),
                                        cache_mem_k.reshape(bs, n_mem, WM_Q), cache_mem_v.reshape(bs, n_mem, WM_Q),
                                        tm=_row_tile(bs, SUBLANES), chunked=False, shared_mem=False, emit_vb=True,
                                        n_experts=n_e)

    y_p, y_s = _moe([(x1_p, h2_p, te_p, tw_p), (x1_s, h2_s, te_s, tw_s)],
                    w_gate, b_gate, w_up, b_up, w_down, b_down)

    n_pg = t // ps
    paged_kv = lambda a: jnp.transpose(a.reshape(1, n_pg, N_KV_A, HEAD_DIM_A, ps), (0, 1, 4, 2, 3))
    return (y_p.reshape(1, t, d), y_s.reshape(bs, 1, d),
            paged_kv(k32_p), paged_kv(v32_p), jnp.transpose(ki32_p, (0, 2, 1)).reshape(1, n_pg, ps, D_IDX),
            mem_k.reshape(1, n_mem, N_HEADS_M, HEAD_DIM_M), mem_v.reshape(1, n_mem, N_HEADS_M, HEAD_DIM_M),
            k32_s.reshape(bs, 1, N_KV_A, HEAD_DIM_A), v32_s.reshape(bs, 1, N_KV_A, HEAD_DIM_A),
            ki32_s.reshape(bs, 1, D_IDX), vb_s.reshape(bs, 1, WIDTH_B))
```

```python
import functools

import jax
import jax.numpy as jnp
from jax import lax
from jax.experimental import pallas as pl
from jax.experimental.pallas import tpu as pltpu

F32 = jnp.float32
BF16 = jnp.bfloat16
I32 = jnp.int32

N_HEADS_A = 8
N_KV_A = 2
HEAD_DIM_A = 64
TOPK_MAX = 256
N_HEADS_IDX = 8
D_IDX = 64
N_GROUPS_B = 4
GROUP_DIM_B = 128
WIDTH_B = N_GROUPS_B * GROUP_DIM_B
CHUNK = 128
N_HEADS_M = 4
HEAD_DIM_M = 128
ROPE_THETA = 500000.0
ROT_FRACTION = 4
N_BRANCH = 3
TOP_K_MOE = 4
SWIGLU_LIMIT = 7.0
SWIGLU_ALPHA = 1.702
EPS = 1e-6

WA_Q = N_HEADS_A * HEAD_DIM_A
WA_KV = N_KV_A * HEAD_DIM_A
WI_Q = N_HEADS_IDX * D_IDX
WM_Q = N_HEADS_M * HEAD_DIM_M
GQA = N_HEADS_A // N_KV_A

LANES = 128
SUBLANES = 8
VMEM_LIMIT = 56 * 1024 * 1024

LOG2_E = 1.4426950408889634
INT_MIN = -(2 ** 31)
INT_MAX = 2 ** 31 - 1
NEG = -1e30

VT_ROWS = HEAD_DIM_A + 16
ATTN_BQ = 128
ATTN_SC = 512
MOE_BM = 512
FF_CHUNK = 512


def _cparams(n_axes=1, vmem=VMEM_LIMIT):
    return pltpu.CompilerParams(dimension_semantics=("arbitrary",) * n_axes, vmem_limit_bytes=vmem)


def _const_spec(shape, single=False):
    zeros = (0,) * len(shape)
    if single:
        return pl.BlockSpec(shape, lambda *_: zeros, pipeline_mode=pl.Buffered(1))
    return pl.BlockSpec(shape, lambda *_: zeros)


def _dot(a, b):
    return jnp.dot(a, b, preferred_element_type=F32)


def _dot_nt(a, b):
    return lax.dot_general(a, b, (((1,), (1,)), ((), ())), preferred_element_type=F32)


def _rms(x, g):
    return x * lax.rsqrt(jnp.mean(x * x, axis=-1, keepdims=True) + EPS) * g


def _split_dot(x, w):
    hi = x.astype(BF16)
    lo = (x - hi.astype(F32)).astype(BF16)
    return _dot(hi, w) + _dot(lo, w)


def _front_kernel(x_ref, gin_ref, wa_ref, wwit_ref, wvt_ref, bd_ref, gqa_ref, gka_ref, cos_ref, sa_ref, sb_ref,
                  qa_ref, k32_ref, khm_ref, v32_ref, vt_ref, qi_ref, ki32_ref, kibf_ref, wit_ref, *, score_scale, paged):
    x = x_ref[...]
    h = _rms(x, gin_ref[...]).astype(BF16)
    p = _dot(h, wa_ref[...])
    cos, sa, sb = cos_ref[...], sa_ref[...], sb_ref[...]
    bd = bd_ref[...]

    def head_norm(v, g):
        ssq = _split_dot(v * v, bd)
        return v * lax.rsqrt(ssq * (1.0 / HEAD_DIM_A) + EPS) * g

    def rope(v):
        return v * cos + pltpu.roll(v, LANES - 8, 1) * sa + pltpu.roll(v, 8, 1) * sb

    for s in range(WA_Q // LANES):
        v = p[:, s * LANES:(s + 1) * LANES]
        v = rope(head_norm(v, gqa_ref[...])) * (HEAD_DIM_A ** -0.5 * LOG2_E)
        vb = v.astype(BF16)
        qa_ref[2 * s] = vb[:, :HEAD_DIM_A]
        qa_ref[2 * s + 1] = vb[:, HEAD_DIM_A:]
    o = WA_Q
    n_pg = x.shape[0] // LANES
    k = rope(head_norm(p[:, o:o + WA_KV], gka_ref[...]))
    if paged:
        for j in range(n_pg):
            k32_ref[j] = k[j * LANES:(j + 1) * LANES, :].T
    else:
        k32_ref[...] = k
    kb = k.astype(BF16)
    khm_ref[0] = kb[:, :HEAD_DIM_A]
    khm_ref[1] = kb[:, HEAD_DIM_A:]
    o += WA_KV
    if not paged:
        v32_ref[...] = p[:, o:o + WA_KV]
    o += WA_KV
    for s in range(WI_Q // LANES):
        vb = rope(p[:, o + s * LANES:o + (s + 1) * LANES]).astype(BF16)
        qi_ref[2 * s] = vb[:, :D_IDX]
        qi_ref[2 * s + 1] = vb[:, D_IDX:]
    o += WI_Q
    ki_slab = rope(p[:, o:o + LANES])
    ki = ki_slab[:, :D_IDX]
    if paged:
        for j in range(n_pg):
            ki32_ref[j] = ki_slab[j * LANES:(j + 1) * LANES, :].T[:D_IDX, :]
    else:
        ki32_ref[...] = ki
    kibf_ref[...] = ki.astype(BF16)
    wit_ref[...] = _dot_nt(wwit_ref[...], h) * score_scale
    vt = _dot_nt(wvt_ref[...], h)
    if paged:
        for j in range(n_pg):
            v32_ref[j] = vt[:, j * LANES:(j + 1) * LANES]
    tail = jnp.where(lax.broadcasted_iota(I32, (VT_ROWS - HEAD_DIM_A, vt.shape[1]), 0) == 0, 1.0, 0.0)
    for g in range(N_KV_A):
        vt_ref[g] = jnp.concatenate([vt[g * HEAD_DIM_A:(g + 1) * HEAD_DIM_A], tail], axis=0).astype(BF16)


def _front(x2d, tabs, gin, wa, wwit, wvt, bd, gqa, gka, tm, paged):
    n, d = x2d.shape
    cos, sa, sb = tabs
    row = lambda w: pl.BlockSpec((tm, w), lambda i: (i, 0))
    hm = lambda nh, w: pl.BlockSpec((nh, tm, w), lambda i: (0, i, 0))
    if paged:
        assert tm % LANES == 0
        f32_out = lambda w: jax.ShapeDtypeStruct((n // LANES, w, LANES), F32)
        f32_spec = lambda w: pl.BlockSpec((tm // LANES, w, LANES), lambda i: (i, 0, 0))
    else:
        f32_out = lambda w: jax.ShapeDtypeStruct((n, w), F32)
        f32_spec = row
    out_shape = (
        jax.ShapeDtypeStruct((N_HEADS_A, n, HEAD_DIM_A), BF16),
        f32_out(WA_KV),
        jax.ShapeDtypeStruct((N_KV_A, n, HEAD_DIM_A), BF16),
        f32_out(WA_KV),
        jax.ShapeDtypeStruct((N_KV_A, VT_ROWS, n), BF16),
        jax.ShapeDtypeStruct((N_HEADS_IDX, n, D_IDX), BF16),
        f32_out(D_IDX),
        jax.ShapeDtypeStruct((n, D_IDX), BF16),
        jax.ShapeDtypeStruct((N_HEADS_IDX, n), F32),
    )
    out_specs = (hm(N_HEADS_A, HEAD_DIM_A), f32_spec(WA_KV), hm(N_KV_A, HEAD_DIM_A), f32_spec(WA_KV),
                 pl.BlockSpec((N_KV_A, VT_ROWS, tm), lambda i: (0, 0, i)), hm(N_HEADS_IDX, D_IDX), f32_spec(D_IDX),
                 row(D_IDX), pl.BlockSpec((N_HEADS_IDX, tm), lambda i: (0, i)))
    in_specs = [row(d), _const_spec(gin.shape), _const_spec(wa.shape), _const_spec(wwit.shape),
                _const_spec(wvt.shape), _const_spec(bd.shape), _const_spec(gqa.shape), _const_spec(gka.shape),
                row(LANES), row(LANES), row(LANES)]
    kern = functools.partial(_front_kernel, score_scale=D_IDX ** -0.5 * N_HEADS_IDX ** -0.5, paged=paged)
    return pl.pallas_call(kern, out_shape=out_shape, grid=(n // tm,), in_specs=in_specs, out_specs=out_specs,
                          compiler_params=_cparams(), name="front")(x2d, gin, wa, wwit, wvt, bd, gqa, gka, cos, sa, sb)


def _memkv_kernel(mem_ref, gmem_ref, w_ref, gkm_ref, k32_ref, v32_ref):
    h = _rms(mem_ref[...], gmem_ref[...]).astype(BF16)
    kv = _dot(h, w_ref[...])
    for hh in range(N_HEADS_M):
        sl = slice(hh * HEAD_DIM_M, (hh + 1) * HEAD_DIM_M)
        k32_ref[:, sl] = _rms(kv[:, sl], gkm_ref[...])
    v32_ref[...] = kv[:, WM_Q:]


def _memory_kv(mem2d, gmem, w, gkm):
    n = mem2d.shape[0]
    out = jax.ShapeDtypeStruct((n, WM_Q), F32)
    return pl.pallas_call(_memkv_kernel, out_shape=(out, out), name="memory_kv",
                          compiler_params=pltpu.CompilerParams(vmem_limit_bytes=VMEM_LIMIT))(mem2d, gmem, w, gkm)


def _sort_key(s):
    s = jnp.where(s == 0.0, 0.0, s)
    bits = pltpu.bitcast(s, I32)
    return bits ^ ((bits >> 31) & INT_MAX)


KEYS_PER_GROUP = 32 * SUBLANES
SEARCH_SLAB = 64


def _bit_planes(words):
    a = list(words)
    j, m = 16, 0x0000FFFF
    while j:
        k = 0
        while k < 32:
            t = (a[k] ^ lax.shift_right_logical(a[k + j], jnp.int32(j))) & m
            a[k] = a[k] ^ t
            a[k + j] = a[k + j] ^ (t << j)
            k = (k + j + 1) & ~j
        j >>= 1
        m = (m ^ (m << j)) & 0xFFFFFFFF if j else m
    return a


ATTN_SUB = 128


def _attn_kernel(qi_ref, qa_ref, wi_ref, ki_ref, k_ref, vt_ref, out_ref,
                 keys_ref, planes_ref, cand_ref, cut_ref, bias_ref, lg_ref, cm_ref, m_ref, acc_ref, *, k_sel, idx_bits):
    bq, sc = ATTN_BQ, ATTN_SC
    wpc = sc // 32
    i = pl.program_id(0)
    n_ch = (i * bq) // sc + 1
    t_pos = i * bq + lax.broadcasted_iota(I32, (1, bq), 1)
    sub = lax.broadcasted_iota(I32, (sc, 1), 0)
    qi = qi_ref[...].reshape(N_HEADS_IDX * bq, D_IDX)
    w = wi_ref[...]

    @pl.when(i == 0)
    def _():
        planes_ref[...] = jnp.zeros(planes_ref.shape, I32)

    def chunk(c):
        return pl.ds(pl.multiple_of(c * sc, sc), sc)

    def score_chunk(c, carry):
        off = pl.multiple_of(c * sc, sc)
        d = _dot_nt(ki_ref[chunk(c), :], qi)
        s = jnp.zeros((sc, bq), F32)
        for h in range(N_HEADS_IDX):
            s = s + jnp.maximum(d[:, h * bq:(h + 1) * bq], 0.0) * w[h:h + 1, :]
        key = jnp.where(off + sub <= t_pos, _sort_key(s), INT_MIN)
        keys_ref[chunk(c), :] = key
        u = key ^ INT_MIN
        for grp in range(sc // KEYS_PER_GROUP):
            base = grp * KEYS_PER_GROUP
            words = _bit_planes([u[base + SUBLANES * v:base + SUBLANES * (v + 1), :] for v in range(32)])
            rows = pl.ds(pl.multiple_of(c * wpc + grp * SUBLANES, SUBLANES), SUBLANES)
            for p in range(32):
                planes_ref[p, rows, :] = words[p]
        return carry

    lax.fori_loop(0, n_ch // 2, lambda j, cr: score_chunk(2 * j + 1, score_chunk(2 * j, cr)), 0)
    lax.fori_loop(n_ch - n_ch % 2, n_ch, score_chunk, 0)

    def count(pred):
        def body(c, acc):
            hit = jnp.where(pred(keys_ref[chunk(c), :], c * sc + sub), 1, 0)
            return acc + hit.reshape(sc // SUBLANES, SUBLANES, bq).sum(axis=0)

        def body2(j, acc):
            return body(2 * j + 1, body(2 * j, acc))
        acc = lax.fori_loop(0, n_ch // 2, body2, jnp.zeros((SUBLANES, bq), I32))
        acc = lax.fori_loop(n_ch - n_ch % 2, n_ch, body, acc)
        return acc.sum(axis=0, keepdims=True)

    slab = SEARCH_SLAB
    n_slab = (n_ch * wpc + slab - 1) // slab
    srow = lax.broadcasted_iota(I32, (slab, 1), 0)

    def slab_rows(sl):
        return pl.ds(pl.multiple_of(sl * slab, slab), slab)

    def init_cand(sl, carry):
        cand_ref[slab_rows(sl), :] = jnp.where(sl * slab + srow < n_ch * wpc, -1, 0) + jnp.zeros((slab, bq), I32)
        return carry

    lax.fori_loop(0, n_slab, init_cand, 0)

    def sweep(prev, cur):
        def body(sl, acc):
            r = slab_rows(sl)
            cand = cand_ref[r, :]
            if prev is not None:
                ones = cand & planes_ref[prev[0], r, :]
                cand = jnp.where(prev[1] != 0, ones, cand ^ ones)
                cand_ref[r, :] = cand
            hits = cand if cur is None else cand & planes_ref[cur, r, :]
            cnt = lax.population_count(hits)
            return acc + cnt.reshape(slab // SUBLANES, SUBLANES, bq).sum(axis=0)
        acc = lax.fori_loop(0, n_slab, body, jnp.zeros((SUBLANES, bq), I32))
        return acc.sum(axis=0, keepdims=True)

    def decide(ones, p, need, thr_u):
        keep = ones >= need
        bit = jnp.int32(1) << (31 - p)
        return jnp.where(keep, 1, 0), jnp.where(keep, need, need - ones), jnp.where(keep, thr_u | bit, thr_u)

    keep, need, thr_u = decide(sweep(None, 0), 0, jnp.full((1, bq), k_sel, I32), jnp.zeros((1, bq), I32))

    def search_pass(p, st):
        keep, need, thr_u = st
        return decide(sweep((p - 1, keep), p), p, need, thr_u)

    keep, need, thr_u = lax.fori_loop(1, 32, search_pass, (keep, need, thr_u))
    tied = sweep((31, keep), None)
    thr = thr_u ^ INT_MIN
    cnt = (k_sel - need) + tied
    cut_ref[...] = jnp.full((1, bq), INT_MAX, I32)

    excess = jnp.logical_and(cnt > k_sel, thr > INT_MIN)
    has_ties = jnp.max(jnp.where(excess, 1, 0)) > 0

    @pl.when(has_ties)
    def _():
        want = k_sel - count(lambda blk, _: blk > thr)

        def tie_body(j, cpos):
            cand = cpos + (jnp.int32(1) << (idx_bits - 1 - j))
            below = count(lambda blk, pos: jnp.where(blk == thr, pos, INT_MAX) < cand)
            return jnp.where(below < want, cand, cpos)

        cpos = lax.fori_loop(0, idx_bits, tie_body, jnp.zeros((1, bq), I32))
        cut_ref[...] = jnp.where(excess, cpos, INT_MAX)

    m_ref[...] = jnp.full(m_ref.shape, NEG, F32)
    acc_ref[...] = jnp.zeros(acc_ref.shape, F32)
    cut = cut_ref[...]
    floor = jnp.where(thr == INT_MIN, INT_MIN, thr - 1)
    n_sub = sc // ATTN_SUB

    qs = [qa_ref[g * GQA:(g + 1) * GQA].reshape(GQA * bq, HEAD_DIM_A) for g in range(N_KV_A)]
    neg_row = jnp.full((1, GQA * bq), NEG, F32)

    def stage(c, par, cb, par_b):
        do_a, do_b = c is not None, cb is not None
        if do_a:
            off = pl.multiple_of(c * sc, sc)
            keyc = keys_ref[chunk(c), :]

            @pl.when(has_ties)
            def _():
                tie = jnp.where(off + sub <= cut, 0.0, NEG)
                bias = jnp.where(keyc > thr, 0.0, jnp.where(keyc == thr, tie, NEG))
                bias_ref[...] = jnp.where(keyc == INT_MIN, NEG, bias)

            @pl.when(jnp.logical_not(has_ties))
            def _():
                bias_ref[...] = jnp.where(keyc > floor, 0.0, NEG)
            cmax = [neg_row] * N_KV_A
        if do_b:
            off_b = pl.multiple_of(cb * sc, sc)
            m_new, acc = [], []
            for g in range(N_KV_A):
                m_old = m_ref[g]
                m_new.append(jnp.maximum(m_old, cm_ref[par_b, g]))
                acc.append(jnp.exp2(m_old - m_new[g]) * acc_ref[g])
                m_ref[g] = m_new[g]
        for r in range(n_sub):
            rows = slice(r * ATTN_SUB, (r + 1) * ATTN_SUB)
            if do_a:
                bias4 = jnp.concatenate([bias_ref[rows, :]] * GQA, axis=1)
                for g in range(N_KV_A):
                    lg = _dot_nt(k_ref[g, pl.ds(off + r * ATTN_SUB, ATTN_SUB), :], qs[g]) + bias4
                    lg_ref[par, g, rows, :] = lg
                    cmax[g] = jnp.maximum(cmax[g], lg.max(axis=0, keepdims=True))
            if do_b:
                for g in range(N_KV_A):
                    p = jnp.exp2(lg_ref[par_b, g, rows, :] - m_new[g])
                    vt = vt_ref[g, :, pl.ds(off_b + r * ATTN_SUB, ATTN_SUB)]
                    acc[g] = acc[g] + _dot(vt, p.astype(BF16))
        if do_b:
            for g in range(N_KV_A):
                acc_ref[g] = acc[g]
        if do_a:
            for g in range(N_KV_A):
                cm_ref[par, g] = cmax[g]

    stage(jnp.int32(0), 0, None, None)

    def stage_pair(j, carry):
        stage(2 * j + 1, 1, 2 * j, 0)
        stage(2 * j + 2, 0, 2 * j + 1, 1)
        return carry

    last = n_ch - 1
    lax.fori_loop(0, last // 2, stage_pair, 0)

    @pl.when(last % 2 == 1)
    def _():
        stage(last, 1, last - 1, 0)
        stage(None, None, last, 1)

    @pl.when(last % 2 == 0)
    def _():
        stage(None, None, last, 0)

    heads = []
    for g in range(N_KV_A):
        acc = acc_ref[g]
        o = acc[:HEAD_DIM_A] / acc[HEAD_DIM_A:HEAD_DIM_A + 1]
        heads += [o[:, hh * bq:(hh + 1) * bq] for hh in range(GQA)]
    out_ref[...] = jnp.concatenate(heads, axis=0).T.astype(out_ref.dtype)


def _prompt_attention(qi_hm, qa_hm, wit, kibf, khm, vt):
    t = kibf.shape[0]
    bq = ATTN_BQ
    k_sel = min(TOPK_MAX, t // 4)
    kern = functools.partial(_attn_kernel, k_sel=k_sel, idx_bits=max(1, (t - 1).bit_length()))
    in_specs = [pl.BlockSpec((N_HEADS_IDX, bq, D_IDX), lambda i: (0, i, 0)),
                pl.BlockSpec((N_HEADS_A, bq, HEAD_DIM_A), lambda i: (0, i, 0)),
                pl.BlockSpec((N_HEADS_IDX, bq), lambda i: (0, i)),
                _const_spec(kibf.shape, True), _const_spec(khm.shape, True), _const_spec(vt.shape, True)]
    n_words = -(-(t // 32) // SEARCH_SLAB) * SEARCH_SLAB
    scratch = [pltpu.VMEM((t, bq), I32), pltpu.VMEM((32, n_words, bq), I32), pltpu.VMEM((n_words, bq), I32),
               pltpu.VMEM((1, bq), I32), pltpu.VMEM((ATTN_SC, bq), F32),
               pltpu.VMEM((2, N_KV_A, ATTN_SC, GQA * bq), F32), pltpu.VMEM((2, N_KV_A, 1, GQA * bq), F32),
               pltpu.VMEM((N_KV_A, 1, GQA * bq), F32),
               pltpu.VMEM((N_KV_A, VT_ROWS, GQA * bq), F32)]
    return pl.pallas_call(kern, out_shape=jax.ShapeDtypeStruct((t, WA_Q), BF16), grid=(t // bq,),
                          in_specs=in_specs, out_specs=pl.BlockSpec((bq, WA_Q), lambda i: (i, 0)),
                          scratch_shapes=scratch, compiler_params=_cparams(), name="attn")(
                              qi_hm, qa_hm, wit, kibf, khm, vt)


SAMPLE_CK = 2048


def _sample_attn_kernel(pt_ref, qi_ref, qa_ref, wi_ref, kin_ref, kn_ref, vn_ref, cidx_hbm, ck_hbm, cv_hbm, out_ref,
                        idx_buf, k_buf, v_buf, scr_ref, cut_ref, sems, *, n_pages, ps, ck, k_sel, idx_bits):
    b = pl.program_id(0)
    nb = pl.num_programs(0)
    past = n_pages * ps
    nc = past // ck

    def page_copy(which, bb, p):
        src, dst = ((cidx_hbm, idx_buf), (ck_hbm, k_buf), (cv_hbm, v_buf))[which]
        cols = pl.ds(pl.multiple_of(p * ps, ps), ps)
        dst = dst.at[:, cols] if which == 0 else dst.at[:, :, cols]
        return pltpu.make_async_copy(src.at[pt_ref[bb, p]], dst, sems.at[which])

    def for_pages(fn):
        lax.fori_loop(0, n_pages, lambda p, c: (fn(p), c)[1], 0, unroll=8)

    def wait_all(which):
        buf = (idx_buf, k_buf, v_buf)[which]
        pltpu.make_async_copy(buf, buf, sems.at[which]).wait()

    @pl.when(b == 0)
    def _():
        for_pages(lambda p: page_copy(0, b, p).start())

    for_pages(lambda p: (page_copy(1, b, p).start(), page_copy(2, b, p).start()))
    wait_all(0)

    qi = qi_ref[0]
    w = wi_ref[0]
    for c in range(nc):
        d = _dot(qi, idx_buf[:, c * ck:(c + 1) * ck].astype(BF16))
        scr_ref[c:c + 1, :] = (jnp.maximum(d, 0.0) * w).sum(axis=0, keepdims=True)
    d_new = (qi.astype(F32) * kin_ref[0].astype(BF16).astype(F32)).sum(axis=-1, keepdims=True)
    s_new = (jnp.maximum(d_new, 0.0) * w).sum(axis=0, keepdims=True)

    @pl.when(b + 1 < nb)
    def _():
        for_pages(lambda p: page_copy(0, b + 1, p).start())

    keys = _sort_key(scr_ref[...])
    key_new = _sort_key(s_new)
    pos = (lax.broadcasted_iota(I32, keys.shape, 0) * ck + lax.broadcasted_iota(I32, keys.shape, 1))

    def count(pred):
        hit = jnp.where(pred(keys, pos), 1, 0)
        parts = [hit[:, j * LANES:(j + 1) * LANES] for j in range(ck // LANES)]
        while len(parts) > 1:
            parts = [a + b for a, b in zip(parts[::2], parts[1::2])] + parts[len(parts) - len(parts) % 2:]
        total = parts[0].sum(axis=1, keepdims=True).sum(axis=0, keepdims=True)
        return total + jnp.where(pred(key_new, past), 1, 0)

    dcol = lax.broadcasted_iota(I32, (16, 1), 0)

    def search_body(j, st):
        thr, cnt = st
        step = jnp.int32(1) << (28 - 4 * j)
        rows = [jnp.zeros((1, LANES), I32)]
        for d in range(1, 16):
            hit = jnp.where(keys >= thr + d * step, 1, 0)
            parts = [hit[:, t * LANES:(t + 1) * LANES] for t in range(ck // LANES)]
            while len(parts) > 1:
                parts = [a + b for a, b in zip(parts[::2], parts[1::2])] + parts[len(parts) - len(parts) % 2:]
            rows.append(parts[0].sum(axis=0, keepdims=True))
        cands = thr + dcol * step
        cnts = jnp.concatenate(rows, axis=0).sum(axis=1, keepdims=True) + jnp.where(key_new >= cands, 1, 0)
        ok = jnp.logical_and(cnts >= k_sel, dcol >= 1)
        digit = jnp.where(ok, 1, 0).sum(axis=0, keepdims=True)
        best = jnp.where(ok, cnts, INT_MAX).min(axis=0, keepdims=True)
        return thr + digit * step, jnp.where(digit > 0, best, cnt)

    st0 = (jnp.full((1, 1), INT_MIN, I32), jnp.full((1, 1), past + 1, I32))
    thr, cnt = lax.fori_loop(0, 8, search_body, st0)
    cut_ref[...] = jnp.full((1, 1), INT_MAX, I32)

    @pl.when(jnp.max(jnp.where(cnt > k_sel, 1, 0)) > 0)
    def _():
        want = k_sel - count(lambda kk, _: kk > thr)

        def tie_body(j, cpos):
            cand = cpos + (jnp.int32(1) << (idx_bits - 1 - j))
            below = count(lambda kk, pp: jnp.where(kk == thr, pp, INT_MAX) < cand)
            return jnp.where(below < want, cand, cpos)

        cut_ref[...] = lax.fori_loop(0, idx_bits, tie_body, jnp.zeros((1, 1), I32))

    cut = cut_ref[...]

    def sel_bias(kk, pp):
        tie = jnp.where(pp <= cut, 0.0, NEG)
        return jnp.where(kk > thr, 0.0, jnp.where(kk == thr, tie, NEG))

    bias = sel_bias(keys, pos)
    bias_new = sel_bias(key_new, past)

    wait_all(1)
    wait_all(2)

    qa = qa_ref[0]
    heads = []
    for g in range(N_KV_A):
        qg = qa[g * GQA:(g + 1) * GQA]
        cols = slice(g * HEAD_DIM_A, (g + 1) * HEAD_DIM_A)
        spans = [slice(c * ck, (c + 1) * ck) for c in range(nc)]
        lgs = [_dot(qg, k_buf[g, :, sp].astype(BF16)) + bias[c:c + 1, :] for c, sp in enumerate(spans)]
        kn = kn_ref[0][:, cols].astype(BF16).astype(F32)
        vn = vn_ref[0][:, cols].astype(BF16).astype(F32)
        lg_new = (qg.astype(F32) * kn).sum(axis=-1, keepdims=True) + bias_new
        m = lg_new
        for lg in lgs:
            m = jnp.maximum(m, lg.max(axis=-1, keepdims=True))
        p_new = jnp.exp2(lg_new - m)
        l = p_new
        acc = p_new * vn
        for lg, sp in zip(lgs, spans):
            p = jnp.exp2(lg - m)
            l = l + p.sum(axis=-1, keepdims=True)
            acc = acc + _dot_nt(p.astype(BF16), v_buf[g, :, sp].astype(BF16))
        o = acc / l
        heads += [o[hh:hh + 1, :] for hh in range(GQA)]
    out_ref[0] = jnp.concatenate(heads, axis=1)


def _sample_attention(page_table, qi_s, qa_s, wi_s, ki_new, k_new, v_new, cache_idx_k, cache_k, cache_v):
    nbatch, n_pages = page_table.shape
    n_pool, ps, d_idx = cache_idx_k.shape
    past = n_pages * ps
    ck = min(SAMPLE_CK, past)
    assert past % ck == 0 and ck % ps == 0
    k_sel = min(TOPK_MAX, (past + 1) // 4)
    kern = functools.partial(_sample_attn_kernel, n_pages=n_pages, ps=ps, ck=ck, k_sel=k_sel,
                             idx_bits=past.bit_length())
    per_b = lambda shape: pl.BlockSpec((1,) + shape, lambda b, pt: (b, 0, 0))
    any_spec = pl.BlockSpec(memory_space=pl.ANY)
    grid_spec = pltpu.PrefetchScalarGridSpec(
        num_scalar_prefetch=1, grid=(nbatch,),
        in_specs=[per_b((N_HEADS_IDX, d_idx)), per_b((N_HEADS_A, HEAD_DIM_A)), per_b((N_HEADS_IDX, 1)),
                  per_b((1, d_idx)), per_b((1, WA_KV)), per_b((1, WA_KV)), any_spec, any_spec, any_spec],
        out_specs=per_b((1, WA_Q)),
        scratch_shapes=[pltpu.VMEM((d_idx, past), F32), pltpu.VMEM((N_KV_A, HEAD_DIM_A, past), F32),
                        pltpu.VMEM((N_KV_A, HEAD_DIM_A, past), F32),
                        pltpu.VMEM((past // ck, ck), F32), pltpu.VMEM((1, 1), I32),
                        pltpu.SemaphoreType.DMA((3,))])
    return pl.pallas_call(kern, out_shape=jax.ShapeDtypeStruct((nbatch, 1, WA_Q), F32), grid_spec=grid_spec,
                          compiler_params=_cparams(), name="sample_attn")(
                              page_table, qi_s, qa_s, wi_s, ki_new, k_new, v_new,
                              jnp.transpose(cache_idx_k, (0, 2, 1)), jnp.transpose(cache_k, (0, 2, 3, 1)),
                              jnp.transpose(cache_v, (0, 2, 3, 1)))


def _rope_tables(pos):
    rot = HEAD_DIM_A // ROT_FRACTION
    half = rot // 2
    inv_freq = ROPE_THETA ** (-jnp.arange(half, dtype=F32) / half)
    ang = pos.astype(F32)[:, None] * inv_freq[None, :]
    cos, sin = jnp.cos(ang), jnp.sin(ang)
    n = pos.shape[0]
    z = lambda w: jnp.zeros((n, w), F32)
    c = jnp.concatenate([cos, cos, jnp.ones((n, HEAD_DIM_A - rot), F32)], axis=1)
    sa = jnp.concatenate([-sin, z(HEAD_DIM_A - half)], axis=1)
    sb = jnp.concatenate([z(half), sin, z(HEAD_DIM_A - rot)], axis=1)
    rep = LANES // HEAD_DIM_A
    return tuple(jnp.tile(a, (1, rep)) for a in (c, sa, sb))


def _prep_params(w_in, g_in, g_qa, g_ka):
    d = w_in.shape[0]
    w_qa, w_ka, w_va, w_qi, w_ki, w_wi = _split_w_in(w_in, d)[:6]
    wa = jnp.concatenate([w_qa, w_ka, w_va, w_qi, w_ki, jnp.zeros((d, LANES - D_IDX), w_in.dtype)], axis=1)
    lane = jnp.arange(LANES)
    bd = (lane[:, None] // HEAD_DIM_A == lane[None, :] // HEAD_DIM_A).astype(BF16)
    rep = LANES // HEAD_DIM_A
    return (g_in.reshape(1, d), wa.astype(BF16), w_wi.T.astype(BF16), w_va.T.astype(BF16), bd,
            jnp.tile(g_qa, rep).reshape(1, LANES), jnp.tile(g_ka, rep).reshape(1, LANES))


def _split_w_in(w_in, d):
    sizes = (WA_Q, WA_KV, WA_KV, WI_Q, D_IDX, N_HEADS_IDX, WIDTH_B, WIDTH_B, WM_Q, N_BRANCH * d)
    parts, o = [], 0
    for s in sizes:
        parts.append(w_in[:, o:o + s])
        o += s
    return parts


def _mix_kernel(*refs, chunked, shared_mem, emit_vb, n_experts):
    it = iter(refs)
    x_ref, oa_ref, gin_ref, wb_ref, gvb_ref, gqm_ref = (next(it) for _ in range(6))
    if chunked:
        wtril_ref, bt_ref = next(it), next(it)
    else:
        wdiag_ref, bdiag_ref = next(it), next(it)
    mk_ref, mv_ref = next(it), next(it)
    wbra_ref, wbrb_ref, wbrm_ref, wout_ref, gffn_ref, wrh_ref, wrl_ref, br_ref = (next(it) for _ in range(8))
    x1_ref, h2_ref, te_ref, tw_ref = (next(it) for _ in range(4))
    vb_ref = next(it) if emit_vb else None

    x = x_ref[...]
    tm, d = x.shape
    h = _rms(x, gin_ref[...]).astype(BF16)
    p = _dot(h, wb_ref[...])
    ub = p[:, :WIDTH_B]
    vb = _rms(p[:, WIDTH_B:2 * WIDTH_B], gvb_ref[...])
    if emit_vb:
        vb_ref[...] = vb
    o = 2 * WIDTH_B
    qm = p[:, o:o + WM_Q]
    o += WM_Q
    gates = p[:, o:o + N_BRANCH * d]

    if chunked:
        vbb = vb.astype(BF16)
        rows = []
        for cc in range(tm // CHUNK):
            cols = []
            for g in range(N_GROUPS_B):
                vg = vbb[cc * CHUNK:(cc + 1) * CHUNK, g * GROUP_DIM_B:(g + 1) * GROUP_DIM_B]
                cols.append(_dot(wtril_ref[g], vg))
            rows.append(jnp.concatenate(cols, axis=1) + bt_ref[...])
        z = jnp.concatenate(rows, axis=0) if len(rows) > 1 else rows[0]
    else:
        z = vb * wdiag_ref[...] + bdiag_ref[...]
    out_b = ub * z

    scale_m = HEAD_DIM_M ** -0.5
    outs = []
    for hh in range(N_HEADS_M):
        sl = slice(hh * HEAD_DIM_M, (hh + 1) * HEAD_DIM_M)
        qh = _rms(qm[:, sl], gqm_ref[...])
        if shared_mem:
            lg = _dot_nt(qh.astype(BF16), mk_ref[:, sl].astype(BF16)) * scale_m
            pm = jnp.exp(lg - lg.max(axis=-1, keepdims=True))
            pm = pm / pm.sum(axis=-1, keepdims=True)
            outs.append(_dot(pm.astype(BF16), mv_ref[:, sl].astype(BF16)))
        else:
            per_row = []
            for r in range(tm):
                kr = mk_ref[r, :, sl]
                lg = (kr * qh[r:r + 1, :]).sum(axis=-1, keepdims=True) * scale_m
                pm = jnp.exp(lg - lg.max(axis=0, keepdims=True))
                pm = pm / pm.sum(axis=0, keepdims=True)
                per_row.append((pm * mv_ref[r, :, sl]).sum(axis=0, keepdims=True))
            outs.append(jnp.concatenate(per_row, axis=0))
    out_m = jnp.concatenate(outs, axis=1)

    sig = jax.nn.sigmoid
    merged = (sig(gates[:, :d]) * _dot(oa_ref[...].astype(BF16), wbra_ref[...])
              + sig(gates[:, d:2 * d]) * _dot(out_b.astype(BF16), wbrb_ref[...])
              + sig(gates[:, 2 * d:]) * _dot(out_m.astype(BF16), wbrm_ref[...]))
    x1 = x + _dot(merged.astype(BF16), wout_ref[...])
    x1_ref[...] = x1
    h2 = _rms(x1, gffn_ref[...])
    h2_ref[...] = h2.astype(h2_ref.dtype)

    hi = h2.astype(BF16)
    lo = (h2 - hi.astype(F32)).astype(BF16)
    lg = _dot(hi, wrh_ref[...]) + _dot(hi, wrl_ref[...]) + _dot(lo, wrh_ref[...]) + br_ref[...]
    lane = lax.broadcasted_iota(I32, lg.shape, 1)
    lg = jnp.where(lane < n_experts, lg, -jnp.inf)
    vals, idxs = [], []
    for _ in range(TOP_K_MOE):
        m = lg.max(axis=-1, keepdims=True)
        idx = jnp.where(lg == m, lane, LANES).min(axis=-1, keepdims=True)
        vals.append(m)
        idxs.append(idx)
        lg = jnp.where(lane == idx, -jnp.inf, lg)
    ex = [jnp.exp(v - vals[0]) for v in vals]
    den = ex[0]
    for e in ex[1:]:
        den = den + e
    te = jnp.full(lane.shape, -1, I32)
    tw = jnp.zeros(lane.shape, F32)
    for j in range(TOP_K_MOE):
        te = jnp.where(lane == j, idxs[j], te)
        tw = jnp.where(lane == j, ex[j] / den, tw)
    te_ref[...] = te
    tw_ref[...] = tw


def _mix(x2d, out_a, mix_params, gmlp_params, mem_k, mem_v, *, tm, chunked, shared_mem, emit_vb, n_experts):
    n, d = x2d.shape
    gin, wb, gvb, gqm, wbra, wbrb, wbrm, wout, gffn, wrh, wrl, br = mix_params
    row = lambda w: pl.BlockSpec((tm, w), lambda i: (i, 0))
    if shared_mem:
        mem_specs = [_const_spec(mem_k.shape), _const_spec(mem_v.shape)]
    else:
        mspec = pl.BlockSpec((tm,) + mem_k.shape[1:], lambda i: (i, 0, 0))
        mem_specs = [mspec, mspec]
    consts = lambda arrs: [_const_spec(a.shape) for a in arrs]
    in_specs = ([row(d), row(WA_Q)] + consts([gin, wb, gvb, gqm]) + consts(gmlp_params) + mem_specs
                + consts([wbra, wbrb, wbrm, wout, gffn, wrh, wrl, br]))
    out_shape = [jax.ShapeDtypeStruct((n, d), F32), jax.ShapeDtypeStruct((n, d), F32),
                 jax.ShapeDtypeStruct((n, LANES), I32), jax.ShapeDtypeStruct((n, LANES), F32)]
    out_specs = [row(d), row(d), row(LANES), row(LANES)]
    if emit_vb:
        out_shape.append(jax.ShapeDtypeStruct((n, WIDTH_B), F32))
        out_specs.append(row(WIDTH_B))
    kern = functools.partial(_mix_kernel, chunked=chunked, shared_mem=shared_mem, emit_vb=emit_vb,
                             n_experts=n_experts)
    return pl.pallas_call(kern, out_shape=tuple(out_shape), grid=(n // tm,), in_specs=in_specs,
                          out_specs=tuple(out_specs), compiler_params=_cparams(), name="mix")(
                              x2d, out_a, gin, wb, gvb, gqm, *gmlp_params, mem_k, mem_v,
                              wbra, wbrb, wbrm, wout, gffn, wrh, wrl, br)


def _prep_mix_params(w_in, g_in, g_vb, g_qm, w_br_a, w_br_b, w_br_m, w_out, g_ffn, w_router, b_router):
    d = w_in.shape[0]
    parts = _split_w_in(w_in, d)
    wb = jnp.concatenate(parts[6:10], axis=1).astype(BF16)
    n_e = w_router.shape[1]
    wr = jnp.pad(w_router, ((0, 0), (0, LANES - n_e)))
    wrh = wr.astype(BF16)
    wrl = (wr - wrh.astype(F32)).astype(BF16)
    br = jnp.pad(b_router, (0, LANES - n_e)).reshape(1, LANES)
    return (g_in.reshape(1, d), wb, g_vb.reshape(1, WIDTH_B), g_qm.reshape(1, HEAD_DIM_M),
            w_br_a.astype(BF16), w_br_b.astype(BF16), w_br_m.astype(BF16), w_out.astype(BF16),
            g_ffn.reshape(1, d), wrh, wrl, br)


def _prep_gmlp_chunked(w_s, b_s):
    tril = jnp.tril(jnp.ones((CHUNK, CHUNK), w_s.dtype))
    wtril = (w_s * tril).astype(BF16)
    bt = jnp.repeat(b_s.T, GROUP_DIM_B, axis=1)
    return wtril, bt


def _prep_gmlp_single(w_s, b_s):
    wdiag = jnp.repeat(w_s[:, 0, 0], GROUP_DIM_B).reshape(1, WIDTH_B)
    bdiag = jnp.repeat(b_s[:, 0], GROUP_DIM_B).reshape(1, WIDTH_B)
    return wdiag, bdiag


MOE_TP = 256


def _plan_kernel(te_ref, tri_ref, upper_ref, lslot_ref, cnt_ref, tstart_ref, gprev_ref, total_ref, carry_ref):
    @pl.when(pl.program_id(0) == 0)
    def _():
        carry_ref[...] = jnp.zeros(carry_ref.shape, F32)

    te = te_ref[...]
    lane = lax.broadcasted_iota(I32, te.shape, 1)
    picks = [te[:, j:j + 1] for j in range(TOP_K_MOE)]
    onehot = jnp.zeros(te.shape, F32)
    for e in picks:
        onehot = onehot + jnp.where(lane == e, 1.0, 0.0)
    before = _dot(tri_ref[...], onehot.astype(BF16))
    cnt = onehot.sum(axis=0, keepdims=True)
    cnt = jnp.floor((cnt + (SUBLANES - 1)) * (1.0 / SUBLANES)) * SUBLANES
    cnt8 = jnp.broadcast_to(cnt, (SUBLANES, LANES)).astype(BF16)
    tstart = _dot(cnt8, upper_ref[...])[0:1, :]
    lslot = jnp.zeros(te.shape, I32)
    for j, e in enumerate(picks):
        r = jnp.where(lane == e, before + tstart, 0.0).sum(axis=-1, keepdims=True).astype(I32)
        lslot = jnp.where(lane == j, r, lslot)
    lslot_ref[...] = lslot
    cnt_ref[0] = cnt.astype(I32)
    tstart_ref[0] = tstart.astype(I32)
    gprev_ref[0] = carry_ref[...].astype(I32)
    carry_ref[...] = carry_ref[...] + cnt
    total_ref[...] = carry_ref[...].astype(I32)


def _moe_plan(te_all):
    n = te_all.shape[0]
    tp = MOE_TP
    n_tiles = n // tp
    r = jnp.arange(tp)
    tri = (r[:, None] > r[None, :]).astype(BF16)
    e = jnp.arange(LANES)
    upper = (e[:, None] < e[None, :]).astype(BF16)
    per_tile = jax.ShapeDtypeStruct((n_tiles, 1, LANES), I32)
    tile_spec = pl.BlockSpec((1, 1, LANES), lambda i: (i, 0, 0))
    return pl.pallas_call(
        _plan_kernel,
        out_shape=(jax.ShapeDtypeStruct((n, LANES), I32), per_tile, per_tile, per_tile,
                   jax.ShapeDtypeStruct((1, LANES), I32)),
        grid=(n_tiles,),
        in_specs=[pl.BlockSpec((tp, LANES), lambda i: (i, 0)), _const_spec((tp, tp)), _const_spec((LANES, LANES))],
        out_specs=(pl.BlockSpec((tp, LANES), lambda i: (i, 0)), tile_spec, tile_spec, tile_spec,
                   _const_spec((1, LANES))),
        scratch_shapes=[pltpu.VMEM((1, LANES), F32)],
        compiler_params=_cparams(), name="moe_plan")(te_all, tri, upper)


def _run_pieces(tp):
    sizes, s = [], SUBLANES
    while s <= tp:
        sizes.append(s)
        s *= 2
    return tuple(reversed(sizes))


def _local_rows(tp, n_e):
    return tp * TOP_K_MOE + n_e * SUBLANES


def _for_runs(tabs, tile, n_e, tp, fn):
    cnt_s, tstart_s, gbase_s = tabs

    def per_expert(e, c):
        idx = tile * n_e + e
        length, src0, dst0 = cnt_s[idx], tstart_s[idx], gbase_s[idx]
        off = jnp.int32(0)
        for size in _run_pieces(tp):
            piece = length & size

            @pl.when(piece != 0)
            def _():
                fn(pl.multiple_of(src0 + off, SUBLANES), pl.multiple_of(dst0 + off, SUBLANES), size)
            off = off + piece
        return c

    lax.fori_loop(0, n_e, per_expert, 0, unroll=4)


def _wait_runs(tabs, tile, n_e, n_rows, copy_of):
    cnt_s, tstart_s, _ = tabs
    last = tile * n_e + n_e - 1
    total = tstart_s[last] + cnt_s[last]
    for size in _run_pieces(n_rows):
        @pl.when((total & size) != 0)
        def _():
            copy_of(size).wait()


def _slot_matrix(lslot, vals, n_rows):
    col = lax.broadcasted_iota(I32, (lslot.shape[0], n_rows), 1)
    m = jnp.zeros(col.shape, F32)
    for j in range(TOP_K_MOE):
        v = 1.0 if vals is None else vals[:, j:j + 1]
        m = m + jnp.where(lslot[:, j:j + 1] == col, v, 0.0)
    return m


def _dispatch_kernel(cnt_s, tstart_s, gbase_s, fstart_s, flen_s, nused_s, lslot_ref, h2_ref, *rest,
                     tp, n_e, tile0, first, bm):
    if first:
        xs_out, stage, sem, zbuf, zsem = rest
    else:
        _, xs_out, stage, sem = rest
    n_rows = _local_rows(tp, n_e)
    i = pl.program_id(0)
    slot = i % 2
    pt = _slot_matrix(lslot_ref[...], None, n_rows).astype(BF16)
    stage[slot] = lax.dot_general(pt, h2_ref[...].astype(BF16), (((0,), (0,)), ((), ())),
                                  preferred_element_type=F32)

    def runs(tile, s, op):
        _for_runs((cnt_s, tstart_s, gbase_s), tile0 + tile, n_e, tp,
                  lambda src, dst, size: op(pltpu.make_async_copy(
                      stage.at[s, pl.ds(src, size), :], xs_out.at[pl.ds(dst, size), :], sem.at[s])))

    def drain(tile, s):
        _wait_runs((cnt_s, tstart_s, gbase_s), tile0 + tile, n_e, n_rows, lambda size: pltpu.make_async_copy(
            stage.at[s, pl.ds(0, size), :], xs_out.at[pl.ds(0, size), :], sem.at[s]))

    @pl.when(i > 0)
    def _():
        drain(i - 1, 1 - slot)

    runs(i, slot, lambda cp: cp.start())

    @pl.when(i == pl.num_programs(0) - 1)
    def _():
        drain(i, slot)
        if first:
            zbuf[...] = jnp.zeros(zbuf.shape, F32)
            nb = xs_out.shape[0] // bm

            def fill(op):
                def per_expert(e, c):
                    length, dst0 = flen_s[e], fstart_s[e]
                    off = jnp.int32(0)
                    for size in _run_pieces(bm):
                        piece = length & size

                        @pl.when(piece != 0)
                        def _():
                            op(pltpu.make_async_copy(
                                zbuf.at[pl.ds(0, size), :],
                                xs_out.at[pl.ds(pl.multiple_of(dst0 + off, SUBLANES), size), :], zsem))
                        off = off + piece
                    return c

                lax.fori_loop(0, n_e, per_expert, 0)
                lax.fori_loop(nused_s[0], nb, lambda b, c: (op(pltpu.make_async_copy(
                    zbuf, xs_out.at[pl.ds(pl.multiple_of(b * bm, bm), bm), :], zsem)), c)[1], 0)

            fill(lambda cp: cp.start())
            fill(lambda cp: cp.wait())


def _dispatch(tabs, fill_tabs, tile0, lslot, h2, xs, n_e, ns, bm):
    n, d = h2.shape
    tp = min(MOE_TP, n)
    n_rows = _local_rows(tp, n_e)
    first = xs is None
    kern = functools.partial(_dispatch_kernel, tp=tp, n_e=n_e, tile0=tile0, first=first, bm=bm)
    any_spec = pl.BlockSpec(memory_space=pl.ANY)
    scratch = [pltpu.VMEM((2, n_rows, d), F32), pltpu.SemaphoreType.DMA((2,))]
    if first:
        scratch += [pltpu.VMEM((bm, d), F32), pltpu.SemaphoreType.DMA]
    grid_spec = pltpu.PrefetchScalarGridSpec(
        num_scalar_prefetch=6, grid=(n // tp,),
        in_specs=[pl.BlockSpec((tp, LANES), lambda i, *_: (i, 0)), pl.BlockSpec((tp, d), lambda i, *_: (i, 0))]
        + ([] if first else [any_spec]),
        out_specs=any_spec, scratch_shapes=scratch)
    args = (*tabs, *fill_tabs, lslot, h2) + (() if first else (xs,))
    return pl.pallas_call(
        kern, out_shape=jax.ShapeDtypeStruct((ns, d), F32), grid_spec=grid_spec,
        input_output_aliases={} if first else {8: 0},
        compiler_params=pltpu.CompilerParams(dimension_semantics=("arbitrary",), has_side_effects=True,
                                             vmem_limit_bytes=VMEM_LIMIT),
        name="dispatch")(*args)


def _expert_kernel(be_ref, nu_ref, xs_ref, wg_ref, wu_ref, wd_ref, bg_ref, bu_ref, bd_ref, y_ref,
                   wg_bf, wu_bf, wd_bf):
    b = pl.program_id(0)

    @pl.when(b >= nu_ref[0])
    def _():
        y_ref[...] = jnp.zeros(y_ref.shape, F32)

    @pl.when(b < nu_ref[0])
    def _():
        prev = be_ref[jnp.maximum(b - 1, 0)]

        @pl.when(jnp.logical_or(b == 0, be_ref[b] != prev))
        def _():
            wg_bf[...] = wg_ref[0].astype(BF16)
            wu_bf[...] = wu_ref[0].astype(BF16)
            wd_bf[...] = wd_ref[0].astype(BF16)

        xb = xs_ref[...].astype(BF16)
        ff = wg_bf.shape[1]
        y = jnp.zeros(y_ref.shape, F32) + bd_ref[0]
        for n in range(ff // FF_CHUNK):
            sl = slice(n * FF_CHUNK, (n + 1) * FF_CHUNK)
            hg = jnp.minimum(_dot(xb, wg_bf[:, sl]) + bg_ref[0][:, sl], SWIGLU_LIMIT)
            hu = jnp.clip(_dot(xb, wu_bf[:, sl]) + bu_ref[0][:, sl], -SWIGLU_LIMIT, SWIGLU_LIMIT)
            act = hg * jax.nn.sigmoid(SWIGLU_ALPHA * hg) * (hu + 1.0)
            y = y + _dot(act.astype(BF16), wd_bf[sl, :])
        y_ref[...] = y


def _experts(xs, block_e, n_used, w_gate, b_gate, w_up, b_up, w_down, b_down):
    ns, d = xs.shape
    n_e, _, ff = w_gate.shape
    bm = MOE_BM
    blk = lambda b, be, nu: (jnp.minimum(b, nu[0] - 1), 0)
    wsel = lambda b, be, nu: (be[b], 0, 0)
    grid_spec = pltpu.PrefetchScalarGridSpec(
        num_scalar_prefetch=2, grid=(ns // bm,),
        in_specs=[pl.BlockSpec((bm, d), blk),
                  pl.BlockSpec((1, d, ff), wsel), pl.BlockSpec((1, d, ff), wsel), pl.BlockSpec((1, ff, d), wsel),
                  pl.BlockSpec((1, 1, ff), wsel), pl.BlockSpec((1, 1, ff), wsel), pl.BlockSpec((1, 1, d), wsel)],
        out_specs=pl.BlockSpec((bm, d), lambda b, be, nu: (b, 0)),
        scratch_shapes=[pltpu.VMEM((d, ff), BF16), pltpu.VMEM((d, ff), BF16), pltpu.VMEM((ff, d), BF16)])
    return pl.pallas_call(_expert_kernel, out_shape=jax.ShapeDtypeStruct((ns, d), F32), grid_spec=grid_spec,
                          compiler_params=_cparams(), name="experts")(
                              block_e, n_used, xs, w_gate, w_up, w_down,
                              b_gate.reshape(n_e, 1, ff), b_up.reshape(n_e, 1, ff), b_down.reshape(n_e, 1, d))


def _combine_kernel(cnt_s, tstart_s, gbase_s, lslot_ref, tw_ref, x1_ref, yb_hbm, y_ref, buf, sem, *, tp, n_e, tile0):
    n_rows = _local_rows(tp, n_e)
    i = pl.program_id(0)
    slot = i % 2

    def runs(tile, s, op):
        _for_runs((cnt_s, tstart_s, gbase_s), tile0 + tile, n_e, tp,
                  lambda loc, glob, size: op(pltpu.make_async_copy(
                      yb_hbm.at[pl.ds(glob, size), :], buf.at[s, pl.ds(loc, size), :], sem.at[s])))

    @pl.when(i == 0)
    def _():
        buf[...] = jnp.zeros(buf.shape, F32)
        runs(i, slot, lambda cp: cp.start())

    @pl.when(i + 1 < pl.num_programs(0))
    def _():
        runs(i + 1, 1 - slot, lambda cp: cp.start())

    _wait_runs((cnt_s, tstart_s, gbase_s), tile0 + i, n_e, n_rows, lambda size: pltpu.make_async_copy(
        yb_hbm.at[pl.ds(0, size), :], buf.at[slot, pl.ds(0, size), :], sem.at[slot]))

    ptw = _slot_matrix(lslot_ref[...], tw_ref[...], n_rows)
    rows = buf[slot]
    w_hi = ptw.astype(BF16)
    w_lo = (ptw - w_hi.astype(F32)).astype(BF16)
    r_hi = rows.astype(BF16)
    r_lo = (rows - r_hi.astype(F32)).astype(BF16)
    y_ref[...] = x1_ref[...] + _dot(w_hi, r_hi) + (_dot(w_hi, r_lo) + _dot(w_lo, r_hi))


def _combine(tabs, tile0, lslot, tw, x1, yb, n_e):
    n, d = x1.shape
    tp = min(MOE_TP, n)
    n_rows = _local_rows(tp, n_e)
    kern = functools.partial(_combine_kernel, tp=tp, n_e=n_e, tile0=tile0)
    row = lambda w: pl.BlockSpec((tp, w), lambda i, *_: (i, 0))
    grid_spec = pltpu.PrefetchScalarGridSpec(
        num_scalar_prefetch=3, grid=(n // tp,),
        in_specs=[row(LANES), row(LANES), row(d), pl.BlockSpec(memory_space=pl.ANY)],
        out_specs=row(d),
        scratch_shapes=[pltpu.VMEM((2, n_rows, d), F32), pltpu.SemaphoreType.DMA((2,))])
    return pl.pallas_call(kern, out_shape=jax.ShapeDtypeStruct(x1.shape, F32), grid_spec=grid_spec,
                          compiler_params=_cparams(), name="combine")(*tabs, lslot, tw, x1, yb)


def _moe(groups, w_gate, b_gate, w_up, b_up, w_down, b_down):
    n_e = w_gate.shape[0]
    d = groups[0][0].shape[1]
    tp, bm = MOE_TP, MOE_BM
    sizes = [g[0].shape[0] for g in groups]
    assert all(n % tp == 0 or n < tp for n in sizes), sizes
    parts, tile0s, o = [], [], 0
    for g, n in zip(groups, sizes):
        n_pad = -(-n // tp) * tp
        parts.append(jnp.pad(g[2], ((0, n_pad - n), (0, 0)), constant_values=-1))
        tile0s.append(o // tp)
        o += n_pad
    lslot, cnt3, tstart3, gprev3, total = _moe_plan(jnp.concatenate(parts, axis=0))
    counts = total[0, :n_e]
    padded_cnt = (counts + bm - 1) // bm * bm
    pad_end = jnp.cumsum(padded_cnt)
    pad_start = pad_end - padded_cnt
    n_assign = sum(sizes) * TOP_K_MOE
    n_slack = (o // tp) * n_e * (SUBLANES - 1)
    nb = -(-(n_assign + n_slack) // bm) + n_e
    n_used = (pad_end[-1] // bm).astype(I32)
    blocks = jnp.arange(nb, dtype=I32)
    block_e = jnp.minimum((pad_end[None, :] <= (blocks * bm)[:, None]).sum(axis=1), n_e - 1).astype(I32)
    block_e = jnp.where(blocks < n_used, block_e, block_e[jnp.maximum(n_used - 1, 0)])
    flat = lambda a: a[:, 0, :n_e].reshape(-1).astype(I32)
    gbase3 = pad_start[None, None, :] + gprev3[:, :, :n_e]
    tabs = (flat(cnt3), flat(tstart3), flat(gbase3))

    after_first = gprev3[tile0s[1], 0, :n_e] if len(groups) > 1 else counts
    fill_start = (pad_start + after_first).astype(I32)
    fill_tabs = (fill_start, (pad_end - fill_start).astype(I32), n_used.reshape(1))
    xs, lslots = None, []
    for (x1, h2, te, tw), n, t0 in zip(groups, sizes, tile0s):
        ls = lslot[t0 * tp:t0 * tp + n]
        lslots.append(ls)
        xs = _dispatch(tabs, fill_tabs, t0, ls, h2, xs, n_e, nb * bm, bm)
    yb = _experts(xs, block_e, n_used.reshape(1), w_gate, b_gate, w_up, b_up, w_down, b_down)
    return [_combine(tabs, t0, ls, tw, x1, yb, n_e)
            for (x1, h2, te, tw), ls, t0 in zip(groups, lslots, tile0s)]


def _row_tile(n, want):
    return want if n % want == 0 else n


def kernel(x_prompt, x_sample, mem_prompt, cache_k, cache_v, cache_idx_k, cache_mem_k, cache_mem_v, page_table,
           g_in, w_in, g_qa, g_ka, g_vb, w_s, b_s, g_qm, g_mem, w_mem_kv, g_km, w_br_a, w_br_b, w_br_m, w_out,
           g_ffn, w_router, b_router, w_gate, b_gate, w_up, b_up, w_down, b_down):
    bp, t, d = x_prompt.shape
    bs, ts, _ = x_sample.shape
    assert bp == 1 and ts == 1 and t % CHUNK == 0 and cache_k.shape[1] == LANES
    n_mem = mem_prompt.shape[1]
    n_pages = page_table.shape[1]
    ps = cache_k.shape[1]
    past = n_pages * ps
    n_e = w_router.shape[1]

    fp = _prep_params(w_in, g_in, g_qa, g_ka)
    mp = _prep_mix_params(w_in, g_in, g_vb, g_qm, w_br_a, w_br_b, w_br_m, w_out, g_ffn, w_router, b_router)

    xp = x_prompt.reshape(t, d)
    qa_p, k32_p, khm_p, v32_p, vt_p, qi_p, ki32_p, kibf_p, wit_p = _front(
        xp, _rope_tables(jnp.arange(t, dtype=I32)), *fp, tm=_row_tile(t, 512), paged=True)
    mem_k, mem_v = _memory_kv(mem_prompt.reshape(n_mem, d), g_mem.reshape(1, d), w_mem_kv.astype(BF16),
                              g_km.reshape(1, HEAD_DIM_M))
    out_a_p = _prompt_attention(qi_p, qa_p, wit_p, kibf_p, khm_p, vt_p)
    x1_p, h2_p, te_p, tw_p = _mix(xp, out_a_p, mp, _prep_gmlp_chunked(w_s, b_s), mem_k, mem_v,
                                  tm=_row_tile(t, 256), chunked=True, shared_mem=True, emit_vb=False, n_experts=n_e)

    xs = x_sample.reshape(bs, d)
    qa_s, k32_s, _, v32_s, _, qi_s, ki32_s, _, wit_s = _front(
        xs, _rope_tables(jnp.full((bs,), past, I32)), *fp, tm=bs, paged=False)
    out_a_s = _sample_attention(page_table, jnp.moveaxis(qi_s, 0, 1), jnp.moveaxis(qa_s, 0, 1),
                                wit_s.T.reshape(bs, N_HEADS_IDX, 1), ki32_s.reshape(bs, 1, D_IDX),
                                k32_s.reshape(bs, 1, WA_KV), v32_s.reshape(bs, 1, WA_KV),
                                cache_idx_k, cache_k, cache_v)
    x1_s, h2_s, te_s, tw_s, vb_s = _mix(xs, out_a_s.reshape(bs, WA_Q), mp, _prep_gmlp_single(w_s, b_s),
                                        cache_mem_k.reshape(bs, n_mem, WM_Q), cache_mem_v.reshape(bs, n_mem, WM_Q),
                                        tm=_row_tile(bs, SUBLANES), chunked=False, shared_mem=False, emit_vb=True,
                                        n_experts=n_e)

    y_p, y_s = _moe([(x1_p, h2_p, te_p, tw_p), (x1_s, h2_s, te_s, tw_s)],
                    w_gate, b_gate, w_up, b_up, w_down, b_down)

    n_pg = t // ps
    paged_kv = lambda a: jnp.transpose(a.reshape(1, n_pg, N_KV_A, HEAD_DIM_A, ps), (0, 1, 4, 2, 3))
    return (y_p.reshape(1, t, d), y_s.reshape(bs, 1, d),
            paged_kv(k32_p), paged_kv(v32_p), jnp.transpose(ki32_p, (0, 2, 1)).reshape(1, n_pg, ps, D_IDX),
            mem_k.reshape(1, n_mem, N_HEADS_M, HEAD_DIM_M), mem_v.reshape(1, n_mem, N_HEADS_M, HEAD_DIM_M),
            k32_s.reshape(bs, 1, N_KV_A, HEAD_DIM_A), v32_s.reshape(bs, 1, N_KV_A, HEAD_DIM_A),
            ki32_s.reshape(bs, 1, D_IDX), vb_s.reshape(bs, 1, WIDTH_B))
```

```python
import functools

import jax
import jax.numpy as jnp
from jax import lax
from jax.experimental import pallas as pl
from jax.experimental.pallas import tpu as pltpu

F32 = jnp.float32
BF16 = jnp.bfloat16
I32 = jnp.int32

N_HEADS_A = 8
N_KV_A = 2
HEAD_DIM_A = 64
TOPK_MAX = 256
N_HEADS_IDX = 8
D_IDX = 64
N_GROUPS_B = 4
GROUP_DIM_B = 128
WIDTH_B = N_GROUPS_B * GROUP_DIM_B
CHUNK = 128
N_HEADS_M = 4
HEAD_DIM_M = 128
ROPE_THETA = 500000.0
ROT_FRACTION = 4
N_BRANCH = 3
TOP_K_MOE = 4
SWIGLU_LIMIT = 7.0
SWIGLU_ALPHA = 1.702
EPS = 1e-6

WA_Q = N_HEADS_A * HEAD_DIM_A
WA_KV = N_KV_A * HEAD_DIM_A
WI_Q = N_HEADS_IDX * D_IDX
WM_Q = N_HEADS_M * HEAD_DIM_M
GQA = N_HEADS_A // N_KV_A

LANES = 128
SUBLANES = 8
VMEM_LIMIT = 56 * 1024 * 1024

LOG2_E = 1.4426950408889634
INT_MIN = -(2 ** 31)
INT_MAX = 2 ** 31 - 1
NEG = -1e30

VT_ROWS = HEAD_DIM_A + 16
ATTN_BQ = 128
ATTN_SC = 512
MOE_BM = 512
FF_CHUNK = 512


def _cparams(n_axes=1, vmem=VMEM_LIMIT):
    return pltpu.CompilerParams(dimension_semantics=("arbitrary",) * n_axes, vmem_limit_bytes=vmem)


def _const_spec(shape, single=False):
    zeros = (0,) * len(shape)
    if single:
        return pl.BlockSpec(shape, lambda *_: zeros, pipeline_mode=pl.Buffered(1))
    return pl.BlockSpec(shape, lambda *_: zeros)


def _dot(a, b):
    return jnp.dot(a, b, preferred_element_type=F32)


def _dot_nt(a, b):
    return lax.dot_general(a, b, (((1,), (1,)), ((), ())), preferred_element_type=F32)


def _rms(x, g):
    return x * lax.rsqrt(jnp.mean(x * x, axis=-1, keepdims=True) + EPS) * g


def _split_dot(x, w):
    hi = x.astype(BF16)
    lo = (x - hi.astype(F32)).astype(BF16)
    return _dot(hi, w) + _dot(lo, w)


def _front_kernel(x_ref, gin_ref, wa_ref, wwit_ref, wvt_ref, bd_ref, gqa_ref, gka_ref, cos_ref, sa_ref, sb_ref,
                  qa_ref, k32_ref, khm_ref, v32_ref, vt_ref, qi_ref, ki32_ref, kibf_ref, wit_ref, *, score_scale, paged):
    x = x_ref[...]
    h = _rms(x, gin_ref[...]).astype(BF16)
    p = _dot(h, wa_ref[...])
    cos, sa, sb = cos_ref[...], sa_ref[...], sb_ref[...]
    bd = bd_ref[...]

    def head_norm(v, g):
        ssq = _split_dot(v * v, bd)
        return v * lax.rsqrt(ssq * (1.0 / HEAD_DIM_A) + EPS) * g

    def rope(v):
        return v * cos + pltpu.roll(v, LANES - 8, 1) * sa + pltpu.roll(v, 8, 1) * sb

    for s in range(WA_Q // LANES):
        v = p[:, s * LANES:(s + 1) * LANES]
        v = rope(head_norm(v, gqa_ref[...])) * (HEAD_DIM_A ** -0.5 * LOG2_E)
        vb = v.astype(BF16)
        qa_ref[2 * s] = vb[:, :HEAD_DIM_A]
        qa_ref[2 * s + 1] = vb[:, HEAD_DIM_A:]
    o = WA_Q
    n_pg = x.shape[0] // LANES
    k = rope(head_norm(p[:, o:o + WA_KV], gka_ref[...]))
    if paged:
        for j in range(n_pg):
            k32_ref[j] = k[j * LANES:(j + 1) * LANES, :].T
    else:
        k32_ref[...] = k
    kb = k.astype(BF16)
    khm_ref[0] = kb[:, :HEAD_DIM_A]
    khm_ref[1] = kb[:, HEAD_DIM_A:]
    o += WA_KV
    if not paged:
        v32_ref[...] = p[:, o:o + WA_KV]
    o += WA_KV
    for s in range(WI_Q // LANES):
        vb = rope(p[:, o + s * LANES:o + (s + 1) * LANES]).astype(BF16)
        qi_ref[2 * s] = vb[:, :D_IDX]
        qi_ref[2 * s + 1] = vb[:, D_IDX:]
    o += WI_Q
    ki_slab = rope(p[:, o:o + LANES])
    ki = ki_slab[:, :D_IDX]
    if paged:
        for j in range(n_pg):
            ki32_ref[j] = ki_slab[j * LANES:(j + 1) * LANES, :].T[:D_IDX, :]
    else:
        ki32_ref[...] = ki
    kibf_ref[...] = ki.astype(BF16)
    wit_ref[...] = _dot_nt(wwit_ref[...], h) * score_scale
    vt = _dot_nt(wvt_ref[...], h)
    if paged:
        for j in range(n_pg):
            v32_ref[j] = vt[:, j * LANES:(j + 1) * LANES]
    tail = jnp.where(lax.broadcasted_iota(I32, (VT_ROWS - HEAD_DIM_A, vt.shape[1]), 0) == 0, 1.0, 0.0)
    for g in range(N_KV_A):
        vt_ref[g] = jnp.concatenate([vt[g * HEAD_DIM_A:(g + 1) * HEAD_DIM_A], tail], axis=0).astype(BF16)


def _front(x2d, tabs, gin, wa, wwit, wvt, bd, gqa, gka, tm, paged):
    n, d = x2d.shape
    cos, sa, sb = tabs
    row = lambda w: pl.BlockSpec((tm, w), lambda i: (i, 0))
    hm = lambda nh, w: pl.BlockSpec((nh, tm, w), lambda i: (0, i, 0))
    if paged:
        assert tm % LANES == 0
        f32_out = lambda w: jax.ShapeDtypeStruct((n // LANES, w, LANES), F32)
        f32_spec = lambda w: pl.BlockSpec((tm // LANES, w, LANES), lambda i: (i, 0, 0))
    else:
        f32_out = lambda w: jax.ShapeDtypeStruct((n, w), F32)
        f32_spec = row
    out_shape = (
        jax.ShapeDtypeStruct((N_HEADS_A, n, HEAD_DIM_A), BF16),
        f32_out(WA_KV),
        jax.ShapeDtypeStruct((N_KV_A, n, HEAD_DIM_A), BF16),
        f32_out(WA_KV),
        jax.ShapeDtypeStruct((N_KV_A, VT_ROWS, n), BF16),
        jax.ShapeDtypeStruct((N_HEADS_IDX, n, D_IDX), BF16),
        f32_out(D_IDX),
        jax.ShapeDtypeStruct((n, D_IDX), BF16),
        jax.ShapeDtypeStruct((N_HEADS_IDX, n), F32),
    )
    out_specs = (hm(N_HEADS_A, HEAD_DIM_A), f32_spec(WA_KV), hm(N_KV_A, HEAD_DIM_A), f32_spec(WA_KV),
                 pl.BlockSpec((N_KV_A, VT_ROWS, tm), lambda i: (0, 0, i)), hm(N_HEADS_IDX, D_IDX), f32_spec(D_IDX),
                 row(D_IDX), pl.BlockSpec((N_HEADS_IDX, tm), lambda i: (0, i)))
    in_specs = [row(d), _const_spec(gin.shape), _const_spec(wa.shape), _const_spec(wwit.shape),
                _const_spec(wvt.shape), _const_spec(bd.shape), _const_spec(gqa.shape), _const_spec(gka.shape),
                row(LANES), row(LANES), row(LANES)]
    kern = functools.partial(_front_kernel, score_scale=D_IDX ** -0.5 * N_HEADS_IDX ** -0.5, paged=paged)
    return pl.pallas_call(kern, out_shape=out_shape, grid=(n // tm,), in_specs=in_specs, out_specs=out_specs,
                          compiler_params=_cparams(), name="front")(x2d, gin, wa, wwit, wvt, bd, gqa, gka, cos, sa, sb)


def _memkv_kernel(mem_ref, gmem_ref, w_ref, gkm_ref, k32_ref, v32_ref):
    h = _rms(mem_ref[...], gmem_ref[...]).astype(BF16)
    kv = _dot(h, w_ref[...])
    for hh in range(N_HEADS_M):
        sl = slice(hh * HEAD_DIM_M, (hh + 1) * HEAD_DIM_M)
        k32_ref[:, sl] = _rms(kv[:, sl], gkm_ref[...])
    v32_ref[...] = kv[:, WM_Q:]


def _memory_kv(mem2d, gmem, w, gkm):
    n = mem2d.shape[0]
    out = jax.ShapeDtypeStruct((n, WM_Q), F32)
    return pl.pallas_call(_memkv_kernel, out_shape=(out, out), name="memory_kv",
                          compiler_params=pltpu.CompilerParams(vmem_limit_bytes=VMEM_LIMIT))(mem2d, gmem, w, gkm)


def _sort_key(s):
    s = jnp.where(s == 0.0, 0.0, s)
    bits = pltpu.bitcast(s, I32)
    return bits ^ ((bits >> 31) & INT_MAX)


KEYS_PER_GROUP = 32 * SUBLANES
SEARCH_SLAB = 64


def _bit_planes(words):
    a = list(words)
    j, m = 16, 0x0000FFFF
    while j:
        k = 0
        while k < 32:
            t = (a[k] ^ lax.shift_right_logical(a[k + j], jnp.int32(j))) & m
            a[k] = a[k] ^ t
            a[k + j] = a[k + j] ^ (t << j)
            k = (k + j + 1) & ~j
        j >>= 1
        m = (m ^ (m << j)) & 0xFFFFFFFF if j else m
    return a


ATTN_SUB = 256


def _attn_kernel(qi_ref, qa_ref, wi_ref, ki_ref, k_ref, vt_ref, out_ref,
                 keys_ref, planes_ref, cand_ref, cut_ref, bias_ref, lg_ref, cm_ref, m_ref, acc_ref, *, k_sel, idx_bits):
    bq, sc = ATTN_BQ, ATTN_SC
    wpc = sc // 32
    i = pl.program_id(0)
    n_ch = (i * bq) // sc + 1
    t_pos = i * bq + lax.broadcasted_iota(I32, (1, bq), 1)
    sub = lax.broadcasted_iota(I32, (sc, 1), 0)
    qi = qi_ref[...].reshape(N_HEADS_IDX * bq, D_IDX)
    w = wi_ref[...]

    @pl.when(i == 0)
    def _():
        planes_ref[...] = jnp.zeros(planes_ref.shape, I32)

    def chunk(c):
        return pl.ds(pl.multiple_of(c * sc, sc), sc)

    def score_chunk(c, carry):
        off = pl.multiple_of(c * sc, sc)
        d = _dot_nt(ki_ref[chunk(c), :], qi)
        s = jnp.zeros((sc, bq), F32)
        for h in range(N_HEADS_IDX):
            s = s + jnp.maximum(d[:, h * bq:(h + 1) * bq], 0.0) * w[h:h + 1, :]
        key = jnp.where(off + sub <= t_pos, _sort_key(s), INT_MIN)
        keys_ref[chunk(c), :] = key
        u = key ^ INT_MIN
        for grp in range(sc // KEYS_PER_GROUP):
            base = grp * KEYS_PER_GROUP
            words = _bit_planes([u[base + SUBLANES * v:base + SUBLANES * (v + 1), :] for v in range(32)])
            rows = pl.ds(pl.multiple_of(c * wpc + grp * SUBLANES, SUBLANES), SUBLANES)
            for p in range(32):
                planes_ref[p, rows, :] = words[p]
        return carry

    lax.fori_loop(0, n_ch // 2, lambda j, cr: score_chunk(2 * j + 1, score_chunk(2 * j, cr)), 0)
    lax.fori_loop(n_ch - n_ch % 2, n_ch, score_chunk, 0)

    def count(pred):
        def body(c, acc):
            hit = jnp.where(pred(keys_ref[chunk(c), :], c * sc + sub), 1, 0)
            return acc + hit.reshape(sc // SUBLANES, SUBLANES, bq).sum(axis=0)

        def body2(j, acc):
            return body(2 * j + 1, body(2 * j, acc))
        acc = lax.fori_loop(0, n_ch // 2, body2, jnp.zeros((SUBLANES, bq), I32))
        acc = lax.fori_loop(n_ch - n_ch % 2, n_ch, body, acc)
        return acc.sum(axis=0, keepdims=True)

    slab = SEARCH_SLAB
    n_slab = (n_ch * wpc + slab - 1) // slab
    srow = lax.broadcasted_iota(I32, (slab, 1), 0)

    def slab_rows(sl):
        return pl.ds(pl.multiple_of(sl * slab, slab), slab)

    def init_cand(sl, carry):
        cand_ref[slab_rows(sl), :] = jnp.where(sl * slab + srow < n_ch * wpc, -1, 0) + jnp.zeros((slab, bq), I32)
        return carry

    lax.fori_loop(0, n_slab, init_cand, 0)

    def sweep(prev, cur):
        def body(sl, acc):
            r = slab_rows(sl)
            cand = cand_ref[r, :]
            if prev is not None:
                ones = cand & planes_ref[prev[0], r, :]
                cand = jnp.where(prev[1] != 0, ones, cand ^ ones)
                cand_ref[r, :] = cand
            hits = cand if cur is None else cand & planes_ref[cur, r, :]
            cnt = lax.population_count(hits)
            return acc + cnt.reshape(slab // SUBLANES, SUBLANES, bq).sum(axis=0)
        acc = lax.fori_loop(0, n_slab, body, jnp.zeros((SUBLANES, bq), I32))
        return acc.sum(axis=0, keepdims=True)

    def decide(ones, p, need, thr_u):
        keep = ones >= need
        bit = jnp.int32(1) << (31 - p)
        return jnp.where(keep, 1, 0), jnp.where(keep, need, need - ones), jnp.where(keep, thr_u | bit, thr_u)

    keep, need, thr_u = decide(sweep(None, 0), 0, jnp.full((1, bq), k_sel, I32), jnp.zeros((1, bq), I32))

    def search_pass(p, st):
        keep, need, thr_u = st
        return decide(sweep((p - 1, keep), p), p, need, thr_u)

    keep, need, thr_u = lax.fori_loop(1, 32, search_pass, (keep, need, thr_u))
    tied = sweep((31, keep), None)
    thr = thr_u ^ INT_MIN
    cnt = (k_sel - need) + tied
    cut_ref[...] = jnp.full((1, bq), INT_MAX, I32)

    excess = jnp.logical_and(cnt > k_sel, thr > INT_MIN)
    has_ties = jnp.max(jnp.where(excess, 1, 0)) > 0

    @pl.when(has_ties)
    def _():
        want = k_sel - count(lambda blk, _: blk > thr)

        def tie_body(j, cpos):
            cand = cpos + (jnp.int32(1) << (idx_bits - 1 - j))
            below = count(lambda blk, pos: jnp.where(blk == thr, pos, INT_MAX) < cand)
            return jnp.where(below < want, cand, cpos)

        cpos = lax.fori_loop(0, idx_bits, tie_body, jnp.zeros((1, bq), I32))
        cut_ref[...] = jnp.where(excess, cpos, INT_MAX)

    m_ref[...] = jnp.full(m_ref.shape, NEG, F32)
    acc_ref[...] = jnp.zeros(acc_ref.shape, F32)
    cut = cut_ref[...]
    floor = jnp.where(thr == INT_MIN, INT_MIN, thr - 1)
    n_sub = sc // ATTN_SUB

    qs = [qa_ref[g * GQA:(g + 1) * GQA].reshape(GQA * bq, HEAD_DIM_A) for g in range(N_KV_A)]
    neg_row = jnp.full((1, GQA * bq), NEG, F32)

    def stage(c, par, cb, par_b):
        do_a, do_b = c is not None, cb is not None
        if do_a:
            off = pl.multiple_of(c * sc, sc)
            keyc = keys_ref[chunk(c), :]

            @pl.when(has_ties)
            def _():
                tie = jnp.where(off + sub <= cut, 0.0, NEG)
                bias = jnp.where(keyc > thr, 0.0, jnp.where(keyc == thr, tie, NEG))
                bias_ref[...] = jnp.where(keyc == INT_MIN, NEG, bias)

            @pl.when(jnp.logical_not(has_ties))
            def _():
                bias_ref[...] = jnp.where(keyc > floor, 0.0, NEG)
            cmax = [neg_row] * N_KV_A
        if do_b:
            off_b = pl.multiple_of(cb * sc, sc)
            m_new, acc = [], []
            for g in range(N_KV_A):
                m_old = m_ref[g]
                m_new.append(jnp.maximum(m_old, cm_ref[par_b, g]))
                acc.append(jnp.exp2(m_old - m_new[g]) * acc_ref[g])
                m_ref[g] = m_new[g]
        for r in range(n_sub):
            rows = slice(r * ATTN_SUB, (r + 1) * ATTN_SUB)
            if do_a:
                bias4 = jnp.concatenate([bias_ref[rows, :]] * GQA, axis=1)
                for g in range(N_KV_A):
                    lg = _dot_nt(k_ref[g, pl.ds(off + r * ATTN_SUB, ATTN_SUB), :], qs[g]) + bias4
                    lg_ref[par, g, rows, :] = lg
                    cmax[g] = jnp.maximum(cmax[g], lg.max(axis=0, keepdims=True))
            if do_b:
                for g in range(N_KV_A):
                    p = jnp.exp2(lg_ref[par_b, g, rows, :] - m_new[g])
                    vt = vt_ref[g, :, pl.ds(off_b + r * ATTN_SUB, ATTN_SUB)]
                    acc[g] = acc[g] + _dot(vt, p.astype(BF16))
        if do_b:
            for g in range(N_KV_A):
                acc_ref[g] = acc[g]
        if do_a:
            for g in range(N_KV_A):
                cm_ref[par, g] = cmax[g]

    stage(jnp.int32(0), 0, None, None)

    def stage_pair(j, carry):
        stage(2 * j + 1, 1, 2 * j, 0)
        stage(2 * j + 2, 0, 2 * j + 1, 1)
        return carry

    last = n_ch - 1
    lax.fori_loop(0, last // 2, stage_pair, 0)

    @pl.when(last % 2 == 1)
    def _():
        stage(last, 1, last - 1, 0)
        stage(None, None, last, 1)

    @pl.when(last % 2 == 0)
    def _():
        stage(None, None, last, 0)

    heads = []
    for g in range(N_KV_A):
        acc = acc_ref[g]
        o = acc[:HEAD_DIM_A] / acc[HEAD_DIM_A:HEAD_DIM_A + 1]
        heads += [o[:, hh * bq:(hh + 1) * bq] for hh in range(GQA)]
    out_ref[...] = jnp.concatenate(heads, axis=0).T.astype(out_ref.dtype)


def _prompt_attention(qi_hm, qa_hm, wit, kibf, khm, vt):
    t = kibf.shape[0]
    bq = ATTN_BQ
    k_sel = min(TOPK_MAX, t // 4)
    kern = functools.partial(_attn_kernel, k_sel=k_sel, idx_bits=max(1, (t - 1).bit_length()))
    in_specs = [pl.BlockSpec((N_HEADS_IDX, bq, D_IDX), lambda i: (0, i, 0)),
                pl.BlockSpec((N_HEADS_A, bq, HEAD_DIM_A), lambda i: (0, i, 0)),
                pl.BlockSpec((N_HEADS_IDX, bq), lambda i: (0, i)),
                _const_spec(kibf.shape, True), _const_spec(khm.shape, True), _const_spec(vt.shape, True)]
    n_words = -(-(t // 32) // SEARCH_SLAB) * SEARCH_SLAB
    scratch = [pltpu.VMEM((t, bq), I32), pltpu.VMEM((32, n_words, bq), I32), pltpu.VMEM((n_words, bq), I32),
               pltpu.VMEM((1, bq), I32), pltpu.VMEM((ATTN_SC, bq), F32),
               pltpu.VMEM((2, N_KV_A, ATTN_SC, GQA * bq), F32), pltpu.VMEM((2, N_KV_A, 1, GQA * bq), F32),
               pltpu.VMEM((N_KV_A, 1, GQA * bq), F32),
               pltpu.VMEM((N_KV_A, VT_ROWS, GQA * bq), F32)]
    return pl.pallas_call(kern, out_shape=jax.ShapeDtypeStruct((t, WA_Q), BF16), grid=(t // bq,),
                          in_specs=in_specs, out_specs=pl.BlockSpec((bq, WA_Q), lambda i: (i, 0)),
                          scratch_shapes=scratch, compiler_params=_cparams(), name="attn")(
                              qi_hm, qa_hm, wit, kibf, khm, vt)


SAMPLE_CK = 2048


def _sample_attn_kernel(pt_ref, qi_ref, qa_ref, wi_ref, kin_ref, kn_ref, vn_ref, cidx_hbm, ck_hbm, cv_hbm, out_ref,
                        idx_buf, k_buf, v_buf, scr_ref, cut_ref, sems, *, n_pages, ps, ck, k_sel, idx_bits):
    b = pl.program_id(0)
    nb = pl.num_programs(0)
    past = n_pages * ps
    nc = past // ck

    def page_copy(which, bb, p):
        src, dst = ((cidx_hbm, idx_buf), (ck_hbm, k_buf), (cv_hbm, v_buf))[which]
        cols = pl.ds(pl.multiple_of(p * ps, ps), ps)
        dst = dst.at[:, cols] if which == 0 else dst.at[:, :, cols]
        return pltpu.make_async_copy(src.at[pt_ref[bb, p]], dst, sems.at[which])

    def for_pages(fn):
        lax.fori_loop(0, n_pages, lambda p, c: (fn(p), c)[1], 0, unroll=8)

    def wait_all(which):
        buf = (idx_buf, k_buf, v_buf)[which]
        pltpu.make_async_copy(buf, buf, sems.at[which]).wait()

    @pl.when(b == 0)
    def _():
        for_pages(lambda p: page_copy(0, b, p).start())

    for_pages(lambda p: (page_copy(1, b, p).start(), page_copy(2, b, p).start()))
    wait_all(0)

    qi = qi_ref[0]
    w = wi_ref[0]
    for c in range(nc):
        d = _dot(qi, idx_buf[:, c * ck:(c + 1) * ck].astype(BF16))
        scr_ref[c:c + 1, :] = (jnp.maximum(d, 0.0) * w).sum(axis=0, keepdims=True)
    d_new = (qi.astype(F32) * kin_ref[0].astype(BF16).astype(F32)).sum(axis=-1, keepdims=True)
    s_new = (jnp.maximum(d_new, 0.0) * w).sum(axis=0, keepdims=True)

    @pl.when(b + 1 < nb)
    def _():
        for_pages(lambda p: page_copy(0, b + 1, p).start())

    keys = _sort_key(scr_ref[...])
    key_new = _sort_key(s_new)
    pos = (lax.broadcasted_iota(I32, keys.shape, 0) * ck + lax.broadcasted_iota(I32, keys.shape, 1))

    def count(pred):
        hit = jnp.where(pred(keys, pos), 1, 0)
        parts = [hit[:, j * LANES:(j + 1) * LANES] for j in range(ck // LANES)]
        while len(parts) > 1:
            parts = [a + b for a, b in zip(parts[::2], parts[1::2])] + parts[len(parts) - len(parts) % 2:]
        total = parts[0].sum(axis=1, keepdims=True).sum(axis=0, keepdims=True)
        return total + jnp.where(pred(key_new, past), 1, 0)

    dcol = lax.broadcasted_iota(I32, (16, 1), 0)

    def search_body(j, st):
        thr, cnt = st
        step = jnp.int32(1) << (28 - 4 * j)
        rows = [jnp.zeros((1, LANES), I32)]
        for d in range(1, 16):
            hit = jnp.where(keys >= thr + d * step, 1, 0)
            parts = [hit[:, t * LANES:(t + 1) * LANES] for t in range(ck // LANES)]
            while len(parts) > 1:
                parts = [a + b for a, b in zip(parts[::2], parts[1::2])] + parts[len(parts) - len(parts) % 2:]
            rows.append(parts[0].sum(axis=0, keepdims=True))
        cands = thr + dcol * step
        cnts = jnp.concatenate(rows, axis=0).sum(axis=1, keepdims=True) + jnp.where(key_new >= cands, 1, 0)
        ok = jnp.logical_and(cnts >= k_sel, dcol >= 1)
        digit = jnp.where(ok, 1, 0).sum(axis=0, keepdims=True)
        best = jnp.where(ok, cnts, INT_MAX).min(axis=0, keepdims=True)
        return thr + digit * step, jnp.where(digit > 0, best, cnt)

    st0 = (jnp.full((1, 1), INT_MIN, I32), jnp.full((1, 1), past + 1, I32))
    thr, cnt = lax.fori_loop(0, 8, search_body, st0)
    cut_ref[...] = jnp.full((1, 1), INT_MAX, I32)

    @pl.when(jnp.max(jnp.where(cnt > k_sel, 1, 0)) > 0)
    def _():
        want = k_sel - count(lambda kk, _: kk > thr)

        def tie_body(j, cpos):
            cand = cpos + (jnp.int32(1) << (idx_bits - 1 - j))
            below = count(lambda kk, pp: jnp.where(kk == thr, pp, INT_MAX) < cand)
            return jnp.where(below < want, cand, cpos)

        cut_ref[...] = lax.fori_loop(0, idx_bits, tie_body, jnp.zeros((1, 1), I32))

    cut = cut_ref[...]

    def sel_bias(kk, pp):
        tie = jnp.where(pp <= cut, 0.0, NEG)
        return jnp.where(kk > thr, 0.0, jnp.where(kk == thr, tie, NEG))

    bias = sel_bias(keys, pos)
    bias_new = sel_bias(key_new, past)

    wait_all(1)
    wait_all(2)

    qa = qa_ref[0]
    heads = []
    for g in range(N_KV_A):
        qg = qa[g * GQA:(g + 1) * GQA]
        cols = slice(g * HEAD_DIM_A, (g + 1) * HEAD_DIM_A)
        spans = [slice(c * ck, (c + 1) * ck) for c in range(nc)]
        lgs = [_dot(qg, k_buf[g, :, sp].astype(BF16)) + bias[c:c + 1, :] for c, sp in enumerate(spans)]
        kn = kn_ref[0][:, cols].astype(BF16).astype(F32)
        vn = vn_ref[0][:, cols].astype(BF16).astype(F32)
        lg_new = (qg.astype(F32) * kn).sum(axis=-1, keepdims=True) + bias_new
        m = lg_new
        for lg in lgs:
            m = jnp.maximum(m, lg.max(axis=-1, keepdims=True))
        p_new = jnp.exp2(lg_new - m)
        l = p_new
        acc = p_new * vn
        for lg, sp in zip(lgs, spans):
            p = jnp.exp2(lg - m)
            l = l + p.sum(axis=-1, keepdims=True)
            acc = acc + _dot_nt(p.astype(BF16), v_buf[g, :, sp].astype(BF16))
        o = acc / l
        heads += [o[hh:hh + 1, :] for hh in range(GQA)]
    out_ref[0] = jnp.concatenate(heads, axis=1)


def _sample_attention(page_table, qi_s, qa_s, wi_s, ki_new, k_new, v_new, cache_idx_k, cache_k, cache_v):
    nbatch, n_pages = page_table.shape
    n_pool, ps, d_idx = cache_idx_k.shape
    past = n_pages * ps
    ck = min(SAMPLE_CK, past)
    assert past % ck == 0 and ck % ps == 0
    k_sel = min(TOPK_MAX, (past + 1) // 4)
    kern = functools.partial(_sample_attn_kernel, n_pages=n_pages, ps=ps, ck=ck, k_sel=k_sel,
                             idx_bits=past.bit_length())
    per_b = lambda shape: pl.BlockSpec((1,) + shape, lambda b, pt: (b, 0, 0))
    any_spec = pl.BlockSpec(memory_space=pl.ANY)
    grid_spec = pltpu.PrefetchScalarGridSpec(
        num_scalar_prefetch=1, grid=(nbatch,),
        in_specs=[per_b((N_HEADS_IDX, d_idx)), per_b((N_HEADS_A, HEAD_DIM_A)), per_b((N_HEADS_IDX, 1)),
                  per_b((1, d_idx)), per_b((1, WA_KV)), per_b((1, WA_KV)), any_spec, any_spec, any_spec],
        out_specs=per_b((1, WA_Q)),
        scratch_shapes=[pltpu.VMEM((d_idx, past), F32), pltpu.VMEM((N_KV_A, HEAD_DIM_A, past), F32),
                        pltpu.VMEM((N_KV_A, HEAD_DIM_A, past), F32),
                        pltpu.VMEM((past // ck, ck), F32), pltpu.VMEM((1, 1), I32),
                        pltpu.SemaphoreType.DMA((3,))])
    return pl.pallas_call(kern, out_shape=jax.ShapeDtypeStruct((nbatch, 1, WA_Q), F32), grid_spec=grid_spec,
                          compiler_params=_cparams(), name="sample_attn")(
                              page_table, qi_s, qa_s, wi_s, ki_new, k_new, v_new,
                              jnp.transpose(cache_idx_k, (0, 2, 1)), jnp.transpose(cache_k, (0, 2, 3, 1)),
                              jnp.transpose(cache_v, (0, 2, 3, 1)))


def _rope_tables(pos):
    rot = HEAD_DIM_A // ROT_FRACTION
    half = rot // 2
    inv_freq = ROPE_THETA ** (-jnp.arange(half, dtype=F32) / half)
    ang = pos.astype(F32)[:, None] * inv_freq[None, :]
    cos, sin = jnp.cos(ang), jnp.sin(ang)
    n = pos.shape[0]
    z = lambda w: jnp.zeros((n, w), F32)
    c = jnp.concatenate([cos, cos, jnp.ones((n, HEAD_DIM_A - rot), F32)], axis=1)
    sa = jnp.concatenate([-sin, z(HEAD_DIM_A - half)], axis=1)
    sb = jnp.concatenate([z(half), sin, z(HEAD_DIM_A - rot)], axis=1)
    rep = LANES // HEAD_DIM_A
    return tuple(jnp.tile(a, (1, rep)) for a in (c, sa, sb))


def _prep_params(w_in, g_in, g_qa, g_ka):
    d = w_in.shape[0]
    w_qa, w_ka, w_va, w_qi, w_ki, w_wi = _split_w_in(w_in, d)[:6]
    wa = jnp.concatenate([w_qa, w_ka, w_va, w_qi, w_ki, jnp.zeros((d, LANES - D_IDX), w_in.dtype)], axis=1)
    lane = jnp.arange(LANES)
    bd = (lane[:, None] // HEAD_DIM_A == lane[None, :] // HEAD_DIM_A).astype(BF16)
    rep = LANES // HEAD_DIM_A
    return (g_in.reshape(1, d), wa.astype(BF16), w_wi.T.astype(BF16), w_va.T.astype(BF16), bd,
            jnp.tile(g_qa, rep).reshape(1, LANES), jnp.tile(g_ka, rep).reshape(1, LANES))


def _split_w_in(w_in, d):
    sizes = (WA_Q, WA_KV, WA_KV, WI_Q, D_IDX, N_HEADS_IDX, WIDTH_B, WIDTH_B, WM_Q, N_BRANCH * d)
    parts, o = [], 0
    for s in sizes:
        parts.append(w_in[:, o:o + s])
        o += s
    return parts


def _mix_kernel(*refs, chunked, shared_mem, emit_vb, n_experts):
    it = iter(refs)
    x_ref, oa_ref, gin_ref, wb_ref, gvb_ref, gqm_ref = (next(it) for _ in range(6))
    if chunked:
        wtril_ref, bt_ref = next(it), next(it)
    else:
        wdiag_ref, bdiag_ref = next(it), next(it)
    mk_ref, mv_ref = next(it), next(it)
    wbra_ref, wbrb_ref, wbrm_ref, wout_ref, gffn_ref, wrh_ref, wrl_ref, br_ref = (next(it) for _ in range(8))
    x1_ref, h2_ref, te_ref, tw_ref = (next(it) for _ in range(4))
    vb_ref = next(it) if emit_vb else None

    x = x_ref[...]
    tm, d = x.shape
    h = _rms(x, gin_ref[...]).astype(BF16)
    p = _dot(h, wb_ref[...])
    ub = p[:, :WIDTH_B]
    vb = _rms(p[:, WIDTH_B:2 * WIDTH_B], gvb_ref[...])
    if emit_vb:
        vb_ref[...] = vb
    o = 2 * WIDTH_B
    qm = p[:, o:o + WM_Q]
    o += WM_Q
    gates = p[:, o:o + N_BRANCH * d]

    if chunked:
        vbb = vb.astype(BF16)
        rows = []
        for cc in range(tm // CHUNK):
            cols = []
            for g in range(N_GROUPS_B):
                vg = vbb[cc * CHUNK:(cc + 1) * CHUNK, g * GROUP_DIM_B:(g + 1) * GROUP_DIM_B]
                cols.append(_dot(wtril_ref[g], vg))
            rows.append(jnp.concatenate(cols, axis=1) + bt_ref[...])
        z = jnp.concatenate(rows, axis=0) if len(rows) > 1 else rows[0]
    else:
        z = vb * wdiag_ref[...] + bdiag_ref[...]
    out_b = ub * z

    scale_m = HEAD_DIM_M ** -0.5
    outs = []
    for hh in range(N_HEADS_M):
        sl = slice(hh * HEAD_DIM_M, (hh + 1) * HEAD_DIM_M)
        qh = _rms(qm[:, sl], gqm_ref[...])
        if shared_mem:
            lg = _dot_nt(qh.astype(BF16), mk_ref[:, sl].astype(BF16)) * scale_m
            pm = jnp.exp(lg - lg.max(axis=-1, keepdims=True))
            pm = pm / pm.sum(axis=-1, keepdims=True)
            outs.append(_dot(pm.astype(BF16), mv_ref[:, sl].astype(BF16)))
        else:
            per_row = []
            for r in range(tm):
                kr = mk_ref[r, :, sl]
                lg = (kr * qh[r:r + 1, :]).sum(axis=-1, keepdims=True) * scale_m
                pm = jnp.exp(lg - lg.max(axis=0, keepdims=True))
                pm = pm / pm.sum(axis=0, keepdims=True)
                per_row.append((pm * mv_ref[r, :, sl]).sum(axis=0, keepdims=True))
            outs.append(jnp.concatenate(per_row, axis=0))
    out_m = jnp.concatenate(outs, axis=1)

    sig = jax.nn.sigmoid
    merged = (sig(gates[:, :d]) * _dot(oa_ref[...].astype(BF16), wbra_ref[...])
              + sig(gates[:, d:2 * d]) * _dot(out_b.astype(BF16), wbrb_ref[...])
              + sig(gates[:, 2 * d:]) * _dot(out_m.astype(BF16), wbrm_ref[...]))
    x1 = x + _dot(merged.astype(BF16), wout_ref[...])
    x1_ref[...] = x1
    h2 = _rms(x1, gffn_ref[...])
    h2_ref[...] = h2.astype(h2_ref.dtype)

    hi = h2.astype(BF16)
    lo = (h2 - hi.astype(F32)).astype(BF16)
    lg = _dot(hi, wrh_ref[...]) + _dot(hi, wrl_ref[...]) + _dot(lo, wrh_ref[...]) + br_ref[...]
    lane = lax.broadcasted_iota(I32, lg.shape, 1)
    lg = jnp.where(lane < n_experts, lg, -jnp.inf)
    vals, idxs = [], []
    for _ in range(TOP_K_MOE):
        m = lg.max(axis=-1, keepdims=True)
        idx = jnp.where(lg == m, lane, LANES).min(axis=-1, keepdims=True)
        vals.append(m)
        idxs.append(idx)
        lg = jnp.where(lane == idx, -jnp.inf, lg)
    ex = [jnp.exp(v - vals[0]) for v in vals]
    den = ex[0]
    for e in ex[1:]:
        den = den + e
    te = jnp.full(lane.shape, -1, I32)
    tw = jnp.zeros(lane.shape, F32)
    for j in range(TOP_K_MOE):
        te = jnp.where(lane == j, idxs[j], te)
        tw = jnp.where(lane == j, ex[j] / den, tw)
    te_ref[...] = te
    tw_ref[...] = tw


def _mix(x2d, out_a, mix_params, gmlp_params, mem_k, mem_v, *, tm, chunked, shared_mem, emit_vb, n_experts):
    n, d = x2d.shape
    gin, wb, gvb, gqm, wbra, wbrb, wbrm, wout, gffn, wrh, wrl, br = mix_params
    row = lambda w: pl.BlockSpec((tm, w), lambda i: (i, 0))
    if shared_mem:
        mem_specs = [_const_spec(mem_k.shape), _const_spec(mem_v.shape)]
    else:
        mspec = pl.BlockSpec((tm,) + mem_k.shape[1:], lambda i: (i, 0, 0))
        mem_specs = [mspec, mspec]
    consts = lambda arrs: [_const_spec(a.shape) for a in arrs]
    in_specs = ([row(d), row(WA_Q)] + consts([gin, wb, gvb, gqm]) + consts(gmlp_params) + mem_specs
                + consts([wbra, wbrb, wbrm, wout, gffn, wrh, wrl, br]))
    out_shape = [jax.ShapeDtypeStruct((n, d), F32), jax.ShapeDtypeStruct((n, d), F32),
                 jax.ShapeDtypeStruct((n, LANES), I32), jax.ShapeDtypeStruct((n, LANES), F32)]
    out_specs = [row(d), row(d), row(LANES), row(LANES)]
    if emit_vb:
        out_shape.append(jax.ShapeDtypeStruct((n, WIDTH_B), F32))
        out_specs.append(row(WIDTH_B))
    kern = functools.partial(_mix_kernel, chunked=chunked, shared_mem=shared_mem, emit_vb=emit_vb,
                             n_experts=n_experts)
    return pl.pallas_call(kern, out_shape=tuple(out_shape), grid=(n // tm,), in_specs=in_specs,
                          out_specs=tuple(out_specs), compiler_params=_cparams(), name="mix")(
                              x2d, out_a, gin, wb, gvb, gqm, *gmlp_params, mem_k, mem_v,
                              wbra, wbrb, wbrm, wout, gffn, wrh, wrl, br)


def _prep_mix_params(w_in, g_in, g_vb, g_qm, w_br_a, w_br_b, w_br_m, w_out, g_ffn, w_router, b_router):
    d = w_in.shape[0]
    parts = _split_w_in(w_in, d)
    wb = jnp.concatenate(parts[6:10], axis=1).astype(BF16)
    n_e = w_router.shape[1]
    wr = jnp.pad(w_router, ((0, 0), (0, LANES - n_e)))
    wrh = wr.astype(BF16)
    wrl = (wr - wrh.astype(F32)).astype(BF16)
    br = jnp.pad(b_router, (0, LANES - n_e)).reshape(1, LANES)
    return (g_in.reshape(1, d), wb, g_vb.reshape(1, WIDTH_B), g_qm.reshape(1, HEAD_DIM_M),
            w_br_a.astype(BF16), w_br_b.astype(BF16), w_br_m.astype(BF16), w_out.astype(BF16),
            g_ffn.reshape(1, d), wrh, wrl, br)


def _prep_gmlp_chunked(w_s, b_s):
    tril = jnp.tril(jnp.ones((CHUNK, CHUNK), w_s.dtype))
    wtril = (w_s * tril).astype(BF16)
    bt = jnp.repeat(b_s.T, GROUP_DIM_B, axis=1)
    return wtril, bt


def _prep_gmlp_single(w_s, b_s):
    wdiag = jnp.repeat(w_s[:, 0, 0], GROUP_DIM_B).reshape(1, WIDTH_B)
    bdiag = jnp.repeat(b_s[:, 0], GROUP_DIM_B).reshape(1, WIDTH_B)
    return wdiag, bdiag


MOE_TP = 256


def _plan_kernel(te_ref, tri_ref, upper_ref, lslot_ref, cnt_ref, tstart_ref, gprev_ref, total_ref, carry_ref):
    @pl.when(pl.program_id(0) == 0)
    def _():
        carry_ref[...] = jnp.zeros(carry_ref.shape, F32)

    te = te_ref[...]
    lane = lax.broadcasted_iota(I32, te.shape, 1)
    picks = [te[:, j:j + 1] for j in range(TOP_K_MOE)]
    onehot = jnp.zeros(te.shape, F32)
    for e in picks:
        onehot = onehot + jnp.where(lane == e, 1.0, 0.0)
    before = _dot(tri_ref[...], onehot.astype(BF16))
    cnt = onehot.sum(axis=0, keepdims=True)
    cnt = jnp.floor((cnt + (SUBLANES - 1)) * (1.0 / SUBLANES)) * SUBLANES
    cnt8 = jnp.broadcast_to(cnt, (SUBLANES, LANES)).astype(BF16)
    tstart = _dot(cnt8, upper_ref[...])[0:1, :]
    lslot = jnp.zeros(te.shape, I32)
    for j, e in enumerate(picks):
        r = jnp.where(lane == e, before + tstart, 0.0).sum(axis=-1, keepdims=True).astype(I32)
        lslot = jnp.where(lane == j, r, lslot)
    lslot_ref[...] = lslot
    cnt_ref[0] = cnt.astype(I32)
    tstart_ref[0] = tstart.astype(I32)
    gprev_ref[0] = carry_ref[...].astype(I32)
    carry_ref[...] = carry_ref[...] + cnt
    total_ref[...] = carry_ref[...].astype(I32)


def _moe_plan(te_all):
    n = te_all.shape[0]
    tp = MOE_TP
    n_tiles = n // tp
    r = jnp.arange(tp)
    tri = (r[:, None] > r[None, :]).astype(BF16)
    e = jnp.arange(LANES)
    upper = (e[:, None] < e[None, :]).astype(BF16)
    per_tile = jax.ShapeDtypeStruct((n_tiles, 1, LANES), I32)
    tile_spec = pl.BlockSpec((1, 1, LANES), lambda i: (i, 0, 0))
    return pl.pallas_call(
        _plan_kernel,
        out_shape=(jax.ShapeDtypeStruct((n, LANES), I32), per_tile, per_tile, per_tile,
                   jax.ShapeDtypeStruct((1, LANES), I32)),
        grid=(n_tiles,),
        in_specs=[pl.BlockSpec((tp, LANES), lambda i: (i, 0)), _const_spec((tp, tp)), _const_spec((LANES, LANES))],
        out_specs=(pl.BlockSpec((tp, LANES), lambda i: (i, 0)), tile_spec, tile_spec, tile_spec,
                   _const_spec((1, LANES))),
        scratch_shapes=[pltpu.VMEM((1, LANES), F32)],
        compiler_params=_cparams(), name="moe_plan")(te_all, tri, upper)


def _run_pieces(tp):
    sizes, s = [], SUBLANES
    while s <= tp:
        sizes.append(s)
        s *= 2
    return tuple(reversed(sizes))


def _local_rows(tp, n_e):
    return tp * TOP_K_MOE + n_e * SUBLANES


def _for_runs(tabs, tile, n_e, tp, fn):
    cnt_s, tstart_s, gbase_s = tabs

    def per_expert(e, c):
        idx = tile * n_e + e
        length, src0, dst0 = cnt_s[idx], tstart_s[idx], gbase_s[idx]
        off = jnp.int32(0)
        for size in _run_pieces(tp):
            piece = length & size

            @pl.when(piece != 0)
            def _():
                fn(pl.multiple_of(src0 + off, SUBLANES), pl.multiple_of(dst0 + off, SUBLANES), size)
            off = off + piece
        return c

    lax.fori_loop(0, n_e, per_expert, 0, unroll=4)


def _wait_runs(tabs, tile, n_e, n_rows, copy_of):
    cnt_s, tstart_s, _ = tabs
    last = tile * n_e + n_e - 1
    total = tstart_s[last] + cnt_s[last]
    for size in _run_pieces(n_rows):
        @pl.when((total & size) != 0)
        def _():
            copy_of(size).wait()


def _slot_matrix(lslot, vals, n_rows):
    col = lax.broadcasted_iota(I32, (lslot.shape[0], n_rows), 1)
    m = jnp.zeros(col.shape, F32)
    for j in range(TOP_K_MOE):
        v = 1.0 if vals is None else vals[:, j:j + 1]
        m = m + jnp.where(lslot[:, j:j + 1] == col, v, 0.0)
    return m


def _dispatch_kernel(cnt_s, tstart_s, gbase_s, fstart_s, flen_s, nused_s, lslot_ref, h2_ref, *rest,
                     tp, n_e, tile0, first, bm):
    if first:
        xs_out, stage, sem, zbuf, zsem = rest
    else:
        _, xs_out, stage, sem = rest
    n_rows = _local_rows(tp, n_e)
    i = pl.program_id(0)
    slot = i % 2
    pt = _slot_matrix(lslot_ref[...], None, n_rows).astype(BF16)
    stage[slot] = lax.dot_general(pt, h2_ref[...].astype(BF16), (((0,), (0,)), ((), ())),
                                  preferred_element_type=F32)

    def runs(tile, s, op):
        _for_runs((cnt_s, tstart_s, gbase_s), tile0 + tile, n_e, tp,
                  lambda src, dst, size: op(pltpu.make_async_copy(
                      stage.at[s, pl.ds(src, size), :], xs_out.at[pl.ds(dst, size), :], sem.at[s])))

    def drain(tile, s):
        _wait_runs((cnt_s, tstart_s, gbase_s), tile0 + tile, n_e, n_rows, lambda size: pltpu.make_async_copy(
            stage.at[s, pl.ds(0, size), :], xs_out.at[pl.ds(0, size), :], sem.at[s]))

    @pl.when(i > 0)
    def _():
        drain(i - 1, 1 - slot)

    runs(i, slot, lambda cp: cp.start())

    @pl.when(i == pl.num_programs(0) - 1)
    def _():
        drain(i, slot)
        if first:
            zbuf[...] = jnp.zeros(zbuf.shape, F32)
            nb = xs_out.shape[0] // bm

            def fill(op):
                def per_expert(e, c):
                    length, dst0 = flen_s[e], fstart_s[e]
                    off = jnp.int32(0)
                    for size in _run_pieces(bm):
                        piece = length & size

                        @pl.when(piece != 0)
                        def _():
                            op(pltpu.make_async_copy(
                                zbuf.at[pl.ds(0, size), :],
                                xs_out.at[pl.ds(pl.multiple_of(dst0 + off, SUBLANES), size), :], zsem))
                        off = off + piece
                    return c

                lax.fori_loop(0, n_e, per_expert, 0)
                lax.fori_loop(nused_s[0], nb, lambda b, c: (op(pltpu.make_async_copy(
                    zbuf, xs_out.at[pl.ds(pl.multiple_of(b * bm, bm), bm), :], zsem)), c)[1], 0)

            fill(lambda cp: cp.start())
            fill(lambda cp: cp.wait())


def _dispatch(tabs, fill_tabs, tile0, lslot, h2, xs, n_e, ns, bm):
    n, d = h2.shape
    tp = min(MOE_TP, n)
    n_rows = _local_rows(tp, n_e)
    first = xs is None
    kern = functools.partial(_dispatch_kernel, tp=tp, n_e=n_e, tile0=tile0, first=first, bm=bm)
    any_spec = pl.BlockSpec(memory_space=pl.ANY)
    scratch = [pltpu.VMEM((2, n_rows, d), F32), pltpu.SemaphoreType.DMA((2,))]
    if first:
        scratch += [pltpu.VMEM((bm, d), F32), pltpu.SemaphoreType.DMA]
    grid_spec = pltpu.PrefetchScalarGridSpec(
        num_scalar_prefetch=6, grid=(n // tp,),
        in_specs=[pl.BlockSpec((tp, LANES), lambda i, *_: (i, 0)), pl.BlockSpec((tp, d), lambda i, *_: (i, 0))]
        + ([] if first else [any_spec]),
        out_specs=any_spec, scratch_shapes=scratch)
    args = (*tabs, *fill_tabs, lslot, h2) + (() if first else (xs,))
    return pl.pallas_call(
        kern, out_shape=jax.ShapeDtypeStruct((ns, d), F32), grid_spec=grid_spec,
        input_output_aliases={} if first else {8: 0},
        compiler_params=pltpu.CompilerParams(dimension_semantics=("arbitrary",), has_side_effects=True,
                                             vmem_limit_bytes=VMEM_LIMIT),
        name="dispatch")(*args)


def _expert_kernel(be_ref, nu_ref, xs_ref, wg_ref, wu_ref, wd_ref, bg_ref, bu_ref, bd_ref, y_ref,
                   wg_bf, wu_bf, wd_bf):
    b = pl.program_id(0)

    @pl.when(b >= nu_ref[0])
    def _():
        y_ref[...] = jnp.zeros(y_ref.shape, F32)

    @pl.when(b < nu_ref[0])
    def _():
        prev = be_ref[jnp.maximum(b - 1, 0)]

        @pl.when(jnp.logical_or(b == 0, be_ref[b] != prev))
        def _():
            wg_bf[...] = wg_ref[0].astype(BF16)
            wu_bf[...] = wu_ref[0].astype(BF16)
            wd_bf[...] = wd_ref[0].astype(BF16)

        xb = xs_ref[...].astype(BF16)
        ff = wg_bf.shape[1]
        y = jnp.zeros(y_ref.shape, F32) + bd_ref[0]
        for n in range(ff // FF_CHUNK):
            sl = slice(n * FF_CHUNK, (n + 1) * FF_CHUNK)
            hg = jnp.minimum(_dot(xb, wg_bf[:, sl]) + bg_ref[0][:, sl], SWIGLU_LIMIT)
            hu = jnp.clip(_dot(xb, wu_bf[:, sl]) + bu_ref[0][:, sl], -SWIGLU_LIMIT, SWIGLU_LIMIT)
            act = hg * jax.nn.sigmoid(SWIGLU_ALPHA * hg) * (hu + 1.0)
            y = y + _dot(act.astype(BF16), wd_bf[sl, :])
        y_ref[...] = y


def _experts(xs, block_e, n_used, w_gate, b_gate, w_up, b_up, w_down, b_down):
    ns, d = xs.shape
    n_e, _, ff = w_gate.shape
    bm = MOE_BM
    blk = lambda b, be, nu: (jnp.minimum(b, nu[0] - 1), 0)
    wsel = lambda b, be, nu: (be[b], 0, 0)
    grid_spec = pltpu.PrefetchScalarGridSpec(
        num_scalar_prefetch=2, grid=(ns // bm,),
        in_specs=[pl.BlockSpec((bm, d), blk),
                  pl.BlockSpec((1, d, ff), wsel), pl.BlockSpec((1, d, ff), wsel), pl.BlockSpec((1, ff, d), wsel),
                  pl.BlockSpec((1, 1, ff), wsel), pl.BlockSpec((1, 1, ff), wsel), pl.BlockSpec((1, 1, d), wsel)],
        out_specs=pl.BlockSpec((bm, d), lambda b, be, nu: (b, 0)),
        scratch_shapes=[pltpu.VMEM((d, ff), BF16), pltpu.VMEM((d, ff), BF16), pltpu.VMEM((ff, d), BF16)])
    return pl.pallas_call(_expert_kernel, out_shape=jax.ShapeDtypeStruct((ns, d), F32), grid_spec=grid_spec,
                          compiler_params=_cparams(), name="experts")(
                              block_e, n_used, xs, w_gate, w_up, w_down,
                              b_gate.reshape(n_e, 1, ff), b_up.reshape(n_e, 1, ff), b_down.reshape(n_e, 1, d))


def _combine_kernel(cnt_s, tstart_s, gbase_s, lslot_ref, tw_ref, x1_ref, yb_hbm, y_ref, buf, sem, *, tp, n_e, tile0):
    n_rows = _local_rows(tp, n_e)
    i = pl.program_id(0)
    slot = i % 2

    def runs(tile, s, op):
        _for_runs((cnt_s, tstart_s, gbase_s), tile0 + tile, n_e, tp,
                  lambda loc, glob, size: op(pltpu.make_async_copy(
                      yb_hbm.at[pl.ds(glob, size), :], buf.at[s, pl.ds(loc, size), :], sem.at[s])))

    @pl.when(i == 0)
    def _():
        buf[...] = jnp.zeros(buf.shape, F32)
        runs(i, slot, lambda cp: cp.start())

    @pl.when(i + 1 < pl.num_programs(0))
    def _():
        runs(i + 1, 1 - slot, lambda cp: cp.start())

    _wait_runs((cnt_s, tstart_s, gbase_s), tile0 + i, n_e, n_rows, lambda size: pltpu.make_async_copy(
        yb_hbm.at[pl.ds(0, size), :], buf.at[slot, pl.ds(0, size), :], sem.at[slot]))

    ptw = _slot_matrix(lslot_ref[...], tw_ref[...], n_rows)
    rows = buf[slot]
    w_hi = ptw.astype(BF16)
    w_lo = (ptw - w_hi.astype(F32)).astype(BF16)
    r_hi = rows.astype(BF16)
    r_lo = (rows - r_hi.astype(F32)).astype(BF16)
    y_ref[...] = x1_ref[...] + _dot(w_hi, r_hi) + (_dot(w_hi, r_lo) + _dot(w_lo, r_hi))


def _combine(tabs, tile0, lslot, tw, x1, yb, n_e):
    n, d = x1.shape
    tp = min(MOE_TP, n)
    n_rows = _local_rows(tp, n_e)
    kern = functools.partial(_combine_kernel, tp=tp, n_e=n_e, tile0=tile0)
    row = lambda w: pl.BlockSpec((tp, w), lambda i, *_: (i, 0))
    grid_spec = pltpu.PrefetchScalarGridSpec(
        num_scalar_prefetch=3, grid=(n // tp,),
        in_specs=[row(LANES), row(LANES), row(d), pl.BlockSpec(memory_space=pl.ANY)],
        out_specs=row(d),
        scratch_shapes=[pltpu.VMEM((2, n_rows, d), F32), pltpu.SemaphoreType.DMA((2,))])
    return pl.pallas_call(kern, out_shape=jax.ShapeDtypeStruct(x1.shape, F32), grid_spec=grid_spec,
                          compiler_params=_cparams(), name="combine")(*tabs, lslot, tw, x1, yb)


def _moe(groups, w_gate, b_gate, w_up, b_up, w_down, b_down):
    n_e = w_gate.shape[0]
    d = groups[0][0].shape[1]
    tp, bm = MOE_TP, MOE_BM
    sizes = [g[0].shape[0] for g in groups]
    assert all(n % tp == 0 or n < tp for n in sizes), sizes
    parts, tile0s, o = [], [], 0
    for g, n in zip(groups, sizes):
        n_pad = -(-n // tp) * tp
        parts.append(jnp.pad(g[2], ((0, n_pad - n), (0, 0)), constant_values=-1))
        tile0s.append(o // tp)
        o += n_pad
    lslot, cnt3, tstart3, gprev3, total = _moe_plan(jnp.concatenate(parts, axis=0))
    counts = total[0, :n_e]
    padded_cnt = (counts + bm - 1) // bm * bm
    pad_end = jnp.cumsum(padded_cnt)
    pad_start = pad_end - padded_cnt
    n_assign = sum(sizes) * TOP_K_MOE
    n_slack = (o // tp) * n_e * (SUBLANES - 1)
    nb = -(-(n_assign + n_slack) // bm) + n_e
    n_used = (pad_end[-1] // bm).astype(I32)
    blocks = jnp.arange(nb, dtype=I32)
    block_e = jnp.minimum((pad_end[None, :] <= (blocks * bm)[:, None]).sum(axis=1), n_e - 1).astype(I32)
    block_e = jnp.where(blocks < n_used, block_e, block_e[jnp.maximum(n_used - 1, 0)])
    flat = lambda a: a[:, 0, :n_e].reshape(-1).astype(I32)
    gbase3 = pad_start[None, None, :] + gprev3[:, :, :n_e]
    tabs = (flat(cnt3), flat(tstart3), flat(gbase3))

    after_first = gprev3[tile0s[1], 0, :n_e] if len(groups) > 1 else counts
    fill_start = (pad_start + after_first).astype(I32)
    fill_tabs = (fill_start, (pad_end - fill_start).astype(I32), n_used.reshape(1))
    xs, lslots = None, []
    for (x1, h2, te, tw), n, t0 in zip(groups, sizes, tile0s):
        ls = lslot[t0 * tp:t0 * tp + n]
        lslots.append(ls)
        xs = _dispatch(tabs, fill_tabs, t0, ls, h2, xs, n_e, nb * bm, bm)
    yb = _experts(xs, block_e, n_used.reshape(1), w_gate, b_gate, w_up, b_up, w_down, b_down)
    return [_combine(tabs, t0, ls, tw, x1, yb, n_e)
            for (x1, h2, te, tw), ls, t0 in zip(groups, lslots, tile0s)]


def _row_tile(n, want):
    return want if n % want == 0 else n


def kernel(x_prompt, x_sample, mem_prompt, cache_k, cache_v, cache_idx_k, cache_mem_k, cache_mem_v, page_table,
           g_in, w_in, g_qa, g_ka, g_vb, w_s, b_s, g_qm, g_mem, w_mem_kv, g_km, w_br_a, w_br_b, w_br_m, w_out,
           g_ffn, w_router, b_router, w_gate, b_gate, w_up, b_up, w_down, b_down):
    bp, t, d = x_prompt.shape
    bs, ts, _ = x_sample.shape
    assert bp == 1 and ts == 1 and t % CHUNK == 0 and cache_k.shape[1] == LANES
    n_mem = mem_prompt.shape[1]
    n_pages = page_table.shape[1]
    ps = cache_k.shape[1]
    past = n_pages * ps
    n_e = w_router.shape[1]

    fp = _prep_params(w_in, g_in, g_qa, g_ka)
    mp = _prep_mix_params(w_in, g_in, g_vb, g_qm, w_br_a, w_br_b, w_br_m, w_out, g_ffn, w_router, b_router)

    xp = x_prompt.reshape(t, d)
    qa_p, k32_p, khm_p, v32_p, vt_p, qi_p, ki32_p, kibf_p, wit_p = _front(
        xp, _rope_tables(jnp.arange(t, dtype=I32)), *fp, tm=_row_tile(t, 512), paged=True)
    mem_k, mem_v = _memory_kv(mem_prompt.reshape(n_mem, d), g_mem.reshape(1, d), w_mem_kv.astype(BF16),
                              g_km.reshape(1, HEAD_DIM_M))
    out_a_p = _prompt_attention(qi_p, qa_p, wit_p, kibf_p, khm_p, vt_p)
    x1_p, h2_p, te_p, tw_p = _mix(xp, out_a_p, mp, _prep_gmlp_chunked(w_s, b_s), mem_k, mem_v,
                                  tm=_row_tile(t, 256), chunked=True, shared_mem=True, emit_vb=False, n_experts=n_e)

    xs = x_sample.reshape(bs, d)
    qa_s, k32_s, _, v32_s, _, qi_s, ki32_s, _, wit_s = _front(
        xs, _rope_tables(jnp.full((bs,), past, I32)), *fp, tm=bs, paged=False)
    out_a_s = _sample_attention(page_table, jnp.moveaxis(qi_s, 0, 1), jnp.moveaxis(qa_s, 0, 1),
                                wit_s.T.reshape(bs, N_HEADS_IDX, 1), ki32_s.reshape(bs, 1, D_IDX),
                                k32_s.reshape(bs, 1, WA_KV), v32_s.reshape(bs, 1, WA_KV),
                                cache_idx_k, cache_k, cache_v)
    x1_s, h2_s, te_s, tw_s, vb_s = _mix(xs, out_a_s.reshape(bs, WA_Q), mp, _prep_gmlp_single(w_s, b_s),
                                        cache_mem_k.reshape(bs, n_mem, WM_Q), cache_mem_v.reshape(bs, n_mem, WM_Q),
                                        tm=_row_tile(bs, SUBLANES), chunked=False, shared_mem=False, emit_vb=True,
                                        n_experts=n_e)

    y_p, y_s = _moe([(x1_p, h2_p, te_p, tw_p), (x1_s, h2_s, te_s, tw_s)],
                    w_gate, b_gate, w_up, b_up, w_down, b_down)

    n_pg = t // ps
    paged_kv = lambda a: jnp.transpose(a.reshape(1, n_pg, N_KV_A, HEAD_DIM_A, ps), (0, 1, 4, 2, 3))
    return (y_p.reshape(1, t, d), y_s.reshape(bs, 1, d),
            paged_kv(k32_p), paged_kv(v32_p), jnp.transpose(ki32_p, (0, 2, 1)).reshape(1, n_pg, ps, D_IDX),
            mem_k.reshape(1, n_mem, N_HEADS_M, HEAD_DIM_M), mem_v.reshape(1, n_mem, N_HEADS_M, HEAD_DIM_M),
            k32_s.reshape(bs, 1, N_KV_A, HEAD_DIM_A), v32_s.reshape(bs, 1, N_KV_A, HEAD_DIM_A),
            ki32_s.reshape(bs, 1, D_IDX), vb_s.reshape(bs, 1, WIDTH_B))
```

```python
import functools

import jax
import jax.numpy as jnp
from jax import lax
from jax.experimental import pallas as pl
from jax.experimental.pallas import tpu as pltpu

F32 = jnp.float32
BF16 = jnp.bfloat16
I32 = jnp.int32

N_HEADS_A = 8
N_KV_A = 2
HEAD_DIM_A = 64
TOPK_MAX = 256
N_HEADS_IDX = 8
D_IDX = 64
N_GROUPS_B = 4
GROUP_DIM_B = 128
WIDTH_B = N_GROUPS_B * GROUP_DIM_B
CHUNK = 128
N_HEADS_M = 4
HEAD_DIM_M = 128
ROPE_THETA = 500000.0
ROT_FRACTION = 4
N_BRANCH = 3
TOP_K_MOE = 4
SWIGLU_LIMIT = 7.0
SWIGLU_ALPHA = 1.702
EPS = 1e-6

WA_Q = N_HEADS_A * HEAD_DIM_A
WA_KV = N_KV_A * HEAD_DIM_A
WI_Q = N_HEADS_IDX * D_IDX
WM_Q = N_HEADS_M * HEAD_DIM_M
GQA = N_HEADS_A // N_KV_A

LANES = 128
SUBLANES = 8
VMEM_LIMIT = 56 * 1024 * 1024

LOG2_E = 1.4426950408889634
INT_MIN = -(2 ** 31)
INT_MAX = 2 ** 31 - 1
NEG = -1e30

VT_ROWS = HEAD_DIM_A + 16
ATTN_BQ = 128
ATTN_SC = 512
MOE_BM = 512
FF_CHUNK = 512


def _cparams(n_axes=1, vmem=VMEM_LIMIT):
    return pltpu.CompilerParams(dimension_semantics=("arbitrary",) * n_axes, vmem_limit_bytes=vmem)


def _const_spec(shape, single=False):
    zeros = (0,) * len(shape)
    if single:
        return pl.BlockSpec(shape, lambda *_: zeros, pipeline_mode=pl.Buffered(1))
    return pl.BlockSpec(shape, lambda *_: zeros)


def _dot(a, b):
    return jnp.dot(a, b, preferred_element_type=F32)


def _dot_nt(a, b):
    return lax.dot_general(a, b, (((1,), (1,)), ((), ())), preferred_element_type=F32)


def _rms(x, g):
    return x * lax.rsqrt(jnp.mean(x * x, axis=-1, keepdims=True) + EPS) * g


def _split_dot(x, w):
    hi = x.astype(BF16)
    lo = (x - hi.astype(F32)).astype(BF16)
    return _dot(hi, w) + _dot(lo, w)


def _front_kernel(x_ref, gin_ref, wa_ref, wwit_ref, wvt_ref, bd_ref, gqa_ref, gka_ref, cos_ref, sa_ref, sb_ref,
                  qa_ref, k32_ref, khm_ref, v32_ref, vt_ref, qi_ref, ki32_ref, kibf_ref, wit_ref, *, score_scale, paged):
    x = x_ref[...]
    h = _rms(x, gin_ref[...]).astype(BF16)
    p = _dot(h, wa_ref[...])
    cos, sa, sb = cos_ref[...], sa_ref[...], sb_ref[...]
    bd = bd_ref[...]

    def head_norm(v, g):
        ssq = _split_dot(v * v, bd)
        return v * lax.rsqrt(ssq * (1.0 / HEAD_DIM_A) + EPS) * g

    def rope(v):
        return v * cos + pltpu.roll(v, LANES - 8, 1) * sa + pltpu.roll(v, 8, 1) * sb

    for s in range(WA_Q // LANES):
        v = p[:, s * LANES:(s + 1) * LANES]
        v = rope(head_norm(v, gqa_ref[...])) * (HEAD_DIM_A ** -0.5 * LOG2_E)
        vb = v.astype(BF16)
        qa_ref[2 * s] = vb[:, :HEAD_DIM_A]
        qa_ref[2 * s + 1] = vb[:, HEAD_DIM_A:]
    o = WA_Q
    n_pg = x.shape[0] // LANES
    k = rope(head_norm(p[:, o:o + WA_KV], gka_ref[...]))
    if paged:
        for j in range(n_pg):
            k32_ref[j] = k[j * LANES:(j + 1) * LANES, :].T
    else:
        k32_ref[...] = k
    kb = k.astype(BF16)
    khm_ref[0] = kb[:, :HEAD_DIM_A]
    khm_ref[1] = kb[:, HEAD_DIM_A:]
    o += WA_KV
    if not paged:
        v32_ref[...] = p[:, o:o + WA_KV]
    o += WA_KV
    for s in range(WI_Q // LANES):
        vb = rope(p[:, o + s * LANES:o + (s + 1) * LANES]).astype(BF16)
        qi_ref[2 * s] = vb[:, :D_IDX]
        qi_ref[2 * s + 1] = vb[:, D_IDX:]
    o += WI_Q
    ki_slab = rope(p[:, o:o + LANES])
    ki = ki_slab[:, :D_IDX]
    if paged:
        for j in range(n_pg):
            ki32_ref[j] = ki_slab[j * LANES:(j + 1) * LANES, :].T[:D_IDX, :]
    else:
        ki32_ref[...] = ki
    kibf_ref[...] = ki.astype(BF16)
    wit_ref[...] = _dot_nt(wwit_ref[...], h) * score_scale
    vt = _dot_nt(wvt_ref[...], h)
    if paged:
        for j in range(n_pg):
            v32_ref[j] = vt[:, j * LANES:(j + 1) * LANES]
    tail = jnp.where(lax.broadcasted_iota(I32, (VT_ROWS - HEAD_DIM_A, vt.shape[1]), 0) == 0, 1.0, 0.0)
    for g in range(N_KV_A):
        vt_ref[g] = jnp.concatenate([vt[g * HEAD_DIM_A:(g + 1) * HEAD_DIM_A], tail], axis=0).astype(BF16)


def _front(x2d, tabs, gin, wa, wwit, wvt, bd, gqa, gka, tm, paged):
    n, d = x2d.shape
    cos, sa, sb = tabs
    row = lambda w: pl.BlockSpec((tm, w), lambda i: (i, 0))
    hm = lambda nh, w: pl.BlockSpec((nh, tm, w), lambda i: (0, i, 0))
    if paged:
        assert tm % LANES == 0
        f32_out = lambda w: jax.ShapeDtypeStruct((n // LANES, w, LANES), F32)
        f32_spec = lambda w: pl.BlockSpec((tm // LANES, w, LANES), lambda i: (i, 0, 0))
    else:
        f32_out = lambda w: jax.ShapeDtypeStruct((n, w), F32)
        f32_spec = row
    out_shape = (
        jax.ShapeDtypeStruct((N_HEADS_A, n, HEAD_DIM_A), BF16),
        f32_out(WA_KV),
        jax.ShapeDtypeStruct((N_KV_A, n, HEAD_DIM_A), BF16),
        f32_out(WA_KV),
        jax.ShapeDtypeStruct((N_KV_A, VT_ROWS, n), BF16),
        jax.ShapeDtypeStruct((N_HEADS_IDX, n, D_IDX), BF16),
        f32_out(D_IDX),
        jax.ShapeDtypeStruct((n, D_IDX), BF16),
        jax.ShapeDtypeStruct((N_HEADS_IDX, n), F32),
    )
    out_specs = (hm(N_HEADS_A, HEAD_DIM_A), f32_spec(WA_KV), hm(N_KV_A, HEAD_DIM_A), f32_spec(WA_KV),
                 pl.BlockSpec((N_KV_A, VT_ROWS, tm), lambda i: (0, 0, i)), hm(N_HEADS_IDX, D_IDX), f32_spec(D_IDX),
                 row(D_IDX), pl.BlockSpec((N_HEADS_IDX, tm), lambda i: (0, i)))
    in_specs = [row(d), _const_spec(gin.shape), _const_spec(wa.shape), _const_spec(wwit.shape),
                _const_spec(wvt.shape), _const_spec(bd.shape), _const_spec(gqa.shape), _const_spec(gka.shape),
                row(LANES), row(LANES), row(LANES)]
    kern = functools.partial(_front_kernel, score_scale=D_IDX ** -0.5 * N_HEADS_IDX ** -0.5, paged=paged)
    return pl.pallas_call(kern, out_shape=out_shape, grid=(n // tm,), in_specs=in_specs, out_specs=out_specs,
                          compiler_params=_cparams(), name="front")(x2d, gin, wa, wwit, wvt, bd, gqa, gka, cos, sa, sb)


def _memkv_kernel(mem_ref, gmem_ref, w_ref, gkm_ref, k32_ref, v32_ref):
    h = _rms(mem_ref[...], gmem_ref[...]).astype(BF16)
    kv = _dot(h, w_ref[...])
    for hh in range(N_HEADS_M):
        sl = slice(hh * HEAD_DIM_M, (hh + 1) * HEAD_DIM_M)
        k32_ref[:, sl] = _rms(kv[:, sl], gkm_ref[...])
    v32_ref[...] = kv[:, WM_Q:]


def _memory_kv(mem2d, gmem, w, gkm):
    n = mem2d.shape[0]
    out = jax.ShapeDtypeStruct((n, WM_Q), F32)
    return pl.pallas_call(_memkv_kernel, out_shape=(out, out), name="memory_kv",
                          compiler_params=pltpu.CompilerParams(vmem_limit_bytes=VMEM_LIMIT))(mem2d, gmem, w, gkm)


def _sort_key(s):
    s = jnp.where(s == 0.0, 0.0, s)
    bits = pltpu.bitcast(s, I32)
    return bits ^ ((bits >> 31) & INT_MAX)


KEYS_PER_GROUP = 32 * SUBLANES
SEARCH_SLAB = 128


def _bit_planes(words):
    a = list(words)
    j, m = 16, 0x0000FFFF
    while j:
        k = 0
        while k < 32:
            t = (a[k] ^ lax.shift_right_logical(a[k + j], jnp.int32(j))) & m
            a[k] = a[k] ^ t
            a[k + j] = a[k + j] ^ (t << j)
            k = (k + j + 1) & ~j
        j >>= 1
        m = (m ^ (m << j)) & 0xFFFFFFFF if j else m
    return a


ATTN_SUB = 256


def _attn_kernel(qi_ref, qa_ref, wi_ref, ki_ref, k_ref, vt_ref, out_ref,
                 keys_ref, planes_ref, cand_ref, cut_ref, bias_ref, lg_ref, cm_ref, m_ref, acc_ref, *, k_sel, idx_bits):
    bq, sc = ATTN_BQ, ATTN_SC
    wpc = sc // 32
    i = pl.program_id(0)
    n_ch = (i * bq) // sc + 1
    t_pos = i * bq + lax.broadcasted_iota(I32, (1, bq), 1)
    sub = lax.broadcasted_iota(I32, (sc, 1), 0)
    qi = qi_ref[...].reshape(N_HEADS_IDX * bq, D_IDX)
    w = wi_ref[...]

    @pl.when(i == 0)
    def _():
        planes_ref[...] = jnp.zeros(planes_ref.shape, I32)

    def chunk(c):
        return pl.ds(pl.multiple_of(c * sc, sc), sc)

    def score_chunk(c, carry):
        off = pl.multiple_of(c * sc, sc)
        d = _dot_nt(ki_ref[chunk(c), :], qi)
        s = jnp.zeros((sc, bq), F32)
        for h in range(N_HEADS_IDX):
            s = s + jnp.maximum(d[:, h * bq:(h + 1) * bq], 0.0) * w[h:h + 1, :]
        key = jnp.where(off + sub <= t_pos, _sort_key(s), INT_MIN)
        keys_ref[chunk(c), :] = key
        u = key ^ INT_MIN
        for grp in range(sc // KEYS_PER_GROUP):
            base = grp * KEYS_PER_GROUP
            words = _bit_planes([u[base + SUBLANES * v:base + SUBLANES * (v + 1), :] for v in range(32)])
            rows = pl.ds(pl.multiple_of(c * wpc + grp * SUBLANES, SUBLANES), SUBLANES)
            for p in range(32):
                planes_ref[p, rows, :] = words[p]
        return carry

    lax.fori_loop(0, n_ch // 2, lambda j, cr: score_chunk(2 * j + 1, score_chunk(2 * j, cr)), 0)
    lax.fori_loop(n_ch - n_ch % 2, n_ch, score_chunk, 0)

    def count(pred):
        def body(c, acc):
            hit = jnp.where(pred(keys_ref[chunk(c), :], c * sc + sub), 1, 0)
            return acc + hit.reshape(sc // SUBLANES, SUBLANES, bq).sum(axis=0)

        def body2(j, acc):
            return body(2 * j + 1, body(2 * j, acc))
        acc = lax.fori_loop(0, n_ch // 2, body2, jnp.zeros((SUBLANES, bq), I32))
        acc = lax.fori_loop(n_ch - n_ch % 2, n_ch, body, acc)
        return acc.sum(axis=0, keepdims=True)

    slab = SEARCH_SLAB
    n_slab = (n_ch * wpc + slab - 1) // slab
    srow = lax.broadcasted_iota(I32, (slab, 1), 0)

    def slab_rows(sl):
        return pl.ds(pl.multiple_of(sl * slab, slab), slab)

    def init_cand(sl, carry):
        cand_ref[slab_rows(sl), :] = jnp.where(sl * slab + srow < n_ch * wpc, -1, 0) + jnp.zeros((slab, bq), I32)
        return carry

    lax.fori_loop(0, n_slab, init_cand, 0)

    def sweep(prev, cur):
        def body(sl, acc):
            r = slab_rows(sl)
            cand = cand_ref[r, :]
            if prev is not None:
                ones = cand & planes_ref[prev[0], r, :]
                cand = jnp.where(prev[1] != 0, ones, cand ^ ones)
                cand_ref[r, :] = cand
            hits = cand if cur is None else cand & planes_ref[cur, r, :]
            cnt = lax.population_count(hits)
            return acc + cnt.reshape(slab // SUBLANES, SUBLANES, bq).sum(axis=0)
        acc = lax.fori_loop(0, n_slab, body, jnp.zeros((SUBLANES, bq), I32))
        return acc.sum(axis=0, keepdims=True)

    def decide(ones, p, need, thr_u):
        keep = ones >= need
        bit = jnp.int32(1) << (31 - p)
        return jnp.where(keep, 1, 0), jnp.where(keep, need, need - ones), jnp.where(keep, thr_u | bit, thr_u)

    keep, need, thr_u = decide(sweep(None, 0), 0, jnp.full((1, bq), k_sel, I32), jnp.zeros((1, bq), I32))

    def search_pass(p, st):
        keep, need, thr_u = st
        return decide(sweep((p - 1, keep), p), p, need, thr_u)

    keep, need, thr_u = lax.fori_loop(1, 32, search_pass, (keep, need, thr_u))
    tied = sweep((31, keep), None)
    thr = thr_u ^ INT_MIN
    cnt = (k_sel - need) + tied
    cut_ref[...] = jnp.full((1, bq), INT_MAX, I32)

    excess = jnp.logical_and(cnt > k_sel, thr > INT_MIN)
    has_ties = jnp.max(jnp.where(excess, 1, 0)) > 0

    @pl.when(has_ties)
    def _():
        want = k_sel - count(lambda blk, _: blk > thr)

        def tie_body(j, cpos):
            cand = cpos + (jnp.int32(1) << (idx_bits - 1 - j))
            below = count(lambda blk, pos: jnp.where(blk == thr, pos, INT_MAX) < cand)
            return jnp.where(below < want, cand, cpos)

        cpos = lax.fori_loop(0, idx_bits, tie_body, jnp.zeros((1, bq), I32))
        cut_ref[...] = jnp.where(excess, cpos, INT_MAX)

    m_ref[...] = jnp.full(m_ref.shape, NEG, F32)
    acc_ref[...] = jnp.zeros(acc_ref.shape, F32)
    cut = cut_ref[...]
    floor = jnp.where(thr == INT_MIN, INT_MIN, thr - 1)
    n_sub = sc // ATTN_SUB

    qs = [qa_ref[g * GQA:(g + 1) * GQA].reshape(GQA * bq, HEAD_DIM_A) for g in range(N_KV_A)]
    neg_row = jnp.full((1, GQA * bq), NEG, F32)

    def stage(c, par, cb, par_b):
        do_a, do_b = c is not None, cb is not None
        if do_a:
            off = pl.multiple_of(c * sc, sc)
            keyc = keys_ref[chunk(c), :]

            @pl.when(has_ties)
            def _():
                tie = jnp.where(off + sub <= cut, 0.0, NEG)
                bias = jnp.where(keyc > thr, 0.0, jnp.where(keyc == thr, tie, NEG))
                bias_ref[...] = jnp.where(keyc == INT_MIN, NEG, bias)

            @pl.when(jnp.logical_not(has_ties))
            def _():
                bias_ref[...] = jnp.where(keyc > floor, 0.0, NEG)
            cmax = [neg_row] * N_KV_A
        if do_b:
            off_b = pl.multiple_of(cb * sc, sc)
            m_new, acc = [], []
            for g in range(N_KV_A):
                m_old = m_ref[g]
                m_new.append(jnp.maximum(m_old, cm_ref[par_b, g]))
                acc.append(jnp.exp2(m_old - m_new[g]) * acc_ref[g])
                m_ref[g] = m_new[g]
        for r in range(n_sub):
            rows = slice(r * ATTN_SUB, (r + 1) * ATTN_SUB)
            if do_a:
                bias4 = jnp.concatenate([bias_ref[rows, :]] * GQA, axis=1)
                for g in range(N_KV_A):
                    lg = _dot_nt(k_ref[g, pl.ds(off + r * ATTN_SUB, ATTN_SUB), :], qs[g]) + bias4
                    lg_ref[par, g, rows, :] = lg
                    cmax[g] = jnp.maximum(cmax[g], lg.max(axis=0, keepdims=True))
            if do_b:
                for g in range(N_KV_A):
                    p = jnp.exp2(lg_ref[par_b, g, rows, :] - m_new[g])
                    vt = vt_ref[g, :, pl.ds(off_b + r * ATTN_SUB, ATTN_SUB)]
                    acc[g] = acc[g] + _dot(vt, p.astype(BF16))
        if do_b:
            for g in range(N_KV_A):
                acc_ref[g] = acc[g]
        if do_a:
            for g in range(N_KV_A):
                cm_ref[par, g] = cmax[g]

    stage(jnp.int32(0), 0, None, None)

    def stage_pair(j, carry):
        stage(2 * j + 1, 1, 2 * j, 0)
        stage(2 * j + 2, 0, 2 * j + 1, 1)
        return carry

    last = n_ch - 1
    lax.fori_loop(0, last // 2, stage_pair, 0)

    @pl.when(last % 2 == 1)
    def _():
        stage(last, 1, last - 1, 0)
        stage(None, None, last, 1)

    @pl.when(last % 2 == 0)
    def _():
        stage(None, None, last, 0)

    heads = []
    for g in range(N_KV_A):
        acc = acc_ref[g]
        o = acc[:HEAD_DIM_A] / acc[HEAD_DIM_A:HEAD_DIM_A + 1]
        heads += [o[:, hh * bq:(hh + 1) * bq] for hh in range(GQA)]
    out_ref[...] = jnp.concatenate(heads, axis=0).T.astype(out_ref.dtype)


def _prompt_attention(qi_hm, qa_hm, wit, kibf, khm, vt):
    t = kibf.shape[0]
    bq = ATTN_BQ
    k_sel = min(TOPK_MAX, t // 4)
    kern = functools.partial(_attn_kernel, k_sel=k_sel, idx_bits=max(1, (t - 1).bit_length()))
    in_specs = [pl.BlockSpec((N_HEADS_IDX, bq, D_IDX), lambda i: (0, i, 0)),
                pl.BlockSpec((N_HEADS_A, bq, HEAD_DIM_A), lambda i: (0, i, 0)),
                pl.BlockSpec((N_HEADS_IDX, bq), lambda i: (0, i)),
                _const_spec(kibf.shape, True), _const_spec(khm.shape, True), _const_spec(vt.shape, True)]
    n_words = -(-(t // 32) // SEARCH_SLAB) * SEARCH_SLAB
    scratch = [pltpu.VMEM((t, bq), I32), pltpu.VMEM((32, n_words, bq), I32), pltpu.VMEM((n_words, bq), I32),
               pltpu.VMEM((1, bq), I32), pltpu.VMEM((ATTN_SC, bq), F32),
               pltpu.VMEM((2, N_KV_A, ATTN_SC, GQA * bq), F32), pltpu.VMEM((2, N_KV_A, 1, GQA * bq), F32),
               pltpu.VMEM((N_KV_A, 1, GQA * bq), F32),
               pltpu.VMEM((N_KV_A, VT_ROWS, GQA * bq), F32)]
    return pl.pallas_call(kern, out_shape=jax.ShapeDtypeStruct((t, WA_Q), BF16), grid=(t // bq,),
                          in_specs=in_specs, out_specs=pl.BlockSpec((bq, WA_Q), lambda i: (i, 0)),
                          scratch_shapes=scratch, compiler_params=_cparams(), name="attn")(
                              qi_hm, qa_hm, wit, kibf, khm, vt)


SAMPLE_CK = 2048


def _sample_attn_kernel(pt_ref, qi_ref, qa_ref, wi_ref, kin_ref, kn_ref, vn_ref, cidx_hbm, ck_hbm, cv_hbm, out_ref,
                        idx_buf, k_buf, v_buf, scr_ref, cut_ref, sems, *, n_pages, ps, ck, k_sel, idx_bits):
    b = pl.program_id(0)
    nb = pl.num_programs(0)
    past = n_pages * ps
    nc = past // ck

    def page_copy(which, bb, p):
        src, dst = ((cidx_hbm, idx_buf), (ck_hbm, k_buf), (cv_hbm, v_buf))[which]
        cols = pl.ds(pl.multiple_of(p * ps, ps), ps)
        dst = dst.at[:, cols] if which == 0 else dst.at[:, :, cols]
        return pltpu.make_async_copy(src.at[pt_ref[bb, p]], dst, sems.at[which])

    def for_pages(fn):
        lax.fori_loop(0, n_pages, lambda p, c: (fn(p), c)[1], 0, unroll=8)

    def wait_all(which):
        buf = (idx_buf, k_buf, v_buf)[which]
        pltpu.make_async_copy(buf, buf, sems.at[which]).wait()

    @pl.when(b == 0)
    def _():
        for_pages(lambda p: page_copy(0, b, p).start())

    for_pages(lambda p: (page_copy(1, b, p).start(), page_copy(2, b, p).start()))
    wait_all(0)

    qi = qi_ref[0]
    w = wi_ref[0]
    for c in range(nc):
        d = _dot(qi, idx_buf[:, c * ck:(c + 1) * ck].astype(BF16))
        scr_ref[c:c + 1, :] = (jnp.maximum(d, 0.0) * w).sum(axis=0, keepdims=True)
    d_new = (qi.astype(F32) * kin_ref[0].astype(BF16).astype(F32)).sum(axis=-1, keepdims=True)
    s_new = (jnp.maximum(d_new, 0.0) * w).sum(axis=0, keepdims=True)

    @pl.when(b + 1 < nb)
    def _():
        for_pages(lambda p: page_copy(0, b + 1, p).start())

    keys = _sort_key(scr_ref[...])
    key_new = _sort_key(s_new)
    pos = (lax.broadcasted_iota(I32, keys.shape, 0) * ck + lax.broadcasted_iota(I32, keys.shape, 1))

    def count(pred):
        hit = jnp.where(pred(keys, pos), 1, 0)
        parts = [hit[:, j * LANES:(j + 1) * LANES] for j in range(ck // LANES)]
        while len(parts) > 1:
            parts = [a + b for a, b in zip(parts[::2], parts[1::2])] + parts[len(parts) - len(parts) % 2:]
        total = parts[0].sum(axis=1, keepdims=True).sum(axis=0, keepdims=True)
        return total + jnp.where(pred(key_new, past), 1, 0)

    dcol = lax.broadcasted_iota(I32, (16, 1), 0)

    def search_body(j, st):
        thr, cnt = st
        step = jnp.int32(1) << (28 - 4 * j)
        rows = [jnp.zeros((1, LANES), I32)]
        for d in range(1, 16):
            hit = jnp.where(keys >= thr + d * step, 1, 0)
            parts = [hit[:, t * LANES:(t + 1) * LANES] for t in range(ck // LANES)]
            while len(parts) > 1:
                parts = [a + b for a, b in zip(parts[::2], parts[1::2])] + parts[len(parts) - len(parts) % 2:]
            rows.append(parts[0].sum(axis=0, keepdims=True))
        cands = thr + dcol * step
        cnts = jnp.concatenate(rows, axis=0).sum(axis=1, keepdims=True) + jnp.where(key_new >= cands, 1, 0)
        ok = jnp.logical_and(cnts >= k_sel, dcol >= 1)
        digit = jnp.where(ok, 1, 0).sum(axis=0, keepdims=True)
        best = jnp.where(ok, cnts, INT_MAX).min(axis=0, keepdims=True)
        return thr + digit * step, jnp.where(digit > 0, best, cnt)

    st0 = (jnp.full((1, 1), INT_MIN, I32), jnp.full((1, 1), past + 1, I32))
    thr, cnt = lax.fori_loop(0, 8, search_body, st0)
    cut_ref[...] = jnp.full((1, 1), INT_MAX, I32)

    @pl.when(jnp.max(jnp.where(cnt > k_sel, 1, 0)) > 0)
    def _():
        want = k_sel - count(lambda kk, _: kk > thr)

        def tie_body(j, cpos):
            cand = cpos + (jnp.int32(1) << (idx_bits - 1 - j))
            below = count(lambda kk, pp: jnp.where(kk == thr, pp, INT_MAX) < cand)
            return jnp.where(below < want, cand, cpos)

        cut_ref[...] = lax.fori_loop(0, idx_bits, tie_body, jnp.zeros((1, 1), I32))

    cut = cut_ref[...]

    def sel_bias(kk, pp):
        tie = jnp.where(pp <= cut, 0.0, NEG)
        return jnp.where(kk > thr, 0.0, jnp.where(kk == thr, tie, NEG))

    bias = sel_bias(keys, pos)
    bias_new = sel_bias(key_new, past)

    wait_all(1)
    wait_all(2)

    qa = qa_ref[0]
    heads = []
    for g in range(N_KV_A):
        qg = qa[g * GQA:(g + 1) * GQA]
        cols = slice(g * HEAD_DIM_A, (g + 1) * HEAD_DIM_A)
        spans = [slice(c * ck, (c + 1) * ck) for c in range(nc)]
        lgs = [_dot(qg, k_buf[g, :, sp].astype(BF16)) + bias[c:c + 1, :] for c, sp in enumerate(spans)]
        kn = kn_ref[0][:, cols].astype(BF16).astype(F32)
        vn = vn_ref[0][:, cols].astype(BF16).astype(F32)
        lg_new = (qg.astype(F32) * kn).sum(axis=-1, keepdims=True) + bias_new
        m = lg_new
        for lg in lgs:
            m = jnp.maximum(m, lg.max(axis=-1, keepdims=True))
        p_new = jnp.exp2(lg_new - m)
        l = p_new
        acc = p_new * vn
        for lg, sp in zip(lgs, spans):
            p = jnp.exp2(lg - m)
            l = l + p.sum(axis=-1, keepdims=True)
            acc = acc + _dot_nt(p.astype(BF16), v_buf[g, :, sp].astype(BF16))
        o = acc / l
        heads += [o[hh:hh + 1, :] for hh in range(GQA)]
    out_ref[0] = jnp.concatenate(heads, axis=1)


def _sample_attention(page_table, qi_s, qa_s, wi_s, ki_new, k_new, v_new, cache_idx_k, cache_k, cache_v):
    nbatch, n_pages = page_table.shape
    n_pool, ps, d_idx = cache_idx_k.shape
    past = n_pages * ps
    ck = min(SAMPLE_CK, past)
    assert past % ck == 0 and ck % ps == 0
    k_sel = min(TOPK_MAX, (past + 1) // 4)
    kern = functools.partial(_sample_attn_kernel, n_pages=n_pages, ps=ps, ck=ck, k_sel=k_sel,
                             idx_bits=past.bit_length())
    per_b = lambda shape: pl.BlockSpec((1,) + shape, lambda b, pt: (b, 0, 0))
    any_spec = pl.BlockSpec(memory_space=pl.ANY)
    grid_spec = pltpu.PrefetchScalarGridSpec(
        num_scalar_prefetch=1, grid=(nbatch,),
        in_specs=[per_b((N_HEADS_IDX, d_idx)), per_b((N_HEADS_A, HEAD_DIM_A)), per_b((N_HEADS_IDX, 1)),
                  per_b((1, d_idx)), per_b((1, WA_KV)), per_b((1, WA_KV)), any_spec, any_spec, any_spec],
        out_specs=per_b((1, WA_Q)),
        scratch_shapes=[pltpu.VMEM((d_idx, past), F32), pltpu.VMEM((N_KV_A, HEAD_DIM_A, past), F32),
                        pltpu.VMEM((N_KV_A, HEAD_DIM_A, past), F32),
                        pltpu.VMEM((past // ck, ck), F32), pltpu.VMEM((1, 1), I32),
                        pltpu.SemaphoreType.DMA((3,))])
    return pl.pallas_call(kern, out_shape=jax.ShapeDtypeStruct((nbatch, 1, WA_Q), F32), grid_spec=grid_spec,
                          compiler_params=_cparams(), name="sample_attn")(
                              page_table, qi_s, qa_s, wi_s, ki_new, k_new, v_new,
                              jnp.transpose(cache_idx_k, (0, 2, 1)), jnp.transpose(cache_k, (0, 2, 3, 1)),
                              jnp.transpose(cache_v, (0, 2, 3, 1)))


def _rope_tables(pos):
    rot = HEAD_DIM_A // ROT_FRACTION
    half = rot // 2
    inv_freq = ROPE_THETA ** (-jnp.arange(half, dtype=F32) / half)
    ang = pos.astype(F32)[:, None] * inv_freq[None, :]
    cos, sin = jnp.cos(ang), jnp.sin(ang)
    n = pos.shape[0]
    z = lambda w: jnp.zeros((n, w), F32)
    c = jnp.concatenate([cos, cos, jnp.ones((n, HEAD_DIM_A - rot), F32)], axis=1)
    sa = jnp.concatenate([-sin, z(HEAD_DIM_A - half)], axis=1)
    sb = jnp.concatenate([z(half), sin, z(HEAD_DIM_A - rot)], axis=1)
    rep = LANES // HEAD_DIM_A
    return tuple(jnp.tile(a, (1, rep)) for a in (c, sa, sb))


def _prep_params(w_in, g_in, g_qa, g_ka):
    d = w_in.shape[0]
    w_qa, w_ka, w_va, w_qi, w_ki, w_wi = _split_w_in(w_in, d)[:6]
    wa = jnp.concatenate([w_qa, w_ka, w_va, w_qi, w_ki, jnp.zeros((d, LANES - D_IDX), w_in.dtype)], axis=1)
    lane = jnp.arange(LANES)
    bd = (lane[:, None] // HEAD_DIM_A == lane[None, :] // HEAD_DIM_A).astype(BF16)
    rep = LANES // HEAD_DIM_A
    return (g_in.reshape(1, d), wa.astype(BF16), w_wi.T.astype(BF16), w_va.T.astype(BF16), bd,
            jnp.tile(g_qa, rep).reshape(1, LANES), jnp.tile(g_ka, rep).reshape(1, LANES))


def _split_w_in(w_in, d):
    sizes = (WA_Q, WA_KV, WA_KV, WI_Q, D_IDX, N_HEADS_IDX, WIDTH_B, WIDTH_B, WM_Q, N_BRANCH * d)
    parts, o = [], 0
    for s in sizes:
        parts.append(w_in[:, o:o + s])
        o += s
    return parts


def _mix_kernel(*refs, chunked, shared_mem, emit_vb, n_experts):
    it = iter(refs)
    x_ref, oa_ref, gin_ref, wb_ref, gvb_ref, gqm_ref = (next(it) for _ in range(6))
    if chunked:
        wtril_ref, bt_ref = next(it), next(it)
    else:
        wdiag_ref, bdiag_ref = next(it), next(it)
    mk_ref, mv_ref = next(it), next(it)
    wbra_ref, wbrb_ref, wbrm_ref, wout_ref, gffn_ref, wrh_ref, wrl_ref, br_ref = (next(it) for _ in range(8))
    x1_ref, h2_ref, te_ref, tw_ref = (next(it) for _ in range(4))
    vb_ref = next(it) if emit_vb else None

    x = x_ref[...]
    tm, d = x.shape
    h = _rms(x, gin_ref[...]).astype(BF16)
    p = _dot(h, wb_ref[...])
    ub = p[:, :WIDTH_B]
    vb = _rms(p[:, WIDTH_B:2 * WIDTH_B], gvb_ref[...])
    if emit_vb:
        vb_ref[...] = vb
    o = 2 * WIDTH_B
    qm = p[:, o:o + WM_Q]
    o += WM_Q
    gates = p[:, o:o + N_BRANCH * d]

    if chunked:
        vbb = vb.astype(BF16)
        rows = []
        for cc in range(tm // CHUNK):
            cols = []
            for g in range(N_GROUPS_B):
                vg = vbb[cc * CHUNK:(cc + 1) * CHUNK, g * GROUP_DIM_B:(g + 1) * GROUP_DIM_B]
                cols.append(_dot(wtril_ref[g], vg))
            rows.append(jnp.concatenate(cols, axis=1) + bt_ref[...])
        z = jnp.concatenate(rows, axis=0) if len(rows) > 1 else rows[0]
    else:
        z = vb * wdiag_ref[...] + bdiag_ref[...]
    out_b = ub * z

    scale_m = HEAD_DIM_M ** -0.5
    outs = []
    for hh in range(N_HEADS_M):
        sl = slice(hh * HEAD_DIM_M, (hh + 1) * HEAD_DIM_M)
        qh = _rms(qm[:, sl], gqm_ref[...])
        if shared_mem:
            lg = _dot_nt(qh.astype(BF16), mk_ref[:, sl].astype(BF16)) * scale_m
            pm = jnp.exp(lg - lg.max(axis=-1, keepdims=True))
            pm = pm / pm.sum(axis=-1, keepdims=True)
            outs.append(_dot(pm.astype(BF16), mv_ref[:, sl].astype(BF16)))
        else:
            per_row = []
            for r in range(tm):
                kr = mk_ref[r, :, sl]
                lg = (kr * qh[r:r + 1, :]).sum(axis=-1, keepdims=True) * scale_m
                pm = jnp.exp(lg - lg.max(axis=0, keepdims=True))
                pm = pm / pm.sum(axis=0, keepdims=True)
                per_row.append((pm * mv_ref[r, :, sl]).sum(axis=0, keepdims=True))
            outs.append(jnp.concatenate(per_row, axis=0))
    out_m = jnp.concatenate(outs, axis=1)

    sig = jax.nn.sigmoid
    merged = (sig(gates[:, :d]) * _dot(oa_ref[...].astype(BF16), wbra_ref[...])
              + sig(gates[:, d:2 * d]) * _dot(out_b.astype(BF16), wbrb_ref[...])
              + sig(gates[:, 2 * d:]) * _dot(out_m.astype(BF16), wbrm_ref[...]))
    x1 = x + _dot(merged.astype(BF16), wout_ref[...])
    x1_ref[...] = x1
    h2 = _rms(x1, gffn_ref[...])
    h2_ref[...] = h2.astype(h2_ref.dtype)

    hi = h2.astype(BF16)
    lo = (h2 - hi.astype(F32)).astype(BF16)
    lg = _dot(hi, wrh_ref[...]) + _dot(hi, wrl_ref[...]) + _dot(lo, wrh_ref[...]) + br_ref[...]
    lane = lax.broadcasted_iota(I32, lg.shape, 1)
    lg = jnp.where(lane < n_experts, lg, -jnp.inf)
    vals, idxs = [], []
    for _ in range(TOP_K_MOE):
        m = lg.max(axis=-1, keepdims=True)
        idx = jnp.where(lg == m, lane, LANES).min(axis=-1, keepdims=True)
        vals.append(m)
        idxs.append(idx)
        lg = jnp.where(lane == idx, -jnp.inf, lg)
    ex = [jnp.exp(v - vals[0]) for v in vals]
    den = ex[0]
    for e in ex[1:]:
        den = den + e
    te = jnp.full(lane.shape, -1, I32)
    tw = jnp.zeros(lane.shape, F32)
    for j in range(TOP_K_MOE):
        te = jnp.where(lane == j, idxs[j], te)
        tw = jnp.where(lane == j, ex[j] / den, tw)
    te_ref[...] = te
    tw_ref[...] = tw


def _mix(x2d, out_a, mix_params, gmlp_params, mem_k, mem_v, *, tm, chunked, shared_mem, emit_vb, n_experts):
    n, d = x2d.shape
    gin, wb, gvb, gqm, wbra, wbrb, wbrm, wout, gffn, wrh, wrl, br = mix_params
    row = lambda w: pl.BlockSpec((tm, w), lambda i: (i, 0))
    if shared_mem:
        mem_specs = [_const_spec(mem_k.shape), _const_spec(mem_v.shape)]
    else:
        mspec = pl.BlockSpec((tm,) + mem_k.shape[1:], lambda i: (i, 0, 0))
        mem_specs = [mspec, mspec]
    consts = lambda arrs: [_const_spec(a.shape) for a in arrs]
    in_specs = ([row(d), row(WA_Q)] + consts([gin, wb, gvb, gqm]) + consts(gmlp_params) + mem_specs
                + consts([wbra, wbrb, wbrm, wout, gffn, wrh, wrl, br]))
    out_shape = [jax.ShapeDtypeStruct((n, d), F32), jax.ShapeDtypeStruct((n, d), F32),
                 jax.ShapeDtypeStruct((n, LANES), I32), jax.ShapeDtypeStruct((n, LANES), F32)]
    out_specs = [row(d), row(d), row(LANES), row(LANES)]
    if emit_vb:
        out_shape.append(jax.ShapeDtypeStruct((n, WIDTH_B), F32))
        out_specs.append(row(WIDTH_B))
    kern = functools.partial(_mix_kernel, chunked=chunked, shared_mem=shared_mem, emit_vb=emit_vb,
                             n_experts=n_experts)
    return pl.pallas_call(kern, out_shape=tuple(out_shape), grid=(n // tm,), in_specs=in_specs,
                          out_specs=tuple(out_specs), compiler_params=_cparams(), name="mix")(
                              x2d, out_a, gin, wb, gvb, gqm, *gmlp_params, mem_k, mem_v,
                              wbra, wbrb, wbrm, wout, gffn, wrh, wrl, br)


def _prep_mix_params(w_in, g_in, g_vb, g_qm, w_br_a, w_br_b, w_br_m, w_out, g_ffn, w_router, b_router):
    d = w_in.shape[0]
    parts = _split_w_in(w_in, d)
    wb = jnp.concatenate(parts[6:10], axis=1).astype(BF16)
    n_e = w_router.shape[1]
    wr = jnp.pad(w_router, ((0, 0), (0, LANES - n_e)))
    wrh = wr.astype(BF16)
    wrl = (wr - wrh.astype(F32)).astype(BF16)
    br = jnp.pad(b_router, (0, LANES - n_e)).reshape(1, LANES)
    return (g_in.reshape(1, d), wb, g_vb.reshape(1, WIDTH_B), g_qm.reshape(1, HEAD_DIM_M),
            w_br_a.astype(BF16), w_br_b.astype(BF16), w_br_m.astype(BF16), w_out.astype(BF16),
            g_ffn.reshape(1, d), wrh, wrl, br)


def _prep_gmlp_chunked(w_s, b_s):
    tril = jnp.tril(jnp.ones((CHUNK, CHUNK), w_s.dtype))
    wtril = (w_s * tril).astype(BF16)
    bt = jnp.repeat(b_s.T, GROUP_DIM_B, axis=1)
    return wtril, bt


def _prep_gmlp_single(w_s, b_s):
    wdiag = jnp.repeat(w_s[:, 0, 0], GROUP_DIM_B).reshape(1, WIDTH_B)
    bdiag = jnp.repeat(b_s[:, 0], GROUP_DIM_B).reshape(1, WIDTH_B)
    return wdiag, bdiag


MOE_TP = 256


def _plan_kernel(te_ref, tri_ref, upper_ref, lslot_ref, cnt_ref, tstart_ref, gprev_ref, total_ref, carry_ref):
    @pl.when(pl.program_id(0) == 0)
    def _():
        carry_ref[...] = jnp.zeros(carry_ref.shape, F32)

    te = te_ref[...]
    lane = lax.broadcasted_iota(I32, te.shape, 1)
    picks = [te[:, j:j + 1] for j in range(TOP_K_MOE)]
    onehot = jnp.zeros(te.shape, F32)
    for e in picks:
        onehot = onehot + jnp.where(lane == e, 1.0, 0.0)
    before = _dot(tri_ref[...], onehot.astype(BF16))
    cnt = onehot.sum(axis=0, keepdims=True)
    cnt = jnp.floor((cnt + (SUBLANES - 1)) * (1.0 / SUBLANES)) * SUBLANES
    cnt8 = jnp.broadcast_to(cnt, (SUBLANES, LANES)).astype(BF16)
    tstart = _dot(cnt8, upper_ref[...])[0:1, :]
    lslot = jnp.zeros(te.shape, I32)
    for j, e in enumerate(picks):
        r = jnp.where(lane == e, before + tstart, 0.0).sum(axis=-1, keepdims=True).astype(I32)
        lslot = jnp.where(lane == j, r, lslot)
    lslot_ref[...] = lslot
    cnt_ref[0] = cnt.astype(I32)
    tstart_ref[0] = tstart.astype(I32)
    gprev_ref[0] = carry_ref[...].astype(I32)
    carry_ref[...] = carry_ref[...] + cnt
    total_ref[...] = carry_ref[...].astype(I32)


def _moe_plan(te_all):
    n = te_all.shape[0]
    tp = MOE_TP
    n_tiles = n // tp
    r = jnp.arange(tp)
    tri = (r[:, None] > r[None, :]).astype(BF16)
    e = jnp.arange(LANES)
    upper = (e[:, None] < e[None, :]).astype(BF16)
    per_tile = jax.ShapeDtypeStruct((n_tiles, 1, LANES), I32)
    tile_spec = pl.BlockSpec((1, 1, LANES), lambda i: (i, 0, 0))
    return pl.pallas_call(
        _plan_kernel,
        out_shape=(jax.ShapeDtypeStruct((n, LANES), I32), per_tile, per_tile, per_tile,
                   jax.ShapeDtypeStruct((1, LANES), I32)),
        grid=(n_tiles,),
        in_specs=[pl.BlockSpec((tp, LANES), lambda i: (i, 0)), _const_spec((tp, tp)), _const_spec((LANES, LANES))],
        out_specs=(pl.BlockSpec((tp, LANES), lambda i: (i, 0)), tile_spec, tile_spec, tile_spec,
                   _const_spec((1, LANES))),
        scratch_shapes=[pltpu.VMEM((1, LANES), F32)],
        compiler_params=_cparams(), name="moe_plan")(te_all, tri, upper)


def _run_pieces(tp):
    sizes, s = [], SUBLANES
    while s <= tp:
        sizes.append(s)
        s *= 2
    return tuple(reversed(sizes))


def _local_rows(tp, n_e):
    return tp * TOP_K_MOE + n_e * SUBLANES


def _for_runs(tabs, tile, n_e, tp, fn):
    cnt_s, tstart_s, gbase_s = tabs

    def per_expert(e, c):
        idx = tile * n_e + e
        length, src0, dst0 = cnt_s[idx], tstart_s[idx], gbase_s[idx]
        off = jnp.int32(0)
        for size in _run_pieces(tp):
            piece = length & size

            @pl.when(piece != 0)
            def _():
                fn(pl.multiple_of(src0 + off, SUBLANES), pl.multiple_of(dst0 + off, SUBLANES), size)
            off = off + piece
        return c

    lax.fori_loop(0, n_e, per_expert, 0, unroll=4)


def _wait_runs(tabs, tile, n_e, n_rows, copy_of):
    cnt_s, tstart_s, _ = tabs
    last = tile * n_e + n_e - 1
    total = tstart_s[last] + cnt_s[last]
    for size in _run_pieces(n_rows):
        @pl.when((total & size) != 0)
        def _():
            copy_of(size).wait()


def _slot_matrix(lslot, vals, n_rows):
    col = lax.broadcasted_iota(I32, (lslot.shape[0], n_rows), 1)
    m = jnp.zeros(col.shape, F32)
    for j in range(TOP_K_MOE):
        v = 1.0 if vals is None else vals[:, j:j + 1]
        m = m + jnp.where(lslot[:, j:j + 1] == col, v, 0.0)
    return m


def _dispatch_kernel(cnt_s, tstart_s, gbase_s, fstart_s, flen_s, nused_s, lslot_ref, h2_ref, *rest,
                     tp, n_e, tile0, first, bm):
    if first:
        xs_out, stage, sem, zbuf, zsem = rest
    else:
        _, xs_out, stage, sem = rest
    n_rows = _local_rows(tp, n_e)
    i = pl.program_id(0)
    slot = i % 2
    pt = _slot_matrix(lslot_ref[...], None, n_rows).astype(BF16)
    stage[slot] = lax.dot_general(pt, h2_ref[...].astype(BF16), (((0,), (0,)), ((), ())),
                                  preferred_element_type=F32)

    def runs(tile, s, op):
        _for_runs((cnt_s, tstart_s, gbase_s), tile0 + tile, n_e, tp,
                  lambda src, dst, size: op(pltpu.make_async_copy(
                      stage.at[s, pl.ds(src, size), :], xs_out.at[pl.ds(dst, size), :], sem.at[s])))

    def drain(tile, s):
        _wait_runs((cnt_s, tstart_s, gbase_s), tile0 + tile, n_e, n_rows, lambda size: pltpu.make_async_copy(
            stage.at[s, pl.ds(0, size), :], xs_out.at[pl.ds(0, size), :], sem.at[s]))

    @pl.when(i > 0)
    def _():
        drain(i - 1, 1 - slot)

    runs(i, slot, lambda cp: cp.start())

    @pl.when(i == pl.num_programs(0) - 1)
    def _():
        drain(i, slot)
        if first:
            zbuf[...] = jnp.zeros(zbuf.shape, F32)
            nb = xs_out.shape[0] // bm

            def fill(op):
                def per_expert(e, c):
                    length, dst0 = flen_s[e], fstart_s[e]
                    off = jnp.int32(0)
                    for size in _run_pieces(bm):
                        piece = length & size

                        @pl.when(piece != 0)
                        def _():
                            op(pltpu.make_async_copy(
                                zbuf.at[pl.ds(0, size), :],
                                xs_out.at[pl.ds(pl.multiple_of(dst0 + off, SUBLANES), size), :], zsem))
                        off = off + piece
                    return c

                lax.fori_loop(0, n_e, per_expert, 0)
                lax.fori_loop(nused_s[0], nb, lambda b, c: (op(pltpu.make_async_copy(
                    zbuf, xs_out.at[pl.ds(pl.multiple_of(b * bm, bm), bm), :], zsem)), c)[1], 0)

            fill(lambda cp: cp.start())
            fill(lambda cp: cp.wait())


def _dispatch(tabs, fill_tabs, tile0, lslot, h2, xs, n_e, ns, bm):
    n, d = h2.shape
    tp = min(MOE_TP, n)
    n_rows = _local_rows(tp, n_e)
    first = xs is None
    kern = functools.partial(_dispatch_kernel, tp=tp, n_e=n_e, tile0=tile0, first=first, bm=bm)
    any_spec = pl.BlockSpec(memory_space=pl.ANY)
    scratch = [pltpu.VMEM((2, n_rows, d), F32), pltpu.SemaphoreType.DMA((2,))]
    if first:
        scratch += [pltpu.VMEM((bm, d), F32), pltpu.SemaphoreType.DMA]
    grid_spec = pltpu.PrefetchScalarGridSpec(
        num_scalar_prefetch=6, grid=(n // tp,),
        in_specs=[pl.BlockSpec((tp, LANES), lambda i, *_: (i, 0)), pl.BlockSpec((tp, d), lambda i, *_: (i, 0))]
        + ([] if first else [any_spec]),
        out_specs=any_spec, scratch_shapes=scratch)
    args = (*tabs, *fill_tabs, lslot, h2) + (() if first else (xs,))
    return pl.pallas_call(
        kern, out_shape=jax.ShapeDtypeStruct((ns, d), F32), grid_spec=grid_spec,
        input_output_aliases={} if first else {8: 0},
        compiler_params=pltpu.CompilerParams(dimension_semantics=("arbitrary",), has_side_effects=True,
                                             vmem_limit_bytes=VMEM_LIMIT),
        name="dispatch")(*args)


def _expert_kernel(be_ref, nu_ref, xs_ref, wg_ref, wu_ref, wd_ref, bg_ref, bu_ref, bd_ref, y_ref,
                   wg_bf, wu_bf, wd_bf):
    b = pl.program_id(0)

    @pl.when(b >= nu_ref[0])
    def _():
        y_ref[...] = jnp.zeros(y_ref.shape, F32)

    @pl.when(b < nu_ref[0])
    def _():
        prev = be_ref[jnp.maximum(b - 1, 0)]

        @pl.when(jnp.logical_or(b == 0, be_ref[b] != prev))
        def _():
            wg_bf[...] = wg_ref[0].astype(BF16)
            wu_bf[...] = wu_ref[0].astype(BF16)
            wd_bf[...] = wd_ref[0].astype(BF16)

        xb = xs_ref[...].astype(BF16)
        ff = wg_bf.shape[1]
        y = jnp.zeros(y_ref.shape, F32) + bd_ref[0]
        for n in range(ff // FF_CHUNK):
            sl = slice(n * FF_CHUNK, (n + 1) * FF_CHUNK)
            hg = jnp.minimum(_dot(xb, wg_bf[:, sl]) + bg_ref[0][:, sl], SWIGLU_LIMIT)
            hu = jnp.clip(_dot(xb, wu_bf[:, sl]) + bu_ref[0][:, sl], -SWIGLU_LIMIT, SWIGLU_LIMIT)
            act = hg * jax.nn.sigmoid(SWIGLU_ALPHA * hg) * (hu + 1.0)
            y = y + _dot(act.astype(BF16), wd_bf[sl, :])
        y_ref[...] = y


def _experts(xs, block_e, n_used, w_gate, b_gate, w_up, b_up, w_down, b_down):
    ns, d = xs.shape
    n_e, _, ff = w_gate.shape
    bm = MOE_BM
    blk = lambda b, be, nu: (jnp.minimum(b, nu[0] - 1), 0)
    wsel = lambda b, be, nu: (be[b], 0, 0)
    grid_spec = pltpu.PrefetchScalarGridSpec(
        num_scalar_prefetch=2, grid=(ns // bm,),
        in_specs=[pl.BlockSpec((bm, d), blk),
                  pl.BlockSpec((1, d, ff), wsel), pl.BlockSpec((1, d, ff), wsel), pl.BlockSpec((1, ff, d), wsel),
                  pl.BlockSpec((1, 1, ff), wsel), pl.BlockSpec((1, 1, ff), wsel), pl.BlockSpec((1, 1, d), wsel)],
        out_specs=pl.BlockSpec((bm, d), lambda b, be, nu: (b, 0)),
        scratch_shapes=[pltpu.VMEM((d, ff), BF16), pltpu.VMEM((d, ff), BF16), pltpu.VMEM((ff, d), BF16)])
    return pl.pallas_call(_expert_kernel, out_shape=jax.ShapeDtypeStruct((ns, d), F32), grid_spec=grid_spec,
                          compiler_params=_cparams(), name="experts")(
                              block_e, n_used, xs, w_gate, w_up, w_down,
                              b_gate.reshape(n_e, 1, ff), b_up.reshape(n_e, 1, ff), b_down.reshape(n_e, 1, d))


def _combine_kernel(cnt_s, tstart_s, gbase_s, lslot_ref, tw_ref, x1_ref, yb_hbm, y_ref, buf, sem, *, tp, n_e, tile0):
    n_rows = _local_rows(tp, n_e)
    i = pl.program_id(0)
    slot = i % 2

    def runs(tile, s, op):
        _for_runs((cnt_s, tstart_s, gbase_s), tile0 + tile, n_e, tp,
                  lambda loc, glob, size: op(pltpu.make_async_copy(
                      yb_hbm.at[pl.ds(glob, size), :], buf.at[s, pl.ds(loc, size), :], sem.at[s])))

    @pl.when(i == 0)
    def _():
        buf[...] = jnp.zeros(buf.shape, F32)
        runs(i, slot, lambda cp: cp.start())

    @pl.when(i + 1 < pl.num_programs(0))
    def _():
        runs(i + 1, 1 - slot, lambda cp: cp.start())

    _wait_runs((cnt_s, tstart_s, gbase_s), tile0 + i, n_e, n_rows, lambda size: pltpu.make_async_copy(
        yb_hbm.at[pl.ds(0, size), :], buf.at[slot, pl.ds(0, size), :], sem.at[slot]))

    ptw = _slot_matrix(lslot_ref[...], tw_ref[...], n_rows)
    rows = buf[slot]
    w_hi = ptw.astype(BF16)
    w_lo = (ptw - w_hi.astype(F32)).astype(BF16)
    r_hi = rows.astype(BF16)
    r_lo = (rows - r_hi.astype(F32)).astype(BF16)
    y_ref[...] = x1_ref[...] + _dot(w_hi, r_hi) + (_dot(w_hi, r_lo) + _dot(w_lo, r_hi))


def _combine(tabs, tile0, lslot, tw, x1, yb, n_e):
    n, d = x1.shape
    tp = min(MOE_TP, n)
    n_rows = _local_rows(tp, n_e)
    kern = functools.partial(_combine_kernel, tp=tp, n_e=n_e, tile0=tile0)
    row = lambda w: pl.BlockSpec((tp, w), lambda i, *_: (i, 0))
    grid_spec = pltpu.PrefetchScalarGridSpec(
        num_scalar_prefetch=3, grid=(n // tp,),
        in_specs=[row(LANES), row(LANES), row(d), pl.BlockSpec(memory_space=pl.ANY)],
        out_specs=row(d),
        scratch_shapes=[pltpu.VMEM((2, n_rows, d), F32), pltpu.SemaphoreType.DMA((2,))])
    return pl.pallas_call(kern, out_shape=jax.ShapeDtypeStruct(x1.shape, F32), grid_spec=grid_spec,
                          compiler_params=_cparams(), name="combine")(*tabs, lslot, tw, x1, yb)


def _moe(groups, w_gate, b_gate, w_up, b_up, w_down, b_down):
    n_e = w_gate.shape[0]
    d = groups[0][0].shape[1]
    tp, bm = MOE_TP, MOE_BM
    sizes = [g[0].shape[0] for g in groups]
    assert all(n % tp == 0 or n < tp for n in sizes), sizes
    parts, tile0s, o = [], [], 0
    for g, n in zip(groups, sizes):
        n_pad = -(-n // tp) * tp
        parts.append(jnp.pad(g[2], ((0, n_pad - n), (0, 0)), constant_values=-1))
        tile0s.append(o // tp)
        o += n_pad
    lslot, cnt3, tstart3, gprev3, total = _moe_plan(jnp.concatenate(parts, axis=0))
    counts = total[0, :n_e]
    padded_cnt = (counts + bm - 1) // bm * bm
    pad_end = jnp.cumsum(padded_cnt)
    pad_start = pad_end - padded_cnt
    n_assign = sum(sizes) * TOP_K_MOE
    n_slack = (o // tp) * n_e * (SUBLANES - 1)
    nb = -(-(n_assign + n_slack) // bm) + n_e
    n_used = (pad_end[-1] // bm).astype(I32)
    blocks = jnp.arange(nb, dtype=I32)
    block_e = jnp.minimum((pad_end[None, :] <= (blocks * bm)[:, None]).sum(axis=1), n_e - 1).astype(I32)
    block_e = jnp.where(blocks < n_used, block_e, block_e[jnp.maximum(n_used - 1, 0)])
    flat = lambda a: a[:, 0, :n_e].reshape(-1).astype(I32)
    gbase3 = pad_start[None, None, :] + gprev3[:, :, :n_e]
    tabs = (flat(cnt3), flat(tstart3), flat(gbase3))

    after_first = gprev3[tile0s[1], 0, :n_e] if len(groups) > 1 else counts
    fill_start = (pad_start + after_first).astype(I32)
    fill_tabs = (fill_start, (pad_end - fill_start).astype(I32), n_used.reshape(1))
    xs, lslots = None, []
    for (x1, h2, te, tw), n, t0 in zip(groups, sizes, tile0s):
        ls = lslot[t0 * tp:t0 * tp + n]
        lslots.append(ls)
        xs = _dispatch(tabs, fill_tabs, t0, ls, h2, xs, n_e, nb * bm, bm)
    yb = _experts(xs, block_e, n_used.reshape(1), w_gate, b_gate, w_up, b_up, w_down, b_down)
    return [_combine(tabs, t0, ls, tw, x1, yb, n_e)
            for (x1, h2, te, tw), ls, t0 in zip(groups, lslots, tile0s)]


def _row_tile(n, want):
    return want if n % want == 0 else n


def kernel(x_prompt, x_sample, mem_prompt, cache_k, cache_v, cache_idx_k, cache_mem_k, cache_mem_v, page_table,
           g_in, w_in, g_qa, g_ka, g_vb, w_s, b_s, g_qm, g_mem, w_mem_kv, g_km, w_br_a, w_br_b, w_br_m, w_out,
           g_ffn, w_router, b_router, w_gate, b_gate, w_up, b_up, w_down, b_down):
    bp, t, d = x_prompt.shape
    bs, ts, _ = x_sample.shape
    assert bp == 1 and ts == 1 and t % CHUNK == 0 and cache_k.shape[1] == LANES
    n_mem = mem_prompt.shape[1]
    n_pages = page_table.shape[1]
    ps = cache_k.shape[1]
    past = n_pages * ps
    n_e = w_router.shape[1]

    fp = _prep_params(w_in, g_in, g_qa, g_ka)
    mp = _prep_mix_params(w_in, g_in, g_vb, g_qm, w_br_a, w_br_b, w_br_m, w_out, g_ffn, w_router, b_router)

    xp = x_prompt.reshape(t, d)
    qa_p, k32_p, khm_p, v32_p, vt_p, qi_p, ki32_p, kibf_p, wit_p = _front(
        xp, _rope_tables(jnp.arange(t, dtype=I32)), *fp, tm=_row_tile(t, 512), paged=True)
    mem_k, mem_v = _memory_kv(mem_prompt.reshape(n_mem, d), g_mem.reshape(1, d), w_mem_kv.astype(BF16),
                              g_km.reshape(1, HEAD_DIM_M))
    out_a_p = _prompt_attention(qi_p, qa_p, wit_p, kibf_p, khm_p, vt_p)
    x1_p, h2_p, te_p, tw_p = _mix(xp, out_a_p, mp, _prep_gmlp_chunked(w_s, b_s), mem_k, mem_v,
                                  tm=_row_tile(t, 512), chunked=True, shared_mem=True, emit_vb=False, n_experts=n_e)

    xs = x_sample.reshape(bs, d)
    qa_s, k32_s, _, v32_s, _, qi_s, ki32_s, _, wit_s = _front(
        xs, _rope_tables(jnp.full((bs,), past, I32)), *fp, tm=bs, paged=False)
    out_a_s = _sample_attention(page_table, jnp.moveaxis(qi_s, 0, 1), jnp.moveaxis(qa_s, 0, 1),
                                wit_s.T.reshape(bs, N_HEADS_IDX, 1), ki32_s.reshape(bs, 1, D_IDX),
                                k32_s.reshape(bs, 1, WA_KV), v32_s.reshape(bs, 1, WA_KV),
                                cache_idx_k, cache_k, cache_v)
    x1_s, h2_s, te_s, tw_s, vb_s = _mix(xs, out_a_s.reshape(bs, WA_Q), mp, _prep_gmlp_single(w_s, b_s),
                                        cache_mem_k.reshape(bs, n_mem, WM_Q), cache_mem_v.reshape(bs, n_mem, WM_Q),
                                        tm=_row_tile(bs, SUBLANES), chunked=False, shared_mem=False, emit_vb=True,
                                        n_experts=n_e)

    y_p, y_s = _moe([(x1_p, h2_p, te_p, tw_p), (x1_s, h2_s, te_s, tw_s)],
                    w_gate, b_gate, w_up, b_up, w_down, b_down)

    n_pg = t // ps
    paged_kv = lambda a: jnp.transpose(a.reshape(1, n_pg, N_KV_A, HEAD_DIM_A, ps), (0, 1, 4, 2, 3))
    return (y_p.reshape(1, t, d), y_s.reshape(bs, 1, d),
            paged_kv(k32_p), paged_kv(v32_p), jnp.transpose(ki32_p, (0, 2, 1)).reshape(1, n_pg, ps, D_IDX),
            mem_k.reshape(1, n_mem, N_HEADS_M, HEAD_DIM_M), mem_v.reshape(1, n_mem, N_HEADS_M, HEAD_DIM_M),
            k32_s.reshape(bs, 1, N_KV_A, HEAD_DIM_A), v32_s.reshape(bs, 1, N_KV_A, HEAD_DIM_A),
            ki32_s.reshape(bs, 1, D_IDX), vb_s.reshape(bs, 1, WIDTH_B))
```

```python
import functools

import jax
import jax.numpy as jnp
from jax import lax
from jax.experimental import pallas as pl
from jax.experimental.pallas import tpu as pltpu

F32 = jnp.float32
BF16 = jnp.bfloat16
I32 = jnp.int32

N_HEADS_A = 8
N_KV_A = 2
HEAD_DIM_A = 64
TOPK_MAX = 256
N_HEADS_IDX = 8
D_IDX = 64
N_GROUPS_B = 4
GROUP_DIM_B = 128
WIDTH_B = N_GROUPS_B * GROUP_DIM_B
CHUNK = 128
N_HEADS_M = 4
HEAD_DIM_M = 128
ROPE_THETA = 500000.0
ROT_FRACTION = 4
N_BRANCH = 3
TOP_K_MOE = 4
SWIGLU_LIMIT = 7.0
SWIGLU_ALPHA = 1.702
EPS = 1e-6

WA_Q = N_HEADS_A * HEAD_DIM_A
WA_KV = N_KV_A * HEAD_DIM_A
WI_Q = N_HEADS_IDX * D_IDX
WM_Q = N_HEADS_M * HEAD_DIM_M
GQA = N_HEADS_A // N_KV_A

LANES = 128
SUBLANES = 8
VMEM_LIMIT = 56 * 1024 * 1024

LOG2_E = 1.4426950408889634
INT_MIN = -(2 ** 31)
INT_MAX = 2 ** 31 - 1
NEG = -1e30

VT_ROWS = HEAD_DIM_A + 16
ATTN_BQ = 128
ATTN_SC = 512
MOE_BM = 512
FF_CHUNK = 512


def _cparams(n_axes=1, vmem=VMEM_LIMIT):
    return pltpu.CompilerParams(dimension_semantics=("arbitrary",) * n_axes, vmem_limit_bytes=vmem)


def _const_spec(shape, single=False):
    zeros = (0,) * len(shape)
    if single:
        return pl.BlockSpec(shape, lambda *_: zeros, pipeline_mode=pl.Buffered(1))
    return pl.BlockSpec(shape, lambda *_: zeros)


def _dot(a, b):
    return jnp.dot(a, b, preferred_element_type=F32)


def _dot_nt(a, b):
    return lax.dot_general(a, b, (((1,), (1,)), ((), ())), preferred_element_type=F32)


def _rms(x, g):
    return x * lax.rsqrt(jnp.mean(x * x, axis=-1, keepdims=True) + EPS) * g


def _split_dot(x, w):
    hi = x.astype(BF16)
    lo = (x - hi.astype(F32)).astype(BF16)
    return _dot(hi, w) + _dot(lo, w)


def _front_kernel(x_ref, gin_ref, wa_ref, wwit_ref, wvt_ref, bd_ref, gqa_ref, gka_ref, cos_ref, sa_ref, sb_ref,
                  qa_ref, k32_ref, khm_ref, v32_ref, vt_ref, qi_ref, ki32_ref, kibf_ref, wit_ref, *, score_scale, paged):
    x = x_ref[...]
    h = _rms(x, gin_ref[...]).astype(BF16)
    p = _dot(h, wa_ref[...])
    cos, sa, sb = cos_ref[...], sa_ref[...], sb_ref[...]
    bd = bd_ref[...]

    def head_norm(v, g):
        ssq = _split_dot(v * v, bd)
        return v * lax.rsqrt(ssq * (1.0 / HEAD_DIM_A) + EPS) * g

    def rope(v):
        return v * cos + pltpu.roll(v, LANES - 8, 1) * sa + pltpu.roll(v, 8, 1) * sb

    for s in range(WA_Q // LANES):
        v = p[:, s * LANES:(s + 1) * LANES]
        v = rope(head_norm(v, gqa_ref[...])) * (HEAD_DIM_A ** -0.5 * LOG2_E)
        vb = v.astype(BF16)
        qa_ref[2 * s] = vb[:, :HEAD_DIM_A]
        qa_ref[2 * s + 1] = vb[:, HEAD_DIM_A:]
    o = WA_Q
    n_pg = x.shape[0] // LANES
    k = rope(head_norm(p[:, o:o + WA_KV], gka_ref[...]))
    if paged:
        for j in range(n_pg):
            k32_ref[j] = k[j * LANES:(j + 1) * LANES, :].T
    else:
        k32_ref[...] = k
    kb = k.astype(BF16)
    khm_ref[0] = kb[:, :HEAD_DIM_A]
    khm_ref[1] = kb[:, HEAD_DIM_A:]
    o += WA_KV
    if not paged:
        v32_ref[...] = p[:, o:o + WA_KV]
    o += WA_KV
    for s in range(WI_Q // LANES):
        vb = rope(p[:, o + s * LANES:o + (s + 1) * LANES]).astype(BF16)
        qi_ref[2 * s] = vb[:, :D_IDX]
        qi_ref[2 * s + 1] = vb[:, D_IDX:]
    o += WI_Q
    ki_slab = rope(p[:, o:o + LANES])
    ki = ki_slab[:, :D_IDX]
    if paged:
        for j in range(n_pg):
            ki32_ref[j] = ki_slab[j * LANES:(j + 1) * LANES, :].T[:D_IDX, :]
    else:
        ki32_ref[...] = ki
    kibf_ref[...] = ki.astype(BF16)
    wit_ref[...] = _dot_nt(wwit_ref[...], h) * score_scale
    vt = _dot_nt(wvt_ref[...], h)
    if paged:
        for j in range(n_pg):
            v32_ref[j] = vt[:, j * LANES:(j + 1) * LANES]
    tail = jnp.where(lax.broadcasted_iota(I32, (VT_ROWS - HEAD_DIM_A, vt.shape[1]), 0) == 0, 1.0, 0.0)
    for g in range(N_KV_A):
        vt_ref[g] = jnp.concatenate([vt[g * HEAD_DIM_A:(g + 1) * HEAD_DIM_A], tail], axis=0).astype(BF16)


def _front(x2d, tabs, gin, wa, wwit, wvt, bd, gqa, gka, tm, paged):
    n, d = x2d.shape
    cos, sa, sb = tabs
    row = lambda w: pl.BlockSpec((tm, w), lambda i: (i, 0))
    hm = lambda nh, w: pl.BlockSpec((nh, tm, w), lambda i: (0, i, 0))
    if paged:
        assert tm % LANES == 0
        f32_out = lambda w: jax.ShapeDtypeStruct((n // LANES, w, LANES), F32)
        f32_spec = lambda w: pl.BlockSpec((tm // LANES, w, LANES), lambda i: (i, 0, 0))
    else:
        f32_out = lambda w: jax.ShapeDtypeStruct((n, w), F32)
        f32_spec = row
    out_shape = (
        jax.ShapeDtypeStruct((N_HEADS_A, n, HEAD_DIM_A), BF16),
        f32_out(WA_KV),
        jax.ShapeDtypeStruct((N_KV_A, n, HEAD_DIM_A), BF16),
        f32_out(WA_KV),
        jax.ShapeDtypeStruct((N_KV_A, VT_ROWS, n), BF16),
        jax.ShapeDtypeStruct((N_HEADS_IDX, n, D_IDX), BF16),
        f32_out(D_IDX),
        jax.ShapeDtypeStruct((n, D_IDX), BF16),
        jax.ShapeDtypeStruct((N_HEADS_IDX, n), F32),
    )
    out_specs = (hm(N_HEADS_A, HEAD_DIM_A), f32_spec(WA_KV), hm(N_KV_A, HEAD_DIM_A), f32_spec(WA_KV),
                 pl.BlockSpec((N_KV_A, VT_ROWS, tm), lambda i: (0, 0, i)), hm(N_HEADS_IDX, D_IDX), f32_spec(D_IDX),
                 row(D_IDX), pl.BlockSpec((N_HEADS_IDX, tm), lambda i: (0, i)))
    in_specs = [row(d), _const_spec(gin.shape), _const_spec(wa.shape), _const_spec(wwit.shape),
                _const_spec(wvt.shape), _const_spec(bd.shape), _const_spec(gqa.shape), _const_spec(gka.shape),
                row(LANES), row(LANES), row(LANES)]
    kern = functools.partial(_front_kernel, score_scale=D_IDX ** -0.5 * N_HEADS_IDX ** -0.5, paged=paged)
    return pl.pallas_call(kern, out_shape=out_shape, grid=(n // tm,), in_specs=in_specs, out_specs=out_specs,
                          compiler_params=_cparams(), name="front")(x2d, gin, wa, wwit, wvt, bd, gqa, gka, cos, sa, sb)


def _memkv_kernel(mem_ref, gmem_ref, w_ref, gkm_ref, k32_ref, v32_ref):
    h = _rms(mem_ref[...], gmem_ref[...]).astype(BF16)
    kv = _dot(h, w_ref[...])
    for hh in range(N_HEADS_M):
        sl = slice(hh * HEAD_DIM_M, (hh + 1) * HEAD_DIM_M)
        k32_ref[:, sl] = _rms(kv[:, sl], gkm_ref[...])
    v32_ref[...] = kv[:, WM_Q:]


def _memory_kv(mem2d, gmem, w, gkm):
    n = mem2d.shape[0]
    out = jax.ShapeDtypeStruct((n, WM_Q), F32)
    return pl.pallas_call(_memkv_kernel, out_shape=(out, out), name="memory_kv",
                          compiler_params=pltpu.CompilerParams(vmem_limit_bytes=VMEM_LIMIT))(mem2d, gmem, w, gkm)


def _sort_key(s):
    s = jnp.where(s == 0.0, 0.0, s)
    bits = pltpu.bitcast(s, I32)
    return bits ^ ((bits >> 31) & INT_MAX)


KEYS_PER_GROUP = 32 * SUBLANES
SEARCH_SLAB = 128


def _bit_planes(words):
    a = list(words)
    j, m = 16, 0x0000FFFF
    while j:
        k = 0
        while k < 32:
            t = (a[k] ^ lax.shift_right_logical(a[k + j], jnp.int32(j))) & m
            a[k] = a[k] ^ t
            a[k + j] = a[k + j] ^ (t << j)
            k = (k + j + 1) & ~j
        j >>= 1
        m = (m ^ (m << j)) & 0xFFFFFFFF if j else m
    return a


ATTN_SUB = 256


def _attn_kernel(qi_ref, qa_ref, wi_ref, ki_ref, k_ref, vt_ref, out_ref,
                 keys_ref, planes_ref, cand_ref, cut_ref, bias_ref, lg_ref, cm_ref, m_ref, acc_ref, *, k_sel, idx_bits):
    bq, sc = ATTN_BQ, ATTN_SC
    wpc = sc // 32
    i = pl.program_id(0)
    n_ch = (i * bq) // sc + 1
    t_pos = i * bq + lax.broadcasted_iota(I32, (1, bq), 1)
    sub = lax.broadcasted_iota(I32, (sc, 1), 0)
    qi = qi_ref[...].reshape(N_HEADS_IDX * bq, D_IDX)
    w = wi_ref[...]

    @pl.when(i == 0)
    def _():
        planes_ref[...] = jnp.zeros(planes_ref.shape, I32)

    def chunk(c):
        return pl.ds(pl.multiple_of(c * sc, sc), sc)

    def score_chunk(c, carry):
        off = pl.multiple_of(c * sc, sc)
        d = _dot_nt(ki_ref[chunk(c), :], qi)
        s = jnp.zeros((sc, bq), F32)
        for h in range(N_HEADS_IDX):
            s = s + jnp.maximum(d[:, h * bq:(h + 1) * bq], 0.0) * w[h:h + 1, :]
        key = jnp.where(off + sub <= t_pos, _sort_key(s), INT_MIN)
        keys_ref[chunk(c), :] = key
        u = key ^ INT_MIN
        for grp in range(sc // KEYS_PER_GROUP):
            base = grp * KEYS_PER_GROUP
            words = _bit_planes([u[base + SUBLANES * v:base + SUBLANES * (v + 1), :] for v in range(32)])
            rows = pl.ds(pl.multiple_of(c * wpc + grp * SUBLANES, SUBLANES), SUBLANES)
            for p in range(32):
                planes_ref[p, rows, :] = words[p]
        return carry

    lax.fori_loop(0, n_ch // 2, lambda j, cr: score_chunk(2 * j + 1, score_chunk(2 * j, cr)), 0)
    lax.fori_loop(n_ch - n_ch % 2, n_ch, score_chunk, 0)

    def count(pred):
        def body(c, acc):
            hit = jnp.where(pred(keys_ref[chunk(c), :], c * sc + sub), 1, 0)
            return acc + hit.reshape(sc // SUBLANES, SUBLANES, bq).sum(axis=0)

        def body2(j, acc):
            return body(2 * j + 1, body(2 * j, acc))
        acc = lax.fori_loop(0, n_ch // 2, body2, jnp.zeros((SUBLANES, bq), I32))
        acc = lax.fori_loop(n_ch - n_ch % 2, n_ch, body, acc)
        return acc.sum(axis=0, keepdims=True)

    slab = SEARCH_SLAB
    n_slab = (n_ch * wpc + slab - 1) // slab
    srow = lax.broadcasted_iota(I32, (slab, 1), 0)

    def slab_rows(sl):
        return pl.ds(pl.multiple_of(sl * slab, slab), slab)

    def init_cand(sl, carry):
        cand_ref[slab_rows(sl), :] = jnp.where(sl * slab + srow < n_ch * wpc, -1, 0) + jnp.zeros((slab, bq), I32)
        return carry

    lax.fori_loop(0, n_slab, init_cand, 0)

    def colsum(x):
        return x.reshape(slab // SUBLANES, SUBLANES, bq).sum(axis=0)

    def sweep(prev, cur):
        def body(sl, accs):
            r = slab_rows(sl)
            cand = cand_ref[r, :]
            if prev is not None:
                pp, digit = prev
                hi, lo = planes_ref[pp, r, :], planes_ref[pp + 1, r, :]
                cand = cand & jnp.where(digit >= 2, hi, ~hi) & jnp.where((digit & 1) != 0, lo, ~lo)
                cand_ref[r, :] = cand
            if cur is None:
                return (accs[0] + colsum(lax.population_count(cand)),)
            hi = cand & planes_ref[cur, r, :]
            lo = planes_ref[cur + 1, r, :]
            d3 = hi & lo
            groups = (d3, hi ^ d3, (cand ^ hi) & lo)
            return tuple(a + colsum(lax.population_count(g)) for a, g in zip(accs, groups))
        n_acc = 1 if cur is None else 3
        accs = lax.fori_loop(0, n_slab, body, tuple(jnp.zeros((SUBLANES, bq), I32) for _ in range(n_acc)))
        return [a.sum(axis=0, keepdims=True) for a in accs]

    def decide(counts, p, need, thr_u):
        n3, n2, n1 = counts
        c3, c2, c1 = n3, n3 + n2, n3 + n2 + n1
        digit = (jnp.where(c3 >= need, 1, 0) + jnp.where(c2 >= need, 1, 0) + jnp.where(c1 >= need, 1, 0))
        above = jnp.where(digit == 3, 0, jnp.where(digit == 2, c3, jnp.where(digit == 1, c2, c1)))
        return digit, need - above, thr_u | (digit << (30 - p))

    digit, need, thr_u = decide(sweep(None, 0), 0, jnp.full((1, bq), k_sel, I32), jnp.zeros((1, bq), I32))

    def search_pass(j, st):
        digit, need, thr_u = st
        p = 2 * j
        return decide(sweep((p - 2, digit), p), p, need, thr_u)

    digit, need, thr_u = lax.fori_loop(1, 16, search_pass, (digit, need, thr_u))
    tied, = sweep((30, digit), None)
    thr = thr_u ^ INT_MIN
    cnt = (k_sel - need) + tied
    cut_ref[...] = jnp.full((1, bq), INT_MAX, I32)

    excess = jnp.logical_and(cnt > k_sel, thr > INT_MIN)
    has_ties = jnp.max(jnp.where(excess, 1, 0)) > 0

    @pl.when(has_ties)
    def _():
        want = k_sel - count(lambda blk, _: blk > thr)

        def tie_body(j, cpos):
            cand = cpos + (jnp.int32(1) << (idx_bits - 1 - j))
            below = count(lambda blk, pos: jnp.where(blk == thr, pos, INT_MAX) < cand)
            return jnp.where(below < want, cand, cpos)

        cpos = lax.fori_loop(0, idx_bits, tie_body, jnp.zeros((1, bq), I32))
        cut_ref[...] = jnp.where(excess, cpos, INT_MAX)

    m_ref[...] = jnp.full(m_ref.shape, NEG, F32)
    acc_ref[...] = jnp.zeros(acc_ref.shape, F32)
    cut = cut_ref[...]
    floor = jnp.where(thr == INT_MIN, INT_MIN, thr - 1)
    n_sub = sc // ATTN_SUB

    qs = [qa_ref[g * GQA:(g + 1) * GQA].reshape(GQA * bq, HEAD_DIM_A) for g in range(N_KV_A)]
    neg_row = jnp.full((1, GQA * bq), NEG, F32)

    def stage(c, par, cb, par_b):
        do_a, do_b = c is not None, cb is not None
        if do_a:
            off = pl.multiple_of(c * sc, sc)
            keyc = keys_ref[chunk(c), :]

            @pl.when(has_ties)
            def _():
                tie = jnp.where(off + sub <= cut, 0.0, NEG)
                bias = jnp.where(keyc > thr, 0.0, jnp.where(keyc == thr, tie, NEG))
                bias_ref[...] = jnp.where(keyc == INT_MIN, NEG, bias)

            @pl.when(jnp.logical_not(has_ties))
            def _():
                bias_ref[...] = jnp.where(keyc > floor, 0.0, NEG)
            cmax = [neg_row] * N_KV_A
        if do_b:
            off_b = pl.multiple_of(cb * sc, sc)
            m_new, acc = [], []
            for g in range(N_KV_A):
                m_old = m_ref[g]
                m_new.append(jnp.maximum(m_old, cm_ref[par_b, g]))
                acc.append(jnp.exp2(m_old - m_new[g]) * acc_ref[g])
                m_ref[g] = m_new[g]
        for r in range(n_sub):
            rows = slice(r * ATTN_SUB, (r + 1) * ATTN_SUB)
            if do_a:
                bias4 = jnp.concatenate([bias_ref[rows, :]] * GQA, axis=1)
                for g in range(N_KV_A):
                    lg = _dot_nt(k_ref[g, pl.ds(off + r * ATTN_SUB, ATTN_SUB), :], qs[g]) + bias4
                    lg_ref[par, g, rows, :] = lg
                    cmax[g] = jnp.maximum(cmax[g], lg.max(axis=0, keepdims=True))
            if do_b:
                for g in range(N_KV_A):
                    p = jnp.exp2(lg_ref[par_b, g, rows, :] - m_new[g])
                    vt = vt_ref[g, :, pl.ds(off_b + r * ATTN_SUB, ATTN_SUB)]
                    acc[g] = acc[g] + _dot(vt, p.astype(BF16))
        if do_b:
            for g in range(N_KV_A):
                acc_ref[g] = acc[g]
        if do_a:
            for g in range(N_KV_A):
                cm_ref[par, g] = cmax[g]

    stage(jnp.int32(0), 0, None, None)

    def stage_pair(j, carry):
        stage(2 * j + 1, 1, 2 * j, 0)
        stage(2 * j + 2, 0, 2 * j + 1, 1)
        return carry

    last = n_ch - 1
    lax.fori_loop(0, last // 2, stage_pair, 0)

    @pl.when(last % 2 == 1)
    def _():
        stage(last, 1, last - 1, 0)
        stage(None, None, last, 1)

    @pl.when(last % 2 == 0)
    def _():
        stage(None, None, last, 0)

    heads = []
    for g in range(N_KV_A):
        acc = acc_ref[g]
        o = acc[:HEAD_DIM_A] / acc[HEAD_DIM_A:HEAD_DIM_A + 1]
        heads += [o[:, hh * bq:(hh + 1) * bq] for hh in range(GQA)]
    out_ref[...] = jnp.concatenate(heads, axis=0).T.astype(out_ref.dtype)


def _prompt_attention(qi_hm, qa_hm, wit, kibf, khm, vt):
    t = kibf.shape[0]
    bq = ATTN_BQ
    k_sel = min(TOPK_MAX, t // 4)
    kern = functools.partial(_attn_kernel, k_sel=k_sel, idx_bits=max(1, (t - 1).bit_length()))
    in_specs = [pl.BlockSpec((N_HEADS_IDX, bq, D_IDX), lambda i: (0, i, 0)),
                pl.BlockSpec((N_HEADS_A, bq, HEAD_DIM_A), lambda i: (0, i, 0)),
                pl.BlockSpec((N_HEADS_IDX, bq), lambda i: (0, i)),
                _const_spec(kibf.shape, True), _const_spec(khm.shape, True), _const_spec(vt.shape, True)]
    n_words = -(-(t // 32) // SEARCH_SLAB) * SEARCH_SLAB
    scratch = [pltpu.VMEM((t, bq), I32), pltpu.VMEM((32, n_words, bq), I32), pltpu.VMEM((n_words, bq), I32),
               pltpu.VMEM((1, bq), I32), pltpu.VMEM((ATTN_SC, bq), F32),
               pltpu.VMEM((2, N_KV_A, ATTN_SC, GQA * bq), F32), pltpu.VMEM((2, N_KV_A, 1, GQA * bq), F32),
               pltpu.VMEM((N_KV_A, 1, GQA * bq), F32),
               pltpu.VMEM((N_KV_A, VT_ROWS, GQA * bq), F32)]
    return pl.pallas_call(kern, out_shape=jax.ShapeDtypeStruct((t, WA_Q), BF16), grid=(t // bq,),
                          in_specs=in_specs, out_specs=pl.BlockSpec((bq, WA_Q), lambda i: (i, 0)),
                          scratch_shapes=scratch, compiler_params=_cparams(), name="attn")(
                              qi_hm, qa_hm, wit, kibf, khm, vt)


SAMPLE_CK = 2048


def _sample_attn_kernel(pt_ref, qi_ref, qa_ref, wi_ref, kin_ref, kn_ref, vn_ref, cidx_hbm, ck_hbm, cv_hbm, out_ref,
                        idx_buf, k_buf, v_buf, scr_ref, cut_ref, sems, *, n_pages, ps, ck, k_sel, idx_bits):
    b = pl.program_id(0)
    nb = pl.num_programs(0)
    past = n_pages * ps
    nc = past // ck

    def page_copy(which, bb, p):
        src, dst = ((cidx_hbm, idx_buf), (ck_hbm, k_buf), (cv_hbm, v_buf))[which]
        cols = pl.ds(pl.multiple_of(p * ps, ps), ps)
        dst = dst.at[:, cols] if which == 0 else dst.at[:, :, cols]
        return pltpu.make_async_copy(src.at[pt_ref[bb, p]], dst, sems.at[which])

    def for_pages(fn):
        lax.fori_loop(0, n_pages, lambda p, c: (fn(p), c)[1], 0, unroll=8)

    def wait_all(which):
        buf = (idx_buf, k_buf, v_buf)[which]
        pltpu.make_async_copy(buf, buf, sems.at[which]).wait()

    @pl.when(b == 0)
    def _():
        for_pages(lambda p: page_copy(0, b, p).start())

    for_pages(lambda p: (page_copy(1, b, p).start(), page_copy(2, b, p).start()))
    wait_all(0)

    qi = qi_ref[0]
    w = wi_ref[0]
    for c in range(nc):
        d = _dot(qi, idx_buf[:, c * ck:(c + 1) * ck].astype(BF16))
        scr_ref[c:c + 1, :] = (jnp.maximum(d, 0.0) * w).sum(axis=0, keepdims=True)
    d_new = (qi.astype(F32) * kin_ref[0].astype(BF16).astype(F32)).sum(axis=-1, keepdims=True)
    s_new = (jnp.maximum(d_new, 0.0) * w).sum(axis=0, keepdims=True)

    @pl.when(b + 1 < nb)
    def _():
        for_pages(lambda p: page_copy(0, b + 1, p).start())

    keys = _sort_key(scr_ref[...])
    key_new = _sort_key(s_new)
    pos = (lax.broadcasted_iota(I32, keys.shape, 0) * ck + lax.broadcasted_iota(I32, keys.shape, 1))

    def count(pred):
        hit = jnp.where(pred(keys, pos), 1, 0)
        parts = [hit[:, j * LANES:(j + 1) * LANES] for j in range(ck // LANES)]
        while len(parts) > 1:
            parts = [a + b for a, b in zip(parts[::2], parts[1::2])] + parts[len(parts) - len(parts) % 2:]
        total = parts[0].sum(axis=1, keepdims=True).sum(axis=0, keepdims=True)
        return total + jnp.where(pred(key_new, past), 1, 0)

    dcol = lax.broadcasted_iota(I32, (16, 1), 0)

    def search_body(j, st):
        thr, cnt = st
        step = jnp.int32(1) << (28 - 4 * j)
        rows = [jnp.zeros((1, LANES), I32)]
        for d in range(1, 16):
            hit = jnp.where(keys >= thr + d * step, 1, 0)
            parts = [hit[:, t * LANES:(t + 1) * LANES] for t in range(ck // LANES)]
            while len(parts) > 1:
                parts = [a + b for a, b in zip(parts[::2], parts[1::2])] + parts[len(parts) - len(parts) % 2:]
            rows.append(parts[0].sum(axis=0, keepdims=True))
        cands = thr + dcol * step
        cnts = jnp.concatenate(rows, axis=0).sum(axis=1, keepdims=True) + jnp.where(key_new >= cands, 1, 0)
        ok = jnp.logical_and(cnts >= k_sel, dcol >= 1)
        digit = jnp.where(ok, 1, 0).sum(axis=0, keepdims=True)
        best = jnp.where(ok, cnts, INT_MAX).min(axis=0, keepdims=True)
        return thr + digit * step, jnp.where(digit > 0, best, cnt)

    st0 = (jnp.full((1, 1), INT_MIN, I32), jnp.full((1, 1), past + 1, I32))
    thr, cnt = lax.fori_loop(0, 8, search_body, st0)
    cut_ref[...] = jnp.full((1, 1), INT_MAX, I32)

    @pl.when(jnp.max(jnp.where(cnt > k_sel, 1, 0)) > 0)
    def _():
        want = k_sel - count(lambda kk, _: kk > thr)

        def tie_body(j, cpos):
            cand = cpos + (jnp.int32(1) << (idx_bits - 1 - j))
            below = count(lambda kk, pp: jnp.where(kk == thr, pp, INT_MAX) < cand)
            return jnp.where(below < want, cand, cpos)

        cut_ref[...] = lax.fori_loop(0, idx_bits, tie_body, jnp.zeros((1, 1), I32))

    cut = cut_ref[...]

    def sel_bias(kk, pp):
        tie = jnp.where(pp <= cut, 0.0, NEG)
        return jnp.where(kk > thr, 0.0, jnp.where(kk == thr, tie, NEG))

    bias = sel_bias(keys, pos)
    bias_new = sel_bias(key_new, past)

    wait_all(1)
    wait_all(2)

    qa = qa_ref[0]
    heads = []
    for g in range(N_KV_A):
        qg = qa[g * GQA:(g + 1) * GQA]
        cols = slice(g * HEAD_DIM_A, (g + 1) * HEAD_DIM_A)
        spans = [slice(c * ck, (c + 1) * ck) for c in range(nc)]
        lgs = [_dot(qg, k_buf[g, :, sp].astype(BF16)) + bias[c:c + 1, :] for c, sp in enumerate(spans)]
        kn = kn_ref[0][:, cols].astype(BF16).astype(F32)
        vn = vn_ref[0][:, cols].astype(BF16).astype(F32)
        lg_new = (qg.astype(F32) * kn).sum(axis=-1, keepdims=True) + bias_new
        m = lg_new
        for lg in lgs:
            m = jnp.maximum(m, lg.max(axis=-1, keepdims=True))
        p_new = jnp.exp2(lg_new - m)
        l = p_new
        acc = p_new * vn
        for lg, sp in zip(lgs, spans):
            p = jnp.exp2(lg - m)
            l = l + p.sum(axis=-1, keepdims=True)
            acc = acc + _dot_nt(p.astype(BF16), v_buf[g, :, sp].astype(BF16))
        o = acc / l
        heads += [o[hh:hh + 1, :] for hh in range(GQA)]
    out_ref[0] = jnp.concatenate(heads, axis=1)


def _sample_attention(page_table, qi_s, qa_s, wi_s, ki_new, k_new, v_new, cache_idx_k, cache_k, cache_v):
    nbatch, n_pages = page_table.shape
    n_pool, ps, d_idx = cache_idx_k.shape
    past = n_pages * ps
    ck = min(SAMPLE_CK, past)
    assert past % ck == 0 and ck % ps == 0
    k_sel = min(TOPK_MAX, (past + 1) // 4)
    kern = functools.partial(_sample_attn_kernel, n_pages=n_pages, ps=ps, ck=ck, k_sel=k_sel,
                             idx_bits=past.bit_length())
    per_b = lambda shape: pl.BlockSpec((1,) + shape, lambda b, pt: (b, 0, 0))
    any_spec = pl.BlockSpec(memory_space=pl.ANY)
    grid_spec = pltpu.PrefetchScalarGridSpec(
        num_scalar_prefetch=1, grid=(nbatch,),
        in_specs=[per_b((N_HEADS_IDX, d_idx)), per_b((N_HEADS_A, HEAD_DIM_A)), per_b((N_HEADS_IDX, 1)),
                  per_b((1, d_idx)), per_b((1, WA_KV)), per_b((1, WA_KV)), any_spec, any_spec, any_spec],
        out_specs=per_b((1, WA_Q)),
        scratch_shapes=[pltpu.VMEM((d_idx, past), F32), pltpu.VMEM((N_KV_A, HEAD_DIM_A, past), F32),
                        pltpu.VMEM((N_KV_A, HEAD_DIM_A, past), F32),
                        pltpu.VMEM((past // ck, ck), F32), pltpu.VMEM((1, 1), I32),
                        pltpu.SemaphoreType.DMA((3,))])
    return pl.pallas_call(kern, out_shape=jax.ShapeDtypeStruct((nbatch, 1, WA_Q), F32), grid_spec=grid_spec,
                          compiler_params=_cparams(), name="sample_attn")(
                              page_table, qi_s, qa_s, wi_s, ki_new, k_new, v_new,
                              jnp.transpose(cache_idx_k, (0, 2, 1)), jnp.transpose(cache_k, (0, 2, 3, 1)),
                              jnp.transpose(cache_v, (0, 2, 3, 1)))


def _rope_tables(pos):
    rot = HEAD_DIM_A // ROT_FRACTION
    half = rot // 2
    inv_freq = ROPE_THETA ** (-jnp.arange(half, dtype=F32) / half)
    ang = pos.astype(F32)[:, None] * inv_freq[None, :]
    cos, sin = jnp.cos(ang), jnp.sin(ang)
    n = pos.shape[0]
    z = lambda w: jnp.zeros((n, w), F32)
    c = jnp.concatenate([cos, cos, jnp.ones((n, HEAD_DIM_A - rot), F32)], axis=1)
    sa = jnp.concatenate([-sin, z(HEAD_DIM_A - half)], axis=1)
    sb = jnp.concatenate([z(half), sin, z(HEAD_DIM_A - rot)], axis=1)
    rep = LANES // HEAD_DIM_A
    return tuple(jnp.tile(a, (1, rep)) for a in (c, sa, sb))


def _prep_params(w_in, g_in, g_qa, g_ka):
    d = w_in.shape[0]
    w_qa, w_ka, w_va, w_qi, w_ki, w_wi = _split_w_in(w_in, d)[:6]
    wa = jnp.concatenate([w_qa, w_ka, w_va, w_qi, w_ki, jnp.zeros((d, LANES - D_IDX), w_in.dtype)], axis=1)
    lane = jnp.arange(LANES)
    bd = (lane[:, None] // HEAD_DIM_A == lane[None, :] // HEAD_DIM_A).astype(BF16)
    rep = LANES // HEAD_DIM_A
    return (g_in.reshape(1, d), wa.astype(BF16), w_wi.T.astype(BF16), w_va.T.astype(BF16), bd,
            jnp.tile(g_qa, rep).reshape(1, LANES), jnp.tile(g_ka, rep).reshape(1, LANES))


def _split_w_in(w_in, d):
    sizes = (WA_Q, WA_KV, WA_KV, WI_Q, D_IDX, N_HEADS_IDX, WIDTH_B, WIDTH_B, WM_Q, N_BRANCH * d)
    parts, o = [], 0
    for s in sizes:
        parts.append(w_in[:, o:o + s])
        o += s
    return parts


def _mix_kernel(*refs, chunked, shared_mem, emit_vb, n_experts):
    it = iter(refs)
    x_ref, oa_ref, gin_ref, wb_ref, gvb_ref, gqm_ref = (next(it) for _ in range(6))
    if chunked:
        wtril_ref, bt_ref = next(it), next(it)
    else:
        wdiag_ref, bdiag_ref = next(it), next(it)
    mk_ref, mv_ref = next(it), next(it)
    wbra_ref, wbrb_ref, wbrm_ref, wout_ref, gffn_ref, wrh_ref, wrl_ref, br_ref = (next(it) for _ in range(8))
    x1_ref, h2_ref, te_ref, tw_ref = (next(it) for _ in range(4))
    vb_ref = next(it) if emit_vb else None

    x = x_ref[...]
    tm, d = x.shape
    h = _rms(x, gin_ref[...]).astype(BF16)
    p = _dot(h, wb_ref[...])
    ub = p[:, :WIDTH_B]
    vb = _rms(p[:, WIDTH_B:2 * WIDTH_B], gvb_ref[...])
    if emit_vb:
        vb_ref[...] = vb
    o = 2 * WIDTH_B
    qm = p[:, o:o + WM_Q]
    o += WM_Q
    gates = p[:, o:o + N_BRANCH * d]

    if chunked:
        vbb = vb.astype(BF16)
        rows = []
        for cc in range(tm // CHUNK):
            cols = []
            for g in range(N_GROUPS_B):
                vg = vbb[cc * CHUNK:(cc + 1) * CHUNK, g * GROUP_DIM_B:(g + 1) * GROUP_DIM_B]
                cols.append(_dot(wtril_ref[g], vg))
            rows.append(jnp.concatenate(cols, axis=1) + bt_ref[...])
        z = jnp.concatenate(rows, axis=0) if len(rows) > 1 else rows[0]
    else:
        z = vb * wdiag_ref[...] + bdiag_ref[...]
    out_b = ub * z

    scale_m = HEAD_DIM_M ** -0.5
    outs = []
    for hh in range(N_HEADS_M):
        sl = slice(hh * HEAD_DIM_M, (hh + 1) * HEAD_DIM_M)
        qh = _rms(qm[:, sl], gqm_ref[...])
        if shared_mem:
            lg = _dot_nt(qh.astype(BF16), mk_ref[:, sl].astype(BF16)) * scale_m
            pm = jnp.exp(lg - lg.max(axis=-1, keepdims=True))
            pm = pm / pm.sum(axis=-1, keepdims=True)
            outs.append(_dot(pm.astype(BF16), mv_ref[:, sl].astype(BF16)))
        else:
            per_row = []
            for r in range(tm):
                kr = mk_ref[r, :, sl]
                lg = (kr * qh[r:r + 1, :]).sum(axis=-1, keepdims=True) * scale_m
                pm = jnp.exp(lg - lg.max(axis=0, keepdims=True))
                pm = pm / pm.sum(axis=0, keepdims=True)
                per_row.append((pm * mv_ref[r, :, sl]).sum(axis=0, keepdims=True))
            outs.append(jnp.concatenate(per_row, axis=0))
    out_m = jnp.concatenate(outs, axis=1)

    sig = jax.nn.sigmoid
    merged = (sig(gates[:, :d]) * _dot(oa_ref[...].astype(BF16), wbra_ref[...])
              + sig(gates[:, d:2 * d]) * _dot(out_b.astype(BF16), wbrb_ref[...])
              + sig(gates[:, 2 * d:]) * _dot(out_m.astype(BF16), wbrm_ref[...]))
    x1 = x + _dot(merged.astype(BF16), wout_ref[...])
    x1_ref[...] = x1
    h2 = _rms(x1, gffn_ref[...])
    h2_ref[...] = h2.astype(h2_ref.dtype)

    hi = h2.astype(BF16)
    lo = (h2 - hi.astype(F32)).astype(BF16)
    lg = _dot(hi, wrh_ref[...]) + _dot(hi, wrl_ref[...]) + _dot(lo, wrh_ref[...]) + br_ref[...]
    lane = lax.broadcasted_iota(I32, lg.shape, 1)
    lg = jnp.where(lane < n_experts, lg, -jnp.inf)
    vals, idxs = [], []
    for _ in range(TOP_K_MOE):
        m = lg.max(axis=-1, keepdims=True)
        idx = jnp.where(lg == m, lane, LANES).min(axis=-1, keepdims=True)
        vals.append(m)
        idxs.append(idx)
        lg = jnp.where(lane == idx, -jnp.inf, lg)
    ex = [jnp.exp(v - vals[0]) for v in vals]
    den = ex[0]
    for e in ex[1:]:
        den = den + e
    te = jnp.full(lane.shape, -1, I32)
    tw = jnp.zeros(lane.shape, F32)
    for j in range(TOP_K_MOE):
        te = jnp.where(lane == j, idxs[j], te)
        tw = jnp.where(lane == j, ex[j] / den, tw)
    te_ref[...] = te
    tw_ref[...] = tw


def _mix(x2d, out_a, mix_params, gmlp_params, mem_k, mem_v, *, tm, chunked, shared_mem, emit_vb, n_experts):
    n, d = x2d.shape
    gin, wb, gvb, gqm, wbra, wbrb, wbrm, wout, gffn, wrh, wrl, br = mix_params
    row = lambda w: pl.BlockSpec((tm, w), lambda i: (i, 0))
    if shared_mem:
        mem_specs = [_const_spec(mem_k.shape), _const_spec(mem_v.shape)]
    else:
        mspec = pl.BlockSpec((tm,) + mem_k.shape[1:], lambda i: (i, 0, 0))
        mem_specs = [mspec, mspec]
    consts = lambda arrs: [_const_spec(a.shape) for a in arrs]
    in_specs = ([row(d), row(WA_Q)] + consts([gin, wb, gvb, gqm]) + consts(gmlp_params) + mem_specs
                + consts([wbra, wbrb, wbrm, wout, gffn, wrh, wrl, br]))
    out_shape = [jax.ShapeDtypeStruct((n, d), F32), jax.ShapeDtypeStruct((n, d), F32),
                 jax.ShapeDtypeStruct((n, LANES), I32), jax.ShapeDtypeStruct((n, LANES), F32)]
    out_specs = [row(d), row(d), row(LANES), row(LANES)]
    if emit_vb:
        out_shape.append(jax.ShapeDtypeStruct((n, WIDTH_B), F32))
        out_specs.append(row(WIDTH_B))
    kern = functools.partial(_mix_kernel, chunked=chunked, shared_mem=shared_mem, emit_vb=emit_vb,
                             n_experts=n_experts)
    return pl.pallas_call(kern, out_shape=tuple(out_shape), grid=(n // tm,), in_specs=in_specs,
                          out_specs=tuple(out_specs), compiler_params=_cparams(), name="mix")(
                              x2d, out_a, gin, wb, gvb, gqm, *gmlp_params, mem_k, mem_v,
                              wbra, wbrb, wbrm, wout, gffn, wrh, wrl, br)


def _prep_mix_params(w_in, g_in, g_vb, g_qm, w_br_a, w_br_b, w_br_m, w_out, g_ffn, w_router, b_router):
    d = w_in.shape[0]
    parts = _split_w_in(w_in, d)
    wb = jnp.concatenate(parts[6:10], axis=1).astype(BF16)
    n_e = w_router.shape[1]
    wr = jnp.pad(w_router, ((0, 0), (0, LANES - n_e)))
    wrh = wr.astype(BF16)
    wrl = (wr - wrh.astype(F32)).astype(BF16)
    br = jnp.pad(b_router, (0, LANES - n_e)).reshape(1, LANES)
    return (g_in.reshape(1, d), wb, g_vb.reshape(1, WIDTH_B), g_qm.reshape(1, HEAD_DIM_M),
            w_br_a.astype(BF16), w_br_b.astype(BF16), w_br_m.astype(BF16), w_out.astype(BF16),
            g_ffn.reshape(1, d), wrh, wrl, br)


def _prep_gmlp_chunked(w_s, b_s):
    tril = jnp.tril(jnp.ones((CHUNK, CHUNK), w_s.dtype))
    wtril = (w_s * tril).astype(BF16)
    bt = jnp.repeat(b_s.T, GROUP_DIM_B, axis=1)
    return wtril, bt


def _prep_gmlp_single(w_s, b_s):
    wdiag = jnp.repeat(w_s[:, 0, 0], GROUP_DIM_B).reshape(1, WIDTH_B)
    bdiag = jnp.repeat(b_s[:, 0], GROUP_DIM_B).reshape(1, WIDTH_B)
    return wdiag, bdiag


MOE_TP = 256


def _plan_kernel(te_ref, tri_ref, upper_ref, lslot_ref, cnt_ref, tstart_ref, gprev_ref, total_ref, carry_ref):
    @pl.when(pl.program_id(0) == 0)
    def _():
        carry_ref[...] = jnp.zeros(carry_ref.shape, F32)

    te = te_ref[...]
    lane = lax.broadcasted_iota(I32, te.shape, 1)
    picks = [te[:, j:j + 1] for j in range(TOP_K_MOE)]
    onehot = jnp.zeros(te.shape, F32)
    for e in picks:
        onehot = onehot + jnp.where(lane == e, 1.0, 0.0)
    before = _dot(tri_ref[...], onehot.astype(BF16))
    cnt = onehot.sum(axis=0, keepdims=True)
    cnt = jnp.floor((cnt + (SUBLANES - 1)) * (1.0 / SUBLANES)) * SUBLANES
    cnt8 = jnp.broadcast_to(cnt, (SUBLANES, LANES)).astype(BF16)
    tstart = _dot(cnt8, upper_ref[...])[0:1, :]
    lslot = jnp.zeros(te.shape, I32)
    for j, e in enumerate(picks):
        r = jnp.where(lane == e, before + tstart, 0.0).sum(axis=-1, keepdims=True).astype(I32)
        lslot = jnp.where(lane == j, r, lslot)
    lslot_ref[...] = lslot
    cnt_ref[0] = cnt.astype(I32)
    tstart_ref[0] = tstart.astype(I32)
    gprev_ref[0] = carry_ref[...].astype(I32)
    carry_ref[...] = carry_ref[...] + cnt
    total_ref[...] = carry_ref[...].astype(I32)


def _moe_plan(te_all):
    n = te_all.shape[0]
    tp = MOE_TP
    n_tiles = n // tp
    r = jnp.arange(tp)
    tri = (r[:, None] > r[None, :]).astype(BF16)
    e = jnp.arange(LANES)
    upper = (e[:, None] < e[None, :]).astype(BF16)
    per_tile = jax.ShapeDtypeStruct((n_tiles, 1, LANES), I32)
    tile_spec = pl.BlockSpec((1, 1, LANES), lambda i: (i, 0, 0))
    return pl.pallas_call(
        _plan_kernel,
        out_shape=(jax.ShapeDtypeStruct((n, LANES), I32), per_tile, per_tile, per_tile,
                   jax.ShapeDtypeStruct((1, LANES), I32)),
        grid=(n_tiles,),
        in_specs=[pl.BlockSpec((tp, LANES), lambda i: (i, 0)), _const_spec((tp, tp)), _const_spec((LANES, LANES))],
        out_specs=(pl.BlockSpec((tp, LANES), lambda i: (i, 0)), tile_spec, tile_spec, tile_spec,
                   _const_spec((1, LANES))),
        scratch_shapes=[pltpu.VMEM((1, LANES), F32)],
        compiler_params=_cparams(), name="moe_plan")(te_all, tri, upper)


def _run_pieces(tp):
    sizes, s = [], SUBLANES
    while s <= tp:
        sizes.append(s)
        s *= 2
    return tuple(reversed(sizes))


def _local_rows(tp, n_e):
    return tp * TOP_K_MOE + n_e * SUBLANES


def _for_runs(tabs, tile, n_e, tp, fn):
    cnt_s, tstart_s, gbase_s = tabs

    def per_expert(e, c):
        idx = tile * n_e + e
        length, src0, dst0 = cnt_s[idx], tstart_s[idx], gbase_s[idx]
        off = jnp.int32(0)
        for size in _run_pieces(tp):
            piece = length & size

            @pl.when(piece != 0)
            def _():
                fn(pl.multiple_of(src0 + off, SUBLANES), pl.multiple_of(dst0 + off, SUBLANES), size)
            off = off + piece
        return c

    lax.fori_loop(0, n_e, per_expert, 0, unroll=4)


def _wait_runs(tabs, tile, n_e, n_rows, copy_of):
    cnt_s, tstart_s, _ = tabs
    last = tile * n_e + n_e - 1
    total = tstart_s[last] + cnt_s[last]
    for size in _run_pieces(n_rows):
        @pl.when((total & size) != 0)
        def _():
            copy_of(size).wait()


def _slot_matrix(lslot, vals, n_rows):
    col = lax.broadcasted_iota(I32, (lslot.shape[0], n_rows), 1)
    m = jnp.zeros(col.shape, F32)
    for j in range(TOP_K_MOE):
        v = 1.0 if vals is None else vals[:, j:j + 1]
        m = m + jnp.where(lslot[:, j:j + 1] == col, v, 0.0)
    return m


def _dispatch_kernel(cnt_s, tstart_s, gbase_s, fstart_s, flen_s, nused_s, lslot_ref, h2_ref, *rest,
                     tp, n_e, tile0, first, bm):
    if first:
        xs_out, stage, sem, zbuf, zsem = rest
    else:
        _, xs_out, stage, sem = rest
    n_rows = _local_rows(tp, n_e)
    i = pl.program_id(0)
    slot = i % 2
    pt = _slot_matrix(lslot_ref[...], None, n_rows).astype(BF16)
    stage[slot] = lax.dot_general(pt, h2_ref[...].astype(BF16), (((0,), (0,)), ((), ())),
                                  preferred_element_type=F32)

    def runs(tile, s, op):
        _for_runs((cnt_s, tstart_s, gbase_s), tile0 + tile, n_e, tp,
                  lambda src, dst, size: op(pltpu.make_async_copy(
                      stage.at[s, pl.ds(src, size), :], xs_out.at[pl.ds(dst, size), :], sem.at[s])))

    def drain(tile, s):
        _wait_runs((cnt_s, tstart_s, gbase_s), tile0 + tile, n_e, n_rows, lambda size: pltpu.make_async_copy(
            stage.at[s, pl.ds(0, size), :], xs_out.at[pl.ds(0, size), :], sem.at[s]))

    @pl.when(i > 0)
    def _():
        drain(i - 1, 1 - slot)

    runs(i, slot, lambda cp: cp.start())

    @pl.when(i == pl.num_programs(0) - 1)
    def _():
        drain(i, slot)
        if first:
            zbuf[...] = jnp.zeros(zbuf.shape, F32)
            nb = xs_out.shape[0] // bm

            def fill(op):
                def per_expert(e, c):
                    length, dst0 = flen_s[e], fstart_s[e]
                    off = jnp.int32(0)
                    for size in _run_pieces(bm):
                        piece = length & size

                        @pl.when(piece != 0)
                        def _():
                            op(pltpu.make_async_copy(
                                zbuf.at[pl.ds(0, size), :],
                                xs_out.at[pl.ds(pl.multiple_of(dst0 + off, SUBLANES), size), :], zsem))
                        off = off + piece
                    return c

                lax.fori_loop(0, n_e, per_expert, 0)
                lax.fori_loop(nused_s[0], nb, lambda b, c: (op(pltpu.make_async_copy(
                    zbuf, xs_out.at[pl.ds(pl.multiple_of(b * bm, bm), bm), :], zsem)), c)[1], 0)

            fill(lambda cp: cp.start())
            fill(lambda cp: cp.wait())


def _dispatch(tabs, fill_tabs, tile0, lslot, h2, xs, n_e, ns, bm):
    n, d = h2.shape
    tp = min(MOE_TP, n)
    n_rows = _local_rows(tp, n_e)
    first = xs is None
    kern = functools.partial(_dispatch_kernel, tp=tp, n_e=n_e, tile0=tile0, first=first, bm=bm)
    any_spec = pl.BlockSpec(memory_space=pl.ANY)
    scratch = [pltpu.VMEM((2, n_rows, d), F32), pltpu.SemaphoreType.DMA((2,))]
    if first:
        scratch += [pltpu.VMEM((bm, d), F32), pltpu.SemaphoreType.DMA]
    grid_spec = pltpu.PrefetchScalarGridSpec(
        num_scalar_prefetch=6, grid=(n // tp,),
        in_specs=[pl.BlockSpec((tp, LANES), lambda i, *_: (i, 0)), pl.BlockSpec((tp, d), lambda i, *_: (i, 0))]
        + ([] if first else [any_spec]),
        out_specs=any_spec, scratch_shapes=scratch)
    args = (*tabs, *fill_tabs, lslot, h2) + (() if first else (xs,))
    return pl.pallas_call(
        kern, out_shape=jax.ShapeDtypeStruct((ns, d), F32), grid_spec=grid_spec,
        input_output_aliases={} if first else {8: 0},
        compiler_params=pltpu.CompilerParams(dimension_semantics=("arbitrary",), has_side_effects=True,
                                             vmem_limit_bytes=VMEM_LIMIT),
        name="dispatch")(*args)


def _expert_kernel(be_ref, nu_ref, xs_ref, wg_ref, wu_ref, wd_ref, bg_ref, bu_ref, bd_ref, y_ref,
                   wg_bf, wu_bf, wd_bf):
    b = pl.program_id(0)

    @pl.when(b >= nu_ref[0])
    def _():
        y_ref[...] = jnp.zeros(y_ref.shape, F32)

    @pl.when(b < nu_ref[0])
    def _():
        prev = be_ref[jnp.maximum(b - 1, 0)]

        @pl.when(jnp.logical_or(b == 0, be_ref[b] != prev))
        def _():
            wg_bf[...] = wg_ref[0].astype(BF16)
            wu_bf[...] = wu_ref[0].astype(BF16)
            wd_bf[...] = wd_ref[0].astype(BF16)

        xb = xs_ref[...].astype(BF16)
        ff = wg_bf.shape[1]
        y = jnp.zeros(y_ref.shape, F32) + bd_ref[0]
        for n in range(ff // FF_CHUNK):
            sl = slice(n * FF_CHUNK, (n + 1) * FF_CHUNK)
            hg = jnp.minimum(_dot(xb, wg_bf[:, sl]) + bg_ref[0][:, sl], SWIGLU_LIMIT)
            hu = jnp.clip(_dot(xb, wu_bf[:, sl]) + bu_ref[0][:, sl], -SWIGLU_LIMIT, SWIGLU_LIMIT)
            act = hg * jax.nn.sigmoid(SWIGLU_ALPHA * hg) * (hu + 1.0)
            y = y + _dot(act.astype(BF16), wd_bf[sl, :])
        y_ref[...] = y


def _experts(xs, block_e, n_used, w_gate, b_gate, w_up, b_up, w_down, b_down):
    ns, d = xs.shape
    n_e, _, ff = w_gate.shape
    bm = MOE_BM
    blk = lambda b, be, nu: (jnp.minimum(b, nu[0] - 1), 0)
    wsel = lambda b, be, nu: (be[b], 0, 0)
    grid_spec = pltpu.PrefetchScalarGridSpec(
        num_scalar_prefetch=2, grid=(ns // bm,),
        in_specs=[pl.BlockSpec((bm, d), blk),
                  pl.BlockSpec((1, d, ff), wsel), pl.BlockSpec((1, d, ff), wsel), pl.BlockSpec((1, ff, d), wsel),
                  pl.BlockSpec((1, 1, ff), wsel), pl.BlockSpec((1, 1, ff), wsel), pl.BlockSpec((1, 1, d), wsel)],
        out_specs=pl.BlockSpec((bm, d), lambda b, be, nu: (b, 0)),
        scratch_shapes=[pltpu.VMEM((d, ff), BF16), pltpu.VMEM((d, ff), BF16), pltpu.VMEM((ff, d), BF16)])
    return pl.pallas_call(_expert_kernel, out_shape=jax.ShapeDtypeStruct((ns, d), F32), grid_spec=grid_spec,
                          compiler_params=_cparams(), name="experts")(
                              block_e, n_used, xs, w_gate, w_up, w_down,
                              b_gate.reshape(n_e, 1, ff), b_up.reshape(n_e, 1, ff), b_down.reshape(n_e, 1, d))


def _combine_kernel(cnt_s, tstart_s, gbase_s, lslot_ref, tw_ref, x1_ref, yb_hbm, y_ref, buf, sem, *, tp, n_e, tile0):
    n_rows = _local_rows(tp, n_e)
    i = pl.program_id(0)
    slot = i % 2

    def runs(tile, s, op):
        _for_runs((cnt_s, tstart_s, gbase_s), tile0 + tile, n_e, tp,
                  lambda loc, glob, size: op(pltpu.make_async_copy(
                      yb_hbm.at[pl.ds(glob, size), :], buf.at[s, pl.ds(loc, size), :], sem.at[s])))

    @pl.when(i == 0)
    def _():
        buf[...] = jnp.zeros(buf.shape, F32)
        runs(i, slot, lambda cp: cp.start())

    @pl.when(i + 1 < pl.num_programs(0))
    def _():
        runs(i + 1, 1 - slot, lambda cp: cp.start())

    _wait_runs((cnt_s, tstart_s, gbase_s), tile0 + i, n_e, n_rows, lambda size: pltpu.make_async_copy(
        yb_hbm.at[pl.ds(0, size), :], buf.at[slot, pl.ds(0, size), :], sem.at[slot]))

    ptw = _slot_matrix(lslot_ref[...], tw_ref[...], n_rows)
    rows = buf[slot]
    w_hi = ptw.astype(BF16)
    w_lo = (ptw - w_hi.astype(F32)).astype(BF16)
    r_hi = rows.astype(BF16)
    r_lo = (rows - r_hi.astype(F32)).astype(BF16)
    y_ref[...] = x1_ref[...] + _dot(w_hi, r_hi) + (_dot(w_hi, r_lo) + _dot(w_lo, r_hi))


def _combine(tabs, tile0, lslot, tw, x1, yb, n_e):
    n, d = x1.shape
    tp = min(MOE_TP, n)
    n_rows = _local_rows(tp, n_e)
    kern = functools.partial(_combine_kernel, tp=tp, n_e=n_e, tile0=tile0)
    row = lambda w: pl.BlockSpec((tp, w), lambda i, *_: (i, 0))
    grid_spec = pltpu.PrefetchScalarGridSpec(
        num_scalar_prefetch=3, grid=(n // tp,),
        in_specs=[row(LANES), row(LANES), row(d), pl.BlockSpec(memory_space=pl.ANY)],
        out_specs=row(d),
        scratch_shapes=[pltpu.VMEM((2, n_rows, d), F32), pltpu.SemaphoreType.DMA((2,))])
    return pl.pallas_call(kern, out_shape=jax.ShapeDtypeStruct(x1.shape, F32), grid_spec=grid_spec,
                          compiler_params=_cparams(), name="combine")(*tabs, lslot, tw, x1, yb)


def _moe(groups, w_gate, b_gate, w_up, b_up, w_down, b_down):
    n_e = w_gate.shape[0]
    d = groups[0][0].shape[1]
    tp, bm = MOE_TP, MOE_BM
    sizes = [g[0].shape[0] for g in groups]
    assert all(n % tp == 0 or n < tp for n in sizes), sizes
    parts, tile0s, o = [], [], 0
    for g, n in zip(groups, sizes):
        n_pad = -(-n // tp) * tp
        parts.append(jnp.pad(g[2], ((0, n_pad - n), (0, 0)), constant_values=-1))
        tile0s.append(o // tp)
        o += n_pad
    lslot, cnt3, tstart3, gprev3, total = _moe_plan(jnp.concatenate(parts, axis=0))
    counts = total[0, :n_e]
    padded_cnt = (counts + bm - 1) // bm * bm
    pad_end = jnp.cumsum(padded_cnt)
    pad_start = pad_end - padded_cnt
    n_assign = sum(sizes) * TOP_K_MOE
    n_slack = (o // tp) * n_e * (SUBLANES - 1)
    nb = -(-(n_assign + n_slack) // bm) + n_e
    n_used = (pad_end[-1] // bm).astype(I32)
    blocks = jnp.arange(nb, dtype=I32)
    block_e = jnp.minimum((pad_end[None, :] <= (blocks * bm)[:, None]).sum(axis=1), n_e - 1).astype(I32)
    block_e = jnp.where(blocks < n_used, block_e, block_e[jnp.maximum(n_used - 1, 0)])
    flat = lambda a: a[:, 0, :n_e].reshape(-1).astype(I32)
    gbase3 = pad_start[None, None, :] + gprev3[:, :, :n_e]
    tabs = (flat(cnt3), flat(tstart3), flat(gbase3))

    after_first = gprev3[tile0s[1], 0, :n_e] if len(groups) > 1 else counts
    fill_start = (pad_start + after_first).astype(I32)
    fill_tabs = (fill_start, (pad_end - fill_start).astype(I32), n_used.reshape(1))
    xs, lslots = None, []
    for (x1, h2, te, tw), n, t0 in zip(groups, sizes, tile0s):
        ls = lslot[t0 * tp:t0 * tp + n]
        lslots.append(ls)
        xs = _dispatch(tabs, fill_tabs, t0, ls, h2, xs, n_e, nb * bm, bm)
    yb = _experts(xs, block_e, n_used.reshape(1), w_gate, b_gate, w_up, b_up, w_down, b_down)
    return [_combine(tabs, t0, ls, tw, x1, yb, n_e)
            for (x1, h2, te, tw), ls, t0 in zip(groups, lslots, tile0s)]


def _row_tile(n, want):
    return want if n % want == 0 else n


def kernel(x_prompt, x_sample, mem_prompt, cache_k, cache_v, cache_idx_k, cache_mem_k, cache_mem_v, page_table,
           g_in, w_in, g_qa, g_ka, g_vb, w_s, b_s, g_qm, g_mem, w_mem_kv, g_km, w_br_a, w_br_b, w_br_m, w_out,
           g_ffn, w_router, b_router, w_gate, b_gate, w_up, b_up, w_down, b_down):
    bp, t, d = x_prompt.shape
    bs, ts, _ = x_sample.shape
    assert bp == 1 and ts == 1 and t % CHUNK == 0 and cache_k.shape[1] == LANES
    n_mem = mem_prompt.shape[1]
    n_pages = page_table.shape[1]
    ps = cache_k.shape[1]
    past = n_pages * ps
    n_e = w_router.shape[1]

    fp = _prep_params(w_in, g_in, g_qa, g_ka)
    mp = _prep_mix_params(w_in, g_in, g_vb, g_qm, w_br_a, w_br_b, w_br_m, w_out, g_ffn, w_router, b_router)

    xp = x_prompt.reshape(t, d)
    qa_p, k32_p, khm_p, v32_p, vt_p, qi_p, ki32_p, kibf_p, wit_p = _front(
        xp, _rope_tables(jnp.arange(t, dtype=I32)), *fp, tm=_row_tile(t, 512), paged=True)
    mem_k, mem_v = _memory_kv(mem_prompt.reshape(n_mem, d), g_mem.reshape(1, d), w_mem_kv.astype(BF16),
                              g_km.reshape(1, HEAD_DIM_M))
    out_a_p = _prompt_attention(qi_p, qa_p, wit_p, kibf_p, khm_p, vt_p)
    x1_p, h2_p, te_p, tw_p = _mix(xp, out_a_p, mp, _prep_gmlp_chunked(w_s, b_s), mem_k, mem_v,
                                  tm=_row_tile(t, 512), chunked=True, shared_mem=True, emit_vb=False, n_experts=n_e)

    xs = x_sample.reshape(bs, d)
    qa_s, k32_s, _, v32_s, _, qi_s, ki32_s, _, wit_s = _front(
        xs, _rope_tables(jnp.full((bs,), past, I32)), *fp, tm=bs, paged=False)
    out_a_s = _sample_attention(page_table, jnp.moveaxis(qi_s, 0, 1), jnp.moveaxis(qa_s, 0, 1),
                                wit_s.T.reshape(bs, N_HEADS_IDX, 1), ki32_s.reshape(bs, 1, D_IDX),
                                k32_s.reshape(bs, 1, WA_KV), v32_s.reshape(bs, 1, WA_KV),
                                cache_idx_k, cache_k, cache_v)
    x1_s, h2_s, te_s, tw_s, vb_s = _mix(xs, out_a_s.reshape(bs, WA_Q), mp, _prep_gmlp_single(w_s, b_s),
                                        cache_mem_k.reshape(bs, n_mem, WM_Q), cache_mem_v.reshape(bs, n_mem, WM_Q),
                                        tm=_row_tile(bs, SUBLANES), chunked=False, shared_mem=False, emit_vb=True,
                                        n_experts=n_e)

    y_p, y_s = _moe([(x1_p, h2_p, te_p, tw_p), (x1_s, h2_s, te_s, tw_s)],
                    w_gate, b_gate, w_up, b_up, w_down, b_down)

    n_pg = t // ps
    paged_kv = lambda a: jnp.transpose(a.reshape(1, n_pg, N_KV_A, HEAD_DIM_A, ps), (0, 1, 4, 2, 3))
    return (y_p.reshape(1, t, d), y_s.reshape(bs, 1, d),
            paged_kv(k32_p), paged_kv(v32_p), jnp.transpose(ki32_p, (0, 2, 1)).reshape(1, n_pg, ps, D_IDX),
            mem_k.reshape(1, n_mem, N_HEADS_M, HEAD_DIM_M), mem_v.reshape(1, n_mem, N_HEADS_M, HEAD_DIM_M),
            k32_s.reshape(bs, 1, N_KV_A, HEAD_DIM_A), v32_s.reshape(bs, 1, N_KV_A, HEAD_DIM_A),
            ki32_s.reshape(bs, 1, D_IDX), vb_s.reshape(bs, 1, WIDTH_B))
```

```python
import functools

import jax
import jax.numpy as jnp
from jax import lax
from jax.experimental import pallas as pl
from jax.experimental.pallas import tpu as pltpu

F32 = jnp.float32
BF16 = jnp.bfloat16
I32 = jnp.int32

N_HEADS_A = 8
N_KV_A = 2
HEAD_DIM_A = 64
TOPK_MAX = 256
N_HEADS_IDX = 8
D_IDX = 64
N_GROUPS_B = 4
GROUP_DIM_B = 128
WIDTH_B = N_GROUPS_B * GROUP_DIM_B
CHUNK = 128
N_HEADS_M = 4
HEAD_DIM_M = 128
ROPE_THETA = 500000.0
ROT_FRACTION = 4
N_BRANCH = 3
TOP_K_MOE = 4
SWIGLU_LIMIT = 7.0
SWIGLU_ALPHA = 1.702
EPS = 1e-6

WA_Q = N_HEADS_A * HEAD_DIM_A
WA_KV = N_KV_A * HEAD_DIM_A
WI_Q = N_HEADS_IDX * D_IDX
WM_Q = N_HEADS_M * HEAD_DIM_M
GQA = N_HEADS_A // N_KV_A

LANES = 128
SUBLANES = 8
VMEM_LIMIT = 56 * 1024 * 1024

LOG2_E = 1.4426950408889634
INT_MIN = -(2 ** 31)
INT_MAX = 2 ** 31 - 1
NEG = -1e30

VT_ROWS = HEAD_DIM_A + 16
ATTN_BQ = 128
ATTN_SC = 512
MOE_BM = 512
FF_CHUNK = 512


def _cparams(n_axes=1, vmem=VMEM_LIMIT):
    return pltpu.CompilerParams(dimension_semantics=("arbitrary",) * n_axes, vmem_limit_bytes=vmem)


def _const_spec(shape, single=False):
    zeros = (0,) * len(shape)
    if single:
        return pl.BlockSpec(shape, lambda *_: zeros, pipeline_mode=pl.Buffered(1))
    return pl.BlockSpec(shape, lambda *_: zeros)


def _dot(a, b):
    return jnp.dot(a, b, preferred_element_type=F32)


def _dot_nt(a, b):
    return lax.dot_general(a, b, (((1,), (1,)), ((), ())), preferred_element_type=F32)


def _rms(x, g):
    return x * lax.rsqrt(jnp.mean(x * x, axis=-1, keepdims=True) + EPS) * g


def _split_dot(x, w):
    hi = x.astype(BF16)
    lo = (x - hi.astype(F32)).astype(BF16)
    return _dot(hi, w) + _dot(lo, w)


def _front_kernel(x_ref, gin_ref, wa_ref, wwit_ref, wvt_ref, bd_ref, gqa_ref, gka_ref, cos_ref, sa_ref, sb_ref,
                  qa_ref, k32_ref, khm_ref, v32_ref, vt_ref, qi_ref, ki32_ref, kibf_ref, wit_ref, *, score_scale, paged):
    x = x_ref[...]
    h = _rms(x, gin_ref[...]).astype(BF16)
    p = _dot(h, wa_ref[...])
    cos, sa, sb = cos_ref[...], sa_ref[...], sb_ref[...]
    bd = bd_ref[...]

    def head_norm(v, g):
        ssq = _split_dot(v * v, bd)
        return v * lax.rsqrt(ssq * (1.0 / HEAD_DIM_A) + EPS) * g

    def rope(v):
        return v * cos + pltpu.roll(v, LANES - 8, 1) * sa + pltpu.roll(v, 8, 1) * sb

    for s in range(WA_Q // LANES):
        v = p[:, s * LANES:(s + 1) * LANES]
        v = rope(head_norm(v, gqa_ref[...])) * (HEAD_DIM_A ** -0.5 * LOG2_E)
        vb = v.astype(BF16)
        qa_ref[2 * s] = vb[:, :HEAD_DIM_A]
        qa_ref[2 * s + 1] = vb[:, HEAD_DIM_A:]
    o = WA_Q
    n_pg = x.shape[0] // LANES
    k = rope(head_norm(p[:, o:o + WA_KV], gka_ref[...]))
    if paged:
        for j in range(n_pg):
            k32_ref[j] = k[j * LANES:(j + 1) * LANES, :].T
    else:
        k32_ref[...] = k
    kb = k.astype(BF16)
    khm_ref[0] = kb[:, :HEAD_DIM_A]
    khm_ref[1] = kb[:, HEAD_DIM_A:]
    o += WA_KV
    if not paged:
        v32_ref[...] = p[:, o:o + WA_KV]
    o += WA_KV
    for s in range(WI_Q // LANES):
        vb = rope(p[:, o + s * LANES:o + (s + 1) * LANES]).astype(BF16)
        qi_ref[2 * s] = vb[:, :D_IDX]
        qi_ref[2 * s + 1] = vb[:, D_IDX:]
    o += WI_Q
    ki_slab = rope(p[:, o:o + LANES])
    ki = ki_slab[:, :D_IDX]
    if paged:
        for j in range(n_pg):
            ki32_ref[j] = ki_slab[j * LANES:(j + 1) * LANES, :].T[:D_IDX, :]
    else:
        ki32_ref[...] = ki
    kibf_ref[...] = ki.astype(BF16)
    wit_ref[...] = _dot_nt(wwit_ref[...], h) * score_scale
    vt = _dot_nt(wvt_ref[...], h)
    if paged:
        for j in range(n_pg):
            v32_ref[j] = vt[:, j * LANES:(j + 1) * LANES]
    tail = jnp.where(lax.broadcasted_iota(I32, (VT_ROWS - HEAD_DIM_A, vt.shape[1]), 0) == 0, 1.0, 0.0)
    for g in range(N_KV_A):
        vt_ref[g] = jnp.concatenate([vt[g * HEAD_DIM_A:(g + 1) * HEAD_DIM_A], tail], axis=0).astype(BF16)


def _front(x2d, tabs, gin, wa, wwit, wvt, bd, gqa, gka, tm, paged):
    n, d = x2d.shape
    cos, sa, sb = tabs
    row = lambda w: pl.BlockSpec((tm, w), lambda i: (i, 0))
    hm = lambda nh, w: pl.BlockSpec((nh, tm, w), lambda i: (0, i, 0))
    if paged:
        assert tm % LANES == 0
        f32_out = lambda w: jax.ShapeDtypeStruct((n // LANES, w, LANES), F32)
        f32_spec = lambda w: pl.BlockSpec((tm // LANES, w, LANES), lambda i: (i, 0, 0))
    else:
        f32_out = lambda w: jax.ShapeDtypeStruct((n, w), F32)
        f32_spec = row
    out_shape = (
        jax.ShapeDtypeStruct((N_HEADS_A, n, HEAD_DIM_A), BF16),
        f32_out(WA_KV),
        jax.ShapeDtypeStruct((N_KV_A, n, HEAD_DIM_A), BF16),
        f32_out(WA_KV),
        jax.ShapeDtypeStruct((N_KV_A, VT_ROWS, n), BF16),
        jax.ShapeDtypeStruct((N_HEADS_IDX, n, D_IDX), BF16),
        f32_out(D_IDX),
        jax.ShapeDtypeStruct((n, D_IDX), BF16),
        jax.ShapeDtypeStruct((N_HEADS_IDX, n), F32),
    )
    out_specs = (hm(N_HEADS_A, HEAD_DIM_A), f32_spec(WA_KV), hm(N_KV_A, HEAD_DIM_A), f32_spec(WA_KV),
                 pl.BlockSpec((N_KV_A, VT_ROWS, tm), lambda i: (0, 0, i)), hm(N_HEADS_IDX, D_IDX), f32_spec(D_IDX),
                 row(D_IDX), pl.BlockSpec((N_HEADS_IDX, tm), lambda i: (0, i)))
    in_specs = [row(d), _const_spec(gin.shape), _const_spec(wa.shape), _const_spec(wwit.shape),
                _const_spec(wvt.shape), _const_spec(bd.shape), _const_spec(gqa.shape), _const_spec(gka.shape),
                row(LANES), row(LANES), row(LANES)]
    kern = functools.partial(_front_kernel, score_scale=D_IDX ** -0.5 * N_HEADS_IDX ** -0.5, paged=paged)
    return pl.pallas_call(kern, out_shape=out_shape, grid=(n // tm,), in_specs=in_specs, out_specs=out_specs,
                          compiler_params=_cparams(), name="front")(x2d, gin, wa, wwit, wvt, bd, gqa, gka, cos, sa, sb)


def _memkv_kernel(mem_ref, gmem_ref, w_ref, gkm_ref, k32_ref, v32_ref):
    h = _rms(mem_ref[...], gmem_ref[...]).astype(BF16)
    kv = _dot(h, w_ref[...])
    for hh in range(N_HEADS_M):
        sl = slice(hh * HEAD_DIM_M, (hh + 1) * HEAD_DIM_M)
        k32_ref[:, sl] = _rms(kv[:, sl], gkm_ref[...])
    v32_ref[...] = kv[:, WM_Q:]


def _memory_kv(mem2d, gmem, w, gkm):
    n = mem2d.shape[0]
    out = jax.ShapeDtypeStruct((n, WM_Q), F32)
    return pl.pallas_call(_memkv_kernel, out_shape=(out, out), name="memory_kv",
                          compiler_params=pltpu.CompilerParams(vmem_limit_bytes=VMEM_LIMIT))(mem2d, gmem, w, gkm)


def _sort_key(s):
    s = jnp.where(s == 0.0, 0.0, s)
    bits = pltpu.bitcast(s, I32)
    return bits ^ ((bits >> 31) & INT_MAX)


KEYS_PER_GROUP = 32 * SUBLANES
SEARCH_SLAB = 128


def _bit_planes(words):
    a = list(words)
    j, m = 16, 0x0000FFFF
    while j:
        k = 0
        while k < 32:
            t = (a[k] ^ lax.shift_right_logical(a[k + j], jnp.int32(j))) & m
            a[k] = a[k] ^ t
            a[k + j] = a[k + j] ^ (t << j)
            k = (k + j + 1) & ~j
        j >>= 1
        m = (m ^ (m << j)) & 0xFFFFFFFF if j else m
    return a


ATTN_SUB = 256


def _attn_kernel(qi_ref, qa_ref, wi_ref, ki_ref, k_ref, vt_ref, out_ref,
                 keys_ref, planes_ref, cand_ref, cut_ref, bias_ref, lg_ref, cm_ref, m_ref, acc_ref, *, k_sel, idx_bits):
    bq, sc = ATTN_BQ, ATTN_SC
    wpc = sc // 32
    i = pl.program_id(0)
    n_ch = (i * bq) // sc + 1
    t_pos = i * bq + lax.broadcasted_iota(I32, (1, bq), 1)
    sub = lax.broadcasted_iota(I32, (sc, 1), 0)
    qi = qi_ref[...].reshape(N_HEADS_IDX * bq, D_IDX)
    w = wi_ref[...]

    @pl.when(i == 0)
    def _():
        planes_ref[...] = jnp.zeros(planes_ref.shape, I32)

    def chunk(c):
        return pl.ds(pl.multiple_of(c * sc, sc), sc)

    def score_chunk(c, carry):
        off = pl.multiple_of(c * sc, sc)
        d = _dot_nt(ki_ref[chunk(c), :], qi)
        s = jnp.zeros((sc, bq), F32)
        for h in range(N_HEADS_IDX):
            s = s + jnp.maximum(d[:, h * bq:(h + 1) * bq], 0.0) * w[h:h + 1, :]
        key = jnp.where(off + sub <= t_pos, _sort_key(s), INT_MIN)
        keys_ref[chunk(c), :] = key
        u = key ^ INT_MIN
        for grp in range(sc // KEYS_PER_GROUP):
            base = grp * KEYS_PER_GROUP
            words = _bit_planes([u[base + SUBLANES * v:base + SUBLANES * (v + 1), :] for v in range(32)])
            rows = pl.ds(pl.multiple_of(c * wpc + grp * SUBLANES, SUBLANES), SUBLANES)
            for p in range(32):
                planes_ref[p, rows, :] = words[p]
        return carry

    lax.fori_loop(0, n_ch // 2, lambda j, cr: score_chunk(2 * j + 1, score_chunk(2 * j, cr)), 0)
    lax.fori_loop(n_ch - n_ch % 2, n_ch, score_chunk, 0)

    def count(pred):
        def body(c, acc):
            hit = jnp.where(pred(keys_ref[chunk(c), :], c * sc + sub), 1, 0)
            return acc + hit.reshape(sc // SUBLANES, SUBLANES, bq).sum(axis=0)

        def body2(j, acc):
            return body(2 * j + 1, body(2 * j, acc))
        acc = lax.fori_loop(0, n_ch // 2, body2, jnp.zeros((SUBLANES, bq), I32))
        acc = lax.fori_loop(n_ch - n_ch % 2, n_ch, body, acc)
        return acc.sum(axis=0, keepdims=True)

    slab = SEARCH_SLAB
    n_slab = (n_ch * wpc + slab - 1) // slab
    srow = lax.broadcasted_iota(I32, (slab, 1), 0)

    def slab_rows(sl):
        return pl.ds(pl.multiple_of(sl * slab, slab), slab)

    def init_cand(sl, carry):
        cand_ref[slab_rows(sl), :] = jnp.where(sl * slab + srow < n_ch * wpc, -1, 0) + jnp.zeros((slab, bq), I32)
        return carry

    lax.fori_loop(0, n_slab, init_cand, 0)

    def sweep(prev, cur):
        def body(sl, acc):
            r = slab_rows(sl)
            cand = cand_ref[r, :]
            if prev is not None:
                ones = cand & planes_ref[prev[0], r, :]
                cand = jnp.where(prev[1] != 0, ones, cand ^ ones)
                cand_ref[r, :] = cand
            hits = cand if cur is None else cand & planes_ref[cur, r, :]
            cnt = lax.population_count(hits)
            return acc + cnt.reshape(slab // SUBLANES, SUBLANES, bq).sum(axis=0)
        acc = lax.fori_loop(0, n_slab, body, jnp.zeros((SUBLANES, bq), I32))
        return acc.sum(axis=0, keepdims=True)

    def decide(ones, p, need, thr_u):
        keep = ones >= need
        bit = jnp.int32(1) << (31 - p)
        return jnp.where(keep, 1, 0), jnp.where(keep, need, need - ones), jnp.where(keep, thr_u | bit, thr_u)

    keep, need, thr_u = decide(sweep(None, 0), 0, jnp.full((1, bq), k_sel, I32), jnp.zeros((1, bq), I32))

    def search_pass(p, st):
        keep, need, thr_u = st
        return decide(sweep((p - 1, keep), p), p, need, thr_u)

    keep, need, thr_u = lax.fori_loop(1, 32, search_pass, (keep, need, thr_u))
    tied = sweep((31, keep), None)
    thr = thr_u ^ INT_MIN
    cnt = (k_sel - need) + tied
    cut_ref[...] = jnp.full((1, bq), INT_MAX, I32)

    excess = jnp.logical_and(cnt > k_sel, thr > INT_MIN)
    has_ties = jnp.max(jnp.where(excess, 1, 0)) > 0

    @pl.when(has_ties)
    def _():
        want = k_sel - count(lambda blk, _: blk > thr)

        def tie_body(j, cpos):
            cand = cpos + (jnp.int32(1) << (idx_bits - 1 - j))
            below = count(lambda blk, pos: jnp.where(blk == thr, pos, INT_MAX) < cand)
            return jnp.where(below < want, cand, cpos)

        cpos = lax.fori_loop(0, idx_bits, tie_body, jnp.zeros((1, bq), I32))
        cut_ref[...] = jnp.where(excess, cpos, INT_MAX)

    m_ref[...] = jnp.full(m_ref.shape, NEG, F32)
    acc_ref[...] = jnp.zeros(acc_ref.shape, F32)
    cut = cut_ref[...]
    floor = jnp.where(thr == INT_MIN, INT_MIN, thr - 1)
    n_sub = sc // ATTN_SUB

    qs = [qa_ref[g * GQA:(g + 1) * GQA].reshape(GQA * bq, HEAD_DIM_A) for g in range(N_KV_A)]
    neg_row = jnp.full((1, GQA * bq), NEG, F32)

    def stage(c, par, cb, par_b):
        do_a, do_b = c is not None, cb is not None
        if do_a:
            off = pl.multiple_of(c * sc, sc)
            keyc = keys_ref[chunk(c), :]

            @pl.when(has_ties)
            def _():
                tie = jnp.where(off + sub <= cut, 0.0, NEG)
                bias = jnp.where(keyc > thr, 0.0, jnp.where(keyc == thr, tie, NEG))
                bias_ref[...] = jnp.where(keyc == INT_MIN, NEG, bias)

            @pl.when(jnp.logical_not(has_ties))
            def _():
                bias_ref[...] = jnp.where(keyc > floor, 0.0, NEG)
            cmax = [neg_row] * N_KV_A
        if do_b:
            off_b = pl.multiple_of(cb * sc, sc)
            m_new, acc = [], []
            for g in range(N_KV_A):
                m_old = m_ref[g]
                m_new.append(jnp.maximum(m_old, cm_ref[par_b, g]))
                acc.append(jnp.exp2(m_old - m_new[g]) * acc_ref[g])
                m_ref[g] = m_new[g]
        for r in range(n_sub):
            rows = slice(r * ATTN_SUB, (r + 1) * ATTN_SUB)
            if do_a:
                bias4 = jnp.concatenate([bias_ref[rows, :]] * GQA, axis=1)
                for g in range(N_KV_A):
                    lg = _dot_nt(k_ref[g, pl.ds(off + r * ATTN_SUB, ATTN_SUB), :], qs[g]) + bias4
                    lg_ref[par, g, rows, :] = lg
                    cmax[g] = jnp.maximum(cmax[g], lg.max(axis=0, keepdims=True))
            if do_b:
                for g in range(N_KV_A):
                    p = jnp.exp2(lg_ref[par_b, g, rows, :] - m_new[g])
                    vt = vt_ref[g, :, pl.ds(off_b + r * ATTN_SUB, ATTN_SUB)]
                    acc[g] = acc[g] + _dot(vt, p.astype(BF16))
        if do_b:
            for g in range(N_KV_A):
                acc_ref[g] = acc[g]
        if do_a:
            for g in range(N_KV_A):
                cm_ref[par, g] = cmax[g]

    stage(jnp.int32(0), 0, None, None)

    def stage_pair(j, carry):
        stage(2 * j + 1, 1, 2 * j, 0)
        stage(2 * j + 2, 0, 2 * j + 1, 1)
        return carry

    last = n_ch - 1
    lax.fori_loop(0, last // 2, stage_pair, 0)

    @pl.when(last % 2 == 1)
    def _():
        stage(last, 1, last - 1, 0)
        stage(None, None, last, 1)

    @pl.when(last % 2 == 0)
    def _():
        stage(None, None, last, 0)

    heads = []
    for g in range(N_KV_A):
        acc = acc_ref[g]
        o = acc[:HEAD_DIM_A] / acc[HEAD_DIM_A:HEAD_DIM_A + 1]
        heads += [o[:, hh * bq:(hh + 1) * bq] for hh in range(GQA)]
    out_ref[...] = jnp.concatenate(heads, axis=0).T.astype(out_ref.dtype)


def _prompt_attention(qi_hm, qa_hm, wit, kibf, khm, vt):
    t = kibf.shape[0]
    bq = ATTN_BQ
    k_sel = min(TOPK_MAX, t // 4)
    kern = functools.partial(_attn_kernel, k_sel=k_sel, idx_bits=max(1, (t - 1).bit_length()))
    in_specs = [pl.BlockSpec((N_HEADS_IDX, bq, D_IDX), lambda i: (0, i, 0)),
                pl.BlockSpec((N_HEADS_A, bq, HEAD_DIM_A), lambda i: (0, i, 0)),
                pl.BlockSpec((N_HEADS_IDX, bq), lambda i: (0, i)),
                _const_spec(kibf.shape, True), _const_spec(khm.shape, True), _const_spec(vt.shape, True)]
    n_words = -(-(t // 32) // SEARCH_SLAB) * SEARCH_SLAB
    scratch = [pltpu.VMEM((t, bq), I32), pltpu.VMEM((32, n_words, bq), I32), pltpu.VMEM((n_words, bq), I32),
               pltpu.VMEM((1, bq), I32), pltpu.VMEM((ATTN_SC, bq), F32),
               pltpu.VMEM((2, N_KV_A, ATTN_SC, GQA * bq), F32), pltpu.VMEM((2, N_KV_A, 1, GQA * bq), F32),
               pltpu.VMEM((N_KV_A, 1, GQA * bq), F32),
               pltpu.VMEM((N_KV_A, VT_ROWS, GQA * bq), F32)]
    return pl.pallas_call(kern, out_shape=jax.ShapeDtypeStruct((t, WA_Q), BF16), grid=(t // bq,),
                          in_specs=in_specs, out_specs=pl.BlockSpec((bq, WA_Q), lambda i: (i, 0)),
                          scratch_shapes=scratch, compiler_params=_cparams(), name="attn")(
                              qi_hm, qa_hm, wit, kibf, khm, vt)


SAMPLE_CK = 2048


def _sample_attn_kernel(pt_ref, qi_ref, qa_ref, wi_ref, kin_ref, kn_ref, vn_ref, cidx_hbm, ck_hbm, cv_hbm, out_ref,
                        idx_buf, k_buf, v_buf, scr_ref, cut_ref, sems, *, n_pages, ps, ck, k_sel, idx_bits):
    b = pl.program_id(0)
    nb = pl.num_programs(0)
    past = n_pages * ps
    nc = past // ck

    def page_copy(which, bb, p):
        src, dst = ((cidx_hbm, idx_buf), (ck_hbm, k_buf), (cv_hbm, v_buf))[which]
        cols = pl.ds(pl.multiple_of(p * ps, ps), ps)
        dst = dst.at[:, cols] if which == 0 else dst.at[:, :, cols]
        return pltpu.make_async_copy(src.at[pt_ref[bb, p]], dst, sems.at[which])

    def for_pages(fn):
        lax.fori_loop(0, n_pages, lambda p, c: (fn(p), c)[1], 0, unroll=8)

    def wait_all(which):
        buf = (idx_buf, k_buf, v_buf)[which]
        pltpu.make_async_copy(buf, buf, sems.at[which]).wait()

    @pl.when(b == 0)
    def _():
        for_pages(lambda p: page_copy(0, b, p).start())

    for_pages(lambda p: (page_copy(1, b, p).start(), page_copy(2, b, p).start()))
    wait_all(0)

    qi = qi_ref[0]
    w = wi_ref[0]
    for c in range(nc):
        d = _dot(qi, idx_buf[:, c * ck:(c + 1) * ck].astype(BF16))
        scr_ref[c:c + 1, :] = (jnp.maximum(d, 0.0) * w).sum(axis=0, keepdims=True)
    d_new = (qi.astype(F32) * kin_ref[0].astype(BF16).astype(F32)).sum(axis=-1, keepdims=True)
    s_new = (jnp.maximum(d_new, 0.0) * w).sum(axis=0, keepdims=True)

    @pl.when(b + 1 < nb)
    def _():
        for_pages(lambda p: page_copy(0, b + 1, p).start())

    keys = _sort_key(scr_ref[...])
    key_new = _sort_key(s_new)
    pos = (lax.broadcasted_iota(I32, keys.shape, 0) * ck + lax.broadcasted_iota(I32, keys.shape, 1))

    def count(pred):
        hit = jnp.where(pred(keys, pos), 1, 0)
        parts = [hit[:, j * LANES:(j + 1) * LANES] for j in range(ck // LANES)]
        while len(parts) > 1:
            parts = [a + b for a, b in zip(parts[::2], parts[1::2])] + parts[len(parts) - len(parts) % 2:]
        total = parts[0].sum(axis=1, keepdims=True).sum(axis=0, keepdims=True)
        return total + jnp.where(pred(key_new, past), 1, 0)

    dcol = lax.broadcasted_iota(I32, (16, 1), 0)

    def search_body(j, st):
        thr, cnt = st
        step = jnp.int32(1) << (28 - 4 * j)
        rows = [jnp.zeros((1, LANES), I32)]
        for d in range(1, 16):
            hit = jnp.where(keys >= thr + d * step, 1, 0)
            parts = [hit[:, t * LANES:(t + 1) * LANES] for t in range(ck // LANES)]
            while len(parts) > 1:
                parts = [a + b for a, b in zip(parts[::2], parts[1::2])] + parts[len(parts) - len(parts) % 2:]
            rows.append(parts[0].sum(axis=0, keepdims=True))
        cands = thr + dcol * step
        cnts = jnp.concatenate(rows, axis=0).sum(axis=1, keepdims=True) + jnp.where(key_new >= cands, 1, 0)
        ok = jnp.logical_and(cnts >= k_sel, dcol >= 1)
        digit = jnp.where(ok, 1, 0).sum(axis=0, keepdims=True)
        best = jnp.where(ok, cnts, INT_MAX).min(axis=0, keepdims=True)
        return thr + digit * step, jnp.where(digit > 0, best, cnt)

    st0 = (jnp.full((1, 1), INT_MIN, I32), jnp.full((1, 1), past + 1, I32))
    thr, cnt = lax.fori_loop(0, 8, search_body, st0)
    cut_ref[...] = jnp.full((1, 1), INT_MAX, I32)

    @pl.when(jnp.max(jnp.where(cnt > k_sel, 1, 0)) > 0)
    def _():
        want = k_sel - count(lambda kk, _: kk > thr)

        def tie_body(j, cpos):
            cand = cpos + (jnp.int32(1) << (idx_bits - 1 - j))
            below = count(lambda kk, pp: jnp.where(kk == thr, pp, INT_MAX) < cand)
            return jnp.where(below < want, cand, cpos)

        cut_ref[...] = lax.fori_loop(0, idx_bits, tie_body, jnp.zeros((1, 1), I32))

    cut = cut_ref[...]

    def sel_bias(kk, pp):
        tie = jnp.where(pp <= cut, 0.0, NEG)
        return jnp.where(kk > thr, 0.0, jnp.where(kk == thr, tie, NEG))

    bias = sel_bias(keys, pos)
    bias_new = sel_bias(key_new, past)

    wait_all(1)
    wait_all(2)

    qa = qa_ref[0]
    heads = []
    for g in range(N_KV_A):
        qg = qa[g * GQA:(g + 1) * GQA]
        cols = slice(g * HEAD_DIM_A, (g + 1) * HEAD_DIM_A)
        spans = [slice(c * ck, (c + 1) * ck) for c in range(nc)]
        lgs = [_dot(qg, k_buf[g, :, sp].astype(BF16)) + bias[c:c + 1, :] for c, sp in enumerate(spans)]
        kn = kn_ref[0][:, cols].astype(BF16).astype(F32)
        vn = vn_ref[0][:, cols].astype(BF16).astype(F32)
        lg_new = (qg.astype(F32) * kn).sum(axis=-1, keepdims=True) + bias_new
        m = lg_new
        for lg in lgs:
            m = jnp.maximum(m, lg.max(axis=-1, keepdims=True))
        p_new = jnp.exp2(lg_new - m)
        l = p_new
        acc = p_new * vn
        for lg, sp in zip(lgs, spans):
            p = jnp.exp2(lg - m)
            l = l + p.sum(axis=-1, keepdims=True)
            acc = acc + _dot_nt(p.astype(BF16), v_buf[g, :, sp].astype(BF16))
        o = acc / l
        heads += [o[hh:hh + 1, :] for hh in range(GQA)]
    out_ref[0] = jnp.concatenate(heads, axis=1)


def _sample_attention(page_table, qi_s, qa_s, wi_s, ki_new, k_new, v_new, cache_idx_k, cache_k, cache_v):
    nbatch, n_pages = page_table.shape
    n_pool, ps, d_idx = cache_idx_k.shape
    past = n_pages * ps
    ck = min(SAMPLE_CK, past)
    assert past % ck == 0 and ck % ps == 0
    k_sel = min(TOPK_MAX, (past + 1) // 4)
    kern = functools.partial(_sample_attn_kernel, n_pages=n_pages, ps=ps, ck=ck, k_sel=k_sel,
                             idx_bits=past.bit_length())
    per_b = lambda shape: pl.BlockSpec((1,) + shape, lambda b, pt: (b, 0, 0))
    any_spec = pl.BlockSpec(memory_space=pl.ANY)
    grid_spec = pltpu.PrefetchScalarGridSpec(
        num_scalar_prefetch=1, grid=(nbatch,),
        in_specs=[per_b((N_HEADS_IDX, d_idx)), per_b((N_HEADS_A, HEAD_DIM_A)), per_b((N_HEADS_IDX, 1)),
                  per_b((1, d_idx)), per_b((1, WA_KV)), per_b((1, WA_KV)), any_spec, any_spec, any_spec],
        out_specs=per_b((1, WA_Q)),
        scratch_shapes=[pltpu.VMEM((d_idx, past), F32), pltpu.VMEM((N_KV_A, HEAD_DIM_A, past), F32),
                        pltpu.VMEM((N_KV_A, HEAD_DIM_A, past), F32),
                        pltpu.VMEM((past // ck, ck), F32), pltpu.VMEM((1, 1), I32),
                        pltpu.SemaphoreType.DMA((3,))])
    return pl.pallas_call(kern, out_shape=jax.ShapeDtypeStruct((nbatch, 1, WA_Q), F32), grid_spec=grid_spec,
                          compiler_params=_cparams(), name="sample_attn")(
                              page_table, qi_s, qa_s, wi_s, ki_new, k_new, v_new,
                              jnp.transpose(cache_idx_k, (0, 2, 1)), jnp.transpose(cache_k, (0, 2, 3, 1)),
                              jnp.transpose(cache_v, (0, 2, 3, 1)))


def _rope_tables(pos):
    rot = HEAD_DIM_A // ROT_FRACTION
    half = rot // 2
    inv_freq = ROPE_THETA ** (-jnp.arange(half, dtype=F32) / half)
    ang = pos.astype(F32)[:, None] * inv_freq[None, :]
    cos, sin = jnp.cos(ang), jnp.sin(ang)
    n = pos.shape[0]
    z = lambda w: jnp.zeros((n, w), F32)
    c = jnp.concatenate([cos, cos, jnp.ones((n, HEAD_DIM_A - rot), F32)], axis=1)
    sa = jnp.concatenate([-sin, z(HEAD_DIM_A - half)], axis=1)
    sb = jnp.concatenate([z(half), sin, z(HEAD_DIM_A - rot)], axis=1)
    rep = LANES // HEAD_DIM_A
    return tuple(jnp.tile(a, (1, rep)) for a in (c, sa, sb))


def _prep_params(w_in, g_in, g_qa, g_ka):
    d = w_in.shape[0]
    w_qa, w_ka, w_va, w_qi, w_ki, w_wi = _split_w_in(w_in, d)[:6]
    wa = jnp.concatenate([w_qa, w_ka, w_va, w_qi, w_ki, jnp.zeros((d, LANES - D_IDX), w_in.dtype)], axis=1)
    lane = jnp.arange(LANES)
    bd = (lane[:, None] // HEAD_DIM_A == lane[None, :] // HEAD_DIM_A).astype(BF16)
    rep = LANES // HEAD_DIM_A
    return (g_in.reshape(1, d), wa.astype(BF16), w_wi.T.astype(BF16), w_va.T.astype(BF16), bd,
            jnp.tile(g_qa, rep).reshape(1, LANES), jnp.tile(g_ka, rep).reshape(1, LANES))


def _split_w_in(w_in, d):
    sizes = (WA_Q, WA_KV, WA_KV, WI_Q, D_IDX, N_HEADS_IDX, WIDTH_B, WIDTH_B, WM_Q, N_BRANCH * d)
    parts, o = [], 0
    for s in sizes:
        parts.append(w_in[:, o:o + s])
        o += s
    return parts


def _mix_kernel(*refs, chunked, shared_mem, emit_vb, n_experts):
    it = iter(refs)
    x_ref, oa_ref, gin_ref, wb_ref, gvb_ref, gqm_ref = (next(it) for _ in range(6))
    if chunked:
        wtril_ref, bt_ref = next(it), next(it)
    else:
        wdiag_ref, bdiag_ref = next(it), next(it)
    mk_ref, mv_ref = next(it), next(it)
    wbra_ref, wbrb_ref, wbrm_ref, wout_ref, gffn_ref, wrh_ref, wrl_ref, br_ref = (next(it) for _ in range(8))
    x1_ref, h2_ref, te_ref, tw_ref = (next(it) for _ in range(4))
    vb_ref = next(it) if emit_vb else None

    x = x_ref[...]
    tm, d = x.shape
    h = _rms(x, gin_ref[...]).astype(BF16)
    p = _dot(h, wb_ref[...])
    ub = p[:, :WIDTH_B]
    vb = _rms(p[:, WIDTH_B:2 * WIDTH_B], gvb_ref[...])
    if emit_vb:
        vb_ref[...] = vb
    o = 2 * WIDTH_B
    qm = p[:, o:o + WM_Q]
    o += WM_Q
    gates = p[:, o:o + N_BRANCH * d]

    if chunked:
        vbb = vb.astype(BF16)
        rows = []
        for cc in range(tm // CHUNK):
            cols = []
            for g in range(N_GROUPS_B):
                vg = vbb[cc * CHUNK:(cc + 1) * CHUNK, g * GROUP_DIM_B:(g + 1) * GROUP_DIM_B]
                cols.append(_dot(wtril_ref[g], vg))
            rows.append(jnp.concatenate(cols, axis=1) + bt_ref[...])
        z = jnp.concatenate(rows, axis=0) if len(rows) > 1 else rows[0]
    else:
        z = vb * wdiag_ref[...] + bdiag_ref[...]
    out_b = ub * z

    scale_m = HEAD_DIM_M ** -0.5
    outs = []
    for hh in range(N_HEADS_M):
        sl = slice(hh * HEAD_DIM_M, (hh + 1) * HEAD_DIM_M)
        qh = _rms(qm[:, sl], gqm_ref[...])
        if shared_mem:
            lg = _dot_nt(qh.astype(BF16), mk_ref[:, sl].astype(BF16)) * scale_m
            pm = jnp.exp(lg - lg.max(axis=-1, keepdims=True))
            pm = pm / pm.sum(axis=-1, keepdims=True)
            outs.append(_dot(pm.astype(BF16), mv_ref[:, sl].astype(BF16)))
        else:
            per_row = []
            for r in range(tm):
                kr = mk_ref[r, :, sl]
                lg = (kr * qh[r:r + 1, :]).sum(axis=-1, keepdims=True) * scale_m
                pm = jnp.exp(lg - lg.max(axis=0, keepdims=True))
                pm = pm / pm.sum(axis=0, keepdims=True)
                per_row.append((pm * mv_ref[r, :, sl]).sum(axis=0, keepdims=True))
            outs.append(jnp.concatenate(per_row, axis=0))
    out_m = jnp.concatenate(outs, axis=1)

    sig = jax.nn.sigmoid
    merged = (sig(gates[:, :d]) * _dot(oa_ref[...].astype(BF16), wbra_ref[...])
              + sig(gates[:, d:2 * d]) * _dot(out_b.astype(BF16), wbrb_ref[...])
              + sig(gates[:, 2 * d:]) * _dot(out_m.astype(BF16), wbrm_ref[...]))
    x1 = x + _dot(merged.astype(BF16), wout_ref[...])
    x1_ref[...] = x1
    h2 = _rms(x1, gffn_ref[...])
    h2_ref[...] = h2.astype(h2_ref.dtype)

    hi = h2.astype(BF16)
    lo = (h2 - hi.astype(F32)).astype(BF16)
    lg = _dot(hi, wrh_ref[...]) + _dot(hi, wrl_ref[...]) + _dot(lo, wrh_ref[...]) + br_ref[...]
    lane = lax.broadcasted_iota(I32, lg.shape, 1)
    lg = jnp.where(lane < n_experts, lg, -jnp.inf)
    vals, idxs = [], []
    for _ in range(TOP_K_MOE):
        m = lg.max(axis=-1, keepdims=True)
        idx = jnp.where(lg == m, lane, LANES).min(axis=-1, keepdims=True)
        vals.append(m)
        idxs.append(idx)
        lg = jnp.where(lane == idx, -jnp.inf, lg)
    ex = [jnp.exp(v - vals[0]) for v in vals]
    den = ex[0]
    for e in ex[1:]:
        den = den + e
    te = jnp.full(lane.shape, -1, I32)
    tw = jnp.zeros(lane.shape, F32)
    for j in range(TOP_K_MOE):
        te = jnp.where(lane == j, idxs[j], te)
        tw = jnp.where(lane == j, ex[j] / den, tw)
    te_ref[...] = te
    tw_ref[...] = tw


def _mix(x2d, out_a, mix_params, gmlp_params, mem_k, mem_v, *, tm, chunked, shared_mem, emit_vb, n_experts):
    n, d = x2d.shape
    gin, wb, gvb, gqm, wbra, wbrb, wbrm, wout, gffn, wrh, wrl, br = mix_params
    row = lambda w: pl.BlockSpec((tm, w), lambda i: (i, 0))
    if shared_mem:
        mem_specs = [_const_spec(mem_k.shape), _const_spec(mem_v.shape)]
    else:
        mspec = pl.BlockSpec((tm,) + mem_k.shape[1:], lambda i: (i, 0, 0))
        mem_specs = [mspec, mspec]
    consts = lambda arrs: [_const_spec(a.shape) for a in arrs]
    in_specs = ([row(d), row(WA_Q)] + consts([gin, wb, gvb, gqm]) + consts(gmlp_params) + mem_specs
                + consts([wbra, wbrb, wbrm, wout, gffn, wrh, wrl, br]))
    out_shape = [jax.ShapeDtypeStruct((n, d), F32), jax.ShapeDtypeStruct((n, d), F32),
                 jax.ShapeDtypeStruct((n, LANES), I32), jax.ShapeDtypeStruct((n, LANES), F32)]
    out_specs = [row(d), row(d), row(LANES), row(LANES)]
    if emit_vb:
        out_shape.append(jax.ShapeDtypeStruct((n, WIDTH_B), F32))
        out_specs.append(row(WIDTH_B))
    kern = functools.partial(_mix_kernel, chunked=chunked, shared_mem=shared_mem, emit_vb=emit_vb,
                             n_experts=n_experts)
    return pl.pallas_call(kern, out_shape=tuple(out_shape), grid=(n // tm,), in_specs=in_specs,
                          out_specs=tuple(out_specs), compiler_params=_cparams(), name="mix")(
                              x2d, out_a, gin, wb, gvb, gqm, *gmlp_params, mem_k, mem_v,
                              wbra, wbrb, wbrm, wout, gffn, wrh, wrl, br)


def _prep_mix_params(w_in, g_in, g_vb, g_qm, w_br_a, w_br_b, w_br_m, w_out, g_ffn, w_router, b_router):
    d = w_in.shape[0]
    parts = _split_w_in(w_in, d)
    wb = jnp.concatenate(parts[6:10], axis=1).astype(BF16)
    n_e = w_router.shape[1]
    wr = jnp.pad(w_router, ((0, 0), (0, LANES - n_e)))
    wrh = wr.astype(BF16)
    wrl = (wr - wrh.astype(F32)).astype(BF16)
    br = jnp.pad(b_router, (0, LANES - n_e)).reshape(1, LANES)
    return (g_in.reshape(1, d), wb, g_vb.reshape(1, WIDTH_B), g_qm.reshape(1, HEAD_DIM_M),
            w_br_a.astype(BF16), w_br_b.astype(BF16), w_br_m.astype(BF16), w_out.astype(BF16),
            g_ffn.reshape(1, d), wrh, wrl, br)


def _prep_gmlp_chunked(w_s, b_s):
    tril = jnp.tril(jnp.ones((CHUNK, CHUNK), w_s.dtype))
    wtril = (w_s * tril).astype(BF16)
    bt = jnp.repeat(b_s.T, GROUP_DIM_B, axis=1)
    return wtril, bt


def _prep_gmlp_single(w_s, b_s):
    wdiag = jnp.repeat(w_s[:, 0, 0], GROUP_DIM_B).reshape(1, WIDTH_B)
    bdiag = jnp.repeat(b_s[:, 0], GROUP_DIM_B).reshape(1, WIDTH_B)
    return wdiag, bdiag


MOE_TP = 256


def _plan_kernel(te_ref, tri_ref, upper_ref, lslot_ref, cnt_ref, tstart_ref, gprev_ref, total_ref, carry_ref):
    @pl.when(pl.program_id(0) == 0)
    def _():
        carry_ref[...] = jnp.zeros(carry_ref.shape, F32)

    te = te_ref[...]
    lane = lax.broadcasted_iota(I32, te.shape, 1)
    picks = [te[:, j:j + 1] for j in range(TOP_K_MOE)]
    onehot = jnp.zeros(te.shape, F32)
    for e in picks:
        onehot = onehot + jnp.where(lane == e, 1.0, 0.0)
    before = _dot(tri_ref[...], onehot.astype(BF16))
    cnt = onehot.sum(axis=0, keepdims=True)
    cnt = jnp.floor((cnt + (SUBLANES - 1)) * (1.0 / SUBLANES)) * SUBLANES
    cnt8 = jnp.broadcast_to(cnt, (SUBLANES, LANES)).astype(BF16)
    tstart = _dot(cnt8, upper_ref[...])[0:1, :]
    lslot = jnp.zeros(te.shape, I32)
    for j, e in enumerate(picks):
        r = jnp.where(lane == e, before + tstart, 0.0).sum(axis=-1, keepdims=True).astype(I32)
        lslot = jnp.where(lane == j, r, lslot)
    lslot_ref[...] = lslot
    cnt_ref[0] = cnt.astype(I32)
    tstart_ref[0] = tstart.astype(I32)
    gprev_ref[0] = carry_ref[...].astype(I32)
    carry_ref[...] = carry_ref[...] + cnt
    total_ref[...] = carry_ref[...].astype(I32)


def _moe_plan(te_all):
    n = te_all.shape[0]
    tp = MOE_TP
    n_tiles = n // tp
    r = jnp.arange(tp)
    tri = (r[:, None] > r[None, :]).astype(BF16)
    e = jnp.arange(LANES)
    upper = (e[:, None] < e[None, :]).astype(BF16)
    per_tile = jax.ShapeDtypeStruct((n_tiles, 1, LANES), I32)
    tile_spec = pl.BlockSpec((1, 1, LANES), lambda i: (i, 0, 0))
    return pl.pallas_call(
        _plan_kernel,
        out_shape=(jax.ShapeDtypeStruct((n, LANES), I32), per_tile, per_tile, per_tile,
                   jax.ShapeDtypeStruct((1, LANES), I32)),
        grid=(n_tiles,),
        in_specs=[pl.BlockSpec((tp, LANES), lambda i: (i, 0)), _const_spec((tp, tp)), _const_spec((LANES, LANES))],
        out_specs=(pl.BlockSpec((tp, LANES), lambda i: (i, 0)), tile_spec, tile_spec, tile_spec,
                   _const_spec((1, LANES))),
        scratch_shapes=[pltpu.VMEM((1, LANES), F32)],
        compiler_params=_cparams(), name="moe_plan")(te_all, tri, upper)


def _run_pieces(tp):
    sizes, s = [], SUBLANES
    while s <= tp:
        sizes.append(s)
        s *= 2
    return tuple(reversed(sizes))


def _local_rows(tp, n_e):
    return tp * TOP_K_MOE + n_e * SUBLANES


def _for_runs(tabs, tile, n_e, tp, fn):
    cnt_s, tstart_s, gbase_s = tabs

    def per_expert(e, c):
        idx = tile * n_e + e
        length, src0, dst0 = cnt_s[idx], tstart_s[idx], gbase_s[idx]
        off = jnp.int32(0)
        for size in _run_pieces(tp):
            piece = length & size

            @pl.when(piece != 0)
            def _():
                fn(pl.multiple_of(src0 + off, SUBLANES), pl.multiple_of(dst0 + off, SUBLANES), size)
            off = off + piece
        return c

    lax.fori_loop(0, n_e, per_expert, 0, unroll=4)


def _wait_runs(tabs, tile, n_e, n_rows, copy_of):
    cnt_s, tstart_s, _ = tabs
    last = tile * n_e + n_e - 1
    total = tstart_s[last] + cnt_s[last]
    for size in _run_pieces(n_rows):
        @pl.when((total & size) != 0)
        def _():
            copy_of(size).wait()


def _slot_matrix(lslot, vals, n_rows):
    col = lax.broadcasted_iota(I32, (lslot.shape[0], n_rows), 1)
    m = jnp.zeros(col.shape, F32)
    for j in range(TOP_K_MOE):
        v = 1.0 if vals is None else vals[:, j:j + 1]
        m = m + jnp.where(lslot[:, j:j + 1] == col, v, 0.0)
    return m


def _dispatch_kernel(cnt_s, tstart_s, gbase_s, fstart_s, flen_s, nused_s, lslot_ref, h2_ref, *rest,
                     tp, n_e, tile0, first, bm):
    if first:
        xs_out, stage, sem, zbuf, zsem = rest
    else:
        _, xs_out, stage, sem = rest
    n_rows = _local_rows(tp, n_e)
    i = pl.program_id(0)
    slot = i % 2
    pt = _slot_matrix(lslot_ref[...], None, n_rows).astype(BF16)
    stage[slot] = lax.dot_general(pt, h2_ref[...].astype(BF16), (((0,), (0,)), ((), ())),
                                  preferred_element_type=F32)

    def runs(tile, s):
        _for_runs((cnt_s, tstart_s, gbase_s), tile0 + tile, n_e, tp,
                  lambda src, dst, size: pltpu.make_async_copy(
                      stage.at[s, pl.ds(src, size), :], xs_out.at[pl.ds(dst, size), :],
                      sem.at[s]).start(priority=size.bit_length() % 2))

    def drain(tile, s):
        _wait_runs((cnt_s, tstart_s, gbase_s), tile0 + tile, n_e, n_rows, lambda size: pltpu.make_async_copy(
            stage.at[s, pl.ds(0, size), :], xs_out.at[pl.ds(0, size), :], sem.at[s]))

    @pl.when(i > 0)
    def _():
        drain(i - 1, 1 - slot)

    runs(i, slot)

    @pl.when(i == pl.num_programs(0) - 1)
    def _():
        drain(i, slot)
        if first:
            zbuf[...] = jnp.zeros(zbuf.shape, F32)
            nb = xs_out.shape[0] // bm

            def fill(op):
                def per_expert(e, c):
                    length, dst0 = flen_s[e], fstart_s[e]
                    off = jnp.int32(0)
                    for size in _run_pieces(bm):
                        piece = length & size

                        @pl.when(piece != 0)
                        def _():
                            op(pltpu.make_async_copy(
                                zbuf.at[pl.ds(0, size), :],
                                xs_out.at[pl.ds(pl.multiple_of(dst0 + off, SUBLANES), size), :], zsem))
                        off = off + piece
                    return c

                lax.fori_loop(0, n_e, per_expert, 0)
                lax.fori_loop(nused_s[0], nb, lambda b, c: (op(pltpu.make_async_copy(
                    zbuf, xs_out.at[pl.ds(pl.multiple_of(b * bm, bm), bm), :], zsem)), c)[1], 0)

            fill(lambda cp: cp.start())
            fill(lambda cp: cp.wait())


def _dispatch(tabs, fill_tabs, tile0, lslot, h2, xs, n_e, ns, bm):
    n, d = h2.shape
    tp = min(MOE_TP, n)
    n_rows = _local_rows(tp, n_e)
    first = xs is None
    kern = functools.partial(_dispatch_kernel, tp=tp, n_e=n_e, tile0=tile0, first=first, bm=bm)
    any_spec = pl.BlockSpec(memory_space=pl.ANY)
    scratch = [pltpu.VMEM((2, n_rows, d), F32), pltpu.SemaphoreType.DMA((2,))]
    if first:
        scratch += [pltpu.VMEM((bm, d), F32), pltpu.SemaphoreType.DMA]
    grid_spec = pltpu.PrefetchScalarGridSpec(
        num_scalar_prefetch=6, grid=(n // tp,),
        in_specs=[pl.BlockSpec((tp, LANES), lambda i, *_: (i, 0)), pl.BlockSpec((tp, d), lambda i, *_: (i, 0))]
        + ([] if first else [any_spec]),
        out_specs=any_spec, scratch_shapes=scratch)
    args = (*tabs, *fill_tabs, lslot, h2) + (() if first else (xs,))
    return pl.pallas_call(
        kern, out_shape=jax.ShapeDtypeStruct((ns, d), F32), grid_spec=grid_spec,
        input_output_aliases={} if first else {8: 0},
        compiler_params=pltpu.CompilerParams(dimension_semantics=("arbitrary",), has_side_effects=True,
                                             vmem_limit_bytes=VMEM_LIMIT),
        name="dispatch")(*args)


def _expert_kernel(be_ref, nu_ref, xs_ref, wg_ref, wu_ref, wd_ref, bg_ref, bu_ref, bd_ref, y_ref,
                   wg_bf, wu_bf, wd_bf):
    b = pl.program_id(0)

    @pl.when(b >= nu_ref[0])
    def _():
        y_ref[...] = jnp.zeros(y_ref.shape, F32)

    @pl.when(b < nu_ref[0])
    def _():
        prev = be_ref[jnp.maximum(b - 1, 0)]

        @pl.when(jnp.logical_or(b == 0, be_ref[b] != prev))
        def _():
            wg_bf[...] = wg_ref[0].astype(BF16)
            wu_bf[...] = wu_ref[0].astype(BF16)
            wd_bf[...] = wd_ref[0].astype(BF16)

        xb = xs_ref[...].astype(BF16)
        ff = wg_bf.shape[1]
        y = jnp.zeros(y_ref.shape, F32) + bd_ref[0]
        for n in range(ff // FF_CHUNK):
            sl = slice(n * FF_CHUNK, (n + 1) * FF_CHUNK)
            hg = jnp.minimum(_dot(xb, wg_bf[:, sl]) + bg_ref[0][:, sl], SWIGLU_LIMIT)
            hu = jnp.clip(_dot(xb, wu_bf[:, sl]) + bu_ref[0][:, sl], -SWIGLU_LIMIT, SWIGLU_LIMIT)
            act = hg * jax.nn.sigmoid(SWIGLU_ALPHA * hg) * (hu + 1.0)
            y = y + _dot(act.astype(BF16), wd_bf[sl, :])
        y_ref[...] = y


def _experts(xs, block_e, n_used, w_gate, b_gate, w_up, b_up, w_down, b_down):
    ns, d = xs.shape
    n_e, _, ff = w_gate.shape
    bm = MOE_BM
    blk = lambda b, be, nu: (jnp.minimum(b, nu[0] - 1), 0)
    wsel = lambda b, be, nu: (be[b], 0, 0)
    grid_spec = pltpu.PrefetchScalarGridSpec(
        num_scalar_prefetch=2, grid=(ns // bm,),
        in_specs=[pl.BlockSpec((bm, d), blk),
                  pl.BlockSpec((1, d, ff), wsel), pl.BlockSpec((1, d, ff), wsel), pl.BlockSpec((1, ff, d), wsel),
                  pl.BlockSpec((1, 1, ff), wsel), pl.BlockSpec((1, 1, ff), wsel), pl.BlockSpec((1, 1, d), wsel)],
        out_specs=pl.BlockSpec((bm, d), lambda b, be, nu: (b, 0)),
        scratch_shapes=[pltpu.VMEM((d, ff), BF16), pltpu.VMEM((d, ff), BF16), pltpu.VMEM((ff, d), BF16)])
    return pl.pallas_call(_expert_kernel, out_shape=jax.ShapeDtypeStruct((ns, d), F32), grid_spec=grid_spec,
                          compiler_params=_cparams(), name="experts")(
                              block_e, n_used, xs, w_gate, w_up, w_down,
                              b_gate.reshape(n_e, 1, ff), b_up.reshape(n_e, 1, ff), b_down.reshape(n_e, 1, d))


def _combine_kernel(cnt_s, tstart_s, gbase_s, lslot_ref, tw_ref, x1_ref, yb_hbm, y_ref, buf, sem, *, tp, n_e, tile0):
    n_rows = _local_rows(tp, n_e)
    i = pl.program_id(0)
    slot = i % 2

    def runs(tile, s):
        _for_runs((cnt_s, tstart_s, gbase_s), tile0 + tile, n_e, tp,
                  lambda loc, glob, size: pltpu.make_async_copy(
                      yb_hbm.at[pl.ds(glob, size), :], buf.at[s, pl.ds(loc, size), :],
                      sem.at[s]).start(priority=size.bit_length() % 2))

    @pl.when(i == 0)
    def _():
        buf[...] = jnp.zeros(buf.shape, F32)
        runs(i, slot)

    @pl.when(i + 1 < pl.num_programs(0))
    def _():
        runs(i + 1, 1 - slot)

    _wait_runs((cnt_s, tstart_s, gbase_s), tile0 + i, n_e, n_rows, lambda size: pltpu.make_async_copy(
        yb_hbm.at[pl.ds(0, size), :], buf.at[slot, pl.ds(0, size), :], sem.at[slot]))

    ptw = _slot_matrix(lslot_ref[...], tw_ref[...], n_rows)
    rows = buf[slot]
    w_hi = ptw.astype(BF16)
    w_lo = (ptw - w_hi.astype(F32)).astype(BF16)
    r_hi = rows.astype(BF16)
    r_lo = (rows - r_hi.astype(F32)).astype(BF16)
    y_ref[...] = x1_ref[...] + _dot(w_hi, r_hi) + (_dot(w_hi, r_lo) + _dot(w_lo, r_hi))


def _combine(tabs, tile0, lslot, tw, x1, yb, n_e):
    n, d = x1.shape
    tp = min(MOE_TP, n)
    n_rows = _local_rows(tp, n_e)
    kern = functools.partial(_combine_kernel, tp=tp, n_e=n_e, tile0=tile0)
    row = lambda w: pl.BlockSpec((tp, w), lambda i, *_: (i, 0))
    grid_spec = pltpu.PrefetchScalarGridSpec(
        num_scalar_prefetch=3, grid=(n // tp,),
        in_specs=[row(LANES), row(LANES), row(d), pl.BlockSpec(memory_space=pl.ANY)],
        out_specs=row(d),
        scratch_shapes=[pltpu.VMEM((2, n_rows, d), F32), pltpu.SemaphoreType.DMA((2,))])
    return pl.pallas_call(kern, out_shape=jax.ShapeDtypeStruct(x1.shape, F32), grid_spec=grid_spec,
                          compiler_params=_cparams(), name="combine")(*tabs, lslot, tw, x1, yb)


def _moe(groups, w_gate, b_gate, w_up, b_up, w_down, b_down):
    n_e = w_gate.shape[0]
    d = groups[0][0].shape[1]
    tp, bm = MOE_TP, MOE_BM
    sizes = [g[0].shape[0] for g in groups]
    assert all(n % tp == 0 or n < tp for n in sizes), sizes
    parts, tile0s, o = [], [], 0
    for g, n in zip(groups, sizes):
        n_pad = -(-n // tp) * tp
        parts.append(jnp.pad(g[2], ((0, n_pad - n), (0, 0)), constant_values=-1))
        tile0s.append(o // tp)
        o += n_pad
    lslot, cnt3, tstart3, gprev3, total = _moe_plan(jnp.concatenate(parts, axis=0))
    counts = total[0, :n_e]
    padded_cnt = (counts + bm - 1) // bm * bm
    pad_end = jnp.cumsum(padded_cnt)
    pad_start = pad_end - padded_cnt
    n_assign = sum(sizes) * TOP_K_MOE
    n_slack = (o // tp) * n_e * (SUBLANES - 1)
    nb = -(-(n_assign + n_slack) // bm) + n_e
    n_used = (pad_end[-1] // bm).astype(I32)
    blocks = jnp.arange(nb, dtype=I32)
    block_e = jnp.minimum((pad_end[None, :] <= (blocks * bm)[:, None]).sum(axis=1), n_e - 1).astype(I32)
    block_e = jnp.where(blocks < n_used, block_e, block_e[jnp.maximum(n_used - 1, 0)])
    flat = lambda a: a[:, 0, :n_e].reshape(-1).astype(I32)
    gbase3 = pad_start[None, None, :] + gprev3[:, :, :n_e]
    tabs = (flat(cnt3), flat(tstart3), flat(gbase3))

    after_first = gprev3[tile0s[1], 0, :n_e] if len(groups) > 1 else counts
    fill_start = (pad_start + after_first).astype(I32)
    fill_tabs = (fill_start, (pad_end - fill_start).astype(I32), n_used.reshape(1))
    xs, lslots = None, []
    for (x1, h2, te, tw), n, t0 in zip(groups, sizes, tile0s):
        ls = lslot[t0 * tp:t0 * tp + n]
        lslots.append(ls)
        xs = _dispatch(tabs, fill_tabs, t0, ls, h2, xs, n_e, nb * bm, bm)
    yb = _experts(xs, block_e, n_used.reshape(1), w_gate, b_gate, w_up, b_up, w_down, b_down)
    return [_combine(tabs, t0, ls, tw, x1, yb, n_e)
            for (x1, h2, te, tw), ls, t0 in zip(groups, lslots, tile0s)]


def _row_tile(n, want):
    return want if n % want == 0 else n


def kernel(x_prompt, x_sample, mem_prompt, cache_k, cache_v, cache_idx_k, cache_mem_k, cache_mem_v, page_table,
           g_in, w_in, g_qa, g_ka, g_vb, w_s, b_s, g_qm, g_mem, w_mem_kv, g_km, w_br_a, w_br_b, w_br_m, w_out,
           g_ffn, w_router, b_router, w_gate, b_gate, w_up, b_up, w_down, b_down):
    bp, t, d = x_prompt.shape
    bs, ts, _ = x_sample.shape
    assert bp == 1 and ts == 1 and t % CHUNK == 0 and cache_k.shape[1] == LANES
    n_mem = mem_prompt.shape[1]
    n_pages = page_table.shape[1]
    ps = cache_k.shape[1]
    past = n_pages * ps
    n_e = w_router.shape[1]

    fp = _prep_params(w_in, g_in, g_qa, g_ka)
    mp = _prep_mix_params(w_in, g_in, g_vb, g_qm, w_br_a, w_br_b, w_br_m, w_out, g_ffn, w_router, b_router)

    xp = x_prompt.reshape(t, d)
    qa_p, k32_p, khm_p, v32_p, vt_p, qi_p, ki32_p, kibf_p, wit_p = _front(
        xp, _rope_tables(jnp.arange(t, dtype=I32)), *fp, tm=_row_tile(t, 512), paged=True)
    mem_k, mem_v = _memory_kv(mem_prompt.reshape(n_mem, d), g_mem.reshape(1, d), w_mem_kv.astype(BF16),
                              g_km.reshape(1, HEAD_DIM_M))
    out_a_p = _prompt_attention(qi_p, qa_p, wit_p, kibf_p, khm_p, vt_p)
    x1_p, h2_p, te_p, tw_p = _mix(xp, out_a_p, mp, _prep_gmlp_chunked(w_s, b_s), mem_k, mem_v,
                                  tm=_row_tile(t, 512), chunked=True, shared_mem=True, emit_vb=False, n_experts=n_e)

    xs = x_sample.reshape(bs, d)
    qa_s, k32_s, _, v32_s, _, qi_s, ki32_s, _, wit_s = _front(
        xs, _rope_tables(jnp.full((bs,), past, I32)), *fp, tm=bs, paged=False)
    out_a_s = _sample_attention(page_table, jnp.moveaxis(qi_s, 0, 1), jnp.moveaxis(qa_s, 0, 1),
                                wit_s.T.reshape(bs, N_HEADS_IDX, 1), ki32_s.reshape(bs, 1, D_IDX),
                                k32_s.reshape(bs, 1, WA_KV), v32_s.reshape(bs, 1, WA_KV),
                                cache_idx_k, cache_k, cache_v)
    x1_s, h2_s, te_s, tw_s, vb_s = _mix(xs, out_a_s.reshape(bs, WA_Q), mp, _prep_gmlp_single(w_s, b_s),
                                        cache_mem_k.reshape(bs, n_mem, WM_Q), cache_mem_v.reshape(bs, n_mem, WM_Q),
                                        tm=_row_tile(bs, SUBLANES), chunked=False, shared_mem=False, emit_vb=True,
                                        n_experts=n_e)

    y_p, y_s = _moe([(x1_p, h2_p, te_p, tw_p), (x1_s, h2_s, te_s, tw_s)],
                    w_gate, b_gate, w_up, b_up, w_down, b_down)

    n_pg = t // ps
    paged_kv = lambda a: jnp.transpose(a.reshape(1, n_pg, N_KV_A, HEAD_DIM_A, ps), (0, 1, 4, 2, 3))
    return (y_p.reshape(1, t, d), y_s.reshape(bs, 1, d),
            paged_kv(k32_p), paged_kv(v32_p), jnp.transpose(ki32_p, (0, 2, 1)).reshape(1, n_pg, ps, D_IDX),
            mem_k.reshape(1, n_mem, N_HEADS_M, HEAD_DIM_M), mem_v.reshape(1, n_mem, N_HEADS_M, HEAD_DIM_M),
            k32_s.reshape(bs, 1, N_KV_A, HEAD_DIM_A), v32_s.reshape(bs, 1, N_KV_A, HEAD_DIM_A),
            ki32_s.reshape(bs, 1, D_IDX), vb_s.reshape(bs, 1, WIDTH_B))
```
